```python
import jax, jax.numpy as jnp
from jax import lax
import numpy as np

D_MODEL = 1024
BATCH = 8
SEQ = 8192
DEPTH = 1

CHUNK = 64
RNN_WIDTH = 1024
RNN_HEADS = 8
RNN_HEAD_DIM = RNN_WIDTH // RNN_HEADS
CONV_WIDTH = 4
LRU_C = 8.0
SGU_WIDTH = 1024
SGU_GROUPS = 8
SGU_GROUP_DIM = SGU_WIDTH // SGU_GROUPS
SGU_BLOCK = 2 * CHUNK
D_FF = 3 * D_MODEL
FFN_CONV_WIDTH = 3
N_BRANCHES = 2
N_MOD = 6
EPS = 1e-6
IN_COLS = 2 * RNN_WIDTH + 2 * SGU_WIDTH + N_BRANCHES * D_MODEL

kernel_name = "hybrid_rglru_sgu_convffn_block"


def _rmsnorm(x, g):
    x32 = x.astype(jnp.float32)
    y = x32 * lax.rsqrt(jnp.mean(x32 * x32, axis=-1, keepdims=True) + EPS)
    return (y * g.astype(jnp.float32)).astype(x.dtype)


def _layernorm(x, g, b):
    x32 = x.astype(jnp.float32)
    mu = jnp.mean(x32, axis=-1, keepdims=True)
    var = jnp.mean(jnp.square(x32 - mu), axis=-1, keepdims=True)
    y = (x32 - mu) * lax.rsqrt(var + EPS)
    return (y * g.astype(jnp.float32) + b.astype(jnp.float32)).astype(x.dtype)


def _modulated_norm(x, g, shift, scale):
    return _rmsnorm(x, g) * (1.0 + scale[:, None, :]) + shift[:, None, :]


def _causal_dwconv(x, w, b):
    k_width = w.shape[0]
    seq = x.shape[1]
    xp = jnp.pad(x, ((0, 0), (k_width - 1, 0), (0, 0)))
    y = b + xp[:, 0:seq, :] * w[0]
    for k in range(1, k_width):
        y = y + xp[:, k:k + seq, :] * w[k]
    return y


def _block_diag(x, w, b):
    bsz, seq, _ = x.shape
    xh = x.reshape(bsz, seq, RNN_HEADS, RNN_HEAD_DIM)
    y = jnp.einsum("bshi,hij->bshj", xh, w)
    return y.reshape(bsz, seq, RNN_WIDTH) + b


def _lin_combine(left, right):
    a_l, u_l = left
    a_r, u_r = right
    return a_l * a_r, a_r * u_l + u_r


def _rg_lru(x, w_a, b_a, w_x, b_x, lam):
    r = jax.nn.sigmoid(_block_diag(x, w_a, b_a).astype(jnp.float32))
    i = jax.nn.sigmoid(_block_diag(x, w_x, b_x).astype(jnp.float32))
    log_a = LRU_C * r * jax.nn.log_sigmoid(lam.astype(jnp.float32))
    a = jnp.exp(log_a)
    mult = jnp.sqrt(-jnp.expm1(2.0 * log_a))
    u = mult * (i * x.astype(jnp.float32))
    _, h = lax.associative_scan(_lin_combine, (a, u), axis=1)
    return h.astype(x.dtype)


def _spatial_gating(u, v, ln_g, ln_b, w_s, b_s):
    bsz, seq, _ = v.shape
    n_blk = seq // SGU_BLOCK
    v = _layernorm(v, ln_g, ln_b)
    vb = v.reshape(bsz, n_blk, SGU_BLOCK, SGU_GROUPS, SGU_GROUP_DIM)
    mask = jnp.tril(jnp.ones((SGU_BLOCK, SGU_BLOCK), dtype=w_s.dtype))
    mixed = jnp.einsum("gts,bnsgd->bntgd", w_s * mask, vb)
    mixed = mixed + jnp.transpose(b_s)[None, None, :, :, None]
    return u * mixed.reshape(bsz, seq, SGU_WIDTH)


def _fwd_setup_inputs(seed: int = 0) -> dict:
    key = jax.random.key(seed)
    ks = jax.random.split(key, 26)

    def nrm(k, shape, scale):
        return jax.random.normal(k, shape, jnp.float32) * scale

    a_c = jax.random.uniform(ks[12], (DEPTH, RNN_WIDTH), jnp.float32, 0.9, 0.999)
    s = a_c ** (1.0 / LRU_C)
    lru_lambda = jnp.log(s) - jnp.log1p(-s)

    return {
        "x": nrm(ks[0], (BATCH, SEQ, D_MODEL), 1.0),
        "c": nrm(ks[1], (BATCH, D_MODEL), 1.0),
        "w_ada": nrm(ks[2], (DEPTH, D_MODEL, N_MOD * D_MODEL), D_MODEL ** -0.5),
        "b_ada": nrm(ks[3], (DEPTH, N_MOD * D_MODEL), 0.02),
        "norm_mix_g": 1.0 + nrm(ks[4], (DEPTH, D_MODEL), 0.05),
        "w_in": nrm(ks[5], (DEPTH, D_MODEL, IN_COLS), D_MODEL ** -0.5),
        "rnn_conv_w": nrm(ks[6], (DEPTH, CONV_WIDTH, RNN_WIDTH), CONV_WIDTH ** -0.5),
        "rnn_conv_b": nrm(ks[7], (DEPTH, RNN_WIDTH), 0.02),
        "lru_w_a": nrm(ks[8], (DEPTH, RNN_HEADS, RNN_HEAD_DIM, RNN_HEAD_DIM), RNN_HEAD_DIM ** -0.5),
        "lru_b_a": nrm(ks[9], (DEPTH, RNN_WIDTH), 0.02),
        "lru_w_x": nrm(ks[10], (DEPTH, RNN_HEADS, RNN_HEAD_DIM, RNN_HEAD_DIM), RNN_HEAD_DIM ** -0.5),
        "lru_b_x": nrm(ks[11], (DEPTH, RNN_WIDTH), 0.02),
        "lru_lambda": lru_lambda,
        "sgu_ln_g": 1.0 + nrm(ks[13], (DEPTH, SGU_WIDTH), 0.05),
        "sgu_ln_b": nrm(ks[14], (DEPTH, SGU_WIDTH), 0.02),
        "sgu_w_s": nrm(ks[15], (DEPTH, SGU_GROUPS, SGU_BLOCK, SGU_BLOCK), SGU_BLOCK ** -0.5),
        "sgu_b_s": 1.0 + nrm(ks[16], (DEPTH, SGU_GROUPS, SGU_BLOCK), 0.1),
        "w_branch_a": nrm(ks[17], (DEPTH, RNN_WIDTH, D_MODEL), RNN_WIDTH ** -0.5),
        "w_branch_b": nrm(ks[18], (DEPTH, SGU_WIDTH, D_MODEL), SGU_WIDTH ** -0.5),
        "w_out": nrm(ks[19], (DEPTH, D_MODEL, D_MODEL), D_MODEL ** -0.5),
        "norm_ffn_g": 1.0 + nrm(ks[20], (DEPTH, D_MODEL), 0.05),
        "w_up": nrm(ks[21], (DEPTH, D_MODEL, 2 * D_FF), D_MODEL ** -0.5),
        "ffn_conv_w": nrm(ks[22], (DEPTH, FFN_CONV_WIDTH, 2 * D_FF), FFN_CONV_WIDTH ** -0.5),
        "ffn_conv_b": nrm(ks[23], (DEPTH, 2 * D_FF), 0.02),
        "w_down": nrm(ks[24], (DEPTH, D_FF, D_MODEL), D_FF ** -0.5),
        "norm_final_g": 1.0 + nrm(ks[25], (D_MODEL,), 0.05),
    }


def _fwd_reference(x, c, w_ada, b_ada, norm_mix_g, w_in, rnn_conv_w, rnn_conv_b,
              lru_w_a, lru_b_a, lru_w_x, lru_b_x, lru_lambda,
              sgu_ln_g, sgu_ln_b, sgu_w_s, sgu_b_s,
              w_branch_a, w_branch_b, w_out,
              norm_ffn_g, w_up, ffn_conv_w, ffn_conv_b, w_down, norm_final_g):
    split_idx = [RNN_WIDTH, 2 * RNN_WIDTH, 2 * RNN_WIDTH + SGU_WIDTH,
                 2 * RNN_WIDTH + 2 * SGU_WIDTH, 2 * RNN_WIDTH + 2 * SGU_WIDTH + D_MODEL]
    c_act = jax.nn.silu(c)
    for l in range(DEPTH):
        mod = c_act @ w_ada[l] + b_ada[l]
        shift1, scale1, gate1, shift2, scale2, gate2 = jnp.split(mod, N_MOD, axis=-1)

        h = _modulated_norm(x, norm_mix_g[l], shift1, scale1)
        z = h @ w_in[l]
        xr, gr, zu, zv, ga, gb = jnp.split(z, split_idx, axis=-1)

        xr = _causal_dwconv(xr, rnn_conv_w[l], rnn_conv_b[l])
        y_a = _rg_lru(xr, lru_w_a[l], lru_b_a[l], lru_w_x[l], lru_b_x[l], lru_lambda[l])
        y_a = (y_a * jax.nn.gelu(gr)) @ w_branch_a[l]

        y_b = _spatial_gating(jax.nn.gelu(zu), jax.nn.gelu(zv), sgu_ln_g[l], sgu_ln_b[l],
                              sgu_w_s[l], sgu_b_s[l])
        y_b = y_b @ w_branch_b[l]

        merged = jax.nn.sigmoid(ga) * y_a + jax.nn.sigmoid(gb) * y_b
        x = x + gate1[:, None, :] * (merged @ w_out[l])

        h = _modulated_norm(x, norm_ffn_g[l], shift2, scale2)
        hid = _causal_dwconv(h @ w_up[l], ffn_conv_w[l], ffn_conv_b[l])
        act, val = jnp.split(hid, 2, axis=-1)
        x = x + gate2[:, None, :] * ((jax.nn.gelu(act) * val) @ w_down[l])

    return _rmsnorm(x, norm_final_g)


import jax as _jax
import jax.numpy as _jnp

TWIN_FORMAT = 'train_step'
FWD_PARAMS = ['x', 'c', 'w_ada', 'b_ada', 'norm_mix_g', 'w_in', 'rnn_conv_w', 'rnn_conv_b', 'lru_w_a', 'lru_b_a', 'lru_w_x', 'lru_b_x', 'lru_lambda', 'sgu_ln_g', 'sgu_ln_b', 'sgu_w_s', 'sgu_b_s', 'w_branch_a', 'w_branch_b', 'w_out', 'norm_ffn_g', 'w_up', 'ffn_conv_w', 'ffn_conv_b', 'w_down', 'norm_final_g']
TWIN_WEIGHTS = ['w_ada', 'b_ada', 'norm_mix_g', 'w_in', 'rnn_conv_w', 'rnn_conv_b', 'lru_w_a', 'lru_b_a', 'lru_w_x', 'lru_b_x', 'lru_lambda', 'sgu_ln_g', 'sgu_ln_b', 'sgu_w_s', 'sgu_b_s', 'w_branch_a', 'w_branch_b', 'w_out', 'norm_ffn_g', 'w_up', 'ffn_conv_w', 'ffn_conv_b', 'w_down', 'norm_final_g']
TWIN_DIFF_INPUT = 'x'
TWIN_INPUTS = ['x', 'c', 'w_ada', 'b_ada', 'norm_mix_g', 'w_in', 'rnn_conv_w', 'rnn_conv_b', 'lru_w_a', 'lru_b_a', 'lru_w_x', 'lru_b_x', 'lru_lambda', 'sgu_ln_g', 'sgu_ln_b', 'sgu_w_s', 'sgu_b_s', 'w_branch_a', 'w_branch_b', 'w_out', 'norm_ffn_g', 'w_up', 'ffn_conv_w', 'ffn_conv_b', 'w_down', 'norm_final_g', 'loss_target', 'm_w_ada', 'm_b_ada', 'm_norm_mix_g', 'm_w_in', 'm_rnn_conv_w', 'm_rnn_conv_b', 'm_lru_w_a', 'm_lru_b_a', 'm_lru_w_x', 'm_lru_b_x', 'm_lru_lambda', 'm_sgu_ln_g', 'm_sgu_ln_b', 'm_sgu_w_s', 'm_sgu_b_s', 'm_w_branch_a', 'm_w_branch_b', 'm_w_out', 'm_norm_ffn_g', 'm_w_up', 'm_ffn_conv_w', 'm_ffn_conv_b', 'm_w_down', 'm_norm_final_g', 'v_w_ada', 'v_b_ada', 'v_norm_mix_g', 'v_w_in', 'v_rnn_conv_w', 'v_rnn_conv_b', 'v_lru_w_a', 'v_lru_b_a', 'v_lru_w_x', 'v_lru_b_x', 'v_lru_lambda', 'v_sgu_ln_g', 'v_sgu_ln_b', 'v_sgu_w_s', 'v_sgu_b_s', 'v_w_branch_a', 'v_w_branch_b', 'v_w_out', 'v_norm_ffn_g', 'v_w_up', 'v_ffn_conv_w', 'v_ffn_conv_b', 'v_w_down', 'v_norm_final_g']
TWIN_OUTPUTS = ['loss', 'grad_x', 'grad_w_ada', 'grad_b_ada', 'grad_norm_mix_g', 'grad_w_in', 'grad_rnn_conv_w', 'grad_rnn_conv_b', 'grad_lru_w_a', 'grad_lru_b_a', 'grad_lru_w_x', 'grad_lru_b_x', 'grad_lru_lambda', 'grad_sgu_ln_g', 'grad_sgu_ln_b', 'grad_sgu_w_s', 'grad_sgu_b_s', 'grad_w_branch_a', 'grad_w_branch_b', 'grad_w_out', 'grad_norm_ffn_g', 'grad_w_up', 'grad_ffn_conv_w', 'grad_ffn_conv_b', 'grad_w_down', 'grad_norm_final_g', 'delta_w_ada', 'delta_b_ada', 'delta_norm_mix_g', 'delta_w_in', 'delta_rnn_conv_w', 'delta_rnn_conv_b', 'delta_lru_w_a', 'delta_lru_b_a', 'delta_lru_w_x', 'delta_lru_b_x', 'delta_lru_lambda', 'delta_sgu_ln_g', 'delta_sgu_ln_b', 'delta_sgu_w_s', 'delta_sgu_b_s', 'delta_w_branch_a', 'delta_w_branch_b', 'delta_w_out', 'delta_norm_ffn_g', 'delta_w_up', 'delta_ffn_conv_w', 'delta_ffn_conv_b', 'delta_w_down', 'delta_norm_final_g', 'new_m_w_ada', 'new_m_b_ada', 'new_m_norm_mix_g', 'new_m_w_in', 'new_m_rnn_conv_w', 'new_m_rnn_conv_b', 'new_m_lru_w_a', 'new_m_lru_b_a', 'new_m_lru_w_x', 'new_m_lru_b_x', 'new_m_lru_lambda', 'new_m_sgu_ln_g', 'new_m_sgu_ln_b', 'new_m_sgu_w_s', 'new_m_sgu_b_s', 'new_m_w_branch_a', 'new_m_w_branch_b', 'new_m_w_out', 'new_m_norm_ffn_g', 'new_m_w_up', 'new_m_ffn_conv_w', 'new_m_ffn_conv_b', 'new_m_w_down', 'new_m_norm_final_g', 'new_v_w_ada', 'new_v_b_ada', 'new_v_norm_mix_g', 'new_v_w_in', 'new_v_rnn_conv_w', 'new_v_rnn_conv_b', 'new_v_lru_w_a', 'new_v_lru_b_a', 'new_v_lru_w_x', 'new_v_lru_b_x', 'new_v_lru_lambda', 'new_v_sgu_ln_g', 'new_v_sgu_ln_b', 'new_v_sgu_w_s', 'new_v_sgu_b_s', 'new_v_w_branch_a', 'new_v_w_branch_b', 'new_v_w_out', 'new_v_norm_ffn_g', 'new_v_w_up', 'new_v_ffn_conv_w', 'new_v_ffn_conv_b', 'new_v_w_down', 'new_v_norm_final_g']
TWIN_LEAF_KINDS = {'loss': 'loss', 'grad_x': 'grad_x', 'grad_w_ada': 'grad_w', 'grad_b_ada': 'grad_w', 'grad_norm_mix_g': 'grad_w', 'grad_w_in': 'grad_w', 'grad_rnn_conv_w': 'grad_w', 'grad_rnn_conv_b': 'grad_w', 'grad_lru_w_a': 'grad_w', 'grad_lru_b_a': 'grad_w', 'grad_lru_w_x': 'grad_w', 'grad_lru_b_x': 'grad_w', 'grad_lru_lambda': 'grad_w', 'grad_sgu_ln_g': 'grad_w', 'grad_sgu_ln_b': 'grad_w', 'grad_sgu_w_s': 'grad_w', 'grad_sgu_b_s': 'grad_w', 'grad_w_branch_a': 'grad_w', 'grad_w_branch_b': 'grad_w', 'grad_w_out': 'grad_w', 'grad_norm_ffn_g': 'grad_w', 'grad_w_up': 'grad_w', 'grad_ffn_conv_w': 'grad_w', 'grad_ffn_conv_b': 'grad_w', 'grad_w_down': 'grad_w', 'grad_norm_final_g': 'grad_w', 'delta_w_ada': 'delta_w', 'delta_b_ada': 'delta_w', 'delta_norm_mix_g': 'delta_w', 'delta_w_in': 'delta_w', 'delta_rnn_conv_w': 'delta_w', 'delta_rnn_conv_b': 'delta_w', 'delta_lru_w_a': 'delta_w', 'delta_lru_b_a': 'delta_w', 'delta_lru_w_x': 'delta_w', 'delta_lru_b_x': 'delta_w', 'delta_lru_lambda': 'delta_w', 'delta_sgu_ln_g': 'delta_w', 'delta_sgu_ln_b': 'delta_w', 'delta_sgu_w_s': 'delta_w', 'delta_sgu_b_s': 'delta_w', 'delta_w_branch_a': 'delta_w', 'delta_w_branch_b': 'delta_w', 'delta_w_out': 'delta_w', 'delta_norm_ffn_g': 'delta_w', 'delta_w_up': 'delta_w', 'delta_ffn_conv_w': 'delta_w', 'delta_ffn_conv_b': 'delta_w', 'delta_w_down': 'delta_w', 'delta_norm_final_g': 'delta_w', 'new_m_w_ada': 'new_m', 'new_m_b_ada': 'new_m', 'new_m_norm_mix_g': 'new_m', 'new_m_w_in': 'new_m', 'new_m_rnn_conv_w': 'new_m', 'new_m_rnn_conv_b': 'new_m', 'new_m_lru_w_a': 'new_m', 'new_m_lru_b_a': 'new_m', 'new_m_lru_w_x': 'new_m', 'new_m_lru_b_x': 'new_m', 'new_m_lru_lambda': 'new_m', 'new_m_sgu_ln_g': 'new_m', 'new_m_sgu_ln_b': 'new_m', 'new_m_sgu_w_s': 'new_m', 'new_m_sgu_b_s': 'new_m', 'new_m_w_branch_a': 'new_m', 'new_m_w_branch_b': 'new_m', 'new_m_w_out': 'new_m', 'new_m_norm_ffn_g': 'new_m', 'new_m_w_up': 'new_m', 'new_m_ffn_conv_w': 'new_m', 'new_m_ffn_conv_b': 'new_m', 'new_m_w_down': 'new_m', 'new_m_norm_final_g': 'new_m', 'new_v_w_ada': 'new_v', 'new_v_b_ada': 'new_v', 'new_v_norm_mix_g': 'new_v', 'new_v_w_in': 'new_v', 'new_v_rnn_conv_w': 'new_v', 'new_v_rnn_conv_b': 'new_v', 'new_v_lru_w_a': 'new_v', 'new_v_lru_b_a': 'new_v', 'new_v_lru_w_x': 'new_v', 'new_v_lru_b_x': 'new_v', 'new_v_lru_lambda': 'new_v', 'new_v_sgu_ln_g': 'new_v', 'new_v_sgu_ln_b': 'new_v', 'new_v_sgu_w_s': 'new_v', 'new_v_sgu_b_s': 'new_v', 'new_v_w_branch_a': 'new_v', 'new_v_w_branch_b': 'new_v', 'new_v_w_out': 'new_v', 'new_v_norm_ffn_g': 'new_v', 'new_v_w_up': 'new_v', 'new_v_ffn_conv_w': 'new_v', 'new_v_ffn_conv_b': 'new_v', 'new_v_w_down': 'new_v', 'new_v_norm_final_g': 'new_v'}


def _forward(args):
    return _fwd_reference(*[args[k] for k in FWD_PARAMS])


def _output_shape():
    def fwd():
        inp = _fwd_setup_inputs(0)
        return _fwd_reference(*[inp[k] for k in FWD_PARAMS])
    out = _jax.eval_shape(fwd)
    return out.shape, out.dtype

N_MICROBATCH = 1
ADAM_LR = 0.001
ADAM_B1 = 0.9
ADAM_B2 = 0.999
ADAM_EPS = 1e-08
ADAM_WD = 0.01
ADAM_STEP = 10
PER_EXAMPLE_BATCH_AXIS = {'x': 0, 'c': 0, 'loss_target': 0}
SHARED_INPUTS = []
_WEIGHT_DTYPES = {'w_ada': _jnp.float32, 'b_ada': _jnp.float32, 'norm_mix_g': _jnp.float32, 'w_in': _jnp.float32, 'rnn_conv_w': _jnp.float32, 'rnn_conv_b': _jnp.float32, 'lru_w_a': _jnp.float32, 'lru_b_a': _jnp.float32, 'lru_w_x': _jnp.float32, 'lru_b_x': _jnp.float32, 'lru_lambda': _jnp.float32, 'sgu_ln_g': _jnp.float32, 'sgu_ln_b': _jnp.float32, 'sgu_w_s': _jnp.float32, 'sgu_b_s': _jnp.float32, 'w_branch_a': _jnp.float32, 'w_branch_b': _jnp.float32, 'w_out': _jnp.float32, 'norm_ffn_g': _jnp.float32, 'w_up': _jnp.float32, 'ffn_conv_w': _jnp.float32, 'ffn_conv_b': _jnp.float32, 'w_down': _jnp.float32, 'norm_final_g': _jnp.float32}
MOMENT_SCALE = {'w_ada': 4.824586e-01, 'b_ada': 1.038882e+00, 'norm_mix_g': 2.561106e-01, 'w_in': 2.397017e-01, 'rnn_conv_w': 6.339265e-01, 'rnn_conv_b': 8.722702e-01, 'lru_w_a': 5.368442e-02, 'lru_b_a': 8.141718e-02, 'lru_w_x': 1.198776e-01, 'lru_b_x': 2.115952e-01, 'lru_lambda': 2.502228e-01, 'sgu_ln_g': 4.483251e-02, 'sgu_ln_b': 4.446879e-02, 'sgu_w_s': 4.437520e-02, 'sgu_b_s': 6.762570e-02, 'w_branch_a': 4.787645e-01, 'w_branch_b': 9.675316e-02, 'w_out': 4.801644e-01, 'norm_ffn_g': 1.634627e-01, 'w_up': 7.449098e-02, 'ffn_conv_w': 7.552903e-02, 'ffn_conv_b': 6.567034e-02, 'w_down': 1.305631e-01, 'norm_final_g': 6.568924e+01}


def _to_microbatches(a, axis):
    t = _jnp.moveaxis(a, axis, 0)
    t = t.reshape((N_MICROBATCH, t.shape[0] // N_MICROBATCH) + t.shape[1:])
    return _jnp.moveaxis(t, 1, axis + 1)


def setup_inputs(seed: int = 0) -> dict:
    inp = _fwd_setup_inputs(seed)
    key = _jax.random.fold_in(_jax.random.key(seed), 7919)
    shape, _ = _output_shape()
    out = dict(inp)
    out["loss_target"] = _jax.random.normal(_jax.random.fold_in(key, 0), shape, _jnp.float32)
    for i, name in enumerate(TWIN_WEIGHTS):
        w = inp[name].astype(_jnp.float32)
        if MOMENT_SCALE is None:
            s = _jnp.sqrt(_jnp.mean(_jnp.square(w)) + 1e-30)
        else:
            s = MOMENT_SCALE[name]
        km, kv = _jax.random.split(_jax.random.fold_in(key, i + 1))
        out[name] = w
        out["m_" + name] = s * _jax.random.normal(km, w.shape, _jnp.float32)
        out["v_" + name] = (s * s) * _jax.random.uniform(kv, w.shape, _jnp.float32, 0.5, 1.5)
    if N_MICROBATCH > 1:
        for name, axis in PER_EXAMPLE_BATCH_AXIS.items():
            out[name] = _to_microbatches(out[name], axis)
    return {'x': out['x'], 'c': out['c'], 'w_ada': out['w_ada'], 'b_ada': out['b_ada'], 'norm_mix_g': out['norm_mix_g'], 'w_in': out['w_in'], 'rnn_conv_w': out['rnn_conv_w'], 'rnn_conv_b': out['rnn_conv_b'], 'lru_w_a': out['lru_w_a'], 'lru_b_a': out['lru_b_a'], 'lru_w_x': out['lru_w_x'], 'lru_b_x': out['lru_b_x'], 'lru_lambda': out['lru_lambda'], 'sgu_ln_g': out['sgu_ln_g'], 'sgu_ln_b': out['sgu_ln_b'], 'sgu_w_s': out['sgu_w_s'], 'sgu_b_s': out['sgu_b_s'], 'w_branch_a': out['w_branch_a'], 'w_branch_b': out['w_branch_b'], 'w_out': out['w_out'], 'norm_ffn_g': out['norm_ffn_g'], 'w_up': out['w_up'], 'ffn_conv_w': out['ffn_conv_w'], 'ffn_conv_b': out['ffn_conv_b'], 'w_down': out['w_down'], 'norm_final_g': out['norm_final_g'], 'loss_target': out['loss_target'], 'm_w_ada': out['m_w_ada'], 'm_b_ada': out['m_b_ada'], 'm_norm_mix_g': out['m_norm_mix_g'], 'm_w_in': out['m_w_in'], 'm_rnn_conv_w': out['m_rnn_conv_w'], 'm_rnn_conv_b': out['m_rnn_conv_b'], 'm_lru_w_a': out['m_lru_w_a'], 'm_lru_b_a': out['m_lru_b_a'], 'm_lru_w_x': out['m_lru_w_x'], 'm_lru_b_x': out['m_lru_b_x'], 'm_lru_lambda': out['m_lru_lambda'], 'm_sgu_ln_g': out['m_sgu_ln_g'], 'm_sgu_ln_b': out['m_sgu_ln_b'], 'm_sgu_w_s': out['m_sgu_w_s'], 'm_sgu_b_s': out['m_sgu_b_s'], 'm_w_branch_a': out['m_w_branch_a'], 'm_w_branch_b': out['m_w_branch_b'], 'm_w_out': out['m_w_out'], 'm_norm_ffn_g': out['m_norm_ffn_g'], 'm_w_up': out['m_w_up'], 'm_ffn_conv_w': out['m_ffn_conv_w'], 'm_ffn_conv_b': out['m_ffn_conv_b'], 'm_w_down': out['m_w_down'], 'm_norm_final_g': out['m_norm_final_g'], 'v_w_ada': out['v_w_ada'], 'v_b_ada': out['v_b_ada'], 'v_norm_mix_g': out['v_norm_mix_g'], 'v_w_in': out['v_w_in'], 'v_rnn_conv_w': out['v_rnn_conv_w'], 'v_rnn_conv_b': out['v_rnn_conv_b'], 'v_lru_w_a': out['v_lru_w_a'], 'v_lru_b_a': out['v_lru_b_a'], 'v_lru_w_x': out['v_lru_w_x'], 'v_lru_b_x': out['v_lru_b_x'], 'v_lru_lambda': out['v_lru_lambda'], 'v_sgu_ln_g': out['v_sgu_ln_g'], 'v_sgu_ln_b': out['v_sgu_ln_b'], 'v_sgu_w_s': out['v_sgu_w_s'], 'v_sgu_b_s': out['v_sgu_b_s'], 'v_w_branch_a': out['v_w_branch_a'], 'v_w_branch_b': out['v_w_branch_b'], 'v_w_out': out['v_w_out'], 'v_norm_ffn_g': out['v_norm_ffn_g'], 'v_w_up': out['v_w_up'], 'v_ffn_conv_w': out['v_ffn_conv_w'], 'v_ffn_conv_b': out['v_ffn_conv_b'], 'v_w_down': out['v_w_down'], 'v_norm_final_g': out['v_norm_final_g']}


def _loss(weights, diff, rest, loss_target):
    with _jax.named_scope("forward"):
        args = {**rest, TWIN_DIFF_INPUT: diff, **{k: w.astype(_WEIGHT_DTYPES[k]) for k, w in weights.items()}}
        y = _forward(args)
    with _jax.named_scope("loss_head"):
        err = _jnp.square(y.astype(_jnp.float32) - loss_target)
        return 0.5 * _jnp.sum(_jnp.mean(err, axis=-1)) if err.ndim else 0.5 * err


def _adamw(w, g, m, v):
    m = ADAM_B1 * m + (1.0 - ADAM_B1) * g
    v = ADAM_B2 * v + (1.0 - ADAM_B2) * _jnp.square(g)
    m_hat = m / (1.0 - ADAM_B1 ** ADAM_STEP)
    v_hat = v / (1.0 - ADAM_B2 ** ADAM_STEP)
    delta = -ADAM_LR * (m_hat / (_jnp.sqrt(v_hat) + ADAM_EPS) + ADAM_WD * w)
    return delta, m, v


def reference(x, c, w_ada, b_ada, norm_mix_g, w_in, rnn_conv_w, rnn_conv_b, lru_w_a, lru_b_a, lru_w_x, lru_b_x, lru_lambda, sgu_ln_g, sgu_ln_b, sgu_w_s, sgu_b_s, w_branch_a, w_branch_b, w_out, norm_ffn_g, w_up, ffn_conv_w, ffn_conv_b, w_down, norm_final_g, loss_target, m_w_ada, m_b_ada, m_norm_mix_g, m_w_in, m_rnn_conv_w, m_rnn_conv_b, m_lru_w_a, m_lru_b_a, m_lru_w_x, m_lru_b_x, m_lru_lambda, m_sgu_ln_g, m_sgu_ln_b, m_sgu_w_s, m_sgu_b_s, m_w_branch_a, m_w_branch_b, m_w_out, m_norm_ffn_g, m_w_up, m_ffn_conv_w, m_ffn_conv_b, m_w_down, m_norm_final_g, v_w_ada, v_b_ada, v_norm_mix_g, v_w_in, v_rnn_conv_w, v_rnn_conv_b, v_lru_w_a, v_lru_b_a, v_lru_w_x, v_lru_b_x, v_lru_lambda, v_sgu_ln_g, v_sgu_ln_b, v_sgu_w_s, v_sgu_b_s, v_w_branch_a, v_w_branch_b, v_w_out, v_norm_ffn_g, v_w_up, v_ffn_conv_w, v_ffn_conv_b, v_w_down, v_norm_final_g):
    given = dict(x=x, c=c, w_ada=w_ada, b_ada=b_ada, norm_mix_g=norm_mix_g, w_in=w_in, rnn_conv_w=rnn_conv_w, rnn_conv_b=rnn_conv_b, lru_w_a=lru_w_a, lru_b_a=lru_b_a, lru_w_x=lru_w_x, lru_b_x=lru_b_x, lru_lambda=lru_lambda, sgu_ln_g=sgu_ln_g, sgu_ln_b=sgu_ln_b, sgu_w_s=sgu_w_s, sgu_b_s=sgu_b_s, w_branch_a=w_branch_a, w_branch_b=w_branch_b, w_out=w_out, norm_ffn_g=norm_ffn_g, w_up=w_up, ffn_conv_w=ffn_conv_w, ffn_conv_b=ffn_conv_b, w_down=w_down, norm_final_g=norm_final_g, loss_target=loss_target, m_w_ada=m_w_ada, m_b_ada=m_b_ada, m_norm_mix_g=m_norm_mix_g, m_w_in=m_w_in, m_rnn_conv_w=m_rnn_conv_w, m_rnn_conv_b=m_rnn_conv_b, m_lru_w_a=m_lru_w_a, m_lru_b_a=m_lru_b_a, m_lru_w_x=m_lru_w_x, m_lru_b_x=m_lru_b_x, m_lru_lambda=m_lru_lambda, m_sgu_ln_g=m_sgu_ln_g, m_sgu_ln_b=m_sgu_ln_b, m_sgu_w_s=m_sgu_w_s, m_sgu_b_s=m_sgu_b_s, m_w_branch_a=m_w_branch_a, m_w_branch_b=m_w_branch_b, m_w_out=m_w_out, m_norm_ffn_g=m_norm_ffn_g, m_w_up=m_w_up, m_ffn_conv_w=m_ffn_conv_w, m_ffn_conv_b=m_ffn_conv_b, m_w_down=m_w_down, m_norm_final_g=m_norm_final_g, v_w_ada=v_w_ada, v_b_ada=v_b_ada, v_norm_mix_g=v_norm_mix_g, v_w_in=v_w_in, v_rnn_conv_w=v_rnn_conv_w, v_rnn_conv_b=v_rnn_conv_b, v_lru_w_a=v_lru_w_a, v_lru_b_a=v_lru_b_a, v_lru_w_x=v_lru_w_x, v_lru_b_x=v_lru_b_x, v_lru_lambda=v_lru_lambda, v_sgu_ln_g=v_sgu_ln_g, v_sgu_ln_b=v_sgu_ln_b, v_sgu_w_s=v_sgu_w_s, v_sgu_b_s=v_sgu_b_s, v_w_branch_a=v_w_branch_a, v_w_branch_b=v_w_branch_b, v_w_out=v_w_out, v_norm_ffn_g=v_norm_ffn_g, v_w_up=v_w_up, v_ffn_conv_w=v_ffn_conv_w, v_ffn_conv_b=v_ffn_conv_b, v_w_down=v_w_down, v_norm_final_g=v_norm_final_g)
    weights = {n: given[n] for n in TWIN_WEIGHTS}
    shared = {n: given[n] for n in SHARED_INPUTS}
    per_example = {n: given[n] for n in ['x', 'c']}
    grad_fn = _jax.value_and_grad(_loss, argnums=(0, 1))

    def one_microbatch(ex, loss_target):
        ex = dict(ex)
        diff = ex.pop(TWIN_DIFF_INPUT)
        return grad_fn(weights, diff, {**shared, **ex}, loss_target)

    if N_MICROBATCH == 1:
        loss, (grad_w, grad_x) = one_microbatch(per_example, given["loss_target"])
    else:
        def body(carry, xs):
            loss_sum, grad_sum = carry
            l_k, (gw_k, gx_k) = one_microbatch(xs[0], xs[1])
            with _jax.named_scope("update"):
                return (loss_sum + l_k, _jax.tree.map(_jnp.add, grad_sum, gw_k)), gx_k

        init = (_jnp.zeros((), _jnp.float32), _jax.tree.map(_jnp.zeros_like, weights))
        (loss, grad_w), grad_x = _jax.lax.scan(body, init, (per_example, given["loss_target"]))
    with _jax.named_scope("update"):
        delta_w, new_m, new_v = {}, {}, {}
        for n in TWIN_WEIGHTS:
            delta_w[n], new_m[n], new_v[n] = _adamw(weights[n], grad_w[n], given["m_" + n], given["v_" + n])
    return (loss, grad_x, *[grad_w[n] for n in TWIN_WEIGHTS], *[delta_w[n] for n in TWIN_WEIGHTS],
            *[new_m[n] for n in TWIN_WEIGHTS], *[new_v[n] for n in TWIN_WEIGHTS])
```

```python
import functools

import jax
import jax.numpy as jnp
from jax import lax
from jax.experimental import pallas as pl
from jax.experimental.pallas import tpu as pltpu

F32 = jnp.float32
BF = jnp.bfloat16

D = 1024
HEADS = 8
HD = D // HEADS
SGU_BLOCK = 128
N_CHIPS = 4
N_DEV = 8
EPS = 1e-6
LRU_C = 8.0
LANES = 128
SUBLANES = 8

ADAM_LR = 0.001
ADAM_B1 = 0.9
ADAM_B2 = 0.999
ADAM_EPS = 1e-08
ADAM_WD = 0.01
ADAM_STEP = 10

GELU_K0 = 0.7978845608028654
GELU_K1 = 0.044715

HBM_SPEC = pl.BlockSpec(memory_space=pltpu.HBM)
MESH_ID = pl.DeviceIdType.MESH


def _pcall(body, *, name, out_shape, grid=(), in_specs=None, out_specs=None, scratch=(), vmem_mb=32, aliases=None):
    kw = {}
    if aliases:
        kw["input_output_aliases"] = aliases
    if grid:
        params = pltpu.CompilerParams(dimension_semantics=("arbitrary",) * len(grid),
                                      vmem_limit_bytes=vmem_mb * 2 ** 20)
    else:
        params = pltpu.CompilerParams(vmem_limit_bytes=vmem_mb * 2 ** 20)
    return pl.pallas_call(body, name=name, out_shape=out_shape, grid=grid, in_specs=in_specs, out_specs=out_specs,
                          scratch_shapes=list(scratch), compiler_params=params, **kw)


def _gelu(x):
    return 0.5 * x * (1.0 + jnp.tanh(GELU_K0 * (x + GELU_K1 * x * x * x)))


def _gelu_and_grad(x):
    x2 = x * x
    t = jnp.tanh(GELU_K0 * x * (1.0 + GELU_K1 * x2))
    g = 0.5 * x * (1.0 + t)
    dg = 0.5 * (1.0 + t) + 0.5 * x * (1.0 - t * t) * (GELU_K0 * (1.0 + 3.0 * GELU_K1 * x2))
    return g, dg


def _sigmoid(x):
    return 1.0 / (1.0 + jnp.exp(-x))


def _log_sigmoid(x):
    e = jnp.exp(-jnp.abs(x))
    u = 1.0 + e
    d = u - 1.0
    l1p = jnp.where(d == 0.0, e, jnp.log(u) * (e / jnp.where(d == 0.0, 1.0, d)))
    return jnp.minimum(x, 0.0) - l1p


def _dot(a, b):
    return jnp.dot(a, b, preferred_element_type=F32)


def _dot_nt(a, b):
    return lax.dot_general(a, b, (((1,), (1,)), ((), ())), preferred_element_type=F32)


def _dot_tn(a, b):
    return lax.dot_general(a, b, (((0,), (0,)), ((), ())), preferred_element_type=F32)


def _shift_down(x, halo, s):
    r = pltpu.roll(x, s, 0)
    rows = lax.broadcasted_iota(jnp.int32, (SUBLANES, x.shape[1]), 0)
    head = jnp.where(rows < s, pltpu.roll(halo, s, 0), r[0:SUBLANES])
    return jnp.concatenate([head, r[SUBLANES:]], axis=0)


def _shift_up(x, halo, s):
    n = x.shape[0]
    r = pltpu.roll(x, n - s, 0)
    rows = lax.broadcasted_iota(jnp.int32, (SUBLANES, x.shape[1]), 0)
    tail = jnp.where(rows >= SUBLANES - s, pltpu.roll(halo, SUBLANES - s, 0), r[n - SUBLANES:n])
    return jnp.concatenate([r[:n - SUBLANES], tail], axis=0)


def _colsum(x):
    return jnp.sum(x, axis=0, keepdims=True)


def _rms_stats(x):
    r = lax.rsqrt(jnp.mean(x * x, axis=-1, keepdims=True) + EPS)
    return r, x * r


def _lru_gates(xc, wa_ref, ba, wx_ref, bx, lam):
    pr, pi = [], []
    for hh in range(HEADS):
        xh = xc[:, hh * HD:(hh + 1) * HD].astype(BF)
        pr.append(_dot(xh, wa_ref[hh].astype(BF)))
        pi.append(_dot(xh, wx_ref[hh].astype(BF)))
    r = _sigmoid(jnp.concatenate(pr, axis=1) + ba)
    ig = _sigmoid(jnp.concatenate(pi, axis=1) + bx)
    ls = _log_sigmoid(lam)
    log_a = LRU_C * r * ls
    a = jnp.exp(log_a)
    x2 = 2.0 * log_a
    u = a * a
    lu = jnp.log(jnp.maximum(u, 1e-37))
    em1 = jnp.where(lu == 0.0, x2, jnp.where(u < 1e-30, -1.0, (u - 1.0) * x2 / jnp.where(lu == 0.0, 1.0, lu)))
    mult = jnp.sqrt(-em1)
    return r, ig, ls, a, mult


def _sgu_mix(vln, ws_ref, bst_ref, tb):
    ri = lax.broadcasted_iota(jnp.int32, (SGU_BLOCK, SGU_BLOCK), 0)
    ci = lax.broadcasted_iota(jnp.int32, (SGU_BLOCK, SGU_BLOCK), 1)
    wm = [jnp.where(ri >= ci, ws_ref[g], 0.0).astype(BF) for g in range(HEADS)]
    blocks = []
    for blk in range(tb // SGU_BLOCK):
        cols = []
        for g in range(HEADS):
            vb = vln[blk * SGU_BLOCK:(blk + 1) * SGU_BLOCK, g * HD:(g + 1) * HD].astype(BF)
            cols.append(_dot(wm[g], vb) + bst_ref[:, g:g + 1])
        blocks.append(jnp.concatenate(cols, axis=1))
    mixed = blocks[0] if len(blocks) == 1 else jnp.concatenate(blocks, axis=0)
    return wm, mixed


def _layernorm_stats(v):
    mu = jnp.mean(v, axis=-1, keepdims=True)
    vc = v - mu
    rstd = lax.rsqrt(jnp.mean(vc * vc, axis=-1, keepdims=True) + EPS)
    return rstd, vc * rstd


def _my_xyc():
    return lax.axis_index("x"), lax.axis_index("y"), lax.axis_index("c")


def _exchange_xy(srcs, scatter, name):
    n = len(srcs)
    out_shape = []
    for s in srcs:
        shard = s.shape[1:] if scatter else s.shape
        out_shape.append(jax.ShapeDtypeStruct((N_CHIPS,) + tuple(shard), s.dtype))

    def body(*refs):
        src, out = refs[:n], refs[n:2 * n]
        send_sems, recv_sems, loc_sems = refs[2 * n:]
        x, y, c = _my_xyc()
        me = 2 * x + y
        remote, local = [], []
        for a in range(n):
            def piece(p, a=a):
                return src[a].at[p] if scatter else src[a]
            lc = pltpu.make_async_copy(piece(me), out[a].at[me], loc_sems.at[a])
            lc.start()
            local.append(lc)
            for k, (fx, fy) in enumerate(((1, 0), (0, 1), (1, 1))):
                px = 1 - x if fx else x
                py = 1 - y if fy else y
                cp = pltpu.make_async_remote_copy(
                    src_ref=piece(2 * px + py), dst_ref=out[a].at[me],
                    send_sem=send_sems.at[a, k], recv_sem=recv_sems.at[a, k],
                    device_id=(px, py, c), device_id_type=MESH_ID)
                cp.start()
                remote.append(cp)
        for cp in remote:
            cp.wait()
        for lc in local:
            lc.wait()

    return _pcall(body, name=name, out_shape=out_shape, in_specs=[HBM_SPEC] * n, out_specs=[HBM_SPEC] * n,
                  scratch=[pltpu.SemaphoreType.DMA((n, 3)), pltpu.SemaphoreType.DMA((n, 3)),
                           pltpu.SemaphoreType.DMA((n,))])(*srcs)


def _exchange_core(srcs, name):
    n = len(srcs)
    out_shape = [jax.ShapeDtypeStruct(s.shape, s.dtype) for s in srcs]

    def body(*refs):
        src, out = refs[:n], refs[n:2 * n]
        send_sems, recv_sems = refs[2 * n:]
        x, y, c = _my_xyc()
        cps = []
        for a in range(n):
            cp = pltpu.make_async_remote_copy(
                src_ref=src[a], dst_ref=out[a], send_sem=send_sems.at[a], recv_sem=recv_sems.at[a],
                device_id=(x, y, 1 - c), device_id_type=MESH_ID)
            cp.start()
            cps.append(cp)
        for cp in cps:
            cp.wait()

    return _pcall(body, name=name, out_shape=out_shape, in_specs=[HBM_SPEC] * n, out_specs=[HBM_SPEC] * n,
                  scratch=[pltpu.SemaphoreType.DMA((n,)), pltpu.SemaphoreType.DMA((n,))])(*srcs)


def _gather8(src, name):
    def body(src_ref, out_ref, send_sems, recv_sems, loc_sem):
        x, y, c = _my_xyc()
        me = 4 * x + 2 * y + c
        lc = pltpu.make_async_copy(src_ref, out_ref.at[me], loc_sem)
        lc.start()
        cps = []
        for k in range(1, N_DEV):
            px = 1 - x if (k >> 2) & 1 else x
            py = 1 - y if (k >> 1) & 1 else y
            pc = 1 - c if k & 1 else c
            cp = pltpu.make_async_remote_copy(
                src_ref=src_ref, dst_ref=out_ref.at[me], send_sem=send_sems.at[k - 1], recv_sem=recv_sems.at[k - 1],
                device_id=(px, py, pc), device_id_type=MESH_ID)
            cp.start()
            cps.append(cp)
        for cp in cps:
            cp.wait()
        lc.wait()

    return _pcall(body, name=name, out_shape=jax.ShapeDtypeStruct((N_DEV,) + src.shape, src.dtype),
                  in_specs=[HBM_SPEC], out_specs=HBM_SPEC,
                  scratch=[pltpu.SemaphoreType.DMA((N_DEV - 1,)), pltpu.SemaphoreType.DMA((N_DEV - 1,)),
                           pltpu.SemaphoreType.DMA])(src)


def _cast_shards(arrs):
    n = len(arrs)

    def body(*refs):
        for a in range(n):
            refs[n + a][...] = refs[a][...].astype(BF)

    specs = [pl.BlockSpec((s.shape[0] // 4, s.shape[1]), lambda i: (i, 0)) for s in arrs]
    return _pcall(body, name="cast_shards", grid=(4,), in_specs=specs, out_specs=specs,
                  out_shape=[jax.ShapeDtypeStruct(s.shape, BF) for s in arrs])(*arrs)


def _row_tile(rows, cols):
    t = rows
    while t * cols * 4 > (3 << 19) and t % 16 == 0:
        t //= 2
    return t


def _sum_parts(parts, name):
    p, rows, cols = parts.shape
    tr = _row_tile(rows, cols * p // 2)

    def body(p_ref, o_ref):
        acc = p_ref[0]
        for k in range(1, p):
            acc = acc + p_ref[k]
        o_ref[...] = acc

    return _pcall(body, name=name, grid=(rows // tr,),
                  in_specs=[pl.BlockSpec((p, tr, cols), lambda i: (0, i, 0))],
                  out_specs=pl.BlockSpec((tr, cols), lambda i: (i, 0)),
                  out_shape=jax.ShapeDtypeStruct((rows, cols), F32), vmem_mb=48)(parts)


def _adamw_math(w, g, m, v):
    m2 = ADAM_B1 * m + (1.0 - ADAM_B1) * g
    v2 = ADAM_B2 * v + (1.0 - ADAM_B2) * (g * g)
    m_hat = m2 / (1.0 - ADAM_B1 ** ADAM_STEP)
    v_hat = v2 / (1.0 - ADAM_B2 ** ADAM_STEP)
    delta = -ADAM_LR * (m_hat / (jnp.sqrt(v_hat) + ADAM_EPS) + ADAM_WD * w)
    return delta, m2, v2


def _adamw(w, m, v, grads, name):
    rows, cols = w.shape
    tr = _row_tile(rows, cols)
    ng = len(grads)

    def body(*refs):
        w_ref, m_ref, v_ref = refs[:3]
        g = refs[3][...]
        for k in range(1, ng):
            g = g + refs[3 + k][...]
        g_ref, d_ref, m2_ref, v2_ref = refs[3 + ng:]
        delta, m2, v2 = _adamw_math(w_ref[...], g, m_ref[...], v_ref[...])
        g_ref[...] = g
        d_ref[...] = delta
        m2_ref[...] = m2
        v2_ref[...] = v2

    spec = pl.BlockSpec((tr, cols), lambda i: (i, 0))
    return _pcall(body, name=name, grid=(rows // tr,), in_specs=[spec] * (3 + ng), out_specs=[spec] * 4,
                  out_shape=[jax.ShapeDtypeStruct((rows, cols), F32)] * 4, vmem_mb=48)(w, m, v, *grads)


def _ada_adamw(ct, dmod, w, m, v):
    rows, cols = w.shape
    tr = _row_tile(rows, cols)

    def body(ct_ref, dm_ref, w_ref, m_ref, v_ref, g_ref, d_ref, m2_ref, v2_ref):
        cv = ct_ref[...]
        ca = cv * _sigmoid(cv)
        g = ca[:, 0:1] * dm_ref[0:1, :]
        for b in range(1, N_DEV):
            g = g + ca[:, b:b + 1] * dm_ref[b:b + 1, :]
        delta, m2, v2 = _adamw_math(w_ref[...], g, m_ref[...], v_ref[...])
        g_ref[...] = g
        d_ref[...] = delta
        m2_ref[...] = m2
        v2_ref[...] = v2

    spec = pl.BlockSpec((tr, cols), lambda i: (i, 0))
    return _pcall(body, name="ada_adamw", grid=(rows // tr,),
                  in_specs=[pl.BlockSpec((tr, N_DEV), lambda i: (i, 0)), pl.BlockSpec((N_DEV, cols), lambda i: (0, 0)),
                            spec, spec, spec],
                  out_specs=[spec] * 4, out_shape=[jax.ShapeDtypeStruct((rows, cols), F32)] * 4,
                  vmem_mb=48)(ct, dmod, w, m, v)


def _mod_fwd(c_all, w, b):
    cols = w.shape[1]
    tn = cols // 3

    def body(c_ref, w_ref, b_ref, o_ref):
        cv = c_ref[...]
        ca = (cv * _sigmoid(cv)).astype(BF)
        o_ref[...] = _dot(ca, w_ref[...].astype(BF)) + b_ref[...]

    return _pcall(body, name="mod_fwd", grid=(3,),
                  in_specs=[pl.BlockSpec((N_DEV, D), lambda j: (0, 0)), pl.BlockSpec((D, tn), lambda j: (0, j)),
                            pl.BlockSpec((1, tn), lambda j: (0, j))],
                  out_specs=pl.BlockSpec((N_DEV, tn), lambda j: (0, j)),
                  out_shape=jax.ShapeDtypeStruct((N_DEV, cols), F32))(c_all, w, b)


def _modnorm_matmul(x, g, scale, shift, w4, name, tm=512, tn=768):
    T = x.shape[0]
    tm = min(tm, T)
    ns = w4.shape[2]
    nj = ns // tn

    def body(x_ref, g_ref, sc_ref, sh_ref, w_ref, h_ref, z_ref, hs):
        j = pl.program_id(1)

        @pl.when(j == 0)
        def _():
            xv = x_ref[...]
            _, xh = _rms_stats(xv)
            h = (xh * g_ref[...]) * (1.0 + sc_ref[...]) + sh_ref[...]
            hs[...] = h.astype(BF)
            h_ref[...] = hs[...]

        z_ref[...] = _dot(hs[...], w_ref[0])

    vec = pl.BlockSpec((1, D), lambda i, j: (0, 0))
    return _pcall(body, name=name, grid=(T // tm, N_CHIPS * nj),
                  in_specs=[pl.BlockSpec((tm, D), lambda i, j: (i, 0)), vec, vec, vec,
                            pl.BlockSpec((1, D, tn), lambda i, j: (j // nj, 0, j % nj))],
                  out_specs=[pl.BlockSpec((tm, D), lambda i, j: (i, 0)), pl.BlockSpec((tm, tn), lambda i, j: (i, j))],
                  out_shape=[jax.ShapeDtypeStruct((T, D), BF), jax.ShapeDtypeStruct((T, N_CHIPS * ns), F32)],
                  scratch=[pltpu.VMEM((tm, D), BF)])(x, g, scale, shift, w4)


def _rglru_fwd(z, cw, cb, wa, ba, wx, bx, lam, tb=256):
    T = z.shape[0]
    tb = min(tb, T)

    def body(xr_ref, gr_ref, cw_ref, cb_ref, wa_ref, ba_ref, wx_ref, bx_ref, lam_ref, h_ref, ya_ref, prev, hc):
        i = pl.program_id(0)

        @pl.when(i == 0)
        def _():
            prev[...] = jnp.zeros_like(prev)
            hc[...] = jnp.zeros_like(hc)

        xr = xr_ref[...]
        pv = prev[...]
        xc = (cb_ref[...] + cw_ref[3:4, :] * xr + cw_ref[2:3, :] * _shift_down(xr, pv, 1)
              + cw_ref[1:2, :] * _shift_down(xr, pv, 2) + cw_ref[0:1, :] * _shift_down(xr, pv, 3))
        prev[...] = xr[tb - SUBLANES:tb]
        _, ig, _, a, mult = _lru_gates(xc, wa_ref, ba_ref[...], wx_ref, bx_ref[...], lam_ref[...])
        u = mult * (ig * xc)
        rows = lax.broadcasted_iota(jnp.int32, (tb, D), 0)
        d = 1
        while d < tb:
            keep = rows >= d
            a_s = jnp.where(keep, pltpu.roll(a, d, 0), 1.0)
            u_s = jnp.where(keep, pltpu.roll(u, d, 0), 0.0)
            u = a * u_s + u
            a = a * a_s
            d *= 2
        h = u + a * hc[SUBLANES - 1:SUBLANES, :]
        hc[...] = h[tb - SUBLANES:tb]
        h_ref[...] = h
        ya_ref[...] = (h * _gelu(gr_ref[...])).astype(BF)

    vec = pl.BlockSpec((1, D), lambda i: (0, 0))
    wspec = pl.BlockSpec((HEADS, HD, HD), lambda i: (0, 0, 0))
    return _pcall(body, name="rglru_fwd", grid=(T // tb,),
                  in_specs=[pl.BlockSpec((tb, D), lambda i: (i, 0)), pl.BlockSpec((tb, D), lambda i: (i, 1)),
                            pl.BlockSpec((4, D), lambda i: (0, 0)), vec, wspec, vec, wspec, vec, vec],
                  out_specs=[pl.BlockSpec((tb, D), lambda i: (i, 0))] * 2,
                  out_shape=[jax.ShapeDtypeStruct((T, D), F32), jax.ShapeDtypeStruct((T, D), BF)],
                  scratch=[pltpu.VMEM((SUBLANES, D), F32), pltpu.VMEM((SUBLANES, D), F32)],
                  vmem_mb=48)(z, z, cw, cb, wa, ba, wx, bx, lam)


def _sgu_fwd(z, lg, lb, ws, bst, tb=256):
    T = z.shape[0]
    tb = min(tb, T)

    def body(zu_ref, zv_ref, lg_ref, lb_ref, ws_ref, bst_ref, yb_ref):
        _, xh = _layernorm_stats(_gelu(zv_ref[...]))
        vln = xh * lg_ref[...] + lb_ref[...]
        _, mixed = _sgu_mix(vln, ws_ref, bst_ref, tb)
        yb_ref[...] = (_gelu(zu_ref[...]) * mixed).astype(BF)

    vec = pl.BlockSpec((1, D), lambda i: (0, 0))
    return _pcall(body, name="sgu_fwd", grid=(T // tb,),
                  in_specs=[pl.BlockSpec((tb, D), lambda i: (i, 2)), pl.BlockSpec((tb, D), lambda i: (i, 3)), vec, vec,
                            pl.BlockSpec((HEADS, SGU_BLOCK, SGU_BLOCK), lambda i: (0, 0, 0)),
                            pl.BlockSpec((SGU_BLOCK, HEADS), lambda i: (0, 0))],
                  out_specs=pl.BlockSpec((tb, D), lambda i: (i, 0)),
                  out_shape=jax.ShapeDtypeStruct((T, D), BF))(z, z, lg, lb, ws, bst)


def _mix_out(ya_pre, yb_pre, z, x, gate1, wba, wbb, wo, tm=256):
    T = x.shape[0]
    tm = min(tm, T)

    def body(yap_ref, ybp_ref, ga_ref, gb_ref, x_ref, g1_ref, wa_ref, wb_ref, wo_ref,
             x2_ref, mg_ref, ya_ref, yb_ref, o_ref):
        ya = _dot(yap_ref[...], wa_ref[...])
        yb = _dot(ybp_ref[...], wb_ref[...])
        merged = (_sigmoid(ga_ref[...]) * ya + _sigmoid(gb_ref[...]) * yb).astype(BF)
        o = _dot(merged, wo_ref[...])
        x2_ref[...] = x_ref[...] + g1_ref[...] * o
        mg_ref[...] = merged
        ya_ref[...] = ya.astype(BF)
        yb_ref[...] = yb.astype(BF)
        o_ref[...] = o.astype(BF)

    row = pl.BlockSpec((tm, D), lambda i: (i, 0))
    wspec = pl.BlockSpec((D, D), lambda i: (0, 0))
    return _pcall(body, name="mix_out", grid=(T // tm,),
                  in_specs=[row, row, pl.BlockSpec((tm, D), lambda i: (i, 4)), pl.BlockSpec((tm, D), lambda i: (i, 5)),
                            row, pl.BlockSpec((1, D), lambda i: (0, 0)), wspec, wspec, wspec],
                  out_specs=[row] * 5,
                  out_shape=[jax.ShapeDtypeStruct((T, D), F32)] + [jax.ShapeDtypeStruct((T, D), BF)] * 4,
                  vmem_mb=48)(ya_pre, yb_pre, z, z, x, gate1, wba, wbb, wo)


def _ffn_gate(up, cw, cb, tm=512, cw_blk=768):
    T = up.shape[0]
    tm = min(tm, T)
    dff = up.shape[1] // 2
    ncb = dff // cw_blk

    def body(ua_ref, uv_ref, wa_ref, wv_ref, ba_ref, bv_ref, f_ref, pa, pv):
        i = pl.program_id(1)

        @pl.when(i == 0)
        def _():
            pa[...] = jnp.zeros_like(pa)
            pv[...] = jnp.zeros_like(pv)

        def conv(u_ref, w_ref, b_ref, prev):
            u = u_ref[...]
            p = prev[...]
            hid = (b_ref[...] + w_ref[2:3, :] * u + w_ref[1:2, :] * _shift_down(u, p, 1)
                   + w_ref[0:1, :] * _shift_down(u, p, 2))
            prev[...] = u[tm - SUBLANES:tm]
            return hid

        act = conv(ua_ref, wa_ref, ba_ref, pa)
        val = conv(uv_ref, wv_ref, bv_ref, pv)
        f_ref[...] = (_gelu(act) * val).astype(BF)

    return _pcall(body, name="ffn_gate", grid=(ncb, T // tm),
                  in_specs=[pl.BlockSpec((tm, cw_blk), lambda cbk, i: (i, cbk)),
                            pl.BlockSpec((tm, cw_blk), lambda cbk, i: (i, ncb + cbk)),
                            pl.BlockSpec((3, cw_blk), lambda cbk, i: (0, cbk)),
                            pl.BlockSpec((3, cw_blk), lambda cbk, i: (0, ncb + cbk)),
                            pl.BlockSpec((1, cw_blk), lambda cbk, i: (0, cbk)),
                            pl.BlockSpec((1, cw_blk), lambda cbk, i: (0, ncb + cbk))],
                  out_specs=pl.BlockSpec((tm, cw_blk), lambda cbk, i: (i, cbk)),
                  out_shape=jax.ShapeDtypeStruct((T, dff), BF),
                  scratch=[pltpu.VMEM((SUBLANES, cw_blk), F32)] * 2)(up, up, cw, cw, cb, cb)


def _ffn_down_loss(f, wd, x2, gate2, gf, target, tm=512):
    T = x2.shape[0]
    tm = min(tm, T)
    dff = f.shape[1]

    def body(f_ref, wd_ref, x2_ref, g2_ref, gf_ref, t_ref, loss_ref, dx3_ref, dfo_ref, dgf_ref, dg2_ref):
        i = pl.program_id(0)

        @pl.when(i == 0)
        def _():
            loss_ref[...] = jnp.zeros_like(loss_ref)
            dgf_ref[...] = jnp.zeros_like(dgf_ref)
            dg2_ref[...] = jnp.zeros_like(dg2_ref)

        fo = _dot(f_ref[...], wd_ref[...])
        x3 = x2_ref[...] + g2_ref[...] * fo
        rstd, xh = _rms_stats(x3)
        err = xh * gf_ref[...] - t_ref[...]
        loss_ref[...] += 0.5 * jnp.sum(jnp.mean(err * err, axis=-1, keepdims=True), axis=0, keepdims=True)
        dy = err * (1.0 / D)
        dgf_ref[...] += _colsum(dy * xh)
        dxh = dy * gf_ref[...]
        dx3 = rstd * (dxh - xh * jnp.mean(dxh * xh, axis=-1, keepdims=True))
        dg2_ref[...] += _colsum(dx3 * fo)
        dx3_ref[...] = dx3
        dfo_ref[...] = (g2_ref[...] * dx3).astype(BF)

    row = pl.BlockSpec((tm, D), lambda i: (i, 0))
    vec = pl.BlockSpec((1, D), lambda i: (0, 0))
    return _pcall(body, name="ffn_down_loss", grid=(T // tm,),
                  in_specs=[pl.BlockSpec((tm, dff), lambda i: (i, 0)), pl.BlockSpec((dff, D), lambda i: (0, 0)),
                            row, vec, vec, row],
                  out_specs=[pl.BlockSpec((1, LANES), lambda i: (0, 0)), row, row, vec, vec],
                  out_shape=[jax.ShapeDtypeStruct((1, LANES), F32), jax.ShapeDtypeStruct((T, D), F32),
                             jax.ShapeDtypeStruct((T, D), BF), jax.ShapeDtypeStruct((1, D), F32),
                             jax.ShapeDtypeStruct((1, D), F32)],
                  vmem_mb=48)(f, wd, x2, gate2, gf, target)


def _halo_spec(tm, cols, col_blk, nrow):
    per = tm // SUBLANES
    return pl.BlockSpec((SUBLANES, cols), lambda cbk, i: (jnp.maximum((nrow - 1 - i) * per - 1, 0), col_blk(cbk)))


def _ffn_bwd(dfo, wd, up, cw, cb, tm=512, cw_blk=768):
    T = up.shape[0]
    tm = min(tm, T)
    dff = up.shape[1] // 2
    ncb = dff // cw_blk
    nrow = T // tm

    def body(dfo_ref, wd_ref, ua_ref, uv_ref, ha_ref, hv_ref, wa_ref, wv_ref, ba_ref, bv_ref,
             du_ref, dwd_ref, dwa_ref, dwv_ref, dba_ref, dbv_ref, na, nv):
        i = pl.program_id(1)
        first_block = i == nrow - 1

        @pl.when(i == 0)
        def _():
            na[...] = jnp.zeros_like(na)
            nv[...] = jnp.zeros_like(nv)
            dwd_ref[...] = jnp.zeros_like(dwd_ref)
            dwa_ref[...] = jnp.zeros_like(dwa_ref)
            dwv_ref[...] = jnp.zeros_like(dwv_ref)
            dba_ref[...] = jnp.zeros_like(dba_ref)
            dbv_ref[...] = jnp.zeros_like(dbv_ref)

        def conv(u_ref, halo_ref, w_ref, b_ref):
            u = u_ref[...]
            p = jnp.where(first_block, 0.0, halo_ref[...])
            s1 = _shift_down(u, p, 1)
            s2 = _shift_down(u, p, 2)
            hid = b_ref[...] + w_ref[2:3, :] * u + w_ref[1:2, :] * s1 + w_ref[0:1, :] * s2
            return u, s1, s2, hid

        ua, ua1, ua2, act = conv(ua_ref, ha_ref, wa_ref, ba_ref)
        uv, uv1, uv2, val = conv(uv_ref, hv_ref, wv_ref, bv_ref)
        ga, dga = _gelu_and_grad(act)
        dfo_t = dfo_ref[...]
        dwd_ref[...] += _dot_tn((ga * val).astype(BF), dfo_t)
        df = _dot_nt(dfo_t, wd_ref[...])
        dact = df * val * dga
        dval = df * ga

        def conv_bwd(dh, u, u1, u2, w_ref, nxt, col, dw_ref, db_ref):
            n8 = nxt[...]
            du = w_ref[2:3, :] * dh + w_ref[1:2, :] * _shift_up(dh, n8, 1) + w_ref[0:1, :] * _shift_up(dh, n8, 2)
            nxt[...] = dh[0:SUBLANES]
            du_ref[:, col:col + cw_blk] = du.astype(BF)
            dw_ref[2:3, :] += _colsum(dh * u)
            dw_ref[1:2, :] += _colsum(dh * u1)
            dw_ref[0:1, :] += _colsum(dh * u2)
            db_ref[...] += _colsum(dh)

        conv_bwd(dact, ua, ua1, ua2, wa_ref, na, 0, dwa_ref, dba_ref)
        conv_bwd(dval, uv, uv1, uv2, wv_ref, nv, cw_blk, dwv_ref, dbv_ref)

    rev = lambda cbk, i: (nrow - 1 - i, cbk)
    rev_v = lambda cbk, i: (nrow - 1 - i, ncb + cbk)
    w3a = pl.BlockSpec((3, cw_blk), lambda cbk, i: (0, cbk))
    w3v = pl.BlockSpec((3, cw_blk), lambda cbk, i: (0, ncb + cbk))
    b1a = pl.BlockSpec((1, cw_blk), lambda cbk, i: (0, cbk))
    b1v = pl.BlockSpec((1, cw_blk), lambda cbk, i: (0, ncb + cbk))
    return _pcall(body, name="ffn_bwd", grid=(ncb, nrow),
                  in_specs=[pl.BlockSpec((tm, D), lambda cbk, i: (nrow - 1 - i, 0)),
                            pl.BlockSpec((cw_blk, D), lambda cbk, i: (cbk, 0)),
                            pl.BlockSpec((tm, cw_blk), rev), pl.BlockSpec((tm, cw_blk), rev_v),
                            _halo_spec(tm, cw_blk, lambda cbk: cbk, nrow),
                            _halo_spec(tm, cw_blk, lambda cbk: ncb + cbk, nrow),
                            w3a, w3v, b1a, b1v],
                  out_specs=[pl.BlockSpec((tm, 2 * cw_blk), rev),
                             pl.BlockSpec((cw_blk, D), lambda cbk, i: (cbk, 0)), w3a, w3a, b1a, b1a],
                  out_shape=[jax.ShapeDtypeStruct((T, 2 * dff), BF),
                             jax.ShapeDtypeStruct((dff, D), F32),
                             jax.ShapeDtypeStruct((3, dff), F32), jax.ShapeDtypeStruct((3, dff), F32),
                             jax.ShapeDtypeStruct((1, dff), F32), jax.ShapeDtypeStruct((1, dff), F32)],
                  scratch=[pltpu.VMEM((SUBLANES, cw_blk), F32)] * 2,
                  vmem_mb=48)(dfo, wd, up, up, up, up, cw, cw, cb, cb)


def _ffn_col_block(t, ncb):
    return jnp.where(t < ncb, 2 * t, 2 * (t - ncb) + 1)


def _mm_tn_cols(a, b, name, nshard, nb, colmap=None, tm=512):
    T, M = a.shape
    tm = min(tm, T)
    ns = b.shape[1] // nshard
    per = ns // nb
    cmap = colmap if colmap is not None else (lambda t: t)

    def body(a_ref, b_ref, o_ref):
        k = pl.program_id(1)

        @pl.when(k == 0)
        def _():
            o_ref[...] = jnp.zeros_like(o_ref)

        o_ref[0] += _dot_tn(a_ref[...], b_ref[...])

    return _pcall(body, name=name, grid=(nshard * per, T // tm),
                  in_specs=[pl.BlockSpec((tm, M), lambda t, k: (k, 0)),
                            pl.BlockSpec((tm, nb), lambda t, k: (k, cmap(t)))],
                  out_specs=pl.BlockSpec((1, M, nb), lambda t, k: (t // per, 0, t % per)),
                  out_shape=jax.ShapeDtypeStruct((nshard, M, ns), F32), vmem_mb=48)(a, b)


def _mm_nt_normbwd(dz, w4, x, resid, g, scale, name, gate=None, o=None, colmap=None, tm=512, tk=768):
    T = x.shape[0]
    tm = min(tm, T)
    ns = w4.shape[2]
    nj = ns // tk
    nk = N_CHIPS * nj
    gated = gate is not None
    cmap = colmap if colmap is not None else (lambda t: t)

    def body(*refs):
        if gated:
            (dz_ref, w_ref, x_ref, r_ref, g_ref, sc_ref, gt_ref, o_ref,
             dx_ref, dsh_ref, dsc_ref, dg_ref, do_ref, dgt_ref, acc) = refs
        else:
            dz_ref, w_ref, x_ref, r_ref, g_ref, sc_ref, dx_ref, dsh_ref, dsc_ref, dg_ref, acc = refs
        i = pl.program_id(0)
        k = pl.program_id(1)

        @pl.when((i == 0) & (k == 0))
        def _():
            dsh_ref[...] = jnp.zeros_like(dsh_ref)
            dsc_ref[...] = jnp.zeros_like(dsc_ref)
            dg_ref[...] = jnp.zeros_like(dg_ref)
            if gated:
                dgt_ref[...] = jnp.zeros_like(dgt_ref)

        @pl.when(k == 0)
        def _():
            acc[...] = jnp.zeros_like(acc)

        acc[...] += _dot_nt(dz_ref[...], w_ref[0])

        @pl.when(k == nk - 1)
        def _():
            dh = acc[...]
            rstd, xh = _rms_stats(x_ref[...])
            dsh_ref[...] += _colsum(dh)
            dsc_ref[...] += _colsum(dh * (xh * g_ref[...]))
            dn = dh * (1.0 + sc_ref[...])
            dg_ref[...] += _colsum(dn * xh)
            dxh = dn * g_ref[...]
            dx = r_ref[...] + rstd * (dxh - xh * jnp.mean(dxh * xh, axis=-1, keepdims=True))
            dx_ref[...] = dx
            if gated:
                do_ref[...] = (gt_ref[...] * dx).astype(BF)
                dgt_ref[...] += _colsum(dx * o_ref[...].astype(F32))

    row = pl.BlockSpec((tm, D), lambda i, k: (i, 0))
    vec = pl.BlockSpec((1, D), lambda i, k: (0, 0))
    in_specs = [pl.BlockSpec((tm, tk), lambda i, k: (i, cmap(k))),
                pl.BlockSpec((1, D, tk), lambda i, k: (k // nj, 0, k % nj)), row, row, vec, vec]
    out_specs = [row, vec, vec, vec]
    out_shape = [jax.ShapeDtypeStruct((T, D), F32)] + [jax.ShapeDtypeStruct((1, D), F32)] * 3
    args = [dz, w4, x, resid, g, scale]
    if gated:
        in_specs += [vec, row]
        out_specs += [row, vec]
        out_shape += [jax.ShapeDtypeStruct((T, D), BF), jax.ShapeDtypeStruct((1, D), F32)]
        args += [gate, o]
    return _pcall(body, name=name, grid=(T // tm, nk), in_specs=in_specs, out_specs=out_specs, out_shape=out_shape,
                  scratch=[pltpu.VMEM((tm, D), F32)], vmem_mb=48)(*args)


def _mix_bwd(do, ya, yb, z, wo, wba, wbb, tm=256):
    T = do.shape[0]
    tm = min(tm, T)

    def body(do_ref, ya_ref, yb_ref, ga_ref, gb_ref, wo_ref, wa_ref, wb_ref,
             dz_ref, dya_ref, dyb_ref, dyap_ref, dybp_ref):
        dm = _dot_nt(do_ref[...], wo_ref[...])
        sa = _sigmoid(ga_ref[...])
        sb = _sigmoid(gb_ref[...])
        dya = (sa * dm).astype(BF)
        dyb = (sb * dm).astype(BF)
        dz_ref[:, 0:D] = (dm * ya_ref[...].astype(F32) * sa * (1.0 - sa)).astype(BF)
        dz_ref[:, D:2 * D] = (dm * yb_ref[...].astype(F32) * sb * (1.0 - sb)).astype(BF)
        dya_ref[...] = dya
        dyb_ref[...] = dyb
        dyap_ref[...] = _dot_nt(dya, wa_ref[...]).astype(BF)
        dybp_ref[...] = _dot_nt(dyb, wb_ref[...]).astype(BF)

    row = pl.BlockSpec((tm, D), lambda i: (i, 0))
    wspec = pl.BlockSpec((D, D), lambda i: (0, 0))
    return _pcall(body, name="mix_bwd", grid=(T // tm,),
                  in_specs=[row, row, row, pl.BlockSpec((tm, D), lambda i: (i, 4)),
                            pl.BlockSpec((tm, D), lambda i: (i, 5)), wspec, wspec, wspec],
                  out_specs=[pl.BlockSpec((tm, 2 * D), lambda i: (i, 2)), row, row, row, row],
                  out_shape=[jax.ShapeDtypeStruct((T, 6 * D), BF)] + [jax.ShapeDtypeStruct((T, D), BF)] * 4,
                  vmem_mb=48)(do, ya, yb, z, z, wo, wba, wbb)


def _sgu_bwd(dz, dyb_pre, z, lg, lb, ws, bst, tb=256):
    T = z.shape[0]
    tb = min(tb, T)

    def body(dz_in, dy_ref, zu_ref, zv_ref, lg_ref, lb_ref, ws_ref, bst_ref,
             dz_ref, dws_ref, dbst_ref, dlg_ref, dlb_ref):
        del dz_in
        i = pl.program_id(0)

        @pl.when(i == 0)
        def _():
            dws_ref[...] = jnp.zeros_like(dws_ref)
            dbst_ref[...] = jnp.zeros_like(dbst_ref)
            dlg_ref[...] = jnp.zeros_like(dlg_ref)
            dlb_ref[...] = jnp.zeros_like(dlb_ref)

        gu, dgu = _gelu_and_grad(zu_ref[...])
        gv, dgv = _gelu_and_grad(zv_ref[...])
        rstd, xh = _layernorm_stats(gv)
        vln = xh * lg_ref[...] + lb_ref[...]
        wm, mixed = _sgu_mix(vln, ws_ref, bst_ref, tb)
        dy = dy_ref[...].astype(F32)
        dz_ref[:, 0:D] = (dy * mixed * dgu).astype(BF)
        dmixed = dy * gu
        ri = lax.broadcasted_iota(jnp.int32, (SGU_BLOCK, SGU_BLOCK), 0)
        ci = lax.broadcasted_iota(jnp.int32, (SGU_BLOCK, SGU_BLOCK), 1)
        blocks = []
        for blk in range(tb // SGU_BLOCK):
            rs = slice(blk * SGU_BLOCK, (blk + 1) * SGU_BLOCK)
            cols = []
            for g in range(HEADS):
                cs = slice(g * HD, (g + 1) * HD)
                dmg = dmixed[rs, cs]
                dmb = dmg.astype(BF)
                dbst_ref[:, g:g + 1] += jnp.sum(dmg, axis=1, keepdims=True)
                dws_ref[g] += jnp.where(ri >= ci, _dot_nt(dmb, vln[rs, cs].astype(BF)), 0.0)
                cols.append(_dot_tn(wm[g], dmb))
            blocks.append(jnp.concatenate(cols, axis=1))
        dvln = blocks[0] if len(blocks) == 1 else jnp.concatenate(blocks, axis=0)
        dlg_ref[...] += _colsum(dvln * xh)
        dlb_ref[...] += _colsum(dvln)
        dxh = dvln * lg_ref[...]
        dgv_in = rstd * (dxh - jnp.mean(dxh, axis=-1, keepdims=True)
                         - xh * jnp.mean(dxh * xh, axis=-1, keepdims=True))
        dz_ref[:, D:2 * D] = (dgv_in * dgv).astype(BF)

    row = pl.BlockSpec((tb, D), lambda i: (i, 0))
    vec = pl.BlockSpec((1, D), lambda i: (0, 0))
    wspec = pl.BlockSpec((HEADS, SGU_BLOCK, SGU_BLOCK), lambda i: (0, 0, 0))
    bspec = pl.BlockSpec((SGU_BLOCK, HEADS), lambda i: (0, 0))
    return _pcall(body, name="sgu_bwd", grid=(T // tb,),
                  in_specs=[HBM_SPEC, row, pl.BlockSpec((tb, D), lambda i: (i, 2)),
                            pl.BlockSpec((tb, D), lambda i: (i, 3)), vec, vec, wspec, bspec],
                  out_specs=[pl.BlockSpec((tb, 2 * D), lambda i: (i, 1)), wspec, bspec, vec, vec],
                  out_shape=[jax.ShapeDtypeStruct(dz.shape, BF),
                             jax.ShapeDtypeStruct((HEADS, SGU_BLOCK, SGU_BLOCK), F32),
                             jax.ShapeDtypeStruct((SGU_BLOCK, HEADS), F32),
                             jax.ShapeDtypeStruct((1, D), F32), jax.ShapeDtypeStruct((1, D), F32)],
                  aliases={0: 0}, vmem_mb=48)(dz, dyb_pre, z, z, lg, lb, ws, bst)


def _rglru_bwd(dz, dya_pre, z, h, cw, cb, wa, ba, wx, bx, lam, tb=256):
    T = z.shape[0]
    tb = min(tb, T)
    nrow = T // tb
    per = tb // SUBLANES

    def body(dz_in, dy_ref, xr_ref, xh_ref, gr_ref, h_ref, hh_ref, cw_ref, cb_ref, wa_ref, ba_ref, wx_ref, bx_ref,
             lam_ref, dz_ref, dcw_ref, dcb_ref, dwa_ref, dba_ref, dwx_ref, dbx_ref, dlam_ref, carry, nxt):
        del dz_in
        i = pl.program_id(0)
        first_block = i == nrow - 1

        @pl.when(i == 0)
        def _():
            carry[...] = jnp.zeros_like(carry)
            nxt[...] = jnp.zeros_like(nxt)
            for ref in (dcw_ref, dcb_ref, dwa_ref, dba_ref, dwx_ref, dbx_ref, dlam_ref):
                ref[...] = jnp.zeros_like(ref)

        xr = xr_ref[...]
        pv = jnp.where(first_block, 0.0, xh_ref[...])
        s1 = _shift_down(xr, pv, 1)
        s2 = _shift_down(xr, pv, 2)
        s3 = _shift_down(xr, pv, 3)
        xc = cb_ref[...] + cw_ref[3:4, :] * xr + cw_ref[2:3, :] * s1 + cw_ref[1:2, :] * s2 + cw_ref[0:1, :] * s3
        lam = lam_ref[...]
        r, ig, ls, a, mult = _lru_gates(xc, wa_ref, ba_ref[...], wx_ref, bx_ref[...], lam)
        hv = h_ref[...]
        hprev = _shift_down(hv, jnp.where(first_block, 0.0, hh_ref[...]), 1)
        gg, dgg = _gelu_and_grad(gr_ref[...])
        dy = dy_ref[...].astype(F32)
        dz_ref[:, D:2 * D] = (dy * hv * dgg).astype(BF)

        rows = lax.broadcasted_iota(jnp.int32, (tb, D), 0)
        v = dy * gg + jnp.where(rows == tb - 1, carry[0:1, :], 0.0)
        q = jnp.where(rows < tb - 1, pltpu.roll(a, tb - 1, 0), 0.0)
        d = 1
        while d < tb:
            keep = rows < tb - d
            q_s = jnp.where(keep, pltpu.roll(q, tb - d, 0), 1.0)
            v_s = jnp.where(keep, pltpu.roll(v, tb - d, 0), 0.0)
            v = v + q * v_s
            q = q * q_s
            d *= 2
        gsc = v
        carry[...] = (a * gsc)[0:SUBLANES]

        xi = ig * xc
        dmult = gsc * xi
        dxi = gsc * mult
        dig = dxi * xc
        dxc = dxi * ig
        dlog_a = gsc * hprev * a - dmult * (a * a) / mult
        dlam_ref[...] += _colsum(dlog_a * r) * (LRU_C * _sigmoid(-lam))
        dpr = dlog_a * (LRU_C * ls) * r * (1.0 - r)
        dpi = dig * ig * (1.0 - ig)
        dba_ref[...] += _colsum(dpr)
        dbx_ref[...] += _colsum(dpi)
        back = []
        for hh in range(HEADS):
            cs = slice(hh * HD, (hh + 1) * HD)
            xh = xc[:, cs].astype(BF)
            dprh = dpr[:, cs].astype(BF)
            dpih = dpi[:, cs].astype(BF)
            dwa_ref[hh] += _dot_tn(xh, dprh)
            dwx_ref[hh] += _dot_tn(xh, dpih)
            back.append(_dot_nt(dprh, wa_ref[hh].astype(BF)) + _dot_nt(dpih, wx_ref[hh].astype(BF)))
        dxc = dxc + jnp.concatenate(back, axis=1)

        n8 = nxt[...]
        dxr = (cw_ref[3:4, :] * dxc + cw_ref[2:3, :] * _shift_up(dxc, n8, 1)
               + cw_ref[1:2, :] * _shift_up(dxc, n8, 2) + cw_ref[0:1, :] * _shift_up(dxc, n8, 3))
        nxt[...] = dxc[0:SUBLANES]
        dz_ref[:, 0:D] = dxr.astype(BF)
        dcw_ref[3:4, :] += _colsum(dxc * xr)
        dcw_ref[2:3, :] += _colsum(dxc * s1)
        dcw_ref[1:2, :] += _colsum(dxc * s2)
        dcw_ref[0:1, :] += _colsum(dxc * s3)
        dcb_ref[...] += _colsum(dxc)

    rev = lambda col: (lambda i: (nrow - 1 - i, col))
    halo = lambda col: pl.BlockSpec((SUBLANES, D), lambda i: (jnp.maximum((nrow - 1 - i) * per - 1, 0), col))
    vec = pl.BlockSpec((1, D), lambda i: (0, 0))
    wspec = pl.BlockSpec((HEADS, HD, HD), lambda i: (0, 0, 0))
    c4 = pl.BlockSpec((4, D), lambda i: (0, 0))
    wshape = jax.ShapeDtypeStruct((HEADS, HD, HD), F32)
    vshape = jax.ShapeDtypeStruct((1, D), F32)
    return _pcall(body, name="rglru_bwd", grid=(nrow,),
                  in_specs=[HBM_SPEC, pl.BlockSpec((tb, D), rev(0)), pl.BlockSpec((tb, D), rev(0)), halo(0),
                            pl.BlockSpec((tb, D), rev(1)), pl.BlockSpec((tb, D), rev(0)), halo(0),
                            c4, vec, wspec, vec, wspec, vec, vec],
                  out_specs=[pl.BlockSpec((tb, 2 * D), rev(0)), c4, vec, wspec, vec, wspec, vec, vec],
                  out_shape=[jax.ShapeDtypeStruct(dz.shape, BF), jax.ShapeDtypeStruct((4, D), F32), vshape,
                             wshape, vshape, wshape, vshape, vshape],
                  scratch=[pltpu.VMEM((SUBLANES, D), F32), pltpu.VMEM((SUBLANES, D), F32)],
                  aliases={0: 0}, vmem_mb=56)(dz, dya_pre, z, z, z, h, h, cw, cb, wa, ba, wx, bx, lam)


def _pack_rows(parts):
    out = []
    for p in parts:
        q = p.reshape(-1, LANES)
        pad = (-q.shape[0]) % SUBLANES
        if pad:
            q = jnp.concatenate([q, jnp.zeros((pad, LANES), q.dtype)], axis=0)
        out.append(q)
    return jnp.concatenate(out, axis=0)


def _rows_of(shape):
    n = 1
    for s in shape:
        n *= s
    rows = n // LANES
    return rows + (-rows) % SUBLANES


def kernel(x, c, w_ada, b_ada, norm_mix_g, w_in, rnn_conv_w, rnn_conv_b, lru_w_a, lru_b_a, lru_w_x, lru_b_x, lru_lambda, sgu_ln_g, sgu_ln_b, sgu_w_s, sgu_b_s, w_branch_a, w_branch_b, w_out, norm_ffn_g, w_up, ffn_conv_w, ffn_conv_b, w_down, norm_final_g, loss_target, m_w_ada, m_b_ada, m_norm_mix_g, m_w_in, m_rnn_conv_w, m_rnn_conv_b, m_lru_w_a, m_lru_b_a, m_lru_w_x, m_lru_b_x, m_lru_lambda, m_sgu_ln_g, m_sgu_ln_b, m_sgu_w_s, m_sgu_b_s, m_w_branch_a, m_w_branch_b, m_w_out, m_norm_ffn_g, m_w_up, m_ffn_conv_w, m_ffn_conv_b, m_w_down, m_norm_final_g, v_w_ada, v_b_ada, v_norm_mix_g, v_w_in, v_rnn_conv_w, v_rnn_conv_b, v_lru_w_a, v_lru_b_a, v_lru_w_x, v_lru_b_x, v_lru_lambda, v_sgu_ln_g, v_sgu_ln_b, v_sgu_w_s, v_sgu_b_s, v_w_branch_a, v_w_branch_b, v_w_out, v_norm_ffn_g, v_w_up, v_ffn_conv_w, v_ffn_conv_b, v_w_down, v_norm_final_g):
    args = dict(locals())
    T = x.shape[1]
    mx, my, mc = lax.axis_index("x"), lax.axis_index("y"), lax.axis_index("c")
    chip = 2 * mx + my
    dev = 2 * chip + mc
    vec = lambda a: a.reshape(1, -1)

    xt = x.reshape(T, D)
    tgt = loss_target.reshape(T, D)
    ns = w_in.shape[2]
    dff = w_down.shape[1] * N_CHIPS

    c_all = _gather8(c.reshape(SUBLANES, LANES), "gather_c").reshape(N_DEV, D)
    b_ada_sh = lax.dynamic_slice(b_ada, (0, chip * ns), (1, ns))
    mod_sh = _mod_fwd(c_all, w_ada[0], b_ada_sh)

    w_in_b, w_up_b, w_down_b, wba_b, wbb_b, wo_b = _cast_shards(
        [w_in[0], w_up[0], w_down[0], w_branch_a[0], w_branch_b[0], w_out[0]])
    (w_in4, w_up4, w_down4, wba4, wbb4, wo4, rcw4, fcw4, mod4) = _exchange_xy(
        [w_in_b, w_up_b, w_down_b, wba_b, wbb_b, wo_b, rnn_conv_w[0], ffn_conv_w[0], mod_sh], False, "gather_weights")
    wd_full = w_down4.reshape(dff, D)
    wba_full = wba4.reshape(D, D)
    wbb_full = wbb4.reshape(D, D)
    wo_full = wo4.reshape(D, D)
    rcw_full = jnp.transpose(rcw4, (1, 0, 2)).reshape(4, D)
    fcw_full = jnp.transpose(fcw4, (1, 0, 2)).reshape(3, 2 * dff)
    mod = lax.dynamic_index_in_dim(mod4, dev, axis=1, keepdims=False).reshape(1, 6 * D)
    shift1, scale1, gate1, shift2, scale2, gate2 = [mod[:, k * D:(k + 1) * D] for k in range(6)]

    h1, z = _modnorm_matmul(xt, norm_mix_g, scale1, shift1, w_in4, "norm_in_proj")
    bst = jnp.transpose(sgu_b_s[0])
    h_lru, ya_pre = _rglru_fwd(z, rcw_full, rnn_conv_b, lru_w_a[0], lru_b_a, lru_w_x[0], lru_b_x, lru_lambda)
    yb_pre = _sgu_fwd(z, sgu_ln_g, sgu_ln_b, sgu_w_s[0], bst)
    x2, merged, ya, yb, o1 = _mix_out(ya_pre, yb_pre, z, xt, gate1, wba_full, wbb_full, wo_full)
    h2, up = _modnorm_matmul(x2, norm_ffn_g, scale2, shift2, w_up4, "norm_up_proj")
    f = _ffn_gate(up, fcw_full, ffn_conv_b)
    loss_part, dx3, dfo, dgf, dgate2 = _ffn_down_loss(f, wd_full, x2, gate2, vec(norm_final_g), tgt)

    ffn_blk = 768
    ffn_map = functools.partial(_ffn_col_block, ncb=dff // ffn_blk)
    dup, dwd, dfcw_a, dfcw_v, dfcb_a, dfcb_v = _ffn_bwd(dfo, wd_full, up, fcw_full, ffn_conv_b, cw_blk=ffn_blk)
    dw_up4 = _mm_tn_cols(h2, dup, "dw_up", N_CHIPS, ffn_blk, colmap=ffn_map)
    dx2, dshift2, dscale2, dg_ffn, do1, dgate1 = _mm_nt_normbwd(
        dup, w_up4, x2, dx3, norm_ffn_g, scale2, "dh2_norm_bwd", gate=gate1, o=o1, colmap=ffn_map, tk=ffn_blk)
    dz, dya, dyb, dya_pre, dyb_pre = _mix_bwd(do1, ya, yb, z, wo_full, wba_full, wbb_full)
    dwo = _mm_tn_cols(merged, do1, "dw_out", 1, D)
    dwba = _mm_tn_cols(ya_pre, dya, "dw_branch_a", 1, D)
    dwbb = _mm_tn_cols(yb_pre, dyb, "dw_branch_b", 1, D)
    dz, dws, dbst, dlg, dlb = _sgu_bwd(dz, dyb_pre, z, sgu_ln_g, sgu_ln_b, sgu_w_s[0], bst)
    dz, drcw, drcb, dwa, dba, dwx, dbx, dlam = _rglru_bwd(
        dz, dya_pre, z, h_lru, rcw_full, rnn_conv_b, lru_w_a[0], lru_b_a, lru_w_x[0], lru_b_x, lru_lambda)
    dw_in4 = _mm_tn_cols(h1, dz, "dw_in", N_CHIPS, ns)
    grad_x, dshift1, dscale1, dg_mix = _mm_nt_normbwd(dz, w_in4, xt, dx2, norm_mix_g, scale1, "dh1_norm_bwd")
    dmod = jnp.concatenate([dshift1, dscale1, dgate1, dshift2, dscale2, dgate2], axis=1)

    big = [("w_in", dw_in4), ("w_up", dw_up4), ("w_down", dwd.reshape(N_CHIPS, dff // N_CHIPS, D)),
           ("w_branch_a", dwba.reshape(N_CHIPS, D // N_CHIPS, D)), ("w_branch_b", dwbb.reshape(N_CHIPS, D // N_CHIPS, D)),
           ("w_out", dwo.reshape(N_CHIPS, D // N_CHIPS, D))]
    parts = _exchange_xy([g for _, g in big], True, "scatter_grads")
    sums = [_sum_parts(p, "sum_chips_" + n) for (n, _), p in zip(big, parts)]
    others = _exchange_core(sums, "swap_core_sums")
    out = {}
    for (n, _), mine, other in zip(big, sums, others):
        shape = args[n].shape
        res = _adamw(args[n][0], args["m_" + n][0], args["v_" + n][0], [mine, other], "adamw_" + n)
        for kind, r in zip(("grad_", "delta_", "new_m_", "new_v_"), res):
            out[kind + n] = r.reshape(shape)

    small = [("b_ada", dmod), ("norm_mix_g", dg_mix), ("rnn_conv_b", drcb), ("lru_w_a", dwa), ("lru_b_a", dba),
             ("lru_w_x", dwx), ("lru_b_x", dbx), ("lru_lambda", dlam), ("sgu_ln_g", dlg), ("sgu_ln_b", dlb),
             ("sgu_w_s", dws), ("sgu_b_s", jnp.transpose(dbst)), ("norm_ffn_g", dg_ffn),
             ("ffn_conv_b", jnp.concatenate([dfcb_a, dfcb_v], axis=1)), ("norm_final_g", dgf)]
    dfcw = jnp.concatenate([dfcw_a, dfcw_v], axis=1)
    r_small = sum(_rows_of(args[n].shape) for n, _ in small)
    r_pad = r_small + (-r_small) % 256
    fill = jnp.zeros((r_pad - r_small, LANES), F32)
    g_pack = jnp.concatenate([_pack_rows([g for _, g in small]), fill, _pack_rows([drcw, dfcw])], axis=0)
    g_all = _gather8(g_pack, "gather_small_grads")
    g_sum = _sum_parts(g_all, "sum_small_grads")

    def pack_small(prefix):
        return jnp.concatenate([_pack_rows([args[prefix + n] for n, _ in small]), fill], axis=0)

    res = _adamw(pack_small(""), pack_small("m_"), pack_small("v_"), [g_sum[:r_pad]], "adamw_small")
    off = 0
    for n, _ in small:
        shape = args[n].shape
        rows = _rows_of(shape)
        for kind, r in zip(("grad_", "delta_", "new_m_", "new_v_"), res):
            out[kind + n] = r[off:off + rows].reshape(shape)
        off += rows

    rcw_cols = rnn_conv_w.shape[2]
    g_rcw = lax.dynamic_slice(g_sum[r_pad:r_pad + 32].reshape(4, D), (0, chip * rcw_cols), (4, rcw_cols))
    g_fcw = lax.dynamic_slice(g_sum[r_pad + 32:r_pad + 32 + 144].reshape(3, 2 * dff), (0, chip * ns), (3, ns))
    conv = [("rnn_conv_w", g_rcw), ("ffn_conv_w", g_fcw)]
    res = _adamw(_pack_rows([args[n] for n, _ in conv]), _pack_rows([args["m_" + n] for n, _ in conv]),
                 _pack_rows([args["v_" + n] for n, _ in conv]), [_pack_rows([g for _, g in conv])], "adamw_conv")
    off = 0
    for n, _ in conv:
        shape = args[n].shape
        cnt = shape[1] * shape[2] // LANES
        for kind, r in zip(("grad_", "delta_", "new_m_", "new_v_"), res):
            out[kind + n] = r[off:off + cnt].reshape(shape)
        off += _rows_of(shape)

    dmod_all = g_all[:, 0:6 * D // LANES, :].reshape(N_DEV, 6 * D)
    dmod_sh = lax.dynamic_slice(dmod_all, (0, chip * ns), (N_DEV, ns))
    res = _ada_adamw(jnp.transpose(c_all), dmod_sh, w_ada[0], m_w_ada[0], v_w_ada[0])
    for kind, r in zip(("grad_", "delta_", "new_m_", "new_v_"), res):
        out[kind + "w_ada"] = r.reshape(w_ada.shape)

    loss = lax.psum(loss_part[0, 0], ("x", "y", "c"))
    names = ["w_ada", "b_ada", "norm_mix_g", "w_in", "rnn_conv_w", "rnn_conv_b", "lru_w_a", "lru_b_a", "lru_w_x",
             "lru_b_x", "lru_lambda", "sgu_ln_g", "sgu_ln_b", "sgu_w_s", "sgu_b_s", "w_branch_a", "w_branch_b",
             "w_out", "norm_ffn_g", "w_up", "ffn_conv_w", "ffn_conv_b", "w_down", "norm_final_g"]
    result = [loss, grad_x.reshape(x.shape)]
    for kind in ("grad_", "delta_", "new_m_", "new_v_"):
        result += [out[kind + n] for n in names]
    return tuple(result)
```

```python
import functools

import jax
import jax.numpy as jnp
from jax import lax
from jax.experimental import pallas as pl
from jax.experimental.pallas import tpu as pltpu

F32 = jnp.float32
BF = jnp.bfloat16

D = 1024
HEADS = 8
HD = D // HEADS
SGU_BLOCK = 128
N_CHIPS = 4
N_DEV = 8
EPS = 1e-6
LRU_C = 8.0
LANES = 128
SUBLANES = 8

ADAM_LR = 0.001
ADAM_B1 = 0.9
ADAM_B2 = 0.999
ADAM_EPS = 1e-08
ADAM_WD = 0.01
ADAM_STEP = 10

GELU_K0 = 0.7978845608028654
GELU_K1 = 0.044715

HBM_SPEC = pl.BlockSpec(memory_space=pltpu.HBM)
MESH_ID = pl.DeviceIdType.MESH


def _pcall(body, *, name, out_shape, grid=(), in_specs=None, out_specs=None, scratch=(), vmem_mb=32, aliases=None,
           grid_spec=None):
    kw = {}
    if aliases:
        kw["input_output_aliases"] = aliases
    if grid_spec is not None:
        kw["grid_spec"] = grid_spec
        ndim = len(grid_spec.grid)
    else:
        kw.update(grid=grid, in_specs=in_specs, out_specs=out_specs, scratch_shapes=list(scratch))
        ndim = len(grid)
    if ndim:
        params = pltpu.CompilerParams(dimension_semantics=("arbitrary",) * ndim, vmem_limit_bytes=vmem_mb * 2 ** 20)
    else:
        params = pltpu.CompilerParams(vmem_limit_bytes=vmem_mb * 2 ** 20)
    return pl.pallas_call(body, name=name, out_shape=out_shape, compiler_params=params, **kw)


def _gelu(x):
    return 0.5 * x * (1.0 + jnp.tanh(GELU_K0 * (x + GELU_K1 * x * x * x)))


def _gelu_and_grad(x):
    x2 = x * x
    t = jnp.tanh(GELU_K0 * x * (1.0 + GELU_K1 * x2))
    g = 0.5 * x * (1.0 + t)
    dg = 0.5 * (1.0 + t) + 0.5 * x * (1.0 - t * t) * (GELU_K0 * (1.0 + 3.0 * GELU_K1 * x2))
    return g, dg


def _sigmoid(x):
    return 1.0 / (1.0 + jnp.exp(-x))


def _log_sigmoid(x):
    e = jnp.exp(-jnp.abs(x))
    u = 1.0 + e
    d = u - 1.0
    l1p = jnp.where(d == 0.0, e, jnp.log(u) * (e / jnp.where(d == 0.0, 1.0, d)))
    return jnp.minimum(x, 0.0) - l1p


def _dot(a, b):
    return jnp.dot(a, b, preferred_element_type=F32)


def _dot_nt(a, b):
    return lax.dot_general(a, b, (((1,), (1,)), ((), ())), preferred_element_type=F32)


def _dot_tn(a, b):
    return lax.dot_general(a, b, (((0,), (0,)), ((), ())), preferred_element_type=F32)


def _shift_down(x, halo, s):
    r = pltpu.roll(x, s, 0)
    rows = lax.broadcasted_iota(jnp.int32, (SUBLANES, x.shape[1]), 0)
    head = jnp.where(rows < s, pltpu.roll(halo, s, 0), r[0:SUBLANES])
    return jnp.concatenate([head, r[SUBLANES:]], axis=0)


def _shift_up(x, halo, s):
    n = x.shape[0]
    r = pltpu.roll(x, n - s, 0)
    rows = lax.broadcasted_iota(jnp.int32, (SUBLANES, x.shape[1]), 0)
    tail = jnp.where(rows >= SUBLANES - s, pltpu.roll(halo, SUBLANES - s, 0), r[n - SUBLANES:n])
    return jnp.concatenate([r[:n - SUBLANES], tail], axis=0)


def _colsum(x):
    return jnp.sum(x, axis=0, keepdims=True)


def _rms_stats(x):
    r = lax.rsqrt(jnp.mean(x * x, axis=-1, keepdims=True) + EPS)
    return r, x * r


def _lru_gates(xc, wa_ref, ba, wx_ref, bx, lam):
    pr, pi = [], []
    for hh in range(HEADS):
        xh = xc[:, hh * HD:(hh + 1) * HD].astype(BF)
        pr.append(_dot(xh, wa_ref[hh].astype(BF)))
        pi.append(_dot(xh, wx_ref[hh].astype(BF)))
    r = _sigmoid(jnp.concatenate(pr, axis=1) + ba)
    ig = _sigmoid(jnp.concatenate(pi, axis=1) + bx)
    ls = _log_sigmoid(lam)
    log_a = LRU_C * r * ls
    a = jnp.exp(log_a)
    x2 = 2.0 * log_a
    u = a * a
    lu = jnp.log(jnp.maximum(u, 1e-37))
    em1 = jnp.where(lu == 0.0, x2, jnp.where(u < 1e-30, -1.0, (u - 1.0) * x2 / jnp.where(lu == 0.0, 1.0, lu)))
    mult = jnp.sqrt(-em1)
    return r, ig, ls, a, mult


def _sgu_mix(vln, ws_ref, bst_ref, tb):
    ri = lax.broadcasted_iota(jnp.int32, (SGU_BLOCK, SGU_BLOCK), 0)
    ci = lax.broadcasted_iota(jnp.int32, (SGU_BLOCK, SGU_BLOCK), 1)
    wm = [jnp.where(ri >= ci, ws_ref[g], 0.0).astype(BF) for g in range(HEADS)]
    blocks = []
    for blk in range(tb // SGU_BLOCK):
        cols = []
        for g in range(HEADS):
            vb = vln[blk * SGU_BLOCK:(blk + 1) * SGU_BLOCK, g * HD:(g + 1) * HD].astype(BF)
            cols.append(_dot(wm[g], vb) + bst_ref[:, g:g + 1])
        blocks.append(jnp.concatenate(cols, axis=1))
    mixed = blocks[0] if len(blocks) == 1 else jnp.concatenate(blocks, axis=0)
    return wm, mixed


def _layernorm_stats(v):
    mu = jnp.mean(v, axis=-1, keepdims=True)
    vc = v - mu
    rstd = lax.rsqrt(jnp.mean(vc * vc, axis=-1, keepdims=True) + EPS)
    return rstd, vc * rstd


def _my_xyc():
    return lax.axis_index("x"), lax.axis_index("y"), lax.axis_index("c")


def _exchange_xy(srcs, scatter, name):
    n = len(srcs)
    out_shape = []
    for s in srcs:
        shard = s.shape[1:] if scatter else s.shape
        out_shape.append(jax.ShapeDtypeStruct((N_CHIPS,) + tuple(shard), s.dtype))

    def body(*refs):
        src, out = refs[:n], refs[n:2 * n]
        send_sems, recv_sems, loc_sems = refs[2 * n:]
        x, y, c = _my_xyc()
        me = 2 * x + y
        remote, local = [], []
        for a in range(n):
            def piece(p, a=a):
                return src[a].at[p] if scatter else src[a]
            lc = pltpu.make_async_copy(piece(me), out[a].at[me], loc_sems.at[a])
            lc.start()
            local.append(lc)
            for k, (fx, fy) in enumerate(((1, 0), (0, 1), (1, 1))):
                px = 1 - x if fx else x
                py = 1 - y if fy else y
                cp = pltpu.make_async_remote_copy(
                    src_ref=piece(2 * px + py), dst_ref=out[a].at[me],
                    send_sem=send_sems.at[a, k], recv_sem=recv_sems.at[a, k],
                    device_id=(px, py, c), device_id_type=MESH_ID)
                cp.start()
                remote.append(cp)
        for cp in remote:
            cp.wait()
        for lc in local:
            lc.wait()

    return _pcall(body, name=name, out_shape=out_shape, in_specs=[HBM_SPEC] * n, out_specs=[HBM_SPEC] * n,
                  scratch=[pltpu.SemaphoreType.DMA((n, 3)), pltpu.SemaphoreType.DMA((n, 3)),
                           pltpu.SemaphoreType.DMA((n,))])(*srcs)


def _gather_weights(srcs, halve):
    n = len(srcs)
    out_shape = [jax.ShapeDtypeStruct((N_CHIPS,) + s.shape, s.dtype) for s in srcs]

    def body(*refs):
        src, out = refs[:n], refs[n:2 * n]
        send_sems, recv_sems, fwd_send, fwd_recv, loc_sems = refs[2 * n:]
        x, y, c = _my_xyc()
        me = 2 * x + y
        chips = [(1 - x, y), (x, 1 - y), (1 - x, 1 - y)]

        def half(ref, a, which):
            if not halve[a]:
                return ref
            h = srcs[a].shape[0] // 2
            return ref.at[pl.ds(which * h, h)]

        def ici(a, k, frm):
            px, py = chips[k]
            return pltpu.make_async_remote_copy(
                src_ref=half(src[a], a, c), dst_ref=half(out[a].at[frm], a, c),
                send_sem=send_sems.at[a, k], recv_sem=recv_sems.at[a, k],
                device_id=(px, py, c), device_id_type=MESH_ID)

        def d2d(a, k, which):
            px, py = chips[k]
            rows = half(out[a].at[2 * px + py], a, which)
            return pltpu.make_async_remote_copy(
                src_ref=rows, dst_ref=rows, send_sem=fwd_send.at[a, k], recv_sem=fwd_recv.at[a, k],
                device_id=(x, y, 1 - c), device_id_type=MESH_ID)

        local, sends = [], []
        for a in range(n):
            lc = pltpu.make_async_copy(src[a], out[a].at[me], loc_sems.at[a])
            lc.start()
            local.append(lc)
            for k in range(3):
                cp = ici(a, k, me)
                cp.start()
                sends.append(cp)
        for a in range(n):
            for k in range(3):
                px, py = chips[k]
                ici(a, k, 2 * px + py).wait_recv()
                if halve[a]:
                    fw = d2d(a, k, c)
                    fw.start()
                    sends.append(fw)
        for a in range(n):
            if halve[a]:
                for k in range(3):
                    d2d(a, k, 1 - c).wait_recv()
        for cp in sends:
            cp.wait_send()
        for lc in local:
            lc.wait()

    sem = pltpu.SemaphoreType.DMA((n, 3))
    return _pcall(body, name="gather_weights", out_shape=out_shape, in_specs=[HBM_SPEC] * n,
                  out_specs=[HBM_SPEC] * n, scratch=[sem, sem, sem, sem, pltpu.SemaphoreType.DMA((n,))])(*srcs)


def _send_other_half(grads):
    n = len(grads)
    out_shape = [jax.ShapeDtypeStruct((N_CHIPS, g.shape[1] // 2, g.shape[2]), g.dtype) for g in grads]

    def body(*refs):
        src, out = refs[:n], refs[n:2 * n]
        send_sems, recv_sems = refs[2 * n:]
        x, y, c = _my_xyc()
        cps = []
        for a in range(n):
            h = grads[a].shape[1] // 2
            cp = pltpu.make_async_remote_copy(
                src_ref=src[a].at[:, pl.ds((1 - c) * h, h), :], dst_ref=out[a],
                send_sem=send_sems.at[a], recv_sem=recv_sems.at[a],
                device_id=(x, y, 1 - c), device_id_type=MESH_ID)
            cp.start()
            cps.append(cp)
        for cp in cps:
            cp.wait()

    return _pcall(body, name="send_other_half", out_shape=out_shape, in_specs=[HBM_SPEC] * n,
                  out_specs=[HBM_SPEC] * n,
                  scratch=[pltpu.SemaphoreType.DMA((n,)), pltpu.SemaphoreType.DMA((n,))])(*grads)


def _share_halves(totals):
    n = len(totals)
    out_shape = [jax.ShapeDtypeStruct((2,) + t.shape, t.dtype) for t in totals]

    def body(*refs):
        src, out = refs[:n], refs[n:2 * n]
        send_sems, recv_sems, loc_sems = refs[2 * n:]
        x, y, c = _my_xyc()
        cps, local = [], []
        for a in range(n):
            lc = pltpu.make_async_copy(src[a], out[a].at[c], loc_sems.at[a])
            lc.start()
            local.append(lc)
            cp = pltpu.make_async_remote_copy(
                src_ref=src[a], dst_ref=out[a].at[c], send_sem=send_sems.at[a], recv_sem=recv_sems.at[a],
                device_id=(x, y, 1 - c), device_id_type=MESH_ID)
            cp.start()
            cps.append(cp)
        for cp in cps:
            cp.wait()
        for lc in local:
            lc.wait()

    sem = pltpu.SemaphoreType.DMA((n,))
    return _pcall(body, name="share_halves", out_shape=out_shape, in_specs=[HBM_SPEC] * n,
                  out_specs=[HBM_SPEC] * n, scratch=[sem, sem, sem])(*totals)


def _gather8(src, name):
    def body(src_ref, out_ref, send_sems, recv_sems, loc_sem):
        x, y, c = _my_xyc()
        me = 4 * x + 2 * y + c
        lc = pltpu.make_async_copy(src_ref, out_ref.at[me], loc_sem)
        lc.start()
        cps = []
        for k in range(1, N_DEV):
            px = 1 - x if (k >> 2) & 1 else x
            py = 1 - y if (k >> 1) & 1 else y
            pc = 1 - c if k & 1 else c
            cp = pltpu.make_async_remote_copy(
                src_ref=src_ref, dst_ref=out_ref.at[me], send_sem=send_sems.at[k - 1], recv_sem=recv_sems.at[k - 1],
                device_id=(px, py, pc), device_id_type=MESH_ID)
            cp.start()
            cps.append(cp)
        for cp in cps:
            cp.wait()
        lc.wait()

    return _pcall(body, name=name, out_shape=jax.ShapeDtypeStruct((N_DEV,) + src.shape, src.dtype),
                  in_specs=[HBM_SPEC], out_specs=HBM_SPEC,
                  scratch=[pltpu.SemaphoreType.DMA((N_DEV - 1,)), pltpu.SemaphoreType.DMA((N_DEV - 1,)),
                           pltpu.SemaphoreType.DMA])(src)


def _cast_shards(arrs):
    n = len(arrs)

    def body(*refs):
        for a in range(n):
            refs[n + a][...] = refs[a][...].astype(BF)

    specs = [pl.BlockSpec((s.shape[0] // 4, s.shape[1]), lambda i: (i, 0)) for s in arrs]
    return _pcall(body, name="cast_shards", grid=(4,), in_specs=specs, out_specs=specs,
                  out_shape=[jax.ShapeDtypeStruct(s.shape, BF) for s in arrs])(*arrs)


def _row_tile(rows, cols):
    t = rows
    while t * cols * 4 > (3 << 19) and t % 16 == 0:
        t //= 2
    return t


def _sum_parts(parts, name):
    p, rows, cols = parts.shape
    tr = _row_tile(rows, cols * p // 2)

    def body(p_ref, o_ref):
        acc = p_ref[0].astype(F32)
        for k in range(1, p):
            acc = acc + p_ref[k].astype(F32)
        o_ref[...] = acc

    return _pcall(body, name=name, grid=(rows // tr,),
                  in_specs=[pl.BlockSpec((p, tr, cols), lambda i: (0, i, 0))],
                  out_specs=pl.BlockSpec((tr, cols), lambda i: (i, 0)),
                  out_shape=jax.ShapeDtypeStruct((rows, cols), F32), vmem_mb=48)(parts)


def _add_halves_bf16(core, grad, other, name):
    nchip, rows, cols = grad.shape
    h = rows // 2
    tr = _row_tile(h, cols)
    nh = h // tr

    def body(c_ref, g_ref, o_ref, s_ref):
        del c_ref
        s_ref[...] = (g_ref[...] + o_ref[...]).astype(BF)

    grid_spec = pltpu.PrefetchScalarGridSpec(
        num_scalar_prefetch=1, grid=(nchip, nh),
        in_specs=[pl.BlockSpec((1, tr, cols), lambda p, i, c_ref: (p, c_ref[0] * nh + i, 0)),
                  pl.BlockSpec((1, tr, cols), lambda p, i, c_ref: (p, i, 0))],
        out_specs=pl.BlockSpec((1, tr, cols), lambda p, i, c_ref: (p, i, 0)))
    return _pcall(body, name=name, grid_spec=grid_spec, out_shape=jax.ShapeDtypeStruct((nchip, h, cols), BF),
                  vmem_mb=48)(core, grad, other)


def _adamw_math(w, g, m, v):
    m2 = ADAM_B1 * m + (1.0 - ADAM_B1) * g
    v2 = ADAM_B2 * v + (1.0 - ADAM_B2) * (g * g)
    m_hat = m2 / (1.0 - ADAM_B1 ** ADAM_STEP)
    v_hat = v2 / (1.0 - ADAM_B2 ** ADAM_STEP)
    delta = -ADAM_LR * (m_hat / (jnp.sqrt(v_hat) + ADAM_EPS) + ADAM_WD * w)
    return delta, m2, v2


def _adamw(w, m, v, grads, name):
    rows, cols = w.shape
    tr = _row_tile(rows, cols)
    ng = len(grads)

    def body(*refs):
        w_ref, m_ref, v_ref = refs[:3]
        g = refs[3][...]
        for k in range(1, ng):
            g = g + refs[3 + k][...]
        g_ref, d_ref, m2_ref, v2_ref = refs[3 + ng:]
        delta, m2, v2 = _adamw_math(w_ref[...], g, m_ref[...], v_ref[...])
        g_ref[...] = g
        d_ref[...] = delta
        m2_ref[...] = m2
        v2_ref[...] = v2

    spec = pl.BlockSpec((tr, cols), lambda i: (i, 0))
    return _pcall(body, name=name, grid=(rows // tr,), in_specs=[spec] * (3 + ng), out_specs=[spec] * 4,
                  out_shape=[jax.ShapeDtypeStruct((rows, cols), F32)] * 4, vmem_mb=48)(w, m, v, *grads)


def _ada_adamw(ct, dmod, w, m, v):
    rows, cols = w.shape
    tr = _row_tile(rows, cols)

    def body(ct_ref, dm_ref, w_ref, m_ref, v_ref, g_ref, d_ref, m2_ref, v2_ref):
        cv = ct_ref[...]
        ca = cv * _sigmoid(cv)
        g = ca[:, 0:1] * dm_ref[0:1, :]
        for b in range(1, N_DEV):
            g = g + ca[:, b:b + 1] * dm_ref[b:b + 1, :]
        delta, m2, v2 = _adamw_math(w_ref[...], g, m_ref[...], v_ref[...])
        g_ref[...] = g
        d_ref[...] = delta
        m2_ref[...] = m2
        v2_ref[...] = v2

    spec = pl.BlockSpec((tr, cols), lambda i: (i, 0))
    return _pcall(body, name="ada_adamw", grid=(rows // tr,),
                  in_specs=[pl.BlockSpec((tr, N_DEV), lambda i: (i, 0)), pl.BlockSpec((N_DEV, cols), lambda i: (0, 0)),
                            spec, spec, spec],
                  out_specs=[spec] * 4, out_shape=[jax.ShapeDtypeStruct((rows, cols), F32)] * 4,
                  vmem_mb=48)(ct, dmod, w, m, v)


def _mod_fwd(c_all, w, b):
    cols = w.shape[1]
    tn = cols // 3

    def body(c_ref, w_ref, b_ref, o_ref):
        cv = c_ref[...]
        ca = (cv * _sigmoid(cv)).astype(BF)
        o_ref[...] = _dot(ca, w_ref[...].astype(BF)) + b_ref[...]

    return _pcall(body, name="mod_fwd", grid=(3,),
                  in_specs=[pl.BlockSpec((N_DEV, D), lambda j: (0, 0)), pl.BlockSpec((D, tn), lambda j: (0, j)),
                            pl.BlockSpec((1, tn), lambda j: (0, j))],
                  out_specs=pl.BlockSpec((N_DEV, tn), lambda j: (0, j)),
                  out_shape=jax.ShapeDtypeStruct((N_DEV, cols), F32))(c_all, w, b)


def _modnorm_matmul(x, g, scale, shift, w4, name, tm=512, tn=768):
    T = x.shape[0]
    tm = min(tm, T)
    ns = w4.shape[2]
    nj = ns // tn

    def body(x_ref, g_ref, sc_ref, sh_ref, w_ref, h_ref, z_ref, hs):
        j = pl.program_id(1)

        @pl.when(j == 0)
        def _():
            xv = x_ref[...]
            _, xh = _rms_stats(xv)
            h = (xh * g_ref[...]) * (1.0 + sc_ref[...]) + sh_ref[...]
            hs[...] = h.astype(BF)
            h_ref[...] = hs[...]

        z_ref[...] = _dot(hs[...], w_ref[0])

    vec = pl.BlockSpec((1, D), lambda i, j: (0, 0))
    return _pcall(body, name=name, grid=(T // tm, N_CHIPS * nj),
                  in_specs=[pl.BlockSpec((tm, D), lambda i, j: (i, 0)), vec, vec, vec,
                            pl.BlockSpec((1, D, tn), lambda i, j: (j // nj, 0, j % nj))],
                  out_specs=[pl.BlockSpec((tm, D), lambda i, j: (i, 0)), pl.BlockSpec((tm, tn), lambda i, j: (i, j))],
                  out_shape=[jax.ShapeDtypeStruct((T, D), BF), jax.ShapeDtypeStruct((T, N_CHIPS * ns), F32)],
                  scratch=[pltpu.VMEM((tm, D), BF)])(x, g, scale, shift, w4)


def _rglru_fwd(z, cw, cb, wa, ba, wx, bx, lam, tb=256):
    T = z.shape[0]
    tb = min(tb, T)

    def body(xr_ref, gr_ref, cw_ref, cb_ref, wa_ref, ba_ref, wx_ref, bx_ref, lam_ref, h_ref, ya_ref, prev, hc):
        i = pl.program_id(0)

        @pl.when(i == 0)
        def _():
            prev[...] = jnp.zeros_like(prev)
            hc[...] = jnp.zeros_like(hc)

        xr = xr_ref[...]
        pv = prev[...]
        xc = (cb_ref[...] + cw_ref[3:4, :] * xr + cw_ref[2:3, :] * _shift_down(xr, pv, 1)
              + cw_ref[1:2, :] * _shift_down(xr, pv, 2) + cw_ref[0:1, :] * _shift_down(xr, pv, 3))
        prev[...] = xr[tb - SUBLANES:tb]
        _, ig, _, a, mult = _lru_gates(xc, wa_ref, ba_ref[...], wx_ref, bx_ref[...], lam_ref[...])
        u = mult * (ig * xc)
        rows = lax.broadcasted_iota(jnp.int32, (tb, D), 0)
        d = 1
        while d < tb:
            keep = rows >= d
            a_s = jnp.where(keep, pltpu.roll(a, d, 0), 1.0)
            u_s = jnp.where(keep, pltpu.roll(u, d, 0), 0.0)
            u = a * u_s + u
            a = a * a_s
            d *= 2
        h = u + a * hc[SUBLANES - 1:SUBLANES, :]
        hc[...] = h[tb - SUBLANES:tb]
        h_ref[...] = h
        ya_ref[...] = (h * _gelu(gr_ref[...])).astype(BF)

    vec = pl.BlockSpec((1, D), lambda i: (0, 0))
    wspec = pl.BlockSpec((HEADS, HD, HD), lambda i: (0, 0, 0))
    return _pcall(body, name="rglru_fwd", grid=(T // tb,),
                  in_specs=[pl.BlockSpec((tb, D), lambda i: (i, 0)), pl.BlockSpec((tb, D), lambda i: (i, 1)),
                            pl.BlockSpec((4, D), lambda i: (0, 0)), vec, wspec, vec, wspec, vec, vec],
                  out_specs=[pl.BlockSpec((tb, D), lambda i: (i, 0))] * 2,
                  out_shape=[jax.ShapeDtypeStruct((T, D), F32), jax.ShapeDtypeStruct((T, D), BF)],
                  scratch=[pltpu.VMEM((SUBLANES, D), F32), pltpu.VMEM((SUBLANES, D), F32)],
                  vmem_mb=48)(z, z, cw, cb, wa, ba, wx, bx, lam)


def _sgu_fwd(z, lg, lb, ws, bst, tb=256):
    T = z.shape[0]
    tb = min(tb, T)

    def body(zu_ref, zv_ref, lg_ref, lb_ref, ws_ref, bst_ref, yb_ref):
        _, xh = _layernorm_stats(_gelu(zv_ref[...]))
        vln = xh * lg_ref[...] + lb_ref[...]
        _, mixed = _sgu_mix(vln, ws_ref, bst_ref, tb)
        yb_ref[...] = (_gelu(zu_ref[...]) * mixed).astype(BF)

    vec = pl.BlockSpec((1, D), lambda i: (0, 0))
    return _pcall(body, name="sgu_fwd", grid=(T // tb,),
                  in_specs=[pl.BlockSpec((tb, D), lambda i: (i, 2)), pl.BlockSpec((tb, D), lambda i: (i, 3)), vec, vec,
                            pl.BlockSpec((HEADS, SGU_BLOCK, SGU_BLOCK), lambda i: (0, 0, 0)),
                            pl.BlockSpec((SGU_BLOCK, HEADS), lambda i: (0, 0))],
                  out_specs=pl.BlockSpec((tb, D), lambda i: (i, 0)),
                  out_shape=jax.ShapeDtypeStruct((T, D), BF))(z, z, lg, lb, ws, bst)


def _mix_out(ya_pre, yb_pre, z, x, gate1, wba, wbb, wo, tm=256):
    T = x.shape[0]
    tm = min(tm, T)

    def body(yap_ref, ybp_ref, ga_ref, gb_ref, x_ref, g1_ref, wa_ref, wb_ref, wo_ref,
             x2_ref, mg_ref, ya_ref, yb_ref, o_ref):
        ya = _dot(yap_ref[...], wa_ref[...])
        yb = _dot(ybp_ref[...], wb_ref[...])
        merged = (_sigmoid(ga_ref[...]) * ya + _sigmoid(gb_ref[...]) * yb).astype(BF)
        o = _dot(merged, wo_ref[...])
        x2_ref[...] = x_ref[...] + g1_ref[...] * o
        mg_ref[...] = merged
        ya_ref[...] = ya.astype(BF)
        yb_ref[...] = yb.astype(BF)
        o_ref[...] = o.astype(BF)

    row = pl.BlockSpec((tm, D), lambda i: (i, 0))
    wspec = pl.BlockSpec((D, D), lambda i: (0, 0))
    return _pcall(body, name="mix_out", grid=(T // tm,),
                  in_specs=[row, row, pl.BlockSpec((tm, D), lambda i: (i, 4)), pl.BlockSpec((tm, D), lambda i: (i, 5)),
                            row, pl.BlockSpec((1, D), lambda i: (0, 0)), wspec, wspec, wspec],
                  out_specs=[row] * 5,
                  out_shape=[jax.ShapeDtypeStruct((T, D), F32)] + [jax.ShapeDtypeStruct((T, D), BF)] * 4,
                  vmem_mb=48)(ya_pre, yb_pre, z, z, x, gate1, wba, wbb, wo)


def _ffn_gate(up, cw, cb, tm=512, cw_blk=768):
    T = up.shape[0]
    tm = min(tm, T)
    dff = up.shape[1] // 2
    ncb = dff // cw_blk

    def body(ua_ref, uv_ref, wa_ref, wv_ref, ba_ref, bv_ref, f_ref, pa, pv):
        i = pl.program_id(1)

        @pl.when(i == 0)
        def _():
            pa[...] = jnp.zeros_like(pa)
            pv[...] = jnp.zeros_like(pv)

        def conv(u_ref, w_ref, b_ref, prev):
            u = u_ref[...]
            p = prev[...]
            hid = (b_ref[...] + w_ref[2:3, :] * u + w_ref[1:2, :] * _shift_down(u, p, 1)
                   + w_ref[0:1, :] * _shift_down(u, p, 2))
            prev[...] = u[tm - SUBLANES:tm]
            return hid

        act = conv(ua_ref, wa_ref, ba_ref, pa)
        val = conv(uv_ref, wv_ref, bv_ref, pv)
        f_ref[...] = (_gelu(act) * val).astype(BF)

    return _pcall(body, name="ffn_gate", grid=(ncb, T // tm),
                  in_specs=[pl.BlockSpec((tm, cw_blk), lambda cbk, i: (i, cbk)),
                            pl.BlockSpec((tm, cw_blk), lambda cbk, i: (i, ncb + cbk)),
                            pl.BlockSpec((3, cw_blk), lambda cbk, i: (0, cbk)),
                            pl.BlockSpec((3, cw_blk), lambda cbk, i: (0, ncb + cbk)),
                            pl.BlockSpec((1, cw_blk), lambda cbk, i: (0, cbk)),
                            pl.BlockSpec((1, cw_blk), lambda cbk, i: (0, ncb + cbk))],
                  out_specs=pl.BlockSpec((tm, cw_blk), lambda cbk, i: (i, cbk)),
                  out_shape=jax.ShapeDtypeStruct((T, dff), BF),
                  scratch=[pltpu.VMEM((SUBLANES, cw_blk), F32)] * 2)(up, up, cw, cw, cb, cb)


def _ffn_down_loss(f, wd, x2, gate2, gf, target, tm=512):
    T = x2.shape[0]
    tm = min(tm, T)
    dff = f.shape[1]

    def body(f_ref, wd_ref, x2_ref, g2_ref, gf_ref, t_ref, loss_ref, dx3_ref, dfo_ref, dgf_ref, dg2_ref):
        i = pl.program_id(0)

        @pl.when(i == 0)
        def _():
            loss_ref[...] = jnp.zeros_like(loss_ref)
            dgf_ref[...] = jnp.zeros_like(dgf_ref)
            dg2_ref[...] = jnp.zeros_like(dg2_ref)

        fo = _dot(f_ref[...], wd_ref[...])
        x3 = x2_ref[...] + g2_ref[...] * fo
        rstd, xh = _rms_stats(x3)
        err = xh * gf_ref[...] - t_ref[...]
        loss_ref[...] += 0.5 * jnp.sum(jnp.mean(err * err, axis=-1, keepdims=True), axis=0, keepdims=True)
        dy = err * (1.0 / D)
        dgf_ref[...] += _colsum(dy * xh)
        dxh = dy * gf_ref[...]
        dx3 = rstd * (dxh - xh * jnp.mean(dxh * xh, axis=-1, keepdims=True))
        dg2_ref[...] += _colsum(dx3 * fo)
        dx3_ref[...] = dx3
        dfo_ref[...] = (g2_ref[...] * dx3).astype(BF)

    row = pl.BlockSpec((tm, D), lambda i: (i, 0))
    vec = pl.BlockSpec((1, D), lambda i: (0, 0))
    return _pcall(body, name="ffn_down_loss", grid=(T // tm,),
                  in_specs=[pl.BlockSpec((tm, dff), lambda i: (i, 0)), pl.BlockSpec((dff, D), lambda i: (0, 0)),
                            row, vec, vec, row],
                  out_specs=[pl.BlockSpec((1, LANES), lambda i: (0, 0)), row, row, vec, vec],
                  out_shape=[jax.ShapeDtypeStruct((1, LANES), F32), jax.ShapeDtypeStruct((T, D), F32),
                             jax.ShapeDtypeStruct((T, D), BF), jax.ShapeDtypeStruct((1, D), F32),
                             jax.ShapeDtypeStruct((1, D), F32)],
                  vmem_mb=48)(f, wd, x2, gate2, gf, target)


def _halo_spec(tm, cols, col_blk, nrow):
    per = tm // SUBLANES
    return pl.BlockSpec((SUBLANES, cols), lambda cbk, i: (jnp.maximum((nrow - 1 - i) * per - 1, 0), col_blk(cbk)))


def _ffn_bwd(dfo, wd, up, cw, cb, tm=512, cw_blk=768):
    T = up.shape[0]
    tm = min(tm, T)
    dff = up.shape[1] // 2
    ncb = dff // cw_blk
    nrow = T // tm

    def body(dfo_ref, wd_ref, ua_ref, uv_ref, ha_ref, hv_ref, wa_ref, wv_ref, ba_ref, bv_ref,
             du_ref, dwd_ref, dwa_ref, dwv_ref, dba_ref, dbv_ref, na, nv):
        i = pl.program_id(1)
        first_block = i == nrow - 1

        @pl.when(i == 0)
        def _():
            na[...] = jnp.zeros_like(na)
            nv[...] = jnp.zeros_like(nv)
            dwd_ref[...] = jnp.zeros_like(dwd_ref)
            dwa_ref[...] = jnp.zeros_like(dwa_ref)
            dwv_ref[...] = jnp.zeros_like(dwv_ref)
            dba_ref[...] = jnp.zeros_like(dba_ref)
            dbv_ref[...] = jnp.zeros_like(dbv_ref)

        def conv(u_ref, halo_ref, w_ref, b_ref):
            u = u_ref[...]
            p = jnp.where(first_block, 0.0, halo_ref[...])
            s1 = _shift_down(u, p, 1)
            s2 = _shift_down(u, p, 2)
            hid = b_ref[...] + w_ref[2:3, :] * u + w_ref[1:2, :] * s1 + w_ref[0:1, :] * s2
            return u, s1, s2, hid

        ua, ua1, ua2, act = conv(ua_ref, ha_ref, wa_ref, ba_ref)
        uv, uv1, uv2, val = conv(uv_ref, hv_ref, wv_ref, bv_ref)
        ga, dga = _gelu_and_grad(act)
        dfo_t = dfo_ref[...]
        dwd_ref[...] += _dot_tn((ga * val).astype(BF), dfo_t)
        df = _dot_nt(dfo_t, wd_ref[...])
        dact = df * val * dga
        dval = df * ga

        def conv_bwd(dh, u, u1, u2, w_ref, nxt, col, dw_ref, db_ref):
            n8 = nxt[...]
            du = w_ref[2:3, :] * dh + w_ref[1:2, :] * _shift_up(dh, n8, 1) + w_ref[0:1, :] * _shift_up(dh, n8, 2)
            nxt[...] = dh[0:SUBLANES]
            du_ref[:, col:col + cw_blk] = du.astype(BF)
            dw_ref[2:3, :] += _colsum(dh * u)
            dw_ref[1:2, :] += _colsum(dh * u1)
            dw_ref[0:1, :] += _colsum(dh * u2)
            db_ref[...] += _colsum(dh)

        conv_bwd(dact, ua, ua1, ua2, wa_ref, na, 0, dwa_ref, dba_ref)
        conv_bwd(dval, uv, uv1, uv2, wv_ref, nv, cw_blk, dwv_ref, dbv_ref)

    rev = lambda cbk, i: (nrow - 1 - i, cbk)
    rev_v = lambda cbk, i: (nrow - 1 - i, ncb + cbk)
    w3a = pl.BlockSpec((3, cw_blk), lambda cbk, i: (0, cbk))
    w3v = pl.BlockSpec((3, cw_blk), lambda cbk, i: (0, ncb + cbk))
    b1a = pl.BlockSpec((1, cw_blk), lambda cbk, i: (0, cbk))
    b1v = pl.BlockSpec((1, cw_blk), lambda cbk, i: (0, ncb + cbk))
    return _pcall(body, name="ffn_bwd", grid=(ncb, nrow),
                  in_specs=[pl.BlockSpec((tm, D), lambda cbk, i: (nrow - 1 - i, 0)),
                            pl.BlockSpec((cw_blk, D), lambda cbk, i: (cbk, 0)),
                            pl.BlockSpec((tm, cw_blk), rev), pl.BlockSpec((tm, cw_blk), rev_v),
                            _halo_spec(tm, cw_blk, lambda cbk: cbk, nrow),
                            _halo_spec(tm, cw_blk, lambda cbk: ncb + cbk, nrow),
                            w3a, w3v, b1a, b1v],
                  out_specs=[pl.BlockSpec((tm, 2 * cw_blk), rev),
                             pl.BlockSpec((cw_blk, D), lambda cbk, i: (cbk, 0)), w3a, w3a, b1a, b1a],
                  out_shape=[jax.ShapeDtypeStruct((T, 2 * dff), BF),
                             jax.ShapeDtypeStruct((dff, D), F32),
                             jax.ShapeDtypeStruct((3, dff), F32), jax.ShapeDtypeStruct((3, dff), F32),
                             jax.ShapeDtypeStruct((1, dff), F32), jax.ShapeDtypeStruct((1, dff), F32)],
                  scratch=[pltpu.VMEM((SUBLANES, cw_blk), F32)] * 2,
                  vmem_mb=48)(dfo, wd, up, up, up, up, cw, cw, cb, cb)


def _ffn_col_block(t, ncb):
    return jnp.where(t < ncb, 2 * t, 2 * (t - ncb) + 1)


def _mm_tn_cols(a, b, name, nshard, nb, colmap=None, tm=512):
    T, M = a.shape
    tm = min(tm, T)
    ns = b.shape[1] // nshard
    per = ns // nb
    cmap = colmap if colmap is not None else (lambda t: t)

    def body(a_ref, b_ref, o_ref):
        k = pl.program_id(1)

        @pl.when(k == 0)
        def _():
            o_ref[...] = jnp.zeros_like(o_ref)

        o_ref[0] += _dot_tn(a_ref[...], b_ref[...])

    return _pcall(body, name=name, grid=(nshard * per, T // tm),
                  in_specs=[pl.BlockSpec((tm, M), lambda t, k: (k, 0)),
                            pl.BlockSpec((tm, nb), lambda t, k: (k, cmap(t)))],
                  out_specs=pl.BlockSpec((1, M, nb), lambda t, k: (t // per, 0, t % per)),
                  out_shape=jax.ShapeDtypeStruct((nshard, M, ns), F32), vmem_mb=48)(a, b)


def _mm_nt_normbwd(dz, w4, x, resid, g, scale, name, gate=None, o=None, colmap=None, tm=512, tk=768):
    T = x.shape[0]
    tm = min(tm, T)
    ns = w4.shape[2]
    nj = ns // tk
    nk = N_CHIPS * nj
    gated = gate is not None
    cmap = colmap if colmap is not None else (lambda t: t)

    def body(*refs):
        if gated:
            (dz_ref, w_ref, x_ref, r_ref, g_ref, sc_ref, gt_ref, o_ref,
             dx_ref, dsh_ref, dsc_ref, dg_ref, do_ref, dgt_ref, acc) = refs
        else:
            dz_ref, w_ref, x_ref, r_ref, g_ref, sc_ref, dx_ref, dsh_ref, dsc_ref, dg_ref, acc = refs
        i = pl.program_id(0)
        k = pl.program_id(1)

        @pl.when((i == 0) & (k == 0))
        def _():
            dsh_ref[...] = jnp.zeros_like(dsh_ref)
            dsc_ref[...] = jnp.zeros_like(dsc_ref)
            dg_ref[...] = jnp.zeros_like(dg_ref)
            if gated:
                dgt_ref[...] = jnp.zeros_like(dgt_ref)

        @pl.when(k == 0)
        def _():
            acc[...] = jnp.zeros_like(acc)

        acc[...] += _dot_nt(dz_ref[...], w_ref[0])

        @pl.when(k == nk - 1)
        def _():
            dh = acc[...]
            rstd, xh = _rms_stats(x_ref[...])
            dsh_ref[...] += _colsum(dh)
            dsc_ref[...] += _colsum(dh * (xh * g_ref[...]))
            dn = dh * (1.0 + sc_ref[...])
            dg_ref[...] += _colsum(dn * xh)
            dxh = dn * g_ref[...]
            dx = r_ref[...] + rstd * (dxh - xh * jnp.mean(dxh * xh, axis=-1, keepdims=True))
            dx_ref[...] = dx
            if gated:
                do_ref[...] = (gt_ref[...] * dx).astype(BF)
                dgt_ref[...] += _colsum(dx * o_ref[...].astype(F32))

    row = pl.BlockSpec((tm, D), lambda i, k: (i, 0))
    vec = pl.BlockSpec((1, D), lambda i, k: (0, 0))
    in_specs = [pl.BlockSpec((tm, tk), lambda i, k: (i, cmap(k))),
                pl.BlockSpec((1, D, tk), lambda i, k: (k // nj, 0, k % nj)), row, row, vec, vec]
    out_specs = [row, vec, vec, vec]
    out_shape = [jax.ShapeDtypeStruct((T, D), F32)] + [jax.ShapeDtypeStruct((1, D), F32)] * 3
    args = [dz, w4, x, resid, g, scale]
    if gated:
        in_specs += [vec, row]
        out_specs += [row, vec]
        out_shape += [jax.ShapeDtypeStruct((T, D), BF), jax.ShapeDtypeStruct((1, D), F32)]
        args += [gate, o]
    return _pcall(body, name=name, grid=(T // tm, nk), in_specs=in_specs, out_specs=out_specs, out_shape=out_shape,
                  scratch=[pltpu.VMEM((tm, D), F32)], vmem_mb=48)(*args)


def _mix_bwd(do, ya, yb, z, wo, wba, wbb, tm=256):
    T = do.shape[0]
    tm = min(tm, T)

    def body(do_ref, ya_ref, yb_ref, ga_ref, gb_ref, wo_ref, wa_ref, wb_ref,
             dz_ref, dya_ref, dyb_ref, dyap_ref, dybp_ref):
        dm = _dot_nt(do_ref[...], wo_ref[...])
        sa = _sigmoid(ga_ref[...])
        sb = _sigmoid(gb_ref[...])
        dya = (sa * dm).astype(BF)
        dyb = (sb * dm).astype(BF)
        dz_ref[:, 0:D] = (dm * ya_ref[...].astype(F32) * sa * (1.0 - sa)).astype(BF)
        dz_ref[:, D:2 * D] = (dm * yb_ref[...].astype(F32) * sb * (1.0 - sb)).astype(BF)
        dya_ref[...] = dya
        dyb_ref[...] = dyb
        dyap_ref[...] = _dot_nt(dya, wa_ref[...]).astype(BF)
        dybp_ref[...] = _dot_nt(dyb, wb_ref[...]).astype(BF)

    row = pl.BlockSpec((tm, D), lambda i: (i, 0))
    wspec = pl.BlockSpec((D, D), lambda i: (0, 0))
    return _pcall(body, name="mix_bwd", grid=(T // tm,),
                  in_specs=[row, row, row, pl.BlockSpec((tm, D), lambda i: (i, 4)),
                            pl.BlockSpec((tm, D), lambda i: (i, 5)), wspec, wspec, wspec],
                  out_specs=[pl.BlockSpec((tm, 2 * D), lambda i: (i, 2)), row, row, row, row],
                  out_shape=[jax.ShapeDtypeStruct((T, 6 * D), BF)] + [jax.ShapeDtypeStruct((T, D), BF)] * 4,
                  vmem_mb=48)(do, ya, yb, z, z, wo, wba, wbb)


def _sgu_bwd(dz, dyb_pre, z, lg, lb, ws, bst, tb=256):
    T = z.shape[0]
    tb = min(tb, T)

    def body(dz_in, dy_ref, zu_ref, zv_ref, lg_ref, lb_ref, ws_ref, bst_ref,
             dz_ref, dws_ref, dbst_ref, dlg_ref, dlb_ref):
        del dz_in
        i = pl.program_id(0)

        @pl.when(i == 0)
        def _():
            dws_ref[...] = jnp.zeros_like(dws_ref)
            dbst_ref[...] = jnp.zeros_like(dbst_ref)
            dlg_ref[...] = jnp.zeros_like(dlg_ref)
            dlb_ref[...] = jnp.zeros_like(dlb_ref)

        gu, dgu = _gelu_and_grad(zu_ref[...])
        gv, dgv = _gelu_and_grad(zv_ref[...])
        rstd, xh = _layernorm_stats(gv)
        vln = xh * lg_ref[...] + lb_ref[...]
        wm, mixed = _sgu_mix(vln, ws_ref, bst_ref, tb)
        dy = dy_ref[...].astype(F32)
        dz_ref[:, 0:D] = (dy * mixed * dgu).astype(BF)
        dmixed = dy * gu
        ri = lax.broadcasted_iota(jnp.int32, (SGU_BLOCK, SGU_BLOCK), 0)
        ci = lax.broadcasted_iota(jnp.int32, (SGU_BLOCK, SGU_BLOCK), 1)
        blocks = []
        for blk in range(tb // SGU_BLOCK):
            rs = slice(blk * SGU_BLOCK, (blk + 1) * SGU_BLOCK)
            cols = []
            for g in range(HEADS):
                cs = slice(g * HD, (g + 1) * HD)
                dmg = dmixed[rs, cs]
                dmb = dmg.astype(BF)
                dbst_ref[:, g:g + 1] += jnp.sum(dmg, axis=1, keepdims=True)
                dws_ref[g] += jnp.where(ri >= ci, _dot_nt(dmb, vln[rs, cs].astype(BF)), 0.0)
                cols.append(_dot_tn(wm[g], dmb))
            blocks.append(jnp.concatenate(cols, axis=1))
        dvln = blocks[0] if len(blocks) == 1 else jnp.concatenate(blocks, axis=0)
        dlg_ref[...] += _colsum(dvln * xh)
        dlb_ref[...] += _colsum(dvln)
        dxh = dvln * lg_ref[...]
        dgv_in = rstd * (dxh - jnp.mean(dxh, axis=-1, keepdims=True)
                         - xh * jnp.mean(dxh * xh, axis=-1, keepdims=True))
        dz_ref[:, D:2 * D] = (dgv_in * dgv).astype(BF)

    row = pl.BlockSpec((tb, D), lambda i: (i, 0))
    vec = pl.BlockSpec((1, D), lambda i: (0, 0))
    wspec = pl.BlockSpec((HEADS, SGU_BLOCK, SGU_BLOCK), lambda i: (0, 0, 0))
    bspec = pl.BlockSpec((SGU_BLOCK, HEADS), lambda i: (0, 0))
    return _pcall(body, name="sgu_bwd", grid=(T // tb,),
                  in_specs=[HBM_SPEC, row, pl.BlockSpec((tb, D), lambda i: (i, 2)),
                            pl.BlockSpec((tb, D), lambda i: (i, 3)), vec, vec, wspec, bspec],
                  out_specs=[pl.BlockSpec((tb, 2 * D), lambda i: (i, 1)), wspec, bspec, vec, vec],
                  out_shape=[jax.ShapeDtypeStruct(dz.shape, BF),
                             jax.ShapeDtypeStruct((HEADS, SGU_BLOCK, SGU_BLOCK), F32),
                             jax.ShapeDtypeStruct((SGU_BLOCK, HEADS), F32),
                             jax.ShapeDtypeStruct((1, D), F32), jax.ShapeDtypeStruct((1, D), F32)],
                  aliases={0: 0}, vmem_mb=48)(dz, dyb_pre, z, z, lg, lb, ws, bst)


def _rglru_bwd(dz, dya_pre, z, h, cw, cb, wa, ba, wx, bx, lam, tb=256):
    T = z.shape[0]
    tb = min(tb, T)
    nrow = T // tb
    per = tb // SUBLANES

    def body(dz_in, dy_ref, xr_ref, xh_ref, gr_ref, h_ref, hh_ref, cw_ref, cb_ref, wa_ref, ba_ref, wx_ref, bx_ref,
             lam_ref, dz_ref, dcw_ref, dcb_ref, dwa_ref, dba_ref, dwx_ref, dbx_ref, dlam_ref, carry, nxt):
        del dz_in
        i = pl.program_id(0)
        first_block = i == nrow - 1

        @pl.when(i == 0)
        def _():
            carry[...] = jnp.zeros_like(carry)
            nxt[...] = jnp.zeros_like(nxt)
            for ref in (dcw_ref, dcb_ref, dwa_ref, dba_ref, dwx_ref, dbx_ref, dlam_ref):
                ref[...] = jnp.zeros_like(ref)

        xr = xr_ref[...]
        pv = jnp.where(first_block, 0.0, xh_ref[...])
        s1 = _shift_down(xr, pv, 1)
        s2 = _shift_down(xr, pv, 2)
        s3 = _shift_down(xr, pv, 3)
        xc = cb_ref[...] + cw_ref[3:4, :] * xr + cw_ref[2:3, :] * s1 + cw_ref[1:2, :] * s2 + cw_ref[0:1, :] * s3
        lam = lam_ref[...]
        r, ig, ls, a, mult = _lru_gates(xc, wa_ref, ba_ref[...], wx_ref, bx_ref[...], lam)
        hv = h_ref[...]
        hprev = _shift_down(hv, jnp.where(first_block, 0.0, hh_ref[...]), 1)
        gg, dgg = _gelu_and_grad(gr_ref[...])
        dy = dy_ref[...].astype(F32)
        dz_ref[:, D:2 * D] = (dy * hv * dgg).astype(BF)

        rows = lax.broadcasted_iota(jnp.int32, (tb, D), 0)
        v = dy * gg + jnp.where(rows == tb - 1, carry[0:1, :], 0.0)
        q = jnp.where(rows < tb - 1, pltpu.roll(a, tb - 1, 0), 0.0)
        d = 1
        while d < tb:
            keep = rows < tb - d
            q_s = jnp.where(keep, pltpu.roll(q, tb - d, 0), 1.0)
            v_s = jnp.where(keep, pltpu.roll(v, tb - d, 0), 0.0)
            v = v + q * v_s
            q = q * q_s
            d *= 2
        gsc = v
        carry[...] = (a * gsc)[0:SUBLANES]

        xi = ig * xc
        dmult = gsc * xi
        dxi = gsc * mult
        dig = dxi * xc
        dxc = dxi * ig
        dlog_a = gsc * hprev * a - dmult * (a * a) / mult
        dlam_ref[...] += _colsum(dlog_a * r) * (LRU_C * _sigmoid(-lam))
        dpr = dlog_a * (LRU_C * ls) * r * (1.0 - r)
        dpi = dig * ig * (1.0 - ig)
        dba_ref[...] += _colsum(dpr)
        dbx_ref[...] += _colsum(dpi)
        back = []
        for hh in range(HEADS):
            cs = slice(hh * HD, (hh + 1) * HD)
            xh = xc[:, cs].astype(BF)
            dprh = dpr[:, cs].astype(BF)
            dpih = dpi[:, cs].astype(BF)
            dwa_ref[hh] += _dot_tn(xh, dprh)
            dwx_ref[hh] += _dot_tn(xh, dpih)
            back.append(_dot_nt(dprh, wa_ref[hh].astype(BF)) + _dot_nt(dpih, wx_ref[hh].astype(BF)))
        dxc = dxc + jnp.concatenate(back, axis=1)

        n8 = nxt[...]
        dxr = (cw_ref[3:4, :] * dxc + cw_ref[2:3, :] * _shift_up(dxc, n8, 1)
               + cw_ref[1:2, :] * _shift_up(dxc, n8, 2) + cw_ref[0:1, :] * _shift_up(dxc, n8, 3))
        nxt[...] = dxc[0:SUBLANES]
        dz_ref[:, 0:D] = dxr.astype(BF)
        dcw_ref[3:4, :] += _colsum(dxc * xr)
        dcw_ref[2:3, :] += _colsum(dxc * s1)
        dcw_ref[1:2, :] += _colsum(dxc * s2)
        dcw_ref[0:1, :] += _colsum(dxc * s3)
        dcb_ref[...] += _colsum(dxc)

    rev = lambda col: (lambda i: (nrow - 1 - i, col))
    halo = lambda col: pl.BlockSpec((SUBLANES, D), lambda i: (jnp.maximum((nrow - 1 - i) * per - 1, 0), col))
    vec = pl.BlockSpec((1, D), lambda i: (0, 0))
    wspec = pl.BlockSpec((HEADS, HD, HD), lambda i: (0, 0, 0))
    c4 = pl.BlockSpec((4, D), lambda i: (0, 0))
    wshape = jax.ShapeDtypeStruct((HEADS, HD, HD), F32)
    vshape = jax.ShapeDtypeStruct((1, D), F32)
    return _pcall(body, name="rglru_bwd", grid=(nrow,),
                  in_specs=[HBM_SPEC, pl.BlockSpec((tb, D), rev(0)), pl.BlockSpec((tb, D), rev(0)), halo(0),
                            pl.BlockSpec((tb, D), rev(1)), pl.BlockSpec((tb, D), rev(0)), halo(0),
                            c4, vec, wspec, vec, wspec, vec, vec],
                  out_specs=[pl.BlockSpec((tb, 2 * D), rev(0)), c4, vec, wspec, vec, wspec, vec, vec],
                  out_shape=[jax.ShapeDtypeStruct(dz.shape, BF), jax.ShapeDtypeStruct((4, D), F32), vshape,
                             wshape, vshape, wshape, vshape, vshape],
                  scratch=[pltpu.VMEM((SUBLANES, D), F32), pltpu.VMEM((SUBLANES, D), F32)],
                  aliases={0: 0}, vmem_mb=56)(dz, dya_pre, z, z, z, h, h, cw, cb, wa, ba, wx, bx, lam)


def _pack_rows(parts):
    out = []
    for p in parts:
        q = p.reshape(-1, LANES)
        pad = (-q.shape[0]) % SUBLANES
        if pad:
            q = jnp.concatenate([q, jnp.zeros((pad, LANES), q.dtype)], axis=0)
        out.append(q)
    return jnp.concatenate(out, axis=0)


def _rows_of(shape):
    n = 1
    for s in shape:
        n *= s
    rows = n // LANES
    return rows + (-rows) % SUBLANES


def kernel(x, c, w_ada, b_ada, norm_mix_g, w_in, rnn_conv_w, rnn_conv_b, lru_w_a, lru_b_a, lru_w_x, lru_b_x, lru_lambda, sgu_ln_g, sgu_ln_b, sgu_w_s, sgu_b_s, w_branch_a, w_branch_b, w_out, norm_ffn_g, w_up, ffn_conv_w, ffn_conv_b, w_down, norm_final_g, loss_target, m_w_ada, m_b_ada, m_norm_mix_g, m_w_in, m_rnn_conv_w, m_rnn_conv_b, m_lru_w_a, m_lru_b_a, m_lru_w_x, m_lru_b_x, m_lru_lambda, m_sgu_ln_g, m_sgu_ln_b, m_sgu_w_s, m_sgu_b_s, m_w_branch_a, m_w_branch_b, m_w_out, m_norm_ffn_g, m_w_up, m_ffn_conv_w, m_ffn_conv_b, m_w_down, m_norm_final_g, v_w_ada, v_b_ada, v_norm_mix_g, v_w_in, v_rnn_conv_w, v_rnn_conv_b, v_lru_w_a, v_lru_b_a, v_lru_w_x, v_lru_b_x, v_lru_lambda, v_sgu_ln_g, v_sgu_ln_b, v_sgu_w_s, v_sgu_b_s, v_w_branch_a, v_w_branch_b, v_w_out, v_norm_ffn_g, v_w_up, v_ffn_conv_w, v_ffn_conv_b, v_w_down, v_norm_final_g):
    args = dict(locals())
    T = x.shape[1]
    mx, my, mc = lax.axis_index("x"), lax.axis_index("y"), lax.axis_index("c")
    chip = 2 * mx + my
    dev = 2 * chip + mc
    vec = lambda a: a.reshape(1, -1)

    xt = x.reshape(T, D)
    tgt = loss_target.reshape(T, D)
    ns = w_in.shape[2]
    dff = w_down.shape[1] * N_CHIPS

    c_all = _gather8(c.reshape(SUBLANES, LANES), "gather_c").reshape(N_DEV, D)
    b_ada_sh = lax.dynamic_slice(b_ada, (0, chip * ns), (1, ns))
    mod_sh = _mod_fwd(c_all, w_ada[0], b_ada_sh)

    w_in_b, w_up_b, w_down_b, wba_b, wbb_b, wo_b = _cast_shards(
        [w_in[0], w_up[0], w_down[0], w_branch_a[0], w_branch_b[0], w_out[0]])
    (w_in4, w_up4, w_down4, wba4, wbb4, wo4, rcw4, fcw4, mod4) = _gather_weights(
        [w_in_b, w_up_b, w_down_b, wba_b, wbb_b, wo_b, rnn_conv_w[0], ffn_conv_w[0], mod_sh],
        [True] * 6 + [False] * 3)
    wd_full = w_down4.reshape(dff, D)
    wba_full = wba4.reshape(D, D)
    wbb_full = wbb4.reshape(D, D)
    wo_full = wo4.reshape(D, D)
    rcw_full = jnp.transpose(rcw4, (1, 0, 2)).reshape(4, D)
    fcw_full = jnp.transpose(fcw4, (1, 0, 2)).reshape(3, 2 * dff)
    mod = lax.dynamic_index_in_dim(mod4, dev, axis=1, keepdims=False).reshape(1, 6 * D)
    shift1, scale1, gate1, shift2, scale2, gate2 = [mod[:, k * D:(k + 1) * D] for k in range(6)]

    h1, z = _modnorm_matmul(xt, norm_mix_g, scale1, shift1, w_in4, "norm_in_proj")
    bst = jnp.transpose(sgu_b_s[0])
    h_lru, ya_pre = _rglru_fwd(z, rcw_full, rnn_conv_b, lru_w_a[0], lru_b_a, lru_w_x[0], lru_b_x, lru_lambda)
    yb_pre = _sgu_fwd(z, sgu_ln_g, sgu_ln_b, sgu_w_s[0], bst)
    x2, merged, ya, yb, o1 = _mix_out(ya_pre, yb_pre, z, xt, gate1, wba_full, wbb_full, wo_full)
    h2, up = _modnorm_matmul(x2, norm_ffn_g, scale2, shift2, w_up4, "norm_up_proj")
    f = _ffn_gate(up, fcw_full, ffn_conv_b)
    loss_part, dx3, dfo, dgf, dgate2 = _ffn_down_loss(f, wd_full, x2, gate2, vec(norm_final_g), tgt)

    ffn_blk = 768
    ffn_map = functools.partial(_ffn_col_block, ncb=dff // ffn_blk)
    dup, dwd, dfcw_a, dfcw_v, dfcb_a, dfcb_v = _ffn_bwd(dfo, wd_full, up, fcw_full, ffn_conv_b, cw_blk=ffn_blk)
    dw_up4 = _mm_tn_cols(h2, dup, "dw_up", N_CHIPS, ffn_blk, colmap=ffn_map)
    dx2, dshift2, dscale2, dg_ffn, do1, dgate1 = _mm_nt_normbwd(
        dup, w_up4, x2, dx3, norm_ffn_g, scale2, "dh2_norm_bwd", gate=gate1, o=o1, colmap=ffn_map, tk=ffn_blk)
    dz, dya, dyb, dya_pre, dyb_pre = _mix_bwd(do1, ya, yb, z, wo_full, wba_full, wbb_full)
    dwo = _mm_tn_cols(merged, do1, "dw_out", 1, D)
    dwba = _mm_tn_cols(ya_pre, dya, "dw_branch_a", 1, D)
    dwbb = _mm_tn_cols(yb_pre, dyb, "dw_branch_b", 1, D)
    dz, dws, dbst, dlg, dlb = _sgu_bwd(dz, dyb_pre, z, sgu_ln_g, sgu_ln_b, sgu_w_s[0], bst)
    dz, drcw, drcb, dwa, dba, dwx, dbx, dlam = _rglru_bwd(
        dz, dya_pre, z, h_lru, rcw_full, rnn_conv_b, lru_w_a[0], lru_b_a, lru_w_x[0], lru_b_x, lru_lambda)
    dw_in4 = _mm_tn_cols(h1, dz, "dw_in", N_CHIPS, ns)
    grad_x, dshift1, dscale1, dg_mix = _mm_nt_normbwd(dz, w_in4, xt, dx2, norm_mix_g, scale1, "dh1_norm_bwd")
    dmod = jnp.concatenate([dshift1, dscale1, dgate1, dshift2, dscale2, dgate2], axis=1)

    big = [("w_in", dw_in4), ("w_up", dw_up4), ("w_down", dwd.reshape(N_CHIPS, dff // N_CHIPS, D)),
           ("w_branch_a", dwba.reshape(N_CHIPS, D // N_CHIPS, D)), ("w_branch_b", dwbb.reshape(N_CHIPS, D // N_CHIPS, D)),
           ("w_out", dwo.reshape(N_CHIPS, D // N_CHIPS, D))]
    core = mc.astype(jnp.int32).reshape(1)
    from_core = _send_other_half([g for _, g in big])
    chip_sums = [_add_halves_bf16(core, g, o, "sum_cores_" + n) for (n, g), o in zip(big, from_core)]
    parts = _exchange_xy(chip_sums, True, "scatter_grads")
    totals = [_sum_parts(p, "sum_chips_" + n) for (n, _), p in zip(big, parts)]
    fulls = _share_halves(totals)
    out = {}
    for (n, _), full in zip(big, fulls):
        shape = args[n].shape
        res = _adamw(args[n][0], args["m_" + n][0], args["v_" + n][0], [full.reshape(shape[1:])], "adamw_" + n)
        for kind, r in zip(("grad_", "delta_", "new_m_", "new_v_"), res):
            out[kind + n] = r.reshape(shape)

    small = [("b_ada", dmod), ("norm_mix_g", dg_mix), ("rnn_conv_b", drcb), ("lru_w_a", dwa), ("lru_b_a", dba),
             ("lru_w_x", dwx), ("lru_b_x", dbx), ("lru_lambda", dlam), ("sgu_ln_g", dlg), ("sgu_ln_b", dlb),
             ("sgu_w_s", dws), ("sgu_b_s", jnp.transpose(dbst)), ("norm_ffn_g", dg_ffn),
             ("ffn_conv_b", jnp.concatenate([dfcb_a, dfcb_v], axis=1)), ("norm_final_g", dgf)]
    dfcw = jnp.concatenate([dfcw_a, dfcw_v], axis=1)
    r_small = sum(_rows_of(args[n].shape) for n, _ in small)
    r_pad = r_small + (-r_small) % 256
    fill = jnp.zeros((r_pad - r_small, LANES), F32)
    g_pack = jnp.concatenate([_pack_rows([g for _, g in small]), fill, _pack_rows([drcw, dfcw])], axis=0)
    g_all = _gather8(g_pack, "gather_small_grads")
    g_sum = _sum_parts(g_all, "sum_small_grads")

    def pack_small(prefix):
        return jnp.concatenate([_pack_rows([args[prefix + n] for n, _ in small]), fill], axis=0)

    res = _adamw(pack_small(""), pack_small("m_"), pack_small("v_"), [g_sum[:r_pad]], "adamw_small")
    off = 0
    for n, _ in small:
        shape = args[n].shape
        rows = _rows_of(shape)
        for kind, r in zip(("grad_", "delta_", "new_m_", "new_v_"), res):
            out[kind + n] = r[off:off + rows].reshape(shape)
        off += rows

    rcw_cols = rnn_conv_w.shape[2]
    g_rcw = lax.dynamic_slice(g_sum[r_pad:r_pad + 32].reshape(4, D), (0, chip * rcw_cols), (4, rcw_cols))
    g_fcw = lax.dynamic_slice(g_sum[r_pad + 32:r_pad + 32 + 144].reshape(3, 2 * dff), (0, chip * ns), (3, ns))
    conv = [("rnn_conv_w", g_rcw), ("ffn_conv_w", g_fcw)]
    res = _adamw(_pack_rows([args[n] for n, _ in conv]), _pack_rows([args["m_" + n] for n, _ in conv]),
                 _pack_rows([args["v_" + n] for n, _ in conv]), [_pack_rows([g for _, g in conv])], "adamw_conv")
    off = 0
    for n, _ in conv:
        shape = args[n].shape
        cnt = shape[1] * shape[2] // LANES
        for kind, r in zip(("grad_", "delta_", "new_m_", "new_v_"), res):
            out[kind + n] = r[off:off + cnt].reshape(shape)
        off += _rows_of(shape)

    dmod_all = g_all[:, 0:6 * D // LANES, :].reshape(N_DEV, 6 * D)
    dmod_sh = lax.dynamic_slice(dmod_all, (0, chip * ns), (N_DEV, ns))
    res = _ada_adamw(jnp.transpose(c_all), dmod_sh, w_ada[0], m_w_ada[0], v_w_ada[0])
    for kind, r in zip(("grad_", "delta_", "new_m_", "new_v_"), res):
        out[kind + "w_ada"] = r.reshape(w_ada.shape)

    loss = lax.psum(loss_part[0, 0], ("x", "y", "c"))
    names = ["w_ada", "b_ada", "norm_mix_g", "w_in", "rnn_conv_w", "rnn_conv_b", "lru_w_a", "lru_b_a", "lru_w_x",
             "lru_b_x", "lru_lambda", "sgu_ln_g", "sgu_ln_b", "sgu_w_s", "sgu_b_s", "w_branch_a", "w_branch_b",
             "w_out", "norm_ffn_g", "w_up", "ffn_conv_w", "ffn_conv_b", "w_down", "norm_final_g"]
    result = [loss, grad_x.reshape(x.shape)]
    for kind in ("grad_", "delta_", "new_m_", "new_v_"):
        result += [out[kind + n] for n in names]
    return tuple(result)
```

```python
import functools

import jax
import jax.numpy as jnp
from jax import lax
from jax.experimental import pallas as pl
from jax.experimental.pallas import tpu as pltpu

F32 = jnp.float32
BF = jnp.bfloat16

D = 1024
HEADS = 8
HD = D // HEADS
SGU_BLOCK = 128
N_CHIPS = 4
N_DEV = 8
EPS = 1e-6
LRU_C = 8.0
LANES = 128
SUBLANES = 8

ADAM_LR = 0.001
ADAM_B1 = 0.9
ADAM_B2 = 0.999
ADAM_EPS = 1e-08
ADAM_WD = 0.01
ADAM_STEP = 10

GELU_K0 = 0.7978845608028654
GELU_K1 = 0.044715

HBM_SPEC = pl.BlockSpec(memory_space=pltpu.HBM)
MESH_ID = pl.DeviceIdType.MESH


def _pcall(body, *, name, out_shape, grid=(), in_specs=None, out_specs=None, scratch=(), vmem_mb=32, aliases=None,
           grid_spec=None):
    kw = {}
    if aliases:
        kw["input_output_aliases"] = aliases
    if grid_spec is not None:
        kw["grid_spec"] = grid_spec
        ndim = len(grid_spec.grid)
    else:
        kw.update(grid=grid, in_specs=in_specs, out_specs=out_specs, scratch_shapes=list(scratch))
        ndim = len(grid)
    if ndim:
        params = pltpu.CompilerParams(dimension_semantics=("arbitrary",) * ndim, vmem_limit_bytes=vmem_mb * 2 ** 20)
    else:
        params = pltpu.CompilerParams(vmem_limit_bytes=vmem_mb * 2 ** 20)
    return pl.pallas_call(body, name=name, out_shape=out_shape, compiler_params=params, **kw)


def _gelu(x):
    return 0.5 * x * (1.0 + jnp.tanh(GELU_K0 * (x + GELU_K1 * x * x * x)))


def _gelu_and_grad(x):
    x2 = x * x
    t = jnp.tanh(GELU_K0 * x * (1.0 + GELU_K1 * x2))
    g = 0.5 * x * (1.0 + t)
    dg = 0.5 * (1.0 + t) + 0.5 * x * (1.0 - t * t) * (GELU_K0 * (1.0 + 3.0 * GELU_K1 * x2))
    return g, dg


def _sigmoid(x):
    return 1.0 / (1.0 + jnp.exp(-x))


def _log_sigmoid(x):
    e = jnp.exp(-jnp.abs(x))
    u = 1.0 + e
    d = u - 1.0
    l1p = jnp.where(d == 0.0, e, jnp.log(u) * (e / jnp.where(d == 0.0, 1.0, d)))
    return jnp.minimum(x, 0.0) - l1p


def _dot(a, b):
    return jnp.dot(a, b, preferred_element_type=F32)


def _dot_nt(a, b):
    return lax.dot_general(a, b, (((1,), (1,)), ((), ())), preferred_element_type=F32)


def _dot_tn(a, b):
    return lax.dot_general(a, b, (((0,), (0,)), ((), ())), preferred_element_type=F32)


def _shift_down(x, halo, s):
    r = pltpu.roll(x, s, 0)
    rows = lax.broadcasted_iota(jnp.int32, (SUBLANES, x.shape[1]), 0)
    head = jnp.where(rows < s, pltpu.roll(halo, s, 0), r[0:SUBLANES])
    return jnp.concatenate([head, r[SUBLANES:]], axis=0)


def _shift_up(x, halo, s):
    n = x.shape[0]
    r = pltpu.roll(x, n - s, 0)
    rows = lax.broadcasted_iota(jnp.int32, (SUBLANES, x.shape[1]), 0)
    tail = jnp.where(rows >= SUBLANES - s, pltpu.roll(halo, SUBLANES - s, 0), r[n - SUBLANES:n])
    return jnp.concatenate([r[:n - SUBLANES], tail], axis=0)


def _scan_rows(a, u, reverse):
    n, width = a.shape
    rows = lax.broadcasted_iota(jnp.int32, (n, width), 0)
    d = 1
    while d < n:
        if d < SUBLANES:
            keep = rows < n - d if reverse else rows >= d
            shift = n - d if reverse else d
            a_s = jnp.where(keep, pltpu.roll(a, shift, 0), 1.0)
            u_s = jnp.where(keep, pltpu.roll(u, shift, 0), 0.0)
        elif reverse:
            a_s = jnp.concatenate([a[d:], jnp.ones((d, width), a.dtype)], axis=0)
            u_s = jnp.concatenate([u[d:], jnp.zeros((d, width), u.dtype)], axis=0)
        else:
            a_s = jnp.concatenate([jnp.ones((d, width), a.dtype), a[:n - d]], axis=0)
            u_s = jnp.concatenate([jnp.zeros((d, width), u.dtype), u[:n - d]], axis=0)
        u = a * u_s + u
        a = a * a_s
        d *= 2
    return a, u


def _colsum(x):
    return jnp.sum(x, axis=0, keepdims=True)


def _rms_stats(x):
    r = lax.rsqrt(jnp.mean(x * x, axis=-1, keepdims=True) + EPS)
    return r, x * r


def _lru_gates(xc, wa_ref, ba, wx_ref, bx, lam):
    pr, pi = [], []
    for hh in range(HEADS):
        xh = xc[:, hh * HD:(hh + 1) * HD].astype(BF)
        pr.append(_dot(xh, wa_ref[hh].astype(BF)))
        pi.append(_dot(xh, wx_ref[hh].astype(BF)))
    r = _sigmoid(jnp.concatenate(pr, axis=1) + ba)
    ig = _sigmoid(jnp.concatenate(pi, axis=1) + bx)
    ls = _log_sigmoid(lam)
    log_a = LRU_C * r * ls
    a = jnp.exp(log_a)
    x2 = 2.0 * log_a
    u = a * a
    lu = jnp.log(jnp.maximum(u, 1e-37))
    em1 = jnp.where(lu == 0.0, x2, jnp.where(u < 1e-30, -1.0, (u - 1.0) * x2 / jnp.where(lu == 0.0, 1.0, lu)))
    mult = jnp.sqrt(-em1)
    return r, ig, ls, a, mult


def _sgu_mix(vln, ws_ref, bst_ref, tb):
    ri = lax.broadcasted_iota(jnp.int32, (SGU_BLOCK, SGU_BLOCK), 0)
    ci = lax.broadcasted_iota(jnp.int32, (SGU_BLOCK, SGU_BLOCK), 1)
    wm = [jnp.where(ri >= ci, ws_ref[g], 0.0).astype(BF) for g in range(HEADS)]
    blocks = []
    for blk in range(tb // SGU_BLOCK):
        cols = []
        for g in range(HEADS):
            vb = vln[blk * SGU_BLOCK:(blk + 1) * SGU_BLOCK, g * HD:(g + 1) * HD].astype(BF)
            cols.append(_dot(wm[g], vb) + bst_ref[:, g:g + 1])
        blocks.append(jnp.concatenate(cols, axis=1))
    mixed = blocks[0] if len(blocks) == 1 else jnp.concatenate(blocks, axis=0)
    return wm, mixed


def _layernorm_stats(v):
    mu = jnp.mean(v, axis=-1, keepdims=True)
    vc = v - mu
    rstd = lax.rsqrt(jnp.mean(vc * vc, axis=-1, keepdims=True) + EPS)
    return rstd, vc * rstd


def _my_xyc():
    return lax.axis_index("x"), lax.axis_index("y"), lax.axis_index("c")


def _exchange_xy(srcs, scatter, name):
    n = len(srcs)
    out_shape = []
    for s in srcs:
        shard = s.shape[1:] if scatter else s.shape
        out_shape.append(jax.ShapeDtypeStruct((N_CHIPS,) + tuple(shard), s.dtype))

    def body(*refs):
        src, out = refs[:n], refs[n:2 * n]
        send_sems, recv_sems, loc_sems = refs[2 * n:]
        x, y, c = _my_xyc()
        me = 2 * x + y
        remote, local = [], []
        for a in range(n):
            def piece(p, a=a):
                return src[a].at[p] if scatter else src[a]
            lc = pltpu.make_async_copy(piece(me), out[a].at[me], loc_sems.at[a])
            lc.start()
            local.append(lc)
            for k, (fx, fy) in enumerate(((1, 0), (0, 1), (1, 1))):
                px = 1 - x if fx else x
                py = 1 - y if fy else y
                cp = pltpu.make_async_remote_copy(
                    src_ref=piece(2 * px + py), dst_ref=out[a].at[me],
                    send_sem=send_sems.at[a, k], recv_sem=recv_sems.at[a, k],
                    device_id=(px, py, c), device_id_type=MESH_ID)
                cp.start()
                remote.append(cp)
        for cp in remote:
            cp.wait()
        for lc in local:
            lc.wait()

    return _pcall(body, name=name, out_shape=out_shape, in_specs=[HBM_SPEC] * n, out_specs=[HBM_SPEC] * n,
                  scratch=[pltpu.SemaphoreType.DMA((n, 3)), pltpu.SemaphoreType.DMA((n, 3)),
                           pltpu.SemaphoreType.DMA((n,))])(*srcs)


def _gather_weights(srcs, halve):
    n = len(srcs)
    out_shape = [jax.ShapeDtypeStruct((N_CHIPS,) + s.shape, s.dtype) for s in srcs]

    def body(*refs):
        src, out = refs[:n], refs[n:2 * n]
        send_sems, recv_sems, fwd_send, fwd_recv, loc_sems = refs[2 * n:]
        x, y, c = _my_xyc()
        me = 2 * x + y
        chips = [(1 - x, y), (x, 1 - y), (1 - x, 1 - y)]

        def half(ref, a, which):
            if not halve[a]:
                return ref
            h = srcs[a].shape[0] // 2
            return ref.at[pl.ds(which * h, h)]

        def ici(a, k, frm):
            px, py = chips[k]
            return pltpu.make_async_remote_copy(
                src_ref=half(src[a], a, c), dst_ref=half(out[a].at[frm], a, c),
                send_sem=send_sems.at[a, k], recv_sem=recv_sems.at[a, k],
                device_id=(px, py, c), device_id_type=MESH_ID)

        def d2d(a, k, which):
            px, py = chips[k]
            rows = half(out[a].at[2 * px + py], a, which)
            return pltpu.make_async_remote_copy(
                src_ref=rows, dst_ref=rows, send_sem=fwd_send.at[a, k], recv_sem=fwd_recv.at[a, k],
                device_id=(x, y, 1 - c), device_id_type=MESH_ID)

        local, sends = [], []
        for a in range(n):
            lc = pltpu.make_async_copy(src[a], out[a].at[me], loc_sems.at[a])
            lc.start()
            local.append(lc)
            for k in range(3):
                cp = ici(a, k, me)
                cp.start()
                sends.append(cp)
        for a in range(n):
            for k in range(3):
                px, py = chips[k]
                ici(a, k, 2 * px + py).wait_recv()
                if halve[a]:
                    fw = d2d(a, k, c)
                    fw.start()
                    sends.append(fw)
        for a in range(n):
            if halve[a]:
                for k in range(3):
                    d2d(a, k, 1 - c).wait_recv()
        for cp in sends:
            cp.wait_send()
        for lc in local:
            lc.wait()

    sem = pltpu.SemaphoreType.DMA((n, 3))
    return _pcall(body, name="gather_weights", out_shape=out_shape, in_specs=[HBM_SPEC] * n,
                  out_specs=[HBM_SPEC] * n, scratch=[sem, sem, sem, sem, pltpu.SemaphoreType.DMA((n,))])(*srcs)


def _send_other_half(grads):
    n = len(grads)
    out_shape = [jax.ShapeDtypeStruct((N_CHIPS, g.shape[1] // 2, g.shape[2]), g.dtype) for g in grads]

    def body(*refs):
        src, out = refs[:n], refs[n:2 * n]
        send_sems, recv_sems = refs[2 * n:]
        x, y, c = _my_xyc()
        cps = []
        for a in range(n):
            h = grads[a].shape[1] // 2
            cp = pltpu.make_async_remote_copy(
                src_ref=src[a].at[:, pl.ds((1 - c) * h, h), :], dst_ref=out[a],
                send_sem=send_sems.at[a], recv_sem=recv_sems.at[a],
                device_id=(x, y, 1 - c), device_id_type=MESH_ID)
            cp.start()
            cps.append(cp)
        for cp in cps:
            cp.wait()

    return _pcall(body, name="send_other_half", out_shape=out_shape, in_specs=[HBM_SPEC] * n,
                  out_specs=[HBM_SPEC] * n,
                  scratch=[pltpu.SemaphoreType.DMA((n,)), pltpu.SemaphoreType.DMA((n,))])(*grads)


def _share_halves(totals):
    n = len(totals)
    out_shape = [jax.ShapeDtypeStruct((2,) + t.shape, t.dtype) for t in totals]

    def body(*refs):
        src, out = refs[:n], refs[n:2 * n]
        send_sems, recv_sems, loc_sems = refs[2 * n:]
        x, y, c = _my_xyc()
        cps, local = [], []
        for a in range(n):
            lc = pltpu.make_async_copy(src[a], out[a].at[c], loc_sems.at[a])
            lc.start()
            local.append(lc)
            cp = pltpu.make_async_remote_copy(
                src_ref=src[a], dst_ref=out[a].at[c], send_sem=send_sems.at[a], recv_sem=recv_sems.at[a],
                device_id=(x, y, 1 - c), device_id_type=MESH_ID)
            cp.start()
            cps.append(cp)
        for cp in cps:
            cp.wait()
        for lc in local:
            lc.wait()

    sem = pltpu.SemaphoreType.DMA((n,))
    return _pcall(body, name="share_halves", out_shape=out_shape, in_specs=[HBM_SPEC] * n,
                  out_specs=[HBM_SPEC] * n, scratch=[sem, sem, sem])(*totals)


def _gather8(src, name):
    def body(src_ref, out_ref, send_sems, recv_sems, loc_sem):
        x, y, c = _my_xyc()
        me = 4 * x + 2 * y + c
        lc = pltpu.make_async_copy(src_ref, out_ref.at[me], loc_sem)
        lc.start()
        cps = []
        for k in range(1, N_DEV):
            px = 1 - x if (k >> 2) & 1 else x
            py = 1 - y if (k >> 1) & 1 else y
            pc = 1 - c if k & 1 else c
            cp = pltpu.make_async_remote_copy(
                src_ref=src_ref, dst_ref=out_ref.at[me], send_sem=send_sems.at[k - 1], recv_sem=recv_sems.at[k - 1],
                device_id=(px, py, pc), device_id_type=MESH_ID)
            cp.start()
            cps.append(cp)
        for cp in cps:
            cp.wait()
        lc.wait()

    return _pcall(body, name=name, out_shape=jax.ShapeDtypeStruct((N_DEV,) + src.shape, src.dtype),
                  in_specs=[HBM_SPEC], out_specs=HBM_SPEC,
                  scratch=[pltpu.SemaphoreType.DMA((N_DEV - 1,)), pltpu.SemaphoreType.DMA((N_DEV - 1,)),
                           pltpu.SemaphoreType.DMA])(src)


def _cast_shards(arrs):
    n = len(arrs)

    def body(*refs):
        for a in range(n):
            refs[n + a][...] = refs[a][...].astype(BF)

    specs = [pl.BlockSpec((s.shape[0] // 4, s.shape[1]), lambda i: (i, 0)) for s in arrs]
    return _pcall(body, name="cast_shards", grid=(4,), in_specs=specs, out_specs=specs,
                  out_shape=[jax.ShapeDtypeStruct(s.shape, BF) for s in arrs])(*arrs)


def _row_tile(rows, cols):
    t = rows
    while t * cols * 4 > (3 << 19) and t % 16 == 0:
        t //= 2
    return t


def _sum_parts(parts, name):
    p, rows, cols = parts.shape
    tr = _row_tile(rows, cols * p // 2)

    def body(p_ref, o_ref):
        acc = p_ref[0].astype(F32)
        for k in range(1, p):
            acc = acc + p_ref[k].astype(F32)
        o_ref[...] = acc

    return _pcall(body, name=name, grid=(rows // tr,),
                  in_specs=[pl.BlockSpec((p, tr, cols), lambda i: (0, i, 0))],
                  out_specs=pl.BlockSpec((tr, cols), lambda i: (i, 0)),
                  out_shape=jax.ShapeDtypeStruct((rows, cols), F32), vmem_mb=48)(parts)


def _add_halves_bf16(core, grad, other, name):
    nchip, rows, cols = grad.shape
    h = rows // 2
    tr = _row_tile(h, cols)
    nh = h // tr

    def body(c_ref, g_ref, o_ref, s_ref):
        del c_ref
        s_ref[...] = (g_ref[...] + o_ref[...]).astype(BF)

    grid_spec = pltpu.PrefetchScalarGridSpec(
        num_scalar_prefetch=1, grid=(nchip, nh),
        in_specs=[pl.BlockSpec((1, tr, cols), lambda p, i, c_ref: (p, c_ref[0] * nh + i, 0)),
                  pl.BlockSpec((1, tr, cols), lambda p, i, c_ref: (p, i, 0))],
        out_specs=pl.BlockSpec((1, tr, cols), lambda p, i, c_ref: (p, i, 0)))
    return _pcall(body, name=name, grid_spec=grid_spec, out_shape=jax.ShapeDtypeStruct((nchip, h, cols), BF),
                  vmem_mb=48)(core, grad, other)


def _adamw_math(w, g, m, v):
    m2 = ADAM_B1 * m + (1.0 - ADAM_B1) * g
    v2 = ADAM_B2 * v + (1.0 - ADAM_B2) * (g * g)
    m_hat = m2 / (1.0 - ADAM_B1 ** ADAM_STEP)
    v_hat = v2 / (1.0 - ADAM_B2 ** ADAM_STEP)
    delta = -ADAM_LR * (m_hat / (jnp.sqrt(v_hat) + ADAM_EPS) + ADAM_WD * w)
    return delta, m2, v2


def _adamw(w, m, v, grads, name):
    rows, cols = w.shape
    tr = _row_tile(rows, cols)
    ng = len(grads)

    def body(*refs):
        w_ref, m_ref, v_ref = refs[:3]
        g = refs[3][...]
        for k in range(1, ng):
            g = g + refs[3 + k][...]
        g_ref, d_ref, m2_ref, v2_ref = refs[3 + ng:]
        delta, m2, v2 = _adamw_math(w_ref[...], g, m_ref[...], v_ref[...])
        g_ref[...] = g
        d_ref[...] = delta
        m2_ref[...] = m2
        v2_ref[...] = v2

    spec = pl.BlockSpec((tr, cols), lambda i: (i, 0))
    return _pcall(body, name=name, grid=(rows // tr,), in_specs=[spec] * (3 + ng), out_specs=[spec] * 4,
                  out_shape=[jax.ShapeDtypeStruct((rows, cols), F32)] * 4, vmem_mb=48)(w, m, v, *grads)


def _ada_adamw(ct, dmod, w, m, v):
    rows, cols = w.shape
    tr = _row_tile(rows, cols)

    def body(ct_ref, dm_ref, w_ref, m_ref, v_ref, g_ref, d_ref, m2_ref, v2_ref):
        cv = ct_ref[...]
        ca = cv * _sigmoid(cv)
        g = ca[:, 0:1] * dm_ref[0:1, :]
        for b in range(1, N_DEV):
            g = g + ca[:, b:b + 1] * dm_ref[b:b + 1, :]
        delta, m2, v2 = _adamw_math(w_ref[...], g, m_ref[...], v_ref[...])
        g_ref[...] = g
        d_ref[...] = delta
        m2_ref[...] = m2
        v2_ref[...] = v2

    spec = pl.BlockSpec((tr, cols), lambda i: (i, 0))
    return _pcall(body, name="ada_adamw", grid=(rows // tr,),
                  in_specs=[pl.BlockSpec((tr, N_DEV), lambda i: (i, 0)), pl.BlockSpec((N_DEV, cols), lambda i: (0, 0)),
                            spec, spec, spec],
                  out_specs=[spec] * 4, out_shape=[jax.ShapeDtypeStruct((rows, cols), F32)] * 4,
                  vmem_mb=48)(ct, dmod, w, m, v)


def _mod_fwd(c_all, w, b):
    cols = w.shape[1]
    tn = cols // 3

    def body(c_ref, w_ref, b_ref, o_ref):
        cv = c_ref[...]
        ca = (cv * _sigmoid(cv)).astype(BF)
        o_ref[...] = _dot(ca, w_ref[...].astype(BF)) + b_ref[...]

    return _pcall(body, name="mod_fwd", grid=(3,),
                  in_specs=[pl.BlockSpec((N_DEV, D), lambda j: (0, 0)), pl.BlockSpec((D, tn), lambda j: (0, j)),
                            pl.BlockSpec((1, tn), lambda j: (0, j))],
                  out_specs=pl.BlockSpec((N_DEV, tn), lambda j: (0, j)),
                  out_shape=jax.ShapeDtypeStruct((N_DEV, cols), F32))(c_all, w, b)


def _resident(shape):
    zeros = (0,) * len(shape)
    return pl.BlockSpec(shape, lambda *_: zeros, pipeline_mode=pl.Buffered(1))


def _modnorm_matmul(x, g, scale, shift, w4, name, tm=256):
    T = x.shape[0]
    tm = min(tm, T)
    ns = w4.shape[2]

    def body(x_ref, g_ref, sc_ref, sh_ref, w_ref, h_ref, z_ref):
        _, xh = _rms_stats(x_ref[...])
        h = ((xh * g_ref[...]) * (1.0 + sc_ref[...]) + sh_ref[...]).astype(BF)
        h_ref[...] = h
        for j in range(N_CHIPS):
            z_ref[:, j * ns:(j + 1) * ns] = _dot(h, w_ref[j])

    vec = pl.BlockSpec((1, D), lambda i: (0, 0))
    return _pcall(body, name=name, grid=(T // tm,),
                  in_specs=[pl.BlockSpec((tm, D), lambda i: (i, 0)), vec, vec, vec, _resident(w4.shape)],
                  out_specs=[pl.BlockSpec((tm, D), lambda i: (i, 0)), pl.BlockSpec((tm, N_CHIPS * ns), lambda i: (i, 0))],
                  out_shape=[jax.ShapeDtypeStruct((T, D), BF), jax.ShapeDtypeStruct((T, N_CHIPS * ns), F32)],
                  vmem_mb=48)(x, g, scale, shift, w4)


def _rglru_fwd(z, cw, cb, wa, ba, wx, bx, lam, tb=256):
    T = z.shape[0]
    tb = min(tb, T)

    def body(xr_ref, gr_ref, cw_ref, cb_ref, wa_ref, ba_ref, wx_ref, bx_ref, lam_ref, h_ref, ya_ref, prev, hc):
        i = pl.program_id(0)

        @pl.when(i == 0)
        def _():
            prev[...] = jnp.zeros_like(prev)
            hc[...] = jnp.zeros_like(hc)

        xr = xr_ref[...]
        pv = prev[...]
        xc = (cb_ref[...] + cw_ref[3:4, :] * xr + cw_ref[2:3, :] * _shift_down(xr, pv, 1)
              + cw_ref[1:2, :] * _shift_down(xr, pv, 2) + cw_ref[0:1, :] * _shift_down(xr, pv, 3))
        prev[...] = xr[tb - SUBLANES:tb]
        _, ig, _, a, mult = _lru_gates(xc, wa_ref, ba_ref[...], wx_ref, bx_ref[...], lam_ref[...])
        a, u = _scan_rows(a, mult * (ig * xc), reverse=False)
        h = u + a * hc[SUBLANES - 1:SUBLANES, :]
        hc[...] = h[tb - SUBLANES:tb]
        h_ref[...] = h
        ya_ref[...] = (h * _gelu(gr_ref[...])).astype(BF)

    vec = pl.BlockSpec((1, D), lambda i: (0, 0))
    wspec = pl.BlockSpec((HEADS, HD, HD), lambda i: (0, 0, 0))
    return _pcall(body, name="rglru_fwd", grid=(T // tb,),
                  in_specs=[pl.BlockSpec((tb, D), lambda i: (i, 0)), pl.BlockSpec((tb, D), lambda i: (i, 1)),
                            pl.BlockSpec((4, D), lambda i: (0, 0)), vec, wspec, vec, wspec, vec, vec],
                  out_specs=[pl.BlockSpec((tb, D), lambda i: (i, 0))] * 2,
                  out_shape=[jax.ShapeDtypeStruct((T, D), F32), jax.ShapeDtypeStruct((T, D), BF)],
                  scratch=[pltpu.VMEM((SUBLANES, D), F32), pltpu.VMEM((SUBLANES, D), F32)],
                  vmem_mb=48)(z, z, cw, cb, wa, ba, wx, bx, lam)


def _sgu_fwd(z, lg, lb, ws, bst, tb=256):
    T = z.shape[0]
    tb = min(tb, T)

    def body(zu_ref, zv_ref, lg_ref, lb_ref, ws_ref, bst_ref, yb_ref):
        _, xh = _layernorm_stats(_gelu(zv_ref[...]))
        vln = xh * lg_ref[...] + lb_ref[...]
        _, mixed = _sgu_mix(vln, ws_ref, bst_ref, tb)
        yb_ref[...] = (_gelu(zu_ref[...]) * mixed).astype(BF)

    vec = pl.BlockSpec((1, D), lambda i: (0, 0))
    return _pcall(body, name="sgu_fwd", grid=(T // tb,),
                  in_specs=[pl.BlockSpec((tb, D), lambda i: (i, 2)), pl.BlockSpec((tb, D), lambda i: (i, 3)), vec, vec,
                            pl.BlockSpec((HEADS, SGU_BLOCK, SGU_BLOCK), lambda i: (0, 0, 0)),
                            pl.BlockSpec((SGU_BLOCK, HEADS), lambda i: (0, 0))],
                  out_specs=pl.BlockSpec((tb, D), lambda i: (i, 0)),
                  out_shape=jax.ShapeDtypeStruct((T, D), BF))(z, z, lg, lb, ws, bst)


def _mix_out(ya_pre, yb_pre, z, x, gate1, wba, wbb, wo, tm=256):
    T = x.shape[0]
    tm = min(tm, T)

    def body(yap_ref, ybp_ref, ga_ref, gb_ref, x_ref, g1_ref, wa_ref, wb_ref, wo_ref,
             x2_ref, mg_ref, ya_ref, yb_ref, o_ref):
        ya = _dot(yap_ref[...], wa_ref[...])
        yb = _dot(ybp_ref[...], wb_ref[...])
        merged = (_sigmoid(ga_ref[...]) * ya + _sigmoid(gb_ref[...]) * yb).astype(BF)
        o = _dot(merged, wo_ref[...])
        x2_ref[...] = x_ref[...] + g1_ref[...] * o
        mg_ref[...] = merged
        ya_ref[...] = ya.astype(BF)
        yb_ref[...] = yb.astype(BF)
        o_ref[...] = o.astype(BF)

    row = pl.BlockSpec((tm, D), lambda i: (i, 0))
    wspec = pl.BlockSpec((D, D), lambda i: (0, 0))
    return _pcall(body, name="mix_out", grid=(T // tm,),
                  in_specs=[row, row, pl.BlockSpec((tm, D), lambda i: (i, 4)), pl.BlockSpec((tm, D), lambda i: (i, 5)),
                            row, pl.BlockSpec((1, D), lambda i: (0, 0)), wspec, wspec, wspec],
                  out_specs=[row] * 5,
                  out_shape=[jax.ShapeDtypeStruct((T, D), F32)] + [jax.ShapeDtypeStruct((T, D), BF)] * 4,
                  vmem_mb=48)(ya_pre, yb_pre, z, z, x, gate1, wba, wbb, wo)


def _ffn_gate(up, cw, cb, tm=512, cw_blk=768):
    T = up.shape[0]
    tm = min(tm, T)
    dff = up.shape[1] // 2
    ncb = dff // cw_blk

    def body(ua_ref, uv_ref, wa_ref, wv_ref, ba_ref, bv_ref, f_ref, ga_ref, vd_ref, pa, pv):
        i = pl.program_id(1)

        @pl.when(i == 0)
        def _():
            pa[...] = jnp.zeros_like(pa)
            pv[...] = jnp.zeros_like(pv)

        def conv(u_ref, w_ref, b_ref, prev):
            u = u_ref[...]
            p = prev[...]
            hid = (b_ref[...] + w_ref[2:3, :] * u + w_ref[1:2, :] * _shift_down(u, p, 1)
                   + w_ref[0:1, :] * _shift_down(u, p, 2))
            prev[...] = u[tm - SUBLANES:tm]
            return hid

        act = conv(ua_ref, wa_ref, ba_ref, pa)
        val = conv(uv_ref, wv_ref, bv_ref, pv)
        ga, dga = _gelu_and_grad(act)
        f_ref[...] = (ga * val).astype(BF)
        ga_ref[...] = ga.astype(BF)
        vd_ref[...] = (val * dga).astype(BF)

    blk = pl.BlockSpec((tm, cw_blk), lambda cbk, i: (i, cbk))
    return _pcall(body, name="ffn_gate", grid=(ncb, T // tm),
                  in_specs=[pl.BlockSpec((tm, cw_blk), lambda cbk, i: (i, cbk)),
                            pl.BlockSpec((tm, cw_blk), lambda cbk, i: (i, ncb + cbk)),
                            pl.BlockSpec((3, cw_blk), lambda cbk, i: (0, cbk)),
                            pl.BlockSpec((3, cw_blk), lambda cbk, i: (0, ncb + cbk)),
                            pl.BlockSpec((1, cw_blk), lambda cbk, i: (0, cbk)),
                            pl.BlockSpec((1, cw_blk), lambda cbk, i: (0, ncb + cbk))],
                  out_specs=[blk] * 3, out_shape=[jax.ShapeDtypeStruct((T, dff), BF)] * 3,
                  scratch=[pltpu.VMEM((SUBLANES, cw_blk), F32)] * 2)(up, up, cw, cw, cb, cb)


def _ffn_down_loss(f, wd, x2, gate2, gf, target, tm=512):
    T = x2.shape[0]
    tm = min(tm, T)
    dff = f.shape[1]

    def body(f_ref, wd_ref, x2_ref, g2_ref, gf_ref, t_ref, loss_ref, dx3_ref, dfo_ref, dgf_ref, dg2_ref):
        i = pl.program_id(0)

        @pl.when(i == 0)
        def _():
            loss_ref[...] = jnp.zeros_like(loss_ref)
            dgf_ref[...] = jnp.zeros_like(dgf_ref)
            dg2_ref[...] = jnp.zeros_like(dg2_ref)

        fo = _dot(f_ref[...], wd_ref[...])
        x3 = x2_ref[...] + g2_ref[...] * fo
        rstd, xh = _rms_stats(x3)
        err = xh * gf_ref[...] - t_ref[...]
        loss_ref[...] += 0.5 * jnp.sum(jnp.mean(err * err, axis=-1, keepdims=True), axis=0, keepdims=True)
        dy = err * (1.0 / D)
        dgf_ref[...] += _colsum(dy * xh)
        dxh = dy * gf_ref[...]
        dx3 = rstd * (dxh - xh * jnp.mean(dxh * xh, axis=-1, keepdims=True))
        dg2_ref[...] += _colsum(dx3 * fo)
        dx3_ref[...] = dx3
        dfo_ref[...] = (g2_ref[...] * dx3).astype(BF)

    row = pl.BlockSpec((tm, D), lambda i: (i, 0))
    vec = pl.BlockSpec((1, D), lambda i: (0, 0))
    return _pcall(body, name="ffn_down_loss", grid=(T // tm,),
                  in_specs=[pl.BlockSpec((tm, dff), lambda i: (i, 0)), pl.BlockSpec((dff, D), lambda i: (0, 0)),
                            row, vec, vec, row],
                  out_specs=[pl.BlockSpec((1, LANES), lambda i: (0, 0)), row, row, vec, vec],
                  out_shape=[jax.ShapeDtypeStruct((1, LANES), F32), jax.ShapeDtypeStruct((T, D), F32),
                             jax.ShapeDtypeStruct((T, D), BF), jax.ShapeDtypeStruct((1, D), F32),
                             jax.ShapeDtypeStruct((1, D), F32)],
                  vmem_mb=48)(f, wd, x2, gate2, gf, target)


def _halo_spec(tm, cols, col_blk, nrow):
    per = tm // SUBLANES
    return pl.BlockSpec((SUBLANES, cols), lambda cbk, i: (jnp.maximum((nrow - 1 - i) * per - 1, 0), col_blk(cbk)))


def _ffn_bwd(dfo, wd, up, f, ga, vd, cw, tm=256, cw_blk=1536):
    T = up.shape[0]
    tm = min(tm, T)
    dff = up.shape[1] // 2
    ncb = dff // cw_blk
    nrow = T // tm

    def body(dfo_ref, wd_ref, ua_ref, uv_ref, ha_ref, hv_ref, f_ref, ga_ref, vd_ref, wa_ref, wv_ref,
             du_ref, dwd_ref, dwa_ref, dwv_ref, dba_ref, dbv_ref, na, nv):
        i = pl.program_id(1)
        first_block = i == nrow - 1

        @pl.when(i == 0)
        def _():
            na[...] = jnp.zeros_like(na)
            nv[...] = jnp.zeros_like(nv)
            dwd_ref[...] = jnp.zeros_like(dwd_ref)
            dwa_ref[...] = jnp.zeros_like(dwa_ref)
            dwv_ref[...] = jnp.zeros_like(dwv_ref)
            dba_ref[...] = jnp.zeros_like(dba_ref)
            dbv_ref[...] = jnp.zeros_like(dbv_ref)

        dfo_t = dfo_ref[...]
        dwd_ref[...] += _dot_tn(f_ref[...], dfo_t)
        df = _dot_nt(dfo_t, wd_ref[...])

        def conv_bwd(dh, u_ref, halo_ref, w_ref, nxt, col, dw_ref, db_ref):
            n8 = nxt[...]
            du = w_ref[2:3, :] * dh + w_ref[1:2, :] * _shift_up(dh, n8, 1) + w_ref[0:1, :] * _shift_up(dh, n8, 2)
            nxt[...] = dh[0:SUBLANES]
            du_ref[:, col:col + cw_blk] = du.astype(BF)
            u = u_ref[...]
            p = jnp.where(first_block, 0.0, halo_ref[...])
            dw_ref[2:3, :] += _colsum(dh * u)
            dw_ref[1:2, :] += _colsum(dh * _shift_down(u, p, 1))
            dw_ref[0:1, :] += _colsum(dh * _shift_down(u, p, 2))
            db_ref[...] += _colsum(dh)

        conv_bwd(df * vd_ref[...].astype(F32), ua_ref, ha_ref, wa_ref, na, 0, dwa_ref, dba_ref)
        conv_bwd(df * ga_ref[...].astype(F32), uv_ref, hv_ref, wv_ref, nv, cw_blk, dwv_ref, dbv_ref)

    rev = lambda cbk, i: (nrow - 1 - i, cbk)
    rev_v = lambda cbk, i: (nrow - 1 - i, ncb + cbk)
    blk = pl.BlockSpec((tm, cw_blk), rev)
    w3a = pl.BlockSpec((3, cw_blk), lambda cbk, i: (0, cbk))
    w3v = pl.BlockSpec((3, cw_blk), lambda cbk, i: (0, ncb + cbk))
    b1a = pl.BlockSpec((1, cw_blk), lambda cbk, i: (0, cbk))
    return _pcall(body, name="ffn_bwd", grid=(ncb, nrow),
                  in_specs=[pl.BlockSpec((tm, D), lambda cbk, i: (nrow - 1 - i, 0)),
                            pl.BlockSpec((cw_blk, D), lambda cbk, i: (cbk, 0)),
                            blk, pl.BlockSpec((tm, cw_blk), rev_v),
                            _halo_spec(tm, cw_blk, lambda cbk: cbk, nrow),
                            _halo_spec(tm, cw_blk, lambda cbk: ncb + cbk, nrow),
                            blk, blk, blk, w3a, w3v],
                  out_specs=[pl.BlockSpec((tm, 2 * cw_blk), rev),
                             pl.BlockSpec((cw_blk, D), lambda cbk, i: (cbk, 0)), w3a, w3a, b1a, b1a],
                  out_shape=[jax.ShapeDtypeStruct((T, 2 * dff), BF),
                             jax.ShapeDtypeStruct((dff, D), F32),
                             jax.ShapeDtypeStruct((3, dff), F32), jax.ShapeDtypeStruct((3, dff), F32),
                             jax.ShapeDtypeStruct((1, dff), F32), jax.ShapeDtypeStruct((1, dff), F32)],
                  scratch=[pltpu.VMEM((SUBLANES, cw_blk), F32)] * 2,
                  vmem_mb=56)(dfo, wd, up, up, up, up, f, ga, vd, cw, cw)


def _ffn_col_block(t, ncb):
    return jnp.where(t < ncb, 2 * t, 2 * (t - ncb) + 1)


def _mm_tn_cols(a, b, name, nshard, nb, colmap=None, tm=512):
    T, M = a.shape
    tm = min(tm, T)
    ns = b.shape[1] // nshard
    per = ns // nb
    cmap = colmap if colmap is not None else (lambda t: t)

    def body(a_ref, b_ref, o_ref):
        k = pl.program_id(1)

        @pl.when(k == 0)
        def _():
            o_ref[...] = jnp.zeros_like(o_ref)

        o_ref[0] += _dot_tn(a_ref[...], b_ref[...])

    return _pcall(body, name=name, grid=(nshard * per, T // tm),
                  in_specs=[pl.BlockSpec((tm, M), lambda t, k: (k, 0)),
                            pl.BlockSpec((tm, nb), lambda t, k: (k, cmap(t)))],
                  out_specs=pl.BlockSpec((1, M, nb), lambda t, k: (t // per, 0, t % per)),
                  out_shape=jax.ShapeDtypeStruct((nshard, M, ns), F32), vmem_mb=48)(a, b)


def _mm_nt_normbwd(dz, w4, x, resid, g, scale, name, gate=None, o=None, dz_blocks=(0, 1, 2, 3), tm=256):
    T = x.shape[0]
    tm = min(tm, T)
    ns = w4.shape[2]
    gated = gate is not None

    def body(*refs):
        if gated:
            (dz_ref, w_ref, x_ref, r_ref, g_ref, sc_ref, gt_ref, o_ref,
             dx_ref, dsh_ref, dsc_ref, dg_ref, do_ref, dgt_ref) = refs
        else:
            dz_ref, w_ref, x_ref, r_ref, g_ref, sc_ref, dx_ref, dsh_ref, dsc_ref, dg_ref = refs
        i = pl.program_id(0)

        @pl.when(i == 0)
        def _():
            dsh_ref[...] = jnp.zeros_like(dsh_ref)
            dsc_ref[...] = jnp.zeros_like(dsc_ref)
            dg_ref[...] = jnp.zeros_like(dg_ref)
            if gated:
                dgt_ref[...] = jnp.zeros_like(dgt_ref)

        dh = None
        for j in range(N_CHIPS):
            blk = dz_blocks[j]
            part = _dot_nt(dz_ref[:, blk * ns:(blk + 1) * ns], w_ref[j])
            dh = part if dh is None else dh + part
        rstd, xh = _rms_stats(x_ref[...])
        dsh_ref[...] += _colsum(dh)
        dsc_ref[...] += _colsum(dh * (xh * g_ref[...]))
        dn = dh * (1.0 + sc_ref[...])
        dg_ref[...] += _colsum(dn * xh)
        dxh = dn * g_ref[...]
        dx = r_ref[...] + rstd * (dxh - xh * jnp.mean(dxh * xh, axis=-1, keepdims=True))
        dx_ref[...] = dx
        if gated:
            do_ref[...] = (gt_ref[...] * dx).astype(BF)
            dgt_ref[...] += _colsum(dx * o_ref[...].astype(F32))

    row = pl.BlockSpec((tm, D), lambda i: (i, 0))
    vec = pl.BlockSpec((1, D), lambda i: (0, 0))
    in_specs = [pl.BlockSpec((tm, N_CHIPS * ns), lambda i: (i, 0)), _resident(w4.shape), row, row, vec, vec]
    out_specs = [row, vec, vec, vec]
    out_shape = [jax.ShapeDtypeStruct((T, D), F32)] + [jax.ShapeDtypeStruct((1, D), F32)] * 3
    args = [dz, w4, x, resid, g, scale]
    if gated:
        in_specs += [vec, row]
        out_specs += [row, vec]
        out_shape += [jax.ShapeDtypeStruct((T, D), BF), jax.ShapeDtypeStruct((1, D), F32)]
        args += [gate, o]
    return _pcall(body, name=name, grid=(T // tm,), in_specs=in_specs, out_specs=out_specs, out_shape=out_shape,
                  vmem_mb=48)(*args)


def _mix_bwd(do, ya, yb, z, wo, wba, wbb, tm=256):
    T = do.shape[0]
    tm = min(tm, T)

    def body(do_ref, ya_ref, yb_ref, ga_ref, gb_ref, wo_ref, wa_ref, wb_ref,
             dz_ref, dya_ref, dyb_ref, dyap_ref, dybp_ref):
        dm = _dot_nt(do_ref[...], wo_ref[...])
        sa = _sigmoid(ga_ref[...])
        sb = _sigmoid(gb_ref[...])
        dya = (sa * dm).astype(BF)
        dyb = (sb * dm).astype(BF)
        dz_ref[:, 0:D] = (dm * ya_ref[...].astype(F32) * sa * (1.0 - sa)).astype(BF)
        dz_ref[:, D:2 * D] = (dm * yb_ref[...].astype(F32) * sb * (1.0 - sb)).astype(BF)
        dya_ref[...] = dya
        dyb_ref[...] = dyb
        dyap_ref[...] = _dot_nt(dya, wa_ref[...]).astype(BF)
        dybp_ref[...] = _dot_nt(dyb, wb_ref[...]).astype(BF)

    row = pl.BlockSpec((tm, D), lambda i: (i, 0))
    wspec = pl.BlockSpec((D, D), lambda i: (0, 0))
    return _pcall(body, name="mix_bwd", grid=(T // tm,),
                  in_specs=[row, row, row, pl.BlockSpec((tm, D), lambda i: (i, 4)),
                            pl.BlockSpec((tm, D), lambda i: (i, 5)), wspec, wspec, wspec],
                  out_specs=[pl.BlockSpec((tm, 2 * D), lambda i: (i, 2)), row, row, row, row],
                  out_shape=[jax.ShapeDtypeStruct((T, 6 * D), BF)] + [jax.ShapeDtypeStruct((T, D), BF)] * 4,
                  vmem_mb=48)(do, ya, yb, z, z, wo, wba, wbb)


def _sgu_bwd(dz, dyb_pre, z, lg, lb, ws, bst, tb=256):
    T = z.shape[0]
    tb = min(tb, T)

    def body(dz_in, dy_ref, zu_ref, zv_ref, lg_ref, lb_ref, ws_ref, bst_ref,
             dz_ref, dws_ref, dbst_ref, dlg_ref, dlb_ref):
        del dz_in
        i = pl.program_id(0)

        @pl.when(i == 0)
        def _():
            dws_ref[...] = jnp.zeros_like(dws_ref)
            dbst_ref[...] = jnp.zeros_like(dbst_ref)
            dlg_ref[...] = jnp.zeros_like(dlg_ref)
            dlb_ref[...] = jnp.zeros_like(dlb_ref)

        gu, dgu = _gelu_and_grad(zu_ref[...])
        gv, dgv = _gelu_and_grad(zv_ref[...])
        rstd, xh = _layernorm_stats(gv)
        vln = xh * lg_ref[...] + lb_ref[...]
        wm, mixed = _sgu_mix(vln, ws_ref, bst_ref, tb)
        dy = dy_ref[...].astype(F32)
        dz_ref[:, 0:D] = (dy * mixed * dgu).astype(BF)
        dmixed = dy * gu
        ri = lax.broadcasted_iota(jnp.int32, (SGU_BLOCK, SGU_BLOCK), 0)
        ci = lax.broadcasted_iota(jnp.int32, (SGU_BLOCK, SGU_BLOCK), 1)
        blocks = []
        for blk in range(tb // SGU_BLOCK):
            rs = slice(blk * SGU_BLOCK, (blk + 1) * SGU_BLOCK)
            cols = []
            for g in range(HEADS):
                cs = slice(g * HD, (g + 1) * HD)
                dmg = dmixed[rs, cs]
                dmb = dmg.astype(BF)
                dbst_ref[:, g:g + 1] += jnp.sum(dmg, axis=1, keepdims=True)
                dws_ref[g] += jnp.where(ri >= ci, _dot_nt(dmb, vln[rs, cs].astype(BF)), 0.0)
                cols.append(_dot_tn(wm[g], dmb))
            blocks.append(jnp.concatenate(cols, axis=1))
        dvln = blocks[0] if len(blocks) == 1 else jnp.concatenate(blocks, axis=0)
        dlg_ref[...] += _colsum(dvln * xh)
        dlb_ref[...] += _colsum(dvln)
        dxh = dvln * lg_ref[...]
        dgv_in = rstd * (dxh - jnp.mean(dxh, axis=-1, keepdims=True)
                         - xh * jnp.mean(dxh * xh, axis=-1, keepdims=True))
        dz_ref[:, D:2 * D] = (dgv_in * dgv).astype(BF)

    row = pl.BlockSpec((tb, D), lambda i: (i, 0))
    vec = pl.BlockSpec((1, D), lambda i: (0, 0))
    wspec = pl.BlockSpec((HEADS, SGU_BLOCK, SGU_BLOCK), lambda i: (0, 0, 0))
    bspec = pl.BlockSpec((SGU_BLOCK, HEADS), lambda i: (0, 0))
    return _pcall(body, name="sgu_bwd", grid=(T // tb,),
                  in_specs=[HBM_SPEC, row, pl.BlockSpec((tb, D), lambda i: (i, 2)),
                            pl.BlockSpec((tb, D), lambda i: (i, 3)), vec, vec, wspec, bspec],
                  out_specs=[pl.BlockSpec((tb, 2 * D), lambda i: (i, 1)), wspec, bspec, vec, vec],
                  out_shape=[jax.ShapeDtypeStruct(dz.shape, BF),
                             jax.ShapeDtypeStruct((HEADS, SGU_BLOCK, SGU_BLOCK), F32),
                             jax.ShapeDtypeStruct((SGU_BLOCK, HEADS), F32),
                             jax.ShapeDtypeStruct((1, D), F32), jax.ShapeDtypeStruct((1, D), F32)],
                  aliases={0: 0}, vmem_mb=48)(dz, dyb_pre, z, z, lg, lb, ws, bst)


def _rglru_bwd(dz, dya_pre, z, h, cw, cb, wa, ba, wx, bx, lam, tb=256):
    T = z.shape[0]
    tb = min(tb, T)
    nrow = T // tb
    per = tb // SUBLANES

    def body(dz_in, dy_ref, xr_ref, xh_ref, gr_ref, h_ref, hh_ref, cw_ref, cb_ref, wa_ref, ba_ref, wx_ref, bx_ref,
             lam_ref, dz_ref, dcw_ref, dcb_ref, dwa_ref, dba_ref, dwx_ref, dbx_ref, dlam_ref, carry, nxt):
        del dz_in
        i = pl.program_id(0)
        first_block = i == nrow - 1

        @pl.when(i == 0)
        def _():
            carry[...] = jnp.zeros_like(carry)
            nxt[...] = jnp.zeros_like(nxt)
            for ref in (dcw_ref, dcb_ref, dwa_ref, dba_ref, dwx_ref, dbx_ref, dlam_ref):
                ref[...] = jnp.zeros_like(ref)

        xr = xr_ref[...]
        pv = jnp.where(first_block, 0.0, xh_ref[...])
        s1 = _shift_down(xr, pv, 1)
        s2 = _shift_down(xr, pv, 2)
        s3 = _shift_down(xr, pv, 3)
        xc = cb_ref[...] + cw_ref[3:4, :] * xr + cw_ref[2:3, :] * s1 + cw_ref[1:2, :] * s2 + cw_ref[0:1, :] * s3
        lam = lam_ref[...]
        r, ig, ls, a, mult = _lru_gates(xc, wa_ref, ba_ref[...], wx_ref, bx_ref[...], lam)
        hv = h_ref[...]
        hprev = _shift_down(hv, jnp.where(first_block, 0.0, hh_ref[...]), 1)
        gg, dgg = _gelu_and_grad(gr_ref[...])
        dy = dy_ref[...].astype(F32)
        dz_ref[:, D:2 * D] = (dy * hv * dgg).astype(BF)

        rows = lax.broadcasted_iota(jnp.int32, (tb, D), 0)
        v = dy * gg + jnp.where(rows == tb - 1, carry[0:1, :], 0.0)
        q = jnp.where(rows < tb - 1, pltpu.roll(a, tb - 1, 0), 0.0)
        _, gsc = _scan_rows(q, v, reverse=True)
        carry[...] = (a * gsc)[0:SUBLANES]

        xi = ig * xc
        dmult = gsc * xi
        dxi = gsc * mult
        dig = dxi * xc
        dxc = dxi * ig
        dlog_a = gsc * hprev * a - dmult * (a * a) / mult
        dlam_ref[...] += _colsum(dlog_a * r) * (LRU_C * _sigmoid(-lam))
        dpr = dlog_a * (LRU_C * ls) * r * (1.0 - r)
        dpi = dig * ig * (1.0 - ig)
        dba_ref[...] += _colsum(dpr)
        dbx_ref[...] += _colsum(dpi)
        back = []
        for hh in range(HEADS):
            cs = slice(hh * HD, (hh + 1) * HD)
            xh = xc[:, cs].astype(BF)
            dprh = dpr[:, cs].astype(BF)
            dpih = dpi[:, cs].astype(BF)
            dwa_ref[hh] += _dot_tn(xh, dprh)
            dwx_ref[hh] += _dot_tn(xh, dpih)
            back.append(_dot_nt(dprh, wa_ref[hh].astype(BF)) + _dot_nt(dpih, wx_ref[hh].astype(BF)))
        dxc = dxc + jnp.concatenate(back, axis=1)

        n8 = nxt[...]
        dxr = (cw_ref[3:4, :] * dxc + cw_ref[2:3, :] * _shift_up(dxc, n8, 1)
               + cw_ref[1:2, :] * _shift_up(dxc, n8, 2) + cw_ref[0:1, :] * _shift_up(dxc, n8, 3))
        nxt[...] = dxc[0:SUBLANES]
        dz_ref[:, 0:D] = dxr.astype(BF)
        dcw_ref[3:4, :] += _colsum(dxc * xr)
        dcw_ref[2:3, :] += _colsum(dxc * s1)
        dcw_ref[1:2, :] += _colsum(dxc * s2)
        dcw_ref[0:1, :] += _colsum(dxc * s3)
        dcb_ref[...] += _colsum(dxc)

    rev = lambda col: (lambda i: (nrow - 1 - i, col))
    halo = lambda col: pl.BlockSpec((SUBLANES, D), lambda i: (jnp.maximum((nrow - 1 - i) * per - 1, 0), col))
    vec = pl.BlockSpec((1, D), lambda i: (0, 0))
    wspec = pl.BlockSpec((HEADS, HD, HD), lambda i: (0, 0, 0))
    c4 = pl.BlockSpec((4, D), lambda i: (0, 0))
    wshape = jax.ShapeDtypeStruct((HEADS, HD, HD), F32)
    vshape = jax.ShapeDtypeStruct((1, D), F32)
    return _pcall(body, name="rglru_bwd", grid=(nrow,),
                  in_specs=[HBM_SPEC, pl.BlockSpec((tb, D), rev(0)), pl.BlockSpec((tb, D), rev(0)), halo(0),
                            pl.BlockSpec((tb, D), rev(1)), pl.BlockSpec((tb, D), rev(0)), halo(0),
                            c4, vec, wspec, vec, wspec, vec, vec],
                  out_specs=[pl.BlockSpec((tb, 2 * D), rev(0)), c4, vec, wspec, vec, wspec, vec, vec],
                  out_shape=[jax.ShapeDtypeStruct(dz.shape, BF), jax.ShapeDtypeStruct((4, D), F32), vshape,
                             wshape, vshape, wshape, vshape, vshape],
                  scratch=[pltpu.VMEM((SUBLANES, D), F32), pltpu.VMEM((SUBLANES, D), F32)],
                  aliases={0: 0}, vmem_mb=56)(dz, dya_pre, z, z, z, h, h, cw, cb, wa, ba, wx, bx, lam)


def _pack_rows(parts):
    out = []
    for p in parts:
        q = p.reshape(-1, LANES)
        pad = (-q.shape[0]) % SUBLANES
        if pad:
            q = jnp.concatenate([q, jnp.zeros((pad, LANES), q.dtype)], axis=0)
        out.append(q)
    return jnp.concatenate(out, axis=0)


def _rows_of(shape):
    n = 1
    for s in shape:
        n *= s
    rows = n // LANES
    return rows + (-rows) % SUBLANES


def kernel(x, c, w_ada, b_ada, norm_mix_g, w_in, rnn_conv_w, rnn_conv_b, lru_w_a, lru_b_a, lru_w_x, lru_b_x, lru_lambda, sgu_ln_g, sgu_ln_b, sgu_w_s, sgu_b_s, w_branch_a, w_branch_b, w_out, norm_ffn_g, w_up, ffn_conv_w, ffn_conv_b, w_down, norm_final_g, loss_target, m_w_ada, m_b_ada, m_norm_mix_g, m_w_in, m_rnn_conv_w, m_rnn_conv_b, m_lru_w_a, m_lru_b_a, m_lru_w_x, m_lru_b_x, m_lru_lambda, m_sgu_ln_g, m_sgu_ln_b, m_sgu_w_s, m_sgu_b_s, m_w_branch_a, m_w_branch_b, m_w_out, m_norm_ffn_g, m_w_up, m_ffn_conv_w, m_ffn_conv_b, m_w_down, m_norm_final_g, v_w_ada, v_b_ada, v_norm_mix_g, v_w_in, v_rnn_conv_w, v_rnn_conv_b, v_lru_w_a, v_lru_b_a, v_lru_w_x, v_lru_b_x, v_lru_lambda, v_sgu_ln_g, v_sgu_ln_b, v_sgu_w_s, v_sgu_b_s, v_w_branch_a, v_w_branch_b, v_w_out, v_norm_ffn_g, v_w_up, v_ffn_conv_w, v_ffn_conv_b, v_w_down, v_norm_final_g):
    args = dict(locals())
    T = x.shape[1]
    mx, my, mc = lax.axis_index("x"), lax.axis_index("y"), lax.axis_index("c")
    chip = 2 * mx + my
    dev = 2 * chip + mc
    vec = lambda a: a.reshape(1, -1)

    xt = x.reshape(T, D)
    tgt = loss_target.reshape(T, D)
    ns = w_in.shape[2]
    dff = w_down.shape[1] * N_CHIPS

    c_all = _gather8(c.reshape(SUBLANES, LANES), "gather_c").reshape(N_DEV, D)
    b_ada_sh = lax.dynamic_slice(b_ada, (0, chip * ns), (1, ns))
    mod_sh = _mod_fwd(c_all, w_ada[0], b_ada_sh)

    w_in_b, w_up_b, w_down_b, wba_b, wbb_b, wo_b = _cast_shards(
        [w_in[0], w_up[0], w_down[0], w_branch_a[0], w_branch_b[0], w_out[0]])
    (w_in4, w_up4, w_down4, wba4, wbb4, wo4, rcw4, fcw4, mod4) = _gather_weights(
        [w_in_b, w_up_b, w_down_b, wba_b, wbb_b, wo_b, rnn_conv_w[0], ffn_conv_w[0], mod_sh],
        [True] * 6 + [False] * 3)
    wd_full = w_down4.reshape(dff, D)
    wba_full = wba4.reshape(D, D)
    wbb_full = wbb4.reshape(D, D)
    wo_full = wo4.reshape(D, D)
    rcw_full = jnp.transpose(rcw4, (1, 0, 2)).reshape(4, D)
    fcw_full = jnp.transpose(fcw4, (1, 0, 2)).reshape(3, 2 * dff)
    mod = lax.dynamic_index_in_dim(mod4, dev, axis=1, keepdims=False).reshape(1, 6 * D)
    shift1, scale1, gate1, shift2, scale2, gate2 = [mod[:, k * D:(k + 1) * D] for k in range(6)]

    h1, z = _modnorm_matmul(xt, norm_mix_g, scale1, shift1, w_in4, "norm_in_proj")
    bst = jnp.transpose(sgu_b_s[0])
    h_lru, ya_pre = _rglru_fwd(z, rcw_full, rnn_conv_b, lru_w_a[0], lru_b_a, lru_w_x[0], lru_b_x, lru_lambda)
    yb_pre = _sgu_fwd(z, sgu_ln_g, sgu_ln_b, sgu_w_s[0], bst)
    x2, merged, ya, yb, o1 = _mix_out(ya_pre, yb_pre, z, xt, gate1, wba_full, wbb_full, wo_full)
    h2, up = _modnorm_matmul(x2, norm_ffn_g, scale2, shift2, w_up4, "norm_up_proj")
    f, ffn_ga, ffn_vd = _ffn_gate(up, fcw_full, ffn_conv_b)
    loss_part, dx3, dfo, dgf, dgate2 = _ffn_down_loss(f, wd_full, x2, gate2, vec(norm_final_g), tgt)

    ffn_map = functools.partial(_ffn_col_block, ncb=2)
    dup, dwd, dfcw_a, dfcw_v, dfcb_a, dfcb_v = _ffn_bwd(dfo, wd_full, up, f, ffn_ga, ffn_vd, fcw_full, cw_blk=ns)
    dw_up4 = _mm_tn_cols(h2, dup, "dw_up", N_CHIPS, ns, colmap=ffn_map)
    dx2, dshift2, dscale2, dg_ffn, do1, dgate1 = _mm_nt_normbwd(
        dup, w_up4, x2, dx3, norm_ffn_g, scale2, "dh2_norm_bwd", gate=gate1, o=o1, dz_blocks=(0, 2, 1, 3))
    dz, dya, dyb, dya_pre, dyb_pre = _mix_bwd(do1, ya, yb, z, wo_full, wba_full, wbb_full)
    dwo = _mm_tn_cols(merged, do1, "dw_out", 1, D)
    dwba = _mm_tn_cols(ya_pre, dya, "dw_branch_a", 1, D)
    dwbb = _mm_tn_cols(yb_pre, dyb, "dw_branch_b", 1, D)
    dz, dws, dbst, dlg, dlb = _sgu_bwd(dz, dyb_pre, z, sgu_ln_g, sgu_ln_b, sgu_w_s[0], bst)
    dz, drcw, drcb, dwa, dba, dwx, dbx, dlam = _rglru_bwd(
        dz, dya_pre, z, h_lru, rcw_full, rnn_conv_b, lru_w_a[0], lru_b_a, lru_w_x[0], lru_b_x, lru_lambda)
    dw_in4 = _mm_tn_cols(h1, dz, "dw_in", N_CHIPS, ns)
    grad_x, dshift1, dscale1, dg_mix = _mm_nt_normbwd(dz, w_in4, xt, dx2, norm_mix_g, scale1, "dh1_norm_bwd")
    dmod = jnp.concatenate([dshift1, dscale1, dgate1, dshift2, dscale2, dgate2], axis=1)

    big = [("w_in", dw_in4), ("w_up", dw_up4), ("w_down", dwd.reshape(N_CHIPS, dff // N_CHIPS, D)),
           ("w_branch_a", dwba.reshape(N_CHIPS, D // N_CHIPS, D)), ("w_branch_b", dwbb.reshape(N_CHIPS, D // N_CHIPS, D)),
           ("w_out", dwo.reshape(N_CHIPS, D // N_CHIPS, D))]
    core = mc.astype(jnp.int32).reshape(1)
    from_core = _send_other_half([g for _, g in big])
    chip_sums = [_add_halves_bf16(core, g, o, "sum_cores_" + n) for (n, g), o in zip(big, from_core)]
    parts = _exchange_xy(chip_sums, True, "scatter_grads")
    totals = [_sum_parts(p, "sum_chips_" + n) for (n, _), p in zip(big, parts)]
    fulls = _share_halves(totals)
    out = {}
    for (n, _), full in zip(big, fulls):
        shape = args[n].shape
        res = _adamw(args[n][0], args["m_" + n][0], args["v_" + n][0], [full.reshape(shape[1:])], "adamw_" + n)
        for kind, r in zip(("grad_", "delta_", "new_m_", "new_v_"), res):
            out[kind + n] = r.reshape(shape)

    small = [("b_ada", dmod), ("norm_mix_g", dg_mix), ("rnn_conv_b", drcb), ("lru_w_a", dwa), ("lru_b_a", dba),
             ("lru_w_x", dwx), ("lru_b_x", dbx), ("lru_lambda", dlam), ("sgu_ln_g", dlg), ("sgu_ln_b", dlb),
             ("sgu_w_s", dws), ("sgu_b_s", jnp.transpose(dbst)), ("norm_ffn_g", dg_ffn),
             ("ffn_conv_b", jnp.concatenate([dfcb_a, dfcb_v], axis=1)), ("norm_final_g", dgf)]
    dfcw = jnp.concatenate([dfcw_a, dfcw_v], axis=1)
    r_small = sum(_rows_of(args[n].shape) for n, _ in small)
    r_pad = r_small + (-r_small) % 256
    fill = jnp.zeros((r_pad - r_small, LANES), F32)
    g_pack = jnp.concatenate([_pack_rows([g for _, g in small]), fill, _pack_rows([drcw, dfcw])], axis=0)
    g_all = _gather8(g_pack, "gather_small_grads")
    g_sum = _sum_parts(g_all, "sum_small_grads")

    def pack_small(prefix):
        return jnp.concatenate([_pack_rows([args[prefix + n] for n, _ in small]), fill], axis=0)

    res = _adamw(pack_small(""), pack_small("m_"), pack_small("v_"), [g_sum[:r_pad]], "adamw_small")
    off = 0
    for n, _ in small:
        shape = args[n].shape
        rows = _rows_of(shape)
        for kind, r in zip(("grad_", "delta_", "new_m_", "new_v_"), res):
            out[kind + n] = r[off:off + rows].reshape(shape)
        off += rows

    rcw_cols = rnn_conv_w.shape[2]
    g_rcw = lax.dynamic_slice(g_sum[r_pad:r_pad + 32].reshape(4, D), (0, chip * rcw_cols), (4, rcw_cols))
    g_fcw = lax.dynamic_slice(g_sum[r_pad + 32:r_pad + 32 + 144].reshape(3, 2 * dff), (0, chip * ns), (3, ns))
    conv = [("rnn_conv_w", g_rcw), ("ffn_conv_w", g_fcw)]
    res = _adamw(_pack_rows([args[n] for n, _ in conv]), _pack_rows([args["m_" + n] for n, _ in conv]),
                 _pack_rows([args["v_" + n] for n, _ in conv]), [_pack_rows([g for _, g in conv])], "adamw_conv")
    off = 0
    for n, _ in conv:
        shape = args[n].shape
        cnt = shape[1] * shape[2] // LANES
        for kind, r in zip(("grad_", "delta_", "new_m_", "new_v_"), res):
            out[kind + n] = r[off:off + cnt].reshape(shape)
        off += _rows_of(shape)

    dmod_all = g_all[:, 0:6 * D // LANES, :].reshape(N_DEV, 6 * D)
    dmod_sh = lax.dynamic_slice(dmod_all, (0, chip * ns), (N_DEV, ns))
    res = _ada_adamw(jnp.transpose(c_all), dmod_sh, w_ada[0], m_w_ada[0], v_w_ada[0])
    for kind, r in zip(("grad_", "delta_", "new_m_", "new_v_"), res):
        out[kind + "w_ada"] = r.reshape(w_ada.shape)

    loss = lax.psum(loss_part[0, 0], ("x", "y", "c"))
    names = ["w_ada", "b_ada", "norm_mix_g", "w_in", "rnn_conv_w", "rnn_conv_b", "lru_w_a", "lru_b_a", "lru_w_x",
             "lru_b_x", "lru_lambda", "sgu_ln_g", "sgu_ln_b", "sgu_w_s", "sgu_b_s", "w_branch_a", "w_branch_b",
             "w_out", "norm_ffn_g", "w_up", "ffn_conv_w", "ffn_conv_b", "w_down", "norm_final_g"]
    result = [loss, grad_x.reshape(x.shape)]
    for kind in ("grad_", "delta_", "new_m_", "new_v_"):
        result += [out[kind + n] for n in names]
    return tuple(result)
```

```python
import functools

import jax
import jax.numpy as jnp
from jax import lax
from jax.experimental import pallas as pl
from jax.experimental.pallas import tpu as pltpu

F32 = jnp.float32
BF = jnp.bfloat16

D = 1024
HEADS = 8
HD = D // HEADS
SGU_BLOCK = 128
N_CHIPS = 4
N_DEV = 8
EPS = 1e-6
LRU_C = 8.0
LANES = 128
SUBLANES = 8

ADAM_LR = 0.001
ADAM_B1 = 0.9
ADAM_B2 = 0.999
ADAM_EPS = 1e-08
ADAM_WD = 0.01
ADAM_STEP = 10

GELU_K0 = 0.7978845608028654
GELU_K1 = 0.044715

HBM_SPEC = pl.BlockSpec(memory_space=pltpu.HBM)
MESH_ID = pl.DeviceIdType.MESH


def _pcall(body, *, name, out_shape, grid=(), in_specs=None, out_specs=None, scratch=(), vmem_mb=32, aliases=None,
           grid_spec=None):
    kw = {}
    if aliases:
        kw["input_output_aliases"] = aliases
    if grid_spec is not None:
        kw["grid_spec"] = grid_spec
        ndim = len(grid_spec.grid)
    else:
        kw.update(grid=grid, in_specs=in_specs, out_specs=out_specs, scratch_shapes=list(scratch))
        ndim = len(grid)
    if ndim:
        params = pltpu.CompilerParams(dimension_semantics=("arbitrary",) * ndim, vmem_limit_bytes=vmem_mb * 2 ** 20)
    else:
        params = pltpu.CompilerParams(vmem_limit_bytes=vmem_mb * 2 ** 20)
    return pl.pallas_call(body, name=name, out_shape=out_shape, compiler_params=params, **kw)


def _gelu(x):
    return 0.5 * x * (1.0 + jnp.tanh(GELU_K0 * (x + GELU_K1 * x * x * x)))


def _gelu_and_grad(x):
    x2 = x * x
    t = jnp.tanh(GELU_K0 * x * (1.0 + GELU_K1 * x2))
    g = 0.5 * x * (1.0 + t)
    dg = 0.5 * (1.0 + t) + 0.5 * x * (1.0 - t * t) * (GELU_K0 * (1.0 + 3.0 * GELU_K1 * x2))
    return g, dg


def _sigmoid(x):
    return 1.0 / (1.0 + jnp.exp(-x))


def _sigmoid_t(x):
    return 0.5 * jnp.tanh(0.5 * x) + 0.5


def _log_sigmoid(x):
    e = jnp.exp(-jnp.abs(x))
    u = 1.0 + e
    d = u - 1.0
    l1p = jnp.where(d == 0.0, e, jnp.log(u) * (e / jnp.where(d == 0.0, 1.0, d)))
    return jnp.minimum(x, 0.0) - l1p


def _dot(a, b):
    return jnp.dot(a, b, preferred_element_type=F32)


def _dot_nt(a, b):
    return lax.dot_general(a, b, (((1,), (1,)), ((), ())), preferred_element_type=F32)


def _dot_tn(a, b):
    return lax.dot_general(a, b, (((0,), (0,)), ((), ())), preferred_element_type=F32)


def _shift_down(x, halo, s):
    r = pltpu.roll(x, s, 0)
    rows = lax.broadcasted_iota(jnp.int32, (SUBLANES, x.shape[1]), 0)
    head = jnp.where(rows < s, pltpu.roll(halo, s, 0), r[0:SUBLANES])
    return jnp.concatenate([head, r[SUBLANES:]], axis=0)


def _shift_up(x, halo, s):
    n = x.shape[0]
    r = pltpu.roll(x, n - s, 0)
    rows = lax.broadcasted_iota(jnp.int32, (SUBLANES, x.shape[1]), 0)
    tail = jnp.where(rows >= SUBLANES - s, pltpu.roll(halo, SUBLANES - s, 0), r[n - SUBLANES:n])
    return jnp.concatenate([r[:n - SUBLANES], tail], axis=0)


def _scan_rows(a, u, reverse):
    n, width = a.shape
    rows = lax.broadcasted_iota(jnp.int32, (n, width), 0)
    d = 1
    while d < n:
        if d < SUBLANES:
            keep = rows < n - d if reverse else rows >= d
            shift = n - d if reverse else d
            a_s = jnp.where(keep, pltpu.roll(a, shift, 0), 1.0)
            u_s = jnp.where(keep, pltpu.roll(u, shift, 0), 0.0)
        elif reverse:
            a_s = jnp.concatenate([a[d:], jnp.ones((d, width), a.dtype)], axis=0)
            u_s = jnp.concatenate([u[d:], jnp.zeros((d, width), u.dtype)], axis=0)
        else:
            a_s = jnp.concatenate([jnp.ones((d, width), a.dtype), a[:n - d]], axis=0)
            u_s = jnp.concatenate([jnp.zeros((d, width), u.dtype), u[:n - d]], axis=0)
        u = a * u_s + u
        a = a * a_s
        d *= 2
    return a, u


def _colsum(x):
    return jnp.sum(x, axis=0, keepdims=True)


def _rms_stats(x):
    r = lax.rsqrt(jnp.mean(x * x, axis=-1, keepdims=True) + EPS)
    return r, x * r


def _lru_gates(xc, wa_ref, ba, wx_ref, bx, lam):
    pr, pi = [], []
    for hh in range(HEADS):
        xh = xc[:, hh * HD:(hh + 1) * HD].astype(BF)
        pr.append(_dot(xh, wa_ref[hh].astype(BF)))
        pi.append(_dot(xh, wx_ref[hh].astype(BF)))
    r = _sigmoid_t(jnp.concatenate(pr, axis=1) + ba)
    ig = _sigmoid_t(jnp.concatenate(pi, axis=1) + bx)
    ls = _log_sigmoid(lam)
    log_a = LRU_C * r * ls
    a = jnp.exp(log_a)
    x2 = 2.0 * log_a
    u = a * a
    lu = jnp.log(jnp.maximum(u, 1e-37))
    em1 = jnp.where(lu == 0.0, x2, jnp.where(u < 1e-30, -1.0, (u - 1.0) * x2 / jnp.where(lu == 0.0, 1.0, lu)))
    mult = jnp.sqrt(-em1)
    return r, ig, ls, a, mult


def _sgu_mix(vln, ws_ref, bst_ref, tb):
    ri = lax.broadcasted_iota(jnp.int32, (SGU_BLOCK, SGU_BLOCK), 0)
    ci = lax.broadcasted_iota(jnp.int32, (SGU_BLOCK, SGU_BLOCK), 1)
    wm = [jnp.where(ri >= ci, ws_ref[g], 0.0).astype(BF) for g in range(HEADS)]
    blocks = []
    for blk in range(tb // SGU_BLOCK):
        cols = []
        for g in range(HEADS):
            vb = vln[blk * SGU_BLOCK:(blk + 1) * SGU_BLOCK, g * HD:(g + 1) * HD].astype(BF)
            cols.append(_dot(wm[g], vb) + bst_ref[:, g:g + 1])
        blocks.append(jnp.concatenate(cols, axis=1))
    mixed = blocks[0] if len(blocks) == 1 else jnp.concatenate(blocks, axis=0)
    return wm, mixed


def _layernorm_stats(v):
    mu = jnp.mean(v, axis=-1, keepdims=True)
    vc = v - mu
    rstd = lax.rsqrt(jnp.mean(vc * vc, axis=-1, keepdims=True) + EPS)
    return rstd, vc * rstd


def _my_xyc():
    return lax.axis_index("x"), lax.axis_index("y"), lax.axis_index("c")


def _exchange_xy(srcs, scatter, name):
    n = len(srcs)
    out_shape = []
    for s in srcs:
        shard = s.shape[1:] if scatter else s.shape
        out_shape.append(jax.ShapeDtypeStruct((N_CHIPS,) + tuple(shard), s.dtype))

    def body(*refs):
        src, out = refs[:n], refs[n:2 * n]
        send_sems, recv_sems, loc_sems = refs[2 * n:]
        x, y, c = _my_xyc()
        me = 2 * x + y
        remote, local = [], []
        for a in range(n):
            def piece(p, a=a):
                return src[a].at[p] if scatter else src[a]
            lc = pltpu.make_async_copy(piece(me), out[a].at[me], loc_sems.at[a])
            lc.start()
            local.append(lc)
            for k, (fx, fy) in enumerate(((1, 0), (0, 1), (1, 1))):
                px = 1 - x if fx else x
                py = 1 - y if fy else y
                cp = pltpu.make_async_remote_copy(
                    src_ref=piece(2 * px + py), dst_ref=out[a].at[me],
                    send_sem=send_sems.at[a, k], recv_sem=recv_sems.at[a, k],
                    device_id=(px, py, c), device_id_type=MESH_ID)
                cp.start()
                remote.append(cp)
        for cp in remote:
            cp.wait()
        for lc in local:
            lc.wait()

    return _pcall(body, name=name, out_shape=out_shape, in_specs=[HBM_SPEC] * n, out_specs=[HBM_SPEC] * n,
                  scratch=[pltpu.SemaphoreType.DMA((n, 3)), pltpu.SemaphoreType.DMA((n, 3)),
                           pltpu.SemaphoreType.DMA((n,))])(*srcs)


def _gather_weights(srcs, halve):
    n = len(srcs)
    out_shape = [jax.ShapeDtypeStruct((N_CHIPS,) + s.shape, s.dtype) for s in srcs]

    def body(*refs):
        src, out = refs[:n], refs[n:2 * n]
        send_sems, recv_sems, fwd_send, fwd_recv, loc_sems = refs[2 * n:]
        x, y, c = _my_xyc()
        me = 2 * x + y
        chips = [(1 - x, y), (x, 1 - y), (1 - x, 1 - y)]

        def half(ref, a, which):
            if not halve[a]:
                return ref
            h = srcs[a].shape[0] // 2
            return ref.at[pl.ds(which * h, h)]

        def ici(a, k, frm):
            px, py = chips[k]
            return pltpu.make_async_remote_copy(
                src_ref=half(src[a], a, c), dst_ref=half(out[a].at[frm], a, c),
                send_sem=send_sems.at[a, k], recv_sem=recv_sems.at[a, k],
                device_id=(px, py, c), device_id_type=MESH_ID)

        def d2d(a, k, which):
            px, py = chips[k]
            rows = half(out[a].at[2 * px + py], a, which)
            return pltpu.make_async_remote_copy(
                src_ref=rows, dst_ref=rows, send_sem=fwd_send.at[a, k], recv_sem=fwd_recv.at[a, k],
                device_id=(x, y, 1 - c), device_id_type=MESH_ID)

        local, sends = [], []
        for a in range(n):
            lc = pltpu.make_async_copy(src[a], out[a].at[me], loc_sems.at[a])
            lc.start()
            local.append(lc)
            for k in range(3):
                cp = ici(a, k, me)
                cp.start()
                sends.append(cp)
        for a in range(n):
            for k in range(3):
                px, py = chips[k]
                ici(a, k, 2 * px + py).wait_recv()
                if halve[a]:
                    fw = d2d(a, k, c)
                    fw.start()
                    sends.append(fw)
        for a in range(n):
            if halve[a]:
                for k in range(3):
                    d2d(a, k, 1 - c).wait_recv()
        for cp in sends:
            cp.wait_send()
        for lc in local:
            lc.wait()

    sem = pltpu.SemaphoreType.DMA((n, 3))
    return _pcall(body, name="gather_weights", out_shape=out_shape, in_specs=[HBM_SPEC] * n,
                  out_specs=[HBM_SPEC] * n, scratch=[sem, sem, sem, sem, pltpu.SemaphoreType.DMA((n,))])(*srcs)


def _send_other_half(grads):
    n = len(grads)
    out_shape = [jax.ShapeDtypeStruct((N_CHIPS, g.shape[1] // 2, g.shape[2]), g.dtype) for g in grads]

    def body(*refs):
        src, out = refs[:n], refs[n:2 * n]
        send_sems, recv_sems = refs[2 * n:]
        x, y, c = _my_xyc()
        cps = []
        for a in range(n):
            h = grads[a].shape[1] // 2
            cp = pltpu.make_async_remote_copy(
                src_ref=src[a].at[:, pl.ds((1 - c) * h, h), :], dst_ref=out[a],
                send_sem=send_sems.at[a], recv_sem=recv_sems.at[a],
                device_id=(x, y, 1 - c), device_id_type=MESH_ID)
            cp.start()
            cps.append(cp)
        for cp in cps:
            cp.wait()

    return _pcall(body, name="send_other_half", out_shape=out_shape, in_specs=[HBM_SPEC] * n,
                  out_specs=[HBM_SPEC] * n,
                  scratch=[pltpu.SemaphoreType.DMA((n,)), pltpu.SemaphoreType.DMA((n,))])(*grads)


def _share_halves(totals):
    n = len(totals)
    out_shape = [jax.ShapeDtypeStruct((2,) + t.shape, t.dtype) for t in totals]

    def body(*refs):
        src, out = refs[:n], refs[n:2 * n]
        send_sems, recv_sems, loc_sems = refs[2 * n:]
        x, y, c = _my_xyc()
        cps, local = [], []
        for a in range(n):
            lc = pltpu.make_async_copy(src[a], out[a].at[c], loc_sems.at[a])
            lc.start()
            local.append(lc)
            cp = pltpu.make_async_remote_copy(
                src_ref=src[a], dst_ref=out[a].at[c], send_sem=send_sems.at[a], recv_sem=recv_sems.at[a],
                device_id=(x, y, 1 - c), device_id_type=MESH_ID)
            cp.start()
            cps.append(cp)
        for cp in cps:
            cp.wait()
        for lc in local:
            lc.wait()

    sem = pltpu.SemaphoreType.DMA((n,))
    return _pcall(body, name="share_halves", out_shape=out_shape, in_specs=[HBM_SPEC] * n,
                  out_specs=[HBM_SPEC] * n, scratch=[sem, sem, sem])(*totals)


def _gather8(src, name):
    def body(src_ref, out_ref, send_sems, recv_sems, loc_sem):
        x, y, c = _my_xyc()
        me = 4 * x + 2 * y + c
        lc = pltpu.make_async_copy(src_ref, out_ref.at[me], loc_sem)
        lc.start()
        cps = []
        for k in range(1, N_DEV):
            px = 1 - x if (k >> 2) & 1 else x
            py = 1 - y if (k >> 1) & 1 else y
            pc = 1 - c if k & 1 else c
            cp = pltpu.make_async_remote_copy(
                src_ref=src_ref, dst_ref=out_ref.at[me], send_sem=send_sems.at[k - 1], recv_sem=recv_sems.at[k - 1],
                device_id=(px, py, pc), device_id_type=MESH_ID)
            cp.start()
            cps.append(cp)
        for cp in cps:
            cp.wait()
        lc.wait()

    return _pcall(body, name=name, out_shape=jax.ShapeDtypeStruct((N_DEV,) + src.shape, src.dtype),
                  in_specs=[HBM_SPEC], out_specs=HBM_SPEC,
                  scratch=[pltpu.SemaphoreType.DMA((N_DEV - 1,)), pltpu.SemaphoreType.DMA((N_DEV - 1,)),
                           pltpu.SemaphoreType.DMA])(src)


def _cast_shards(arrs):
    n = len(arrs)

    def body(*refs):
        for a in range(n):
            refs[n + a][...] = refs[a][...].astype(BF)

    specs = [pl.BlockSpec((s.shape[0] // 4, s.shape[1]), lambda i: (i, 0)) for s in arrs]
    return _pcall(body, name="cast_shards", grid=(4,), in_specs=specs, out_specs=specs,
                  out_shape=[jax.ShapeDtypeStruct(s.shape, BF) for s in arrs])(*arrs)


def _row_tile(rows, cols):
    t = rows
    while t * cols * 4 > (3 << 19) and t % 16 == 0:
        t //= 2
    return t


def _sum_parts(parts, name):
    p, rows, cols = parts.shape
    tr = _row_tile(rows, cols * p // 2)

    def body(p_ref, o_ref):
        acc = p_ref[0].astype(F32)
        for k in range(1, p):
            acc = acc + p_ref[k].astype(F32)
        o_ref[...] = acc

    return _pcall(body, name=name, grid=(rows // tr,),
                  in_specs=[pl.BlockSpec((p, tr, cols), lambda i: (0, i, 0))],
                  out_specs=pl.BlockSpec((tr, cols), lambda i: (i, 0)),
                  out_shape=jax.ShapeDtypeStruct((rows, cols), F32), vmem_mb=48)(parts)


def _add_halves_bf16(core, grad, other, name):
    nchip, rows, cols = grad.shape
    h = rows // 2
    tr = _row_tile(h, cols)
    nh = h // tr

    def body(c_ref, g_ref, o_ref, s_ref):
        del c_ref
        s_ref[...] = (g_ref[...] + o_ref[...]).astype(BF)

    grid_spec = pltpu.PrefetchScalarGridSpec(
        num_scalar_prefetch=1, grid=(nchip, nh),
        in_specs=[pl.BlockSpec((1, tr, cols), lambda p, i, c_ref: (p, c_ref[0] * nh + i, 0)),
                  pl.BlockSpec((1, tr, cols), lambda p, i, c_ref: (p, i, 0))],
        out_specs=pl.BlockSpec((1, tr, cols), lambda p, i, c_ref: (p, i, 0)))
    return _pcall(body, name=name, grid_spec=grid_spec, out_shape=jax.ShapeDtypeStruct((nchip, h, cols), BF),
                  vmem_mb=48)(core, grad, other)


def _adamw_math(w, g, m, v):
    m2 = ADAM_B1 * m + (1.0 - ADAM_B1) * g
    v2 = ADAM_B2 * v + (1.0 - ADAM_B2) * (g * g)
    m_hat = m2 / (1.0 - ADAM_B1 ** ADAM_STEP)
    v_hat = v2 / (1.0 - ADAM_B2 ** ADAM_STEP)
    delta = -ADAM_LR * (m_hat / (jnp.sqrt(v_hat) + ADAM_EPS) + ADAM_WD * w)
    return delta, m2, v2


def _adamw(w, m, v, grads, name):
    rows, cols = w.shape
    tr = _row_tile(rows, cols)
    ng = len(grads)

    def body(*refs):
        w_ref, m_ref, v_ref = refs[:3]
        g = refs[3][...]
        for k in range(1, ng):
            g = g + refs[3 + k][...]
        g_ref, d_ref, m2_ref, v2_ref = refs[3 + ng:]
        delta, m2, v2 = _adamw_math(w_ref[...], g, m_ref[...], v_ref[...])
        g_ref[...] = g
        d_ref[...] = delta
        m2_ref[...] = m2
        v2_ref[...] = v2

    spec = pl.BlockSpec((tr, cols), lambda i: (i, 0))
    return _pcall(body, name=name, grid=(rows // tr,), in_specs=[spec] * (3 + ng), out_specs=[spec] * 4,
                  out_shape=[jax.ShapeDtypeStruct((rows, cols), F32)] * 4, vmem_mb=48)(w, m, v, *grads)


def _ada_adamw(ct, dmod, w, m, v):
    rows, cols = w.shape
    tr = _row_tile(rows, cols)

    def body(ct_ref, dm_ref, w_ref, m_ref, v_ref, g_ref, d_ref, m2_ref, v2_ref):
        cv = ct_ref[...]
        ca = cv * _sigmoid(cv)
        g = ca[:, 0:1] * dm_ref[0:1, :]
        for b in range(1, N_DEV):
            g = g + ca[:, b:b + 1] * dm_ref[b:b + 1, :]
        delta, m2, v2 = _adamw_math(w_ref[...], g, m_ref[...], v_ref[...])
        g_ref[...] = g
        d_ref[...] = delta
        m2_ref[...] = m2
        v2_ref[...] = v2

    spec = pl.BlockSpec((tr, cols), lambda i: (i, 0))
    return _pcall(body, name="ada_adamw", grid=(rows // tr,),
                  in_specs=[pl.BlockSpec((tr, N_DEV), lambda i: (i, 0)), pl.BlockSpec((N_DEV, cols), lambda i: (0, 0)),
                            spec, spec, spec],
                  out_specs=[spec] * 4, out_shape=[jax.ShapeDtypeStruct((rows, cols), F32)] * 4,
                  vmem_mb=48)(ct, dmod, w, m, v)


def _mod_fwd(c_all, w, b):
    cols = w.shape[1]
    tn = cols // 3

    def body(c_ref, w_ref, b_ref, o_ref):
        cv = c_ref[...]
        ca = (cv * _sigmoid(cv)).astype(BF)
        o_ref[...] = _dot(ca, w_ref[...].astype(BF)) + b_ref[...]

    return _pcall(body, name="mod_fwd", grid=(3,),
                  in_specs=[pl.BlockSpec((N_DEV, D), lambda j: (0, 0)), pl.BlockSpec((D, tn), lambda j: (0, j)),
                            pl.BlockSpec((1, tn), lambda j: (0, j))],
                  out_specs=pl.BlockSpec((N_DEV, tn), lambda j: (0, j)),
                  out_shape=jax.ShapeDtypeStruct((N_DEV, cols), F32))(c_all, w, b)


def _resident(shape):
    zeros = (0,) * len(shape)
    return pl.BlockSpec(shape, lambda *_: zeros, pipeline_mode=pl.Buffered(1))


def _modnorm_matmul(x, g, scale, shift, w4, name, tm=256):
    T = x.shape[0]
    tm = min(tm, T)
    ns = w4.shape[2]

    def body(x_ref, g_ref, sc_ref, sh_ref, w_ref, h_ref, z_ref):
        _, xh = _rms_stats(x_ref[...])
        h = ((xh * g_ref[...]) * (1.0 + sc_ref[...]) + sh_ref[...]).astype(BF)
        h_ref[...] = h
        for j in range(N_CHIPS):
            z_ref[:, j * ns:(j + 1) * ns] = _dot(h, w_ref[j])

    vec = pl.BlockSpec((1, D), lambda i: (0, 0))
    return _pcall(body, name=name, grid=(T // tm,),
                  in_specs=[pl.BlockSpec((tm, D), lambda i: (i, 0)), vec, vec, vec, _resident(w4.shape)],
                  out_specs=[pl.BlockSpec((tm, D), lambda i: (i, 0)), pl.BlockSpec((tm, N_CHIPS * ns), lambda i: (i, 0))],
                  out_shape=[jax.ShapeDtypeStruct((T, D), BF), jax.ShapeDtypeStruct((T, N_CHIPS * ns), F32)],
                  vmem_mb=48)(x, g, scale, shift, w4)


def _rglru_fwd(z, cw, cb, wa, ba, wx, bx, lam, tb=256):
    T = z.shape[0]
    tb = min(tb, T)

    def body(xr_ref, gr_ref, cw_ref, cb_ref, wa_ref, ba_ref, wx_ref, bx_ref, lam_ref, h_ref, ya_ref, prev, hc):
        i = pl.program_id(0)

        @pl.when(i == 0)
        def _():
            prev[...] = jnp.zeros_like(prev)
            hc[...] = jnp.zeros_like(hc)

        xr = xr_ref[...]
        pv = prev[...]
        xc = (cb_ref[...] + cw_ref[3:4, :] * xr + cw_ref[2:3, :] * _shift_down(xr, pv, 1)
              + cw_ref[1:2, :] * _shift_down(xr, pv, 2) + cw_ref[0:1, :] * _shift_down(xr, pv, 3))
        prev[...] = xr[tb - SUBLANES:tb]
        _, ig, _, a, mult = _lru_gates(xc, wa_ref, ba_ref[...], wx_ref, bx_ref[...], lam_ref[...])
        a, u = _scan_rows(a, mult * (ig * xc), reverse=False)
        h = u + a * hc[SUBLANES - 1:SUBLANES, :]
        hc[...] = h[tb - SUBLANES:tb]
        h_ref[...] = h
        ya_ref[...] = (h * _gelu(gr_ref[...])).astype(BF)

    vec = pl.BlockSpec((1, D), lambda i: (0, 0))
    wspec = pl.BlockSpec((HEADS, HD, HD), lambda i: (0, 0, 0))
    return _pcall(body, name="rglru_fwd", grid=(T // tb,),
                  in_specs=[pl.BlockSpec((tb, D), lambda i: (i, 0)), pl.BlockSpec((tb, D), lambda i: (i, 1)),
                            pl.BlockSpec((4, D), lambda i: (0, 0)), vec, wspec, vec, wspec, vec, vec],
                  out_specs=[pl.BlockSpec((tb, D), lambda i: (i, 0))] * 2,
                  out_shape=[jax.ShapeDtypeStruct((T, D), F32), jax.ShapeDtypeStruct((T, D), BF)],
                  scratch=[pltpu.VMEM((SUBLANES, D), F32), pltpu.VMEM((SUBLANES, D), F32)],
                  vmem_mb=48)(z, z, cw, cb, wa, ba, wx, bx, lam)


def _sgu_fwd(z, lg, lb, ws, bst, tb=256):
    T = z.shape[0]
    tb = min(tb, T)

    def body(zu_ref, zv_ref, lg_ref, lb_ref, ws_ref, bst_ref, yb_ref):
        _, xh = _layernorm_stats(_gelu(zv_ref[...]))
        vln = xh * lg_ref[...] + lb_ref[...]
        _, mixed = _sgu_mix(vln, ws_ref, bst_ref, tb)
        yb_ref[...] = (_gelu(zu_ref[...]) * mixed).astype(BF)

    vec = pl.BlockSpec((1, D), lambda i: (0, 0))
    return _pcall(body, name="sgu_fwd", grid=(T // tb,),
                  in_specs=[pl.BlockSpec((tb, D), lambda i: (i, 2)), pl.BlockSpec((tb, D), lambda i: (i, 3)), vec, vec,
                            pl.BlockSpec((HEADS, SGU_BLOCK, SGU_BLOCK), lambda i: (0, 0, 0)),
                            pl.BlockSpec((SGU_BLOCK, HEADS), lambda i: (0, 0))],
                  out_specs=pl.BlockSpec((tb, D), lambda i: (i, 0)),
                  out_shape=jax.ShapeDtypeStruct((T, D), BF))(z, z, lg, lb, ws, bst)


def _mix_out(ya_pre, yb_pre, z, x, gate1, wba, wbb, wo, tm=256):
    T = x.shape[0]
    tm = min(tm, T)

    def body(yap_ref, ybp_ref, ga_ref, gb_ref, x_ref, g1_ref, wa_ref, wb_ref, wo_ref,
             x2_ref, mg_ref, ya_ref, yb_ref, o_ref):
        ya = _dot(yap_ref[...], wa_ref[...])
        yb = _dot(ybp_ref[...], wb_ref[...])
        merged = (_sigmoid_t(ga_ref[...]) * ya + _sigmoid_t(gb_ref[...]) * yb).astype(BF)
        o = _dot(merged, wo_ref[...])
        x2_ref[...] = x_ref[...] + g1_ref[...] * o
        mg_ref[...] = merged
        ya_ref[...] = ya.astype(BF)
        yb_ref[...] = yb.astype(BF)
        o_ref[...] = o.astype(BF)

    row = pl.BlockSpec((tm, D), lambda i: (i, 0))
    wspec = pl.BlockSpec((D, D), lambda i: (0, 0))
    return _pcall(body, name="mix_out", grid=(T // tm,),
                  in_specs=[row, row, pl.BlockSpec((tm, D), lambda i: (i, 4)), pl.BlockSpec((tm, D), lambda i: (i, 5)),
                            row, pl.BlockSpec((1, D), lambda i: (0, 0)), wspec, wspec, wspec],
                  out_specs=[row] * 5,
                  out_shape=[jax.ShapeDtypeStruct((T, D), F32)] + [jax.ShapeDtypeStruct((T, D), BF)] * 4,
                  vmem_mb=48)(ya_pre, yb_pre, z, z, x, gate1, wba, wbb, wo)


def _ffn_gate(up, cw, cb, tm=512, cw_blk=768):
    T = up.shape[0]
    tm = min(tm, T)
    dff = up.shape[1] // 2
    ncb = dff // cw_blk

    def body(ua_ref, uv_ref, wa_ref, wv_ref, ba_ref, bv_ref, f_ref, ga_ref, vd_ref, pa, pv):
        i = pl.program_id(1)

        @pl.when(i == 0)
        def _():
            pa[...] = jnp.zeros_like(pa)
            pv[...] = jnp.zeros_like(pv)

        def conv(u_ref, w_ref, b_ref, prev):
            u = u_ref[...]
            p = prev[...]
            hid = (b_ref[...] + w_ref[2:3, :] * u + w_ref[1:2, :] * _shift_down(u, p, 1)
                   + w_ref[0:1, :] * _shift_down(u, p, 2))
            prev[...] = u[tm - SUBLANES:tm]
            return hid

        act = conv(ua_ref, wa_ref, ba_ref, pa)
        val = conv(uv_ref, wv_ref, bv_ref, pv)
        ga, dga = _gelu_and_grad(act)
        f_ref[...] = (ga * val).astype(BF)
        ga_ref[...] = ga.astype(BF)
        vd_ref[...] = (val * dga).astype(BF)

    blk = pl.BlockSpec((tm, cw_blk), lambda cbk, i: (i, cbk))
    return _pcall(body, name="ffn_gate", grid=(ncb, T // tm),
                  in_specs=[pl.BlockSpec((tm, cw_blk), lambda cbk, i: (i, cbk)),
                            pl.BlockSpec((tm, cw_blk), lambda cbk, i: (i, ncb + cbk)),
                            pl.BlockSpec((3, cw_blk), lambda cbk, i: (0, cbk)),
                            pl.BlockSpec((3, cw_blk), lambda cbk, i: (0, ncb + cbk)),
                            pl.BlockSpec((1, cw_blk), lambda cbk, i: (0, cbk)),
                            pl.BlockSpec((1, cw_blk), lambda cbk, i: (0, ncb + cbk))],
                  out_specs=[blk] * 3, out_shape=[jax.ShapeDtypeStruct((T, dff), BF)] * 3,
                  scratch=[pltpu.VMEM((SUBLANES, cw_blk), F32)] * 2)(up, up, cw, cw, cb, cb)


def _ffn_down_loss(f, wd, x2, gate2, gf, target, tm=512):
    T = x2.shape[0]
    tm = min(tm, T)
    dff = f.shape[1]

    def body(f_ref, wd_ref, x2_ref, g2_ref, gf_ref, t_ref, loss_ref, dx3_ref, dfo_ref, dgf_ref, dg2_ref):
        i = pl.program_id(0)

        @pl.when(i == 0)
        def _():
            loss_ref[...] = jnp.zeros_like(loss_ref)
            dgf_ref[...] = jnp.zeros_like(dgf_ref)
            dg2_ref[...] = jnp.zeros_like(dg2_ref)

        fo = _dot(f_ref[...], wd_ref[...])
        x3 = x2_ref[...] + g2_ref[...] * fo
        rstd, xh = _rms_stats(x3)
        err = xh * gf_ref[...] - t_ref[...]
        loss_ref[...] += 0.5 * jnp.sum(jnp.mean(err * err, axis=-1, keepdims=True), axis=0, keepdims=True)
        dy = err * (1.0 / D)
        dgf_ref[...] += _colsum(dy * xh)
        dxh = dy * gf_ref[...]
        dx3 = rstd * (dxh - xh * jnp.mean(dxh * xh, axis=-1, keepdims=True))
        dg2_ref[...] += _colsum(dx3 * fo)
        dx3_ref[...] = dx3
        dfo_ref[...] = (g2_ref[...] * dx3).astype(BF)

    row = pl.BlockSpec((tm, D), lambda i: (i, 0))
    vec = pl.BlockSpec((1, D), lambda i: (0, 0))
    return _pcall(body, name="ffn_down_loss", grid=(T // tm,),
                  in_specs=[pl.BlockSpec((tm, dff), lambda i: (i, 0)), pl.BlockSpec((dff, D), lambda i: (0, 0)),
                            row, vec, vec, row],
                  out_specs=[pl.BlockSpec((1, LANES), lambda i: (0, 0)), row, row, vec, vec],
                  out_shape=[jax.ShapeDtypeStruct((1, LANES), F32), jax.ShapeDtypeStruct((T, D), F32),
                             jax.ShapeDtypeStruct((T, D), BF), jax.ShapeDtypeStruct((1, D), F32),
                             jax.ShapeDtypeStruct((1, D), F32)],
                  vmem_mb=48)(f, wd, x2, gate2, gf, target)


def _ffn_bwd(dfo, wd, up, ga, vd, cw, tm=256, cw_blk=1536):
    T = up.shape[0]
    tm = min(tm, T)
    dff = up.shape[1] // 2
    ncb = dff // cw_blk
    nrow = T // tm

    def body(dfo_ref, wd_ref, ua_ref, uv_ref, ga_ref, vd_ref, wa_ref, wv_ref,
             du_ref, dwa_ref, dwv_ref, dba_ref, dbv_ref, na, nv):
        i = pl.program_id(1)

        @pl.when(i == 0)
        def _():
            na[...] = jnp.zeros_like(na)
            nv[...] = jnp.zeros_like(nv)
            dwa_ref[...] = jnp.zeros_like(dwa_ref)
            dwv_ref[...] = jnp.zeros_like(dwv_ref)
            dba_ref[...] = jnp.zeros_like(dba_ref)
            dbv_ref[...] = jnp.zeros_like(dbv_ref)

        df = _dot_nt(dfo_ref[...], wd_ref[...])

        def conv_bwd(dh, u_ref, w_ref, nxt, col, dw_ref, db_ref):
            n8 = nxt[...]
            dh1 = _shift_up(dh, n8, 1)
            dh2 = _shift_up(dh, n8, 2)
            nxt[...] = dh[0:SUBLANES]
            du_ref[:, col:col + cw_blk] = (w_ref[2:3, :] * dh + w_ref[1:2, :] * dh1 + w_ref[0:1, :] * dh2).astype(BF)
            u = u_ref[...]
            dw_ref[2:3, :] += _colsum(dh * u)
            dw_ref[1:2, :] += _colsum(dh1 * u)
            dw_ref[0:1, :] += _colsum(dh2 * u)
            db_ref[...] += _colsum(dh)

        conv_bwd(df * vd_ref[...].astype(F32), ua_ref, wa_ref, na, 0, dwa_ref, dba_ref)
        conv_bwd(df * ga_ref[...].astype(F32), uv_ref, wv_ref, nv, cw_blk, dwv_ref, dbv_ref)

    rev = lambda cbk, i: (nrow - 1 - i, cbk)
    rev_v = lambda cbk, i: (nrow - 1 - i, ncb + cbk)
    blk = pl.BlockSpec((tm, cw_blk), rev)
    w3a = pl.BlockSpec((3, cw_blk), lambda cbk, i: (0, cbk))
    w3v = pl.BlockSpec((3, cw_blk), lambda cbk, i: (0, ncb + cbk))
    b1a = pl.BlockSpec((1, cw_blk), lambda cbk, i: (0, cbk))
    return _pcall(body, name="ffn_bwd", grid=(ncb, nrow),
                  in_specs=[pl.BlockSpec((tm, D), lambda cbk, i: (nrow - 1 - i, 0)),
                            pl.BlockSpec((cw_blk, D), lambda cbk, i: (cbk, 0)),
                            blk, pl.BlockSpec((tm, cw_blk), rev_v), blk, blk, w3a, w3v],
                  out_specs=[pl.BlockSpec((tm, 2 * cw_blk), rev), w3a, w3a, b1a, b1a],
                  out_shape=[jax.ShapeDtypeStruct((T, 2 * dff), BF),
                             jax.ShapeDtypeStruct((3, dff), F32), jax.ShapeDtypeStruct((3, dff), F32),
                             jax.ShapeDtypeStruct((1, dff), F32), jax.ShapeDtypeStruct((1, dff), F32)],
                  scratch=[pltpu.VMEM((SUBLANES, cw_blk), F32)] * 2,
                  vmem_mb=48)(dfo, wd, up, up, ga, vd, cw, cw)


def _ffn_col_block(t, ncb):
    return jnp.where(t < ncb, 2 * t, 2 * (t - ncb) + 1)


def _mm_tn_cols(a, b, name, nshard, nb, colmap=None, mb=None, tm=1024):
    T, M = a.shape
    tm = min(tm, T)
    mb = M if mb is None else mb
    ns = b.shape[1] // nshard
    per = ns // nb
    cmap = colmap if colmap is not None else (lambda t: t)

    def body(a_ref, b_ref, o_ref):
        k = pl.program_id(2)

        @pl.when(k == 0)
        def _():
            o_ref[...] = jnp.zeros_like(o_ref)

        o_ref[0] += _dot_tn(a_ref[...], b_ref[...])

    return _pcall(body, name=name, grid=(M // mb, nshard * per, T // tm),
                  in_specs=[pl.BlockSpec((tm, mb), lambda m, t, k: (k, m)),
                            pl.BlockSpec((tm, nb), lambda m, t, k: (k, cmap(t)))],
                  out_specs=pl.BlockSpec((1, mb, nb), lambda m, t, k: (t // per, m, t % per)),
                  out_shape=jax.ShapeDtypeStruct((nshard, M, ns), F32), vmem_mb=48)(a, b)


def _mm_nt_normbwd(dz, w4, x, resid, g, scale, name, gate=None, o=None, dz_blocks=(0, 1, 2, 3), tm=256):
    T = x.shape[0]
    tm = min(tm, T)
    ns = w4.shape[2]
    gated = gate is not None

    def body(*refs):
        if gated:
            (dz_ref, w_ref, x_ref, r_ref, g_ref, sc_ref, gt_ref, o_ref,
             dx_ref, dsh_ref, dsc_ref, dg_ref, do_ref, dgt_ref) = refs
        else:
            dz_ref, w_ref, x_ref, r_ref, g_ref, sc_ref, dx_ref, dsh_ref, dsc_ref, dg_ref = refs
        i = pl.program_id(0)

        @pl.when(i == 0)
        def _():
            dsh_ref[...] = jnp.zeros_like(dsh_ref)
            dsc_ref[...] = jnp.zeros_like(dsc_ref)
            dg_ref[...] = jnp.zeros_like(dg_ref)
            if gated:
                dgt_ref[...] = jnp.zeros_like(dgt_ref)

        dh = None
        for j in range(N_CHIPS):
            blk = dz_blocks[j]
            part = _dot_nt(dz_ref[:, blk * ns:(blk + 1) * ns], w_ref[j])
            dh = part if dh is None else dh + part
        rstd, xh = _rms_stats(x_ref[...])
        dsh_ref[...] += _colsum(dh)
        dsc_ref[...] += _colsum(dh * (xh * g_ref[...]))
        dn = dh * (1.0 + sc_ref[...])
        dg_ref[...] += _colsum(dn * xh)
        dxh = dn * g_ref[...]
        dx = r_ref[...] + rstd * (dxh - xh * jnp.mean(dxh * xh, axis=-1, keepdims=True))
        dx_ref[...] = dx
        if gated:
            do_ref[...] = (gt_ref[...] * dx).astype(BF)
            dgt_ref[...] += _colsum(dx * o_ref[...].astype(F32))

    row = pl.BlockSpec((tm, D), lambda i: (i, 0))
    vec = pl.BlockSpec((1, D), lambda i: (0, 0))
    in_specs = [pl.BlockSpec((tm, N_CHIPS * ns), lambda i: (i, 0)), _resident(w4.shape), row, row, vec, vec]
    out_specs = [row, vec, vec, vec]
    out_shape = [jax.ShapeDtypeStruct((T, D), F32)] + [jax.ShapeDtypeStruct((1, D), F32)] * 3
    args = [dz, w4, x, resid, g, scale]
    if gated:
        in_specs += [vec, row]
        out_specs += [row, vec]
        out_shape += [jax.ShapeDtypeStruct((T, D), BF), jax.ShapeDtypeStruct((1, D), F32)]
        args += [gate, o]
    return _pcall(body, name=name, grid=(T // tm,), in_specs=in_specs, out_specs=out_specs, out_shape=out_shape,
                  vmem_mb=48)(*args)


def _mix_bwd(do, ya, yb, z, wo, wba, wbb, tm=256):
    T = do.shape[0]
    tm = min(tm, T)

    def body(do_ref, ya_ref, yb_ref, ga_ref, gb_ref, wo_ref, wa_ref, wb_ref,
             dz_ref, dya_ref, dyb_ref, dyap_ref, dybp_ref):
        dm = _dot_nt(do_ref[...], wo_ref[...])
        sa = _sigmoid_t(ga_ref[...])
        sb = _sigmoid_t(gb_ref[...])
        dya = (sa * dm).astype(BF)
        dyb = (sb * dm).astype(BF)
        dz_ref[:, 0:D] = (dm * ya_ref[...].astype(F32) * sa * (1.0 - sa)).astype(BF)
        dz_ref[:, D:2 * D] = (dm * yb_ref[...].astype(F32) * sb * (1.0 - sb)).astype(BF)
        dya_ref[...] = dya
        dyb_ref[...] = dyb
        dyap_ref[...] = _dot_nt(dya, wa_ref[...]).astype(BF)
        dybp_ref[...] = _dot_nt(dyb, wb_ref[...]).astype(BF)

    row = pl.BlockSpec((tm, D), lambda i: (i, 0))
    wspec = pl.BlockSpec((D, D), lambda i: (0, 0))
    return _pcall(body, name="mix_bwd", grid=(T // tm,),
                  in_specs=[row, row, row, pl.BlockSpec((tm, D), lambda i: (i, 4)),
                            pl.BlockSpec((tm, D), lambda i: (i, 5)), wspec, wspec, wspec],
                  out_specs=[pl.BlockSpec((tm, 2 * D), lambda i: (i, 2)), row, row, row, row],
                  out_shape=[jax.ShapeDtypeStruct((T, 6 * D), BF)] + [jax.ShapeDtypeStruct((T, D), BF)] * 4,
                  vmem_mb=48)(do, ya, yb, z, z, wo, wba, wbb)


def _sgu_bwd(dz, dyb_pre, z, lg, lb, ws, bst, tb=256):
    T = z.shape[0]
    tb = min(tb, T)

    def body(dz_in, dy_ref, zu_ref, zv_ref, lg_ref, lb_ref, ws_ref, bst_ref,
             dz_ref, dws_ref, dbst_ref, dlg_ref, dlb_ref):
        del dz_in
        i = pl.program_id(0)

        @pl.when(i == 0)
        def _():
            dws_ref[...] = jnp.zeros_like(dws_ref)
            dbst_ref[...] = jnp.zeros_like(dbst_ref)
            dlg_ref[...] = jnp.zeros_like(dlg_ref)
            dlb_ref[...] = jnp.zeros_like(dlb_ref)

        gu, dgu = _gelu_and_grad(zu_ref[...])
        gv, dgv = _gelu_and_grad(zv_ref[...])
        rstd, xh = _layernorm_stats(gv)
        vln = xh * lg_ref[...] + lb_ref[...]
        wm, mixed = _sgu_mix(vln, ws_ref, bst_ref, tb)
        dy = dy_ref[...].astype(F32)
        dz_ref[:, 0:D] = (dy * mixed * dgu).astype(BF)
        dmixed = dy * gu
        ri = lax.broadcasted_iota(jnp.int32, (SGU_BLOCK, SGU_BLOCK), 0)
        ci = lax.broadcasted_iota(jnp.int32, (SGU_BLOCK, SGU_BLOCK), 1)
        blocks = []
        for blk in range(tb // SGU_BLOCK):
            rs = slice(blk * SGU_BLOCK, (blk + 1) * SGU_BLOCK)
            cols = []
            for g in range(HEADS):
                cs = slice(g * HD, (g + 1) * HD)
                dmg = dmixed[rs, cs]
                dmb = dmg.astype(BF)
                dbst_ref[:, g:g + 1] += jnp.sum(dmg, axis=1, keepdims=True)
                dws_ref[g] += jnp.where(ri >= ci, _dot_nt(dmb, vln[rs, cs].astype(BF)), 0.0)
                cols.append(_dot_tn(wm[g], dmb))
            blocks.append(jnp.concatenate(cols, axis=1))
        dvln = blocks[0] if len(blocks) == 1 else jnp.concatenate(blocks, axis=0)
        dlg_ref[...] += _colsum(dvln * xh)
        dlb_ref[...] += _colsum(dvln)
        dxh = dvln * lg_ref[...]
        dgv_in = rstd * (dxh - jnp.mean(dxh, axis=-1, keepdims=True)
                         - xh * jnp.mean(dxh * xh, axis=-1, keepdims=True))
        dz_ref[:, D:2 * D] = (dgv_in * dgv).astype(BF)

    row = pl.BlockSpec((tb, D), lambda i: (i, 0))
    vec = pl.BlockSpec((1, D), lambda i: (0, 0))
    wspec = pl.BlockSpec((HEADS, SGU_BLOCK, SGU_BLOCK), lambda i: (0, 0, 0))
    bspec = pl.BlockSpec((SGU_BLOCK, HEADS), lambda i: (0, 0))
    return _pcall(body, name="sgu_bwd", grid=(T // tb,),
                  in_specs=[HBM_SPEC, row, pl.BlockSpec((tb, D), lambda i: (i, 2)),
                            pl.BlockSpec((tb, D), lambda i: (i, 3)), vec, vec, wspec, bspec],
                  out_specs=[pl.BlockSpec((tb, 2 * D), lambda i: (i, 1)), wspec, bspec, vec, vec],
                  out_shape=[jax.ShapeDtypeStruct(dz.shape, BF),
                             jax.ShapeDtypeStruct((HEADS, SGU_BLOCK, SGU_BLOCK), F32),
                             jax.ShapeDtypeStruct((SGU_BLOCK, HEADS), F32),
                             jax.ShapeDtypeStruct((1, D), F32), jax.ShapeDtypeStruct((1, D), F32)],
                  aliases={0: 0}, vmem_mb=48)(dz, dyb_pre, z, z, lg, lb, ws, bst)


def _rglru_bwd(dz, dya_pre, z, h, cw, cb, wa, ba, wx, bx, lam, tb=256):
    T = z.shape[0]
    tb = min(tb, T)
    nrow = T // tb
    per = tb // SUBLANES

    def body(dz_in, dy_ref, xr_ref, xh_ref, gr_ref, h_ref, hh_ref, cw_ref, cb_ref, wa_ref, ba_ref, wx_ref, bx_ref,
             lam_ref, dz_ref, dcw_ref, dcb_ref, dwa_ref, dba_ref, dwx_ref, dbx_ref, dlam_ref, carry, nxt):
        del dz_in
        i = pl.program_id(0)
        first_block = i == nrow - 1

        @pl.when(i == 0)
        def _():
            carry[...] = jnp.zeros_like(carry)
            nxt[...] = jnp.zeros_like(nxt)
            for ref in (dcw_ref, dcb_ref, dwa_ref, dba_ref, dwx_ref, dbx_ref, dlam_ref):
                ref[...] = jnp.zeros_like(ref)

        xr = xr_ref[...]
        pv = jnp.where(first_block, 0.0, xh_ref[...])
        s1 = _shift_down(xr, pv, 1)
        s2 = _shift_down(xr, pv, 2)
        s3 = _shift_down(xr, pv, 3)
        xc = cb_ref[...] + cw_ref[3:4, :] * xr + cw_ref[2:3, :] * s1 + cw_ref[1:2, :] * s2 + cw_ref[0:1, :] * s3
        lam = lam_ref[...]
        r, ig, ls, a, mult = _lru_gates(xc, wa_ref, ba_ref[...], wx_ref, bx_ref[...], lam)
        hv = h_ref[...]
        hprev = _shift_down(hv, jnp.where(first_block, 0.0, hh_ref[...]), 1)
        gg, dgg = _gelu_and_grad(gr_ref[...])
        dy = dy_ref[...].astype(F32)
        dz_ref[:, D:2 * D] = (dy * hv * dgg).astype(BF)

        rows = lax.broadcasted_iota(jnp.int32, (tb, D), 0)
        v = dy * gg + jnp.where(rows == tb - 1, carry[0:1, :], 0.0)
        q = jnp.where(rows < tb - 1, pltpu.roll(a, tb - 1, 0), 0.0)
        _, gsc = _scan_rows(q, v, reverse=True)
        carry[...] = (a * gsc)[0:SUBLANES]

        xi = ig * xc
        dmult = gsc * xi
        dxi = gsc * mult
        dig = dxi * xc
        dxc = dxi * ig
        dlog_a = gsc * hprev * a - dmult * (a * a) / mult
        dlam_ref[...] += _colsum(dlog_a * r) * (LRU_C * _sigmoid(-lam))
        dpr = dlog_a * (LRU_C * ls) * r * (1.0 - r)
        dpi = dig * ig * (1.0 - ig)
        dba_ref[...] += _colsum(dpr)
        dbx_ref[...] += _colsum(dpi)
        back = []
        for hh in range(HEADS):
            cs = slice(hh * HD, (hh + 1) * HD)
            xh = xc[:, cs].astype(BF)
            dprh = dpr[:, cs].astype(BF)
            dpih = dpi[:, cs].astype(BF)
            dwa_ref[hh] += _dot_tn(xh, dprh)
            dwx_ref[hh] += _dot_tn(xh, dpih)
            back.append(_dot_nt(dprh, wa_ref[hh].astype(BF)) + _dot_nt(dpih, wx_ref[hh].astype(BF)))
        dxc = dxc + jnp.concatenate(back, axis=1)

        n8 = nxt[...]
        dxr = (cw_ref[3:4, :] * dxc + cw_ref[2:3, :] * _shift_up(dxc, n8, 1)
               + cw_ref[1:2, :] * _shift_up(dxc, n8, 2) + cw_ref[0:1, :] * _shift_up(dxc, n8, 3))
        nxt[...] = dxc[0:SUBLANES]
        dz_ref[:, 0:D] = dxr.astype(BF)
        dcw_ref[3:4, :] += _colsum(dxc * xr)
        dcw_ref[2:3, :] += _colsum(dxc * s1)
        dcw_ref[1:2, :] += _colsum(dxc * s2)
        dcw_ref[0:1, :] += _colsum(dxc * s3)
        dcb_ref[...] += _colsum(dxc)

    rev = lambda col: (lambda i: (nrow - 1 - i, col))
    halo = lambda col: pl.BlockSpec((SUBLANES, D), lambda i: (jnp.maximum((nrow - 1 - i) * per - 1, 0), col))
    vec = pl.BlockSpec((1, D), lambda i: (0, 0))
    wspec = pl.BlockSpec((HEADS, HD, HD), lambda i: (0, 0, 0))
    c4 = pl.BlockSpec((4, D), lambda i: (0, 0))
    wshape = jax.ShapeDtypeStruct((HEADS, HD, HD), F32)
    vshape = jax.ShapeDtypeStruct((1, D), F32)
    return _pcall(body, name="rglru_bwd", grid=(nrow,),
                  in_specs=[HBM_SPEC, pl.BlockSpec((tb, D), rev(0)), pl.BlockSpec((tb, D), rev(0)), halo(0),
                            pl.BlockSpec((tb, D), rev(1)), pl.BlockSpec((tb, D), rev(0)), halo(0),
                            c4, vec, wspec, vec, wspec, vec, vec],
                  out_specs=[pl.BlockSpec((tb, 2 * D), rev(0)), c4, vec, wspec, vec, wspec, vec, vec],
                  out_shape=[jax.ShapeDtypeStruct(dz.shape, BF), jax.ShapeDtypeStruct((4, D), F32), vshape,
                             wshape, vshape, wshape, vshape, vshape],
                  scratch=[pltpu.VMEM((SUBLANES, D), F32), pltpu.VMEM((SUBLANES, D), F32)],
                  aliases={0: 0}, vmem_mb=56)(dz, dya_pre, z, z, z, h, h, cw, cb, wa, ba, wx, bx, lam)


def _pack_rows(parts):
    out = []
    for p in parts:
        q = p.reshape(-1, LANES)
        pad = (-q.shape[0]) % SUBLANES
        if pad:
            q = jnp.concatenate([q, jnp.zeros((pad, LANES), q.dtype)], axis=0)
        out.append(q)
    return jnp.concatenate(out, axis=0)


def _rows_of(shape):
    n = 1
    for s in shape:
        n *= s
    rows = n // LANES
    return rows + (-rows) % SUBLANES


def kernel(x, c, w_ada, b_ada, norm_mix_g, w_in, rnn_conv_w, rnn_conv_b, lru_w_a, lru_b_a, lru_w_x, lru_b_x, lru_lambda, sgu_ln_g, sgu_ln_b, sgu_w_s, sgu_b_s, w_branch_a, w_branch_b, w_out, norm_ffn_g, w_up, ffn_conv_w, ffn_conv_b, w_down, norm_final_g, loss_target, m_w_ada, m_b_ada, m_norm_mix_g, m_w_in, m_rnn_conv_w, m_rnn_conv_b, m_lru_w_a, m_lru_b_a, m_lru_w_x, m_lru_b_x, m_lru_lambda, m_sgu_ln_g, m_sgu_ln_b, m_sgu_w_s, m_sgu_b_s, m_w_branch_a, m_w_branch_b, m_w_out, m_norm_ffn_g, m_w_up, m_ffn_conv_w, m_ffn_conv_b, m_w_down, m_norm_final_g, v_w_ada, v_b_ada, v_norm_mix_g, v_w_in, v_rnn_conv_w, v_rnn_conv_b, v_lru_w_a, v_lru_b_a, v_lru_w_x, v_lru_b_x, v_lru_lambda, v_sgu_ln_g, v_sgu_ln_b, v_sgu_w_s, v_sgu_b_s, v_w_branch_a, v_w_branch_b, v_w_out, v_norm_ffn_g, v_w_up, v_ffn_conv_w, v_ffn_conv_b, v_w_down, v_norm_final_g):
    args = dict(locals())
    T = x.shape[1]
    mx, my, mc = lax.axis_index("x"), lax.axis_index("y"), lax.axis_index("c")
    chip = 2 * mx + my
    dev = 2 * chip + mc
    vec = lambda a: a.reshape(1, -1)

    xt = x.reshape(T, D)
    tgt = loss_target.reshape(T, D)
    ns = w_in.shape[2]
    dff = w_down.shape[1] * N_CHIPS

    c_all = _gather8(c.reshape(SUBLANES, LANES), "gather_c").reshape(N_DEV, D)
    b_ada_sh = lax.dynamic_slice(b_ada, (0, chip * ns), (1, ns))
    mod_sh = _mod_fwd(c_all, w_ada[0], b_ada_sh)

    w_in_b, w_up_b, w_down_b, wba_b, wbb_b, wo_b = _cast_shards(
        [w_in[0], w_up[0], w_down[0], w_branch_a[0], w_branch_b[0], w_out[0]])
    (w_in4, w_up4, w_down4, wba4, wbb4, wo4, rcw4, fcw4, mod4) = _gather_weights(
        [w_in_b, w_up_b, w_down_b, wba_b, wbb_b, wo_b, rnn_conv_w[0], ffn_conv_w[0], mod_sh],
        [True] * 6 + [False] * 3)
    wd_full = w_down4.reshape(dff, D)
    wba_full = wba4.reshape(D, D)
    wbb_full = wbb4.reshape(D, D)
    wo_full = wo4.reshape(D, D)
    rcw_full = jnp.transpose(rcw4, (1, 0, 2)).reshape(4, D)
    fcw_full = jnp.transpose(fcw4, (1, 0, 2)).reshape(3, 2 * dff)
    mod = lax.dynamic_index_in_dim(mod4, dev, axis=1, keepdims=False).reshape(1, 6 * D)
    shift1, scale1, gate1, shift2, scale2, gate2 = [mod[:, k * D:(k + 1) * D] for k in range(6)]

    h1, z = _modnorm_matmul(xt, norm_mix_g, scale1, shift1, w_in4, "norm_in_proj")
    bst = jnp.transpose(sgu_b_s[0])
    h_lru, ya_pre = _rglru_fwd(z, rcw_full, rnn_conv_b, lru_w_a[0], lru_b_a, lru_w_x[0], lru_b_x, lru_lambda)
    yb_pre = _sgu_fwd(z, sgu_ln_g, sgu_ln_b, sgu_w_s[0], bst)
    x2, merged, ya, yb, o1 = _mix_out(ya_pre, yb_pre, z, xt, gate1, wba_full, wbb_full, wo_full)
    h2, up = _modnorm_matmul(x2, norm_ffn_g, scale2, shift2, w_up4, "norm_up_proj")
    f, ffn_ga, ffn_vd = _ffn_gate(up, fcw_full, ffn_conv_b)
    loss_part, dx3, dfo, dgf, dgate2 = _ffn_down_loss(f, wd_full, x2, gate2, vec(norm_final_g), tgt)

    ffn_map = functools.partial(_ffn_col_block, ncb=2)
    dup, dfcw_a, dfcw_v, dfcb_a, dfcb_v = _ffn_bwd(dfo, wd_full, up, ffn_ga, ffn_vd, fcw_full, cw_blk=ns)
    dwd = _mm_tn_cols(f, dfo, "dw_down", 1, D, mb=D)
    dw_up4 = _mm_tn_cols(h2, dup, "dw_up", N_CHIPS, ns, colmap=ffn_map)
    dx2, dshift2, dscale2, dg_ffn, do1, dgate1 = _mm_nt_normbwd(
        dup, w_up4, x2, dx3, norm_ffn_g, scale2, "dh2_norm_bwd", gate=gate1, o=o1, dz_blocks=(0, 2, 1, 3))
    dz, dya, dyb, dya_pre, dyb_pre = _mix_bwd(do1, ya, yb, z, wo_full, wba_full, wbb_full)
    dwo = _mm_tn_cols(merged, do1, "dw_out", 1, D)
    dwba = _mm_tn_cols(ya_pre, dya, "dw_branch_a", 1, D)
    dwbb = _mm_tn_cols(yb_pre, dyb, "dw_branch_b", 1, D)
    dz, dws, dbst, dlg, dlb = _sgu_bwd(dz, dyb_pre, z, sgu_ln_g, sgu_ln_b, sgu_w_s[0], bst)
    dz, drcw, drcb, dwa, dba, dwx, dbx, dlam = _rglru_bwd(
        dz, dya_pre, z, h_lru, rcw_full, rnn_conv_b, lru_w_a[0], lru_b_a, lru_w_x[0], lru_b_x, lru_lambda)
    dw_in4 = _mm_tn_cols(h1, dz, "dw_in", N_CHIPS, ns)
    grad_x, dshift1, dscale1, dg_mix = _mm_nt_normbwd(dz, w_in4, xt, dx2, norm_mix_g, scale1, "dh1_norm_bwd")
    dmod = jnp.concatenate([dshift1, dscale1, dgate1, dshift2, dscale2, dgate2], axis=1)

    big = [("w_in", dw_in4), ("w_up", dw_up4), ("w_down", dwd.reshape(N_CHIPS, dff // N_CHIPS, D)),
           ("w_branch_a", dwba.reshape(N_CHIPS, D // N_CHIPS, D)), ("w_branch_b", dwbb.reshape(N_CHIPS, D // N_CHIPS, D)),
           ("w_out", dwo.reshape(N_CHIPS, D // N_CHIPS, D))]
    core = mc.astype(jnp.int32).reshape(1)
    from_core = _send_other_half([g for _, g in big])
    chip_sums = [_add_halves_bf16(core, g, o, "sum_cores_" + n) for (n, g), o in zip(big, from_core)]
    parts = _exchange_xy(chip_sums, True, "scatter_grads")
    totals = [_sum_parts(p, "sum_chips_" + n) for (n, _), p in zip(big, parts)]
    fulls = _share_halves(totals)
    out = {}
    for (n, _), full in zip(big, fulls):
        shape = args[n].shape
        res = _adamw(args[n][0], args["m_" + n][0], args["v_" + n][0], [full.reshape(shape[1:])], "adamw_" + n)
        for kind, r in zip(("grad_", "delta_", "new_m_", "new_v_"), res):
            out[kind + n] = r.reshape(shape)

    small = [("b_ada", dmod), ("norm_mix_g", dg_mix), ("rnn_conv_b", drcb), ("lru_w_a", dwa), ("lru_b_a", dba),
             ("lru_w_x", dwx), ("lru_b_x", dbx), ("lru_lambda", dlam), ("sgu_ln_g", dlg), ("sgu_ln_b", dlb),
             ("sgu_w_s", dws), ("sgu_b_s", jnp.transpose(dbst)), ("norm_ffn_g", dg_ffn),
             ("ffn_conv_b", jnp.concatenate([dfcb_a, dfcb_v], axis=1)), ("norm_final_g", dgf)]
    dfcw = jnp.concatenate([dfcw_a, dfcw_v], axis=1)
    r_small = sum(_rows_of(args[n].shape) for n, _ in small)
    r_pad = r_small + (-r_small) % 256
    fill = jnp.zeros((r_pad - r_small, LANES), F32)
    g_pack = jnp.concatenate([_pack_rows([g for _, g in small]), fill, _pack_rows([drcw, dfcw])], axis=0)
    g_all = _gather8(g_pack, "gather_small_grads")
    g_sum = _sum_parts(g_all, "sum_small_grads")

    def pack_small(prefix):
        return jnp.concatenate([_pack_rows([args[prefix + n] for n, _ in small]), fill], axis=0)

    res = _adamw(pack_small(""), pack_small("m_"), pack_small("v_"), [g_sum[:r_pad]], "adamw_small")
    off = 0
    for n, _ in small:
        shape = args[n].shape
        rows = _rows_of(shape)
        for kind, r in zip(("grad_", "delta_", "new_m_", "new_v_"), res):
            out[kind + n] = r[off:off + rows].reshape(shape)
        off += rows

    rcw_cols = rnn_conv_w.shape[2]
    g_rcw = lax.dynamic_slice(g_sum[r_pad:r_pad + 32].reshape(4, D), (0, chip * rcw_cols), (4, rcw_cols))
    g_fcw = lax.dynamic_slice(g_sum[r_pad + 32:r_pad + 32 + 144].reshape(3, 2 * dff), (0, chip * ns), (3, ns))
    conv = [("rnn_conv_w", g_rcw), ("ffn_conv_w", g_fcw)]
    res = _adamw(_pack_rows([args[n] for n, _ in conv]), _pack_rows([args["m_" + n] for n, _ in conv]),
                 _pack_rows([args["v_" + n] for n, _ in conv]), [_pack_rows([g for _, g in conv])], "adamw_conv")
    off = 0
    for n, _ in conv:
        shape = args[n].shape
        cnt = shape[1] * shape[2] // LANES
        for kind, r in zip(("grad_", "delta_", "new_m_", "new_v_"), res):
            out[kind + n] = r[off:off + cnt].reshape(shape)
        off += _rows_of(shape)

    dmod_all = g_all[:, 0:6 * D // LANES, :].reshape(N_DEV, 6 * D)
    dmod_sh = lax.dynamic_slice(dmod_all, (0, chip * ns), (N_DEV, ns))
    res = _ada_adamw(jnp.transpose(c_all), dmod_sh, w_ada[0], m_w_ada[0], v_w_ada[0])
    for kind, r in zip(("grad_", "delta_", "new_m_", "new_v_"), res):
        out[kind + "w_ada"] = r.reshape(w_ada.shape)

    loss = lax.psum(loss_part[0, 0], ("x", "y", "c"))
    names = ["w_ada", "b_ada", "norm_mix_g", "w_in", "rnn_conv_w", "rnn_conv_b", "lru_w_a", "lru_b_a", "lru_w_x",
             "lru_b_x", "lru_lambda", "sgu_ln_g", "sgu_ln_b", "sgu_w_s", "sgu_b_s", "w_branch_a", "w_branch_b",
             "w_out", "norm_ffn_g", "w_up", "ffn_conv_w", "ffn_conv_b", "w_down", "norm_final_g"]
    result = [loss, grad_x.reshape(x.shape)]
    for kind in ("grad_", "delta_", "new_m_", "new_v_"):
        result += [out[kind + n] for n in names]
    return tuple(result)
```

```python
import functools

import jax
import jax.numpy as jnp
from jax import lax
from jax.experimental import pallas as pl
from jax.experimental.pallas import tpu as pltpu

F32 = jnp.float32
BF = jnp.bfloat16

D = 1024
HEADS = 8
HD = D // HEADS
SGU_BLOCK = 128
N_CHIPS = 4
N_DEV = 8
EPS = 1e-6
LRU_C = 8.0
LANES = 128
SUBLANES = 8

ADAM_LR = 0.001
ADAM_B1 = 0.9
ADAM_B2 = 0.999
ADAM_EPS = 1e-08
ADAM_WD = 0.01
ADAM_STEP = 10

GELU_K0 = 0.7978845608028654
GELU_K1 = 0.044715

HBM_SPEC = pl.BlockSpec(memory_space=pltpu.HBM)
MESH_ID = pl.DeviceIdType.MESH


def _pcall(body, *, name, out_shape, grid=(), in_specs=None, out_specs=None, scratch=(), vmem_mb=32, aliases=None,
           grid_spec=None):
    kw = {}
    if aliases:
        kw["input_output_aliases"] = aliases
    if grid_spec is not None:
        kw["grid_spec"] = grid_spec
        ndim = len(grid_spec.grid)
    else:
        kw.update(grid=grid, in_specs=in_specs, out_specs=out_specs, scratch_shapes=list(scratch))
        ndim = len(grid)
    if ndim:
        params = pltpu.CompilerParams(dimension_semantics=("arbitrary",) * ndim, vmem_limit_bytes=vmem_mb * 2 ** 20)
    else:
        params = pltpu.CompilerParams(vmem_limit_bytes=vmem_mb * 2 ** 20)
    return pl.pallas_call(body, name=name, out_shape=out_shape, compiler_params=params, **kw)


def _gelu(x):
    return 0.5 * x * (1.0 + jnp.tanh(GELU_K0 * (x + GELU_K1 * x * x * x)))


def _gelu_and_grad(x):
    x2 = x * x
    t = jnp.tanh(GELU_K0 * x * (1.0 + GELU_K1 * x2))
    g = 0.5 * x * (1.0 + t)
    dg = 0.5 * (1.0 + t) + 0.5 * x * (1.0 - t * t) * (GELU_K0 * (1.0 + 3.0 * GELU_K1 * x2))
    return g, dg


def _sigmoid(x):
    return 1.0 / (1.0 + jnp.exp(-x))


def _sigmoid_t(x):
    return 0.5 * jnp.tanh(0.5 * x) + 0.5


def _log_sigmoid(x):
    e = jnp.exp(-jnp.abs(x))
    u = 1.0 + e
    d = u - 1.0
    l1p = jnp.where(d == 0.0, e, jnp.log(u) * (e / jnp.where(d == 0.0, 1.0, d)))
    return jnp.minimum(x, 0.0) - l1p


def _dot(a, b):
    return jnp.dot(a, b, preferred_element_type=F32)


def _dot_nt(a, b):
    return lax.dot_general(a, b, (((1,), (1,)), ((), ())), preferred_element_type=F32)


def _dot_tn(a, b):
    return lax.dot_general(a, b, (((0,), (0,)), ((), ())), preferred_element_type=F32)


def _shift_down(x, halo, s):
    r = pltpu.roll(x, s, 0)
    rows = lax.broadcasted_iota(jnp.int32, (SUBLANES, x.shape[1]), 0)
    head = jnp.where(rows < s, pltpu.roll(halo, s, 0), r[0:SUBLANES])
    return jnp.concatenate([head, r[SUBLANES:]], axis=0)


def _shift_up(x, halo, s):
    n = x.shape[0]
    r = pltpu.roll(x, n - s, 0)
    rows = lax.broadcasted_iota(jnp.int32, (SUBLANES, x.shape[1]), 0)
    tail = jnp.where(rows >= SUBLANES - s, pltpu.roll(halo, SUBLANES - s, 0), r[n - SUBLANES:n])
    return jnp.concatenate([r[:n - SUBLANES], tail], axis=0)


def _scan_rows(a, u, reverse):
    n, width = a.shape
    rows = lax.broadcasted_iota(jnp.int32, (n, width), 0)
    d = 1
    while d < n:
        if d < SUBLANES:
            keep = rows < n - d if reverse else rows >= d
            shift = n - d if reverse else d
            a_s = jnp.where(keep, pltpu.roll(a, shift, 0), 1.0)
            u_s = jnp.where(keep, pltpu.roll(u, shift, 0), 0.0)
        elif reverse:
            a_s = jnp.concatenate([a[d:], jnp.ones((d, width), a.dtype)], axis=0)
            u_s = jnp.concatenate([u[d:], jnp.zeros((d, width), u.dtype)], axis=0)
        else:
            a_s = jnp.concatenate([jnp.ones((d, width), a.dtype), a[:n - d]], axis=0)
            u_s = jnp.concatenate([jnp.zeros((d, width), u.dtype), u[:n - d]], axis=0)
        u = a * u_s + u
        a = a * a_s
        d *= 2
    return a, u


def _colsum(x):
    return jnp.sum(x, axis=0, keepdims=True)


def _rms_stats(x):
    r = lax.rsqrt(jnp.mean(x * x, axis=-1, keepdims=True) + EPS)
    return r, x * r


def _lru_gates(xc, wa_ref, ba, wx_ref, bx, lam):
    pr, pi = [], []
    for hh in range(HEADS):
        xh = xc[:, hh * HD:(hh + 1) * HD].astype(BF)
        pr.append(_dot(xh, wa_ref[hh].astype(BF)))
        pi.append(_dot(xh, wx_ref[hh].astype(BF)))
    r = _sigmoid_t(jnp.concatenate(pr, axis=1) + ba)
    ig = _sigmoid_t(jnp.concatenate(pi, axis=1) + bx)
    ls = _log_sigmoid(lam)
    log_a = LRU_C * r * ls
    a = jnp.exp(log_a)
    x2 = 2.0 * log_a
    u = a * a
    lu = jnp.log(jnp.maximum(u, 1e-37))
    em1 = jnp.where(lu == 0.0, x2, jnp.where(u < 1e-30, -1.0, (u - 1.0) * x2 / jnp.where(lu == 0.0, 1.0, lu)))
    mult = jnp.sqrt(-em1)
    return r, ig, ls, a, mult


def _sgu_mix(vln, ws_ref, bst_ref, tb):
    ri = lax.broadcasted_iota(jnp.int32, (SGU_BLOCK, SGU_BLOCK), 0)
    ci = lax.broadcasted_iota(jnp.int32, (SGU_BLOCK, SGU_BLOCK), 1)
    wm = [jnp.where(ri >= ci, ws_ref[g], 0.0).astype(BF) for g in range(HEADS)]
    blocks = []
    for blk in range(tb // SGU_BLOCK):
        cols = []
        for g in range(HEADS):
            vb = vln[blk * SGU_BLOCK:(blk + 1) * SGU_BLOCK, g * HD:(g + 1) * HD].astype(BF)
            cols.append(_dot(wm[g], vb) + bst_ref[:, g:g + 1])
        blocks.append(jnp.concatenate(cols, axis=1))
    mixed = blocks[0] if len(blocks) == 1 else jnp.concatenate(blocks, axis=0)
    return wm, mixed


def _layernorm_stats(v):
    mu = jnp.mean(v, axis=-1, keepdims=True)
    vc = v - mu
    rstd = lax.rsqrt(jnp.mean(vc * vc, axis=-1, keepdims=True) + EPS)
    return rstd, vc * rstd


def _my_xyc():
    return lax.axis_index("x"), lax.axis_index("y"), lax.axis_index("c")


def _gather_weights(srcs, halve):
    n = len(srcs)
    out_shape = [jax.ShapeDtypeStruct((N_CHIPS,) + s.shape, s.dtype) for s in srcs]

    def body(*refs):
        src, out = refs[:n], refs[n:2 * n]
        send_sems, recv_sems, fwd_send, fwd_recv, loc_sems = refs[2 * n:]
        x, y, c = _my_xyc()
        me = 2 * x + y
        chips = [(1 - x, y), (x, 1 - y), (1 - x, 1 - y)]

        def half(ref, a, which):
            if not halve[a]:
                return ref
            h = srcs[a].shape[0] // 2
            return ref.at[pl.ds(which * h, h)]

        def ici(a, k, frm):
            px, py = chips[k]
            return pltpu.make_async_remote_copy(
                src_ref=half(src[a], a, c), dst_ref=half(out[a].at[frm], a, c),
                send_sem=send_sems.at[a, k], recv_sem=recv_sems.at[a, k],
                device_id=(px, py, c), device_id_type=MESH_ID)

        def d2d(a, k, which):
            px, py = chips[k]
            rows = half(out[a].at[2 * px + py], a, which)
            return pltpu.make_async_remote_copy(
                src_ref=rows, dst_ref=rows, send_sem=fwd_send.at[a, k], recv_sem=fwd_recv.at[a, k],
                device_id=(x, y, 1 - c), device_id_type=MESH_ID)

        local, sends = [], []
        for a in range(n):
            lc = pltpu.make_async_copy(src[a], out[a].at[me], loc_sems.at[a])
            lc.start()
            local.append(lc)
            for k in range(3):
                cp = ici(a, k, me)
                cp.start()
                sends.append(cp)
        for a in range(n):
            for k in range(3):
                px, py = chips[k]
                ici(a, k, 2 * px + py).wait_recv()
                if halve[a]:
                    fw = d2d(a, k, c)
                    fw.start()
                    sends.append(fw)
        for a in range(n):
            if halve[a]:
                for k in range(3):
                    d2d(a, k, 1 - c).wait_recv()
        for cp in sends:
            cp.wait_send()
        for lc in local:
            lc.wait()

    sem = pltpu.SemaphoreType.DMA((n, 3))
    return _pcall(body, name="gather_weights", out_shape=out_shape, in_specs=[HBM_SPEC] * n,
                  out_specs=[HBM_SPEC] * n, scratch=[sem, sem, sem, sem, pltpu.SemaphoreType.DMA((n,))])(*srcs)


SEM_SPEC = pl.BlockSpec(memory_space=pltpu.SEMAPHORE)


def _remote_start(srcs, lands, plan, ncopies, name):
    n, m = len(srcs), len(lands)

    def body(*refs):
        src, land = refs[:n], refs[n:n + m]
        send_sems, recv_sems = refs[n + m], refs[n + m + 1]
        token = refs[-1]
        x, y, c = _my_xyc()
        for i, (s, d, dev) in enumerate(plan(src, land, x, y, c)):
            pltpu.make_async_remote_copy(src_ref=s, dst_ref=d, send_sem=send_sems.at[i], recv_sem=recv_sems.at[i],
                                         device_id=dev, device_id_type=MESH_ID).start()
        token[...] = jnp.zeros_like(token)

    bufs = list(srcs) + list(lands)
    out = pl.pallas_call(
        body, name=name,
        out_shape=(pltpu.SemaphoreType.DMA((ncopies,)), pltpu.SemaphoreType.DMA((ncopies,)),
                   *[pltpu.HBM(b.shape, b.dtype) for b in bufs], jax.ShapeDtypeStruct((SUBLANES, LANES), F32)),
        in_specs=[HBM_SPEC] * (n + m),
        out_specs=(SEM_SPEC, SEM_SPEC, *[HBM_SPEC] * (n + m), pl.BlockSpec(memory_space=pltpu.VMEM)),
        input_output_aliases={i: 2 + i for i in range(n + m)},
        compiler_params=pltpu.CompilerParams(has_side_effects=pltpu.SideEffectType.DATAFLOW_SIDE_EFFECTING),
    )(*[pltpu.with_memory_space_constraint(b, pltpu.HBM) for b in bufs])
    return (out[0], out[1], out[2:2 + n], out[2 + n:2 + n + m]), out[-1]


def _remote_wait(handle, plan, after, name):
    send_sems, recv_sems, srcs, lands = handle
    n, m = len(srcs), len(lands)

    def body(*refs):
        src, land = refs[:n], refs[n:n + m]
        ssem, rsem = refs[n + m], refs[n + m + 1]
        x, y, c = _my_xyc()
        for i, (s, d, dev) in enumerate(plan(src, land, x, y, c)):
            cp = pltpu.make_async_remote_copy(src_ref=s, dst_ref=d, send_sem=ssem.at[i], recv_sem=rsem.at[i],
                                              device_id=dev, device_id_type=MESH_ID)
            cp.wait_send()
            cp.wait_recv()

    bufs = list(srcs) + list(lands)
    out = pl.pallas_call(
        body, name=name, out_shape=tuple(pltpu.HBM(b.shape, b.dtype) for b in bufs),
        in_specs=[HBM_SPEC] * (n + m) + [SEM_SPEC, SEM_SPEC, pl.BlockSpec(memory_space=pl.ANY)],
        out_specs=tuple([HBM_SPEC] * (n + m)), input_output_aliases={i: i for i in range(n + m)},
        compiler_params=pltpu.CompilerParams(has_side_effects=pltpu.SideEffectType.DATAFLOW_SIDE_EFFECTING),
    )(*bufs, send_sems, recv_sems, after)
    return out[:n], out[n:]


def _chips_of(x, y):
    return [(1 - x, y), (x, 1 - y), (1 - x, 1 - y)]


def _gather_half_plan(shapes):
    def plan(src, land, x, y, c):
        me = 2 * x + y
        out = []
        for a, shape in enumerate(shapes):
            h = shape[0] // 2
            rows = pl.ds(c * h, h)
            for px, py in _chips_of(x, y):
                out.append((src[a].at[rows], land[a].at[me, rows], (px, py, c)))
        return out

    return plan


def _forward_halves(srcs, lands):
    n = len(srcs)

    def body(*refs):
        src, land = refs[:n], refs[2 * n:3 * n]
        send_sems, recv_sems, loc_sems = refs[3 * n:]
        x, y, c = _my_xyc()
        me = 2 * x + y

        def fwd(a, k, which):
            px, py = _chips_of(x, y)[k]
            h = srcs[a].shape[0] // 2
            rows = land[a].at[2 * px + py, pl.ds(which * h, h)]
            return pltpu.make_async_remote_copy(
                src_ref=rows, dst_ref=rows, send_sem=send_sems.at[a, k], recv_sem=recv_sems.at[a, k],
                device_id=(x, y, 1 - c), device_id_type=MESH_ID)

        local, sends = [], []
        for a in range(n):
            lc = pltpu.make_async_copy(src[a], land[a].at[me], loc_sems.at[a])
            lc.start()
            local.append(lc)
            for k in range(3):
                cp = fwd(a, k, c)
                cp.start()
                sends.append(cp)
        for a in range(n):
            for k in range(3):
                fwd(a, k, 1 - c).wait_recv()
        for cp in sends:
            cp.wait_send()
        for lc in local:
            lc.wait()

    sem = pltpu.SemaphoreType.DMA((n, 3))
    return _pcall(body, name="forward_halves", out_shape=[jax.ShapeDtypeStruct(l.shape, l.dtype) for l in lands],
                  in_specs=[HBM_SPEC] * (2 * n), out_specs=[HBM_SPEC] * n, aliases={n + a: a for a in range(n)},
                  scratch=[sem, sem, pltpu.SemaphoreType.DMA((n,))])(*srcs, *lands)


def _send_other_half(grads, name):
    n = len(grads)
    out_shape = [jax.ShapeDtypeStruct((N_CHIPS, g.shape[1] // 2, g.shape[2]), g.dtype) for g in grads]

    def body(*refs):
        src, out = refs[:n], refs[n:2 * n]
        send_sems, recv_sems = refs[2 * n:]
        x, y, c = _my_xyc()
        cps = []
        for a in range(n):
            h = grads[a].shape[1] // 2
            cp = pltpu.make_async_remote_copy(
                src_ref=src[a].at[:, pl.ds((1 - c) * h, h), :], dst_ref=out[a],
                send_sem=send_sems.at[a], recv_sem=recv_sems.at[a],
                device_id=(x, y, 1 - c), device_id_type=MESH_ID)
            cp.start()
            cps.append(cp)
        for cp in cps:
            cp.wait()

    return _pcall(body, name=name, out_shape=out_shape, in_specs=[HBM_SPEC] * n,
                  out_specs=[HBM_SPEC] * n,
                  scratch=[pltpu.SemaphoreType.DMA((n,)), pltpu.SemaphoreType.DMA((n,))])(*grads)


def _share_halves(totals):
    n = len(totals)
    out_shape = [jax.ShapeDtypeStruct((2,) + t.shape, t.dtype) for t in totals]

    def body(*refs):
        src, out = refs[:n], refs[n:2 * n]
        send_sems, recv_sems, loc_sems = refs[2 * n:]
        x, y, c = _my_xyc()
        cps, local = [], []
        for a in range(n):
            lc = pltpu.make_async_copy(src[a], out[a].at[c], loc_sems.at[a])
            lc.start()
            local.append(lc)
            cp = pltpu.make_async_remote_copy(
                src_ref=src[a], dst_ref=out[a].at[c], send_sem=send_sems.at[a], recv_sem=recv_sems.at[a],
                device_id=(x, y, 1 - c), device_id_type=MESH_ID)
            cp.start()
            cps.append(cp)
        for cp in cps:
            cp.wait()
        for lc in local:
            lc.wait()

    sem = pltpu.SemaphoreType.DMA((n,))
    return _pcall(body, name="share_halves", out_shape=out_shape, in_specs=[HBM_SPEC] * n,
                  out_specs=[HBM_SPEC] * n, scratch=[sem, sem, sem])(*totals)


def _gather8(src, name):
    def body(src_ref, out_ref, send_sems, recv_sems, loc_sem):
        x, y, c = _my_xyc()
        me = 4 * x + 2 * y + c
        lc = pltpu.make_async_copy(src_ref, out_ref.at[me], loc_sem)
        lc.start()
        cps = []
        for k in range(1, N_DEV):
            px = 1 - x if (k >> 2) & 1 else x
            py = 1 - y if (k >> 1) & 1 else y
            pc = 1 - c if k & 1 else c
            cp = pltpu.make_async_remote_copy(
                src_ref=src_ref, dst_ref=out_ref.at[me], send_sem=send_sems.at[k - 1], recv_sem=recv_sems.at[k - 1],
                device_id=(px, py, pc), device_id_type=MESH_ID)
            cp.start()
            cps.append(cp)
        for cp in cps:
            cp.wait()
        lc.wait()

    return _pcall(body, name=name, out_shape=jax.ShapeDtypeStruct((N_DEV,) + src.shape, src.dtype),
                  in_specs=[HBM_SPEC], out_specs=HBM_SPEC,
                  scratch=[pltpu.SemaphoreType.DMA((N_DEV - 1,)), pltpu.SemaphoreType.DMA((N_DEV - 1,)),
                           pltpu.SemaphoreType.DMA])(src)


def _cast_shards(arrs, name, after=None):
    n = len(arrs)
    extra = [] if after is None else [after]

    def body(*refs):
        ins, outs = refs[:n], refs[n + len(extra):]
        for a in range(n):
            outs[a][...] = ins[a][...].astype(BF)

    specs = [pl.BlockSpec((s.shape[0] // 4, s.shape[1]), lambda i: (i, 0)) for s in arrs]
    return _pcall(body, name=name, grid=(4,), in_specs=specs + [pl.BlockSpec(memory_space=pl.ANY)] * len(extra),
                  out_specs=specs, out_shape=[jax.ShapeDtypeStruct(s.shape, BF) for s in arrs])(*arrs, *extra)


def _row_tile(rows, cols):
    t = rows
    while t * cols * 4 > (3 << 19) and t % 16 == 0:
        t //= 2
    return t


def _sum_parts(parts, name):
    p, rows, cols = parts.shape
    tr = _row_tile(rows, cols * p // 2)

    def body(p_ref, o_ref):
        acc = p_ref[0].astype(F32)
        for k in range(1, p):
            acc = acc + p_ref[k].astype(F32)
        o_ref[...] = acc

    return _pcall(body, name=name, grid=(rows // tr,),
                  in_specs=[pl.BlockSpec((p, tr, cols), lambda i: (0, i, 0))],
                  out_specs=pl.BlockSpec((tr, cols), lambda i: (i, 0)),
                  out_shape=jax.ShapeDtypeStruct((rows, cols), F32), vmem_mb=48)(parts)


def _add_halves_bf16(core, grad, other, name):
    nchip, rows, cols = grad.shape
    h = rows // 2
    tr = _row_tile(h, cols)
    nh = h // tr

    def body(c_ref, g_ref, o_ref, s_ref):
        del c_ref
        s_ref[...] = (g_ref[...] + o_ref[...]).astype(BF)

    grid_spec = pltpu.PrefetchScalarGridSpec(
        num_scalar_prefetch=1, grid=(nchip, nh),
        in_specs=[pl.BlockSpec((1, tr, cols), lambda p, i, c_ref: (p, c_ref[0] * nh + i, 0)),
                  pl.BlockSpec((1, tr, cols), lambda p, i, c_ref: (p, i, 0))],
        out_specs=pl.BlockSpec((1, tr, cols), lambda p, i, c_ref: (p, i, 0)))
    return _pcall(body, name=name, grid_spec=grid_spec, out_shape=jax.ShapeDtypeStruct((nchip, h, cols), BF),
                  vmem_mb=48)(core, grad, other)


def _sum_own_and_landed(chip, sums, landed, name):
    _, rows, cols = sums.shape
    tr = _row_tile(rows, 2 * cols)

    def body(chip_ref, own_ref, land_ref, o_ref):
        del chip_ref
        acc = own_ref[0].astype(F32)
        for k in range(3):
            acc = acc + land_ref[k].astype(F32)
        o_ref[...] = acc

    grid_spec = pltpu.PrefetchScalarGridSpec(
        num_scalar_prefetch=1, grid=(rows // tr,),
        in_specs=[pl.BlockSpec((1, tr, cols), lambda i, chip_ref: (chip_ref[0], i, 0)),
                  pl.BlockSpec((3, tr, cols), lambda i, chip_ref: (0, i, 0))],
        out_specs=pl.BlockSpec((tr, cols), lambda i, chip_ref: (i, 0)))
    return _pcall(body, name=name, grid_spec=grid_spec, out_shape=jax.ShapeDtypeStruct((rows, cols), F32),
                  vmem_mb=48)(chip, sums, landed)


def _scatter_plan(count):
    def plan(src, land, x, y, c):
        out = []
        for a in range(count):
            for k, (px, py) in enumerate(_chips_of(x, y)):
                out.append((src[a].at[2 * px + py], land[a].at[k], (px, py, c)))
        return out

    return plan


def _adamw_math(w, g, m, v):
    m2 = ADAM_B1 * m + (1.0 - ADAM_B1) * g
    v2 = ADAM_B2 * v + (1.0 - ADAM_B2) * (g * g)
    m_hat = m2 / (1.0 - ADAM_B1 ** ADAM_STEP)
    v_hat = v2 / (1.0 - ADAM_B2 ** ADAM_STEP)
    delta = -ADAM_LR * (m_hat / (jnp.sqrt(v_hat) + ADAM_EPS) + ADAM_WD * w)
    return delta, m2, v2


def _adamw(w, m, v, grads, name):
    rows, cols = w.shape
    tr = _row_tile(rows, cols)
    ng = len(grads)

    def body(*refs):
        w_ref, m_ref, v_ref = refs[:3]
        g = refs[3][...]
        for k in range(1, ng):
            g = g + refs[3 + k][...]
        g_ref, d_ref, m2_ref, v2_ref = refs[3 + ng:]
        delta, m2, v2 = _adamw_math(w_ref[...], g, m_ref[...], v_ref[...])
        g_ref[...] = g
        d_ref[...] = delta
        m2_ref[...] = m2
        v2_ref[...] = v2

    spec = pl.BlockSpec((tr, cols), lambda i: (i, 0))
    return _pcall(body, name=name, grid=(rows // tr,), in_specs=[spec] * (3 + ng), out_specs=[spec] * 4,
                  out_shape=[jax.ShapeDtypeStruct((rows, cols), F32)] * 4, vmem_mb=48)(w, m, v, *grads)


def _ada_adamw(ct, dmod, w, m, v):
    rows, cols = w.shape
    tr = _row_tile(rows, cols)

    def body(ct_ref, dm_ref, w_ref, m_ref, v_ref, g_ref, d_ref, m2_ref, v2_ref):
        cv = ct_ref[...]
        ca = cv * _sigmoid(cv)
        g = ca[:, 0:1] * dm_ref[0:1, :]
        for b in range(1, N_DEV):
            g = g + ca[:, b:b + 1] * dm_ref[b:b + 1, :]
        delta, m2, v2 = _adamw_math(w_ref[...], g, m_ref[...], v_ref[...])
        g_ref[...] = g
        d_ref[...] = delta
        m2_ref[...] = m2
        v2_ref[...] = v2

    spec = pl.BlockSpec((tr, cols), lambda i: (i, 0))
    return _pcall(body, name="ada_adamw", grid=(rows // tr,),
                  in_specs=[pl.BlockSpec((tr, N_DEV), lambda i: (i, 0)), pl.BlockSpec((N_DEV, cols), lambda i: (0, 0)),
                            spec, spec, spec],
                  out_specs=[spec] * 4, out_shape=[jax.ShapeDtypeStruct((rows, cols), F32)] * 4,
                  vmem_mb=48)(ct, dmod, w, m, v)


def _mod_fwd(c_all, w, b):
    cols = w.shape[1]
    tn = cols // 3

    def body(c_ref, w_ref, b_ref, o_ref):
        cv = c_ref[...]
        ca = (cv * _sigmoid(cv)).astype(BF)
        o_ref[...] = _dot(ca, w_ref[...].astype(BF)) + b_ref[...]

    return _pcall(body, name="mod_fwd", grid=(3,),
                  in_specs=[pl.BlockSpec((N_DEV, D), lambda j: (0, 0)), pl.BlockSpec((D, tn), lambda j: (0, j)),
                            pl.BlockSpec((1, tn), lambda j: (0, j))],
                  out_specs=pl.BlockSpec((N_DEV, tn), lambda j: (0, j)),
                  out_shape=jax.ShapeDtypeStruct((N_DEV, cols), F32))(c_all, w, b)


def _resident(shape):
    zeros = (0,) * len(shape)
    return pl.BlockSpec(shape, lambda *_: zeros, pipeline_mode=pl.Buffered(1))


def _modnorm_matmul(x, g, scale, shift, w4, name, tm=256):
    T = x.shape[0]
    tm = min(tm, T)
    ns = w4.shape[2]

    def body(x_ref, g_ref, sc_ref, sh_ref, w_ref, h_ref, z_ref):
        _, xh = _rms_stats(x_ref[...])
        h = ((xh * g_ref[...]) * (1.0 + sc_ref[...]) + sh_ref[...]).astype(BF)
        h_ref[...] = h
        for j in range(N_CHIPS):
            z_ref[:, j * ns:(j + 1) * ns] = _dot(h, w_ref[j])

    vec = pl.BlockSpec((1, D), lambda i: (0, 0))
    return _pcall(body, name=name, grid=(T // tm,),
                  in_specs=[pl.BlockSpec((tm, D), lambda i: (i, 0)), vec, vec, vec, _resident(w4.shape)],
                  out_specs=[pl.BlockSpec((tm, D), lambda i: (i, 0)), pl.BlockSpec((tm, N_CHIPS * ns), lambda i: (i, 0))],
                  out_shape=[jax.ShapeDtypeStruct((T, D), BF), jax.ShapeDtypeStruct((T, N_CHIPS * ns), F32)],
                  vmem_mb=48)(x, g, scale, shift, w4)


def _rglru_fwd(z, cw, cb, wa, ba, wx, bx, lam, tb=256):
    T = z.shape[0]
    tb = min(tb, T)

    def body(xr_ref, gr_ref, cw_ref, cb_ref, wa_ref, ba_ref, wx_ref, bx_ref, lam_ref, h_ref, ya_ref, prev, hc):
        i = pl.program_id(0)

        @pl.when(i == 0)
        def _():
            prev[...] = jnp.zeros_like(prev)
            hc[...] = jnp.zeros_like(hc)

        xr = xr_ref[...]
        pv = prev[...]
        xc = (cb_ref[...] + cw_ref[3:4, :] * xr + cw_ref[2:3, :] * _shift_down(xr, pv, 1)
              + cw_ref[1:2, :] * _shift_down(xr, pv, 2) + cw_ref[0:1, :] * _shift_down(xr, pv, 3))
        prev[...] = xr[tb - SUBLANES:tb]
        _, ig, _, a, mult = _lru_gates(xc, wa_ref, ba_ref[...], wx_ref, bx_ref[...], lam_ref[...])
        a, u = _scan_rows(a, mult * (ig * xc), reverse=False)
        h = u + a * hc[SUBLANES - 1:SUBLANES, :]
        hc[...] = h[tb - SUBLANES:tb]
        h_ref[...] = h
        ya_ref[...] = (h * _gelu(gr_ref[...])).astype(BF)

    vec = pl.BlockSpec((1, D), lambda i: (0, 0))
    wspec = pl.BlockSpec((HEADS, HD, HD), lambda i: (0, 0, 0))
    return _pcall(body, name="rglru_fwd", grid=(T // tb,),
                  in_specs=[pl.BlockSpec((tb, D), lambda i: (i, 0)), pl.BlockSpec((tb, D), lambda i: (i, 1)),
                            pl.BlockSpec((4, D), lambda i: (0, 0)), vec, wspec, vec, wspec, vec, vec],
                  out_specs=[pl.BlockSpec((tb, D), lambda i: (i, 0))] * 2,
                  out_shape=[jax.ShapeDtypeStruct((T, D), F32), jax.ShapeDtypeStruct((T, D), BF)],
                  scratch=[pltpu.VMEM((SUBLANES, D), F32), pltpu.VMEM((SUBLANES, D), F32)],
                  vmem_mb=48)(z, z, cw, cb, wa, ba, wx, bx, lam)


def _sgu_fwd(z, lg, lb, ws, bst, tb=256):
    T = z.shape[0]
    tb = min(tb, T)

    def body(zu_ref, zv_ref, lg_ref, lb_ref, ws_ref, bst_ref, yb_ref):
        _, xh = _layernorm_stats(_gelu(zv_ref[...]))
        vln = xh * lg_ref[...] + lb_ref[...]
        _, mixed = _sgu_mix(vln, ws_ref, bst_ref, tb)
        yb_ref[...] = (_gelu(zu_ref[...]) * mixed).astype(BF)

    vec = pl.BlockSpec((1, D), lambda i: (0, 0))
    return _pcall(body, name="sgu_fwd", grid=(T // tb,),
                  in_specs=[pl.BlockSpec((tb, D), lambda i: (i, 2)), pl.BlockSpec((tb, D), lambda i: (i, 3)), vec, vec,
                            pl.BlockSpec((HEADS, SGU_BLOCK, SGU_BLOCK), lambda i: (0, 0, 0)),
                            pl.BlockSpec((SGU_BLOCK, HEADS), lambda i: (0, 0))],
                  out_specs=pl.BlockSpec((tb, D), lambda i: (i, 0)),
                  out_shape=jax.ShapeDtypeStruct((T, D), BF))(z, z, lg, lb, ws, bst)


def _mix_out(ya_pre, yb_pre, z, x, gate1, wba, wbb, wo, tm=256):
    T = x.shape[0]
    tm = min(tm, T)

    def body(yap_ref, ybp_ref, ga_ref, gb_ref, x_ref, g1_ref, wa_ref, wb_ref, wo_ref,
             x2_ref, mg_ref, ya_ref, yb_ref, o_ref):
        ya = _dot(yap_ref[...], wa_ref[...])
        yb = _dot(ybp_ref[...], wb_ref[...])
        merged = (_sigmoid_t(ga_ref[...]) * ya + _sigmoid_t(gb_ref[...]) * yb).astype(BF)
        o = _dot(merged, wo_ref[...])
        x2_ref[...] = x_ref[...] + g1_ref[...] * o
        mg_ref[...] = merged
        ya_ref[...] = ya.astype(BF)
        yb_ref[...] = yb.astype(BF)
        o_ref[...] = o.astype(BF)

    row = pl.BlockSpec((tm, D), lambda i: (i, 0))
    wspec = pl.BlockSpec((D, D), lambda i: (0, 0))
    return _pcall(body, name="mix_out", grid=(T // tm,),
                  in_specs=[row, row, pl.BlockSpec((tm, D), lambda i: (i, 4)), pl.BlockSpec((tm, D), lambda i: (i, 5)),
                            row, pl.BlockSpec((1, D), lambda i: (0, 0)), wspec, wspec, wspec],
                  out_specs=[row] * 5,
                  out_shape=[jax.ShapeDtypeStruct((T, D), F32)] + [jax.ShapeDtypeStruct((T, D), BF)] * 4,
                  vmem_mb=48)(ya_pre, yb_pre, z, z, x, gate1, wba, wbb, wo)


def _ffn_gate(up, cw, cb, tm=512, cw_blk=768):
    T = up.shape[0]
    tm = min(tm, T)
    dff = up.shape[1] // 2
    ncb = dff // cw_blk

    def body(ua_ref, uv_ref, wa_ref, wv_ref, ba_ref, bv_ref, f_ref, ga_ref, vd_ref, pa, pv):
        i = pl.program_id(1)

        @pl.when(i == 0)
        def _():
            pa[...] = jnp.zeros_like(pa)
            pv[...] = jnp.zeros_like(pv)

        def conv(u_ref, w_ref, b_ref, prev):
            u = u_ref[...]
            p = prev[...]
            hid = (b_ref[...] + w_ref[2:3, :] * u + w_ref[1:2, :] * _shift_down(u, p, 1)
                   + w_ref[0:1, :] * _shift_down(u, p, 2))
            prev[...] = u[tm - SUBLANES:tm]
            return hid

        act = conv(ua_ref, wa_ref, ba_ref, pa)
        val = conv(uv_ref, wv_ref, bv_ref, pv)
        ga, dga = _gelu_and_grad(act)
        f_ref[...] = (ga * val).astype(BF)
        ga_ref[...] = ga.astype(BF)
        vd_ref[...] = (val * dga).astype(BF)

    blk = pl.BlockSpec((tm, cw_blk), lambda cbk, i: (i, cbk))
    return _pcall(body, name="ffn_gate", grid=(ncb, T // tm),
                  in_specs=[pl.BlockSpec((tm, cw_blk), lambda cbk, i: (i, cbk)),
                            pl.BlockSpec((tm, cw_blk), lambda cbk, i: (i, ncb + cbk)),
                            pl.BlockSpec((3, cw_blk), lambda cbk, i: (0, cbk)),
                            pl.BlockSpec((3, cw_blk), lambda cbk, i: (0, ncb + cbk)),
                            pl.BlockSpec((1, cw_blk), lambda cbk, i: (0, cbk)),
                            pl.BlockSpec((1, cw_blk), lambda cbk, i: (0, ncb + cbk))],
                  out_specs=[blk] * 3, out_shape=[jax.ShapeDtypeStruct((T, dff), BF)] * 3,
                  scratch=[pltpu.VMEM((SUBLANES, cw_blk), F32)] * 2)(up, up, cw, cw, cb, cb)


def _ffn_down_loss(f, wd, x2, gate2, gf, target, tm=512):
    T = x2.shape[0]
    tm = min(tm, T)
    dff = f.shape[1]

    def body(f_ref, wd_ref, x2_ref, g2_ref, gf_ref, t_ref, loss_ref, dx3_ref, dfo_ref, dgf_ref, dg2_ref):
        i = pl.program_id(0)

        @pl.when(i == 0)
        def _():
            loss_ref[...] = jnp.zeros_like(loss_ref)
            dgf_ref[...] = jnp.zeros_like(dgf_ref)
            dg2_ref[...] = jnp.zeros_like(dg2_ref)

        fo = _dot(f_ref[...], wd_ref[...])
        x3 = x2_ref[...] + g2_ref[...] * fo
        rstd, xh = _rms_stats(x3)
        err = xh * gf_ref[...] - t_ref[...]
        loss_ref[...] += 0.5 * jnp.sum(jnp.mean(err * err, axis=-1, keepdims=True), axis=0, keepdims=True)
        dy = err * (1.0 / D)
        dgf_ref[...] += _colsum(dy * xh)
        dxh = dy * gf_ref[...]
        dx3 = rstd * (dxh - xh * jnp.mean(dxh * xh, axis=-1, keepdims=True))
        dg2_ref[...] += _colsum(dx3 * fo)
        dx3_ref[...] = dx3
        dfo_ref[...] = (g2_ref[...] * dx3).astype(BF)

    row = pl.BlockSpec((tm, D), lambda i: (i, 0))
    vec = pl.BlockSpec((1, D), lambda i: (0, 0))
    return _pcall(body, name="ffn_down_loss", grid=(T // tm,),
                  in_specs=[pl.BlockSpec((tm, dff), lambda i: (i, 0)), pl.BlockSpec((dff, D), lambda i: (0, 0)),
                            row, vec, vec, row],
                  out_specs=[pl.BlockSpec((1, LANES), lambda i: (0, 0)), row, row, vec, vec],
                  out_shape=[jax.ShapeDtypeStruct((1, LANES), F32), jax.ShapeDtypeStruct((T, D), F32),
                             jax.ShapeDtypeStruct((T, D), BF), jax.ShapeDtypeStruct((1, D), F32),
                             jax.ShapeDtypeStruct((1, D), F32)],
                  vmem_mb=48)(f, wd, x2, gate2, gf, target)


def _ffn_bwd(dfo, wd, up, ga, vd, cw, tm=256, cw_blk=1536):
    T = up.shape[0]
    tm = min(tm, T)
    dff = up.shape[1] // 2
    ncb = dff // cw_blk
    nrow = T // tm

    def body(dfo_ref, wd_ref, ua_ref, uv_ref, ga_ref, vd_ref, wa_ref, wv_ref,
             du_ref, dwa_ref, dwv_ref, dba_ref, dbv_ref, na, nv):
        i = pl.program_id(1)

        @pl.when(i == 0)
        def _():
            na[...] = jnp.zeros_like(na)
            nv[...] = jnp.zeros_like(nv)
            dwa_ref[...] = jnp.zeros_like(dwa_ref)
            dwv_ref[...] = jnp.zeros_like(dwv_ref)
            dba_ref[...] = jnp.zeros_like(dba_ref)
            dbv_ref[...] = jnp.zeros_like(dbv_ref)

        df = _dot_nt(dfo_ref[...], wd_ref[...])

        def conv_bwd(dh, u_ref, w_ref, nxt, col, dw_ref, db_ref):
            n8 = nxt[...]
            dh1 = _shift_up(dh, n8, 1)
            dh2 = _shift_up(dh, n8, 2)
            nxt[...] = dh[0:SUBLANES]
            du_ref[:, col:col + cw_blk] = (w_ref[2:3, :] * dh + w_ref[1:2, :] * dh1 + w_ref[0:1, :] * dh2).astype(BF)
            u = u_ref[...]
            dw_ref[2:3, :] += _colsum(dh * u)
            dw_ref[1:2, :] += _colsum(dh1 * u)
            dw_ref[0:1, :] += _colsum(dh2 * u)
            db_ref[...] += _colsum(dh)

        conv_bwd(df * vd_ref[...].astype(F32), ua_ref, wa_ref, na, 0, dwa_ref, dba_ref)
        conv_bwd(df * ga_ref[...].astype(F32), uv_ref, wv_ref, nv, cw_blk, dwv_ref, dbv_ref)

    rev = lambda cbk, i: (nrow - 1 - i, cbk)
    rev_v = lambda cbk, i: (nrow - 1 - i, ncb + cbk)
    blk = pl.BlockSpec((tm, cw_blk), rev)
    w3a = pl.BlockSpec((3, cw_blk), lambda cbk, i: (0, cbk))
    w3v = pl.BlockSpec((3, cw_blk), lambda cbk, i: (0, ncb + cbk))
    b1a = pl.BlockSpec((1, cw_blk), lambda cbk, i: (0, cbk))
    return _pcall(body, name="ffn_bwd", grid=(ncb, nrow),
                  in_specs=[pl.BlockSpec((tm, D), lambda cbk, i: (nrow - 1 - i, 0)),
                            pl.BlockSpec((cw_blk, D), lambda cbk, i: (cbk, 0)),
                            blk, pl.BlockSpec((tm, cw_blk), rev_v), blk, blk, w3a, w3v],
                  out_specs=[pl.BlockSpec((tm, 2 * cw_blk), rev), w3a, w3a, b1a, b1a],
                  out_shape=[jax.ShapeDtypeStruct((T, 2 * dff), BF),
                             jax.ShapeDtypeStruct((3, dff), F32), jax.ShapeDtypeStruct((3, dff), F32),
                             jax.ShapeDtypeStruct((1, dff), F32), jax.ShapeDtypeStruct((1, dff), F32)],
                  scratch=[pltpu.VMEM((SUBLANES, cw_blk), F32)] * 2,
                  vmem_mb=48)(dfo, wd, up, up, ga, vd, cw, cw)


def _ffn_col_block(t, ncb):
    return jnp.where(t < ncb, 2 * t, 2 * (t - ncb) + 1)


def _mm_tn_cols(a, b, name, nshard, nb, colmap=None, mb=None, tm=1024):
    T, M = a.shape
    tm = min(tm, T)
    mb = M if mb is None else mb
    ns = b.shape[1] // nshard
    per = ns // nb
    cmap = colmap if colmap is not None else (lambda t: t)

    def body(a_ref, b_ref, o_ref):
        k = pl.program_id(2)

        @pl.when(k == 0)
        def _():
            o_ref[...] = jnp.zeros_like(o_ref)

        o_ref[0] += _dot_tn(a_ref[...], b_ref[...])

    return _pcall(body, name=name, grid=(M // mb, nshard * per, T // tm),
                  in_specs=[pl.BlockSpec((tm, mb), lambda m, t, k: (k, m)),
                            pl.BlockSpec((tm, nb), lambda m, t, k: (k, cmap(t)))],
                  out_specs=pl.BlockSpec((1, mb, nb), lambda m, t, k: (t // per, m, t % per)),
                  out_shape=jax.ShapeDtypeStruct((nshard, M, ns), F32), vmem_mb=48)(a, b)


def _mm_nt_normbwd(dz, w4, x, resid, g, scale, name, gate=None, o=None, dz_blocks=(0, 1, 2, 3), tm=256):
    T = x.shape[0]
    tm = min(tm, T)
    ns = w4.shape[2]
    gated = gate is not None

    def body(*refs):
        if gated:
            (dz_ref, w_ref, x_ref, r_ref, g_ref, sc_ref, gt_ref, o_ref,
             dx_ref, dsh_ref, dsc_ref, dg_ref, do_ref, dgt_ref) = refs
        else:
            dz_ref, w_ref, x_ref, r_ref, g_ref, sc_ref, dx_ref, dsh_ref, dsc_ref, dg_ref = refs
        i = pl.program_id(0)

        @pl.when(i == 0)
        def _():
            dsh_ref[...] = jnp.zeros_like(dsh_ref)
            dsc_ref[...] = jnp.zeros_like(dsc_ref)
            dg_ref[...] = jnp.zeros_like(dg_ref)
            if gated:
                dgt_ref[...] = jnp.zeros_like(dgt_ref)

        dh = None
        for j in range(N_CHIPS):
            blk = dz_blocks[j]
            part = _dot_nt(dz_ref[:, blk * ns:(blk + 1) * ns], w_ref[j])
            dh = part if dh is None else dh + part
        rstd, xh = _rms_stats(x_ref[...])
        dsh_ref[...] += _colsum(dh)
        dsc_ref[...] += _colsum(dh * (xh * g_ref[...]))
        dn = dh * (1.0 + sc_ref[...])
        dg_ref[...] += _colsum(dn * xh)
        dxh = dn * g_ref[...]
        dx = r_ref[...] + rstd * (dxh - xh * jnp.mean(dxh * xh, axis=-1, keepdims=True))
        dx_ref[...] = dx
        if gated:
            do_ref[...] = (gt_ref[...] * dx).astype(BF)
            dgt_ref[...] += _colsum(dx * o_ref[...].astype(F32))

    row = pl.BlockSpec((tm, D), lambda i: (i, 0))
    vec = pl.BlockSpec((1, D), lambda i: (0, 0))
    in_specs = [pl.BlockSpec((tm, N_CHIPS * ns), lambda i: (i, 0)), _resident(w4.shape), row, row, vec, vec]
    out_specs = [row, vec, vec, vec]
    out_shape = [jax.ShapeDtypeStruct((T, D), F32)] + [jax.ShapeDtypeStruct((1, D), F32)] * 3
    args = [dz, w4, x, resid, g, scale]
    if gated:
        in_specs += [vec, row]
        out_specs += [row, vec]
        out_shape += [jax.ShapeDtypeStruct((T, D), BF), jax.ShapeDtypeStruct((1, D), F32)]
        args += [gate, o]
    return _pcall(body, name=name, grid=(T // tm,), in_specs=in_specs, out_specs=out_specs, out_shape=out_shape,
                  vmem_mb=48)(*args)


def _mix_bwd(do, ya, yb, z, wo, wba, wbb, tm=256):
    T = do.shape[0]
    tm = min(tm, T)

    def body(do_ref, ya_ref, yb_ref, ga_ref, gb_ref, wo_ref, wa_ref, wb_ref,
             dz_ref, dya_ref, dyb_ref, dyap_ref, dybp_ref):
        dm = _dot_nt(do_ref[...], wo_ref[...])
        sa = _sigmoid_t(ga_ref[...])
        sb = _sigmoid_t(gb_ref[...])
        dya = (sa * dm).astype(BF)
        dyb = (sb * dm).astype(BF)
        dz_ref[:, 0:D] = (dm * ya_ref[...].astype(F32) * sa * (1.0 - sa)).astype(BF)
        dz_ref[:, D:2 * D] = (dm * yb_ref[...].astype(F32) * sb * (1.0 - sb)).astype(BF)
        dya_ref[...] = dya
        dyb_ref[...] = dyb
        dyap_ref[...] = _dot_nt(dya, wa_ref[...]).astype(BF)
        dybp_ref[...] = _dot_nt(dyb, wb_ref[...]).astype(BF)

    row = pl.BlockSpec((tm, D), lambda i: (i, 0))
    wspec = pl.BlockSpec((D, D), lambda i: (0, 0))
    return _pcall(body, name="mix_bwd", grid=(T // tm,),
                  in_specs=[row, row, row, pl.BlockSpec((tm, D), lambda i: (i, 4)),
                            pl.BlockSpec((tm, D), lambda i: (i, 5)), wspec, wspec, wspec],
                  out_specs=[pl.BlockSpec((tm, 2 * D), lambda i: (i, 2)), row, row, row, row],
                  out_shape=[jax.ShapeDtypeStruct((T, 6 * D), BF)] + [jax.ShapeDtypeStruct((T, D), BF)] * 4,
                  vmem_mb=48)(do, ya, yb, z, z, wo, wba, wbb)


def _sgu_bwd(dz, dyb_pre, z, lg, lb, ws, bst, tb=256):
    T = z.shape[0]
    tb = min(tb, T)

    def body(dz_in, dy_ref, zu_ref, zv_ref, lg_ref, lb_ref, ws_ref, bst_ref,
             dz_ref, dws_ref, dbst_ref, dlg_ref, dlb_ref):
        del dz_in
        i = pl.program_id(0)

        @pl.when(i == 0)
        def _():
            dws_ref[...] = jnp.zeros_like(dws_ref)
            dbst_ref[...] = jnp.zeros_like(dbst_ref)
            dlg_ref[...] = jnp.zeros_like(dlg_ref)
            dlb_ref[...] = jnp.zeros_like(dlb_ref)

        gu, dgu = _gelu_and_grad(zu_ref[...])
        gv, dgv = _gelu_and_grad(zv_ref[...])
        rstd, xh = _layernorm_stats(gv)
        vln = xh * lg_ref[...] + lb_ref[...]
        wm, mixed = _sgu_mix(vln, ws_ref, bst_ref, tb)
        dy = dy_ref[...].astype(F32)
        dz_ref[:, 0:D] = (dy * mixed * dgu).astype(BF)
        dmixed = dy * gu
        ri = lax.broadcasted_iota(jnp.int32, (SGU_BLOCK, SGU_BLOCK), 0)
        ci = lax.broadcasted_iota(jnp.int32, (SGU_BLOCK, SGU_BLOCK), 1)
        blocks = []
        for blk in range(tb // SGU_BLOCK):
            rs = slice(blk * SGU_BLOCK, (blk + 1) * SGU_BLOCK)
            cols = []
            for g in range(HEADS):
                cs = slice(g * HD, (g + 1) * HD)
                dmg = dmixed[rs, cs]
                dmb = dmg.astype(BF)
                dbst_ref[:, g:g + 1] += jnp.sum(dmg, axis=1, keepdims=True)
                dws_ref[g] += jnp.where(ri >= ci, _dot_nt(dmb, vln[rs, cs].astype(BF)), 0.0)
                cols.append(_dot_tn(wm[g], dmb))
            blocks.append(jnp.concatenate(cols, axis=1))
        dvln = blocks[0] if len(blocks) == 1 else jnp.concatenate(blocks, axis=0)
        dlg_ref[...] += _colsum(dvln * xh)
        dlb_ref[...] += _colsum(dvln)
        dxh = dvln * lg_ref[...]
        dgv_in = rstd * (dxh - jnp.mean(dxh, axis=-1, keepdims=True)
                         - xh * jnp.mean(dxh * xh, axis=-1, keepdims=True))
        dz_ref[:, D:2 * D] = (dgv_in * dgv).astype(BF)

    row = pl.BlockSpec((tb, D), lambda i: (i, 0))
    vec = pl.BlockSpec((1, D), lambda i: (0, 0))
    wspec = pl.BlockSpec((HEADS, SGU_BLOCK, SGU_BLOCK), lambda i: (0, 0, 0))
    bspec = pl.BlockSpec((SGU_BLOCK, HEADS), lambda i: (0, 0))
    return _pcall(body, name="sgu_bwd", grid=(T // tb,),
                  in_specs=[HBM_SPEC, row, pl.BlockSpec((tb, D), lambda i: (i, 2)),
                            pl.BlockSpec((tb, D), lambda i: (i, 3)), vec, vec, wspec, bspec],
                  out_specs=[pl.BlockSpec((tb, 2 * D), lambda i: (i, 1)), wspec, bspec, vec, vec],
                  out_shape=[jax.ShapeDtypeStruct(dz.shape, BF),
                             jax.ShapeDtypeStruct((HEADS, SGU_BLOCK, SGU_BLOCK), F32),
                             jax.ShapeDtypeStruct((SGU_BLOCK, HEADS), F32),
                             jax.ShapeDtypeStruct((1, D), F32), jax.ShapeDtypeStruct((1, D), F32)],
                  aliases={0: 0}, vmem_mb=48)(dz, dyb_pre, z, z, lg, lb, ws, bst)


def _rglru_bwd(dz, dya_pre, z, h, cw, cb, wa, ba, wx, bx, lam, tb=256):
    T = z.shape[0]
    tb = min(tb, T)
    nrow = T // tb
    per = tb // SUBLANES

    def body(dz_in, dy_ref, xr_ref, xh_ref, gr_ref, h_ref, hh_ref, cw_ref, cb_ref, wa_ref, ba_ref, wx_ref, bx_ref,
             lam_ref, dz_ref, dcw_ref, dcb_ref, dwa_ref, dba_ref, dwx_ref, dbx_ref, dlam_ref, carry, nxt):
        del dz_in
        i = pl.program_id(0)
        first_block = i == nrow - 1

        @pl.when(i == 0)
        def _():
            carry[...] = jnp.zeros_like(carry)
            nxt[...] = jnp.zeros_like(nxt)
            for ref in (dcw_ref, dcb_ref, dwa_ref, dba_ref, dwx_ref, dbx_ref, dlam_ref):
                ref[...] = jnp.zeros_like(ref)

        xr = xr_ref[...]
        pv = jnp.where(first_block, 0.0, xh_ref[...])
        s1 = _shift_down(xr, pv, 1)
        s2 = _shift_down(xr, pv, 2)
        s3 = _shift_down(xr, pv, 3)
        xc = cb_ref[...] + cw_ref[3:4, :] * xr + cw_ref[2:3, :] * s1 + cw_ref[1:2, :] * s2 + cw_ref[0:1, :] * s3
        lam = lam_ref[...]
        r, ig, ls, a, mult = _lru_gates(xc, wa_ref, ba_ref[...], wx_ref, bx_ref[...], lam)
        hv = h_ref[...]
        hprev = _shift_down(hv, jnp.where(first_block, 0.0, hh_ref[...]), 1)
        gg, dgg = _gelu_and_grad(gr_ref[...])
        dy = dy_ref[...].astype(F32)
        dz_ref[:, D:2 * D] = (dy * hv * dgg).astype(BF)

        rows = lax.broadcasted_iota(jnp.int32, (tb, D), 0)
        v = dy * gg + jnp.where(rows == tb - 1, carry[0:1, :], 0.0)
        q = jnp.where(rows < tb - 1, pltpu.roll(a, tb - 1, 0), 0.0)
        _, gsc = _scan_rows(q, v, reverse=True)
        carry[...] = (a * gsc)[0:SUBLANES]

        xi = ig * xc
        dmult = gsc * xi
        dxi = gsc * mult
        dig = dxi * xc
        dxc = dxi * ig
        dlog_a = gsc * hprev * a - dmult * (a * a) / mult
        dlam_ref[...] += _colsum(dlog_a * r) * (LRU_C * _sigmoid(-lam))
        dpr = dlog_a * (LRU_C * ls) * r * (1.0 - r)
        dpi = dig * ig * (1.0 - ig)
        dba_ref[...] += _colsum(dpr)
        dbx_ref[...] += _colsum(dpi)
        back = []
        for hh in range(HEADS):
            cs = slice(hh * HD, (hh + 1) * HD)
            xh = xc[:, cs].astype(BF)
            dprh = dpr[:, cs].astype(BF)
            dpih = dpi[:, cs].astype(BF)
            dwa_ref[hh] += _dot_tn(xh, dprh)
            dwx_ref[hh] += _dot_tn(xh, dpih)
            back.append(_dot_nt(dprh, wa_ref[hh].astype(BF)) + _dot_nt(dpih, wx_ref[hh].astype(BF)))
        dxc = dxc + jnp.concatenate(back, axis=1)

        n8 = nxt[...]
        dxr = (cw_ref[3:4, :] * dxc + cw_ref[2:3, :] * _shift_up(dxc, n8, 1)
               + cw_ref[1:2, :] * _shift_up(dxc, n8, 2) + cw_ref[0:1, :] * _shift_up(dxc, n8, 3))
        nxt[...] = dxc[0:SUBLANES]
        dz_ref[:, 0:D] = dxr.astype(BF)
        dcw_ref[3:4, :] += _colsum(dxc * xr)
        dcw_ref[2:3, :] += _colsum(dxc * s1)
        dcw_ref[1:2, :] += _colsum(dxc * s2)
        dcw_ref[0:1, :] += _colsum(dxc * s3)
        dcb_ref[...] += _colsum(dxc)

    rev = lambda col: (lambda i: (nrow - 1 - i, col))
    halo = lambda col: pl.BlockSpec((SUBLANES, D), lambda i: (jnp.maximum((nrow - 1 - i) * per - 1, 0), col))
    vec = pl.BlockSpec((1, D), lambda i: (0, 0))
    wspec = pl.BlockSpec((HEADS, HD, HD), lambda i: (0, 0, 0))
    c4 = pl.BlockSpec((4, D), lambda i: (0, 0))
    wshape = jax.ShapeDtypeStruct((HEADS, HD, HD), F32)
    vshape = jax.ShapeDtypeStruct((1, D), F32)
    return _pcall(body, name="rglru_bwd", grid=(nrow,),
                  in_specs=[HBM_SPEC, pl.BlockSpec((tb, D), rev(0)), pl.BlockSpec((tb, D), rev(0)), halo(0),
                            pl.BlockSpec((tb, D), rev(1)), pl.BlockSpec((tb, D), rev(0)), halo(0),
                            c4, vec, wspec, vec, wspec, vec, vec],
                  out_specs=[pl.BlockSpec((tb, 2 * D), rev(0)), c4, vec, wspec, vec, wspec, vec, vec],
                  out_shape=[jax.ShapeDtypeStruct(dz.shape, BF), jax.ShapeDtypeStruct((4, D), F32), vshape,
                             wshape, vshape, wshape, vshape, vshape],
                  scratch=[pltpu.VMEM((SUBLANES, D), F32), pltpu.VMEM((SUBLANES, D), F32)],
                  aliases={0: 0}, vmem_mb=56)(dz, dya_pre, z, z, z, h, h, cw, cb, wa, ba, wx, bx, lam)


def _pack_rows(parts):
    out = []
    for p in parts:
        q = p.reshape(-1, LANES)
        pad = (-q.shape[0]) % SUBLANES
        if pad:
            q = jnp.concatenate([q, jnp.zeros((pad, LANES), q.dtype)], axis=0)
        out.append(q)
    return jnp.concatenate(out, axis=0)


def _rows_of(shape):
    n = 1
    for s in shape:
        n *= s
    rows = n // LANES
    return rows + (-rows) % SUBLANES


def kernel(x, c, w_ada, b_ada, norm_mix_g, w_in, rnn_conv_w, rnn_conv_b, lru_w_a, lru_b_a, lru_w_x, lru_b_x, lru_lambda, sgu_ln_g, sgu_ln_b, sgu_w_s, sgu_b_s, w_branch_a, w_branch_b, w_out, norm_ffn_g, w_up, ffn_conv_w, ffn_conv_b, w_down, norm_final_g, loss_target, m_w_ada, m_b_ada, m_norm_mix_g, m_w_in, m_rnn_conv_w, m_rnn_conv_b, m_lru_w_a, m_lru_b_a, m_lru_w_x, m_lru_b_x, m_lru_lambda, m_sgu_ln_g, m_sgu_ln_b, m_sgu_w_s, m_sgu_b_s, m_w_branch_a, m_w_branch_b, m_w_out, m_norm_ffn_g, m_w_up, m_ffn_conv_w, m_ffn_conv_b, m_w_down, m_norm_final_g, v_w_ada, v_b_ada, v_norm_mix_g, v_w_in, v_rnn_conv_w, v_rnn_conv_b, v_lru_w_a, v_lru_b_a, v_lru_w_x, v_lru_b_x, v_lru_lambda, v_sgu_ln_g, v_sgu_ln_b, v_sgu_w_s, v_sgu_b_s, v_w_branch_a, v_w_branch_b, v_w_out, v_norm_ffn_g, v_w_up, v_ffn_conv_w, v_ffn_conv_b, v_w_down, v_norm_final_g):
    args = dict(locals())
    T = x.shape[1]
    mx, my, mc = lax.axis_index("x"), lax.axis_index("y"), lax.axis_index("c")
    chip = 2 * mx + my
    dev = 2 * chip + mc
    vec = lambda a: a.reshape(1, -1)

    xt = x.reshape(T, D)
    tgt = loss_target.reshape(T, D)
    ns = w_in.shape[2]
    dff = w_down.shape[1] * N_CHIPS

    c_all = _gather8(c.reshape(SUBLANES, LANES), "gather_c").reshape(N_DEV, D)
    b_ada_sh = lax.dynamic_slice(b_ada, (0, chip * ns), (1, ns))
    mod_sh = _mod_fwd(c_all, w_ada[0], b_ada_sh)

    (w_in_b,) = _cast_shards([w_in[0]], "cast_w_in")
    w_in4, rcw4, fcw4, mod4 = _gather_weights([w_in_b, rnn_conv_w[0], ffn_conv_w[0], mod_sh],
                                              [True, False, False, False])
    late = _cast_shards([w_up[0], w_down[0], w_branch_a[0], w_branch_b[0], w_out[0]], "cast_late", after=mod4)
    late_plan = _gather_half_plan([w.shape for w in late])
    late_handle, late_token = _remote_start(
        late, [lax.empty((N_CHIPS,) + w.shape, w.dtype) for w in late], late_plan, 3 * len(late), "gather_late_start")
    rcw_full = jnp.transpose(rcw4, (1, 0, 2)).reshape(4, D)
    fcw_full = jnp.transpose(fcw4, (1, 0, 2)).reshape(3, 2 * dff)
    mod = lax.dynamic_index_in_dim(mod4, dev, axis=1, keepdims=False).reshape(1, 6 * D)
    shift1, scale1, gate1, shift2, scale2, gate2 = [mod[:, k * D:(k + 1) * D] for k in range(6)]

    h1, z = _modnorm_matmul(xt, norm_mix_g, scale1 + late_token[0:1, 0:1], shift1, w_in4, "norm_in_proj")
    bst = jnp.transpose(sgu_b_s[0])
    h_lru, ya_pre = _rglru_fwd(z, rcw_full, rnn_conv_b, lru_w_a[0], lru_b_a, lru_w_x[0], lru_b_x, lru_lambda)
    yb_pre = _sgu_fwd(z, sgu_ln_g, sgu_ln_b, sgu_w_s[0], bst)
    late, late_lands = _remote_wait(late_handle, late_plan, yb_pre, "gather_late_wait")
    w_up4, w_down4, wba4, wbb4, wo4 = _forward_halves(late, late_lands)
    wd_full = w_down4.reshape(dff, D)
    wba_full = wba4.reshape(D, D)
    wbb_full = wbb4.reshape(D, D)
    wo_full = wo4.reshape(D, D)
    x2, merged, ya, yb, o1 = _mix_out(ya_pre, yb_pre, z, xt, gate1, wba_full, wbb_full, wo_full)
    h2, up = _modnorm_matmul(x2, norm_ffn_g, scale2, shift2, w_up4, "norm_up_proj")
    f, ffn_ga, ffn_vd = _ffn_gate(up, fcw_full, ffn_conv_b)
    loss_part, dx3, dfo, dgf, dgate2 = _ffn_down_loss(f, wd_full, x2, gate2, vec(norm_final_g), tgt)

    ffn_map = functools.partial(_ffn_col_block, ncb=2)
    dup, dfcw_a, dfcw_v, dfcb_a, dfcb_v = _ffn_bwd(dfo, wd_full, up, ffn_ga, ffn_vd, fcw_full, cw_blk=ns)
    dwd = _mm_tn_cols(f, dfo, "dw_down", 1, D, mb=D)
    dw_up4 = _mm_tn_cols(h2, dup, "dw_up", N_CHIPS, ns, colmap=ffn_map)
    dx2, dshift2, dscale2, dg_ffn, do1, dgate1 = _mm_nt_normbwd(
        dup, w_up4, x2, dx3, norm_ffn_g, scale2, "dh2_norm_bwd", gate=gate1, o=o1, dz_blocks=(0, 2, 1, 3))
    dz, dya, dyb, dya_pre, dyb_pre = _mix_bwd(do1, ya, yb, z, wo_full, wba_full, wbb_full)
    dwo = _mm_tn_cols(merged, do1, "dw_out", 1, D)
    dwba = _mm_tn_cols(ya_pre, dya, "dw_branch_a", 1, D)
    dwbb = _mm_tn_cols(yb_pre, dyb, "dw_branch_b", 1, D)

    core = mc.astype(jnp.int32).reshape(1)
    chip_id = chip.astype(jnp.int32).reshape(1)

    def reduce_start(group, name):
        from_core = _send_other_half([g for _, g in group], "swap_halves_" + name)
        sums = [_add_halves_bf16(core, g, o, "sum_cores_" + n) for (n, g), o in zip(group, from_core)]
        lands = [lax.empty((3,) + s.shape[1:], s.dtype) for s in sums]
        return _remote_start(sums, lands, _scatter_plan(len(group)), 3 * len(group), "scatter_start_" + name)

    def reduce_finish(group, handle, after, name):
        sums, landed = _remote_wait(handle, _scatter_plan(len(group)), after, "scatter_wait_" + name)
        return [_sum_own_and_landed(chip_id, s, l, "sum_chips_" + n) for (n, _), s, l in zip(group, sums, landed)]

    group1 = [("w_up", dw_up4), ("w_down", dwd.reshape(N_CHIPS, dff // N_CHIPS, D)),
              ("w_branch_a", dwba.reshape(N_CHIPS, D // N_CHIPS, D)),
              ("w_branch_b", dwbb.reshape(N_CHIPS, D // N_CHIPS, D)), ("w_out", dwo.reshape(N_CHIPS, D // N_CHIPS, D))]
    handle1, token1 = reduce_start(group1, "late")
    dz, dws, dbst, dlg, dlb = _sgu_bwd(dz, dyb_pre, z, sgu_ln_g + token1[0:1, 0:1], sgu_ln_b, sgu_w_s[0], bst)
    dz, drcw, drcb, dwa, dba, dwx, dbx, dlam = _rglru_bwd(
        dz, dya_pre, z, h_lru, rcw_full, rnn_conv_b, lru_w_a[0], lru_b_a, lru_w_x[0], lru_b_x, lru_lambda)
    totals1 = reduce_finish(group1, handle1, drcb, "late")
    group2 = [("w_in", _mm_tn_cols(h1, dz, "dw_in", N_CHIPS, ns))]
    handle2, token2 = reduce_start(group2, "in")
    grad_x, dshift1, dscale1, dg_mix = _mm_nt_normbwd(
        dz, w_in4, xt, dx2, norm_mix_g + token2[0:1, 0:1], scale1, "dh1_norm_bwd")
    totals2 = reduce_finish(group2, handle2, dg_mix, "in")
    dmod = jnp.concatenate([dshift1, dscale1, dgate1, dshift2, dscale2, dgate2], axis=1)

    big = group1 + group2
    fulls = _share_halves(totals1 + totals2)
    out = {}
    for (n, _), full in zip(big, fulls):
        shape = args[n].shape
        res = _adamw(args[n][0], args["m_" + n][0], args["v_" + n][0], [full.reshape(shape[1:])], "adamw_" + n)
        for kind, r in zip(("grad_", "delta_", "new_m_", "new_v_"), res):
            out[kind + n] = r.reshape(shape)

    small = [("b_ada", dmod), ("norm_mix_g", dg_mix), ("rnn_conv_b", drcb), ("lru_w_a", dwa), ("lru_b_a", dba),
             ("lru_w_x", dwx), ("lru_b_x", dbx), ("lru_lambda", dlam), ("sgu_ln_g", dlg), ("sgu_ln_b", dlb),
             ("sgu_w_s", dws), ("sgu_b_s", jnp.transpose(dbst)), ("norm_ffn_g", dg_ffn),
             ("ffn_conv_b", jnp.concatenate([dfcb_a, dfcb_v], axis=1)), ("norm_final_g", dgf)]
    dfcw = jnp.concatenate([dfcw_a, dfcw_v], axis=1)
    r_small = sum(_rows_of(args[n].shape) for n, _ in small)
    r_pad = r_small + (-r_small) % 256
    fill = jnp.zeros((r_pad - r_small, LANES), F32)
    g_pack = jnp.concatenate([_pack_rows([g for _, g in small]), fill, _pack_rows([drcw, dfcw])], axis=0)
    g_all = _gather8(g_pack, "gather_small_grads")
    g_sum = _sum_parts(g_all, "sum_small_grads")

    def pack_small(prefix):
        return jnp.concatenate([_pack_rows([args[prefix + n] for n, _ in small]), fill], axis=0)

    res = _adamw(pack_small(""), pack_small("m_"), pack_small("v_"), [g_sum[:r_pad]], "adamw_small")
    off = 0
    for n, _ in small:
        shape = args[n].shape
        rows = _rows_of(shape)
        for kind, r in zip(("grad_", "delta_", "new_m_", "new_v_"), res):
            out[kind + n] = r[off:off + rows].reshape(shape)
        off += rows

    rcw_cols = rnn_conv_w.shape[2]
    g_rcw = lax.dynamic_slice(g_sum[r_pad:r_pad + 32].reshape(4, D), (0, chip * rcw_cols), (4, rcw_cols))
    g_fcw = lax.dynamic_slice(g_sum[r_pad + 32:r_pad + 32 + 144].reshape(3, 2 * dff), (0, chip * ns), (3, ns))
    conv = [("rnn_conv_w", g_rcw), ("ffn_conv_w", g_fcw)]
    res = _adamw(_pack_rows([args[n] for n, _ in conv]), _pack_rows([args["m_" + n] for n, _ in conv]),
                 _pack_rows([args["v_" + n] for n, _ in conv]), [_pack_rows([g for _, g in conv])], "adamw_conv")
    off = 0
    for n, _ in conv:
        shape = args[n].shape
        cnt = shape[1] * shape[2] // LANES
        for kind, r in zip(("grad_", "delta_", "new_m_", "new_v_"), res):
            out[kind + n] = r[off:off + cnt].reshape(shape)
        off += _rows_of(shape)

    dmod_all = g_all[:, 0:6 * D // LANES, :].reshape(N_DEV, 6 * D)
    dmod_sh = lax.dynamic_slice(dmod_all, (0, chip * ns), (N_DEV, ns))
    res = _ada_adamw(jnp.transpose(c_all), dmod_sh, w_ada[0], m_w_ada[0], v_w_ada[0])
    for kind, r in zip(("grad_", "delta_", "new_m_", "new_v_"), res):
        out[kind + "w_ada"] = r.reshape(w_ada.shape)

    loss = lax.psum(loss_part[0, 0], ("x", "y", "c"))
    names = ["w_ada", "b_ada", "norm_mix_g", "w_in", "rnn_conv_w", "rnn_conv_b", "lru_w_a", "lru_b_a", "lru_w_x",
             "lru_b_x", "lru_lambda", "sgu_ln_g", "sgu_ln_b", "sgu_w_s", "sgu_b_s", "w_branch_a", "w_branch_b",
             "w_out", "norm_ffn_g", "w_up", "ffn_conv_w", "ffn_conv_b", "w_down", "norm_final_g"]
    result = [loss, grad_x.reshape(x.shape)]
    for kind in ("grad_", "delta_", "new_m_", "new_v_"):
        result += [out[kind + n] for n in names]
    return tuple(result)
```

```python
import functools

import jax
import jax.numpy as jnp
from jax import lax
from jax.experimental import pallas as pl
from jax.experimental.pallas import tpu as pltpu

F32 = jnp.float32
BF = jnp.bfloat16

D = 1024
HEADS = 8
HD = D // HEADS
SGU_BLOCK = 128
N_CHIPS = 4
N_DEV = 8
EPS = 1e-6
LRU_C = 8.0
LANES = 128
SUBLANES = 8

ADAM_LR = 0.001
ADAM_B1 = 0.9
ADAM_B2 = 0.999
ADAM_EPS = 1e-08
ADAM_WD = 0.01
ADAM_STEP = 10

GELU_K0 = 0.7978845608028654
GELU_K1 = 0.044715

HBM_SPEC = pl.BlockSpec(memory_space=pltpu.HBM)
MESH_ID = pl.DeviceIdType.MESH


def _pcall(body, *, name, out_shape, grid=(), in_specs=None, out_specs=None, scratch=(), vmem_mb=32, aliases=None,
           grid_spec=None):
    kw = {}
    if aliases:
        kw["input_output_aliases"] = aliases
    if grid_spec is not None:
        kw["grid_spec"] = grid_spec
        ndim = len(grid_spec.grid)
    else:
        kw.update(grid=grid, in_specs=in_specs, out_specs=out_specs, scratch_shapes=list(scratch))
        ndim = len(grid)
    if ndim:
        params = pltpu.CompilerParams(dimension_semantics=("arbitrary",) * ndim, vmem_limit_bytes=vmem_mb * 2 ** 20)
    else:
        params = pltpu.CompilerParams(vmem_limit_bytes=vmem_mb * 2 ** 20)
    return pl.pallas_call(body, name=name, out_shape=out_shape, compiler_params=params, **kw)


def _gelu(x):
    return 0.5 * x * (1.0 + jnp.tanh(GELU_K0 * (x + GELU_K1 * x * x * x)))


def _gelu_and_grad(x):
    x2 = x * x
    t = jnp.tanh(GELU_K0 * x * (1.0 + GELU_K1 * x2))
    g = 0.5 * x * (1.0 + t)
    dg = 0.5 * (1.0 + t) + 0.5 * x * (1.0 - t * t) * (GELU_K0 * (1.0 + 3.0 * GELU_K1 * x2))
    return g, dg


def _sigmoid(x):
    return 1.0 / (1.0 + jnp.exp(-x))


def _sigmoid_t(x):
    return 0.5 * jnp.tanh(0.5 * x) + 0.5


def _log_sigmoid(x):
    e = jnp.exp(-jnp.abs(x))
    u = 1.0 + e
    d = u - 1.0
    l1p = jnp.where(d == 0.0, e, jnp.log(u) * (e / jnp.where(d == 0.0, 1.0, d)))
    return jnp.minimum(x, 0.0) - l1p


def _dot(a, b):
    return jnp.dot(a, b, preferred_element_type=F32)


def _dot_nt(a, b):
    return lax.dot_general(a, b, (((1,), (1,)), ((), ())), preferred_element_type=F32)


def _dot_tn(a, b):
    return lax.dot_general(a, b, (((0,), (0,)), ((), ())), preferred_element_type=F32)


def _shift_down(x, halo, s):
    r = pltpu.roll(x, s, 0)
    rows = lax.broadcasted_iota(jnp.int32, (SUBLANES, x.shape[1]), 0)
    head = jnp.where(rows < s, pltpu.roll(halo, s, 0), r[0:SUBLANES])
    return jnp.concatenate([head, r[SUBLANES:]], axis=0)


def _shift_up(x, halo, s):
    n = x.shape[0]
    r = pltpu.roll(x, n - s, 0)
    rows = lax.broadcasted_iota(jnp.int32, (SUBLANES, x.shape[1]), 0)
    tail = jnp.where(rows >= SUBLANES - s, pltpu.roll(halo, SUBLANES - s, 0), r[n - SUBLANES:n])
    return jnp.concatenate([r[:n - SUBLANES], tail], axis=0)


def _scan_rows(a, u, reverse):
    n, width = a.shape
    rows = lax.broadcasted_iota(jnp.int32, (n, width), 0)
    d = 1
    while d < n:
        if d < SUBLANES:
            keep = rows < n - d if reverse else rows >= d
            shift = n - d if reverse else d
            a_s = jnp.where(keep, pltpu.roll(a, shift, 0), 1.0)
            u_s = jnp.where(keep, pltpu.roll(u, shift, 0), 0.0)
        elif reverse:
            a_s = jnp.concatenate([a[d:], jnp.ones((d, width), a.dtype)], axis=0)
            u_s = jnp.concatenate([u[d:], jnp.zeros((d, width), u.dtype)], axis=0)
        else:
            a_s = jnp.concatenate([jnp.ones((d, width), a.dtype), a[:n - d]], axis=0)
            u_s = jnp.concatenate([jnp.zeros((d, width), u.dtype), u[:n - d]], axis=0)
        u = a * u_s + u
        a = a * a_s
        d *= 2
    return a, u


def _colsum(x):
    return jnp.sum(x, axis=0, keepdims=True)


def _rms_stats(x):
    r = lax.rsqrt(jnp.mean(x * x, axis=-1, keepdims=True) + EPS)
    return r, x * r


def _lru_gates(xc, wa_ref, ba, wx_ref, bx, lam):
    pr, pi = [], []
    for hh in range(HEADS):
        xh = xc[:, hh * HD:(hh + 1) * HD].astype(BF)
        pr.append(_dot(xh, wa_ref[hh].astype(BF)))
        pi.append(_dot(xh, wx_ref[hh].astype(BF)))
    r = _sigmoid_t(jnp.concatenate(pr, axis=1) + ba)
    ig = _sigmoid_t(jnp.concatenate(pi, axis=1) + bx)
    ls = _log_sigmoid(lam)
    log_a = LRU_C * r * ls
    a = jnp.exp(log_a)
    x2 = 2.0 * log_a
    u = a * a
    lu = jnp.log(jnp.maximum(u, 1e-37))
    em1 = jnp.where(lu == 0.0, x2, jnp.where(u < 1e-30, -1.0, (u - 1.0) * x2 / jnp.where(lu == 0.0, 1.0, lu)))
    mult = jnp.sqrt(-em1)
    return r, ig, ls, a, mult


def _sgu_mix(vln, ws_ref, bst_ref, tb):
    ri = lax.broadcasted_iota(jnp.int32, (SGU_BLOCK, SGU_BLOCK), 0)
    ci = lax.broadcasted_iota(jnp.int32, (SGU_BLOCK, SGU_BLOCK), 1)
    wm = [jnp.where(ri >= ci, ws_ref[g], 0.0).astype(BF) for g in range(HEADS)]
    blocks = []
    for blk in range(tb // SGU_BLOCK):
        cols = []
        for g in range(HEADS):
            vb = vln[blk * SGU_BLOCK:(blk + 1) * SGU_BLOCK, g * HD:(g + 1) * HD].astype(BF)
            cols.append(_dot(wm[g], vb) + bst_ref[:, g:g + 1])
        blocks.append(jnp.concatenate(cols, axis=1))
    mixed = blocks[0] if len(blocks) == 1 else jnp.concatenate(blocks, axis=0)
    return wm, mixed


def _layernorm_stats(v):
    mu = jnp.mean(v, axis=-1, keepdims=True)
    vc = v - mu
    rstd = lax.rsqrt(jnp.mean(vc * vc, axis=-1, keepdims=True) + EPS)
    return rstd, vc * rstd


def _my_xyc():
    return lax.axis_index("x"), lax.axis_index("y"), lax.axis_index("c")


def _gather_weights(srcs, halve):
    n = len(srcs)
    out_shape = [jax.ShapeDtypeStruct((N_CHIPS,) + s.shape, s.dtype) for s in srcs]

    def body(*refs):
        src, out = refs[:n], refs[n:2 * n]
        send_sems, recv_sems, fwd_send, fwd_recv, loc_sems = refs[2 * n:]
        x, y, c = _my_xyc()
        me = 2 * x + y
        chips = [(1 - x, y), (x, 1 - y), (1 - x, 1 - y)]

        def half(ref, a, which):
            if not halve[a]:
                return ref
            h = srcs[a].shape[0] // 2
            return ref.at[pl.ds(which * h, h)]

        def ici(a, k, frm):
            px, py = chips[k]
            return pltpu.make_async_remote_copy(
                src_ref=half(src[a], a, c), dst_ref=half(out[a].at[frm], a, c),
                send_sem=send_sems.at[a, k], recv_sem=recv_sems.at[a, k],
                device_id=(px, py, c), device_id_type=MESH_ID)

        def d2d(a, k, which):
            px, py = chips[k]
            rows = half(out[a].at[2 * px + py], a, which)
            return pltpu.make_async_remote_copy(
                src_ref=rows, dst_ref=rows, send_sem=fwd_send.at[a, k], recv_sem=fwd_recv.at[a, k],
                device_id=(x, y, 1 - c), device_id_type=MESH_ID)

        local, sends = [], []
        for a in range(n):
            lc = pltpu.make_async_copy(src[a], out[a].at[me], loc_sems.at[a])
            lc.start()
            local.append(lc)
            for k in range(3):
                cp = ici(a, k, me)
                cp.start()
                sends.append(cp)
        for a in range(n):
            for k in range(3):
                px, py = chips[k]
                ici(a, k, 2 * px + py).wait_recv()
                if halve[a]:
                    fw = d2d(a, k, c)
                    fw.start()
                    sends.append(fw)
        for a in range(n):
            if halve[a]:
                for k in range(3):
                    d2d(a, k, 1 - c).wait_recv()
        for cp in sends:
            cp.wait_send()
        for lc in local:
            lc.wait()

    sem = pltpu.SemaphoreType.DMA((n, 3))
    return _pcall(body, name="gather_weights", out_shape=out_shape, in_specs=[HBM_SPEC] * n,
                  out_specs=[HBM_SPEC] * n, scratch=[sem, sem, sem, sem, pltpu.SemaphoreType.DMA((n,))])(*srcs)


SEM_SPEC = pl.BlockSpec(memory_space=pltpu.SEMAPHORE)


def _remote_start(srcs, lands, plan, ncopies, name):
    n, m = len(srcs), len(lands)

    def body(*refs):
        src, land = refs[:n], refs[n:n + m]
        send_sems, recv_sems = refs[n + m], refs[n + m + 1]
        token = refs[-1]
        x, y, c = _my_xyc()
        for i, (s, d, dev) in enumerate(plan(src, land, x, y, c)):
            pltpu.make_async_remote_copy(src_ref=s, dst_ref=d, send_sem=send_sems.at[i], recv_sem=recv_sems.at[i],
                                         device_id=dev, device_id_type=MESH_ID).start()
        token[...] = jnp.zeros_like(token)

    bufs = list(srcs) + list(lands)
    out = pl.pallas_call(
        body, name=name,
        out_shape=(pltpu.SemaphoreType.DMA((ncopies,)), pltpu.SemaphoreType.DMA((ncopies,)),
                   *[pltpu.HBM(b.shape, b.dtype) for b in bufs], jax.ShapeDtypeStruct((SUBLANES, LANES), F32)),
        in_specs=[HBM_SPEC] * (n + m),
        out_specs=(SEM_SPEC, SEM_SPEC, *[HBM_SPEC] * (n + m), pl.BlockSpec(memory_space=pltpu.VMEM)),
        input_output_aliases={i: 2 + i for i in range(n + m)},
        compiler_params=pltpu.CompilerParams(has_side_effects=pltpu.SideEffectType.DATAFLOW_SIDE_EFFECTING),
    )(*[pltpu.with_memory_space_constraint(b, pltpu.HBM) for b in bufs])
    return (out[0], out[1], out[2:2 + n], out[2 + n:2 + n + m]), out[-1]


def _remote_wait(handle, plan, after, name):
    send_sems, recv_sems, srcs, lands = handle
    n, m = len(srcs), len(lands)

    def body(*refs):
        src, land = refs[:n], refs[n:n + m]
        ssem, rsem = refs[n + m], refs[n + m + 1]
        x, y, c = _my_xyc()
        for i, (s, d, dev) in enumerate(plan(src, land, x, y, c)):
            cp = pltpu.make_async_remote_copy(src_ref=s, dst_ref=d, send_sem=ssem.at[i], recv_sem=rsem.at[i],
                                              device_id=dev, device_id_type=MESH_ID)
            cp.wait_send()
            cp.wait_recv()

    bufs = list(srcs) + list(lands)
    out = pl.pallas_call(
        body, name=name, out_shape=tuple(pltpu.HBM(b.shape, b.dtype) for b in bufs),
        in_specs=[HBM_SPEC] * (n + m) + [SEM_SPEC, SEM_SPEC, pl.BlockSpec(memory_space=pl.ANY)],
        out_specs=tuple([HBM_SPEC] * (n + m)), input_output_aliases={i: i for i in range(n + m)},
        compiler_params=pltpu.CompilerParams(has_side_effects=pltpu.SideEffectType.DATAFLOW_SIDE_EFFECTING),
    )(*bufs, send_sems, recv_sems, after)
    return out[:n], out[n:]


def _chips_of(x, y):
    return [(1 - x, y), (x, 1 - y), (1 - x, 1 - y)]


def _gather_half_plan(shapes):
    def plan(src, land, x, y, c):
        me = 2 * x + y
        out = []
        for a, shape in enumerate(shapes):
            h = shape[0] // 2
            rows = pl.ds(c * h, h)
            for px, py in _chips_of(x, y):
                out.append((src[a].at[rows], land[a].at[me, rows], (px, py, c)))
        return out

    return plan


def _forward_halves(srcs, lands):
    n = len(srcs)

    def body(*refs):
        src, land = refs[:n], refs[2 * n:3 * n]
        send_sems, recv_sems, loc_sems = refs[3 * n:]
        x, y, c = _my_xyc()
        me = 2 * x + y

        def fwd(a, k, which):
            px, py = _chips_of(x, y)[k]
            h = srcs[a].shape[0] // 2
            rows = land[a].at[2 * px + py, pl.ds(which * h, h)]
            return pltpu.make_async_remote_copy(
                src_ref=rows, dst_ref=rows, send_sem=send_sems.at[a, k], recv_sem=recv_sems.at[a, k],
                device_id=(x, y, 1 - c), device_id_type=MESH_ID)

        local, sends = [], []
        for a in range(n):
            lc = pltpu.make_async_copy(src[a], land[a].at[me], loc_sems.at[a])
            lc.start()
            local.append(lc)
            for k in range(3):
                cp = fwd(a, k, c)
                cp.start()
                sends.append(cp)
        for a in range(n):
            for k in range(3):
                fwd(a, k, 1 - c).wait_recv()
        for cp in sends:
            cp.wait_send()
        for lc in local:
            lc.wait()

    sem = pltpu.SemaphoreType.DMA((n, 3))
    return _pcall(body, name="forward_halves", out_shape=[jax.ShapeDtypeStruct(l.shape, l.dtype) for l in lands],
                  in_specs=[HBM_SPEC] * (2 * n), out_specs=[HBM_SPEC] * n, aliases={n + a: a for a in range(n)},
                  scratch=[sem, sem, pltpu.SemaphoreType.DMA((n,))])(*srcs, *lands)


def _send_other_half(grads, name):
    n = len(grads)
    out_shape = [jax.ShapeDtypeStruct((N_CHIPS, g.shape[1] // 2, g.shape[2]), g.dtype) for g in grads]

    def body(*refs):
        src, out = refs[:n], refs[n:2 * n]
        send_sems, recv_sems = refs[2 * n:]
        x, y, c = _my_xyc()
        cps = []
        for a in range(n):
            h = grads[a].shape[1] // 2
            cp = pltpu.make_async_remote_copy(
                src_ref=src[a].at[:, pl.ds((1 - c) * h, h), :], dst_ref=out[a],
                send_sem=send_sems.at[a], recv_sem=recv_sems.at[a],
                device_id=(x, y, 1 - c), device_id_type=MESH_ID)
            cp.start()
            cps.append(cp)
        for cp in cps:
            cp.wait()

    return _pcall(body, name=name, out_shape=out_shape, in_specs=[HBM_SPEC] * n,
                  out_specs=[HBM_SPEC] * n,
                  scratch=[pltpu.SemaphoreType.DMA((n,)), pltpu.SemaphoreType.DMA((n,))])(*grads)


def _share_halves(totals):
    n = len(totals)
    out_shape = [jax.ShapeDtypeStruct((2,) + t.shape, t.dtype) for t in totals]

    def body(*refs):
        src, out = refs[:n], refs[n:2 * n]
        send_sems, recv_sems, loc_sems = refs[2 * n:]
        x, y, c = _my_xyc()
        cps, local = [], []
        for a in range(n):
            lc = pltpu.make_async_copy(src[a], out[a].at[c], loc_sems.at[a])
            lc.start()
            local.append(lc)
            cp = pltpu.make_async_remote_copy(
                src_ref=src[a], dst_ref=out[a].at[c], send_sem=send_sems.at[a], recv_sem=recv_sems.at[a],
                device_id=(x, y, 1 - c), device_id_type=MESH_ID)
            cp.start()
            cps.append(cp)
        for cp in cps:
            cp.wait()
        for lc in local:
            lc.wait()

    sem = pltpu.SemaphoreType.DMA((n,))
    return _pcall(body, name="share_halves", out_shape=out_shape, in_specs=[HBM_SPEC] * n,
                  out_specs=[HBM_SPEC] * n, scratch=[sem, sem, sem])(*totals)


def _gather8(src, name):
    def body(src_ref, out_ref, send_sems, recv_sems, loc_sem):
        x, y, c = _my_xyc()
        me = 4 * x + 2 * y + c
        lc = pltpu.make_async_copy(src_ref, out_ref.at[me], loc_sem)
        lc.start()
        cps = []
        for k in range(1, N_DEV):
            px = 1 - x if (k >> 2) & 1 else x
            py = 1 - y if (k >> 1) & 1 else y
            pc = 1 - c if k & 1 else c
            cp = pltpu.make_async_remote_copy(
                src_ref=src_ref, dst_ref=out_ref.at[me], send_sem=send_sems.at[k - 1], recv_sem=recv_sems.at[k - 1],
                device_id=(px, py, pc), device_id_type=MESH_ID)
            cp.start()
            cps.append(cp)
        for cp in cps:
            cp.wait()
        lc.wait()

    return _pcall(body, name=name, out_shape=jax.ShapeDtypeStruct((N_DEV,) + src.shape, src.dtype),
                  in_specs=[HBM_SPEC], out_specs=HBM_SPEC,
                  scratch=[pltpu.SemaphoreType.DMA((N_DEV - 1,)), pltpu.SemaphoreType.DMA((N_DEV - 1,)),
                           pltpu.SemaphoreType.DMA])(src)


def _cast_shards(arrs, name, after=None):
    n = len(arrs)
    extra = [] if after is None else [after]

    def body(*refs):
        ins, outs = refs[:n], refs[n + len(extra):]
        for a in range(n):
            outs[a][...] = ins[a][...].astype(BF)

    specs = [pl.BlockSpec((s.shape[0] // 4, s.shape[1]), lambda i: (i, 0)) for s in arrs]
    return _pcall(body, name=name, grid=(4,), in_specs=specs + [pl.BlockSpec(memory_space=pl.ANY)] * len(extra),
                  out_specs=specs, out_shape=[jax.ShapeDtypeStruct(s.shape, BF) for s in arrs])(*arrs, *extra)


def _row_tile(rows, cols):
    t = rows
    while t * cols * 4 > (3 << 19) and t % 16 == 0:
        t //= 2
    return t


def _sum_parts(parts, name):
    p, rows, cols = parts.shape
    tr = _row_tile(rows, cols * p // 2)

    def body(p_ref, o_ref):
        acc = p_ref[0].astype(F32)
        for k in range(1, p):
            acc = acc + p_ref[k].astype(F32)
        o_ref[...] = acc

    return _pcall(body, name=name, grid=(rows // tr,),
                  in_specs=[pl.BlockSpec((p, tr, cols), lambda i: (0, i, 0))],
                  out_specs=pl.BlockSpec((tr, cols), lambda i: (i, 0)),
                  out_shape=jax.ShapeDtypeStruct((rows, cols), F32), vmem_mb=48)(parts)


def _add_halves_bf16(core, grad, other, name):
    nchip, rows, cols = grad.shape
    h = rows // 2
    tr = _row_tile(h, cols)
    nh = h // tr

    def body(c_ref, g_ref, o_ref, s_ref):
        del c_ref
        s_ref[...] = (g_ref[...] + o_ref[...]).astype(BF)

    grid_spec = pltpu.PrefetchScalarGridSpec(
        num_scalar_prefetch=1, grid=(nchip, nh),
        in_specs=[pl.BlockSpec((1, tr, cols), lambda p, i, c_ref: (p, c_ref[0] * nh + i, 0)),
                  pl.BlockSpec((1, tr, cols), lambda p, i, c_ref: (p, i, 0))],
        out_specs=pl.BlockSpec((1, tr, cols), lambda p, i, c_ref: (p, i, 0)))
    return _pcall(body, name=name, grid_spec=grid_spec, out_shape=jax.ShapeDtypeStruct((nchip, h, cols), BF),
                  vmem_mb=48)(core, grad, other)


def _sum_own_and_landed(chip, sums, landed, name):
    _, rows, cols = sums.shape
    tr = _row_tile(rows, 2 * cols)

    def body(chip_ref, own_ref, land_ref, o_ref):
        del chip_ref
        acc = own_ref[0].astype(F32)
        for k in range(3):
            acc = acc + land_ref[k].astype(F32)
        o_ref[...] = acc

    grid_spec = pltpu.PrefetchScalarGridSpec(
        num_scalar_prefetch=1, grid=(rows // tr,),
        in_specs=[pl.BlockSpec((1, tr, cols), lambda i, chip_ref: (chip_ref[0], i, 0)),
                  pl.BlockSpec((3, tr, cols), lambda i, chip_ref: (0, i, 0))],
        out_specs=pl.BlockSpec((tr, cols), lambda i, chip_ref: (i, 0)))
    return _pcall(body, name=name, grid_spec=grid_spec, out_shape=jax.ShapeDtypeStruct((rows, cols), F32),
                  vmem_mb=48)(chip, sums, landed)


def _swap_core(arr, name):
    def body(src, out, send_sem, recv_sem):
        x, y, c = _my_xyc()
        cp = pltpu.make_async_remote_copy(src_ref=src, dst_ref=out, send_sem=send_sem, recv_sem=recv_sem,
                                          device_id=(x, y, 1 - c), device_id_type=MESH_ID)
        cp.start()
        cp.wait()

    return _pcall(body, name=name, out_shape=jax.ShapeDtypeStruct(arr.shape, arr.dtype), in_specs=[HBM_SPEC],
                  out_specs=HBM_SPEC, scratch=[pltpu.SemaphoreType.DMA, pltpu.SemaphoreType.DMA])(arr)


def _add_pair(a, b, name):
    rows, cols = a.shape
    tr = _row_tile(rows, 2 * cols)

    def body(a_ref, b_ref, o_ref):
        o_ref[...] = a_ref[...] + b_ref[...]

    spec = pl.BlockSpec((tr, cols), lambda i: (i, 0))
    return _pcall(body, name=name, grid=(rows // tr,), in_specs=[spec, spec], out_specs=spec,
                  out_shape=jax.ShapeDtypeStruct((rows, cols), F32))(a, b)


def _sum_chips_in_order(chip, own, landed, name):
    rows, cols = own.shape
    tr = _row_tile(rows, 4 * cols)

    def body(chip_ref, own_ref, land_ref, o_ref):
        me = chip_ref[0]
        acc = None
        for p in range(N_CHIPS):
            q = p ^ me
            k = jnp.where(q == 2, 0, jnp.where(q == 1, 1, 2))
            term = jnp.where(q == 0, own_ref[...], land_ref[k])
            acc = term if acc is None else acc + term
        o_ref[...] = acc

    grid_spec = pltpu.PrefetchScalarGridSpec(
        num_scalar_prefetch=1, grid=(rows // tr,),
        in_specs=[pl.BlockSpec((tr, cols), lambda i, chip_ref: (i, 0)),
                  pl.BlockSpec((3, tr, cols), lambda i, chip_ref: (0, i, 0))],
        out_specs=pl.BlockSpec((tr, cols), lambda i, chip_ref: (i, 0)))
    return _pcall(body, name=name, grid_spec=grid_spec, out_shape=jax.ShapeDtypeStruct((rows, cols), F32))(
        chip, own, landed)


def _bcast_plan(src, land, x, y, c):
    return [(src[0], land[0].at[k], (px, py, c)) for k, (px, py) in enumerate(_chips_of(x, y))]


def _scatter_plan(count):
    def plan(src, land, x, y, c):
        out = []
        for a in range(count):
            for k, (px, py) in enumerate(_chips_of(x, y)):
                out.append((src[a].at[2 * px + py], land[a].at[k], (px, py, c)))
        return out

    return plan


def _adamw_math(w, g, m, v):
    m2 = ADAM_B1 * m + (1.0 - ADAM_B1) * g
    v2 = ADAM_B2 * v + (1.0 - ADAM_B2) * (g * g)
    m_hat = m2 / (1.0 - ADAM_B1 ** ADAM_STEP)
    v_hat = v2 / (1.0 - ADAM_B2 ** ADAM_STEP)
    delta = -ADAM_LR * (m_hat / (jnp.sqrt(v_hat) + ADAM_EPS) + ADAM_WD * w)
    return delta, m2, v2


def _adamw(w, m, v, grads, name):
    rows, cols = w.shape
    tr = _row_tile(rows, cols)
    ng = len(grads)

    def body(*refs):
        w_ref, m_ref, v_ref = refs[:3]
        g = refs[3][...]
        for k in range(1, ng):
            g = g + refs[3 + k][...]
        g_ref, d_ref, m2_ref, v2_ref = refs[3 + ng:]
        delta, m2, v2 = _adamw_math(w_ref[...], g, m_ref[...], v_ref[...])
        g_ref[...] = g
        d_ref[...] = delta
        m2_ref[...] = m2
        v2_ref[...] = v2

    spec = pl.BlockSpec((tr, cols), lambda i: (i, 0))
    return _pcall(body, name=name, grid=(rows // tr,), in_specs=[spec] * (3 + ng), out_specs=[spec] * 4,
                  out_shape=[jax.ShapeDtypeStruct((rows, cols), F32)] * 4, vmem_mb=48)(w, m, v, *grads)


def _ada_adamw(ct, dmod, w, m, v):
    rows, cols = w.shape
    tr = _row_tile(rows, cols)

    def body(ct_ref, dm_ref, w_ref, m_ref, v_ref, g_ref, d_ref, m2_ref, v2_ref):
        cv = ct_ref[...]
        ca = cv * _sigmoid(cv)
        g = ca[:, 0:1] * dm_ref[0:1, :]
        for b in range(1, N_DEV):
            g = g + ca[:, b:b + 1] * dm_ref[b:b + 1, :]
        delta, m2, v2 = _adamw_math(w_ref[...], g, m_ref[...], v_ref[...])
        g_ref[...] = g
        d_ref[...] = delta
        m2_ref[...] = m2
        v2_ref[...] = v2

    spec = pl.BlockSpec((tr, cols), lambda i: (i, 0))
    return _pcall(body, name="ada_adamw", grid=(rows // tr,),
                  in_specs=[pl.BlockSpec((tr, N_DEV), lambda i: (i, 0)), pl.BlockSpec((N_DEV, cols), lambda i: (0, 0)),
                            spec, spec, spec],
                  out_specs=[spec] * 4, out_shape=[jax.ShapeDtypeStruct((rows, cols), F32)] * 4,
                  vmem_mb=48)(ct, dmod, w, m, v)


def _mod_fwd(c_all, w, b):
    cols = w.shape[1]
    tn = cols // 3

    def body(c_ref, w_ref, b_ref, o_ref):
        cv = c_ref[...]
        ca = (cv * _sigmoid(cv)).astype(BF)
        o_ref[...] = _dot(ca, w_ref[...].astype(BF)) + b_ref[...]

    return _pcall(body, name="mod_fwd", grid=(3,),
                  in_specs=[pl.BlockSpec((N_DEV, D), lambda j: (0, 0)), pl.BlockSpec((D, tn), lambda j: (0, j)),
                            pl.BlockSpec((1, tn), lambda j: (0, j))],
                  out_specs=pl.BlockSpec((N_DEV, tn), lambda j: (0, j)),
                  out_shape=jax.ShapeDtypeStruct((N_DEV, cols), F32))(c_all, w, b)


def _resident(shape):
    zeros = (0,) * len(shape)
    return pl.BlockSpec(shape, lambda *_: zeros, pipeline_mode=pl.Buffered(1))


def _modnorm_matmul(x, g, scale, shift, w4, name, tm=256):
    T = x.shape[0]
    tm = min(tm, T)
    ns = w4.shape[2]

    def body(x_ref, g_ref, sc_ref, sh_ref, w_ref, h_ref, z_ref):
        _, xh = _rms_stats(x_ref[...])
        h = ((xh * g_ref[...]) * (1.0 + sc_ref[...]) + sh_ref[...]).astype(BF)
        h_ref[...] = h
        for j in range(N_CHIPS):
            z_ref[:, j * ns:(j + 1) * ns] = _dot(h, w_ref[j])

    vec = pl.BlockSpec((1, D), lambda i: (0, 0))
    return _pcall(body, name=name, grid=(T // tm,),
                  in_specs=[pl.BlockSpec((tm, D), lambda i: (i, 0)), vec, vec, vec, _resident(w4.shape)],
                  out_specs=[pl.BlockSpec((tm, D), lambda i: (i, 0)), pl.BlockSpec((tm, N_CHIPS * ns), lambda i: (i, 0))],
                  out_shape=[jax.ShapeDtypeStruct((T, D), BF), jax.ShapeDtypeStruct((T, N_CHIPS * ns), F32)],
                  vmem_mb=48)(x, g, scale, shift, w4)


def _rglru_fwd(z, cw, cb, wa, ba, wx, bx, lam, tb=256):
    T = z.shape[0]
    tb = min(tb, T)

    def body(xr_ref, gr_ref, cw_ref, cb_ref, wa_ref, ba_ref, wx_ref, bx_ref, lam_ref, h_ref, ya_ref, prev, hc):
        i = pl.program_id(0)

        @pl.when(i == 0)
        def _():
            prev[...] = jnp.zeros_like(prev)
            hc[...] = jnp.zeros_like(hc)

        xr = xr_ref[...]
        pv = prev[...]
        xc = (cb_ref[...] + cw_ref[3:4, :] * xr + cw_ref[2:3, :] * _shift_down(xr, pv, 1)
              + cw_ref[1:2, :] * _shift_down(xr, pv, 2) + cw_ref[0:1, :] * _shift_down(xr, pv, 3))
        prev[...] = xr[tb - SUBLANES:tb]
        _, ig, _, a, mult = _lru_gates(xc, wa_ref, ba_ref[...], wx_ref, bx_ref[...], lam_ref[...])
        a, u = _scan_rows(a, mult * (ig * xc), reverse=False)
        h = u + a * hc[SUBLANES - 1:SUBLANES, :]
        hc[...] = h[tb - SUBLANES:tb]
        h_ref[...] = h
        ya_ref[...] = (h * _gelu(gr_ref[...])).astype(BF)

    vec = pl.BlockSpec((1, D), lambda i: (0, 0))
    wspec = pl.BlockSpec((HEADS, HD, HD), lambda i: (0, 0, 0))
    return _pcall(body, name="rglru_fwd", grid=(T // tb,),
                  in_specs=[pl.BlockSpec((tb, D), lambda i: (i, 0)), pl.BlockSpec((tb, D), lambda i: (i, 1)),
                            pl.BlockSpec((4, D), lambda i: (0, 0)), vec, wspec, vec, wspec, vec, vec],
                  out_specs=[pl.BlockSpec((tb, D), lambda i: (i, 0))] * 2,
                  out_shape=[jax.ShapeDtypeStruct((T, D), F32), jax.ShapeDtypeStruct((T, D), BF)],
                  scratch=[pltpu.VMEM((SUBLANES, D), F32), pltpu.VMEM((SUBLANES, D), F32)],
                  vmem_mb=48)(z, z, cw, cb, wa, ba, wx, bx, lam)


def _sgu_fwd(z, lg, lb, ws, bst, tb=256):
    T = z.shape[0]
    tb = min(tb, T)

    def body(zu_ref, zv_ref, lg_ref, lb_ref, ws_ref, bst_ref, yb_ref):
        _, xh = _layernorm_stats(_gelu(zv_ref[...]))
        vln = xh * lg_ref[...] + lb_ref[...]
        _, mixed = _sgu_mix(vln, ws_ref, bst_ref, tb)
        yb_ref[...] = (_gelu(zu_ref[...]) * mixed).astype(BF)

    vec = pl.BlockSpec((1, D), lambda i: (0, 0))
    return _pcall(body, name="sgu_fwd", grid=(T // tb,),
                  in_specs=[pl.BlockSpec((tb, D), lambda i: (i, 2)), pl.BlockSpec((tb, D), lambda i: (i, 3)), vec, vec,
                            pl.BlockSpec((HEADS, SGU_BLOCK, SGU_BLOCK), lambda i: (0, 0, 0)),
                            pl.BlockSpec((SGU_BLOCK, HEADS), lambda i: (0, 0))],
                  out_specs=pl.BlockSpec((tb, D), lambda i: (i, 0)),
                  out_shape=jax.ShapeDtypeStruct((T, D), BF))(z, z, lg, lb, ws, bst)


def _mix_out(ya_pre, yb_pre, z, x, gate1, wba, wbb, wo, tm=256):
    T = x.shape[0]
    tm = min(tm, T)

    def body(yap_ref, ybp_ref, ga_ref, gb_ref, x_ref, g1_ref, wa_ref, wb_ref, wo_ref,
             x2_ref, mg_ref, ya_ref, yb_ref, o_ref):
        ya = _dot(yap_ref[...], wa_ref[...])
        yb = _dot(ybp_ref[...], wb_ref[...])
        merged = (_sigmoid_t(ga_ref[...]) * ya + _sigmoid_t(gb_ref[...]) * yb).astype(BF)
        o = _dot(merged, wo_ref[...])
        x2_ref[...] = x_ref[...] + g1_ref[...] * o
        mg_ref[...] = merged
        ya_ref[...] = ya.astype(BF)
        yb_ref[...] = yb.astype(BF)
        o_ref[...] = o.astype(BF)

    row = pl.BlockSpec((tm, D), lambda i: (i, 0))
    wspec = pl.BlockSpec((D, D), lambda i: (0, 0))
    return _pcall(body, name="mix_out", grid=(T // tm,),
                  in_specs=[row, row, pl.BlockSpec((tm, D), lambda i: (i, 4)), pl.BlockSpec((tm, D), lambda i: (i, 5)),
                            row, pl.BlockSpec((1, D), lambda i: (0, 0)), wspec, wspec, wspec],
                  out_specs=[row] * 5,
                  out_shape=[jax.ShapeDtypeStruct((T, D), F32)] + [jax.ShapeDtypeStruct((T, D), BF)] * 4,
                  vmem_mb=48)(ya_pre, yb_pre, z, z, x, gate1, wba, wbb, wo)


def _ffn_gate(up, cw, cb, tm=512, cw_blk=768):
    T = up.shape[0]
    tm = min(tm, T)
    dff = up.shape[1] // 2
    ncb = dff // cw_blk

    def body(ua_ref, uv_ref, wa_ref, wv_ref, ba_ref, bv_ref, f_ref, ga_ref, vd_ref, pa, pv):
        i = pl.program_id(1)

        @pl.when(i == 0)
        def _():
            pa[...] = jnp.zeros_like(pa)
            pv[...] = jnp.zeros_like(pv)

        def conv(u_ref, w_ref, b_ref, prev):
            u = u_ref[...]
            p = prev[...]
            hid = (b_ref[...] + w_ref[2:3, :] * u + w_ref[1:2, :] * _shift_down(u, p, 1)
                   + w_ref[0:1, :] * _shift_down(u, p, 2))
            prev[...] = u[tm - SUBLANES:tm]
            return hid

        act = conv(ua_ref, wa_ref, ba_ref, pa)
        val = conv(uv_ref, wv_ref, bv_ref, pv)
        ga, dga = _gelu_and_grad(act)
        f_ref[...] = (ga * val).astype(BF)
        ga_ref[...] = ga.astype(BF)
        vd_ref[...] = (val * dga).astype(BF)

    blk = pl.BlockSpec((tm, cw_blk), lambda cbk, i: (i, cbk))
    return _pcall(body, name="ffn_gate", grid=(ncb, T // tm),
                  in_specs=[pl.BlockSpec((tm, cw_blk), lambda cbk, i: (i, cbk)),
                            pl.BlockSpec((tm, cw_blk), lambda cbk, i: (i, ncb + cbk)),
                            pl.BlockSpec((3, cw_blk), lambda cbk, i: (0, cbk)),
                            pl.BlockSpec((3, cw_blk), lambda cbk, i: (0, ncb + cbk)),
                            pl.BlockSpec((1, cw_blk), lambda cbk, i: (0, cbk)),
                            pl.BlockSpec((1, cw_blk), lambda cbk, i: (0, ncb + cbk))],
                  out_specs=[blk] * 3, out_shape=[jax.ShapeDtypeStruct((T, dff), BF)] * 3,
                  scratch=[pltpu.VMEM((SUBLANES, cw_blk), F32)] * 2)(up, up, cw, cw, cb, cb)


def _ffn_down_loss(f, wd, x2, gate2, gf, target, tm=512):
    T = x2.shape[0]
    tm = min(tm, T)
    dff = f.shape[1]

    def body(f_ref, wd_ref, x2_ref, g2_ref, gf_ref, t_ref, loss_ref, dx3_ref, dfo_ref, dgf_ref, dg2_ref):
        i = pl.program_id(0)

        @pl.when(i == 0)
        def _():
            loss_ref[...] = jnp.zeros_like(loss_ref)
            dgf_ref[...] = jnp.zeros_like(dgf_ref)
            dg2_ref[...] = jnp.zeros_like(dg2_ref)

        fo = _dot(f_ref[...], wd_ref[...])
        x3 = x2_ref[...] + g2_ref[...] * fo
        rstd, xh = _rms_stats(x3)
        err = xh * gf_ref[...] - t_ref[...]
        loss_ref[...] += 0.5 * jnp.sum(jnp.mean(err * err, axis=-1, keepdims=True), axis=0, keepdims=True)
        dy = err * (1.0 / D)
        dgf_ref[...] += _colsum(dy * xh)
        dxh = dy * gf_ref[...]
        dx3 = rstd * (dxh - xh * jnp.mean(dxh * xh, axis=-1, keepdims=True))
        dg2_ref[...] += _colsum(dx3 * fo)
        dx3_ref[...] = dx3
        dfo_ref[...] = (g2_ref[...] * dx3).astype(BF)

    row = pl.BlockSpec((tm, D), lambda i: (i, 0))
    vec = pl.BlockSpec((1, D), lambda i: (0, 0))
    return _pcall(body, name="ffn_down_loss", grid=(T // tm,),
                  in_specs=[pl.BlockSpec((tm, dff), lambda i: (i, 0)), pl.BlockSpec((dff, D), lambda i: (0, 0)),
                            row, vec, vec, row],
                  out_specs=[pl.BlockSpec((1, LANES), lambda i: (0, 0)), row, row, vec, vec],
                  out_shape=[jax.ShapeDtypeStruct((1, LANES), F32), jax.ShapeDtypeStruct((T, D), F32),
                             jax.ShapeDtypeStruct((T, D), BF), jax.ShapeDtypeStruct((1, D), F32),
                             jax.ShapeDtypeStruct((1, D), F32)],
                  vmem_mb=48)(f, wd, x2, gate2, gf, target)


def _ffn_bwd(dfo, wd, up, ga, vd, cw, tm=256, cw_blk=1536):
    T = up.shape[0]
    tm = min(tm, T)
    dff = up.shape[1] // 2
    ncb = dff // cw_blk
    nrow = T // tm

    def body(dfo_ref, wd_ref, ua_ref, uv_ref, ga_ref, vd_ref, wa_ref, wv_ref,
             du_ref, dwa_ref, dwv_ref, dba_ref, dbv_ref, na, nv):
        i = pl.program_id(1)

        @pl.when(i == 0)
        def _():
            na[...] = jnp.zeros_like(na)
            nv[...] = jnp.zeros_like(nv)
            dwa_ref[...] = jnp.zeros_like(dwa_ref)
            dwv_ref[...] = jnp.zeros_like(dwv_ref)
            dba_ref[...] = jnp.zeros_like(dba_ref)
            dbv_ref[...] = jnp.zeros_like(dbv_ref)

        df = _dot_nt(dfo_ref[...], wd_ref[...])

        def conv_bwd(dh, u_ref, w_ref, nxt, col, dw_ref, db_ref):
            n8 = nxt[...]
            dh1 = _shift_up(dh, n8, 1)
            dh2 = _shift_up(dh, n8, 2)
            nxt[...] = dh[0:SUBLANES]
            du_ref[:, col:col + cw_blk] = (w_ref[2:3, :] * dh + w_ref[1:2, :] * dh1 + w_ref[0:1, :] * dh2).astype(BF)
            u = u_ref[...]
            dw_ref[2:3, :] += _colsum(dh * u)
            dw_ref[1:2, :] += _colsum(dh1 * u)
            dw_ref[0:1, :] += _colsum(dh2 * u)
            db_ref[...] += _colsum(dh)

        conv_bwd(df * vd_ref[...].astype(F32), ua_ref, wa_ref, na, 0, dwa_ref, dba_ref)
        conv_bwd(df * ga_ref[...].astype(F32), uv_ref, wv_ref, nv, cw_blk, dwv_ref, dbv_ref)

    rev = lambda cbk, i: (nrow - 1 - i, cbk)
    rev_v = lambda cbk, i: (nrow - 1 - i, ncb + cbk)
    blk = pl.BlockSpec((tm, cw_blk), rev)
    w3a = pl.BlockSpec((3, cw_blk), lambda cbk, i: (0, cbk))
    w3v = pl.BlockSpec((3, cw_blk), lambda cbk, i: (0, ncb + cbk))
    b1a = pl.BlockSpec((1, cw_blk), lambda cbk, i: (0, cbk))
    return _pcall(body, name="ffn_bwd", grid=(ncb, nrow),
                  in_specs=[pl.BlockSpec((tm, D), lambda cbk, i: (nrow - 1 - i, 0)),
                            pl.BlockSpec((cw_blk, D), lambda cbk, i: (cbk, 0)),
                            blk, pl.BlockSpec((tm, cw_blk), rev_v), blk, blk, w3a, w3v],
                  out_specs=[pl.BlockSpec((tm, 2 * cw_blk), rev), w3a, w3a, b1a, b1a],
                  out_shape=[jax.ShapeDtypeStruct((T, 2 * dff), BF),
                             jax.ShapeDtypeStruct((3, dff), F32), jax.ShapeDtypeStruct((3, dff), F32),
                             jax.ShapeDtypeStruct((1, dff), F32), jax.ShapeDtypeStruct((1, dff), F32)],
                  scratch=[pltpu.VMEM((SUBLANES, cw_blk), F32)] * 2,
                  vmem_mb=48)(dfo, wd, up, up, ga, vd, cw, cw)


def _ffn_col_block(t, ncb):
    return jnp.where(t < ncb, 2 * t, 2 * (t - ncb) + 1)


def _mm_tn_cols(a, b, name, nshard, nb, colmap=None, mb=None, tm=1024):
    T, M = a.shape
    tm = min(tm, T)
    mb = M if mb is None else mb
    ns = b.shape[1] // nshard
    per = ns // nb
    cmap = colmap if colmap is not None else (lambda t: t)

    def body(a_ref, b_ref, o_ref):
        k = pl.program_id(2)

        @pl.when(k == 0)
        def _():
            o_ref[...] = jnp.zeros_like(o_ref)

        o_ref[0] += _dot_tn(a_ref[...], b_ref[...])

    return _pcall(body, name=name, grid=(M // mb, nshard * per, T // tm),
                  in_specs=[pl.BlockSpec((tm, mb), lambda m, t, k: (k, m)),
                            pl.BlockSpec((tm, nb), lambda m, t, k: (k, cmap(t)))],
                  out_specs=pl.BlockSpec((1, mb, nb), lambda m, t, k: (t // per, m, t % per)),
                  out_shape=jax.ShapeDtypeStruct((nshard, M, ns), F32), vmem_mb=48)(a, b)


def _mm_nt_normbwd(dz, w4, x, resid, g, scale, name, gate=None, o=None, dz_blocks=(0, 1, 2, 3), tm=256):
    T = x.shape[0]
    tm = min(tm, T)
    ns = w4.shape[2]
    gated = gate is not None

    def body(*refs):
        if gated:
            (dz_ref, w_ref, x_ref, r_ref, g_ref, sc_ref, gt_ref, o_ref,
             dx_ref, dsh_ref, dsc_ref, dg_ref, do_ref, dgt_ref) = refs
        else:
            dz_ref, w_ref, x_ref, r_ref, g_ref, sc_ref, dx_ref, dsh_ref, dsc_ref, dg_ref = refs
        i = pl.program_id(0)

        @pl.when(i == 0)
        def _():
            dsh_ref[...] = jnp.zeros_like(dsh_ref)
            dsc_ref[...] = jnp.zeros_like(dsc_ref)
            dg_ref[...] = jnp.zeros_like(dg_ref)
            if gated:
                dgt_ref[...] = jnp.zeros_like(dgt_ref)

        dh = None
        for j in range(N_CHIPS):
            blk = dz_blocks[j]
            part = _dot_nt(dz_ref[:, blk * ns:(blk + 1) * ns], w_ref[j])
            dh = part if dh is None else dh + part
        rstd, xh = _rms_stats(x_ref[...])
        dsh_ref[...] += _colsum(dh)
        dsc_ref[...] += _colsum(dh * (xh * g_ref[...]))
        dn = dh * (1.0 + sc_ref[...])
        dg_ref[...] += _colsum(dn * xh)
        dxh = dn * g_ref[...]
        dx = r_ref[...] + rstd * (dxh - xh * jnp.mean(dxh * xh, axis=-1, keepdims=True))
        dx_ref[...] = dx
        if gated:
            do_ref[...] = (gt_ref[...] * dx).astype(BF)
            dgt_ref[...] += _colsum(dx * o_ref[...].astype(F32))

    row = pl.BlockSpec((tm, D), lambda i: (i, 0))
    vec = pl.BlockSpec((1, D), lambda i: (0, 0))
    in_specs = [pl.BlockSpec((tm, N_CHIPS * ns), lambda i: (i, 0)), _resident(w4.shape), row, row, vec, vec]
    out_specs = [row, vec, vec, vec]
    out_shape = [jax.ShapeDtypeStruct((T, D), F32)] + [jax.ShapeDtypeStruct((1, D), F32)] * 3
    args = [dz, w4, x, resid, g, scale]
    if gated:
        in_specs += [vec, row]
        out_specs += [row, vec]
        out_shape += [jax.ShapeDtypeStruct((T, D), BF), jax.ShapeDtypeStruct((1, D), F32)]
        args += [gate, o]
    return _pcall(body, name=name, grid=(T // tm,), in_specs=in_specs, out_specs=out_specs, out_shape=out_shape,
                  vmem_mb=48)(*args)


def _mix_bwd(do, ya, yb, z, wo, wba, wbb, tm=256):
    T = do.shape[0]
    tm = min(tm, T)

    def body(do_ref, ya_ref, yb_ref, ga_ref, gb_ref, wo_ref, wa_ref, wb_ref,
             dz_ref, dya_ref, dyb_ref, dyap_ref, dybp_ref):
        dm = _dot_nt(do_ref[...], wo_ref[...])
        sa = _sigmoid_t(ga_ref[...])
        sb = _sigmoid_t(gb_ref[...])
        dya = (sa * dm).astype(BF)
        dyb = (sb * dm).astype(BF)
        dz_ref[:, 0:D] = (dm * ya_ref[...].astype(F32) * sa * (1.0 - sa)).astype(BF)
        dz_ref[:, D:2 * D] = (dm * yb_ref[...].astype(F32) * sb * (1.0 - sb)).astype(BF)
        dya_ref[...] = dya
        dyb_ref[...] = dyb
        dyap_ref[...] = _dot_nt(dya, wa_ref[...]).astype(BF)
        dybp_ref[...] = _dot_nt(dyb, wb_ref[...]).astype(BF)

    row = pl.BlockSpec((tm, D), lambda i: (i, 0))
    wspec = pl.BlockSpec((D, D), lambda i: (0, 0))
    return _pcall(body, name="mix_bwd", grid=(T // tm,),
                  in_specs=[row, row, row, pl.BlockSpec((tm, D), lambda i: (i, 4)),
                            pl.BlockSpec((tm, D), lambda i: (i, 5)), wspec, wspec, wspec],
                  out_specs=[pl.BlockSpec((tm, 2 * D), lambda i: (i, 2)), row, row, row, row],
                  out_shape=[jax.ShapeDtypeStruct((T, 6 * D), BF)] + [jax.ShapeDtypeStruct((T, D), BF)] * 4,
                  vmem_mb=48)(do, ya, yb, z, z, wo, wba, wbb)


def _sgu_bwd(dz, dyb_pre, z, lg, lb, ws, bst, tb=256):
    T = z.shape[0]
    tb = min(tb, T)

    def body(dz_in, dy_ref, zu_ref, zv_ref, lg_ref, lb_ref, ws_ref, bst_ref,
             dz_ref, dws_ref, dbst_ref, dlg_ref, dlb_ref):
        del dz_in
        i = pl.program_id(0)

        @pl.when(i == 0)
        def _():
            dws_ref[...] = jnp.zeros_like(dws_ref)
            dbst_ref[...] = jnp.zeros_like(dbst_ref)
            dlg_ref[...] = jnp.zeros_like(dlg_ref)
            dlb_ref[...] = jnp.zeros_like(dlb_ref)

        gu, dgu = _gelu_and_grad(zu_ref[...])
        gv, dgv = _gelu_and_grad(zv_ref[...])
        rstd, xh = _layernorm_stats(gv)
        vln = xh * lg_ref[...] + lb_ref[...]
        wm, mixed = _sgu_mix(vln, ws_ref, bst_ref, tb)
        dy = dy_ref[...].astype(F32)
        dz_ref[:, 0:D] = (dy * mixed * dgu).astype(BF)
        dmixed = dy * gu
        ri = lax.broadcasted_iota(jnp.int32, (SGU_BLOCK, SGU_BLOCK), 0)
        ci = lax.broadcasted_iota(jnp.int32, (SGU_BLOCK, SGU_BLOCK), 1)
        blocks = []
        for blk in range(tb // SGU_BLOCK):
            rs = slice(blk * SGU_BLOCK, (blk + 1) * SGU_BLOCK)
            cols = []
            for g in range(HEADS):
                cs = slice(g * HD, (g + 1) * HD)
                dmg = dmixed[rs, cs]
                dmb = dmg.astype(BF)
                dbst_ref[:, g:g + 1] += jnp.sum(dmg, axis=1, keepdims=True)
                dws_ref[g] += jnp.where(ri >= ci, _dot_nt(dmb, vln[rs, cs].astype(BF)), 0.0)
                cols.append(_dot_tn(wm[g], dmb))
            blocks.append(jnp.concatenate(cols, axis=1))
        dvln = blocks[0] if len(blocks) == 1 else jnp.concatenate(blocks, axis=0)
        dlg_ref[...] += _colsum(dvln * xh)
        dlb_ref[...] += _colsum(dvln)
        dxh = dvln * lg_ref[...]
        dgv_in = rstd * (dxh - jnp.mean(dxh, axis=-1, keepdims=True)
                         - xh * jnp.mean(dxh * xh, axis=-1, keepdims=True))
        dz_ref[:, D:2 * D] = (dgv_in * dgv).astype(BF)

    row = pl.BlockSpec((tb, D), lambda i: (i, 0))
    vec = pl.BlockSpec((1, D), lambda i: (0, 0))
    wspec = pl.BlockSpec((HEADS, SGU_BLOCK, SGU_BLOCK), lambda i: (0, 0, 0))
    bspec = pl.BlockSpec((SGU_BLOCK, HEADS), lambda i: (0, 0))
    return _pcall(body, name="sgu_bwd", grid=(T // tb,),
                  in_specs=[HBM_SPEC, row, pl.BlockSpec((tb, D), lambda i: (i, 2)),
                            pl.BlockSpec((tb, D), lambda i: (i, 3)), vec, vec, wspec, bspec],
                  out_specs=[pl.BlockSpec((tb, 2 * D), lambda i: (i, 1)), wspec, bspec, vec, vec],
                  out_shape=[jax.ShapeDtypeStruct(dz.shape, BF),
                             jax.ShapeDtypeStruct((HEADS, SGU_BLOCK, SGU_BLOCK), F32),
                             jax.ShapeDtypeStruct((SGU_BLOCK, HEADS), F32),
                             jax.ShapeDtypeStruct((1, D), F32), jax.ShapeDtypeStruct((1, D), F32)],
                  aliases={0: 0}, vmem_mb=48)(dz, dyb_pre, z, z, lg, lb, ws, bst)


def _rglru_bwd(dz, dya_pre, z, h, cw, cb, wa, ba, wx, bx, lam, tb=256):
    T = z.shape[0]
    tb = min(tb, T)
    nrow = T // tb
    per = tb // SUBLANES

    def body(dz_in, dy_ref, xr_ref, xh_ref, gr_ref, h_ref, hh_ref, cw_ref, cb_ref, wa_ref, ba_ref, wx_ref, bx_ref,
             lam_ref, dz_ref, dcw_ref, dcb_ref, dwa_ref, dba_ref, dwx_ref, dbx_ref, dlam_ref, carry, nxt):
        del dz_in
        i = pl.program_id(0)
        first_block = i == nrow - 1

        @pl.when(i == 0)
        def _():
            carry[...] = jnp.zeros_like(carry)
            nxt[...] = jnp.zeros_like(nxt)
            for ref in (dcw_ref, dcb_ref, dwa_ref, dba_ref, dwx_ref, dbx_ref, dlam_ref):
                ref[...] = jnp.zeros_like(ref)

        xr = xr_ref[...]
        pv = jnp.where(first_block, 0.0, xh_ref[...])
        s1 = _shift_down(xr, pv, 1)
        s2 = _shift_down(xr, pv, 2)
        s3 = _shift_down(xr, pv, 3)
        xc = cb_ref[...] + cw_ref[3:4, :] * xr + cw_ref[2:3, :] * s1 + cw_ref[1:2, :] * s2 + cw_ref[0:1, :] * s3
        lam = lam_ref[...]
        r, ig, ls, a, mult = _lru_gates(xc, wa_ref, ba_ref[...], wx_ref, bx_ref[...], lam)
        hv = h_ref[...]
        hprev = _shift_down(hv, jnp.where(first_block, 0.0, hh_ref[...]), 1)
        gg, dgg = _gelu_and_grad(gr_ref[...])
        dy = dy_ref[...].astype(F32)
        dz_ref[:, D:2 * D] = (dy * hv * dgg).astype(BF)

        rows = lax.broadcasted_iota(jnp.int32, (tb, D), 0)
        v = dy * gg + jnp.where(rows == tb - 1, carry[0:1, :], 0.0)
        q = jnp.where(rows < tb - 1, pltpu.roll(a, tb - 1, 0), 0.0)
        _, gsc = _scan_rows(q, v, reverse=True)
        carry[...] = (a * gsc)[0:SUBLANES]

        xi = ig * xc
        dmult = gsc * xi
        dxi = gsc * mult
        dig = dxi * xc
        dxc = dxi * ig
        dlog_a = gsc * hprev * a - dmult * (a * a) / mult
        dlam_ref[...] += _colsum(dlog_a * r) * (LRU_C * _sigmoid(-lam))
        dpr = dlog_a * (LRU_C * ls) * r * (1.0 - r)
        dpi = dig * ig * (1.0 - ig)
        dba_ref[...] += _colsum(dpr)
        dbx_ref[...] += _colsum(dpi)
        back = []
        for hh in range(HEADS):
            cs = slice(hh * HD, (hh + 1) * HD)
            xh = xc[:, cs].astype(BF)
            dprh = dpr[:, cs].astype(BF)
            dpih = dpi[:, cs].astype(BF)
            dwa_ref[hh] += _dot_tn(xh, dprh)
            dwx_ref[hh] += _dot_tn(xh, dpih)
            back.append(_dot_nt(dprh, wa_ref[hh].astype(BF)) + _dot_nt(dpih, wx_ref[hh].astype(BF)))
        dxc = dxc + jnp.concatenate(back, axis=1)

        n8 = nxt[...]
        dxr = (cw_ref[3:4, :] * dxc + cw_ref[2:3, :] * _shift_up(dxc, n8, 1)
               + cw_ref[1:2, :] * _shift_up(dxc, n8, 2) + cw_ref[0:1, :] * _shift_up(dxc, n8, 3))
        nxt[...] = dxc[0:SUBLANES]
        dz_ref[:, 0:D] = dxr.astype(BF)
        dcw_ref[3:4, :] += _colsum(dxc * xr)
        dcw_ref[2:3, :] += _colsum(dxc * s1)
        dcw_ref[1:2, :] += _colsum(dxc * s2)
        dcw_ref[0:1, :] += _colsum(dxc * s3)
        dcb_ref[...] += _colsum(dxc)

    rev = lambda col: (lambda i: (nrow - 1 - i, col))
    halo = lambda col: pl.BlockSpec((SUBLANES, D), lambda i: (jnp.maximum((nrow - 1 - i) * per - 1, 0), col))
    vec = pl.BlockSpec((1, D), lambda i: (0, 0))
    wspec = pl.BlockSpec((HEADS, HD, HD), lambda i: (0, 0, 0))
    c4 = pl.BlockSpec((4, D), lambda i: (0, 0))
    wshape = jax.ShapeDtypeStruct((HEADS, HD, HD), F32)
    vshape = jax.ShapeDtypeStruct((1, D), F32)
    return _pcall(body, name="rglru_bwd", grid=(nrow,),
                  in_specs=[HBM_SPEC, pl.BlockSpec((tb, D), rev(0)), pl.BlockSpec((tb, D), rev(0)), halo(0),
                            pl.BlockSpec((tb, D), rev(1)), pl.BlockSpec((tb, D), rev(0)), halo(0),
                            c4, vec, wspec, vec, wspec, vec, vec],
                  out_specs=[pl.BlockSpec((tb, 2 * D), rev(0)), c4, vec, wspec, vec, wspec, vec, vec],
                  out_shape=[jax.ShapeDtypeStruct(dz.shape, BF), jax.ShapeDtypeStruct((4, D), F32), vshape,
                             wshape, vshape, wshape, vshape, vshape],
                  scratch=[pltpu.VMEM((SUBLANES, D), F32), pltpu.VMEM((SUBLANES, D), F32)],
                  aliases={0: 0}, vmem_mb=56)(dz, dya_pre, z, z, z, h, h, cw, cb, wa, ba, wx, bx, lam)


def _pack_rows(parts):
    out = []
    for p in parts:
        q = p.reshape(-1, LANES)
        pad = (-q.shape[0]) % SUBLANES
        if pad:
            q = jnp.concatenate([q, jnp.zeros((pad, LANES), q.dtype)], axis=0)
        out.append(q)
    return jnp.concatenate(out, axis=0)


def _rows_of(shape):
    n = 1
    for s in shape:
        n *= s
    rows = n // LANES
    return rows + (-rows) % SUBLANES


def kernel(x, c, w_ada, b_ada, norm_mix_g, w_in, rnn_conv_w, rnn_conv_b, lru_w_a, lru_b_a, lru_w_x, lru_b_x, lru_lambda, sgu_ln_g, sgu_ln_b, sgu_w_s, sgu_b_s, w_branch_a, w_branch_b, w_out, norm_ffn_g, w_up, ffn_conv_w, ffn_conv_b, w_down, norm_final_g, loss_target, m_w_ada, m_b_ada, m_norm_mix_g, m_w_in, m_rnn_conv_w, m_rnn_conv_b, m_lru_w_a, m_lru_b_a, m_lru_w_x, m_lru_b_x, m_lru_lambda, m_sgu_ln_g, m_sgu_ln_b, m_sgu_w_s, m_sgu_b_s, m_w_branch_a, m_w_branch_b, m_w_out, m_norm_ffn_g, m_w_up, m_ffn_conv_w, m_ffn_conv_b, m_w_down, m_norm_final_g, v_w_ada, v_b_ada, v_norm_mix_g, v_w_in, v_rnn_conv_w, v_rnn_conv_b, v_lru_w_a, v_lru_b_a, v_lru_w_x, v_lru_b_x, v_lru_lambda, v_sgu_ln_g, v_sgu_ln_b, v_sgu_w_s, v_sgu_b_s, v_w_branch_a, v_w_branch_b, v_w_out, v_norm_ffn_g, v_w_up, v_ffn_conv_w, v_ffn_conv_b, v_w_down, v_norm_final_g):
    args = dict(locals())
    T = x.shape[1]
    mx, my, mc = lax.axis_index("x"), lax.axis_index("y"), lax.axis_index("c")
    chip = 2 * mx + my
    dev = 2 * chip + mc
    vec = lambda a: a.reshape(1, -1)

    xt = x.reshape(T, D)
    tgt = loss_target.reshape(T, D)
    ns = w_in.shape[2]
    dff = w_down.shape[1] * N_CHIPS

    c_all = _gather8(c.reshape(SUBLANES, LANES), "gather_c").reshape(N_DEV, D)
    b_ada_sh = lax.dynamic_slice(b_ada, (0, chip * ns), (1, ns))
    mod_sh = _mod_fwd(c_all, w_ada[0], b_ada_sh)

    (w_in_b,) = _cast_shards([w_in[0]], "cast_w_in")
    w_in4, rcw4, fcw4, mod4 = _gather_weights([w_in_b, rnn_conv_w[0], ffn_conv_w[0], mod_sh],
                                              [True, False, False, False])
    late = _cast_shards([w_up[0], w_down[0], w_branch_a[0], w_branch_b[0], w_out[0]], "cast_late", after=mod4)
    late_plan = _gather_half_plan([w.shape for w in late])
    late_handle, late_token = _remote_start(
        late, [lax.empty((N_CHIPS,) + w.shape, w.dtype) for w in late], late_plan, 3 * len(late), "gather_late_start")
    rcw_full = jnp.transpose(rcw4, (1, 0, 2)).reshape(4, D)
    fcw_full = jnp.transpose(fcw4, (1, 0, 2)).reshape(3, 2 * dff)
    mod = lax.dynamic_index_in_dim(mod4, dev, axis=1, keepdims=False).reshape(1, 6 * D)
    shift1, scale1, gate1, shift2, scale2, gate2 = [mod[:, k * D:(k + 1) * D] for k in range(6)]

    h1, z = _modnorm_matmul(xt, norm_mix_g, scale1 + late_token[0:1, 0:1], shift1, w_in4, "norm_in_proj")
    bst = jnp.transpose(sgu_b_s[0])
    h_lru, ya_pre = _rglru_fwd(z, rcw_full, rnn_conv_b, lru_w_a[0], lru_b_a, lru_w_x[0], lru_b_x, lru_lambda)
    yb_pre = _sgu_fwd(z, sgu_ln_g, sgu_ln_b, sgu_w_s[0], bst)
    late, late_lands = _remote_wait(late_handle, late_plan, yb_pre, "gather_late_wait")
    w_up4, w_down4, wba4, wbb4, wo4 = _forward_halves(late, late_lands)
    wd_full = w_down4.reshape(dff, D)
    wba_full = wba4.reshape(D, D)
    wbb_full = wbb4.reshape(D, D)
    wo_full = wo4.reshape(D, D)
    x2, merged, ya, yb, o1 = _mix_out(ya_pre, yb_pre, z, xt, gate1, wba_full, wbb_full, wo_full)
    h2, up = _modnorm_matmul(x2, norm_ffn_g, scale2, shift2, w_up4, "norm_up_proj")
    f, ffn_ga, ffn_vd = _ffn_gate(up, fcw_full, ffn_conv_b)
    loss_part, dx3, dfo, dgf, dgate2 = _ffn_down_loss(f, wd_full, x2, gate2, vec(norm_final_g), tgt)

    ffn_map = functools.partial(_ffn_col_block, ncb=2)
    dup, dfcw_a, dfcw_v, dfcb_a, dfcb_v = _ffn_bwd(dfo, wd_full, up, ffn_ga, ffn_vd, fcw_full, cw_blk=ns)
    dwd = _mm_tn_cols(f, dfo, "dw_down", 1, D, mb=D)
    dw_up4 = _mm_tn_cols(h2, dup, "dw_up", N_CHIPS, ns, colmap=ffn_map)
    dx2, dshift2, dscale2, dg_ffn, do1, dgate1 = _mm_nt_normbwd(
        dup, w_up4, x2, dx3, norm_ffn_g, scale2, "dh2_norm_bwd", gate=gate1, o=o1, dz_blocks=(0, 2, 1, 3))
    dz, dya, dyb, dya_pre, dyb_pre = _mix_bwd(do1, ya, yb, z, wo_full, wba_full, wbb_full)
    dwo = _mm_tn_cols(merged, do1, "dw_out", 1, D)
    dwba = _mm_tn_cols(ya_pre, dya, "dw_branch_a", 1, D)
    dwbb = _mm_tn_cols(yb_pre, dyb, "dw_branch_b", 1, D)

    core = mc.astype(jnp.int32).reshape(1)
    chip_id = chip.astype(jnp.int32).reshape(1)

    def reduce_start(group, name):
        from_core = _send_other_half([g for _, g in group], "swap_halves_" + name)
        sums = [_add_halves_bf16(core, g, o, "sum_cores_" + n) for (n, g), o in zip(group, from_core)]
        lands = [lax.empty((3,) + s.shape[1:], s.dtype) for s in sums]
        return _remote_start(sums, lands, _scatter_plan(len(group)), 3 * len(group), "scatter_start_" + name)

    def reduce_finish(group, handle, after, name):
        sums, landed = _remote_wait(handle, _scatter_plan(len(group)), after, "scatter_wait_" + name)
        return [_sum_own_and_landed(chip_id, s, l, "sum_chips_" + n) for (n, _), s, l in zip(group, sums, landed)]

    group1 = [("w_up", dw_up4), ("w_down", dwd.reshape(N_CHIPS, dff // N_CHIPS, D)),
              ("w_branch_a", dwba.reshape(N_CHIPS, D // N_CHIPS, D)),
              ("w_branch_b", dwbb.reshape(N_CHIPS, D // N_CHIPS, D)), ("w_out", dwo.reshape(N_CHIPS, D // N_CHIPS, D))]
    handle1, token1 = reduce_start(group1, "late")
    dz, dws, dbst, dlg, dlb = _sgu_bwd(dz, dyb_pre, z, sgu_ln_g + token1[0:1, 0:1], sgu_ln_b, sgu_w_s[0], bst)
    dz, drcw, drcb, dwa, dba, dwx, dbx, dlam = _rglru_bwd(
        dz, dya_pre, z, h_lru, rcw_full, rnn_conv_b, lru_w_a[0], lru_b_a, lru_w_x[0], lru_b_x, lru_lambda)
    early_small = [("rnn_conv_b", drcb), ("lru_w_a", dwa), ("lru_b_a", dba), ("lru_w_x", dwx), ("lru_b_x", dbx),
                   ("lru_lambda", dlam), ("sgu_ln_g", dlg), ("sgu_ln_b", dlb), ("sgu_w_s", dws),
                   ("sgu_b_s", jnp.transpose(dbst)), ("norm_ffn_g", dg_ffn),
                   ("ffn_conv_b", jnp.concatenate([dfcb_a, dfcb_v], axis=1)), ("norm_final_g", dgf)]
    r_early = sum(_rows_of(args[n].shape) for n, _ in early_small)
    early_pack = _pack_rows([g for _, g in early_small] + [drcw, jnp.concatenate([dfcw_a, dfcw_v], axis=1)])
    early_pack = jnp.concatenate(
        [early_pack, jnp.zeros(((-early_pack.shape[0]) % 256, LANES), F32)], axis=0)
    early_chip = _add_pair(early_pack, _swap_core(early_pack, "swap_small_grads"), "sum_cores_small_grads")
    early_handle, token3 = _remote_start([early_chip], [lax.empty((3,) + early_chip.shape, F32)], _bcast_plan, 3,
                                         "small_grads_start")
    totals1 = reduce_finish(group1, handle1, drcb, "late")
    group2 = [("w_in", _mm_tn_cols(h1, dz, "dw_in", N_CHIPS, ns))]
    handle2, token2 = reduce_start(group2, "in")
    grad_x, dshift1, dscale1, dg_mix = _mm_nt_normbwd(
        dz, w_in4, xt, dx2, norm_mix_g + (token2[0:1, 0:1] + token3[0:1, 0:1]), scale1, "dh1_norm_bwd")
    totals2 = reduce_finish(group2, handle2, dg_mix, "in")
    dmod = jnp.concatenate([dshift1, dscale1, dgate1, dshift2, dscale2, dgate2], axis=1)

    big = group1 + group2
    fulls = _share_halves(totals1 + totals2)
    out = {}
    for (n, _), full in zip(big, fulls):
        shape = args[n].shape
        res = _adamw(args[n][0], args["m_" + n][0], args["v_" + n][0], [full.reshape(shape[1:])], "adamw_" + n)
        for kind, r in zip(("grad_", "delta_", "new_m_", "new_v_"), res):
            out[kind + n] = r.reshape(shape)

    late_small = [("b_ada", dmod), ("norm_mix_g", dg_mix)]
    small = late_small + early_small
    late_all = _gather8(_pack_rows([g for _, g in late_small]), "gather_late_small_grads")
    late_sum = _sum_parts(late_all, "sum_late_small_grads")
    _, (early_landed,) = _remote_wait(early_handle, _bcast_plan, dg_mix, "small_grads_wait")
    early_sum = _sum_chips_in_order(chip_id, early_chip, early_landed, "sum_early_small_grads")
    r_small = sum(_rows_of(args[n].shape) for n, _ in small)
    r_pad = r_small + (-r_small) % 256
    fill = jnp.zeros((r_pad - r_small, LANES), F32)
    g_small = jnp.concatenate([late_sum, early_sum[:r_early], fill], axis=0)

    def pack_small(prefix):
        return jnp.concatenate([_pack_rows([args[prefix + n] for n, _ in small]), fill], axis=0)

    res = _adamw(pack_small(""), pack_small("m_"), pack_small("v_"), [g_small], "adamw_small")
    off = 0
    for n, _ in small:
        shape = args[n].shape
        rows = _rows_of(shape)
        for kind, r in zip(("grad_", "delta_", "new_m_", "new_v_"), res):
            out[kind + n] = r[off:off + rows].reshape(shape)
        off += rows

    rcw_cols = rnn_conv_w.shape[2]
    g_rcw = lax.dynamic_slice(early_sum[r_early:r_early + 32].reshape(4, D), (0, chip * rcw_cols), (4, rcw_cols))
    g_fcw = lax.dynamic_slice(early_sum[r_early + 32:r_early + 32 + 144].reshape(3, 2 * dff), (0, chip * ns), (3, ns))
    conv = [("rnn_conv_w", g_rcw), ("ffn_conv_w", g_fcw)]
    res = _adamw(_pack_rows([args[n] for n, _ in conv]), _pack_rows([args["m_" + n] for n, _ in conv]),
                 _pack_rows([args["v_" + n] for n, _ in conv]), [_pack_rows([g for _, g in conv])], "adamw_conv")
    off = 0
    for n, _ in conv:
        shape = args[n].shape
        cnt = shape[1] * shape[2] // LANES
        for kind, r in zip(("grad_", "delta_", "new_m_", "new_v_"), res):
            out[kind + n] = r[off:off + cnt].reshape(shape)
        off += _rows_of(shape)

    dmod_all = late_all[:, 0:6 * D // LANES, :].reshape(N_DEV, 6 * D)
    dmod_sh = lax.dynamic_slice(dmod_all, (0, chip * ns), (N_DEV, ns))
    res = _ada_adamw(jnp.transpose(c_all), dmod_sh, w_ada[0], m_w_ada[0], v_w_ada[0])
    for kind, r in zip(("grad_", "delta_", "new_m_", "new_v_"), res):
        out[kind + "w_ada"] = r.reshape(w_ada.shape)

    loss = lax.psum(loss_part[0, 0], ("x", "y", "c"))
    names = ["w_ada", "b_ada", "norm_mix_g", "w_in", "rnn_conv_w", "rnn_conv_b", "lru_w_a", "lru_b_a", "lru_w_x",
             "lru_b_x", "lru_lambda", "sgu_ln_g", "sgu_ln_b", "sgu_w_s", "sgu_b_s", "w_branch_a", "w_branch_b",
             "w_out", "norm_ffn_g", "w_up", "ffn_conv_w", "ffn_conv_b", "w_down", "norm_final_g"]
    result = [loss, grad_x.reshape(x.shape)]
    for kind in ("grad_", "delta_", "new_m_", "new_v_"):
        result += [out[kind + n] for n in names]
    return tuple(result)
```

```python
import functools

import jax
import jax.numpy as jnp
from jax import lax
from jax.experimental import pallas as pl
from jax.experimental.pallas import tpu as pltpu

F32 = jnp.float32
BF = jnp.bfloat16

D = 1024
HEADS = 8
HD = D // HEADS
SGU_BLOCK = 128
N_CHIPS = 4
N_DEV = 8
EPS = 1e-6
LRU_C = 8.0
LANES = 128
SUBLANES = 8

ADAM_LR = 0.001
ADAM_B1 = 0.9
ADAM_B2 = 0.999
ADAM_EPS = 1e-08
ADAM_WD = 0.01
ADAM_STEP = 10

GELU_K0 = 0.7978845608028654
GELU_K1 = 0.044715

HBM_SPEC = pl.BlockSpec(memory_space=pltpu.HBM)
MESH_ID = pl.DeviceIdType.MESH


def _pcall(body, *, name, out_shape, grid=(), in_specs=None, out_specs=None, scratch=(), vmem_mb=32, aliases=None,
           grid_spec=None):
    kw = {}
    if aliases:
        kw["input_output_aliases"] = aliases
    if grid_spec is not None:
        kw["grid_spec"] = grid_spec
        ndim = len(grid_spec.grid)
    else:
        kw.update(grid=grid, in_specs=in_specs, out_specs=out_specs, scratch_shapes=list(scratch))
        ndim = len(grid)
    if ndim:
        params = pltpu.CompilerParams(dimension_semantics=("arbitrary",) * ndim, vmem_limit_bytes=vmem_mb * 2 ** 20)
    else:
        params = pltpu.CompilerParams(vmem_limit_bytes=vmem_mb * 2 ** 20)
    return pl.pallas_call(body, name=name, out_shape=out_shape, compiler_params=params, **kw)


def _gelu(x):
    return 0.5 * x * (1.0 + jnp.tanh(GELU_K0 * (x + GELU_K1 * x * x * x)))


def _gelu_and_grad(x):
    x2 = x * x
    t = jnp.tanh(GELU_K0 * x * (1.0 + GELU_K1 * x2))
    g = 0.5 * x * (1.0 + t)
    dg = 0.5 * (1.0 + t) + 0.5 * x * (1.0 - t * t) * (GELU_K0 * (1.0 + 3.0 * GELU_K1 * x2))
    return g, dg


def _sigmoid(x):
    return 1.0 / (1.0 + jnp.exp(-x))


def _sigmoid_t(x):
    return 0.5 * jnp.tanh(0.5 * x) + 0.5


def _log_sigmoid(x):
    e = jnp.exp(-jnp.abs(x))
    u = 1.0 + e
    d = u - 1.0
    l1p = jnp.where(d == 0.0, e, jnp.log(u) * (e / jnp.where(d == 0.0, 1.0, d)))
    return jnp.minimum(x, 0.0) - l1p


def _dot(a, b):
    return jnp.dot(a, b, preferred_element_type=F32)


def _dot_nt(a, b):
    return lax.dot_general(a, b, (((1,), (1,)), ((), ())), preferred_element_type=F32)


def _dot_tn(a, b):
    return lax.dot_general(a, b, (((0,), (0,)), ((), ())), preferred_element_type=F32)


def _shift_down(x, halo, s):
    r = pltpu.roll(x, s, 0)
    rows = lax.broadcasted_iota(jnp.int32, (SUBLANES, x.shape[1]), 0)
    head = jnp.where(rows < s, pltpu.roll(halo, s, 0), r[0:SUBLANES])
    return jnp.concatenate([head, r[SUBLANES:]], axis=0)


def _shift_up(x, halo, s):
    n = x.shape[0]
    r = pltpu.roll(x, n - s, 0)
    rows = lax.broadcasted_iota(jnp.int32, (SUBLANES, x.shape[1]), 0)
    tail = jnp.where(rows >= SUBLANES - s, pltpu.roll(halo, SUBLANES - s, 0), r[n - SUBLANES:n])
    return jnp.concatenate([r[:n - SUBLANES], tail], axis=0)


def _scan_rows(a, u, reverse):
    n, width = a.shape
    rows = lax.broadcasted_iota(jnp.int32, (n, width), 0)
    d = 1
    while d < n:
        if d < SUBLANES:
            keep = rows < n - d if reverse else rows >= d
            shift = n - d if reverse else d
            a_s = jnp.where(keep, pltpu.roll(a, shift, 0), 1.0)
            u_s = jnp.where(keep, pltpu.roll(u, shift, 0), 0.0)
        elif reverse:
            a_s = jnp.concatenate([a[d:], jnp.ones((d, width), a.dtype)], axis=0)
            u_s = jnp.concatenate([u[d:], jnp.zeros((d, width), u.dtype)], axis=0)
        else:
            a_s = jnp.concatenate([jnp.ones((d, width), a.dtype), a[:n - d]], axis=0)
            u_s = jnp.concatenate([jnp.zeros((d, width), u.dtype), u[:n - d]], axis=0)
        u = a * u_s + u
        a = a * a_s
        d *= 2
    return a, u


def _colsum(x):
    return jnp.sum(x, axis=0, keepdims=True)


def _rms_stats(x):
    r = lax.rsqrt(jnp.mean(x * x, axis=-1, keepdims=True) + EPS)
    return r, x * r


def _lru_gates(xc, wa_ref, ba, wx_ref, bx, lam):
    pr, pi = [], []
    for hh in range(HEADS):
        xh = xc[:, hh * HD:(hh + 1) * HD].astype(BF)
        pr.append(_dot(xh, wa_ref[hh].astype(BF)))
        pi.append(_dot(xh, wx_ref[hh].astype(BF)))
    r = _sigmoid_t(jnp.concatenate(pr, axis=1) + ba)
    ig = _sigmoid_t(jnp.concatenate(pi, axis=1) + bx)
    ls = _log_sigmoid(lam)
    log_a = LRU_C * r * ls
    a = jnp.exp(log_a)
    x2 = 2.0 * log_a
    u = a * a
    lu = jnp.log(jnp.maximum(u, 1e-37))
    em1 = jnp.where(lu == 0.0, x2, jnp.where(u < 1e-30, -1.0, (u - 1.0) * x2 / jnp.where(lu == 0.0, 1.0, lu)))
    mult = jnp.sqrt(-em1)
    return r, ig, ls, a, mult


def _sgu_mix(vln, ws_ref, bst_ref, tb):
    ri = lax.broadcasted_iota(jnp.int32, (SGU_BLOCK, SGU_BLOCK), 0)
    ci = lax.broadcasted_iota(jnp.int32, (SGU_BLOCK, SGU_BLOCK), 1)
    wm = [jnp.where(ri >= ci, ws_ref[g], 0.0).astype(BF) for g in range(HEADS)]
    blocks = []
    for blk in range(tb // SGU_BLOCK):
        cols = []
        for g in range(HEADS):
            vb = vln[blk * SGU_BLOCK:(blk + 1) * SGU_BLOCK, g * HD:(g + 1) * HD].astype(BF)
            cols.append(_dot(wm[g], vb) + bst_ref[:, g:g + 1])
        blocks.append(jnp.concatenate(cols, axis=1))
    mixed = blocks[0] if len(blocks) == 1 else jnp.concatenate(blocks, axis=0)
    return wm, mixed


def _layernorm_stats(v):
    mu = jnp.mean(v, axis=-1, keepdims=True)
    vc = v - mu
    rstd = lax.rsqrt(jnp.mean(vc * vc, axis=-1, keepdims=True) + EPS)
    return rstd, vc * rstd


def _my_xyc():
    return lax.axis_index("x"), lax.axis_index("y"), lax.axis_index("c")


def _gather_weights(srcs, halve):
    n = len(srcs)
    out_shape = [jax.ShapeDtypeStruct((N_CHIPS,) + s.shape, s.dtype) for s in srcs]

    def body(*refs):
        src, out = refs[:n], refs[n:2 * n]
        send_sems, recv_sems, fwd_send, fwd_recv, loc_sems = refs[2 * n:]
        x, y, c = _my_xyc()
        me = 2 * x + y
        chips = [(1 - x, y), (x, 1 - y), (1 - x, 1 - y)]

        def half(ref, a, which):
            if not halve[a]:
                return ref
            h = srcs[a].shape[0] // 2
            return ref.at[pl.ds(which * h, h)]

        def ici(a, k, frm):
            px, py = chips[k]
            return pltpu.make_async_remote_copy(
                src_ref=half(src[a], a, c), dst_ref=half(out[a].at[frm], a, c),
                send_sem=send_sems.at[a, k], recv_sem=recv_sems.at[a, k],
                device_id=(px, py, c), device_id_type=MESH_ID)

        def d2d(a, k, which):
            px, py = chips[k]
            rows = half(out[a].at[2 * px + py], a, which)
            return pltpu.make_async_remote_copy(
                src_ref=rows, dst_ref=rows, send_sem=fwd_send.at[a, k], recv_sem=fwd_recv.at[a, k],
                device_id=(x, y, 1 - c), device_id_type=MESH_ID)

        local, sends = [], []
        for a in range(n):
            lc = pltpu.make_async_copy(src[a], out[a].at[me], loc_sems.at[a])
            lc.start()
            local.append(lc)
            for k in range(3):
                cp = ici(a, k, me)
                cp.start()
                sends.append(cp)
        for a in range(n):
            for k in range(3):
                px, py = chips[k]
                ici(a, k, 2 * px + py).wait_recv()
                if halve[a]:
                    fw = d2d(a, k, c)
                    fw.start()
                    sends.append(fw)
        for a in range(n):
            if halve[a]:
                for k in range(3):
                    d2d(a, k, 1 - c).wait_recv()
        for cp in sends:
            cp.wait_send()
        for lc in local:
            lc.wait()

    sem = pltpu.SemaphoreType.DMA((n, 3))
    return _pcall(body, name="gather_weights", out_shape=out_shape, in_specs=[HBM_SPEC] * n,
                  out_specs=[HBM_SPEC] * n, scratch=[sem, sem, sem, sem, pltpu.SemaphoreType.DMA((n,))])(*srcs)


SEM_SPEC = pl.BlockSpec(memory_space=pltpu.SEMAPHORE)


def _remote_start(srcs, lands, plan, ncopies, name):
    n, m = len(srcs), len(lands)

    def body(*refs):
        src, land = refs[:n], refs[n:n + m]
        send_sems, recv_sems = refs[n + m], refs[n + m + 1]
        token = refs[-1]
        x, y, c = _my_xyc()
        for i, (s, d, dev) in enumerate(plan(src, land, x, y, c)):
            pltpu.make_async_remote_copy(src_ref=s, dst_ref=d, send_sem=send_sems.at[i], recv_sem=recv_sems.at[i],
                                         device_id=dev, device_id_type=MESH_ID).start()
        token[...] = jnp.zeros_like(token)

    bufs = list(srcs) + list(lands)
    out = pl.pallas_call(
        body, name=name,
        out_shape=(pltpu.SemaphoreType.DMA((ncopies,)), pltpu.SemaphoreType.DMA((ncopies,)),
                   *[pltpu.HBM(b.shape, b.dtype) for b in bufs], jax.ShapeDtypeStruct((SUBLANES, LANES), F32)),
        in_specs=[HBM_SPEC] * (n + m),
        out_specs=(SEM_SPEC, SEM_SPEC, *[HBM_SPEC] * (n + m), pl.BlockSpec(memory_space=pltpu.VMEM)),
        input_output_aliases={i: 2 + i for i in range(n + m)},
        compiler_params=pltpu.CompilerParams(has_side_effects=pltpu.SideEffectType.DATAFLOW_SIDE_EFFECTING),
    )(*[pltpu.with_memory_space_constraint(b, pltpu.HBM) for b in bufs])
    return (out[0], out[1], out[2:2 + n], out[2 + n:2 + n + m]), out[-1]


def _remote_wait(handle, plan, after, name):
    send_sems, recv_sems, srcs, lands = handle
    n, m = len(srcs), len(lands)

    def body(*refs):
        src, land = refs[:n], refs[n:n + m]
        ssem, rsem = refs[n + m], refs[n + m + 1]
        x, y, c = _my_xyc()
        for i, (s, d, dev) in enumerate(plan(src, land, x, y, c)):
            cp = pltpu.make_async_remote_copy(src_ref=s, dst_ref=d, send_sem=ssem.at[i], recv_sem=rsem.at[i],
                                              device_id=dev, device_id_type=MESH_ID)
            cp.wait_send()
            cp.wait_recv()

    bufs = list(srcs) + list(lands)
    out = pl.pallas_call(
        body, name=name, out_shape=tuple(pltpu.HBM(b.shape, b.dtype) for b in bufs),
        in_specs=[HBM_SPEC] * (n + m) + [SEM_SPEC, SEM_SPEC, pl.BlockSpec(memory_space=pl.ANY)],
        out_specs=tuple([HBM_SPEC] * (n + m)), input_output_aliases={i: i for i in range(n + m)},
        compiler_params=pltpu.CompilerParams(has_side_effects=pltpu.SideEffectType.DATAFLOW_SIDE_EFFECTING),
    )(*bufs, send_sems, recv_sems, after)
    return out[:n], out[n:]


def _chips_of(x, y):
    return [(1 - x, y), (x, 1 - y), (1 - x, 1 - y)]


def _gather_half_plan(shapes):
    def plan(src, land, x, y, c):
        me = 2 * x + y
        out = []
        for a, shape in enumerate(shapes):
            h = shape[0] // 2
            rows = pl.ds(c * h, h)
            for px, py in _chips_of(x, y):
                out.append((src[a].at[rows], land[a].at[me, rows], (px, py, c)))
        return out

    return plan


def _forward_halves(srcs, lands):
    n = len(srcs)

    def body(*refs):
        src, land = refs[:n], refs[2 * n:3 * n]
        send_sems, recv_sems, loc_sems = refs[3 * n:]
        x, y, c = _my_xyc()
        me = 2 * x + y

        def fwd(a, k, which):
            px, py = _chips_of(x, y)[k]
            h = srcs[a].shape[0] // 2
            rows = land[a].at[2 * px + py, pl.ds(which * h, h)]
            return pltpu.make_async_remote_copy(
                src_ref=rows, dst_ref=rows, send_sem=send_sems.at[a, k], recv_sem=recv_sems.at[a, k],
                device_id=(x, y, 1 - c), device_id_type=MESH_ID)

        local, sends = [], []
        for a in range(n):
            lc = pltpu.make_async_copy(src[a], land[a].at[me], loc_sems.at[a])
            lc.start()
            local.append(lc)
            for k in range(3):
                cp = fwd(a, k, c)
                cp.start()
                sends.append(cp)
        for a in range(n):
            for k in range(3):
                fwd(a, k, 1 - c).wait_recv()
        for cp in sends:
            cp.wait_send()
        for lc in local:
            lc.wait()

    sem = pltpu.SemaphoreType.DMA((n, 3))
    return _pcall(body, name="forward_halves", out_shape=[jax.ShapeDtypeStruct(l.shape, l.dtype) for l in lands],
                  in_specs=[HBM_SPEC] * (2 * n), out_specs=[HBM_SPEC] * n, aliases={n + a: a for a in range(n)},
                  scratch=[sem, sem, pltpu.SemaphoreType.DMA((n,))])(*srcs, *lands)


def _send_other_half(grads, name):
    n = len(grads)
    out_shape = [jax.ShapeDtypeStruct((N_CHIPS, g.shape[1] // 2, g.shape[2]), g.dtype) for g in grads]

    def body(*refs):
        src, out = refs[:n], refs[n:2 * n]
        send_sems, recv_sems = refs[2 * n:]
        x, y, c = _my_xyc()
        cps = []
        for a in range(n):
            h = grads[a].shape[1] // 2
            cp = pltpu.make_async_remote_copy(
                src_ref=src[a].at[:, pl.ds((1 - c) * h, h), :], dst_ref=out[a],
                send_sem=send_sems.at[a], recv_sem=recv_sems.at[a],
                device_id=(x, y, 1 - c), device_id_type=MESH_ID)
            cp.start()
            cps.append(cp)
        for cp in cps:
            cp.wait()

    return _pcall(body, name=name, out_shape=out_shape, in_specs=[HBM_SPEC] * n,
                  out_specs=[HBM_SPEC] * n,
                  scratch=[pltpu.SemaphoreType.DMA((n,)), pltpu.SemaphoreType.DMA((n,))])(*grads)


def _share_halves(totals):
    n = len(totals)
    out_shape = [jax.ShapeDtypeStruct((2,) + t.shape, t.dtype) for t in totals]

    def body(*refs):
        src, out = refs[:n], refs[n:2 * n]
        send_sems, recv_sems, loc_sems = refs[2 * n:]
        x, y, c = _my_xyc()
        cps, local = [], []
        for a in range(n):
            lc = pltpu.make_async_copy(src[a], out[a].at[c], loc_sems.at[a])
            lc.start()
            local.append(lc)
            cp = pltpu.make_async_remote_copy(
                src_ref=src[a], dst_ref=out[a].at[c], send_sem=send_sems.at[a], recv_sem=recv_sems.at[a],
                device_id=(x, y, 1 - c), device_id_type=MESH_ID)
            cp.start()
            cps.append(cp)
        for cp in cps:
            cp.wait()
        for lc in local:
            lc.wait()

    sem = pltpu.SemaphoreType.DMA((n,))
    return _pcall(body, name="share_halves", out_shape=out_shape, in_specs=[HBM_SPEC] * n,
                  out_specs=[HBM_SPEC] * n, scratch=[sem, sem, sem])(*totals)


def _gather8(src, name):
    def body(src_ref, out_ref, send_sems, recv_sems, loc_sem):
        x, y, c = _my_xyc()
        me = 4 * x + 2 * y + c
        lc = pltpu.make_async_copy(src_ref, out_ref.at[me], loc_sem)
        lc.start()
        cps = []
        for k in range(1, N_DEV):
            px = 1 - x if (k >> 2) & 1 else x
            py = 1 - y if (k >> 1) & 1 else y
            pc = 1 - c if k & 1 else c
            cp = pltpu.make_async_remote_copy(
                src_ref=src_ref, dst_ref=out_ref.at[me], send_sem=send_sems.at[k - 1], recv_sem=recv_sems.at[k - 1],
                device_id=(px, py, pc), device_id_type=MESH_ID)
            cp.start()
            cps.append(cp)
        for cp in cps:
            cp.wait()
        lc.wait()

    return _pcall(body, name=name, out_shape=jax.ShapeDtypeStruct((N_DEV,) + src.shape, src.dtype),
                  in_specs=[HBM_SPEC], out_specs=HBM_SPEC,
                  scratch=[pltpu.SemaphoreType.DMA((N_DEV - 1,)), pltpu.SemaphoreType.DMA((N_DEV - 1,)),
                           pltpu.SemaphoreType.DMA])(src)


def _cast_shards(arrs, name, after=None):
    n = len(arrs)
    extra = [] if after is None else [after]

    def body(*refs):
        ins, outs = refs[:n], refs[n + len(extra):]
        for a in range(n):
            outs[a][...] = ins[a][...].astype(BF)

    specs = [pl.BlockSpec((s.shape[0] // 4, s.shape[1]), lambda i: (i, 0)) for s in arrs]
    return _pcall(body, name=name, grid=(4,), in_specs=specs + [pl.BlockSpec(memory_space=pl.ANY)] * len(extra),
                  out_specs=specs, out_shape=[jax.ShapeDtypeStruct(s.shape, BF) for s in arrs])(*arrs, *extra)


def _row_tile(rows, cols):
    t = rows
    while t * cols * 4 > (3 << 19) and t % 16 == 0:
        t //= 2
    return t


def _sum_parts(parts, name):
    p, rows, cols = parts.shape
    tr = _row_tile(rows, cols * p // 2)

    def body(p_ref, o_ref):
        acc = p_ref[0].astype(F32)
        for k in range(1, p):
            acc = acc + p_ref[k].astype(F32)
        o_ref[...] = acc

    return _pcall(body, name=name, grid=(rows // tr,),
                  in_specs=[pl.BlockSpec((p, tr, cols), lambda i: (0, i, 0))],
                  out_specs=pl.BlockSpec((tr, cols), lambda i: (i, 0)),
                  out_shape=jax.ShapeDtypeStruct((rows, cols), F32), vmem_mb=48)(parts)


def _add_halves_bf16(core, grad, other, name):
    nchip, rows, cols = grad.shape
    h = rows // 2
    tr = _row_tile(h, cols)
    nh = h // tr

    def body(c_ref, g_ref, o_ref, s_ref):
        del c_ref
        s_ref[...] = (g_ref[...] + o_ref[...]).astype(BF)

    grid_spec = pltpu.PrefetchScalarGridSpec(
        num_scalar_prefetch=1, grid=(nchip, nh),
        in_specs=[pl.BlockSpec((1, tr, cols), lambda p, i, c_ref: (p, c_ref[0] * nh + i, 0)),
                  pl.BlockSpec((1, tr, cols), lambda p, i, c_ref: (p, i, 0))],
        out_specs=pl.BlockSpec((1, tr, cols), lambda p, i, c_ref: (p, i, 0)))
    return _pcall(body, name=name, grid_spec=grid_spec, out_shape=jax.ShapeDtypeStruct((nchip, h, cols), BF),
                  vmem_mb=48)(core, grad, other)


def _sum_own_and_landed(chip, sums, landed, name):
    _, rows, cols = sums.shape
    tr = _row_tile(rows, 2 * cols)

    def body(chip_ref, own_ref, land_ref, o_ref):
        del chip_ref
        acc = own_ref[0].astype(F32)
        for k in range(3):
            acc = acc + land_ref[k].astype(F32)
        o_ref[...] = acc

    grid_spec = pltpu.PrefetchScalarGridSpec(
        num_scalar_prefetch=1, grid=(rows // tr,),
        in_specs=[pl.BlockSpec((1, tr, cols), lambda i, chip_ref: (chip_ref[0], i, 0)),
                  pl.BlockSpec((3, tr, cols), lambda i, chip_ref: (0, i, 0))],
        out_specs=pl.BlockSpec((tr, cols), lambda i, chip_ref: (i, 0)))
    return _pcall(body, name=name, grid_spec=grid_spec, out_shape=jax.ShapeDtypeStruct((rows, cols), F32),
                  vmem_mb=48)(chip, sums, landed)


def _swap_core(arr, name):
    def body(src, out, send_sem, recv_sem):
        x, y, c = _my_xyc()
        cp = pltpu.make_async_remote_copy(src_ref=src, dst_ref=out, send_sem=send_sem, recv_sem=recv_sem,
                                          device_id=(x, y, 1 - c), device_id_type=MESH_ID)
        cp.start()
        cp.wait()

    return _pcall(body, name=name, out_shape=jax.ShapeDtypeStruct(arr.shape, arr.dtype), in_specs=[HBM_SPEC],
                  out_specs=HBM_SPEC, scratch=[pltpu.SemaphoreType.DMA, pltpu.SemaphoreType.DMA])(arr)


def _add_pair(a, b, name):
    rows, cols = a.shape
    tr = _row_tile(rows, 2 * cols)

    def body(a_ref, b_ref, o_ref):
        o_ref[...] = a_ref[...] + b_ref[...]

    spec = pl.BlockSpec((tr, cols), lambda i: (i, 0))
    return _pcall(body, name=name, grid=(rows // tr,), in_specs=[spec, spec], out_specs=spec,
                  out_shape=jax.ShapeDtypeStruct((rows, cols), F32))(a, b)


def _sum_chips_in_order(chip, own, landed, name):
    rows, cols = own.shape
    tr = _row_tile(rows, 4 * cols)

    def body(chip_ref, own_ref, land_ref, o_ref):
        me = chip_ref[0]
        acc = None
        for p in range(N_CHIPS):
            q = p ^ me
            k = jnp.where(q == 2, 0, jnp.where(q == 1, 1, 2))
            term = jnp.where(q == 0, own_ref[...], land_ref[k])
            acc = term if acc is None else acc + term
        o_ref[...] = acc

    grid_spec = pltpu.PrefetchScalarGridSpec(
        num_scalar_prefetch=1, grid=(rows // tr,),
        in_specs=[pl.BlockSpec((tr, cols), lambda i, chip_ref: (i, 0)),
                  pl.BlockSpec((3, tr, cols), lambda i, chip_ref: (0, i, 0))],
        out_specs=pl.BlockSpec((tr, cols), lambda i, chip_ref: (i, 0)))
    return _pcall(body, name=name, grid_spec=grid_spec, out_shape=jax.ShapeDtypeStruct((rows, cols), F32))(
        chip, own, landed)


def _bcast_plan(src, land, x, y, c):
    return [(src[0], land[0].at[k], (px, py, c)) for k, (px, py) in enumerate(_chips_of(x, y))]


def _scatter_plan(count):
    def plan(src, land, x, y, c):
        out = []
        for a in range(count):
            for k, (px, py) in enumerate(_chips_of(x, y)):
                out.append((src[a].at[2 * px + py], land[a].at[k], (px, py, c)))
        return out

    return plan


def _adamw_math(w, g, m, v):
    m2 = ADAM_B1 * m + (1.0 - ADAM_B1) * g
    v2 = ADAM_B2 * v + (1.0 - ADAM_B2) * (g * g)
    m_hat = m2 / (1.0 - ADAM_B1 ** ADAM_STEP)
    v_hat = v2 / (1.0 - ADAM_B2 ** ADAM_STEP)
    delta = -ADAM_LR * (m_hat / (jnp.sqrt(v_hat) + ADAM_EPS) + ADAM_WD * w)
    return delta, m2, v2


def _adamw(w, m, v, grads, name):
    rows, cols = w.shape
    tr = _row_tile(rows, cols)
    ng = len(grads)

    def body(*refs):
        w_ref, m_ref, v_ref = refs[:3]
        g = refs[3][...]
        for k in range(1, ng):
            g = g + refs[3 + k][...]
        g_ref, d_ref, m2_ref, v2_ref = refs[3 + ng:]
        delta, m2, v2 = _adamw_math(w_ref[...], g, m_ref[...], v_ref[...])
        g_ref[...] = g
        d_ref[...] = delta
        m2_ref[...] = m2
        v2_ref[...] = v2

    spec = pl.BlockSpec((tr, cols), lambda i: (i, 0))
    return _pcall(body, name=name, grid=(rows // tr,), in_specs=[spec] * (3 + ng), out_specs=[spec] * 4,
                  out_shape=[jax.ShapeDtypeStruct((rows, cols), F32)] * 4, vmem_mb=48)(w, m, v, *grads)


def _ada_adamw(ct, dmod, w, m, v):
    rows, cols = w.shape
    tr = _row_tile(rows, cols)

    def body(ct_ref, dm_ref, w_ref, m_ref, v_ref, g_ref, d_ref, m2_ref, v2_ref):
        cv = ct_ref[...]
        ca = cv * _sigmoid(cv)
        g = ca[:, 0:1] * dm_ref[0:1, :]
        for b in range(1, N_DEV):
            g = g + ca[:, b:b + 1] * dm_ref[b:b + 1, :]
        delta, m2, v2 = _adamw_math(w_ref[...], g, m_ref[...], v_ref[...])
        g_ref[...] = g
        d_ref[...] = delta
        m2_ref[...] = m2
        v2_ref[...] = v2

    spec = pl.BlockSpec((tr, cols), lambda i: (i, 0))
    return _pcall(body, name="ada_adamw", grid=(rows // tr,),
                  in_specs=[pl.BlockSpec((tr, N_DEV), lambda i: (i, 0)), pl.BlockSpec((N_DEV, cols), lambda i: (0, 0)),
                            spec, spec, spec],
                  out_specs=[spec] * 4, out_shape=[jax.ShapeDtypeStruct((rows, cols), F32)] * 4,
                  vmem_mb=48)(ct, dmod, w, m, v)


def _mod_fwd(c_all, w, b):
    cols = w.shape[1]
    tn = cols // 3

    def body(c_ref, w_ref, b_ref, o_ref):
        cv = c_ref[...]
        ca = (cv * _sigmoid(cv)).astype(BF)
        o_ref[...] = _dot(ca, w_ref[...].astype(BF)) + b_ref[...]

    return _pcall(body, name="mod_fwd", grid=(3,),
                  in_specs=[pl.BlockSpec((N_DEV, D), lambda j: (0, 0)), pl.BlockSpec((D, tn), lambda j: (0, j)),
                            pl.BlockSpec((1, tn), lambda j: (0, j))],
                  out_specs=pl.BlockSpec((N_DEV, tn), lambda j: (0, j)),
                  out_shape=jax.ShapeDtypeStruct((N_DEV, cols), F32))(c_all, w, b)


def _resident(shape):
    zeros = (0,) * len(shape)
    return pl.BlockSpec(shape, lambda *_: zeros, pipeline_mode=pl.Buffered(1))


def _modnorm_matmul(x, g, scale, shift, w4, name, tm=256):
    T = x.shape[0]
    tm = min(tm, T)
    ns = w4.shape[2]

    def body(x_ref, g_ref, sc_ref, sh_ref, w_ref, h_ref, z_ref):
        _, xh = _rms_stats(x_ref[...])
        h = ((xh * g_ref[...]) * (1.0 + sc_ref[...]) + sh_ref[...]).astype(BF)
        h_ref[...] = h
        for j in range(N_CHIPS):
            z_ref[:, j * ns:(j + 1) * ns] = _dot(h, w_ref[j])

    vec = pl.BlockSpec((1, D), lambda i: (0, 0))
    return _pcall(body, name=name, grid=(T // tm,),
                  in_specs=[pl.BlockSpec((tm, D), lambda i: (i, 0)), vec, vec, vec, _resident(w4.shape)],
                  out_specs=[pl.BlockSpec((tm, D), lambda i: (i, 0)), pl.BlockSpec((tm, N_CHIPS * ns), lambda i: (i, 0))],
                  out_shape=[jax.ShapeDtypeStruct((T, D), BF), jax.ShapeDtypeStruct((T, N_CHIPS * ns), F32)],
                  vmem_mb=48)(x, g, scale, shift, w4)


def _rglru_fwd(z, cw, cb, wa, ba, wx, bx, lam, tb=256):
    T = z.shape[0]
    tb = min(tb, T)

    def body(xr_ref, gr_ref, cw_ref, cb_ref, wa_ref, ba_ref, wx_ref, bx_ref, lam_ref, h_ref, ya_ref, prev, hc):
        i = pl.program_id(0)

        @pl.when(i == 0)
        def _():
            prev[...] = jnp.zeros_like(prev)
            hc[...] = jnp.zeros_like(hc)

        xr = xr_ref[...]
        pv = prev[...]
        xc = (cb_ref[...] + cw_ref[3:4, :] * xr + cw_ref[2:3, :] * _shift_down(xr, pv, 1)
              + cw_ref[1:2, :] * _shift_down(xr, pv, 2) + cw_ref[0:1, :] * _shift_down(xr, pv, 3))
        prev[...] = xr[tb - SUBLANES:tb]
        _, ig, _, a, mult = _lru_gates(xc, wa_ref, ba_ref[...], wx_ref, bx_ref[...], lam_ref[...])
        a, u = _scan_rows(a, mult * (ig * xc), reverse=False)
        h = u + a * hc[SUBLANES - 1:SUBLANES, :]
        hc[...] = h[tb - SUBLANES:tb]
        h_ref[...] = h
        ya_ref[...] = (h * _gelu(gr_ref[...])).astype(BF)

    vec = pl.BlockSpec((1, D), lambda i: (0, 0))
    wspec = pl.BlockSpec((HEADS, HD, HD), lambda i: (0, 0, 0))
    return _pcall(body, name="rglru_fwd", grid=(T // tb,),
                  in_specs=[pl.BlockSpec((tb, D), lambda i: (i, 0)), pl.BlockSpec((tb, D), lambda i: (i, 1)),
                            pl.BlockSpec((4, D), lambda i: (0, 0)), vec, wspec, vec, wspec, vec, vec],
                  out_specs=[pl.BlockSpec((tb, D), lambda i: (i, 0))] * 2,
                  out_shape=[jax.ShapeDtypeStruct((T, D), F32), jax.ShapeDtypeStruct((T, D), BF)],
                  scratch=[pltpu.VMEM((SUBLANES, D), F32), pltpu.VMEM((SUBLANES, D), F32)],
                  vmem_mb=48)(z, z, cw, cb, wa, ba, wx, bx, lam)


def _sgu_fwd(z, lg, lb, ws, bst, tb=256):
    T = z.shape[0]
    tb = min(tb, T)

    def body(zu_ref, zv_ref, lg_ref, lb_ref, ws_ref, bst_ref, yb_ref):
        _, xh = _layernorm_stats(_gelu(zv_ref[...]))
        vln = xh * lg_ref[...] + lb_ref[...]
        _, mixed = _sgu_mix(vln, ws_ref, bst_ref, tb)
        yb_ref[...] = (_gelu(zu_ref[...]) * mixed).astype(BF)

    vec = pl.BlockSpec((1, D), lambda i: (0, 0))
    return _pcall(body, name="sgu_fwd", grid=(T // tb,),
                  in_specs=[pl.BlockSpec((tb, D), lambda i: (i, 2)), pl.BlockSpec((tb, D), lambda i: (i, 3)), vec, vec,
                            pl.BlockSpec((HEADS, SGU_BLOCK, SGU_BLOCK), lambda i: (0, 0, 0)),
                            pl.BlockSpec((SGU_BLOCK, HEADS), lambda i: (0, 0))],
                  out_specs=pl.BlockSpec((tb, D), lambda i: (i, 0)),
                  out_shape=jax.ShapeDtypeStruct((T, D), BF))(z, z, lg, lb, ws, bst)


def _mix_out(ya_pre, yb_pre, z, x, gate1, wba, wbb, wo, tm=256):
    T = x.shape[0]
    tm = min(tm, T)

    def body(yap_ref, ybp_ref, ga_ref, gb_ref, x_ref, g1_ref, wa_ref, wb_ref, wo_ref,
             x2_ref, mg_ref, ya_ref, yb_ref, o_ref):
        ya = _dot(yap_ref[...], wa_ref[...])
        yb = _dot(ybp_ref[...], wb_ref[...])
        merged = (_sigmoid_t(ga_ref[...]) * ya + _sigmoid_t(gb_ref[...]) * yb).astype(BF)
        o = _dot(merged, wo_ref[...])
        x2_ref[...] = x_ref[...] + g1_ref[...] * o
        mg_ref[...] = merged
        ya_ref[...] = ya.astype(BF)
        yb_ref[...] = yb.astype(BF)
        o_ref[...] = o.astype(BF)

    row = pl.BlockSpec((tm, D), lambda i: (i, 0))
    wspec = pl.BlockSpec((D, D), lambda i: (0, 0))
    return _pcall(body, name="mix_out", grid=(T // tm,),
                  in_specs=[row, row, pl.BlockSpec((tm, D), lambda i: (i, 4)), pl.BlockSpec((tm, D), lambda i: (i, 5)),
                            row, pl.BlockSpec((1, D), lambda i: (0, 0)), wspec, wspec, wspec],
                  out_specs=[row] * 5,
                  out_shape=[jax.ShapeDtypeStruct((T, D), F32)] + [jax.ShapeDtypeStruct((T, D), BF)] * 4,
                  vmem_mb=48)(ya_pre, yb_pre, z, z, x, gate1, wba, wbb, wo)


def _ffn_gate(up, cw, cb, tm=512, cw_blk=768):
    T = up.shape[0]
    tm = min(tm, T)
    dff = up.shape[1] // 2
    ncb = dff // cw_blk

    def body(ua_ref, uv_ref, wa_ref, wv_ref, ba_ref, bv_ref, f_ref, ga_ref, vd_ref, pa, pv):
        i = pl.program_id(1)

        @pl.when(i == 0)
        def _():
            pa[...] = jnp.zeros_like(pa)
            pv[...] = jnp.zeros_like(pv)

        def conv(u_ref, w_ref, b_ref, prev):
            u = u_ref[...]
            p = prev[...]
            hid = (b_ref[...] + w_ref[2:3, :] * u + w_ref[1:2, :] * _shift_down(u, p, 1)
                   + w_ref[0:1, :] * _shift_down(u, p, 2))
            prev[...] = u[tm - SUBLANES:tm]
            return hid

        act = conv(ua_ref, wa_ref, ba_ref, pa)
        val = conv(uv_ref, wv_ref, bv_ref, pv)
        ga, dga = _gelu_and_grad(act)
        f_ref[...] = (ga * val).astype(BF)
        ga_ref[...] = ga.astype(BF)
        vd_ref[...] = (val * dga).astype(BF)

    blk = pl.BlockSpec((tm, cw_blk), lambda cbk, i: (i, cbk))
    return _pcall(body, name="ffn_gate", grid=(ncb, T // tm),
                  in_specs=[pl.BlockSpec((tm, cw_blk), lambda cbk, i: (i, cbk)),
                            pl.BlockSpec((tm, cw_blk), lambda cbk, i: (i, ncb + cbk)),
                            pl.BlockSpec((3, cw_blk), lambda cbk, i: (0, cbk)),
                            pl.BlockSpec((3, cw_blk), lambda cbk, i: (0, ncb + cbk)),
                            pl.BlockSpec((1, cw_blk), lambda cbk, i: (0, cbk)),
                            pl.BlockSpec((1, cw_blk), lambda cbk, i: (0, ncb + cbk))],
                  out_specs=[blk] * 3, out_shape=[jax.ShapeDtypeStruct((T, dff), BF)] * 3,
                  scratch=[pltpu.VMEM((SUBLANES, cw_blk), F32)] * 2)(up, up, cw, cw, cb, cb)


def _ffn_down_loss(f, wd, x2, gate2, gf, target, tm=512):
    T = x2.shape[0]
    tm = min(tm, T)
    dff = f.shape[1]

    def body(f_ref, wd_ref, x2_ref, g2_ref, gf_ref, t_ref, loss_ref, dx3_ref, dfo_ref, dgf_ref, dg2_ref):
        i = pl.program_id(0)

        @pl.when(i == 0)
        def _():
            loss_ref[...] = jnp.zeros_like(loss_ref)
            dgf_ref[...] = jnp.zeros_like(dgf_ref)
            dg2_ref[...] = jnp.zeros_like(dg2_ref)

        fo = _dot(f_ref[...], wd_ref[...])
        x3 = x2_ref[...] + g2_ref[...] * fo
        rstd, xh = _rms_stats(x3)
        err = xh * gf_ref[...] - t_ref[...]
        loss_ref[...] += 0.5 * jnp.sum(jnp.mean(err * err, axis=-1, keepdims=True), axis=0, keepdims=True)
        dy = err * (1.0 / D)
        dgf_ref[...] += _colsum(dy * xh)
        dxh = dy * gf_ref[...]
        dx3 = rstd * (dxh - xh * jnp.mean(dxh * xh, axis=-1, keepdims=True))
        dg2_ref[...] += _colsum(dx3 * fo)
        dx3_ref[...] = dx3
        dfo_ref[...] = (g2_ref[...] * dx3).astype(BF)

    row = pl.BlockSpec((tm, D), lambda i: (i, 0))
    vec = pl.BlockSpec((1, D), lambda i: (0, 0))
    return _pcall(body, name="ffn_down_loss", grid=(T // tm,),
                  in_specs=[pl.BlockSpec((tm, dff), lambda i: (i, 0)), pl.BlockSpec((dff, D), lambda i: (0, 0)),
                            row, vec, vec, row],
                  out_specs=[pl.BlockSpec((1, LANES), lambda i: (0, 0)), row, row, vec, vec],
                  out_shape=[jax.ShapeDtypeStruct((1, LANES), F32), jax.ShapeDtypeStruct((T, D), F32),
                             jax.ShapeDtypeStruct((T, D), BF), jax.ShapeDtypeStruct((1, D), F32),
                             jax.ShapeDtypeStruct((1, D), F32)],
                  vmem_mb=48)(f, wd, x2, gate2, gf, target)


def _ffn_fwd(x2, g, scale, shift, gate2, gf, w_up4, wd, cw, cb, target, tm=256, chunk=768):
    T = x2.shape[0]
    tm = min(tm, T)
    ns = w_up4.shape[2]
    dff = wd.shape[0]
    nchunk = dff // chunk
    per = ns // chunk

    def body(x2_ref, g_ref, sc_ref, sh_ref, g2_ref, gf_ref, wu_ref, wd_ref, cw_ref, cb_ref, t_ref,
             h2_ref, up_ref, f_ref, ga_ref, vd_ref, loss_ref, dx3_ref, dfo_ref, dgf_ref, dg2_ref, prev):
        i = pl.program_id(0)

        @pl.when(i == 0)
        def _():
            prev[...] = jnp.zeros_like(prev)
            loss_ref[...] = jnp.zeros_like(loss_ref)
            dgf_ref[...] = jnp.zeros_like(dgf_ref)
            dg2_ref[...] = jnp.zeros_like(dg2_ref)

        x2v = x2_ref[...]
        _, xh2 = _rms_stats(x2v)
        h2 = ((xh2 * g_ref[...]) * (1.0 + sc_ref[...]) + sh_ref[...]).astype(BF)
        h2_ref[...] = h2

        def conv(u, col):
            cs = slice(col, col + chunk)
            p = prev[:, cs]
            hid = (cb_ref[:, cs] + cw_ref[2:3, cs] * u + cw_ref[1:2, cs] * _shift_down(u, p, 1)
                   + cw_ref[0:1, cs] * _shift_down(u, p, 2))
            prev[:, cs] = u[tm - SUBLANES:tm]
            up_ref[:, cs] = u.astype(BF)
            return hid

        fo = None
        for k in range(nchunk):
            col = k * chunk
            off = (k % per) * chunk
            ua = _dot(h2, wu_ref[k // per, :, off:off + chunk])
            uv = _dot(h2, wu_ref[N_CHIPS // 2 + k // per, :, off:off + chunk])
            act = conv(ua, col)
            val = conv(uv, dff + col)
            ga, dga = _gelu_and_grad(act)
            fk = (ga * val).astype(BF)
            f_ref[:, col:col + chunk] = fk
            ga_ref[:, col:col + chunk] = ga.astype(BF)
            vd_ref[:, col:col + chunk] = (val * dga).astype(BF)
            part = _dot(fk, wd_ref[col:col + chunk, :])
            fo = part if fo is None else fo + part

        x3 = x2v + g2_ref[...] * fo
        rstd, xh = _rms_stats(x3)
        err = xh * gf_ref[...] - t_ref[...]
        loss_ref[...] += 0.5 * jnp.sum(jnp.mean(err * err, axis=-1, keepdims=True), axis=0, keepdims=True)
        dy = err * (1.0 / D)
        dgf_ref[...] += _colsum(dy * xh)
        dxh = dy * gf_ref[...]
        dx3 = rstd * (dxh - xh * jnp.mean(dxh * xh, axis=-1, keepdims=True))
        dg2_ref[...] += _colsum(dx3 * fo)
        dx3_ref[...] = dx3
        dfo_ref[...] = (g2_ref[...] * dx3).astype(BF)

    row = pl.BlockSpec((tm, D), lambda i: (i, 0))
    vec = pl.BlockSpec((1, D), lambda i: (0, 0))
    wide = pl.BlockSpec((tm, 2 * dff), lambda i: (i, 0))
    half = pl.BlockSpec((tm, dff), lambda i: (i, 0))
    return _pcall(body, name="ffn_fwd", grid=(T // tm,),
                  in_specs=[row, vec, vec, vec, vec, vec, _resident(w_up4.shape), _resident(wd.shape),
                            _resident(cw.shape), _resident(cb.shape), row],
                  out_specs=[row, wide, half, half, half, pl.BlockSpec((1, LANES), lambda i: (0, 0)), row, row, vec, vec],
                  out_shape=[jax.ShapeDtypeStruct((T, D), BF), jax.ShapeDtypeStruct((T, 2 * dff), BF),
                             jax.ShapeDtypeStruct((T, dff), BF), jax.ShapeDtypeStruct((T, dff), BF),
                             jax.ShapeDtypeStruct((T, dff), BF), jax.ShapeDtypeStruct((1, LANES), F32),
                             jax.ShapeDtypeStruct((T, D), F32), jax.ShapeDtypeStruct((T, D), BF),
                             jax.ShapeDtypeStruct((1, D), F32), jax.ShapeDtypeStruct((1, D), F32)],
                  scratch=[pltpu.VMEM((SUBLANES, 2 * dff), F32)], vmem_mb=56)(
        x2, g, scale, shift, gate2, gf, w_up4, wd, cw, cb, target)


def _ffn_bwd(dfo, wd, up, ga, vd, cw, tm=256, cw_blk=1536):
    T = up.shape[0]
    tm = min(tm, T)
    dff = up.shape[1] // 2
    ncb = dff // cw_blk
    nrow = T // tm

    def body(dfo_ref, wd_ref, ua_ref, uv_ref, ga_ref, vd_ref, wa_ref, wv_ref,
             du_ref, dwa_ref, dwv_ref, dba_ref, dbv_ref, na, nv):
        i = pl.program_id(1)

        @pl.when(i == 0)
        def _():
            na[...] = jnp.zeros_like(na)
            nv[...] = jnp.zeros_like(nv)
            dwa_ref[...] = jnp.zeros_like(dwa_ref)
            dwv_ref[...] = jnp.zeros_like(dwv_ref)
            dba_ref[...] = jnp.zeros_like(dba_ref)
            dbv_ref[...] = jnp.zeros_like(dbv_ref)

        df = _dot_nt(dfo_ref[...], wd_ref[...])

        def conv_bwd(dh, u_ref, w_ref, nxt, col, dw_ref, db_ref):
            n8 = nxt[...]
            dh1 = _shift_up(dh, n8, 1)
            dh2 = _shift_up(dh, n8, 2)
            nxt[...] = dh[0:SUBLANES]
            du_ref[:, col:col + cw_blk] = (w_ref[2:3, :] * dh + w_ref[1:2, :] * dh1 + w_ref[0:1, :] * dh2).astype(BF)
            u = u_ref[...].astype(F32)
            dw_ref[2:3, :] += _colsum(dh * u)
            dw_ref[1:2, :] += _colsum(dh1 * u)
            dw_ref[0:1, :] += _colsum(dh2 * u)
            db_ref[...] += _colsum(dh)

        conv_bwd(df * vd_ref[...].astype(F32), ua_ref, wa_ref, na, 0, dwa_ref, dba_ref)
        conv_bwd(df * ga_ref[...].astype(F32), uv_ref, wv_ref, nv, cw_blk, dwv_ref, dbv_ref)

    rev = lambda cbk, i: (nrow - 1 - i, cbk)
    rev_v = lambda cbk, i: (nrow - 1 - i, ncb + cbk)
    blk = pl.BlockSpec((tm, cw_blk), rev)
    w3a = pl.BlockSpec((3, cw_blk), lambda cbk, i: (0, cbk))
    w3v = pl.BlockSpec((3, cw_blk), lambda cbk, i: (0, ncb + cbk))
    b1a = pl.BlockSpec((1, cw_blk), lambda cbk, i: (0, cbk))
    return _pcall(body, name="ffn_bwd", grid=(ncb, nrow),
                  in_specs=[pl.BlockSpec((tm, D), lambda cbk, i: (nrow - 1 - i, 0)),
                            pl.BlockSpec((cw_blk, D), lambda cbk, i: (cbk, 0)),
                            blk, pl.BlockSpec((tm, cw_blk), rev_v), blk, blk, w3a, w3v],
                  out_specs=[pl.BlockSpec((tm, 2 * cw_blk), rev), w3a, w3a, b1a, b1a],
                  out_shape=[jax.ShapeDtypeStruct((T, 2 * dff), BF),
                             jax.ShapeDtypeStruct((3, dff), F32), jax.ShapeDtypeStruct((3, dff), F32),
                             jax.ShapeDtypeStruct((1, dff), F32), jax.ShapeDtypeStruct((1, dff), F32)],
                  scratch=[pltpu.VMEM((SUBLANES, cw_blk), F32)] * 2,
                  vmem_mb=48)(dfo, wd, up, up, ga, vd, cw, cw)


def _ffn_col_block(t, ncb):
    return jnp.where(t < ncb, 2 * t, 2 * (t - ncb) + 1)


def _mm_tn_cols(a, b, name, nshard, nb, colmap=None, mb=None, tm=1024):
    T, M = a.shape
    tm = min(tm, T)
    mb = M if mb is None else mb
    ns = b.shape[1] // nshard
    per = ns // nb
    cmap = colmap if colmap is not None else (lambda t: t)

    def body(a_ref, b_ref, o_ref):
        k = pl.program_id(2)

        @pl.when(k == 0)
        def _():
            o_ref[...] = jnp.zeros_like(o_ref)

        o_ref[0] += _dot_tn(a_ref[...], b_ref[...])

    return _pcall(body, name=name, grid=(M // mb, nshard * per, T // tm),
                  in_specs=[pl.BlockSpec((tm, mb), lambda m, t, k: (k, m)),
                            pl.BlockSpec((tm, nb), lambda m, t, k: (k, cmap(t)))],
                  out_specs=pl.BlockSpec((1, mb, nb), lambda m, t, k: (t // per, m, t % per)),
                  out_shape=jax.ShapeDtypeStruct((nshard, M, ns), F32), vmem_mb=48)(a, b)


def _mm_nt_normbwd(dz, w4, x, resid, g, scale, name, gate=None, o=None, dz_blocks=(0, 1, 2, 3), tm=256):
    T = x.shape[0]
    tm = min(tm, T)
    ns = w4.shape[2]
    gated = gate is not None

    def body(*refs):
        if gated:
            (dz_ref, w_ref, x_ref, r_ref, g_ref, sc_ref, gt_ref, o_ref,
             dx_ref, dsh_ref, dsc_ref, dg_ref, do_ref, dgt_ref) = refs
        else:
            dz_ref, w_ref, x_ref, r_ref, g_ref, sc_ref, dx_ref, dsh_ref, dsc_ref, dg_ref = refs
        i = pl.program_id(0)

        @pl.when(i == 0)
        def _():
            dsh_ref[...] = jnp.zeros_like(dsh_ref)
            dsc_ref[...] = jnp.zeros_like(dsc_ref)
            dg_ref[...] = jnp.zeros_like(dg_ref)
            if gated:
                dgt_ref[...] = jnp.zeros_like(dgt_ref)

        dh = None
        for j in range(N_CHIPS):
            blk = dz_blocks[j]
            part = _dot_nt(dz_ref[:, blk * ns:(blk + 1) * ns], w_ref[j])
            dh = part if dh is None else dh + part
        rstd, xh = _rms_stats(x_ref[...])
        dsh_ref[...] += _colsum(dh)
        dsc_ref[...] += _colsum(dh * (xh * g_ref[...]))
        dn = dh * (1.0 + sc_ref[...])
        dg_ref[...] += _colsum(dn * xh)
        dxh = dn * g_ref[...]
        dx = r_ref[...] + rstd * (dxh - xh * jnp.mean(dxh * xh, axis=-1, keepdims=True))
        dx_ref[...] = dx
        if gated:
            do_ref[...] = (gt_ref[...] * dx).astype(BF)
            dgt_ref[...] += _colsum(dx * o_ref[...].astype(F32))

    row = pl.BlockSpec((tm, D), lambda i: (i, 0))
    vec = pl.BlockSpec((1, D), lambda i: (0, 0))
    in_specs = [pl.BlockSpec((tm, N_CHIPS * ns), lambda i: (i, 0)), _resident(w4.shape), row, row, vec, vec]
    out_specs = [row, vec, vec, vec]
    out_shape = [jax.ShapeDtypeStruct((T, D), F32)] + [jax.ShapeDtypeStruct((1, D), F32)] * 3
    args = [dz, w4, x, resid, g, scale]
    if gated:
        in_specs += [vec, row]
        out_specs += [row, vec]
        out_shape += [jax.ShapeDtypeStruct((T, D), BF), jax.ShapeDtypeStruct((1, D), F32)]
        args += [gate, o]
    return _pcall(body, name=name, grid=(T // tm,), in_specs=in_specs, out_specs=out_specs, out_shape=out_shape,
                  vmem_mb=48)(*args)


def _mix_bwd(do, ya, yb, z, wo, wba, wbb, tm=256):
    T = do.shape[0]
    tm = min(tm, T)

    def body(do_ref, ya_ref, yb_ref, ga_ref, gb_ref, wo_ref, wa_ref, wb_ref,
             dz_ref, dya_ref, dyb_ref, dyap_ref, dybp_ref):
        dm = _dot_nt(do_ref[...], wo_ref[...])
        sa = _sigmoid_t(ga_ref[...])
        sb = _sigmoid_t(gb_ref[...])
        dya = (sa * dm).astype(BF)
        dyb = (sb * dm).astype(BF)
        dz_ref[:, 0:D] = (dm * ya_ref[...].astype(F32) * sa * (1.0 - sa)).astype(BF)
        dz_ref[:, D:2 * D] = (dm * yb_ref[...].astype(F32) * sb * (1.0 - sb)).astype(BF)
        dya_ref[...] = dya
        dyb_ref[...] = dyb
        dyap_ref[...] = _dot_nt(dya, wa_ref[...]).astype(BF)
        dybp_ref[...] = _dot_nt(dyb, wb_ref[...]).astype(BF)

    row = pl.BlockSpec((tm, D), lambda i: (i, 0))
    wspec = pl.BlockSpec((D, D), lambda i: (0, 0))
    return _pcall(body, name="mix_bwd", grid=(T // tm,),
                  in_specs=[row, row, row, pl.BlockSpec((tm, D), lambda i: (i, 4)),
                            pl.BlockSpec((tm, D), lambda i: (i, 5)), wspec, wspec, wspec],
                  out_specs=[pl.BlockSpec((tm, 2 * D), lambda i: (i, 2)), row, row, row, row],
                  out_shape=[jax.ShapeDtypeStruct((T, 6 * D), BF)] + [jax.ShapeDtypeStruct((T, D), BF)] * 4,
                  vmem_mb=48)(do, ya, yb, z, z, wo, wba, wbb)


def _sgu_bwd(dz, dyb_pre, z, lg, lb, ws, bst, tb=256):
    T = z.shape[0]
    tb = min(tb, T)

    def body(dz_in, dy_ref, zu_ref, zv_ref, lg_ref, lb_ref, ws_ref, bst_ref,
             dz_ref, dws_ref, dbst_ref, dlg_ref, dlb_ref):
        del dz_in
        i = pl.program_id(0)

        @pl.when(i == 0)
        def _():
            dws_ref[...] = jnp.zeros_like(dws_ref)
            dbst_ref[...] = jnp.zeros_like(dbst_ref)
            dlg_ref[...] = jnp.zeros_like(dlg_ref)
            dlb_ref[...] = jnp.zeros_like(dlb_ref)

        gu, dgu = _gelu_and_grad(zu_ref[...])
        gv, dgv = _gelu_and_grad(zv_ref[...])
        rstd, xh = _layernorm_stats(gv)
        vln = xh * lg_ref[...] + lb_ref[...]
        wm, mixed = _sgu_mix(vln, ws_ref, bst_ref, tb)
        dy = dy_ref[...].astype(F32)
        dz_ref[:, 0:D] = (dy * mixed * dgu).astype(BF)
        dmixed = dy * gu
        ri = lax.broadcasted_iota(jnp.int32, (SGU_BLOCK, SGU_BLOCK), 0)
        ci = lax.broadcasted_iota(jnp.int32, (SGU_BLOCK, SGU_BLOCK), 1)
        blocks = []
        for blk in range(tb // SGU_BLOCK):
            rs = slice(blk * SGU_BLOCK, (blk + 1) * SGU_BLOCK)
            cols = []
            for g in range(HEADS):
                cs = slice(g * HD, (g + 1) * HD)
                dmg = dmixed[rs, cs]
                dmb = dmg.astype(BF)
                dbst_ref[:, g:g + 1] += jnp.sum(dmg, axis=1, keepdims=True)
                dws_ref[g] += jnp.where(ri >= ci, _dot_nt(dmb, vln[rs, cs].astype(BF)), 0.0)
                cols.append(_dot_tn(wm[g], dmb))
            blocks.append(jnp.concatenate(cols, axis=1))
        dvln = blocks[0] if len(blocks) == 1 else jnp.concatenate(blocks, axis=0)
        dlg_ref[...] += _colsum(dvln * xh)
        dlb_ref[...] += _colsum(dvln)
        dxh = dvln * lg_ref[...]
        dgv_in = rstd * (dxh - jnp.mean(dxh, axis=-1, keepdims=True)
                         - xh * jnp.mean(dxh * xh, axis=-1, keepdims=True))
        dz_ref[:, D:2 * D] = (dgv_in * dgv).astype(BF)

    row = pl.BlockSpec((tb, D), lambda i: (i, 0))
    vec = pl.BlockSpec((1, D), lambda i: (0, 0))
    wspec = pl.BlockSpec((HEADS, SGU_BLOCK, SGU_BLOCK), lambda i: (0, 0, 0))
    bspec = pl.BlockSpec((SGU_BLOCK, HEADS), lambda i: (0, 0))
    return _pcall(body, name="sgu_bwd", grid=(T // tb,),
                  in_specs=[HBM_SPEC, row, pl.BlockSpec((tb, D), lambda i: (i, 2)),
                            pl.BlockSpec((tb, D), lambda i: (i, 3)), vec, vec, wspec, bspec],
                  out_specs=[pl.BlockSpec((tb, 2 * D), lambda i: (i, 1)), wspec, bspec, vec, vec],
                  out_shape=[jax.ShapeDtypeStruct(dz.shape, BF),
                             jax.ShapeDtypeStruct((HEADS, SGU_BLOCK, SGU_BLOCK), F32),
                             jax.ShapeDtypeStruct((SGU_BLOCK, HEADS), F32),
                             jax.ShapeDtypeStruct((1, D), F32), jax.ShapeDtypeStruct((1, D), F32)],
                  aliases={0: 0}, vmem_mb=48)(dz, dyb_pre, z, z, lg, lb, ws, bst)


def _rglru_bwd(dz, dya_pre, z, h, cw, cb, wa, ba, wx, bx, lam, tb=256):
    T = z.shape[0]
    tb = min(tb, T)
    nrow = T // tb
    per = tb // SUBLANES

    def body(dz_in, dy_ref, xr_ref, xh_ref, gr_ref, h_ref, hh_ref, cw_ref, cb_ref, wa_ref, ba_ref, wx_ref, bx_ref,
             lam_ref, dz_ref, dcw_ref, dcb_ref, dwa_ref, dba_ref, dwx_ref, dbx_ref, dlam_ref, carry, nxt):
        del dz_in
        i = pl.program_id(0)
        first_block = i == nrow - 1

        @pl.when(i == 0)
        def _():
            carry[...] = jnp.zeros_like(carry)
            nxt[...] = jnp.zeros_like(nxt)
            for ref in (dcw_ref, dcb_ref, dwa_ref, dba_ref, dwx_ref, dbx_ref, dlam_ref):
                ref[...] = jnp.zeros_like(ref)

        xr = xr_ref[...]
        pv = jnp.where(first_block, 0.0, xh_ref[...])
        s1 = _shift_down(xr, pv, 1)
        s2 = _shift_down(xr, pv, 2)
        s3 = _shift_down(xr, pv, 3)
        xc = cb_ref[...] + cw_ref[3:4, :] * xr + cw_ref[2:3, :] * s1 + cw_ref[1:2, :] * s2 + cw_ref[0:1, :] * s3
        lam = lam_ref[...]
        r, ig, ls, a, mult = _lru_gates(xc, wa_ref, ba_ref[...], wx_ref, bx_ref[...], lam)
        hv = h_ref[...]
        hprev = _shift_down(hv, jnp.where(first_block, 0.0, hh_ref[...]), 1)
        gg, dgg = _gelu_and_grad(gr_ref[...])
        dy = dy_ref[...].astype(F32)
        dz_ref[:, D:2 * D] = (dy * hv * dgg).astype(BF)

        rows = lax.broadcasted_iota(jnp.int32, (tb, D), 0)
        v = dy * gg + jnp.where(rows == tb - 1, carry[0:1, :], 0.0)
        q = jnp.where(rows < tb - 1, pltpu.roll(a, tb - 1, 0), 0.0)
        _, gsc = _scan_rows(q, v, reverse=True)
        carry[...] = (a * gsc)[0:SUBLANES]

        xi = ig * xc
        dmult = gsc * xi
        dxi = gsc * mult
        dig = dxi * xc
        dxc = dxi * ig
        dlog_a = gsc * hprev * a - dmult * (a * a) / mult
        dlam_ref[...] += _colsum(dlog_a * r) * (LRU_C * _sigmoid(-lam))
        dpr = dlog_a * (LRU_C * ls) * r * (1.0 - r)
        dpi = dig * ig * (1.0 - ig)
        dba_ref[...] += _colsum(dpr)
        dbx_ref[...] += _colsum(dpi)
        back = []
        for hh in range(HEADS):
            cs = slice(hh * HD, (hh + 1) * HD)
            xh = xc[:, cs].astype(BF)
            dprh = dpr[:, cs].astype(BF)
            dpih = dpi[:, cs].astype(BF)
            dwa_ref[hh] += _dot_tn(xh, dprh)
            dwx_ref[hh] += _dot_tn(xh, dpih)
            back.append(_dot_nt(dprh, wa_ref[hh].astype(BF)) + _dot_nt(dpih, wx_ref[hh].astype(BF)))
        dxc = dxc + jnp.concatenate(back, axis=1)

        n8 = nxt[...]
        dxr = (cw_ref[3:4, :] * dxc + cw_ref[2:3, :] * _shift_up(dxc, n8, 1)
               + cw_ref[1:2, :] * _shift_up(dxc, n8, 2) + cw_ref[0:1, :] * _shift_up(dxc, n8, 3))
        nxt[...] = dxc[0:SUBLANES]
        dz_ref[:, 0:D] = dxr.astype(BF)
        dcw_ref[3:4, :] += _colsum(dxc * xr)
        dcw_ref[2:3, :] += _colsum(dxc * s1)
        dcw_ref[1:2, :] += _colsum(dxc * s2)
        dcw_ref[0:1, :] += _colsum(dxc * s3)
        dcb_ref[...] += _colsum(dxc)

    rev = lambda col: (lambda i: (nrow - 1 - i, col))
    halo = lambda col: pl.BlockSpec((SUBLANES, D), lambda i: (jnp.maximum((nrow - 1 - i) * per - 1, 0), col))
    vec = pl.BlockSpec((1, D), lambda i: (0, 0))
    wspec = pl.BlockSpec((HEADS, HD, HD), lambda i: (0, 0, 0))
    c4 = pl.BlockSpec((4, D), lambda i: (0, 0))
    wshape = jax.ShapeDtypeStruct((HEADS, HD, HD), F32)
    vshape = jax.ShapeDtypeStruct((1, D), F32)
    return _pcall(body, name="rglru_bwd", grid=(nrow,),
                  in_specs=[HBM_SPEC, pl.BlockSpec((tb, D), rev(0)), pl.BlockSpec((tb, D), rev(0)), halo(0),
                            pl.BlockSpec((tb, D), rev(1)), pl.BlockSpec((tb, D), rev(0)), halo(0),
                            c4, vec, wspec, vec, wspec, vec, vec],
                  out_specs=[pl.BlockSpec((tb, 2 * D), rev(0)), c4, vec, wspec, vec, wspec, vec, vec],
                  out_shape=[jax.ShapeDtypeStruct(dz.shape, BF), jax.ShapeDtypeStruct((4, D), F32), vshape,
                             wshape, vshape, wshape, vshape, vshape],
                  scratch=[pltpu.VMEM((SUBLANES, D), F32), pltpu.VMEM((SUBLANES, D), F32)],
                  aliases={0: 0}, vmem_mb=56)(dz, dya_pre, z, z, z, h, h, cw, cb, wa, ba, wx, bx, lam)


def _pack_rows(parts):
    out = []
    for p in parts:
        q = p.reshape(-1, LANES)
        pad = (-q.shape[0]) % SUBLANES
        if pad:
            q = jnp.concatenate([q, jnp.zeros((pad, LANES), q.dtype)], axis=0)
        out.append(q)
    return jnp.concatenate(out, axis=0)


def _rows_of(shape):
    n = 1
    for s in shape:
        n *= s
    rows = n // LANES
    return rows + (-rows) % SUBLANES


def kernel(x, c, w_ada, b_ada, norm_mix_g, w_in, rnn_conv_w, rnn_conv_b, lru_w_a, lru_b_a, lru_w_x, lru_b_x, lru_lambda, sgu_ln_g, sgu_ln_b, sgu_w_s, sgu_b_s, w_branch_a, w_branch_b, w_out, norm_ffn_g, w_up, ffn_conv_w, ffn_conv_b, w_down, norm_final_g, loss_target, m_w_ada, m_b_ada, m_norm_mix_g, m_w_in, m_rnn_conv_w, m_rnn_conv_b, m_lru_w_a, m_lru_b_a, m_lru_w_x, m_lru_b_x, m_lru_lambda, m_sgu_ln_g, m_sgu_ln_b, m_sgu_w_s, m_sgu_b_s, m_w_branch_a, m_w_branch_b, m_w_out, m_norm_ffn_g, m_w_up, m_ffn_conv_w, m_ffn_conv_b, m_w_down, m_norm_final_g, v_w_ada, v_b_ada, v_norm_mix_g, v_w_in, v_rnn_conv_w, v_rnn_conv_b, v_lru_w_a, v_lru_b_a, v_lru_w_x, v_lru_b_x, v_lru_lambda, v_sgu_ln_g, v_sgu_ln_b, v_sgu_w_s, v_sgu_b_s, v_w_branch_a, v_w_branch_b, v_w_out, v_norm_ffn_g, v_w_up, v_ffn_conv_w, v_ffn_conv_b, v_w_down, v_norm_final_g):
    args = dict(locals())
    T = x.shape[1]
    mx, my, mc = lax.axis_index("x"), lax.axis_index("y"), lax.axis_index("c")
    chip = 2 * mx + my
    dev = 2 * chip + mc
    vec = lambda a: a.reshape(1, -1)

    xt = x.reshape(T, D)
    tgt = loss_target.reshape(T, D)
    ns = w_in.shape[2]
    dff = w_down.shape[1] * N_CHIPS

    c_all = _gather8(c.reshape(SUBLANES, LANES), "gather_c").reshape(N_DEV, D)
    b_ada_sh = lax.dynamic_slice(b_ada, (0, chip * ns), (1, ns))
    mod_sh = _mod_fwd(c_all, w_ada[0], b_ada_sh)

    (w_in_b,) = _cast_shards([w_in[0]], "cast_w_in")
    w_in4, rcw4, fcw4, mod4 = _gather_weights([w_in_b, rnn_conv_w[0], ffn_conv_w[0], mod_sh],
                                              [True, False, False, False])
    late = _cast_shards([w_up[0], w_down[0], w_branch_a[0], w_branch_b[0], w_out[0]], "cast_late", after=mod4)
    late_plan = _gather_half_plan([w.shape for w in late])
    late_handle, late_token = _remote_start(
        late, [lax.empty((N_CHIPS,) + w.shape, w.dtype) for w in late], late_plan, 3 * len(late), "gather_late_start")
    rcw_full = jnp.transpose(rcw4, (1, 0, 2)).reshape(4, D)
    fcw_full = jnp.transpose(fcw4, (1, 0, 2)).reshape(3, 2 * dff)
    mod = lax.dynamic_index_in_dim(mod4, dev, axis=1, keepdims=False).reshape(1, 6 * D)
    shift1, scale1, gate1, shift2, scale2, gate2 = [mod[:, k * D:(k + 1) * D] for k in range(6)]

    h1, z = _modnorm_matmul(xt, norm_mix_g, scale1 + late_token[0:1, 0:1], shift1, w_in4, "norm_in_proj")
    bst = jnp.transpose(sgu_b_s[0])
    h_lru, ya_pre = _rglru_fwd(z, rcw_full, rnn_conv_b, lru_w_a[0], lru_b_a, lru_w_x[0], lru_b_x, lru_lambda)
    yb_pre = _sgu_fwd(z, sgu_ln_g, sgu_ln_b, sgu_w_s[0], bst)
    late, late_lands = _remote_wait(late_handle, late_plan, yb_pre, "gather_late_wait")
    w_up4, w_down4, wba4, wbb4, wo4 = _forward_halves(late, late_lands)
    wd_full = w_down4.reshape(dff, D)
    wba_full = wba4.reshape(D, D)
    wbb_full = wbb4.reshape(D, D)
    wo_full = wo4.reshape(D, D)
    x2, merged, ya, yb, o1 = _mix_out(ya_pre, yb_pre, z, xt, gate1, wba_full, wbb_full, wo_full)
    h2, up, f, ffn_ga, ffn_vd, loss_part, dx3, dfo, dgf, dgate2 = _ffn_fwd(
        x2, norm_ffn_g, scale2, shift2, gate2, vec(norm_final_g), w_up4, wd_full, fcw_full, ffn_conv_b, tgt)

    ffn_map = functools.partial(_ffn_col_block, ncb=2)
    dup, dfcw_a, dfcw_v, dfcb_a, dfcb_v = _ffn_bwd(dfo, wd_full, up, ffn_ga, ffn_vd, fcw_full, cw_blk=ns)
    dwd = _mm_tn_cols(f, dfo, "dw_down", 1, D, mb=D)
    dw_up4 = _mm_tn_cols(h2, dup, "dw_up", N_CHIPS, ns, colmap=ffn_map)
    dx2, dshift2, dscale2, dg_ffn, do1, dgate1 = _mm_nt_normbwd(
        dup, w_up4, x2, dx3, norm_ffn_g, scale2, "dh2_norm_bwd", gate=gate1, o=o1, dz_blocks=(0, 2, 1, 3))
    dz, dya, dyb, dya_pre, dyb_pre = _mix_bwd(do1, ya, yb, z, wo_full, wba_full, wbb_full)
    dwo = _mm_tn_cols(merged, do1, "dw_out", 1, D)
    dwba = _mm_tn_cols(ya_pre, dya, "dw_branch_a", 1, D)
    dwbb = _mm_tn_cols(yb_pre, dyb, "dw_branch_b", 1, D)

    core = mc.astype(jnp.int32).reshape(1)
    chip_id = chip.astype(jnp.int32).reshape(1)

    def reduce_start(group, name):
        from_core = _send_other_half([g for _, g in group], "swap_halves_" + name)
        sums = [_add_halves_bf16(core, g, o, "sum_cores_" + n) for (n, g), o in zip(group, from_core)]
        lands = [lax.empty((3,) + s.shape[1:], s.dtype) for s in sums]
        return _remote_start(sums, lands, _scatter_plan(len(group)), 3 * len(group), "scatter_start_" + name)

    def reduce_finish(group, handle, after, name):
        sums, landed = _remote_wait(handle, _scatter_plan(len(group)), after, "scatter_wait_" + name)
        return [_sum_own_and_landed(chip_id, s, l, "sum_chips_" + n) for (n, _), s, l in zip(group, sums, landed)]

    group1 = [("w_up", dw_up4), ("w_down", dwd.reshape(N_CHIPS, dff // N_CHIPS, D)),
              ("w_branch_a", dwba.reshape(N_CHIPS, D // N_CHIPS, D)),
              ("w_branch_b", dwbb.reshape(N_CHIPS, D // N_CHIPS, D)), ("w_out", dwo.reshape(N_CHIPS, D // N_CHIPS, D))]
    handle1, token1 = reduce_start(group1, "late")
    dz, dws, dbst, dlg, dlb = _sgu_bwd(dz, dyb_pre, z, sgu_ln_g + token1[0:1, 0:1], sgu_ln_b, sgu_w_s[0], bst)
    dz, drcw, drcb, dwa, dba, dwx, dbx, dlam = _rglru_bwd(
        dz, dya_pre, z, h_lru, rcw_full, rnn_conv_b, lru_w_a[0], lru_b_a, lru_w_x[0], lru_b_x, lru_lambda)
    early_small = [("rnn_conv_b", drcb), ("lru_w_a", dwa), ("lru_b_a", dba), ("lru_w_x", dwx), ("lru_b_x", dbx),
                   ("lru_lambda", dlam), ("sgu_ln_g", dlg), ("sgu_ln_b", dlb), ("sgu_w_s", dws),
                   ("sgu_b_s", jnp.transpose(dbst)), ("norm_ffn_g", dg_ffn),
                   ("ffn_conv_b", jnp.concatenate([dfcb_a, dfcb_v], axis=1)), ("norm_final_g", dgf)]
    r_early = sum(_rows_of(args[n].shape) for n, _ in early_small)
    early_pack = _pack_rows([g for _, g in early_small] + [drcw, jnp.concatenate([dfcw_a, dfcw_v], axis=1)])
    early_pack = jnp.concatenate(
        [early_pack, jnp.zeros(((-early_pack.shape[0]) % 256, LANES), F32)], axis=0)
    early_chip = _add_pair(early_pack, _swap_core(early_pack, "swap_small_grads"), "sum_cores_small_grads")
    early_handle, token3 = _remote_start([early_chip], [lax.empty((3,) + early_chip.shape, F32)], _bcast_plan, 3,
                                         "small_grads_start")
    totals1 = reduce_finish(group1, handle1, drcb, "late")
    group2 = [("w_in", _mm_tn_cols(h1, dz, "dw_in", N_CHIPS, ns))]
    handle2, token2 = reduce_start(group2, "in")
    grad_x, dshift1, dscale1, dg_mix = _mm_nt_normbwd(
        dz, w_in4, xt, dx2, norm_mix_g + (token2[0:1, 0:1] + token3[0:1, 0:1]), scale1, "dh1_norm_bwd")
    totals2 = reduce_finish(group2, handle2, dg_mix, "in")
    dmod = jnp.concatenate([dshift1, dscale1, dgate1, dshift2, dscale2, dgate2], axis=1)

    big = group1 + group2
    fulls = _share_halves(totals1 + totals2)
    out = {}
    for (n, _), full in zip(big, fulls):
        shape = args[n].shape
        res = _adamw(args[n][0], args["m_" + n][0], args["v_" + n][0], [full.reshape(shape[1:])], "adamw_" + n)
        for kind, r in zip(("grad_", "delta_", "new_m_", "new_v_"), res):
            out[kind + n] = r.reshape(shape)

    late_small = [("b_ada", dmod), ("norm_mix_g", dg_mix)]
    small = late_small + early_small
    late_all = _gather8(_pack_rows([g for _, g in late_small]), "gather_late_small_grads")
    late_sum = _sum_parts(late_all, "sum_late_small_grads")
    _, (early_landed,) = _remote_wait(early_handle, _bcast_plan, dg_mix, "small_grads_wait")
    early_sum = _sum_chips_in_order(chip_id, early_chip, early_landed, "sum_early_small_grads")
    r_small = sum(_rows_of(args[n].shape) for n, _ in small)
    r_pad = r_small + (-r_small) % 256
    fill = jnp.zeros((r_pad - r_small, LANES), F32)
    g_small = jnp.concatenate([late_sum, early_sum[:r_early], fill], axis=0)

    def pack_small(prefix):
        return jnp.concatenate([_pack_rows([args[prefix + n] for n, _ in small]), fill], axis=0)

    res = _adamw(pack_small(""), pack_small("m_"), pack_small("v_"), [g_small], "adamw_small")
    off = 0
    for n, _ in small:
        shape = args[n].shape
        rows = _rows_of(shape)
        for kind, r in zip(("grad_", "delta_", "new_m_", "new_v_"), res):
            out[kind + n] = r[off:off + rows].reshape(shape)
        off += rows

    rcw_cols = rnn_conv_w.shape[2]
    g_rcw = lax.dynamic_slice(early_sum[r_early:r_early + 32].reshape(4, D), (0, chip * rcw_cols), (4, rcw_cols))
    g_fcw = lax.dynamic_slice(early_sum[r_early + 32:r_early + 32 + 144].reshape(3, 2 * dff), (0, chip * ns), (3, ns))
    conv = [("rnn_conv_w", g_rcw), ("ffn_conv_w", g_fcw)]
    res = _adamw(_pack_rows([args[n] for n, _ in conv]), _pack_rows([args["m_" + n] for n, _ in conv]),
                 _pack_rows([args["v_" + n] for n, _ in conv]), [_pack_rows([g for _, g in conv])], "adamw_conv")
    off = 0
    for n, _ in conv:
        shape = args[n].shape
        cnt = shape[1] * shape[2] // LANES
        for kind, r in zip(("grad_", "delta_", "new_m_", "new_v_"), res):
            out[kind + n] = r[off:off + cnt].reshape(shape)
        off += _rows_of(shape)

    dmod_all = late_all[:, 0:6 * D // LANES, :].reshape(N_DEV, 6 * D)
    dmod_sh = lax.dynamic_slice(dmod_all, (0, chip * ns), (N_DEV, ns))
    res = _ada_adamw(jnp.transpose(c_all), dmod_sh, w_ada[0], m_w_ada[0], v_w_ada[0])
    for kind, r in zip(("grad_", "delta_", "new_m_", "new_v_"), res):
        out[kind + "w_ada"] = r.reshape(w_ada.shape)

    loss = lax.psum(loss_part[0, 0], ("x", "y", "c"))
    names = ["w_ada", "b_ada", "norm_mix_g", "w_in", "rnn_conv_w", "rnn_conv_b", "lru_w_a", "lru_b_a", "lru_w_x",
             "lru_b_x", "lru_lambda", "sgu_ln_g", "sgu_ln_b", "sgu_w_s", "sgu_b_s", "w_branch_a", "w_branch_b",
             "w_out", "norm_ffn_g", "w_up", "ffn_conv_w", "ffn_conv_b", "w_down", "norm_final_g"]
    result = [loss, grad_x.reshape(x.shape)]
    for kind in ("grad_", "delta_", "new_m_", "new_v_"):
        result += [out[kind + n] for n in names]
    return tuple(result)
```

```python
import functools

import jax
import jax.numpy as jnp
from jax import lax
from jax.experimental import pallas as pl
from jax.experimental.pallas import tpu as pltpu

F32 = jnp.float32
BF = jnp.bfloat16

D = 1024
HEADS = 8
HD = D // HEADS
SGU_BLOCK = 128
N_CHIPS = 4
N_DEV = 8
EPS = 1e-6
LRU_C = 8.0
LANES = 128
SUBLANES = 8

ADAM_LR = 0.001
ADAM_B1 = 0.9
ADAM_B2 = 0.999
ADAM_EPS = 1e-08
ADAM_WD = 0.01
ADAM_STEP = 10

GELU_K0 = 0.7978845608028654
GELU_K1 = 0.044715

HBM_SPEC = pl.BlockSpec(memory_space=pltpu.HBM)
MESH_ID = pl.DeviceIdType.MESH


def _pcall(body, *, name, out_shape, grid=(), in_specs=None, out_specs=None, scratch=(), vmem_mb=32, aliases=None,
           grid_spec=None):
    kw = {}
    if aliases:
        kw["input_output_aliases"] = aliases
    if grid_spec is not None:
        kw["grid_spec"] = grid_spec
        ndim = len(grid_spec.grid)
    else:
        kw.update(grid=grid, in_specs=in_specs, out_specs=out_specs, scratch_shapes=list(scratch))
        ndim = len(grid)
    if ndim:
        params = pltpu.CompilerParams(dimension_semantics=("arbitrary",) * ndim, vmem_limit_bytes=vmem_mb * 2 ** 20)
    else:
        params = pltpu.CompilerParams(vmem_limit_bytes=vmem_mb * 2 ** 20)
    return pl.pallas_call(body, name=name, out_shape=out_shape, compiler_params=params, **kw)


def _gelu_cdf(x, x2):
    return 0.5 * jnp.tanh(x * (GELU_K0 + (GELU_K0 * GELU_K1) * x2)) + 0.5


def _gelu(x):
    return x * _gelu_cdf(x, x * x)


def _gelu_and_grad(x):
    x2 = x * x
    s = _gelu_cdf(x, x2)
    g = x * s
    dg = s * (1.0 + (x - g) * ((2.0 * GELU_K0) + (6.0 * GELU_K0 * GELU_K1) * x2))
    return g, dg


def _sigmoid(x):
    return 1.0 / (1.0 + jnp.exp(-x))


def _sigmoid_t(x):
    return 0.5 * jnp.tanh(0.5 * x) + 0.5


def _log_sigmoid(x):
    e = jnp.exp(-jnp.abs(x))
    u = 1.0 + e
    d = u - 1.0
    l1p = jnp.where(d == 0.0, e, jnp.log(u) * (e / jnp.where(d == 0.0, 1.0, d)))
    return jnp.minimum(x, 0.0) - l1p


def _dot(a, b):
    return jnp.dot(a, b, preferred_element_type=F32)


def _dot_nt(a, b):
    return lax.dot_general(a, b, (((1,), (1,)), ((), ())), preferred_element_type=F32)


def _dot_tn(a, b):
    return lax.dot_general(a, b, (((0,), (0,)), ((), ())), preferred_element_type=F32)


def _shift_down(x, halo, s):
    r = pltpu.roll(x, s, 0)
    rows = lax.broadcasted_iota(jnp.int32, (SUBLANES, x.shape[1]), 0)
    head = jnp.where(rows < s, pltpu.roll(halo, s, 0), r[0:SUBLANES])
    return jnp.concatenate([head, r[SUBLANES:]], axis=0)


def _shift_up(x, halo, s):
    n = x.shape[0]
    r = pltpu.roll(x, n - s, 0)
    rows = lax.broadcasted_iota(jnp.int32, (SUBLANES, x.shape[1]), 0)
    tail = jnp.where(rows >= SUBLANES - s, pltpu.roll(halo, SUBLANES - s, 0), r[n - SUBLANES:n])
    return jnp.concatenate([r[:n - SUBLANES], tail], axis=0)


def _scan_rows(a, u, reverse):
    n, width = a.shape
    rows = lax.broadcasted_iota(jnp.int32, (n, width), 0)
    d = 1
    while d < n:
        if d < SUBLANES:
            keep = rows < n - d if reverse else rows >= d
            shift = n - d if reverse else d
            a_s = jnp.where(keep, pltpu.roll(a, shift, 0), 1.0)
            u_s = jnp.where(keep, pltpu.roll(u, shift, 0), 0.0)
        elif reverse:
            a_s = jnp.concatenate([a[d:], jnp.ones((d, width), a.dtype)], axis=0)
            u_s = jnp.concatenate([u[d:], jnp.zeros((d, width), u.dtype)], axis=0)
        else:
            a_s = jnp.concatenate([jnp.ones((d, width), a.dtype), a[:n - d]], axis=0)
            u_s = jnp.concatenate([jnp.zeros((d, width), u.dtype), u[:n - d]], axis=0)
        u = a * u_s + u
        a = a * a_s
        d *= 2
    return a, u


def _colsum(x):
    return jnp.sum(x, axis=0, keepdims=True)


def _rms_stats(x):
    r = lax.rsqrt(jnp.mean(x * x, axis=-1, keepdims=True) + EPS)
    return r, x * r


def _lru_gates(xc, wa_ref, ba, wx_ref, bx, lam):
    pr, pi = [], []
    for hh in range(HEADS):
        xh = xc[:, hh * HD:(hh + 1) * HD].astype(BF)
        pr.append(_dot(xh, wa_ref[hh].astype(BF)))
        pi.append(_dot(xh, wx_ref[hh].astype(BF)))
    r = _sigmoid_t(jnp.concatenate(pr, axis=1) + ba)
    ig = _sigmoid_t(jnp.concatenate(pi, axis=1) + bx)
    ls = _log_sigmoid(lam)
    log_a = LRU_C * r * ls
    a = jnp.exp(log_a)
    x2 = 2.0 * log_a
    u = a * a
    lu = jnp.log(jnp.maximum(u, 1e-37))
    ratio = x2 * pl.reciprocal(jnp.where(lu == 0.0, 1.0, lu), approx=True)
    em1 = jnp.where(lu == 0.0, x2, jnp.where(u < 1e-30, -1.0, (u - 1.0) * ratio))
    mult = jnp.sqrt(-em1)
    return r, ig, ls, a, mult


def _sgu_mix(vln, ws_ref, bst_ref, tb):
    ri = lax.broadcasted_iota(jnp.int32, (SGU_BLOCK, SGU_BLOCK), 0)
    ci = lax.broadcasted_iota(jnp.int32, (SGU_BLOCK, SGU_BLOCK), 1)
    wm = [jnp.where(ri >= ci, ws_ref[g], 0.0).astype(BF) for g in range(HEADS)]
    blocks = []
    for blk in range(tb // SGU_BLOCK):
        cols = []
        for g in range(HEADS):
            vb = vln[blk * SGU_BLOCK:(blk + 1) * SGU_BLOCK, g * HD:(g + 1) * HD].astype(BF)
            cols.append(_dot(wm[g], vb) + bst_ref[:, g:g + 1])
        blocks.append(jnp.concatenate(cols, axis=1))
    mixed = blocks[0] if len(blocks) == 1 else jnp.concatenate(blocks, axis=0)
    return wm, mixed


def _layernorm_stats(v):
    mu = jnp.mean(v, axis=-1, keepdims=True)
    vc = v - mu
    rstd = lax.rsqrt(jnp.mean(vc * vc, axis=-1, keepdims=True) + EPS)
    return rstd, vc * rstd


def _my_xyc():
    return lax.axis_index("x"), lax.axis_index("y"), lax.axis_index("c")


def _gather_weights(srcs, halve):
    n = len(srcs)
    out_shape = [jax.ShapeDtypeStruct((N_CHIPS,) + s.shape, s.dtype) for s in srcs]

    def body(*refs):
        src, out = refs[:n], refs[n:2 * n]
        send_sems, recv_sems, fwd_send, fwd_recv, loc_sems = refs[2 * n:]
        x, y, c = _my_xyc()
        me = 2 * x + y
        chips = [(1 - x, y), (x, 1 - y), (1 - x, 1 - y)]

        def half(ref, a, which):
            if not halve[a]:
                return ref
            h = srcs[a].shape[0] // 2
            return ref.at[pl.ds(which * h, h)]

        def ici(a, k, frm):
            px, py = chips[k]
            return pltpu.make_async_remote_copy(
                src_ref=half(src[a], a, c), dst_ref=half(out[a].at[frm], a, c),
                send_sem=send_sems.at[a, k], recv_sem=recv_sems.at[a, k],
                device_id=(px, py, c), device_id_type=MESH_ID)

        def d2d(a, k, which):
            px, py = chips[k]
            rows = half(out[a].at[2 * px + py], a, which)
            return pltpu.make_async_remote_copy(
                src_ref=rows, dst_ref=rows, send_sem=fwd_send.at[a, k], recv_sem=fwd_recv.at[a, k],
                device_id=(x, y, 1 - c), device_id_type=MESH_ID)

        local, sends = [], []
        for a in range(n):
            lc = pltpu.make_async_copy(src[a], out[a].at[me], loc_sems.at[a])
            lc.start()
            local.append(lc)
            for k in range(3):
                cp = ici(a, k, me)
                cp.start()
                sends.append(cp)
        for a in range(n):
            for k in range(3):
                px, py = chips[k]
                ici(a, k, 2 * px + py).wait_recv()
                if halve[a]:
                    fw = d2d(a, k, c)
                    fw.start()
                    sends.append(fw)
        for a in range(n):
            if halve[a]:
                for k in range(3):
                    d2d(a, k, 1 - c).wait_recv()
        for cp in sends:
            cp.wait_send()
        for lc in local:
            lc.wait()

    sem = pltpu.SemaphoreType.DMA((n, 3))
    return _pcall(body, name="gather_weights", out_shape=out_shape, in_specs=[HBM_SPEC] * n,
                  out_specs=[HBM_SPEC] * n, scratch=[sem, sem, sem, sem, pltpu.SemaphoreType.DMA((n,))])(*srcs)


SEM_SPEC = pl.BlockSpec(memory_space=pltpu.SEMAPHORE)


def _remote_start(srcs, lands, plan, ncopies, name):
    n, m = len(srcs), len(lands)

    def body(*refs):
        src, land = refs[:n], refs[n:n + m]
        send_sems, recv_sems = refs[n + m], refs[n + m + 1]
        token = refs[-1]
        x, y, c = _my_xyc()
        for i, (s, d, dev) in enumerate(plan(src, land, x, y, c)):
            pltpu.make_async_remote_copy(src_ref=s, dst_ref=d, send_sem=send_sems.at[i], recv_sem=recv_sems.at[i],
                                         device_id=dev, device_id_type=MESH_ID).start()
        token[...] = jnp.zeros_like(token)

    bufs = list(srcs) + list(lands)
    out = pl.pallas_call(
        body, name=name,
        out_shape=(pltpu.SemaphoreType.DMA((ncopies,)), pltpu.SemaphoreType.DMA((ncopies,)),
                   *[pltpu.HBM(b.shape, b.dtype) for b in bufs], jax.ShapeDtypeStruct((SUBLANES, LANES), F32)),
        in_specs=[HBM_SPEC] * (n + m),
        out_specs=(SEM_SPEC, SEM_SPEC, *[HBM_SPEC] * (n + m), pl.BlockSpec(memory_space=pltpu.VMEM)),
        input_output_aliases={i: 2 + i for i in range(n + m)},
        compiler_params=pltpu.CompilerParams(has_side_effects=pltpu.SideEffectType.DATAFLOW_SIDE_EFFECTING),
    )(*[pltpu.with_memory_space_constraint(b, pltpu.HBM) for b in bufs])
    return (out[0], out[1], out[2:2 + n], out[2 + n:2 + n + m]), out[-1]


def _remote_wait(handle, plan, after, name):
    send_sems, recv_sems, srcs, lands = handle
    n, m = len(srcs), len(lands)

    def body(*refs):
        src, land = refs[:n], refs[n:n + m]
        ssem, rsem = refs[n + m], refs[n + m + 1]
        x, y, c = _my_xyc()
        for i, (s, d, dev) in enumerate(plan(src, land, x, y, c)):
            cp = pltpu.make_async_remote_copy(src_ref=s, dst_ref=d, send_sem=ssem.at[i], recv_sem=rsem.at[i],
                                              device_id=dev, device_id_type=MESH_ID)
            cp.wait_send()
            cp.wait_recv()

    bufs = list(srcs) + list(lands)
    out = pl.pallas_call(
        body, name=name, out_shape=tuple(pltpu.HBM(b.shape, b.dtype) for b in bufs),
        in_specs=[HBM_SPEC] * (n + m) + [SEM_SPEC, SEM_SPEC, pl.BlockSpec(memory_space=pl.ANY)],
        out_specs=tuple([HBM_SPEC] * (n + m)), input_output_aliases={i: i for i in range(n + m)},
        compiler_params=pltpu.CompilerParams(has_side_effects=pltpu.SideEffectType.DATAFLOW_SIDE_EFFECTING),
    )(*bufs, send_sems, recv_sems, after)
    return out[:n], out[n:]


def _chips_of(x, y):
    return [(1 - x, y), (x, 1 - y), (1 - x, 1 - y)]


def _gather_half_plan(shapes):
    def plan(src, land, x, y, c):
        me = 2 * x + y
        out = []
        for a, shape in enumerate(shapes):
            h = shape[0] // 2
            rows = pl.ds(c * h, h)
            for px, py in _chips_of(x, y):
                out.append((src[a].at[rows], land[a].at[me, rows], (px, py, c)))
        return out

    return plan


def _forward_halves(srcs, lands):
    n = len(srcs)

    def body(*refs):
        src, land = refs[:n], refs[2 * n:3 * n]
        send_sems, recv_sems, loc_sems = refs[3 * n:]
        x, y, c = _my_xyc()
        me = 2 * x + y

        def fwd(a, k, which):
            px, py = _chips_of(x, y)[k]
            h = srcs[a].shape[0] // 2
            rows = land[a].at[2 * px + py, pl.ds(which * h, h)]
            return pltpu.make_async_remote_copy(
                src_ref=rows, dst_ref=rows, send_sem=send_sems.at[a, k], recv_sem=recv_sems.at[a, k],
                device_id=(x, y, 1 - c), device_id_type=MESH_ID)

        local, sends = [], []
        for a in range(n):
            lc = pltpu.make_async_copy(src[a], land[a].at[me], loc_sems.at[a])
            lc.start()
            local.append(lc)
            for k in range(3):
                cp = fwd(a, k, c)
                cp.start()
                sends.append(cp)
        for a in range(n):
            for k in range(3):
                fwd(a, k, 1 - c).wait_recv()
        for cp in sends:
            cp.wait_send()
        for lc in local:
            lc.wait()

    sem = pltpu.SemaphoreType.DMA((n, 3))
    return _pcall(body, name="forward_halves", out_shape=[jax.ShapeDtypeStruct(l.shape, l.dtype) for l in lands],
                  in_specs=[HBM_SPEC] * (2 * n), out_specs=[HBM_SPEC] * n, aliases={n + a: a for a in range(n)},
                  scratch=[sem, sem, pltpu.SemaphoreType.DMA((n,))])(*srcs, *lands)


def _send_other_half(grads, name):
    n = len(grads)
    out_shape = [jax.ShapeDtypeStruct((N_CHIPS, g.shape[1] // 2, g.shape[2]), g.dtype) for g in grads]

    def body(*refs):
        src, out = refs[:n], refs[n:2 * n]
        send_sems, recv_sems = refs[2 * n:]
        x, y, c = _my_xyc()
        cps = []
        for a in range(n):
            h = grads[a].shape[1] // 2
            cp = pltpu.make_async_remote_copy(
                src_ref=src[a].at[:, pl.ds((1 - c) * h, h), :], dst_ref=out[a],
                send_sem=send_sems.at[a], recv_sem=recv_sems.at[a],
                device_id=(x, y, 1 - c), device_id_type=MESH_ID)
            cp.start()
            cps.append(cp)
        for cp in cps:
            cp.wait()

    return _pcall(body, name=name, out_shape=out_shape, in_specs=[HBM_SPEC] * n,
                  out_specs=[HBM_SPEC] * n,
                  scratch=[pltpu.SemaphoreType.DMA((n,)), pltpu.SemaphoreType.DMA((n,))])(*grads)


def _share_halves(totals):
    n = len(totals)
    out_shape = [jax.ShapeDtypeStruct((2,) + t.shape, t.dtype) for t in totals]

    def body(*refs):
        src, out = refs[:n], refs[n:2 * n]
        send_sems, recv_sems, loc_sems = refs[2 * n:]
        x, y, c = _my_xyc()
        cps, local = [], []
        for a in range(n):
            lc = pltpu.make_async_copy(src[a], out[a].at[c], loc_sems.at[a])
            lc.start()
            local.append(lc)
            cp = pltpu.make_async_remote_copy(
                src_ref=src[a], dst_ref=out[a].at[c], send_sem=send_sems.at[a], recv_sem=recv_sems.at[a],
                device_id=(x, y, 1 - c), device_id_type=MESH_ID)
            cp.start()
            cps.append(cp)
        for cp in cps:
            cp.wait()
        for lc in local:
            lc.wait()

    sem = pltpu.SemaphoreType.DMA((n,))
    return _pcall(body, name="share_halves", out_shape=out_shape, in_specs=[HBM_SPEC] * n,
                  out_specs=[HBM_SPEC] * n, scratch=[sem, sem, sem])(*totals)


def _gather8(src, name):
    def body(src_ref, out_ref, send_sems, recv_sems, loc_sem):
        x, y, c = _my_xyc()
        me = 4 * x + 2 * y + c
        lc = pltpu.make_async_copy(src_ref, out_ref.at[me], loc_sem)
        lc.start()
        cps = []
        for k in range(1, N_DEV):
            px = 1 - x if (k >> 2) & 1 else x
            py = 1 - y if (k >> 1) & 1 else y
            pc = 1 - c if k & 1 else c
            cp = pltpu.make_async_remote_copy(
                src_ref=src_ref, dst_ref=out_ref.at[me], send_sem=send_sems.at[k - 1], recv_sem=recv_sems.at[k - 1],
                device_id=(px, py, pc), device_id_type=MESH_ID)
            cp.start()
            cps.append(cp)
        for cp in cps:
            cp.wait()
        lc.wait()

    return _pcall(body, name=name, out_shape=jax.ShapeDtypeStruct((N_DEV,) + src.shape, src.dtype),
                  in_specs=[HBM_SPEC], out_specs=HBM_SPEC,
                  scratch=[pltpu.SemaphoreType.DMA((N_DEV - 1,)), pltpu.SemaphoreType.DMA((N_DEV - 1,)),
                           pltpu.SemaphoreType.DMA])(src)


def _cast_shards(arrs, name, after=None):
    n = len(arrs)
    extra = [] if after is None else [after]

    def body(*refs):
        ins, outs = refs[:n], refs[n + len(extra):]
        for a in range(n):
            outs[a][...] = ins[a][...].astype(BF)

    specs = [pl.BlockSpec((s.shape[0] // 4, s.shape[1]), lambda i: (i, 0)) for s in arrs]
    return _pcall(body, name=name, grid=(4,), in_specs=specs + [pl.BlockSpec(memory_space=pl.ANY)] * len(extra),
                  out_specs=specs, out_shape=[jax.ShapeDtypeStruct(s.shape, BF) for s in arrs])(*arrs, *extra)


def _row_tile(rows, cols):
    t = rows
    while t * cols * 4 > (3 << 19) and t % 16 == 0:
        t //= 2
    return t


def _sum_parts(parts, name):
    p, rows, cols = parts.shape
    tr = _row_tile(rows, cols * p // 2)

    def body(p_ref, o_ref):
        acc = p_ref[0].astype(F32)
        for k in range(1, p):
            acc = acc + p_ref[k].astype(F32)
        o_ref[...] = acc

    return _pcall(body, name=name, grid=(rows // tr,),
                  in_specs=[pl.BlockSpec((p, tr, cols), lambda i: (0, i, 0))],
                  out_specs=pl.BlockSpec((tr, cols), lambda i: (i, 0)),
                  out_shape=jax.ShapeDtypeStruct((rows, cols), F32), vmem_mb=48)(parts)


def _add_halves_bf16(core, grad, other, name):
    nchip, rows, cols = grad.shape
    h = rows // 2
    tr = _row_tile(h, cols)
    nh = h // tr

    def body(c_ref, g_ref, o_ref, s_ref):
        del c_ref
        s_ref[...] = (g_ref[...] + o_ref[...]).astype(BF)

    grid_spec = pltpu.PrefetchScalarGridSpec(
        num_scalar_prefetch=1, grid=(nchip, nh),
        in_specs=[pl.BlockSpec((1, tr, cols), lambda p, i, c_ref: (p, c_ref[0] * nh + i, 0)),
                  pl.BlockSpec((1, tr, cols), lambda p, i, c_ref: (p, i, 0))],
        out_specs=pl.BlockSpec((1, tr, cols), lambda p, i, c_ref: (p, i, 0)))
    return _pcall(body, name=name, grid_spec=grid_spec, out_shape=jax.ShapeDtypeStruct((nchip, h, cols), BF),
                  vmem_mb=48)(core, grad, other)


def _sum_own_and_landed(chip, sums, landed, name):
    _, rows, cols = sums.shape
    tr = _row_tile(rows, 2 * cols)

    def body(chip_ref, own_ref, land_ref, o_ref):
        del chip_ref
        acc = own_ref[0].astype(F32)
        for k in range(3):
            acc = acc + land_ref[k].astype(F32)
        o_ref[...] = acc

    grid_spec = pltpu.PrefetchScalarGridSpec(
        num_scalar_prefetch=1, grid=(rows // tr,),
        in_specs=[pl.BlockSpec((1, tr, cols), lambda i, chip_ref: (chip_ref[0], i, 0)),
                  pl.BlockSpec((3, tr, cols), lambda i, chip_ref: (0, i, 0))],
        out_specs=pl.BlockSpec((tr, cols), lambda i, chip_ref: (i, 0)))
    return _pcall(body, name=name, grid_spec=grid_spec, out_shape=jax.ShapeDtypeStruct((rows, cols), F32),
                  vmem_mb=48)(chip, sums, landed)


def _swap_core(arr, name):
    def body(src, out, send_sem, recv_sem):
        x, y, c = _my_xyc()
        cp = pltpu.make_async_remote_copy(src_ref=src, dst_ref=out, send_sem=send_sem, recv_sem=recv_sem,
                                          device_id=(x, y, 1 - c), device_id_type=MESH_ID)
        cp.start()
        cp.wait()

    return _pcall(body, name=name, out_shape=jax.ShapeDtypeStruct(arr.shape, arr.dtype), in_specs=[HBM_SPEC],
                  out_specs=HBM_SPEC, scratch=[pltpu.SemaphoreType.DMA, pltpu.SemaphoreType.DMA])(arr)


def _add_pair(a, b, name):
    rows, cols = a.shape
    tr = _row_tile(rows, 2 * cols)

    def body(a_ref, b_ref, o_ref):
        o_ref[...] = a_ref[...] + b_ref[...]

    spec = pl.BlockSpec((tr, cols), lambda i: (i, 0))
    return _pcall(body, name=name, grid=(rows // tr,), in_specs=[spec, spec], out_specs=spec,
                  out_shape=jax.ShapeDtypeStruct((rows, cols), F32))(a, b)


def _sum_chips_in_order(chip, own, landed, name):
    rows, cols = own.shape
    tr = _row_tile(rows, 4 * cols)

    def body(chip_ref, own_ref, land_ref, o_ref):
        me = chip_ref[0]
        acc = None
        for p in range(N_CHIPS):
            q = p ^ me
            k = jnp.where(q == 2, 0, jnp.where(q == 1, 1, 2))
            term = jnp.where(q == 0, own_ref[...], land_ref[k])
            acc = term if acc is None else acc + term
        o_ref[...] = acc

    grid_spec = pltpu.PrefetchScalarGridSpec(
        num_scalar_prefetch=1, grid=(rows // tr,),
        in_specs=[pl.BlockSpec((tr, cols), lambda i, chip_ref: (i, 0)),
                  pl.BlockSpec((3, tr, cols), lambda i, chip_ref: (0, i, 0))],
        out_specs=pl.BlockSpec((tr, cols), lambda i, chip_ref: (i, 0)))
    return _pcall(body, name=name, grid_spec=grid_spec, out_shape=jax.ShapeDtypeStruct((rows, cols), F32))(
        chip, own, landed)


def _bcast_plan(src, land, x, y, c):
    return [(src[0], land[0].at[k], (px, py, c)) for k, (px, py) in enumerate(_chips_of(x, y))]


def _scatter_plan(count):
    def plan(src, land, x, y, c):
        out = []
        for a in range(count):
            for k, (px, py) in enumerate(_chips_of(x, y)):
                out.append((src[a].at[2 * px + py], land[a].at[k], (px, py, c)))
        return out

    return plan


def _adamw_math(w, g, m, v):
    m2 = ADAM_B1 * m + (1.0 - ADAM_B1) * g
    v2 = ADAM_B2 * v + (1.0 - ADAM_B2) * (g * g)
    m_hat = m2 / (1.0 - ADAM_B1 ** ADAM_STEP)
    v_hat = v2 / (1.0 - ADAM_B2 ** ADAM_STEP)
    delta = -ADAM_LR * (m_hat / (jnp.sqrt(v_hat) + ADAM_EPS) + ADAM_WD * w)
    return delta, m2, v2


def _adamw(w, m, v, grads, name):
    rows, cols = w.shape
    tr = _row_tile(rows, cols)
    ng = len(grads)

    def body(*refs):
        w_ref, m_ref, v_ref = refs[:3]
        g = refs[3][...]
        for k in range(1, ng):
            g = g + refs[3 + k][...]
        g_ref, d_ref, m2_ref, v2_ref = refs[3 + ng:]
        delta, m2, v2 = _adamw_math(w_ref[...], g, m_ref[...], v_ref[...])
        g_ref[...] = g
        d_ref[...] = delta
        m2_ref[...] = m2
        v2_ref[...] = v2

    spec = pl.BlockSpec((tr, cols), lambda i: (i, 0))
    return _pcall(body, name=name, grid=(rows // tr,), in_specs=[spec] * (3 + ng), out_specs=[spec] * 4,
                  out_shape=[jax.ShapeDtypeStruct((rows, cols), F32)] * 4, vmem_mb=48)(w, m, v, *grads)


def _ada_adamw(ct, dmod, w, m, v):
    rows, cols = w.shape
    tr = _row_tile(rows, cols)

    def body(ct_ref, dm_ref, w_ref, m_ref, v_ref, g_ref, d_ref, m2_ref, v2_ref):
        cv = ct_ref[...]
        ca = cv * _sigmoid(cv)
        g = ca[:, 0:1] * dm_ref[0:1, :]
        for b in range(1, N_DEV):
            g = g + ca[:, b:b + 1] * dm_ref[b:b + 1, :]
        delta, m2, v2 = _adamw_math(w_ref[...], g, m_ref[...], v_ref[...])
        g_ref[...] = g
        d_ref[...] = delta
        m2_ref[...] = m2
        v2_ref[...] = v2

    spec = pl.BlockSpec((tr, cols), lambda i: (i, 0))
    return _pcall(body, name="ada_adamw", grid=(rows // tr,),
                  in_specs=[pl.BlockSpec((tr, N_DEV), lambda i: (i, 0)), pl.BlockSpec((N_DEV, cols), lambda i: (0, 0)),
                            spec, spec, spec],
                  out_specs=[spec] * 4, out_shape=[jax.ShapeDtypeStruct((rows, cols), F32)] * 4,
                  vmem_mb=48)(ct, dmod, w, m, v)


def _mod_fwd(c_all, w, b):
    cols = w.shape[1]
    tn = cols // 3

    def body(c_ref, w_ref, b_ref, o_ref):
        cv = c_ref[...]
        ca = (cv * _sigmoid(cv)).astype(BF)
        o_ref[...] = _dot(ca, w_ref[...].astype(BF)) + b_ref[...]

    return _pcall(body, name="mod_fwd", grid=(3,),
                  in_specs=[pl.BlockSpec((N_DEV, D), lambda j: (0, 0)), pl.BlockSpec((D, tn), lambda j: (0, j)),
                            pl.BlockSpec((1, tn), lambda j: (0, j))],
                  out_specs=pl.BlockSpec((N_DEV, tn), lambda j: (0, j)),
                  out_shape=jax.ShapeDtypeStruct((N_DEV, cols), F32))(c_all, w, b)


def _resident(shape):
    zeros = (0,) * len(shape)
    return pl.BlockSpec(shape, lambda *_: zeros, pipeline_mode=pl.Buffered(1))


def _modnorm_matmul(x, g, scale, shift, w4, name, tm=256):
    T = x.shape[0]
    tm = min(tm, T)
    ns = w4.shape[2]

    def body(x_ref, g_ref, sc_ref, sh_ref, w_ref, h_ref, z_ref):
        _, xh = _rms_stats(x_ref[...])
        h = ((xh * g_ref[...]) * (1.0 + sc_ref[...]) + sh_ref[...]).astype(BF)
        h_ref[...] = h
        for j in range(N_CHIPS):
            z_ref[:, j * ns:(j + 1) * ns] = _dot(h, w_ref[j])

    vec = pl.BlockSpec((1, D), lambda i: (0, 0))
    return _pcall(body, name=name, grid=(T // tm,),
                  in_specs=[pl.BlockSpec((tm, D), lambda i: (i, 0)), vec, vec, vec, _resident(w4.shape)],
                  out_specs=[pl.BlockSpec((tm, D), lambda i: (i, 0)), pl.BlockSpec((tm, N_CHIPS * ns), lambda i: (i, 0))],
                  out_shape=[jax.ShapeDtypeStruct((T, D), BF), jax.ShapeDtypeStruct((T, N_CHIPS * ns), F32)],
                  vmem_mb=48)(x, g, scale, shift, w4)


def _rglru_fwd(z, cw, cb, wa, ba, wx, bx, lam, tb=256):
    T = z.shape[0]
    tb = min(tb, T)

    def body(xr_ref, gr_ref, cw_ref, cb_ref, wa_ref, ba_ref, wx_ref, bx_ref, lam_ref, h_ref, ya_ref, prev, hc):
        i = pl.program_id(0)

        @pl.when(i == 0)
        def _():
            prev[...] = jnp.zeros_like(prev)
            hc[...] = jnp.zeros_like(hc)

        xr = xr_ref[...]
        pv = prev[...]
        xc = (cb_ref[...] + cw_ref[3:4, :] * xr + cw_ref[2:3, :] * _shift_down(xr, pv, 1)
              + cw_ref[1:2, :] * _shift_down(xr, pv, 2) + cw_ref[0:1, :] * _shift_down(xr, pv, 3))
        prev[...] = xr[tb - SUBLANES:tb]
        _, ig, _, a, mult = _lru_gates(xc, wa_ref, ba_ref[...], wx_ref, bx_ref[...], lam_ref[...])
        a, u = _scan_rows(a, mult * (ig * xc), reverse=False)
        h = u + a * hc[SUBLANES - 1:SUBLANES, :]
        hc[...] = h[tb - SUBLANES:tb]
        h_ref[...] = h
        ya_ref[...] = (h * _gelu(gr_ref[...])).astype(BF)

    vec = pl.BlockSpec((1, D), lambda i: (0, 0))
    wspec = pl.BlockSpec((HEADS, HD, HD), lambda i: (0, 0, 0))
    return _pcall(body, name="rglru_fwd", grid=(T // tb,),
                  in_specs=[pl.BlockSpec((tb, D), lambda i: (i, 0)), pl.BlockSpec((tb, D), lambda i: (i, 1)),
                            pl.BlockSpec((4, D), lambda i: (0, 0)), vec, wspec, vec, wspec, vec, vec],
                  out_specs=[pl.BlockSpec((tb, D), lambda i: (i, 0))] * 2,
                  out_shape=[jax.ShapeDtypeStruct((T, D), F32), jax.ShapeDtypeStruct((T, D), BF)],
                  scratch=[pltpu.VMEM((SUBLANES, D), F32), pltpu.VMEM((SUBLANES, D), F32)],
                  vmem_mb=48)(z, z, cw, cb, wa, ba, wx, bx, lam)


def _sgu_fwd(z, lg, lb, ws, bst, tb=256):
    T = z.shape[0]
    tb = min(tb, T)

    def body(zu_ref, zv_ref, lg_ref, lb_ref, ws_ref, bst_ref, yb_ref):
        _, xh = _layernorm_stats(_gelu(zv_ref[...]))
        vln = xh * lg_ref[...] + lb_ref[...]
        _, mixed = _sgu_mix(vln, ws_ref, bst_ref, tb)
        yb_ref[...] = (_gelu(zu_ref[...]) * mixed).astype(BF)

    vec = pl.BlockSpec((1, D), lambda i: (0, 0))
    return _pcall(body, name="sgu_fwd", grid=(T // tb,),
                  in_specs=[pl.BlockSpec((tb, D), lambda i: (i, 2)), pl.BlockSpec((tb, D), lambda i: (i, 3)), vec, vec,
                            pl.BlockSpec((HEADS, SGU_BLOCK, SGU_BLOCK), lambda i: (0, 0, 0)),
                            pl.BlockSpec((SGU_BLOCK, HEADS), lambda i: (0, 0))],
                  out_specs=pl.BlockSpec((tb, D), lambda i: (i, 0)),
                  out_shape=jax.ShapeDtypeStruct((T, D), BF))(z, z, lg, lb, ws, bst)


def _mix_out(ya_pre, yb_pre, z, x, gate1, wba, wbb, wo, tm=256):
    T = x.shape[0]
    tm = min(tm, T)

    def body(yap_ref, ybp_ref, ga_ref, gb_ref, x_ref, g1_ref, wa_ref, wb_ref, wo_ref,
             x2_ref, mg_ref, ya_ref, yb_ref, o_ref):
        ya = _dot(yap_ref[...], wa_ref[...])
        yb = _dot(ybp_ref[...], wb_ref[...])
        merged = (_sigmoid_t(ga_ref[...]) * ya + _sigmoid_t(gb_ref[...]) * yb).astype(BF)
        o = _dot(merged, wo_ref[...])
        x2_ref[...] = x_ref[...] + g1_ref[...] * o
        mg_ref[...] = merged
        ya_ref[...] = ya.astype(BF)
        yb_ref[...] = yb.astype(BF)
        o_ref[...] = o.astype(BF)

    row = pl.BlockSpec((tm, D), lambda i: (i, 0))
    wspec = pl.BlockSpec((D, D), lambda i: (0, 0))
    return _pcall(body, name="mix_out", grid=(T // tm,),
                  in_specs=[row, row, pl.BlockSpec((tm, D), lambda i: (i, 4)), pl.BlockSpec((tm, D), lambda i: (i, 5)),
                            row, pl.BlockSpec((1, D), lambda i: (0, 0)), wspec, wspec, wspec],
                  out_specs=[row] * 5,
                  out_shape=[jax.ShapeDtypeStruct((T, D), F32)] + [jax.ShapeDtypeStruct((T, D), BF)] * 4,
                  vmem_mb=48)(ya_pre, yb_pre, z, z, x, gate1, wba, wbb, wo)


def _ffn_gate(up, cw, cb, tm=512, cw_blk=768):
    T = up.shape[0]
    tm = min(tm, T)
    dff = up.shape[1] // 2
    ncb = dff // cw_blk

    def body(ua_ref, uv_ref, wa_ref, wv_ref, ba_ref, bv_ref, f_ref, ga_ref, vd_ref, pa, pv):
        i = pl.program_id(1)

        @pl.when(i == 0)
        def _():
            pa[...] = jnp.zeros_like(pa)
            pv[...] = jnp.zeros_like(pv)

        def conv(u_ref, w_ref, b_ref, prev):
            u = u_ref[...]
            p = prev[...]
            hid = (b_ref[...] + w_ref[2:3, :] * u + w_ref[1:2, :] * _shift_down(u, p, 1)
                   + w_ref[0:1, :] * _shift_down(u, p, 2))
            prev[...] = u[tm - SUBLANES:tm]
            return hid

        act = conv(ua_ref, wa_ref, ba_ref, pa)
        val = conv(uv_ref, wv_ref, bv_ref, pv)
        ga, dga = _gelu_and_grad(act)
        f_ref[...] = (ga * val).astype(BF)
        ga_ref[...] = ga.astype(BF)
        vd_ref[...] = (val * dga).astype(BF)

    blk = pl.BlockSpec((tm, cw_blk), lambda cbk, i: (i, cbk))
    return _pcall(body, name="ffn_gate", grid=(ncb, T // tm),
                  in_specs=[pl.BlockSpec((tm, cw_blk), lambda cbk, i: (i, cbk)),
                            pl.BlockSpec((tm, cw_blk), lambda cbk, i: (i, ncb + cbk)),
                            pl.BlockSpec((3, cw_blk), lambda cbk, i: (0, cbk)),
                            pl.BlockSpec((3, cw_blk), lambda cbk, i: (0, ncb + cbk)),
                            pl.BlockSpec((1, cw_blk), lambda cbk, i: (0, cbk)),
                            pl.BlockSpec((1, cw_blk), lambda cbk, i: (0, ncb + cbk))],
                  out_specs=[blk] * 3, out_shape=[jax.ShapeDtypeStruct((T, dff), BF)] * 3,
                  scratch=[pltpu.VMEM((SUBLANES, cw_blk), F32)] * 2)(up, up, cw, cw, cb, cb)


def _ffn_down_loss(f, wd, x2, gate2, gf, target, tm=512):
    T = x2.shape[0]
    tm = min(tm, T)
    dff = f.shape[1]

    def body(f_ref, wd_ref, x2_ref, g2_ref, gf_ref, t_ref, loss_ref, dx3_ref, dfo_ref, dgf_ref, dg2_ref):
        i = pl.program_id(0)

        @pl.when(i == 0)
        def _():
            loss_ref[...] = jnp.zeros_like(loss_ref)
            dgf_ref[...] = jnp.zeros_like(dgf_ref)
            dg2_ref[...] = jnp.zeros_like(dg2_ref)

        fo = _dot(f_ref[...], wd_ref[...])
        x3 = x2_ref[...] + g2_ref[...] * fo
        rstd, xh = _rms_stats(x3)
        err = xh * gf_ref[...] - t_ref[...]
        loss_ref[...] += 0.5 * jnp.sum(jnp.mean(err * err, axis=-1, keepdims=True), axis=0, keepdims=True)
        dy = err * (1.0 / D)
        dgf_ref[...] += _colsum(dy * xh)
        dxh = dy * gf_ref[...]
        dx3 = rstd * (dxh - xh * jnp.mean(dxh * xh, axis=-1, keepdims=True))
        dg2_ref[...] += _colsum(dx3 * fo)
        dx3_ref[...] = dx3
        dfo_ref[...] = (g2_ref[...] * dx3).astype(BF)

    row = pl.BlockSpec((tm, D), lambda i: (i, 0))
    vec = pl.BlockSpec((1, D), lambda i: (0, 0))
    return _pcall(body, name="ffn_down_loss", grid=(T // tm,),
                  in_specs=[pl.BlockSpec((tm, dff), lambda i: (i, 0)), pl.BlockSpec((dff, D), lambda i: (0, 0)),
                            row, vec, vec, row],
                  out_specs=[pl.BlockSpec((1, LANES), lambda i: (0, 0)), row, row, vec, vec],
                  out_shape=[jax.ShapeDtypeStruct((1, LANES), F32), jax.ShapeDtypeStruct((T, D), F32),
                             jax.ShapeDtypeStruct((T, D), BF), jax.ShapeDtypeStruct((1, D), F32),
                             jax.ShapeDtypeStruct((1, D), F32)],
                  vmem_mb=48)(f, wd, x2, gate2, gf, target)


def _ffn_fwd(x2, g, scale, shift, gate2, gf, w_up4, wd, cw, cb, target, tm=256, chunk=768):
    T = x2.shape[0]
    tm = min(tm, T)
    ns = w_up4.shape[2]
    dff = wd.shape[0]
    nchunk = dff // chunk
    per = ns // chunk

    def body(x2_ref, g_ref, sc_ref, sh_ref, g2_ref, gf_ref, wu_ref, wd_ref, cw_ref, cb_ref, t_ref,
             h2_ref, up_ref, f_ref, ga_ref, vd_ref, loss_ref, dx3_ref, dfo_ref, dgf_ref, dg2_ref, prev):
        i = pl.program_id(0)

        @pl.when(i == 0)
        def _():
            prev[...] = jnp.zeros_like(prev)
            loss_ref[...] = jnp.zeros_like(loss_ref)
            dgf_ref[...] = jnp.zeros_like(dgf_ref)
            dg2_ref[...] = jnp.zeros_like(dg2_ref)

        x2v = x2_ref[...]
        _, xh2 = _rms_stats(x2v)
        h2 = ((xh2 * g_ref[...]) * (1.0 + sc_ref[...]) + sh_ref[...]).astype(BF)
        h2_ref[...] = h2

        def conv(u, col):
            cs = slice(col, col + chunk)
            p = prev[:, cs]
            hid = (cb_ref[:, cs] + cw_ref[2:3, cs] * u + cw_ref[1:2, cs] * _shift_down(u, p, 1)
                   + cw_ref[0:1, cs] * _shift_down(u, p, 2))
            prev[:, cs] = u[tm - SUBLANES:tm]
            up_ref[:, cs] = u.astype(BF)
            return hid

        def up_proj(k):
            off = (k % per) * chunk
            return (_dot(h2, wu_ref[k // per, :, off:off + chunk]),
                    _dot(h2, wu_ref[N_CHIPS // 2 + k // per, :, off:off + chunk]))

        fo = None
        nxt = up_proj(0)
        for k in range(nchunk):
            col = k * chunk
            ua, uv = nxt
            if k + 1 < nchunk:
                nxt = up_proj(k + 1)
            act = conv(ua, col)
            val = conv(uv, dff + col)
            ga, dga = _gelu_and_grad(act)
            fk = (ga * val).astype(BF)
            f_ref[:, col:col + chunk] = fk
            ga_ref[:, col:col + chunk] = ga.astype(BF)
            vd_ref[:, col:col + chunk] = (val * dga).astype(BF)
            part = _dot(fk, wd_ref[col:col + chunk, :])
            fo = part if fo is None else fo + part

        x3 = x2v + g2_ref[...] * fo
        rstd, xh = _rms_stats(x3)
        err = xh * gf_ref[...] - t_ref[...]
        loss_ref[...] += 0.5 * jnp.sum(jnp.mean(err * err, axis=-1, keepdims=True), axis=0, keepdims=True)
        dy = err * (1.0 / D)
        dgf_ref[...] += _colsum(dy * xh)
        dxh = dy * gf_ref[...]
        dx3 = rstd * (dxh - xh * jnp.mean(dxh * xh, axis=-1, keepdims=True))
        dg2_ref[...] += _colsum(dx3 * fo)
        dx3_ref[...] = dx3
        dfo_ref[...] = (g2_ref[...] * dx3).astype(BF)

    row = pl.BlockSpec((tm, D), lambda i: (i, 0))
    vec = pl.BlockSpec((1, D), lambda i: (0, 0))
    wide = pl.BlockSpec((tm, 2 * dff), lambda i: (i, 0))
    half = pl.BlockSpec((tm, dff), lambda i: (i, 0))
    return _pcall(body, name="ffn_fwd", grid=(T // tm,),
                  in_specs=[row, vec, vec, vec, vec, vec, _resident(w_up4.shape), _resident(wd.shape),
                            _resident(cw.shape), _resident(cb.shape), row],
                  out_specs=[row, wide, half, half, half, pl.BlockSpec((1, LANES), lambda i: (0, 0)), row, row, vec, vec],
                  out_shape=[jax.ShapeDtypeStruct((T, D), BF), jax.ShapeDtypeStruct((T, 2 * dff), BF),
                             jax.ShapeDtypeStruct((T, dff), BF), jax.ShapeDtypeStruct((T, dff), BF),
                             jax.ShapeDtypeStruct((T, dff), BF), jax.ShapeDtypeStruct((1, LANES), F32),
                             jax.ShapeDtypeStruct((T, D), F32), jax.ShapeDtypeStruct((T, D), BF),
                             jax.ShapeDtypeStruct((1, D), F32), jax.ShapeDtypeStruct((1, D), F32)],
                  scratch=[pltpu.VMEM((SUBLANES, 2 * dff), F32)], vmem_mb=56)(
        x2, g, scale, shift, gate2, gf, w_up4, wd, cw, cb, target)


def _ffn_bwd(dfo, wd, up, ga, vd, cw, tm=256, cw_blk=1536):
    T = up.shape[0]
    tm = min(tm, T)
    dff = up.shape[1] // 2
    ncb = dff // cw_blk
    nrow = T // tm

    def body(dfo_ref, wd_ref, ua_ref, uv_ref, ga_ref, vd_ref, wa_ref, wv_ref,
             du_ref, dwa_ref, dwv_ref, dba_ref, dbv_ref, na, nv):
        i = pl.program_id(1)

        @pl.when(i == 0)
        def _():
            na[...] = jnp.zeros_like(na)
            nv[...] = jnp.zeros_like(nv)
            dwa_ref[...] = jnp.zeros_like(dwa_ref)
            dwv_ref[...] = jnp.zeros_like(dwv_ref)
            dba_ref[...] = jnp.zeros_like(dba_ref)
            dbv_ref[...] = jnp.zeros_like(dbv_ref)

        df = _dot_nt(dfo_ref[...], wd_ref[...])

        def conv_bwd(dh, u_ref, w_ref, nxt, col, dw_ref, db_ref):
            n8 = nxt[...]
            dh1 = _shift_up(dh, n8, 1)
            dh2 = _shift_up(dh, n8, 2)
            nxt[...] = dh[0:SUBLANES]
            du_ref[:, col:col + cw_blk] = (w_ref[2:3, :] * dh + w_ref[1:2, :] * dh1 + w_ref[0:1, :] * dh2).astype(BF)
            u = u_ref[...].astype(F32)
            dw_ref[2:3, :] += _colsum(dh * u)
            dw_ref[1:2, :] += _colsum(dh1 * u)
            dw_ref[0:1, :] += _colsum(dh2 * u)
            db_ref[...] += _colsum(dh)

        conv_bwd(df * vd_ref[...].astype(F32), ua_ref, wa_ref, na, 0, dwa_ref, dba_ref)
        conv_bwd(df * ga_ref[...].astype(F32), uv_ref, wv_ref, nv, cw_blk, dwv_ref, dbv_ref)

    rev = lambda cbk, i: (nrow - 1 - i, cbk)
    rev_v = lambda cbk, i: (nrow - 1 - i, ncb + cbk)
    blk = pl.BlockSpec((tm, cw_blk), rev)
    w3a = pl.BlockSpec((3, cw_blk), lambda cbk, i: (0, cbk))
    w3v = pl.BlockSpec((3, cw_blk), lambda cbk, i: (0, ncb + cbk))
    b1a = pl.BlockSpec((1, cw_blk), lambda cbk, i: (0, cbk))
    return _pcall(body, name="ffn_bwd", grid=(ncb, nrow),
                  in_specs=[pl.BlockSpec((tm, D), lambda cbk, i: (nrow - 1 - i, 0)),
                            pl.BlockSpec((cw_blk, D), lambda cbk, i: (cbk, 0)),
                            blk, pl.BlockSpec((tm, cw_blk), rev_v), blk, blk, w3a, w3v],
                  out_specs=[pl.BlockSpec((tm, 2 * cw_blk), rev), w3a, w3a, b1a, b1a],
                  out_shape=[jax.ShapeDtypeStruct((T, 2 * dff), BF),
                             jax.ShapeDtypeStruct((3, dff), F32), jax.ShapeDtypeStruct((3, dff), F32),
                             jax.ShapeDtypeStruct((1, dff), F32), jax.ShapeDtypeStruct((1, dff), F32)],
                  scratch=[pltpu.VMEM((SUBLANES, cw_blk), F32)] * 2,
                  vmem_mb=48)(dfo, wd, up, up, ga, vd, cw, cw)


def _ffn_bwd_fused(dfo, wd, up, ga, vd, cw, w_up4, x2, resid, g, scale, gate, o, tm=256, chunk=768):
    T = up.shape[0]
    tm = min(tm, T)
    dff = wd.shape[0]
    ns = w_up4.shape[2]
    nchunk = dff // chunk
    per = ns // chunk
    nrow = T // tm

    def body(dfo_ref, wd_ref, up_ref, ga_ref, vd_ref, cw_ref, wu_ref, x_ref, r_ref, g_ref, sc_ref, gt_ref, o_ref,
             du_ref, dcw_ref, dcb_ref, dx_ref, dsh_ref, dsc_ref, dg_ref, do_ref, dgt_ref, nxt):
        i = pl.program_id(0)

        @pl.when(i == 0)
        def _():
            nxt[...] = jnp.zeros_like(nxt)
            for ref in (dcw_ref, dcb_ref, dsh_ref, dsc_ref, dg_ref, dgt_ref):
                ref[...] = jnp.zeros_like(ref)

        dfo_t = dfo_ref[...]

        def conv_bwd(dh, col):
            cs = slice(col, col + chunk)
            n8 = nxt[:, cs]
            dh1 = _shift_up(dh, n8, 1)
            dh2 = _shift_up(dh, n8, 2)
            nxt[:, cs] = dh[0:SUBLANES]
            du = (cw_ref[2:3, cs] * dh + cw_ref[1:2, cs] * dh1 + cw_ref[0:1, cs] * dh2).astype(BF)
            du_ref[:, cs] = du
            u = up_ref[:, cs].astype(F32)
            dcw_ref[2:3, cs] += _colsum(dh * u)
            dcw_ref[1:2, cs] += _colsum(dh1 * u)
            dcw_ref[0:1, cs] += _colsum(dh2 * u)
            dcb_ref[:, cs] += _colsum(dh)
            return du

        def down_bwd(k):
            return _dot_nt(dfo_t, wd_ref[k * chunk:(k + 1) * chunk, :])

        dh = None
        df_next = down_bwd(0)
        for k in range(nchunk):
            col = k * chunk
            off = (k % per) * chunk
            df = df_next
            if k + 1 < nchunk:
                df_next = down_bwd(k + 1)
            du_a = conv_bwd(df * vd_ref[:, col:col + chunk].astype(F32), col)
            du_v = conv_bwd(df * ga_ref[:, col:col + chunk].astype(F32), dff + col)
            part = (_dot_nt(du_a, wu_ref[k // per, :, off:off + chunk])
                    + _dot_nt(du_v, wu_ref[N_CHIPS // 2 + k // per, :, off:off + chunk]))
            dh = part if dh is None else dh + part

        rstd, xh = _rms_stats(x_ref[...])
        dsh_ref[...] += _colsum(dh)
        dsc_ref[...] += _colsum(dh * (xh * g_ref[...]))
        dn = dh * (1.0 + sc_ref[...])
        dg_ref[...] += _colsum(dn * xh)
        dxh = dn * g_ref[...]
        dx = r_ref[...] + rstd * (dxh - xh * jnp.mean(dxh * xh, axis=-1, keepdims=True))
        dx_ref[...] = dx
        do_ref[...] = (gt_ref[...] * dx).astype(BF)
        dgt_ref[...] += _colsum(dx * o_ref[...].astype(F32))

    rev = lambda i: (nrow - 1 - i, 0)
    row = pl.BlockSpec((tm, D), rev)
    vec = pl.BlockSpec((1, D), lambda i: (0, 0))
    wide = pl.BlockSpec((tm, 2 * dff), rev)
    half = pl.BlockSpec((tm, dff), rev)
    cw3 = pl.BlockSpec((3, 2 * dff), lambda i: (0, 0))
    cb1 = pl.BlockSpec((1, 2 * dff), lambda i: (0, 0))
    vshape = jax.ShapeDtypeStruct((1, D), F32)
    return _pcall(body, name="ffn_bwd", grid=(nrow,),
                  in_specs=[row, _resident(wd.shape), wide, half, half, _resident(cw.shape), _resident(w_up4.shape),
                            row, row, vec, vec, vec, row],
                  out_specs=[wide, cw3, cb1, row, vec, vec, vec, row, vec],
                  out_shape=[jax.ShapeDtypeStruct((T, 2 * dff), BF), jax.ShapeDtypeStruct((3, 2 * dff), F32),
                             jax.ShapeDtypeStruct((1, 2 * dff), F32), jax.ShapeDtypeStruct((T, D), F32),
                             vshape, vshape, vshape, jax.ShapeDtypeStruct((T, D), BF), vshape],
                  scratch=[pltpu.VMEM((SUBLANES, 2 * dff), F32)], vmem_mb=60)(
        dfo, wd, up, ga, vd, cw, w_up4, x2, resid, g, scale, gate, o)


def _ffn_col_block(t, ncb):
    return jnp.where(t < ncb, 2 * t, 2 * (t - ncb) + 1)


def _mm_tn_cols(a, b, name, nshard, nb, colmap=None, mb=None, tm=1024):
    T, M = a.shape
    tm = min(tm, T)
    mb = M if mb is None else mb
    ns = b.shape[1] // nshard
    per = ns // nb
    cmap = colmap if colmap is not None else (lambda t: t)

    def body(a_ref, b_ref, o_ref):
        k = pl.program_id(2)

        @pl.when(k == 0)
        def _():
            o_ref[...] = jnp.zeros_like(o_ref)

        o_ref[0] += _dot_tn(a_ref[...], b_ref[...])

    return _pcall(body, name=name, grid=(M // mb, nshard * per, T // tm),
                  in_specs=[pl.BlockSpec((tm, mb), lambda m, t, k: (k, m)),
                            pl.BlockSpec((tm, nb), lambda m, t, k: (k, cmap(t)))],
                  out_specs=pl.BlockSpec((1, mb, nb), lambda m, t, k: (t // per, m, t % per)),
                  out_shape=jax.ShapeDtypeStruct((nshard, M, ns), F32), vmem_mb=48)(a, b)


def _mm_nt_normbwd(dz, w4, x, resid, g, scale, name, gate=None, o=None, dz_blocks=(0, 1, 2, 3), tm=256):
    T = x.shape[0]
    tm = min(tm, T)
    ns = w4.shape[2]
    gated = gate is not None

    def body(*refs):
        if gated:
            (dz_ref, w_ref, x_ref, r_ref, g_ref, sc_ref, gt_ref, o_ref,
             dx_ref, dsh_ref, dsc_ref, dg_ref, do_ref, dgt_ref) = refs
        else:
            dz_ref, w_ref, x_ref, r_ref, g_ref, sc_ref, dx_ref, dsh_ref, dsc_ref, dg_ref = refs
        i = pl.program_id(0)

        @pl.when(i == 0)
        def _():
            dsh_ref[...] = jnp.zeros_like(dsh_ref)
            dsc_ref[...] = jnp.zeros_like(dsc_ref)
            dg_ref[...] = jnp.zeros_like(dg_ref)
            if gated:
                dgt_ref[...] = jnp.zeros_like(dgt_ref)

        dh = None
        for j in range(N_CHIPS):
            blk = dz_blocks[j]
            part = _dot_nt(dz_ref[:, blk * ns:(blk + 1) * ns], w_ref[j])
            dh = part if dh is None else dh + part
        rstd, xh = _rms_stats(x_ref[...])
        dsh_ref[...] += _colsum(dh)
        dsc_ref[...] += _colsum(dh * (xh * g_ref[...]))
        dn = dh * (1.0 + sc_ref[...])
        dg_ref[...] += _colsum(dn * xh)
        dxh = dn * g_ref[...]
        dx = r_ref[...] + rstd * (dxh - xh * jnp.mean(dxh * xh, axis=-1, keepdims=True))
        dx_ref[...] = dx
        if gated:
            do_ref[...] = (gt_ref[...] * dx).astype(BF)
            dgt_ref[...] += _colsum(dx * o_ref[...].astype(F32))

    row = pl.BlockSpec((tm, D), lambda i: (i, 0))
    vec = pl.BlockSpec((1, D), lambda i: (0, 0))
    in_specs = [pl.BlockSpec((tm, N_CHIPS * ns), lambda i: (i, 0)), _resident(w4.shape), row, row, vec, vec]
    out_specs = [row, vec, vec, vec]
    out_shape = [jax.ShapeDtypeStruct((T, D), F32)] + [jax.ShapeDtypeStruct((1, D), F32)] * 3
    args = [dz, w4, x, resid, g, scale]
    if gated:
        in_specs += [vec, row]
        out_specs += [row, vec]
        out_shape += [jax.ShapeDtypeStruct((T, D), BF), jax.ShapeDtypeStruct((1, D), F32)]
        args += [gate, o]
    return _pcall(body, name=name, grid=(T // tm,), in_specs=in_specs, out_specs=out_specs, out_shape=out_shape,
                  vmem_mb=48)(*args)


def _mix_bwd(do, ya, yb, z, wo, wba, wbb, tm=256):
    T = do.shape[0]
    tm = min(tm, T)

    def body(do_ref, ya_ref, yb_ref, ga_ref, gb_ref, wo_ref, wa_ref, wb_ref,
             dz_ref, dya_ref, dyb_ref, dyap_ref, dybp_ref):
        dm = _dot_nt(do_ref[...], wo_ref[...])
        sa = _sigmoid_t(ga_ref[...])
        sb = _sigmoid_t(gb_ref[...])
        dya = (sa * dm).astype(BF)
        dyb = (sb * dm).astype(BF)
        dz_ref[:, 0:D] = (dm * ya_ref[...].astype(F32) * sa * (1.0 - sa)).astype(BF)
        dz_ref[:, D:2 * D] = (dm * yb_ref[...].astype(F32) * sb * (1.0 - sb)).astype(BF)
        dya_ref[...] = dya
        dyb_ref[...] = dyb
        dyap_ref[...] = _dot_nt(dya, wa_ref[...]).astype(BF)
        dybp_ref[...] = _dot_nt(dyb, wb_ref[...]).astype(BF)

    row = pl.BlockSpec((tm, D), lambda i: (i, 0))
    wspec = pl.BlockSpec((D, D), lambda i: (0, 0))
    return _pcall(body, name="mix_bwd", grid=(T // tm,),
                  in_specs=[row, row, row, pl.BlockSpec((tm, D), lambda i: (i, 4)),
                            pl.BlockSpec((tm, D), lambda i: (i, 5)), wspec, wspec, wspec],
                  out_specs=[pl.BlockSpec((tm, 2 * D), lambda i: (i, 2)), row, row, row, row],
                  out_shape=[jax.ShapeDtypeStruct((T, 6 * D), BF)] + [jax.ShapeDtypeStruct((T, D), BF)] * 4,
                  vmem_mb=48)(do, ya, yb, z, z, wo, wba, wbb)


def _sgu_bwd(dz, dyb_pre, z, lg, lb, ws, bst, tb=256):
    T = z.shape[0]
    tb = min(tb, T)

    def body(dz_in, dy_ref, zu_ref, zv_ref, lg_ref, lb_ref, ws_ref, bst_ref,
             dz_ref, dws_ref, dbst_ref, dlg_ref, dlb_ref):
        del dz_in
        i = pl.program_id(0)

        @pl.when(i == 0)
        def _():
            dws_ref[...] = jnp.zeros_like(dws_ref)
            dbst_ref[...] = jnp.zeros_like(dbst_ref)
            dlg_ref[...] = jnp.zeros_like(dlg_ref)
            dlb_ref[...] = jnp.zeros_like(dlb_ref)

        gu, dgu = _gelu_and_grad(zu_ref[...])
        gv, dgv = _gelu_and_grad(zv_ref[...])
        rstd, xh = _layernorm_stats(gv)
        vln = xh * lg_ref[...] + lb_ref[...]
        wm, mixed = _sgu_mix(vln, ws_ref, bst_ref, tb)
        dy = dy_ref[...].astype(F32)
        dz_ref[:, 0:D] = (dy * mixed * dgu).astype(BF)
        dmixed = dy * gu
        ri = lax.broadcasted_iota(jnp.int32, (SGU_BLOCK, SGU_BLOCK), 0)
        ci = lax.broadcasted_iota(jnp.int32, (SGU_BLOCK, SGU_BLOCK), 1)
        blocks = []
        for blk in range(tb // SGU_BLOCK):
            rs = slice(blk * SGU_BLOCK, (blk + 1) * SGU_BLOCK)
            cols = []
            for g in range(HEADS):
                cs = slice(g * HD, (g + 1) * HD)
                dmg = dmixed[rs, cs]
                dmb = dmg.astype(BF)
                dbst_ref[:, g:g + 1] += jnp.sum(dmg, axis=1, keepdims=True)
                dws_ref[g] += jnp.where(ri >= ci, _dot_nt(dmb, vln[rs, cs].astype(BF)), 0.0)
                cols.append(_dot_tn(wm[g], dmb))
            blocks.append(jnp.concatenate(cols, axis=1))
        dvln = blocks[0] if len(blocks) == 1 else jnp.concatenate(blocks, axis=0)
        dlg_ref[...] += _colsum(dvln * xh)
        dlb_ref[...] += _colsum(dvln)
        dxh = dvln * lg_ref[...]
        dgv_in = rstd * (dxh - jnp.mean(dxh, axis=-1, keepdims=True)
                         - xh * jnp.mean(dxh * xh, axis=-1, keepdims=True))
        dz_ref[:, D:2 * D] = (dgv_in * dgv).astype(BF)

    row = pl.BlockSpec((tb, D), lambda i: (i, 0))
    vec = pl.BlockSpec((1, D), lambda i: (0, 0))
    wspec = pl.BlockSpec((HEADS, SGU_BLOCK, SGU_BLOCK), lambda i: (0, 0, 0))
    bspec = pl.BlockSpec((SGU_BLOCK, HEADS), lambda i: (0, 0))
    return _pcall(body, name="sgu_bwd", grid=(T // tb,),
                  in_specs=[HBM_SPEC, row, pl.BlockSpec((tb, D), lambda i: (i, 2)),
                            pl.BlockSpec((tb, D), lambda i: (i, 3)), vec, vec, wspec, bspec],
                  out_specs=[pl.BlockSpec((tb, 2 * D), lambda i: (i, 1)), wspec, bspec, vec, vec],
                  out_shape=[jax.ShapeDtypeStruct(dz.shape, BF),
                             jax.ShapeDtypeStruct((HEADS, SGU_BLOCK, SGU_BLOCK), F32),
                             jax.ShapeDtypeStruct((SGU_BLOCK, HEADS), F32),
                             jax.ShapeDtypeStruct((1, D), F32), jax.ShapeDtypeStruct((1, D), F32)],
                  aliases={0: 0}, vmem_mb=48)(dz, dyb_pre, z, z, lg, lb, ws, bst)


def _rglru_bwd(dz, dya_pre, z, h, cw, cb, wa, ba, wx, bx, lam, tb=256):
    T = z.shape[0]
    tb = min(tb, T)
    nrow = T // tb
    per = tb // SUBLANES

    def body(dz_in, dy_ref, xr_ref, xh_ref, gr_ref, h_ref, hh_ref, cw_ref, cb_ref, wa_ref, ba_ref, wx_ref, bx_ref,
             lam_ref, dz_ref, dcw_ref, dcb_ref, dwa_ref, dba_ref, dwx_ref, dbx_ref, dlam_ref, carry, nxt):
        del dz_in
        i = pl.program_id(0)
        first_block = i == nrow - 1

        @pl.when(i == 0)
        def _():
            carry[...] = jnp.zeros_like(carry)
            nxt[...] = jnp.zeros_like(nxt)
            for ref in (dcw_ref, dcb_ref, dwa_ref, dba_ref, dwx_ref, dbx_ref, dlam_ref):
                ref[...] = jnp.zeros_like(ref)

        xr = xr_ref[...]
        pv = jnp.where(first_block, 0.0, xh_ref[...])
        s1 = _shift_down(xr, pv, 1)
        s2 = _shift_down(xr, pv, 2)
        s3 = _shift_down(xr, pv, 3)
        xc = cb_ref[...] + cw_ref[3:4, :] * xr + cw_ref[2:3, :] * s1 + cw_ref[1:2, :] * s2 + cw_ref[0:1, :] * s3
        lam = lam_ref[...]
        r, ig, ls, a, mult = _lru_gates(xc, wa_ref, ba_ref[...], wx_ref, bx_ref[...], lam)
        hv = h_ref[...]
        hprev = _shift_down(hv, jnp.where(first_block, 0.0, hh_ref[...]), 1)
        gg, dgg = _gelu_and_grad(gr_ref[...])
        dy = dy_ref[...].astype(F32)
        dz_ref[:, D:2 * D] = (dy * hv * dgg).astype(BF)

        rows = lax.broadcasted_iota(jnp.int32, (tb, D), 0)
        v = dy * gg + jnp.where(rows == tb - 1, carry[0:1, :], 0.0)
        q = jnp.where(rows < tb - 1, pltpu.roll(a, tb - 1, 0), 0.0)
        _, gsc = _scan_rows(q, v, reverse=True)
        carry[...] = (a * gsc)[0:SUBLANES]

        xi = ig * xc
        dmult = gsc * xi
        dxi = gsc * mult
        dig = dxi * xc
        dxc = dxi * ig
        dlog_a = gsc * hprev * a - dmult * (a * a) * pl.reciprocal(mult, approx=True)
        dlam_ref[...] += _colsum(dlog_a * r) * (LRU_C * _sigmoid(-lam))
        dpr = dlog_a * (LRU_C * ls) * r * (1.0 - r)
        dpi = dig * ig * (1.0 - ig)
        dba_ref[...] += _colsum(dpr)
        dbx_ref[...] += _colsum(dpi)
        back = []
        for hh in range(HEADS):
            cs = slice(hh * HD, (hh + 1) * HD)
            xh = xc[:, cs].astype(BF)
            dprh = dpr[:, cs].astype(BF)
            dpih = dpi[:, cs].astype(BF)
            dwa_ref[hh] += _dot_tn(xh, dprh)
            dwx_ref[hh] += _dot_tn(xh, dpih)
            back.append(_dot_nt(dprh, wa_ref[hh].astype(BF)) + _dot_nt(dpih, wx_ref[hh].astype(BF)))
        dxc = dxc + jnp.concatenate(back, axis=1)

        n8 = nxt[...]
        dxr = (cw_ref[3:4, :] * dxc + cw_ref[2:3, :] * _shift_up(dxc, n8, 1)
               + cw_ref[1:2, :] * _shift_up(dxc, n8, 2) + cw_ref[0:1, :] * _shift_up(dxc, n8, 3))
        nxt[...] = dxc[0:SUBLANES]
        dz_ref[:, 0:D] = dxr.astype(BF)
        dcw_ref[3:4, :] += _colsum(dxc * xr)
        dcw_ref[2:3, :] += _colsum(dxc * s1)
        dcw_ref[1:2, :] += _colsum(dxc * s2)
        dcw_ref[0:1, :] += _colsum(dxc * s3)
        dcb_ref[...] += _colsum(dxc)

    rev = lambda col: (lambda i: (nrow - 1 - i, col))
    halo = lambda col: pl.BlockSpec((SUBLANES, D), lambda i: (jnp.maximum((nrow - 1 - i) * per - 1, 0), col))
    vec = pl.BlockSpec((1, D), lambda i: (0, 0))
    wspec = pl.BlockSpec((HEADS, HD, HD), lambda i: (0, 0, 0))
    c4 = pl.BlockSpec((4, D), lambda i: (0, 0))
    wshape = jax.ShapeDtypeStruct((HEADS, HD, HD), F32)
    vshape = jax.ShapeDtypeStruct((1, D), F32)
    return _pcall(body, name="rglru_bwd", grid=(nrow,),
                  in_specs=[HBM_SPEC, pl.BlockSpec((tb, D), rev(0)), pl.BlockSpec((tb, D), rev(0)), halo(0),
                            pl.BlockSpec((tb, D), rev(1)), pl.BlockSpec((tb, D), rev(0)), halo(0),
                            c4, vec, wspec, vec, wspec, vec, vec],
                  out_specs=[pl.BlockSpec((tb, 2 * D), rev(0)), c4, vec, wspec, vec, wspec, vec, vec],
                  out_shape=[jax.ShapeDtypeStruct(dz.shape, BF), jax.ShapeDtypeStruct((4, D), F32), vshape,
                             wshape, vshape, wshape, vshape, vshape],
                  scratch=[pltpu.VMEM((SUBLANES, D), F32), pltpu.VMEM((SUBLANES, D), F32)],
                  aliases={0: 0}, vmem_mb=56)(dz, dya_pre, z, z, z, h, h, cw, cb, wa, ba, wx, bx, lam)


def _pack_rows(parts):
    out = []
    for p in parts:
        q = p.reshape(-1, LANES)
        pad = (-q.shape[0]) % SUBLANES
        if pad:
            q = jnp.concatenate([q, jnp.zeros((pad, LANES), q.dtype)], axis=0)
        out.append(q)
    return jnp.concatenate(out, axis=0)


def _rows_of(shape):
    n = 1
    for s in shape:
        n *= s
    rows = n // LANES
    return rows + (-rows) % SUBLANES


def kernel(x, c, w_ada, b_ada, norm_mix_g, w_in, rnn_conv_w, rnn_conv_b, lru_w_a, lru_b_a, lru_w_x, lru_b_x, lru_lambda, sgu_ln_g, sgu_ln_b, sgu_w_s, sgu_b_s, w_branch_a, w_branch_b, w_out, norm_ffn_g, w_up, ffn_conv_w, ffn_conv_b, w_down, norm_final_g, loss_target, m_w_ada, m_b_ada, m_norm_mix_g, m_w_in, m_rnn_conv_w, m_rnn_conv_b, m_lru_w_a, m_lru_b_a, m_lru_w_x, m_lru_b_x, m_lru_lambda, m_sgu_ln_g, m_sgu_ln_b, m_sgu_w_s, m_sgu_b_s, m_w_branch_a, m_w_branch_b, m_w_out, m_norm_ffn_g, m_w_up, m_ffn_conv_w, m_ffn_conv_b, m_w_down, m_norm_final_g, v_w_ada, v_b_ada, v_norm_mix_g, v_w_in, v_rnn_conv_w, v_rnn_conv_b, v_lru_w_a, v_lru_b_a, v_lru_w_x, v_lru_b_x, v_lru_lambda, v_sgu_ln_g, v_sgu_ln_b, v_sgu_w_s, v_sgu_b_s, v_w_branch_a, v_w_branch_b, v_w_out, v_norm_ffn_g, v_w_up, v_ffn_conv_w, v_ffn_conv_b, v_w_down, v_norm_final_g):
    args = dict(locals())
    T = x.shape[1]
    mx, my, mc = lax.axis_index("x"), lax.axis_index("y"), lax.axis_index("c")
    chip = 2 * mx + my
    dev = 2 * chip + mc
    vec = lambda a: a.reshape(1, -1)

    xt = x.reshape(T, D)
    tgt = loss_target.reshape(T, D)
    ns = w_in.shape[2]
    dff = w_down.shape[1] * N_CHIPS

    c_all = _gather8(c.reshape(SUBLANES, LANES), "gather_c").reshape(N_DEV, D)
    b_ada_sh = lax.dynamic_slice(b_ada, (0, chip * ns), (1, ns))
    mod_sh = _mod_fwd(c_all, w_ada[0], b_ada_sh)

    (w_in_b,) = _cast_shards([w_in[0]], "cast_w_in")
    w_in4, rcw4, fcw4, mod4 = _gather_weights([w_in_b, rnn_conv_w[0], ffn_conv_w[0], mod_sh],
                                              [True, False, False, False])
    late = _cast_shards([w_up[0], w_down[0], w_branch_a[0], w_branch_b[0], w_out[0]], "cast_late", after=mod4)
    late_plan = _gather_half_plan([w.shape for w in late])
    late_handle, late_token = _remote_start(
        late, [lax.empty((N_CHIPS,) + w.shape, w.dtype) for w in late], late_plan, 3 * len(late), "gather_late_start")
    rcw_full = jnp.transpose(rcw4, (1, 0, 2)).reshape(4, D)
    fcw_full = jnp.transpose(fcw4, (1, 0, 2)).reshape(3, 2 * dff)
    mod = lax.dynamic_index_in_dim(mod4, dev, axis=1, keepdims=False).reshape(1, 6 * D)
    shift1, scale1, gate1, shift2, scale2, gate2 = [mod[:, k * D:(k + 1) * D] for k in range(6)]

    h1, z = _modnorm_matmul(xt, norm_mix_g, scale1 + late_token[0:1, 0:1], shift1, w_in4, "norm_in_proj")
    bst = jnp.transpose(sgu_b_s[0])
    h_lru, ya_pre = _rglru_fwd(z, rcw_full, rnn_conv_b, lru_w_a[0], lru_b_a, lru_w_x[0], lru_b_x, lru_lambda)
    yb_pre = _sgu_fwd(z, sgu_ln_g, sgu_ln_b, sgu_w_s[0], bst)
    late, late_lands = _remote_wait(late_handle, late_plan, yb_pre, "gather_late_wait")
    w_up4, w_down4, wba4, wbb4, wo4 = _forward_halves(late, late_lands)
    wd_full = w_down4.reshape(dff, D)
    wba_full = wba4.reshape(D, D)
    wbb_full = wbb4.reshape(D, D)
    wo_full = wo4.reshape(D, D)
    x2, merged, ya, yb, o1 = _mix_out(ya_pre, yb_pre, z, xt, gate1, wba_full, wbb_full, wo_full)
    h2, up, f, ffn_ga, ffn_vd, loss_part, dx3, dfo, dgf, dgate2 = _ffn_fwd(
        x2, norm_ffn_g, scale2, shift2, gate2, vec(norm_final_g), w_up4, wd_full, fcw_full, ffn_conv_b, tgt)

    dup, dfcw, dfcb, dx2, dshift2, dscale2, dg_ffn, do1, dgate1 = _ffn_bwd_fused(
        dfo, wd_full, up, ffn_ga, ffn_vd, fcw_full, w_up4, x2, dx3, norm_ffn_g, scale2, gate1, o1)
    dwd = _mm_tn_cols(f, dfo, "dw_down", 1, D, mb=D)
    dw_up4 = _mm_tn_cols(h2, dup, "dw_up", N_CHIPS, ns)
    dz, dya, dyb, dya_pre, dyb_pre = _mix_bwd(do1, ya, yb, z, wo_full, wba_full, wbb_full)
    dwo = _mm_tn_cols(merged, do1, "dw_out", 1, D)
    dwba = _mm_tn_cols(ya_pre, dya, "dw_branch_a", 1, D)
    dwbb = _mm_tn_cols(yb_pre, dyb, "dw_branch_b", 1, D)

    core = mc.astype(jnp.int32).reshape(1)
    chip_id = chip.astype(jnp.int32).reshape(1)

    def reduce_start(group, name):
        from_core = _send_other_half([g for _, g in group], "swap_halves_" + name)
        sums = [_add_halves_bf16(core, g, o, "sum_cores_" + n) for (n, g), o in zip(group, from_core)]
        lands = [lax.empty((3,) + s.shape[1:], s.dtype) for s in sums]
        return _remote_start(sums, lands, _scatter_plan(len(group)), 3 * len(group), "scatter_start_" + name)

    def reduce_finish(group, handle, after, name):
        sums, landed = _remote_wait(handle, _scatter_plan(len(group)), after, "scatter_wait_" + name)
        return [_sum_own_and_landed(chip_id, s, l, "sum_chips_" + n) for (n, _), s, l in zip(group, sums, landed)]

    group1 = [("w_up", dw_up4), ("w_down", dwd.reshape(N_CHIPS, dff // N_CHIPS, D)),
              ("w_branch_a", dwba.reshape(N_CHIPS, D // N_CHIPS, D)),
              ("w_branch_b", dwbb.reshape(N_CHIPS, D // N_CHIPS, D)), ("w_out", dwo.reshape(N_CHIPS, D // N_CHIPS, D))]
    handle1, token1 = reduce_start(group1, "late")
    dz, dws, dbst, dlg, dlb = _sgu_bwd(dz, dyb_pre, z, sgu_ln_g + token1[0:1, 0:1], sgu_ln_b, sgu_w_s[0], bst)
    dz, drcw, drcb, dwa, dba, dwx, dbx, dlam = _rglru_bwd(
        dz, dya_pre, z, h_lru, rcw_full, rnn_conv_b, lru_w_a[0], lru_b_a, lru_w_x[0], lru_b_x, lru_lambda)
    early_small = [("rnn_conv_b", drcb), ("lru_w_a", dwa), ("lru_b_a", dba), ("lru_w_x", dwx), ("lru_b_x", dbx),
                   ("lru_lambda", dlam), ("sgu_ln_g", dlg), ("sgu_ln_b", dlb), ("sgu_w_s", dws),
                   ("sgu_b_s", jnp.transpose(dbst)), ("norm_ffn_g", dg_ffn),
                   ("ffn_conv_b", dfcb), ("norm_final_g", dgf)]
    r_early = sum(_rows_of(args[n].shape) for n, _ in early_small)
    early_pack = _pack_rows([g for _, g in early_small] + [drcw, dfcw])
    early_pack = jnp.concatenate(
        [early_pack, jnp.zeros(((-early_pack.shape[0]) % 256, LANES), F32)], axis=0)
    early_chip = _add_pair(early_pack, _swap_core(early_pack, "swap_small_grads"), "sum_cores_small_grads")
    early_handle, token3 = _remote_start([early_chip], [lax.empty((3,) + early_chip.shape, F32)], _bcast_plan, 3,
                                         "small_grads_start")
    totals1 = reduce_finish(group1, handle1, drcb, "late")
    group2 = [("w_in", _mm_tn_cols(h1, dz, "dw_in", N_CHIPS, ns))]
    handle2, token2 = reduce_start(group2, "in")
    grad_x, dshift1, dscale1, dg_mix = _mm_nt_normbwd(
        dz, w_in4, xt, dx2, norm_mix_g + (token2[0:1, 0:1] + token3[0:1, 0:1]), scale1, "dh1_norm_bwd")
    totals2 = reduce_finish(group2, handle2, dg_mix, "in")
    dmod = jnp.concatenate([dshift1, dscale1, dgate1, dshift2, dscale2, dgate2], axis=1)

    big = group1 + group2
    fulls = _share_halves(totals1 + totals2)
    out = {}
    for (n, _), full in zip(big, fulls):
        shape = args[n].shape
        res = _adamw(args[n][0], args["m_" + n][0], args["v_" + n][0], [full.reshape(shape[1:])], "adamw_" + n)
        for kind, r in zip(("grad_", "delta_", "new_m_", "new_v_"), res):
            out[kind + n] = r.reshape(shape)

    late_small = [("b_ada", dmod), ("norm_mix_g", dg_mix)]
    small = late_small + early_small
    late_all = _gather8(_pack_rows([g for _, g in late_small]), "gather_late_small_grads")
    late_sum = _sum_parts(late_all, "sum_late_small_grads")
    _, (early_landed,) = _remote_wait(early_handle, _bcast_plan, dg_mix, "small_grads_wait")
    early_sum = _sum_chips_in_order(chip_id, early_chip, early_landed, "sum_early_small_grads")
    r_small = sum(_rows_of(args[n].shape) for n, _ in small)
    r_pad = r_small + (-r_small) % 256
    fill = jnp.zeros((r_pad - r_small, LANES), F32)
    g_small = jnp.concatenate([late_sum, early_sum[:r_early], fill], axis=0)

    def pack_small(prefix):
        return jnp.concatenate([_pack_rows([args[prefix + n] for n, _ in small]), fill], axis=0)

    res = _adamw(pack_small(""), pack_small("m_"), pack_small("v_"), [g_small], "adamw_small")
    off = 0
    for n, _ in small:
        shape = args[n].shape
        rows = _rows_of(shape)
        for kind, r in zip(("grad_", "delta_", "new_m_", "new_v_"), res):
            out[kind + n] = r[off:off + rows].reshape(shape)
        off += rows

    rcw_cols = rnn_conv_w.shape[2]
    g_rcw = lax.dynamic_slice(early_sum[r_early:r_early + 32].reshape(4, D), (0, chip * rcw_cols), (4, rcw_cols))
    g_fcw = lax.dynamic_slice(early_sum[r_early + 32:r_early + 32 + 144].reshape(3, 2 * dff), (0, chip * ns), (3, ns))
    conv = [("rnn_conv_w", g_rcw), ("ffn_conv_w", g_fcw)]
    res = _adamw(_pack_rows([args[n] for n, _ in conv]), _pack_rows([args["m_" + n] for n, _ in conv]),
                 _pack_rows([args["v_" + n] for n, _ in conv]), [_pack_rows([g for _, g in conv])], "adamw_conv")
    off = 0
    for n, _ in conv:
        shape = args[n].shape
        cnt = shape[1] * shape[2] // LANES
        for kind, r in zip(("grad_", "delta_", "new_m_", "new_v_"), res):
            out[kind + n] = r[off:off + cnt].reshape(shape)
        off += _rows_of(shape)

    dmod_all = late_all[:, 0:6 * D // LANES, :].reshape(N_DEV, 6 * D)
    dmod_sh = lax.dynamic_slice(dmod_all, (0, chip * ns), (N_DEV, ns))
    res = _ada_adamw(jnp.transpose(c_all), dmod_sh, w_ada[0], m_w_ada[0], v_w_ada[0])
    for kind, r in zip(("grad_", "delta_", "new_m_", "new_v_"), res):
        out[kind + "w_ada"] = r.reshape(w_ada.shape)

    loss = lax.psum(loss_part[0, 0], ("x", "y", "c"))
    names = ["w_ada", "b_ada", "norm_mix_g", "w_in", "rnn_conv_w", "rnn_conv_b", "lru_w_a", "lru_b_a", "lru_w_x",
             "lru_b_x", "lru_lambda", "sgu_ln_g", "sgu_ln_b", "sgu_w_s", "sgu_b_s", "w_branch_a", "w_branch_b",
             "w_out", "norm_ffn_g", "w_up", "ffn_conv_w", "ffn_conv_b", "w_down", "norm_final_g"]
    result = [loss, grad_x.reshape(x.shape)]
    for kind in ("grad_", "delta_", "new_m_", "new_v_"):
        result += [out[kind + n] for n in names]
    return tuple(result)
```

```python
import functools

import jax
import jax.numpy as jnp
from jax import lax
from jax.experimental import pallas as pl
from jax.experimental.pallas import tpu as pltpu

F32 = jnp.float32
BF = jnp.bfloat16

D = 1024
HEADS = 8
HD = D // HEADS
SGU_BLOCK = 128
N_CHIPS = 4
N_DEV = 8
EPS = 1e-6
LRU_C = 8.0
LANES = 128
SUBLANES = 8

ADAM_LR = 0.001
ADAM_B1 = 0.9
ADAM_B2 = 0.999
ADAM_EPS = 1e-08
ADAM_WD = 0.01
ADAM_STEP = 10

GELU_K0 = 0.7978845608028654
GELU_K1 = 0.044715

HBM_SPEC = pl.BlockSpec(memory_space=pltpu.HBM)
MESH_ID = pl.DeviceIdType.MESH


def _pcall(body, *, name, out_shape, grid=(), in_specs=None, out_specs=None, scratch=(), vmem_mb=32, aliases=None,
           grid_spec=None):
    kw = {}
    if aliases:
        kw["input_output_aliases"] = aliases
    if grid_spec is not None:
        kw["grid_spec"] = grid_spec
        ndim = len(grid_spec.grid)
    else:
        kw.update(grid=grid, in_specs=in_specs, out_specs=out_specs, scratch_shapes=list(scratch))
        ndim = len(grid)
    if ndim:
        params = pltpu.CompilerParams(dimension_semantics=("arbitrary",) * ndim, vmem_limit_bytes=vmem_mb * 2 ** 20)
    else:
        params = pltpu.CompilerParams(vmem_limit_bytes=vmem_mb * 2 ** 20)
    return pl.pallas_call(body, name=name, out_shape=out_shape, compiler_params=params, **kw)


def _gelu_cdf(x, x2):
    return 0.5 * jnp.tanh(x * (GELU_K0 + (GELU_K0 * GELU_K1) * x2)) + 0.5


def _gelu(x):
    return x * _gelu_cdf(x, x * x)


def _gelu_and_grad(x):
    x2 = x * x
    s = _gelu_cdf(x, x2)
    g = x * s
    dg = s * (1.0 + (x - g) * ((2.0 * GELU_K0) + (6.0 * GELU_K0 * GELU_K1) * x2))
    return g, dg


def _sigmoid(x):
    return 1.0 / (1.0 + jnp.exp(-x))


def _sigmoid_t(x):
    return 0.5 * jnp.tanh(0.5 * x) + 0.5


def _log_sigmoid(x):
    e = jnp.exp(-jnp.abs(x))
    u = 1.0 + e
    d = u - 1.0
    l1p = jnp.where(d == 0.0, e, jnp.log(u) * (e / jnp.where(d == 0.0, 1.0, d)))
    return jnp.minimum(x, 0.0) - l1p


def _dot(a, b):
    return jnp.dot(a, b, preferred_element_type=F32)


def _dot_nt(a, b):
    return lax.dot_general(a, b, (((1,), (1,)), ((), ())), preferred_element_type=F32)


def _dot_tn(a, b):
    return lax.dot_general(a, b, (((0,), (0,)), ((), ())), preferred_element_type=F32)


def _shift_down(x, halo, s):
    r = pltpu.roll(x, s, 0)
    rows = lax.broadcasted_iota(jnp.int32, (SUBLANES, x.shape[1]), 0)
    head = jnp.where(rows < s, pltpu.roll(halo, s, 0), r[0:SUBLANES])
    return jnp.concatenate([head, r[SUBLANES:]], axis=0)


def _shift_up(x, halo, s):
    n = x.shape[0]
    r = pltpu.roll(x, n - s, 0)
    rows = lax.broadcasted_iota(jnp.int32, (SUBLANES, x.shape[1]), 0)
    tail = jnp.where(rows >= SUBLANES - s, pltpu.roll(halo, SUBLANES - s, 0), r[n - SUBLANES:n])
    return jnp.concatenate([r[:n - SUBLANES], tail], axis=0)


def _scan_rows(a, u, reverse):
    n, width = a.shape
    rows = lax.broadcasted_iota(jnp.int32, (n, width), 0)
    d = 1
    while d < n:
        if d < SUBLANES:
            keep = rows < n - d if reverse else rows >= d
            shift = n - d if reverse else d
            a_s = jnp.where(keep, pltpu.roll(a, shift, 0), 1.0)
            u_s = jnp.where(keep, pltpu.roll(u, shift, 0), 0.0)
        elif reverse:
            a_s = jnp.concatenate([a[d:], jnp.ones((d, width), a.dtype)], axis=0)
            u_s = jnp.concatenate([u[d:], jnp.zeros((d, width), u.dtype)], axis=0)
        else:
            a_s = jnp.concatenate([jnp.ones((d, width), a.dtype), a[:n - d]], axis=0)
            u_s = jnp.concatenate([jnp.zeros((d, width), u.dtype), u[:n - d]], axis=0)
        u = a * u_s + u
        a = a * a_s
        d *= 2
    return a, u


def _colsum(x):
    return jnp.sum(x, axis=0, keepdims=True)


def _rms_stats(x):
    r = lax.rsqrt(jnp.mean(x * x, axis=-1, keepdims=True) + EPS)
    return r, x * r


def _lru_gates(xc, wa_ref, ba, wx_ref, bx, lam, head0=0):
    pr, pi = [], []
    for hh in range(xc.shape[1] // HD):
        xh = xc[:, hh * HD:(hh + 1) * HD].astype(BF)
        pr.append(_dot(xh, wa_ref[head0 + hh].astype(BF)))
        pi.append(_dot(xh, wx_ref[head0 + hh].astype(BF)))
    r = _sigmoid_t((pr[0] if len(pr) == 1 else jnp.concatenate(pr, axis=1)) + ba)
    ig = _sigmoid_t((pi[0] if len(pi) == 1 else jnp.concatenate(pi, axis=1)) + bx)
    ls = _log_sigmoid(lam)
    log_a = LRU_C * r * ls
    a = jnp.exp(log_a)
    x2 = 2.0 * log_a
    u = a * a
    lu = jnp.log(jnp.maximum(u, 1e-37))
    ratio = x2 * pl.reciprocal(jnp.where(lu == 0.0, 1.0, lu), approx=True)
    em1 = jnp.where(lu == 0.0, x2, jnp.where(u < 1e-30, -1.0, (u - 1.0) * ratio))
    mult = jnp.sqrt(-em1)
    return r, ig, ls, a, mult


def _sgu_mix(vln, ws_ref, bst_ref, tb):
    ri = lax.broadcasted_iota(jnp.int32, (SGU_BLOCK, SGU_BLOCK), 0)
    ci = lax.broadcasted_iota(jnp.int32, (SGU_BLOCK, SGU_BLOCK), 1)
    wm = [jnp.where(ri >= ci, ws_ref[g], 0.0).astype(BF) for g in range(HEADS)]
    blocks = []
    for blk in range(tb // SGU_BLOCK):
        cols = []
        for g in range(HEADS):
            vb = vln[blk * SGU_BLOCK:(blk + 1) * SGU_BLOCK, g * HD:(g + 1) * HD].astype(BF)
            cols.append(_dot(wm[g], vb) + bst_ref[:, g:g + 1])
        blocks.append(jnp.concatenate(cols, axis=1))
    mixed = blocks[0] if len(blocks) == 1 else jnp.concatenate(blocks, axis=0)
    return wm, mixed


def _layernorm_stats(v):
    mu = jnp.mean(v, axis=-1, keepdims=True)
    vc = v - mu
    rstd = lax.rsqrt(jnp.mean(vc * vc, axis=-1, keepdims=True) + EPS)
    return rstd, vc * rstd


def _my_xyc():
    return lax.axis_index("x"), lax.axis_index("y"), lax.axis_index("c")


def _gather_weights(srcs, halve):
    n = len(srcs)
    out_shape = [jax.ShapeDtypeStruct((N_CHIPS,) + s.shape, s.dtype) for s in srcs]

    def body(*refs):
        src, out = refs[:n], refs[n:2 * n]
        send_sems, recv_sems, fwd_send, fwd_recv, loc_sems = refs[2 * n:]
        x, y, c = _my_xyc()
        me = 2 * x + y
        chips = [(1 - x, y), (x, 1 - y), (1 - x, 1 - y)]

        def half(ref, a, which):
            if not halve[a]:
                return ref
            h = srcs[a].shape[0] // 2
            return ref.at[pl.ds(which * h, h)]

        def ici(a, k, frm):
            px, py = chips[k]
            return pltpu.make_async_remote_copy(
                src_ref=half(src[a], a, c), dst_ref=half(out[a].at[frm], a, c),
                send_sem=send_sems.at[a, k], recv_sem=recv_sems.at[a, k],
                device_id=(px, py, c), device_id_type=MESH_ID)

        def d2d(a, k, which):
            px, py = chips[k]
            rows = half(out[a].at[2 * px + py], a, which)
            return pltpu.make_async_remote_copy(
                src_ref=rows, dst_ref=rows, send_sem=fwd_send.at[a, k], recv_sem=fwd_recv.at[a, k],
                device_id=(x, y, 1 - c), device_id_type=MESH_ID)

        local, sends = [], []
        for a in range(n):
            lc = pltpu.make_async_copy(src[a], out[a].at[me], loc_sems.at[a])
            lc.start()
            local.append(lc)
            for k in range(3):
                cp = ici(a, k, me)
                cp.start()
                sends.append(cp)
        for a in range(n):
            for k in range(3):
                px, py = chips[k]
                ici(a, k, 2 * px + py).wait_recv()
                if halve[a]:
                    fw = d2d(a, k, c)
                    fw.start()
                    sends.append(fw)
        for a in range(n):
            if halve[a]:
                for k in range(3):
                    d2d(a, k, 1 - c).wait_recv()
        for cp in sends:
            cp.wait_send()
        for lc in local:
            lc.wait()

    sem = pltpu.SemaphoreType.DMA((n, 3))
    return _pcall(body, name="gather_weights", out_shape=out_shape, in_specs=[HBM_SPEC] * n,
                  out_specs=[HBM_SPEC] * n, scratch=[sem, sem, sem, sem, pltpu.SemaphoreType.DMA((n,))])(*srcs)


SEM_SPEC = pl.BlockSpec(memory_space=pltpu.SEMAPHORE)


def _remote_start(srcs, lands, plan, ncopies, name):
    n, m = len(srcs), len(lands)

    def body(*refs):
        src, land = refs[:n], refs[n:n + m]
        send_sems, recv_sems = refs[n + m], refs[n + m + 1]
        token = refs[-1]
        x, y, c = _my_xyc()
        for i, (s, d, dev) in enumerate(plan(src, land, x, y, c)):
            pltpu.make_async_remote_copy(src_ref=s, dst_ref=d, send_sem=send_sems.at[i], recv_sem=recv_sems.at[i],
                                         device_id=dev, device_id_type=MESH_ID).start()
        token[...] = jnp.zeros_like(token)

    bufs = list(srcs) + list(lands)
    out = pl.pallas_call(
        body, name=name,
        out_shape=(pltpu.SemaphoreType.DMA((ncopies,)), pltpu.SemaphoreType.DMA((ncopies,)),
                   *[pltpu.HBM(b.shape, b.dtype) for b in bufs], jax.ShapeDtypeStruct((SUBLANES, LANES), F32)),
        in_specs=[HBM_SPEC] * (n + m),
        out_specs=(SEM_SPEC, SEM_SPEC, *[HBM_SPEC] * (n + m), pl.BlockSpec(memory_space=pltpu.VMEM)),
        input_output_aliases={i: 2 + i for i in range(n + m)},
        compiler_params=pltpu.CompilerParams(has_side_effects=pltpu.SideEffectType.DATAFLOW_SIDE_EFFECTING),
    )(*[pltpu.with_memory_space_constraint(b, pltpu.HBM) for b in bufs])
    return (out[0], out[1], out[2:2 + n], out[2 + n:2 + n + m]), out[-1]


def _remote_wait(handle, plan, after, name):
    send_sems, recv_sems, srcs, lands = handle
    n, m = len(srcs), len(lands)

    def body(*refs):
        src, land = refs[:n], refs[n:n + m]
        ssem, rsem = refs[n + m], refs[n + m + 1]
        x, y, c = _my_xyc()
        for i, (s, d, dev) in enumerate(plan(src, land, x, y, c)):
            cp = pltpu.make_async_remote_copy(src_ref=s, dst_ref=d, send_sem=ssem.at[i], recv_sem=rsem.at[i],
                                              device_id=dev, device_id_type=MESH_ID)
            cp.wait_send()
            cp.wait_recv()

    bufs = list(srcs) + list(lands)
    out = pl.pallas_call(
        body, name=name, out_shape=tuple(pltpu.HBM(b.shape, b.dtype) for b in bufs),
        in_specs=[HBM_SPEC] * (n + m) + [SEM_SPEC, SEM_SPEC, pl.BlockSpec(memory_space=pl.ANY)],
        out_specs=tuple([HBM_SPEC] * (n + m)), input_output_aliases={i: i for i in range(n + m)},
        compiler_params=pltpu.CompilerParams(has_side_effects=pltpu.SideEffectType.DATAFLOW_SIDE_EFFECTING),
    )(*bufs, send_sems, recv_sems, after)
    return out[:n], out[n:]


def _chips_of(x, y):
    return [(1 - x, y), (x, 1 - y), (1 - x, 1 - y)]


def _gather_half_plan(shapes):
    def plan(src, land, x, y, c):
        me = 2 * x + y
        out = []
        for a, shape in enumerate(shapes):
            h = shape[0] // 2
            rows = pl.ds(c * h, h)
            for px, py in _chips_of(x, y):
                out.append((src[a].at[rows], land[a].at[me, rows], (px, py, c)))
        return out

    return plan


def _forward_halves(srcs, lands):
    n = len(srcs)

    def body(*refs):
        src, land = refs[:n], refs[2 * n:3 * n]
        send_sems, recv_sems, loc_sems = refs[3 * n:]
        x, y, c = _my_xyc()
        me = 2 * x + y

        def fwd(a, k, which):
            px, py = _chips_of(x, y)[k]
            h = srcs[a].shape[0] // 2
            rows = land[a].at[2 * px + py, pl.ds(which * h, h)]
            return pltpu.make_async_remote_copy(
                src_ref=rows, dst_ref=rows, send_sem=send_sems.at[a, k], recv_sem=recv_sems.at[a, k],
                device_id=(x, y, 1 - c), device_id_type=MESH_ID)

        local, sends = [], []
        for a in range(n):
            lc = pltpu.make_async_copy(src[a], land[a].at[me], loc_sems.at[a])
            lc.start()
            local.append(lc)
            for k in range(3):
                cp = fwd(a, k, c)
                cp.start()
                sends.append(cp)
        for a in range(n):
            for k in range(3):
                fwd(a, k, 1 - c).wait_recv()
        for cp in sends:
            cp.wait_send()
        for lc in local:
            lc.wait()

    sem = pltpu.SemaphoreType.DMA((n, 3))
    return _pcall(body, name="forward_halves", out_shape=[jax.ShapeDtypeStruct(l.shape, l.dtype) for l in lands],
                  in_specs=[HBM_SPEC] * (2 * n), out_specs=[HBM_SPEC] * n, aliases={n + a: a for a in range(n)},
                  scratch=[sem, sem, pltpu.SemaphoreType.DMA((n,))])(*srcs, *lands)


def _send_other_half(grads, name):
    n = len(grads)
    out_shape = [jax.ShapeDtypeStruct((N_CHIPS, g.shape[1] // 2, g.shape[2]), g.dtype) for g in grads]

    def body(*refs):
        src, out = refs[:n], refs[n:2 * n]
        send_sems, recv_sems = refs[2 * n:]
        x, y, c = _my_xyc()
        cps = []
        for a in range(n):
            h = grads[a].shape[1] // 2
            cp = pltpu.make_async_remote_copy(
                src_ref=src[a].at[:, pl.ds((1 - c) * h, h), :], dst_ref=out[a],
                send_sem=send_sems.at[a], recv_sem=recv_sems.at[a],
                device_id=(x, y, 1 - c), device_id_type=MESH_ID)
            cp.start()
            cps.append(cp)
        for cp in cps:
            cp.wait()

    return _pcall(body, name=name, out_shape=out_shape, in_specs=[HBM_SPEC] * n,
                  out_specs=[HBM_SPEC] * n,
                  scratch=[pltpu.SemaphoreType.DMA((n,)), pltpu.SemaphoreType.DMA((n,))])(*grads)


def _share_halves(totals):
    n = len(totals)
    out_shape = [jax.ShapeDtypeStruct((2,) + t.shape, t.dtype) for t in totals]

    def body(*refs):
        src, out = refs[:n], refs[n:2 * n]
        send_sems, recv_sems, loc_sems = refs[2 * n:]
        x, y, c = _my_xyc()
        cps, local = [], []
        for a in range(n):
            lc = pltpu.make_async_copy(src[a], out[a].at[c], loc_sems.at[a])
            lc.start()
            local.append(lc)
            cp = pltpu.make_async_remote_copy(
                src_ref=src[a], dst_ref=out[a].at[c], send_sem=send_sems.at[a], recv_sem=recv_sems.at[a],
                device_id=(x, y, 1 - c), device_id_type=MESH_ID)
            cp.start()
            cps.append(cp)
        for cp in cps:
            cp.wait()
        for lc in local:
            lc.wait()

    sem = pltpu.SemaphoreType.DMA((n,))
    return _pcall(body, name="share_halves", out_shape=out_shape, in_specs=[HBM_SPEC] * n,
                  out_specs=[HBM_SPEC] * n, scratch=[sem, sem, sem])(*totals)


def _gather8(src, name):
    def body(src_ref, out_ref, send_sems, recv_sems, loc_sem):
        x, y, c = _my_xyc()
        me = 4 * x + 2 * y + c
        lc = pltpu.make_async_copy(src_ref, out_ref.at[me], loc_sem)
        lc.start()
        cps = []
        for k in range(1, N_DEV):
            px = 1 - x if (k >> 2) & 1 else x
            py = 1 - y if (k >> 1) & 1 else y
            pc = 1 - c if k & 1 else c
            cp = pltpu.make_async_remote_copy(
                src_ref=src_ref, dst_ref=out_ref.at[me], send_sem=send_sems.at[k - 1], recv_sem=recv_sems.at[k - 1],
                device_id=(px, py, pc), device_id_type=MESH_ID)
            cp.start()
            cps.append(cp)
        for cp in cps:
            cp.wait()
        lc.wait()

    return _pcall(body, name=name, out_shape=jax.ShapeDtypeStruct((N_DEV,) + src.shape, src.dtype),
                  in_specs=[HBM_SPEC], out_specs=HBM_SPEC,
                  scratch=[pltpu.SemaphoreType.DMA((N_DEV - 1,)), pltpu.SemaphoreType.DMA((N_DEV - 1,)),
                           pltpu.SemaphoreType.DMA])(src)


def _cast_shards(arrs, name, after=None):
    n = len(arrs)
    extra = [] if after is None else [after]

    def body(*refs):
        ins, outs = refs[:n], refs[n + len(extra):]
        for a in range(n):
            outs[a][...] = ins[a][...].astype(BF)

    specs = [pl.BlockSpec((s.shape[0] // 4, s.shape[1]), lambda i: (i, 0)) for s in arrs]
    return _pcall(body, name=name, grid=(4,), in_specs=specs + [pl.BlockSpec(memory_space=pl.ANY)] * len(extra),
                  out_specs=specs, out_shape=[jax.ShapeDtypeStruct(s.shape, BF) for s in arrs])(*arrs, *extra)


def _row_tile(rows, cols):
    t = rows
    while t * cols * 4 > (3 << 19) and t % 16 == 0:
        t //= 2
    return t


def _sum_parts(parts, name):
    p, rows, cols = parts.shape
    tr = _row_tile(rows, cols * p // 2)

    def body(p_ref, o_ref):
        acc = p_ref[0].astype(F32)
        for k in range(1, p):
            acc = acc + p_ref[k].astype(F32)
        o_ref[...] = acc

    return _pcall(body, name=name, grid=(rows // tr,),
                  in_specs=[pl.BlockSpec((p, tr, cols), lambda i: (0, i, 0))],
                  out_specs=pl.BlockSpec((tr, cols), lambda i: (i, 0)),
                  out_shape=jax.ShapeDtypeStruct((rows, cols), F32), vmem_mb=48)(parts)


def _add_halves_bf16(core, grad, other, name):
    nchip, rows, cols = grad.shape
    h = rows // 2
    tr = _row_tile(h, cols)
    nh = h // tr

    def body(c_ref, g_ref, o_ref, s_ref):
        del c_ref
        s_ref[...] = (g_ref[...] + o_ref[...]).astype(BF)

    grid_spec = pltpu.PrefetchScalarGridSpec(
        num_scalar_prefetch=1, grid=(nchip, nh),
        in_specs=[pl.BlockSpec((1, tr, cols), lambda p, i, c_ref: (p, c_ref[0] * nh + i, 0)),
                  pl.BlockSpec((1, tr, cols), lambda p, i, c_ref: (p, i, 0))],
        out_specs=pl.BlockSpec((1, tr, cols), lambda p, i, c_ref: (p, i, 0)))
    return _pcall(body, name=name, grid_spec=grid_spec, out_shape=jax.ShapeDtypeStruct((nchip, h, cols), BF),
                  vmem_mb=48)(core, grad, other)


def _sum_own_and_landed(chip, sums, landed, name):
    _, rows, cols = sums.shape
    tr = _row_tile(rows, 2 * cols)

    def body(chip_ref, own_ref, land_ref, o_ref):
        del chip_ref
        acc = own_ref[0].astype(F32)
        for k in range(3):
            acc = acc + land_ref[k].astype(F32)
        o_ref[...] = acc

    grid_spec = pltpu.PrefetchScalarGridSpec(
        num_scalar_prefetch=1, grid=(rows // tr,),
        in_specs=[pl.BlockSpec((1, tr, cols), lambda i, chip_ref: (chip_ref[0], i, 0)),
                  pl.BlockSpec((3, tr, cols), lambda i, chip_ref: (0, i, 0))],
        out_specs=pl.BlockSpec((tr, cols), lambda i, chip_ref: (i, 0)))
    return _pcall(body, name=name, grid_spec=grid_spec, out_shape=jax.ShapeDtypeStruct((rows, cols), F32),
                  vmem_mb=48)(chip, sums, landed)


def _swap_core(arr, name):
    def body(src, out, send_sem, recv_sem):
        x, y, c = _my_xyc()
        cp = pltpu.make_async_remote_copy(src_ref=src, dst_ref=out, send_sem=send_sem, recv_sem=recv_sem,
                                          device_id=(x, y, 1 - c), device_id_type=MESH_ID)
        cp.start()
        cp.wait()

    return _pcall(body, name=name, out_shape=jax.ShapeDtypeStruct(arr.shape, arr.dtype), in_specs=[HBM_SPEC],
                  out_specs=HBM_SPEC, scratch=[pltpu.SemaphoreType.DMA, pltpu.SemaphoreType.DMA])(arr)


def _add_pair(a, b, name):
    rows, cols = a.shape
    tr = _row_tile(rows, 2 * cols)

    def body(a_ref, b_ref, o_ref):
        o_ref[...] = a_ref[...] + b_ref[...]

    spec = pl.BlockSpec((tr, cols), lambda i: (i, 0))
    return _pcall(body, name=name, grid=(rows // tr,), in_specs=[spec, spec], out_specs=spec,
                  out_shape=jax.ShapeDtypeStruct((rows, cols), F32))(a, b)


def _sum_chips_in_order(chip, own, landed, name):
    rows, cols = own.shape
    tr = _row_tile(rows, 4 * cols)

    def body(chip_ref, own_ref, land_ref, o_ref):
        me = chip_ref[0]
        acc = None
        for p in range(N_CHIPS):
            q = p ^ me
            k = jnp.where(q == 2, 0, jnp.where(q == 1, 1, 2))
            term = jnp.where(q == 0, own_ref[...], land_ref[k])
            acc = term if acc is None else acc + term
        o_ref[...] = acc

    grid_spec = pltpu.PrefetchScalarGridSpec(
        num_scalar_prefetch=1, grid=(rows // tr,),
        in_specs=[pl.BlockSpec((tr, cols), lambda i, chip_ref: (i, 0)),
                  pl.BlockSpec((3, tr, cols), lambda i, chip_ref: (0, i, 0))],
        out_specs=pl.BlockSpec((tr, cols), lambda i, chip_ref: (i, 0)))
    return _pcall(body, name=name, grid_spec=grid_spec, out_shape=jax.ShapeDtypeStruct((rows, cols), F32))(
        chip, own, landed)


def _bcast_plan(src, land, x, y, c):
    return [(src[0], land[0].at[k], (px, py, c)) for k, (px, py) in enumerate(_chips_of(x, y))]


def _scatter_plan(count):
    def plan(src, land, x, y, c):
        out = []
        for a in range(count):
            for k, (px, py) in enumerate(_chips_of(x, y)):
                out.append((src[a].at[2 * px + py], land[a].at[k], (px, py, c)))
        return out

    return plan


def _adamw_math(w, g, m, v):
    m2 = ADAM_B1 * m + (1.0 - ADAM_B1) * g
    v2 = ADAM_B2 * v + (1.0 - ADAM_B2) * (g * g)
    m_hat = m2 / (1.0 - ADAM_B1 ** ADAM_STEP)
    v_hat = v2 / (1.0 - ADAM_B2 ** ADAM_STEP)
    delta = -ADAM_LR * (m_hat / (jnp.sqrt(v_hat) + ADAM_EPS) + ADAM_WD * w)
    return delta, m2, v2


def _adamw(w, m, v, grads, name):
    rows, cols = w.shape
    tr = _row_tile(rows, cols)
    ng = len(grads)

    def body(*refs):
        w_ref, m_ref, v_ref = refs[:3]
        g = refs[3][...]
        for k in range(1, ng):
            g = g + refs[3 + k][...]
        g_ref, d_ref, m2_ref, v2_ref = refs[3 + ng:]
        delta, m2, v2 = _adamw_math(w_ref[...], g, m_ref[...], v_ref[...])
        g_ref[...] = g
        d_ref[...] = delta
        m2_ref[...] = m2
        v2_ref[...] = v2

    spec = pl.BlockSpec((tr, cols), lambda i: (i, 0))
    return _pcall(body, name=name, grid=(rows // tr,), in_specs=[spec] * (3 + ng), out_specs=[spec] * 4,
                  out_shape=[jax.ShapeDtypeStruct((rows, cols), F32)] * 4, vmem_mb=48)(w, m, v, *grads)


def _ada_adamw(ct, dmod, w, m, v):
    rows, cols = w.shape
    tr = _row_tile(rows, cols)

    def body(ct_ref, dm_ref, w_ref, m_ref, v_ref, g_ref, d_ref, m2_ref, v2_ref):
        cv = ct_ref[...]
        ca = cv * _sigmoid(cv)
        g = ca[:, 0:1] * dm_ref[0:1, :]
        for b in range(1, N_DEV):
            g = g + ca[:, b:b + 1] * dm_ref[b:b + 1, :]
        delta, m2, v2 = _adamw_math(w_ref[...], g, m_ref[...], v_ref[...])
        g_ref[...] = g
        d_ref[...] = delta
        m2_ref[...] = m2
        v2_ref[...] = v2

    spec = pl.BlockSpec((tr, cols), lambda i: (i, 0))
    return _pcall(body, name="ada_adamw", grid=(rows // tr,),
                  in_specs=[pl.BlockSpec((tr, N_DEV), lambda i: (i, 0)), pl.BlockSpec((N_DEV, cols), lambda i: (0, 0)),
                            spec, spec, spec],
                  out_specs=[spec] * 4, out_shape=[jax.ShapeDtypeStruct((rows, cols), F32)] * 4,
                  vmem_mb=48)(ct, dmod, w, m, v)


def _mod_fwd(c_all, w, b):
    cols = w.shape[1]
    tn = cols // 3

    def body(c_ref, w_ref, b_ref, o_ref):
        cv = c_ref[...]
        ca = (cv * _sigmoid(cv)).astype(BF)
        o_ref[...] = _dot(ca, w_ref[...].astype(BF)) + b_ref[...]

    return _pcall(body, name="mod_fwd", grid=(3,),
                  in_specs=[pl.BlockSpec((N_DEV, D), lambda j: (0, 0)), pl.BlockSpec((D, tn), lambda j: (0, j)),
                            pl.BlockSpec((1, tn), lambda j: (0, j))],
                  out_specs=pl.BlockSpec((N_DEV, tn), lambda j: (0, j)),
                  out_shape=jax.ShapeDtypeStruct((N_DEV, cols), F32))(c_all, w, b)


def _resident(shape):
    zeros = (0,) * len(shape)
    return pl.BlockSpec(shape, lambda *_: zeros, pipeline_mode=pl.Buffered(1))


def _modnorm_matmul(x, g, scale, shift, w4, name, tm=256):
    T = x.shape[0]
    tm = min(tm, T)
    ns = w4.shape[2]

    def body(x_ref, g_ref, sc_ref, sh_ref, w_ref, h_ref, z_ref):
        _, xh = _rms_stats(x_ref[...])
        h = ((xh * g_ref[...]) * (1.0 + sc_ref[...]) + sh_ref[...]).astype(BF)
        h_ref[...] = h
        for j in range(N_CHIPS):
            z_ref[:, j * ns:(j + 1) * ns] = _dot(h, w_ref[j])

    vec = pl.BlockSpec((1, D), lambda i: (0, 0))
    return _pcall(body, name=name, grid=(T // tm,),
                  in_specs=[pl.BlockSpec((tm, D), lambda i: (i, 0)), vec, vec, vec, _resident(w4.shape)],
                  out_specs=[pl.BlockSpec((tm, D), lambda i: (i, 0)), pl.BlockSpec((tm, N_CHIPS * ns), lambda i: (i, 0))],
                  out_shape=[jax.ShapeDtypeStruct((T, D), BF), jax.ShapeDtypeStruct((T, N_CHIPS * ns), F32)],
                  vmem_mb=48)(x, g, scale, shift, w4)


def _rglru_fwd(z, cw, cb, wa, ba, wx, bx, lam, tb=256):
    T = z.shape[0]
    tb = min(tb, T)

    def body(xr_ref, gr_ref, cw_ref, cb_ref, wa_ref, ba_ref, wx_ref, bx_ref, lam_ref, h_ref, ya_ref, prev, hc):
        i = pl.program_id(0)

        @pl.when(i == 0)
        def _():
            prev[...] = jnp.zeros_like(prev)
            hc[...] = jnp.zeros_like(hc)

        xr = xr_ref[...]
        pv = prev[...]
        xc = (cb_ref[...] + cw_ref[3:4, :] * xr + cw_ref[2:3, :] * _shift_down(xr, pv, 1)
              + cw_ref[1:2, :] * _shift_down(xr, pv, 2) + cw_ref[0:1, :] * _shift_down(xr, pv, 3))
        prev[...] = xr[tb - SUBLANES:tb]
        _, ig, _, a, mult = _lru_gates(xc, wa_ref, ba_ref[...], wx_ref, bx_ref[...], lam_ref[...])
        a, u = _scan_rows(a, mult * (ig * xc), reverse=False)
        h = u + a * hc[SUBLANES - 1:SUBLANES, :]
        hc[...] = h[tb - SUBLANES:tb]
        h_ref[...] = h
        ya_ref[...] = (h * _gelu(gr_ref[...])).astype(BF)

    vec = pl.BlockSpec((1, D), lambda i: (0, 0))
    wspec = pl.BlockSpec((HEADS, HD, HD), lambda i: (0, 0, 0))
    return _pcall(body, name="rglru_fwd", grid=(T // tb,),
                  in_specs=[pl.BlockSpec((tb, D), lambda i: (i, 0)), pl.BlockSpec((tb, D), lambda i: (i, 1)),
                            pl.BlockSpec((4, D), lambda i: (0, 0)), vec, wspec, vec, wspec, vec, vec],
                  out_specs=[pl.BlockSpec((tb, D), lambda i: (i, 0))] * 2,
                  out_shape=[jax.ShapeDtypeStruct((T, D), F32), jax.ShapeDtypeStruct((T, D), BF)],
                  scratch=[pltpu.VMEM((SUBLANES, D), F32), pltpu.VMEM((SUBLANES, D), F32)],
                  vmem_mb=48)(z, z, cw, cb, wa, ba, wx, bx, lam)


def _sgu_fwd(z, lg, lb, ws, bst, tb=256):
    T = z.shape[0]
    tb = min(tb, T)

    def body(zu_ref, zv_ref, lg_ref, lb_ref, ws_ref, bst_ref, yb_ref):
        _, xh = _layernorm_stats(_gelu(zv_ref[...]))
        vln = xh * lg_ref[...] + lb_ref[...]
        _, mixed = _sgu_mix(vln, ws_ref, bst_ref, tb)
        yb_ref[...] = (_gelu(zu_ref[...]) * mixed).astype(BF)

    vec = pl.BlockSpec((1, D), lambda i: (0, 0))
    return _pcall(body, name="sgu_fwd", grid=(T // tb,),
                  in_specs=[pl.BlockSpec((tb, D), lambda i: (i, 2)), pl.BlockSpec((tb, D), lambda i: (i, 3)), vec, vec,
                            pl.BlockSpec((HEADS, SGU_BLOCK, SGU_BLOCK), lambda i: (0, 0, 0)),
                            pl.BlockSpec((SGU_BLOCK, HEADS), lambda i: (0, 0))],
                  out_specs=pl.BlockSpec((tb, D), lambda i: (i, 0)),
                  out_shape=jax.ShapeDtypeStruct((T, D), BF))(z, z, lg, lb, ws, bst)


def _mix_out(ya_pre, yb_pre, z, x, gate1, wba, wbb, wo, tm=256):
    T = x.shape[0]
    tm = min(tm, T)

    def body(yap_ref, ybp_ref, ga_ref, gb_ref, x_ref, g1_ref, wa_ref, wb_ref, wo_ref,
             x2_ref, mg_ref, ya_ref, yb_ref, o_ref):
        ya = _dot(yap_ref[...], wa_ref[...])
        yb = _dot(ybp_ref[...], wb_ref[...])
        merged = (_sigmoid_t(ga_ref[...]) * ya + _sigmoid_t(gb_ref[...]) * yb).astype(BF)
        o = _dot(merged, wo_ref[...])
        x2_ref[...] = x_ref[...] + g1_ref[...] * o
        mg_ref[...] = merged
        ya_ref[...] = ya.astype(BF)
        yb_ref[...] = yb.astype(BF)
        o_ref[...] = o.astype(BF)

    row = pl.BlockSpec((tm, D), lambda i: (i, 0))
    wspec = pl.BlockSpec((D, D), lambda i: (0, 0))
    return _pcall(body, name="mix_out", grid=(T // tm,),
                  in_specs=[row, row, pl.BlockSpec((tm, D), lambda i: (i, 4)), pl.BlockSpec((tm, D), lambda i: (i, 5)),
                            row, pl.BlockSpec((1, D), lambda i: (0, 0)), wspec, wspec, wspec],
                  out_specs=[row] * 5,
                  out_shape=[jax.ShapeDtypeStruct((T, D), F32)] + [jax.ShapeDtypeStruct((T, D), BF)] * 4,
                  vmem_mb=48)(ya_pre, yb_pre, z, z, x, gate1, wba, wbb, wo)


def _mixer_fwd(x, g, scale, shift, gate1, w_in4, cw, cb, wa, ba, wx, bx, lam, lg, lb, ws, bst, wba, wbb, wo,
               tm=256, chunk=256, piece=512):
    T = x.shape[0]
    tm = min(tm, T)
    ns = w_in4.shape[2]
    per = ns // piece

    def body(x_ref, g_ref, sc_ref, sh_ref, g1_ref, w_ref, cw_ref, cb_ref, wa_ref, ba_ref, wx_ref, bx_ref, lam_ref,
             lg_ref, lb_ref, ws_ref, bst_ref, wba_ref, wbb_ref, wo_ref,
             h1_ref, z_ref, hl_ref, yap_ref, ybp_ref, mg_ref, ya_ref, yb_ref, o_ref, x2_ref, prev, hc):
        i = pl.program_id(0)

        @pl.when(i == 0)
        def _():
            prev[...] = jnp.zeros_like(prev)
            hc[...] = jnp.zeros_like(hc)

        xv = x_ref[...]
        _, xh = _rms_stats(xv)
        h = ((xh * g_ref[...]) * (1.0 + sc_ref[...]) + sh_ref[...]).astype(BF)
        h1_ref[...] = h

        def proj(col, width):
            for c0 in range(col, col + width, piece):
                w = min(piece, col + width - c0)
                j, off = c0 // ns, c0 % ns
                z_ref[:, c0:c0 + w] = _dot(h, w_ref[j, :, off:off + w])

        def lru_chunk(c0):
            cs = slice(c0, c0 + chunk)
            xr = z_ref[:, cs]
            pv = prev[:, cs]
            xc = (cb_ref[:, cs] + cw_ref[3:4, cs] * xr + cw_ref[2:3, cs] * _shift_down(xr, pv, 1)
                  + cw_ref[1:2, cs] * _shift_down(xr, pv, 2) + cw_ref[0:1, cs] * _shift_down(xr, pv, 3))
            prev[:, cs] = xr[tm - SUBLANES:tm]
            _, ig, _, a, mult = _lru_gates(xc, wa_ref, ba_ref[:, cs], wx_ref, bx_ref[:, cs], lam_ref[:, cs],
                                           head0=c0 // HD)
            a, u = _scan_rows(a, mult * (ig * xc), reverse=False)
            hv = u + a * hc[SUBLANES - 1:SUBLANES, cs]
            hc[:, cs] = hv[tm - SUBLANES:tm]
            hl_ref[:, cs] = hv
            yap_ref[:, cs] = (hv * _gelu(z_ref[:, D + c0:D + c0 + chunk])).astype(BF)

        proj(0, chunk)
        proj(D, chunk)
        for c0 in range(0, D, chunk):
            if c0 + chunk < D:
                proj(c0 + chunk, chunk)
                proj(D + c0 + chunk, chunk)
            else:
                proj(2 * D, 2 * D)
            lru_chunk(c0)
        proj(4 * D, 2 * D)
        _, xhn = _layernorm_stats(_gelu(z_ref[:, 3 * D:4 * D]))
        vln = xhn * lg_ref[...] + lb_ref[...]
        _, mixed = _sgu_mix(vln, ws_ref, bst_ref, tm)
        ybp = (_gelu(z_ref[:, 2 * D:3 * D]) * mixed).astype(BF)
        ybp_ref[...] = ybp
        ya = _dot(yap_ref[...], wba_ref[...])
        yb = _dot(ybp, wbb_ref[...])
        merged = (_sigmoid_t(z_ref[:, 4 * D:5 * D]) * ya + _sigmoid_t(z_ref[:, 5 * D:6 * D]) * yb).astype(BF)
        o = _dot(merged, wo_ref[...])
        x2_ref[...] = xv + g1_ref[...] * o
        mg_ref[...] = merged
        ya_ref[...] = ya.astype(BF)
        yb_ref[...] = yb.astype(BF)
        o_ref[...] = o.astype(BF)

    row = pl.BlockSpec((tm, D), lambda i: (i, 0))
    vec = pl.BlockSpec((1, D), lambda i: (0, 0))
    bf_row = jax.ShapeDtypeStruct((T, D), BF)
    f32_row = jax.ShapeDtypeStruct((T, D), F32)
    return _pcall(body, name="mixer_fwd", grid=(T // tm,),
                  in_specs=[row, vec, vec, vec, vec, _resident(w_in4.shape), _resident(cw.shape), vec,
                            _resident(wa.shape), vec, _resident(wx.shape), vec, vec, vec, vec,
                            _resident(ws.shape), _resident(bst.shape),
                            _resident(wba.shape), _resident(wbb.shape), _resident(wo.shape)],
                  out_specs=[row, pl.BlockSpec((tm, 6 * D), lambda i: (i, 0)), row, row, row, row, row, row, row, row],
                  out_shape=[bf_row, jax.ShapeDtypeStruct((T, 6 * D), F32), f32_row, bf_row, bf_row, bf_row, bf_row,
                             bf_row, bf_row, f32_row],
                  scratch=[pltpu.VMEM((SUBLANES, D), F32), pltpu.VMEM((SUBLANES, D), F32)], vmem_mb=60)(
        x, g, scale, shift, gate1, w_in4, cw, cb, wa, ba, wx, bx, lam, lg, lb, ws, bst, wba, wbb, wo)


def _ffn_gate(up, cw, cb, tm=512, cw_blk=768):
    T = up.shape[0]
    tm = min(tm, T)
    dff = up.shape[1] // 2
    ncb = dff // cw_blk

    def body(ua_ref, uv_ref, wa_ref, wv_ref, ba_ref, bv_ref, f_ref, ga_ref, vd_ref, pa, pv):
        i = pl.program_id(1)

        @pl.when(i == 0)
        def _():
            pa[...] = jnp.zeros_like(pa)
            pv[...] = jnp.zeros_like(pv)

        def conv(u_ref, w_ref, b_ref, prev):
            u = u_ref[...]
            p = prev[...]
            hid = (b_ref[...] + w_ref[2:3, :] * u + w_ref[1:2, :] * _shift_down(u, p, 1)
                   + w_ref[0:1, :] * _shift_down(u, p, 2))
            prev[...] = u[tm - SUBLANES:tm]
            return hid

        act = conv(ua_ref, wa_ref, ba_ref, pa)
        val = conv(uv_ref, wv_ref, bv_ref, pv)
        ga, dga = _gelu_and_grad(act)
        f_ref[...] = (ga * val).astype(BF)
        ga_ref[...] = ga.astype(BF)
        vd_ref[...] = (val * dga).astype(BF)

    blk = pl.BlockSpec((tm, cw_blk), lambda cbk, i: (i, cbk))
    return _pcall(body, name="ffn_gate", grid=(ncb, T // tm),
                  in_specs=[pl.BlockSpec((tm, cw_blk), lambda cbk, i: (i, cbk)),
                            pl.BlockSpec((tm, cw_blk), lambda cbk, i: (i, ncb + cbk)),
                            pl.BlockSpec((3, cw_blk), lambda cbk, i: (0, cbk)),
                            pl.BlockSpec((3, cw_blk), lambda cbk, i: (0, ncb + cbk)),
                            pl.BlockSpec((1, cw_blk), lambda cbk, i: (0, cbk)),
                            pl.BlockSpec((1, cw_blk), lambda cbk, i: (0, ncb + cbk))],
                  out_specs=[blk] * 3, out_shape=[jax.ShapeDtypeStruct((T, dff), BF)] * 3,
                  scratch=[pltpu.VMEM((SUBLANES, cw_blk), F32)] * 2)(up, up, cw, cw, cb, cb)


def _ffn_down_loss(f, wd, x2, gate2, gf, target, tm=512):
    T = x2.shape[0]
    tm = min(tm, T)
    dff = f.shape[1]

    def body(f_ref, wd_ref, x2_ref, g2_ref, gf_ref, t_ref, loss_ref, dx3_ref, dfo_ref, dgf_ref, dg2_ref):
        i = pl.program_id(0)

        @pl.when(i == 0)
        def _():
            loss_ref[...] = jnp.zeros_like(loss_ref)
            dgf_ref[...] = jnp.zeros_like(dgf_ref)
            dg2_ref[...] = jnp.zeros_like(dg2_ref)

        fo = _dot(f_ref[...], wd_ref[...])
        x3 = x2_ref[...] + g2_ref[...] * fo
        rstd, xh = _rms_stats(x3)
        err = xh * gf_ref[...] - t_ref[...]
        loss_ref[...] += 0.5 * jnp.sum(jnp.mean(err * err, axis=-1, keepdims=True), axis=0, keepdims=True)
        dy = err * (1.0 / D)
        dgf_ref[...] += _colsum(dy * xh)
        dxh = dy * gf_ref[...]
        dx3 = rstd * (dxh - xh * jnp.mean(dxh * xh, axis=-1, keepdims=True))
        dg2_ref[...] += _colsum(dx3 * fo)
        dx3_ref[...] = dx3
        dfo_ref[...] = (g2_ref[...] * dx3).astype(BF)

    row = pl.BlockSpec((tm, D), lambda i: (i, 0))
    vec = pl.BlockSpec((1, D), lambda i: (0, 0))
    return _pcall(body, name="ffn_down_loss", grid=(T // tm,),
                  in_specs=[pl.BlockSpec((tm, dff), lambda i: (i, 0)), pl.BlockSpec((dff, D), lambda i: (0, 0)),
                            row, vec, vec, row],
                  out_specs=[pl.BlockSpec((1, LANES), lambda i: (0, 0)), row, row, vec, vec],
                  out_shape=[jax.ShapeDtypeStruct((1, LANES), F32), jax.ShapeDtypeStruct((T, D), F32),
                             jax.ShapeDtypeStruct((T, D), BF), jax.ShapeDtypeStruct((1, D), F32),
                             jax.ShapeDtypeStruct((1, D), F32)],
                  vmem_mb=48)(f, wd, x2, gate2, gf, target)


def _ffn_fwd(x2, g, scale, shift, gate2, gf, w_up4, wd, cw, cb, target, tm=256, chunk=768):
    T = x2.shape[0]
    tm = min(tm, T)
    ns = w_up4.shape[2]
    dff = wd.shape[0]
    nchunk = dff // chunk
    per = ns // chunk

    def body(x2_ref, g_ref, sc_ref, sh_ref, g2_ref, gf_ref, wu_ref, wd_ref, cw_ref, cb_ref, t_ref,
             h2_ref, up_ref, f_ref, ga_ref, vd_ref, loss_ref, dx3_ref, dfo_ref, dgf_ref, dg2_ref, prev):
        i = pl.program_id(0)

        @pl.when(i == 0)
        def _():
            prev[...] = jnp.zeros_like(prev)
            loss_ref[...] = jnp.zeros_like(loss_ref)
            dgf_ref[...] = jnp.zeros_like(dgf_ref)
            dg2_ref[...] = jnp.zeros_like(dg2_ref)

        x2v = x2_ref[...]
        _, xh2 = _rms_stats(x2v)
        h2 = ((xh2 * g_ref[...]) * (1.0 + sc_ref[...]) + sh_ref[...]).astype(BF)
        h2_ref[...] = h2

        def conv(u, col):
            cs = slice(col, col + chunk)
            p = prev[:, cs]
            hid = (cb_ref[:, cs] + cw_ref[2:3, cs] * u + cw_ref[1:2, cs] * _shift_down(u, p, 1)
                   + cw_ref[0:1, cs] * _shift_down(u, p, 2))
            prev[:, cs] = u[tm - SUBLANES:tm]
            up_ref[:, cs] = u.astype(BF)
            return hid

        def up_proj(k):
            off = (k % per) * chunk
            return (_dot(h2, wu_ref[k // per, :, off:off + chunk]),
                    _dot(h2, wu_ref[N_CHIPS // 2 + k // per, :, off:off + chunk]))

        fo = None
        nxt = up_proj(0)
        for k in range(nchunk):
            col = k * chunk
            ua, uv = nxt
            if k + 1 < nchunk:
                nxt = up_proj(k + 1)
            act = conv(ua, col)
            val = conv(uv, dff + col)
            ga, dga = _gelu_and_grad(act)
            fk = (ga * val).astype(BF)
            f_ref[:, col:col + chunk] = fk
            ga_ref[:, col:col + chunk] = ga.astype(BF)
            vd_ref[:, col:col + chunk] = (val * dga).astype(BF)
            part = _dot(fk, wd_ref[col:col + chunk, :])
            fo = part if fo is None else fo + part

        x3 = x2v + g2_ref[...] * fo
        rstd, xh = _rms_stats(x3)
        err = xh * gf_ref[...] - t_ref[...]
        loss_ref[...] += 0.5 * jnp.sum(jnp.mean(err * err, axis=-1, keepdims=True), axis=0, keepdims=True)
        dy = err * (1.0 / D)
        dgf_ref[...] += _colsum(dy * xh)
        dxh = dy * gf_ref[...]
        dx3 = rstd * (dxh - xh * jnp.mean(dxh * xh, axis=-1, keepdims=True))
        dg2_ref[...] += _colsum(dx3 * fo)
        dx3_ref[...] = dx3
        dfo_ref[...] = (g2_ref[...] * dx3).astype(BF)

    row = pl.BlockSpec((tm, D), lambda i: (i, 0))
    vec = pl.BlockSpec((1, D), lambda i: (0, 0))
    wide = pl.BlockSpec((tm, 2 * dff), lambda i: (i, 0))
    half = pl.BlockSpec((tm, dff), lambda i: (i, 0))
    return _pcall(body, name="ffn_fwd", grid=(T // tm,),
                  in_specs=[row, vec, vec, vec, vec, vec, _resident(w_up4.shape), _resident(wd.shape),
                            _resident(cw.shape), _resident(cb.shape), row],
                  out_specs=[row, wide, half, half, half, pl.BlockSpec((1, LANES), lambda i: (0, 0)), row, row, vec, vec],
                  out_shape=[jax.ShapeDtypeStruct((T, D), BF), jax.ShapeDtypeStruct((T, 2 * dff), BF),
                             jax.ShapeDtypeStruct((T, dff), BF), jax.ShapeDtypeStruct((T, dff), BF),
                             jax.ShapeDtypeStruct((T, dff), BF), jax.ShapeDtypeStruct((1, LANES), F32),
                             jax.ShapeDtypeStruct((T, D), F32), jax.ShapeDtypeStruct((T, D), BF),
                             jax.ShapeDtypeStruct((1, D), F32), jax.ShapeDtypeStruct((1, D), F32)],
                  scratch=[pltpu.VMEM((SUBLANES, 2 * dff), F32)], vmem_mb=56)(
        x2, g, scale, shift, gate2, gf, w_up4, wd, cw, cb, target)


def _ffn_bwd(dfo, wd, up, ga, vd, cw, tm=256, cw_blk=1536):
    T = up.shape[0]
    tm = min(tm, T)
    dff = up.shape[1] // 2
    ncb = dff // cw_blk
    nrow = T // tm

    def body(dfo_ref, wd_ref, ua_ref, uv_ref, ga_ref, vd_ref, wa_ref, wv_ref,
             du_ref, dwa_ref, dwv_ref, dba_ref, dbv_ref, na, nv):
        i = pl.program_id(1)

        @pl.when(i == 0)
        def _():
            na[...] = jnp.zeros_like(na)
            nv[...] = jnp.zeros_like(nv)
            dwa_ref[...] = jnp.zeros_like(dwa_ref)
            dwv_ref[...] = jnp.zeros_like(dwv_ref)
            dba_ref[...] = jnp.zeros_like(dba_ref)
            dbv_ref[...] = jnp.zeros_like(dbv_ref)

        df = _dot_nt(dfo_ref[...], wd_ref[...])

        def conv_bwd(dh, u_ref, w_ref, nxt, col, dw_ref, db_ref):
            n8 = nxt[...]
            dh1 = _shift_up(dh, n8, 1)
            dh2 = _shift_up(dh, n8, 2)
            nxt[...] = dh[0:SUBLANES]
            du_ref[:, col:col + cw_blk] = (w_ref[2:3, :] * dh + w_ref[1:2, :] * dh1 + w_ref[0:1, :] * dh2).astype(BF)
            u = u_ref[...].astype(F32)
            dw_ref[2:3, :] += _colsum(dh * u)
            dw_ref[1:2, :] += _colsum(dh1 * u)
            dw_ref[0:1, :] += _colsum(dh2 * u)
            db_ref[...] += _colsum(dh)

        conv_bwd(df * vd_ref[...].astype(F32), ua_ref, wa_ref, na, 0, dwa_ref, dba_ref)
        conv_bwd(df * ga_ref[...].astype(F32), uv_ref, wv_ref, nv, cw_blk, dwv_ref, dbv_ref)

    rev = lambda cbk, i: (nrow - 1 - i, cbk)
    rev_v = lambda cbk, i: (nrow - 1 - i, ncb + cbk)
    blk = pl.BlockSpec((tm, cw_blk), rev)
    w3a = pl.BlockSpec((3, cw_blk), lambda cbk, i: (0, cbk))
    w3v = pl.BlockSpec((3, cw_blk), lambda cbk, i: (0, ncb + cbk))
    b1a = pl.BlockSpec((1, cw_blk), lambda cbk, i: (0, cbk))
    return _pcall(body, name="ffn_bwd", grid=(ncb, nrow),
                  in_specs=[pl.BlockSpec((tm, D), lambda cbk, i: (nrow - 1 - i, 0)),
                            pl.BlockSpec((cw_blk, D), lambda cbk, i: (cbk, 0)),
                            blk, pl.BlockSpec((tm, cw_blk), rev_v), blk, blk, w3a, w3v],
                  out_specs=[pl.BlockSpec((tm, 2 * cw_blk), rev), w3a, w3a, b1a, b1a],
                  out_shape=[jax.ShapeDtypeStruct((T, 2 * dff), BF),
                             jax.ShapeDtypeStruct((3, dff), F32), jax.ShapeDtypeStruct((3, dff), F32),
                             jax.ShapeDtypeStruct((1, dff), F32), jax.ShapeDtypeStruct((1, dff), F32)],
                  scratch=[pltpu.VMEM((SUBLANES, cw_blk), F32)] * 2,
                  vmem_mb=48)(dfo, wd, up, up, ga, vd, cw, cw)


def _ffn_bwd_fused(dfo, wd, up, ga, vd, cw, w_up4, x2, resid, g, scale, gate, o, tm=256, chunk=768):
    T = up.shape[0]
    tm = min(tm, T)
    dff = wd.shape[0]
    ns = w_up4.shape[2]
    nchunk = dff // chunk
    per = ns // chunk
    nrow = T // tm

    def body(dfo_ref, wd_ref, up_ref, ga_ref, vd_ref, cw_ref, wu_ref, x_ref, r_ref, g_ref, sc_ref, gt_ref, o_ref,
             du_ref, dcw_ref, dcb_ref, dx_ref, dsh_ref, dsc_ref, dg_ref, do_ref, dgt_ref, nxt):
        i = pl.program_id(0)

        @pl.when(i == 0)
        def _():
            nxt[...] = jnp.zeros_like(nxt)
            for ref in (dcw_ref, dcb_ref, dsh_ref, dsc_ref, dg_ref, dgt_ref):
                ref[...] = jnp.zeros_like(ref)

        dfo_t = dfo_ref[...]

        def conv_bwd(dh, col):
            cs = slice(col, col + chunk)
            n8 = nxt[:, cs]
            dh1 = _shift_up(dh, n8, 1)
            dh2 = _shift_up(dh, n8, 2)
            nxt[:, cs] = dh[0:SUBLANES]
            du = (cw_ref[2:3, cs] * dh + cw_ref[1:2, cs] * dh1 + cw_ref[0:1, cs] * dh2).astype(BF)
            du_ref[:, cs] = du
            u = up_ref[:, cs].astype(F32)
            dcw_ref[2:3, cs] += _colsum(dh * u)
            dcw_ref[1:2, cs] += _colsum(dh1 * u)
            dcw_ref[0:1, cs] += _colsum(dh2 * u)
            dcb_ref[:, cs] += _colsum(dh)
            return du

        def down_bwd(k):
            return _dot_nt(dfo_t, wd_ref[k * chunk:(k + 1) * chunk, :])

        dh = None
        df_next = down_bwd(0)
        for k in range(nchunk):
            col = k * chunk
            off = (k % per) * chunk
            df = df_next
            if k + 1 < nchunk:
                df_next = down_bwd(k + 1)
            du_a = conv_bwd(df * vd_ref[:, col:col + chunk].astype(F32), col)
            du_v = conv_bwd(df * ga_ref[:, col:col + chunk].astype(F32), dff + col)
            part = (_dot_nt(du_a, wu_ref[k // per, :, off:off + chunk])
                    + _dot_nt(du_v, wu_ref[N_CHIPS // 2 + k // per, :, off:off + chunk]))
            dh = part if dh is None else dh + part

        rstd, xh = _rms_stats(x_ref[...])
        dsh_ref[...] += _colsum(dh)
        dsc_ref[...] += _colsum(dh * (xh * g_ref[...]))
        dn = dh * (1.0 + sc_ref[...])
        dg_ref[...] += _colsum(dn * xh)
        dxh = dn * g_ref[...]
        dx = r_ref[...] + rstd * (dxh - xh * jnp.mean(dxh * xh, axis=-1, keepdims=True))
        dx_ref[...] = dx
        do_ref[...] = (gt_ref[...] * dx).astype(BF)
        dgt_ref[...] += _colsum(dx * o_ref[...].astype(F32))

    rev = lambda i: (nrow - 1 - i, 0)
    row = pl.BlockSpec((tm, D), rev)
    vec = pl.BlockSpec((1, D), lambda i: (0, 0))
    wide = pl.BlockSpec((tm, 2 * dff), rev)
    half = pl.BlockSpec((tm, dff), rev)
    cw3 = pl.BlockSpec((3, 2 * dff), lambda i: (0, 0))
    cb1 = pl.BlockSpec((1, 2 * dff), lambda i: (0, 0))
    vshape = jax.ShapeDtypeStruct((1, D), F32)
    return _pcall(body, name="ffn_bwd", grid=(nrow,),
                  in_specs=[row, _resident(wd.shape), wide, half, half, _resident(cw.shape), _resident(w_up4.shape),
                            row, row, vec, vec, vec, row],
                  out_specs=[wide, cw3, cb1, row, vec, vec, vec, row, vec],
                  out_shape=[jax.ShapeDtypeStruct((T, 2 * dff), BF), jax.ShapeDtypeStruct((3, 2 * dff), F32),
                             jax.ShapeDtypeStruct((1, 2 * dff), F32), jax.ShapeDtypeStruct((T, D), F32),
                             vshape, vshape, vshape, jax.ShapeDtypeStruct((T, D), BF), vshape],
                  scratch=[pltpu.VMEM((SUBLANES, 2 * dff), F32)], vmem_mb=60)(
        dfo, wd, up, ga, vd, cw, w_up4, x2, resid, g, scale, gate, o)


def _ffn_col_block(t, ncb):
    return jnp.where(t < ncb, 2 * t, 2 * (t - ncb) + 1)


def _mm_tn_cols(a, b, name, nshard, nb, colmap=None, mb=None, tm=1024):
    T, M = a.shape
    tm = min(tm, T)
    mb = M if mb is None else mb
    ns = b.shape[1] // nshard
    per = ns // nb
    cmap = colmap if colmap is not None else (lambda t: t)

    def body(a_ref, b_ref, o_ref):
        k = pl.program_id(2)

        @pl.when(k == 0)
        def _():
            o_ref[...] = jnp.zeros_like(o_ref)

        o_ref[0] += _dot_tn(a_ref[...], b_ref[...])

    return _pcall(body, name=name, grid=(M // mb, nshard * per, T // tm),
                  in_specs=[pl.BlockSpec((tm, mb), lambda m, t, k: (k, m)),
                            pl.BlockSpec((tm, nb), lambda m, t, k: (k, cmap(t)))],
                  out_specs=pl.BlockSpec((1, mb, nb), lambda m, t, k: (t // per, m, t % per)),
                  out_shape=jax.ShapeDtypeStruct((nshard, M, ns), F32), vmem_mb=48)(a, b)


def _mm_nt_normbwd(dz, w4, x, resid, g, scale, name, gate=None, o=None, dz_blocks=(0, 1, 2, 3), tm=256):
    T = x.shape[0]
    tm = min(tm, T)
    ns = w4.shape[2]
    gated = gate is not None

    def body(*refs):
        if gated:
            (dz_ref, w_ref, x_ref, r_ref, g_ref, sc_ref, gt_ref, o_ref,
             dx_ref, dsh_ref, dsc_ref, dg_ref, do_ref, dgt_ref) = refs
        else:
            dz_ref, w_ref, x_ref, r_ref, g_ref, sc_ref, dx_ref, dsh_ref, dsc_ref, dg_ref = refs
        i = pl.program_id(0)

        @pl.when(i == 0)
        def _():
            dsh_ref[...] = jnp.zeros_like(dsh_ref)
            dsc_ref[...] = jnp.zeros_like(dsc_ref)
            dg_ref[...] = jnp.zeros_like(dg_ref)
            if gated:
                dgt_ref[...] = jnp.zeros_like(dgt_ref)

        dh = None
        for j in range(N_CHIPS):
            blk = dz_blocks[j]
            part = _dot_nt(dz_ref[:, blk * ns:(blk + 1) * ns], w_ref[j])
            dh = part if dh is None else dh + part
        rstd, xh = _rms_stats(x_ref[...])
        dsh_ref[...] += _colsum(dh)
        dsc_ref[...] += _colsum(dh * (xh * g_ref[...]))
        dn = dh * (1.0 + sc_ref[...])
        dg_ref[...] += _colsum(dn * xh)
        dxh = dn * g_ref[...]
        dx = r_ref[...] + rstd * (dxh - xh * jnp.mean(dxh * xh, axis=-1, keepdims=True))
        dx_ref[...] = dx
        if gated:
            do_ref[...] = (gt_ref[...] * dx).astype(BF)
            dgt_ref[...] += _colsum(dx * o_ref[...].astype(F32))

    row = pl.BlockSpec((tm, D), lambda i: (i, 0))
    vec = pl.BlockSpec((1, D), lambda i: (0, 0))
    in_specs = [pl.BlockSpec((tm, N_CHIPS * ns), lambda i: (i, 0)), _resident(w4.shape), row, row, vec, vec]
    out_specs = [row, vec, vec, vec]
    out_shape = [jax.ShapeDtypeStruct((T, D), F32)] + [jax.ShapeDtypeStruct((1, D), F32)] * 3
    args = [dz, w4, x, resid, g, scale]
    if gated:
        in_specs += [vec, row]
        out_specs += [row, vec]
        out_shape += [jax.ShapeDtypeStruct((T, D), BF), jax.ShapeDtypeStruct((1, D), F32)]
        args += [gate, o]
    return _pcall(body, name=name, grid=(T // tm,), in_specs=in_specs, out_specs=out_specs, out_shape=out_shape,
                  vmem_mb=48)(*args)


def _mix_bwd(do, ya, yb, z, wo, wba, wbb, tm=256):
    T = do.shape[0]
    tm = min(tm, T)

    def body(do_ref, ya_ref, yb_ref, ga_ref, gb_ref, wo_ref, wa_ref, wb_ref,
             dz_ref, dya_ref, dyb_ref, dyap_ref, dybp_ref):
        dm = _dot_nt(do_ref[...], wo_ref[...])
        sa = _sigmoid_t(ga_ref[...])
        sb = _sigmoid_t(gb_ref[...])
        dya = (sa * dm).astype(BF)
        dyb = (sb * dm).astype(BF)
        dz_ref[:, 0:D] = (dm * ya_ref[...].astype(F32) * sa * (1.0 - sa)).astype(BF)
        dz_ref[:, D:2 * D] = (dm * yb_ref[...].astype(F32) * sb * (1.0 - sb)).astype(BF)
        dya_ref[...] = dya
        dyb_ref[...] = dyb
        dyap_ref[...] = _dot_nt(dya, wa_ref[...]).astype(BF)
        dybp_ref[...] = _dot_nt(dyb, wb_ref[...]).astype(BF)

    row = pl.BlockSpec((tm, D), lambda i: (i, 0))
    wspec = pl.BlockSpec((D, D), lambda i: (0, 0))
    return _pcall(body, name="mix_bwd", grid=(T // tm,),
                  in_specs=[row, row, row, pl.BlockSpec((tm, D), lambda i: (i, 4)),
                            pl.BlockSpec((tm, D), lambda i: (i, 5)), wspec, wspec, wspec],
                  out_specs=[pl.BlockSpec((tm, 2 * D), lambda i: (i, 2)), row, row, row, row],
                  out_shape=[jax.ShapeDtypeStruct((T, 6 * D), BF)] + [jax.ShapeDtypeStruct((T, D), BF)] * 4,
                  vmem_mb=48)(do, ya, yb, z, z, wo, wba, wbb)


def _sgu_bwd(dz, dyb_pre, z, lg, lb, ws, bst, tb=256):
    T = z.shape[0]
    tb = min(tb, T)

    def body(dz_in, dy_ref, zu_ref, zv_ref, lg_ref, lb_ref, ws_ref, bst_ref,
             dz_ref, dws_ref, dbst_ref, dlg_ref, dlb_ref):
        del dz_in
        i = pl.program_id(0)

        @pl.when(i == 0)
        def _():
            dws_ref[...] = jnp.zeros_like(dws_ref)
            dbst_ref[...] = jnp.zeros_like(dbst_ref)
            dlg_ref[...] = jnp.zeros_like(dlg_ref)
            dlb_ref[...] = jnp.zeros_like(dlb_ref)

        gu, dgu = _gelu_and_grad(zu_ref[...])
        gv, dgv = _gelu_and_grad(zv_ref[...])
        rstd, xh = _layernorm_stats(gv)
        vln = xh * lg_ref[...] + lb_ref[...]
        wm, mixed = _sgu_mix(vln, ws_ref, bst_ref, tb)
        dy = dy_ref[...].astype(F32)
        dz_ref[:, 0:D] = (dy * mixed * dgu).astype(BF)
        dmixed = dy * gu
        ri = lax.broadcasted_iota(jnp.int32, (SGU_BLOCK, SGU_BLOCK), 0)
        ci = lax.broadcasted_iota(jnp.int32, (SGU_BLOCK, SGU_BLOCK), 1)
        blocks = []
        for blk in range(tb // SGU_BLOCK):
            rs = slice(blk * SGU_BLOCK, (blk + 1) * SGU_BLOCK)
            cols = []
            for g in range(HEADS):
                cs = slice(g * HD, (g + 1) * HD)
                dmg = dmixed[rs, cs]
                dmb = dmg.astype(BF)
                dbst_ref[:, g:g + 1] += jnp.sum(dmg, axis=1, keepdims=True)
                dws_ref[g] += jnp.where(ri >= ci, _dot_nt(dmb, vln[rs, cs].astype(BF)), 0.0)
                cols.append(_dot_tn(wm[g], dmb))
            blocks.append(jnp.concatenate(cols, axis=1))
        dvln = blocks[0] if len(blocks) == 1 else jnp.concatenate(blocks, axis=0)
        dlg_ref[...] += _colsum(dvln * xh)
        dlb_ref[...] += _colsum(dvln)
        dxh = dvln * lg_ref[...]
        dgv_in = rstd * (dxh - jnp.mean(dxh, axis=-1, keepdims=True)
                         - xh * jnp.mean(dxh * xh, axis=-1, keepdims=True))
        dz_ref[:, D:2 * D] = (dgv_in * dgv).astype(BF)

    row = pl.BlockSpec((tb, D), lambda i: (i, 0))
    vec = pl.BlockSpec((1, D), lambda i: (0, 0))
    wspec = pl.BlockSpec((HEADS, SGU_BLOCK, SGU_BLOCK), lambda i: (0, 0, 0))
    bspec = pl.BlockSpec((SGU_BLOCK, HEADS), lambda i: (0, 0))
    return _pcall(body, name="sgu_bwd", grid=(T // tb,),
                  in_specs=[HBM_SPEC, row, pl.BlockSpec((tb, D), lambda i: (i, 2)),
                            pl.BlockSpec((tb, D), lambda i: (i, 3)), vec, vec, wspec, bspec],
                  out_specs=[pl.BlockSpec((tb, 2 * D), lambda i: (i, 1)), wspec, bspec, vec, vec],
                  out_shape=[jax.ShapeDtypeStruct(dz.shape, BF),
                             jax.ShapeDtypeStruct((HEADS, SGU_BLOCK, SGU_BLOCK), F32),
                             jax.ShapeDtypeStruct((SGU_BLOCK, HEADS), F32),
                             jax.ShapeDtypeStruct((1, D), F32), jax.ShapeDtypeStruct((1, D), F32)],
                  aliases={0: 0}, vmem_mb=48)(dz, dyb_pre, z, z, lg, lb, ws, bst)


def _rglru_bwd(dz, dya_pre, z, h, cw, cb, wa, ba, wx, bx, lam, tb=256):
    T = z.shape[0]
    tb = min(tb, T)
    nrow = T // tb
    per = tb // SUBLANES

    def body(dz_in, dy_ref, xr_ref, xh_ref, gr_ref, h_ref, hh_ref, cw_ref, cb_ref, wa_ref, ba_ref, wx_ref, bx_ref,
             lam_ref, dz_ref, dcw_ref, dcb_ref, dwa_ref, dba_ref, dwx_ref, dbx_ref, dlam_ref, carry, nxt):
        del dz_in
        i = pl.program_id(0)
        first_block = i == nrow - 1

        @pl.when(i == 0)
        def _():
            carry[...] = jnp.zeros_like(carry)
            nxt[...] = jnp.zeros_like(nxt)
            for ref in (dcw_ref, dcb_ref, dwa_ref, dba_ref, dwx_ref, dbx_ref, dlam_ref):
                ref[...] = jnp.zeros_like(ref)

        xr = xr_ref[...]
        pv = jnp.where(first_block, 0.0, xh_ref[...])
        s1 = _shift_down(xr, pv, 1)
        s2 = _shift_down(xr, pv, 2)
        s3 = _shift_down(xr, pv, 3)
        xc = cb_ref[...] + cw_ref[3:4, :] * xr + cw_ref[2:3, :] * s1 + cw_ref[1:2, :] * s2 + cw_ref[0:1, :] * s3
        lam = lam_ref[...]
        r, ig, ls, a, mult = _lru_gates(xc, wa_ref, ba_ref[...], wx_ref, bx_ref[...], lam)
        hv = h_ref[...]
        hprev = _shift_down(hv, jnp.where(first_block, 0.0, hh_ref[...]), 1)
        gg, dgg = _gelu_and_grad(gr_ref[...])
        dy = dy_ref[...].astype(F32)
        dz_ref[:, D:2 * D] = (dy * hv * dgg).astype(BF)

        rows = lax.broadcasted_iota(jnp.int32, (tb, D), 0)
        v = dy * gg + jnp.where(rows == tb - 1, carry[0:1, :], 0.0)
        q = jnp.where(rows < tb - 1, pltpu.roll(a, tb - 1, 0), 0.0)
        _, gsc = _scan_rows(q, v, reverse=True)
        carry[...] = (a * gsc)[0:SUBLANES]

        xi = ig * xc
        dmult = gsc * xi
        dxi = gsc * mult
        dig = dxi * xc
        dxc = dxi * ig
        dlog_a = gsc * hprev * a - dmult * (a * a) * pl.reciprocal(mult, approx=True)
        dlam_ref[...] += _colsum(dlog_a * r) * (LRU_C * _sigmoid(-lam))
        dpr = dlog_a * (LRU_C * ls) * r * (1.0 - r)
        dpi = dig * ig * (1.0 - ig)
        dba_ref[...] += _colsum(dpr)
        dbx_ref[...] += _colsum(dpi)
        back = []
        for hh in range(HEADS):
            cs = slice(hh * HD, (hh + 1) * HD)
            xh = xc[:, cs].astype(BF)
            dprh = dpr[:, cs].astype(BF)
            dpih = dpi[:, cs].astype(BF)
            dwa_ref[hh] += _dot_tn(xh, dprh)
            dwx_ref[hh] += _dot_tn(xh, dpih)
            back.append(_dot_nt(dprh, wa_ref[hh].astype(BF)) + _dot_nt(dpih, wx_ref[hh].astype(BF)))
        dxc = dxc + jnp.concatenate(back, axis=1)

        n8 = nxt[...]
        dxr = (cw_ref[3:4, :] * dxc + cw_ref[2:3, :] * _shift_up(dxc, n8, 1)
               + cw_ref[1:2, :] * _shift_up(dxc, n8, 2) + cw_ref[0:1, :] * _shift_up(dxc, n8, 3))
        nxt[...] = dxc[0:SUBLANES]
        dz_ref[:, 0:D] = dxr.astype(BF)
        dcw_ref[3:4, :] += _colsum(dxc * xr)
        dcw_ref[2:3, :] += _colsum(dxc * s1)
        dcw_ref[1:2, :] += _colsum(dxc * s2)
        dcw_ref[0:1, :] += _colsum(dxc * s3)
        dcb_ref[...] += _colsum(dxc)

    rev = lambda col: (lambda i: (nrow - 1 - i, col))
    halo = lambda col: pl.BlockSpec((SUBLANES, D), lambda i: (jnp.maximum((nrow - 1 - i) * per - 1, 0), col))
    vec = pl.BlockSpec((1, D), lambda i: (0, 0))
    wspec = pl.BlockSpec((HEADS, HD, HD), lambda i: (0, 0, 0))
    c4 = pl.BlockSpec((4, D), lambda i: (0, 0))
    wshape = jax.ShapeDtypeStruct((HEADS, HD, HD), F32)
    vshape = jax.ShapeDtypeStruct((1, D), F32)
    return _pcall(body, name="rglru_bwd", grid=(nrow,),
                  in_specs=[HBM_SPEC, pl.BlockSpec((tb, D), rev(0)), pl.BlockSpec((tb, D), rev(0)), halo(0),
                            pl.BlockSpec((tb, D), rev(1)), pl.BlockSpec((tb, D), rev(0)), halo(0),
                            c4, vec, wspec, vec, wspec, vec, vec],
                  out_specs=[pl.BlockSpec((tb, 2 * D), rev(0)), c4, vec, wspec, vec, wspec, vec, vec],
                  out_shape=[jax.ShapeDtypeStruct(dz.shape, BF), jax.ShapeDtypeStruct((4, D), F32), vshape,
                             wshape, vshape, wshape, vshape, vshape],
                  scratch=[pltpu.VMEM((SUBLANES, D), F32), pltpu.VMEM((SUBLANES, D), F32)],
                  aliases={0: 0}, vmem_mb=56)(dz, dya_pre, z, z, z, h, h, cw, cb, wa, ba, wx, bx, lam)


def _pack_rows(parts):
    out = []
    for p in parts:
        q = p.reshape(-1, LANES)
        pad = (-q.shape[0]) % SUBLANES
        if pad:
            q = jnp.concatenate([q, jnp.zeros((pad, LANES), q.dtype)], axis=0)
        out.append(q)
    return jnp.concatenate(out, axis=0)


def _rows_of(shape):
    n = 1
    for s in shape:
        n *= s
    rows = n // LANES
    return rows + (-rows) % SUBLANES


def kernel(x, c, w_ada, b_ada, norm_mix_g, w_in, rnn_conv_w, rnn_conv_b, lru_w_a, lru_b_a, lru_w_x, lru_b_x, lru_lambda, sgu_ln_g, sgu_ln_b, sgu_w_s, sgu_b_s, w_branch_a, w_branch_b, w_out, norm_ffn_g, w_up, ffn_conv_w, ffn_conv_b, w_down, norm_final_g, loss_target, m_w_ada, m_b_ada, m_norm_mix_g, m_w_in, m_rnn_conv_w, m_rnn_conv_b, m_lru_w_a, m_lru_b_a, m_lru_w_x, m_lru_b_x, m_lru_lambda, m_sgu_ln_g, m_sgu_ln_b, m_sgu_w_s, m_sgu_b_s, m_w_branch_a, m_w_branch_b, m_w_out, m_norm_ffn_g, m_w_up, m_ffn_conv_w, m_ffn_conv_b, m_w_down, m_norm_final_g, v_w_ada, v_b_ada, v_norm_mix_g, v_w_in, v_rnn_conv_w, v_rnn_conv_b, v_lru_w_a, v_lru_b_a, v_lru_w_x, v_lru_b_x, v_lru_lambda, v_sgu_ln_g, v_sgu_ln_b, v_sgu_w_s, v_sgu_b_s, v_w_branch_a, v_w_branch_b, v_w_out, v_norm_ffn_g, v_w_up, v_ffn_conv_w, v_ffn_conv_b, v_w_down, v_norm_final_g):
    args = dict(locals())
    T = x.shape[1]
    mx, my, mc = lax.axis_index("x"), lax.axis_index("y"), lax.axis_index("c")
    chip = 2 * mx + my
    dev = 2 * chip + mc
    vec = lambda a: a.reshape(1, -1)

    xt = x.reshape(T, D)
    tgt = loss_target.reshape(T, D)
    ns = w_in.shape[2]
    dff = w_down.shape[1] * N_CHIPS

    c_all = _gather8(c.reshape(SUBLANES, LANES), "gather_c").reshape(N_DEV, D)
    b_ada_sh = lax.dynamic_slice(b_ada, (0, chip * ns), (1, ns))
    mod_sh = _mod_fwd(c_all, w_ada[0], b_ada_sh)

    mixer_w = _cast_shards([w_in[0], w_branch_a[0], w_branch_b[0], w_out[0]], "cast_mixer_weights")
    w_in4, wba4, wbb4, wo4, rcw4, fcw4, mod4 = _gather_weights(
        list(mixer_w) + [rnn_conv_w[0], ffn_conv_w[0], mod_sh], [True] * 4 + [False] * 3)
    late = _cast_shards([w_up[0], w_down[0]], "cast_late", after=mod4)
    late_plan = _gather_half_plan([w.shape for w in late])
    late_handle, late_token = _remote_start(
        late, [lax.empty((N_CHIPS,) + w.shape, w.dtype) for w in late], late_plan, 3 * len(late), "gather_late_start")
    rcw_full = jnp.transpose(rcw4, (1, 0, 2)).reshape(4, D)
    fcw_full = jnp.transpose(fcw4, (1, 0, 2)).reshape(3, 2 * dff)
    mod = lax.dynamic_index_in_dim(mod4, dev, axis=1, keepdims=False).reshape(1, 6 * D)
    shift1, scale1, gate1, shift2, scale2, gate2 = [mod[:, k * D:(k + 1) * D] for k in range(6)]

    bst = jnp.transpose(sgu_b_s[0])
    wba_full = wba4.reshape(D, D)
    wbb_full = wbb4.reshape(D, D)
    wo_full = wo4.reshape(D, D)
    h1, z, h_lru, ya_pre, yb_pre, merged, ya, yb, o1, x2 = _mixer_fwd(
        xt, norm_mix_g, scale1 + late_token[0:1, 0:1], shift1, gate1, w_in4, rcw_full, rnn_conv_b,
        lru_w_a[0], lru_b_a, lru_w_x[0], lru_b_x, lru_lambda, sgu_ln_g, sgu_ln_b, sgu_w_s[0], bst,
        wba_full, wbb_full, wo_full)
    late, late_lands = _remote_wait(late_handle, late_plan, o1, "gather_late_wait")
    w_up4, w_down4 = _forward_halves(late, late_lands)
    wd_full = w_down4.reshape(dff, D)
    h2, up, f, ffn_ga, ffn_vd, loss_part, dx3, dfo, dgf, dgate2 = _ffn_fwd(
        x2, norm_ffn_g, scale2, shift2, gate2, vec(norm_final_g), w_up4, wd_full, fcw_full, ffn_conv_b, tgt)

    dup, dfcw, dfcb, dx2, dshift2, dscale2, dg_ffn, do1, dgate1 = _ffn_bwd_fused(
        dfo, wd_full, up, ffn_ga, ffn_vd, fcw_full, w_up4, x2, dx3, norm_ffn_g, scale2, gate1, o1)
    dwd = _mm_tn_cols(f, dfo, "dw_down", 1, D, mb=D)
    dw_up4 = _mm_tn_cols(h2, dup, "dw_up", N_CHIPS, ns)
    dz, dya, dyb, dya_pre, dyb_pre = _mix_bwd(do1, ya, yb, z, wo_full, wba_full, wbb_full)
    dwo = _mm_tn_cols(merged, do1, "dw_out", 1, D)
    dwba = _mm_tn_cols(ya_pre, dya, "dw_branch_a", 1, D)
    dwbb = _mm_tn_cols(yb_pre, dyb, "dw_branch_b", 1, D)

    core = mc.astype(jnp.int32).reshape(1)
    chip_id = chip.astype(jnp.int32).reshape(1)

    def reduce_start(group, name):
        from_core = _send_other_half([g for _, g in group], "swap_halves_" + name)
        sums = [_add_halves_bf16(core, g, o, "sum_cores_" + n) for (n, g), o in zip(group, from_core)]
        lands = [lax.empty((3,) + s.shape[1:], s.dtype) for s in sums]
        return _remote_start(sums, lands, _scatter_plan(len(group)), 3 * len(group), "scatter_start_" + name)

    def reduce_finish(group, handle, after, name):
        sums, landed = _remote_wait(handle, _scatter_plan(len(group)), after, "scatter_wait_" + name)
        return [_sum_own_and_landed(chip_id, s, l, "sum_chips_" + n) for (n, _), s, l in zip(group, sums, landed)]

    group1 = [("w_up", dw_up4), ("w_down", dwd.reshape(N_CHIPS, dff // N_CHIPS, D)),
              ("w_branch_a", dwba.reshape(N_CHIPS, D // N_CHIPS, D)),
              ("w_branch_b", dwbb.reshape(N_CHIPS, D // N_CHIPS, D)), ("w_out", dwo.reshape(N_CHIPS, D // N_CHIPS, D))]
    handle1, token1 = reduce_start(group1, "late")
    dz, dws, dbst, dlg, dlb = _sgu_bwd(dz, dyb_pre, z, sgu_ln_g + token1[0:1, 0:1], sgu_ln_b, sgu_w_s[0], bst)
    dz, drcw, drcb, dwa, dba, dwx, dbx, dlam = _rglru_bwd(
        dz, dya_pre, z, h_lru, rcw_full, rnn_conv_b, lru_w_a[0], lru_b_a, lru_w_x[0], lru_b_x, lru_lambda)
    early_small = [("rnn_conv_b", drcb), ("lru_w_a", dwa), ("lru_b_a", dba), ("lru_w_x", dwx), ("lru_b_x", dbx),
                   ("lru_lambda", dlam), ("sgu_ln_g", dlg), ("sgu_ln_b", dlb), ("sgu_w_s", dws),
                   ("sgu_b_s", jnp.transpose(dbst)), ("norm_ffn_g", dg_ffn),
                   ("ffn_conv_b", dfcb), ("norm_final_g", dgf)]
    r_early = sum(_rows_of(args[n].shape) for n, _ in early_small)
    early_pack = _pack_rows([g for _, g in early_small] + [drcw, dfcw])
    early_pack = jnp.concatenate(
        [early_pack, jnp.zeros(((-early_pack.shape[0]) % 256, LANES), F32)], axis=0)
    early_chip = _add_pair(early_pack, _swap_core(early_pack, "swap_small_grads"), "sum_cores_small_grads")
    early_handle, token3 = _remote_start([early_chip], [lax.empty((3,) + early_chip.shape, F32)], _bcast_plan, 3,
                                         "small_grads_start")
    totals1 = reduce_finish(group1, handle1, drcb, "late")
    group2 = [("w_in", _mm_tn_cols(h1, dz, "dw_in", N_CHIPS, ns))]
    handle2, token2 = reduce_start(group2, "in")
    grad_x, dshift1, dscale1, dg_mix = _mm_nt_normbwd(
        dz, w_in4, xt, dx2, norm_mix_g + (token2[0:1, 0:1] + token3[0:1, 0:1]), scale1, "dh1_norm_bwd")
    totals2 = reduce_finish(group2, handle2, dg_mix, "in")
    dmod = jnp.concatenate([dshift1, dscale1, dgate1, dshift2, dscale2, dgate2], axis=1)

    big = group1 + group2
    fulls = _share_halves(totals1 + totals2)
    out = {}
    for (n, _), full in zip(big, fulls):
        shape = args[n].shape
        res = _adamw(args[n][0], args["m_" + n][0], args["v_" + n][0], [full.reshape(shape[1:])], "adamw_" + n)
        for kind, r in zip(("grad_", "delta_", "new_m_", "new_v_"), res):
            out[kind + n] = r.reshape(shape)

    late_small = [("b_ada", dmod), ("norm_mix_g", dg_mix)]
    small = late_small + early_small
    late_all = _gather8(_pack_rows([g for _, g in late_small]), "gather_late_small_grads")
    late_sum = _sum_parts(late_all, "sum_late_small_grads")
    _, (early_landed,) = _remote_wait(early_handle, _bcast_plan, dg_mix, "small_grads_wait")
    early_sum = _sum_chips_in_order(chip_id, early_chip, early_landed, "sum_early_small_grads")
    r_small = sum(_rows_of(args[n].shape) for n, _ in small)
    r_pad = r_small + (-r_small) % 256
    fill = jnp.zeros((r_pad - r_small, LANES), F32)
    g_small = jnp.concatenate([late_sum, early_sum[:r_early], fill], axis=0)

    def pack_small(prefix):
        return jnp.concatenate([_pack_rows([args[prefix + n] for n, _ in small]), fill], axis=0)

    res = _adamw(pack_small(""), pack_small("m_"), pack_small("v_"), [g_small], "adamw_small")
    off = 0
    for n, _ in small:
        shape = args[n].shape
        rows = _rows_of(shape)
        for kind, r in zip(("grad_", "delta_", "new_m_", "new_v_"), res):
            out[kind + n] = r[off:off + rows].reshape(shape)
        off += rows

    rcw_cols = rnn_conv_w.shape[2]
    g_rcw = lax.dynamic_slice(early_sum[r_early:r_early + 32].reshape(4, D), (0, chip * rcw_cols), (4, rcw_cols))
    g_fcw = lax.dynamic_slice(early_sum[r_early + 32:r_early + 32 + 144].reshape(3, 2 * dff), (0, chip * ns), (3, ns))
    conv = [("rnn_conv_w", g_rcw), ("ffn_conv_w", g_fcw)]
    res = _adamw(_pack_rows([args[n] for n, _ in conv]), _pack_rows([args["m_" + n] for n, _ in conv]),
                 _pack_rows([args["v_" + n] for n, _ in conv]), [_pack_rows([g for _, g in conv])], "adamw_conv")
    off = 0
    for n, _ in conv:
        shape = args[n].shape
        cnt = shape[1] * shape[2] // LANES
        for kind, r in zip(("grad_", "delta_", "new_m_", "new_v_"), res):
            out[kind + n] = r[off:off + cnt].reshape(shape)
        off += _rows_of(shape)

    dmod_all = late_all[:, 0:6 * D // LANES, :].reshape(N_DEV, 6 * D)
    dmod_sh = lax.dynamic_slice(dmod_all, (0, chip * ns), (N_DEV, ns))
    res = _ada_adamw(jnp.transpose(c_all), dmod_sh, w_ada[0], m_w_ada[0], v_w_ada[0])
    for kind, r in zip(("grad_", "delta_", "new_m_", "new_v_"), res):
        out[kind + "w_ada"] = r.reshape(w_ada.shape)

    loss = lax.psum(loss_part[0, 0], ("x", "y", "c"))
    names = ["w_ada", "b_ada", "norm_mix_g", "w_in", "rnn_conv_w", "rnn_conv_b", "lru_w_a", "lru_b_a", "lru_w_x",
             "lru_b_x", "lru_lambda", "sgu_ln_g", "sgu_ln_b", "sgu_w_s", "sgu_b_s", "w_branch_a", "w_branch_b",
             "w_out", "norm_ffn_g", "w_up", "ffn_conv_w", "ffn_conv_b", "w_down", "norm_final_g"]
    result = [loss, grad_x.reshape(x.shape)]
    for kind in ("grad_", "delta_", "new_m_", "new_v_"):
        result += [out[kind + n] for n in names]
    return tuple(result)
```

```python
import jax
import jax.numpy as jnp
from jax import lax
from jax.experimental import pallas as pl
from jax.experimental.pallas import tpu as pltpu

F32 = jnp.float32
BF = jnp.bfloat16

D = 1024
HEADS = 8
HD = D // HEADS
SGU_BLOCK = 128
N_CHIPS = 4
N_DEV = 8
EPS = 1e-6
LRU_C = 8.0
LANES = 128
SUBLANES = 8

ADAM_LR = 0.001
ADAM_B1 = 0.9
ADAM_B2 = 0.999
ADAM_EPS = 1e-08
ADAM_WD = 0.01
ADAM_STEP = 10

GELU_K0 = 0.7978845608028654
GELU_K1 = 0.044715

HBM_SPEC = pl.BlockSpec(memory_space=pltpu.HBM)
MESH_ID = pl.DeviceIdType.MESH


def _pcall(body, *, name, out_shape, grid=(), in_specs=None, out_specs=None, scratch=(), vmem_mb=32, aliases=None,
           grid_spec=None):
    kw = {}
    if aliases:
        kw["input_output_aliases"] = aliases
    if grid_spec is not None:
        kw["grid_spec"] = grid_spec
        ndim = len(grid_spec.grid)
    else:
        kw.update(grid=grid, in_specs=in_specs, out_specs=out_specs, scratch_shapes=list(scratch))
        ndim = len(grid)
    if ndim:
        params = pltpu.CompilerParams(dimension_semantics=("arbitrary",) * ndim, vmem_limit_bytes=vmem_mb * 2 ** 20)
    else:
        params = pltpu.CompilerParams(vmem_limit_bytes=vmem_mb * 2 ** 20)
    return pl.pallas_call(body, name=name, out_shape=out_shape, compiler_params=params, **kw)


def _gelu_cdf(x, x2):
    return 0.5 * jnp.tanh(x * (GELU_K0 + (GELU_K0 * GELU_K1) * x2)) + 0.5


def _gelu(x):
    return x * _gelu_cdf(x, x * x)


def _gelu_and_grad(x):
    x2 = x * x
    s = _gelu_cdf(x, x2)
    g = x * s
    dg = s * (1.0 + (x - g) * ((2.0 * GELU_K0) + (6.0 * GELU_K0 * GELU_K1) * x2))
    return g, dg


def _sigmoid(x):
    return 1.0 / (1.0 + jnp.exp(-x))


def _sigmoid_t(x):
    return 0.5 * jnp.tanh(0.5 * x) + 0.5


def _log_sigmoid(x):
    e = jnp.exp(-jnp.abs(x))
    u = 1.0 + e
    d = u - 1.0
    l1p = jnp.where(d == 0.0, e, jnp.log(u) * (e / jnp.where(d == 0.0, 1.0, d)))
    return jnp.minimum(x, 0.0) - l1p


def _dot(a, b):
    return jnp.dot(a, b, preferred_element_type=F32)


def _dot_nt(a, b):
    return lax.dot_general(a, b, (((1,), (1,)), ((), ())), preferred_element_type=F32)


def _dot_tn(a, b):
    return lax.dot_general(a, b, (((0,), (0,)), ((), ())), preferred_element_type=F32)


def _shift_down(x, halo, s):
    r = pltpu.roll(x, s, 0)
    rows = lax.broadcasted_iota(jnp.int32, (SUBLANES, x.shape[1]), 0)
    head = jnp.where(rows < s, pltpu.roll(halo, s, 0), r[0:SUBLANES])
    return jnp.concatenate([head, r[SUBLANES:]], axis=0)


def _shift_up(x, halo, s):
    n = x.shape[0]
    r = pltpu.roll(x, n - s, 0)
    rows = lax.broadcasted_iota(jnp.int32, (SUBLANES, x.shape[1]), 0)
    tail = jnp.where(rows >= SUBLANES - s, pltpu.roll(halo, SUBLANES - s, 0), r[n - SUBLANES:n])
    return jnp.concatenate([r[:n - SUBLANES], tail], axis=0)


def _scan_rows(a, u, reverse):
    n, width = a.shape
    rows = lax.broadcasted_iota(jnp.int32, (n, width), 0)
    d = 1
    while d < n:
        if d < SUBLANES:
            keep = rows < n - d if reverse else rows >= d
            shift = n - d if reverse else d
            a_s = jnp.where(keep, pltpu.roll(a, shift, 0), 1.0)
            u_s = jnp.where(keep, pltpu.roll(u, shift, 0), 0.0)
        elif reverse:
            a_s = jnp.concatenate([a[d:], jnp.ones((d, width), a.dtype)], axis=0)
            u_s = jnp.concatenate([u[d:], jnp.zeros((d, width), u.dtype)], axis=0)
        else:
            a_s = jnp.concatenate([jnp.ones((d, width), a.dtype), a[:n - d]], axis=0)
            u_s = jnp.concatenate([jnp.zeros((d, width), u.dtype), u[:n - d]], axis=0)
        u = a * u_s + u
        a = a * a_s
        d *= 2
    return a, u


def _colsum(x):
    return jnp.sum(x, axis=0, keepdims=True)


def _rms_stats(x):
    r = lax.rsqrt(jnp.mean(x * x, axis=-1, keepdims=True) + EPS)
    return r, x * r


def _lru_gates(xc, wa_ref, ba, wx_ref, bx, lam, head0=0):
    pr, pi = [], []
    for hh in range(xc.shape[1] // HD):
        xh = xc[:, hh * HD:(hh + 1) * HD].astype(BF)
        pr.append(_dot(xh, wa_ref[head0 + hh].astype(BF)))
        pi.append(_dot(xh, wx_ref[head0 + hh].astype(BF)))
    r = _sigmoid_t((pr[0] if len(pr) == 1 else jnp.concatenate(pr, axis=1)) + ba)
    ig = _sigmoid_t((pi[0] if len(pi) == 1 else jnp.concatenate(pi, axis=1)) + bx)
    ls = _log_sigmoid(lam)
    log_a = LRU_C * r * ls
    a = jnp.exp(log_a)
    x2 = 2.0 * log_a
    u = a * a
    lu = jnp.log(jnp.maximum(u, 1e-37))
    ratio = x2 * pl.reciprocal(jnp.where(lu == 0.0, 1.0, lu), approx=True)
    em1 = jnp.where(lu == 0.0, x2, jnp.where(u < 1e-30, -1.0, (u - 1.0) * ratio))
    mult = jnp.sqrt(-em1)
    return r, ig, ls, a, mult


def _sgu_mix(vln, ws_ref, bst_ref, tb):
    ri = lax.broadcasted_iota(jnp.int32, (SGU_BLOCK, SGU_BLOCK), 0)
    ci = lax.broadcasted_iota(jnp.int32, (SGU_BLOCK, SGU_BLOCK), 1)
    wm = [jnp.where(ri >= ci, ws_ref[g], 0.0).astype(BF) for g in range(HEADS)]
    blocks = []
    for blk in range(tb // SGU_BLOCK):
        cols = []
        for g in range(HEADS):
            vb = vln[blk * SGU_BLOCK:(blk + 1) * SGU_BLOCK, g * HD:(g + 1) * HD].astype(BF)
            cols.append(_dot(wm[g], vb) + bst_ref[:, g:g + 1])
        blocks.append(jnp.concatenate(cols, axis=1))
    mixed = blocks[0] if len(blocks) == 1 else jnp.concatenate(blocks, axis=0)
    return wm, mixed


def _layernorm_stats(v):
    mu = jnp.mean(v, axis=-1, keepdims=True)
    vc = v - mu
    rstd = lax.rsqrt(jnp.mean(vc * vc, axis=-1, keepdims=True) + EPS)
    return rstd, vc * rstd


def _my_xyc():
    return lax.axis_index("x"), lax.axis_index("y"), lax.axis_index("c")


def _gather_weights(srcs, halve):
    n = len(srcs)
    out_shape = [jax.ShapeDtypeStruct((N_CHIPS,) + s.shape, s.dtype) for s in srcs]

    def body(*refs):
        src, out = refs[:n], refs[n:2 * n]
        send_sems, recv_sems, fwd_send, fwd_recv, loc_sems = refs[2 * n:]
        x, y, c = _my_xyc()
        me = 2 * x + y
        chips = [(1 - x, y), (x, 1 - y), (1 - x, 1 - y)]

        def half(ref, a, which):
            if not halve[a]:
                return ref
            h = srcs[a].shape[0] // 2
            return ref.at[pl.ds(which * h, h)]

        def ici(a, k, frm):
            px, py = chips[k]
            return pltpu.make_async_remote_copy(
                src_ref=half(src[a], a, c), dst_ref=half(out[a].at[frm], a, c),
                send_sem=send_sems.at[a, k], recv_sem=recv_sems.at[a, k],
                device_id=(px, py, c), device_id_type=MESH_ID)

        def d2d(a, k, which):
            px, py = chips[k]
            rows = half(out[a].at[2 * px + py], a, which)
            return pltpu.make_async_remote_copy(
                src_ref=rows, dst_ref=rows, send_sem=fwd_send.at[a, k], recv_sem=fwd_recv.at[a, k],
                device_id=(x, y, 1 - c), device_id_type=MESH_ID)

        local, sends = [], []
        for a in range(n):
            lc = pltpu.make_async_copy(src[a], out[a].at[me], loc_sems.at[a])
            lc.start()
            local.append(lc)
            for k in range(3):
                cp = ici(a, k, me)
                cp.start()
                sends.append(cp)
        for a in range(n):
            for k in range(3):
                px, py = chips[k]
                ici(a, k, 2 * px + py).wait_recv()
                if halve[a]:
                    fw = d2d(a, k, c)
                    fw.start()
                    sends.append(fw)
        for a in range(n):
            if halve[a]:
                for k in range(3):
                    d2d(a, k, 1 - c).wait_recv()
        for cp in sends:
            cp.wait_send()
        for lc in local:
            lc.wait()

    sem = pltpu.SemaphoreType.DMA((n, 3))
    return _pcall(body, name="gather_weights", out_shape=out_shape, in_specs=[HBM_SPEC] * n,
                  out_specs=[HBM_SPEC] * n, scratch=[sem, sem, sem, sem, pltpu.SemaphoreType.DMA((n,))])(*srcs)


SEM_SPEC = pl.BlockSpec(memory_space=pltpu.SEMAPHORE)


def _remote_start(srcs, lands, plan, ncopies, name):
    n, m = len(srcs), len(lands)

    def body(*refs):
        src, land = refs[:n], refs[n:n + m]
        send_sems, recv_sems = refs[n + m], refs[n + m + 1]
        token = refs[-1]
        x, y, c = _my_xyc()
        for i, (s, d, dev) in enumerate(plan(src, land, x, y, c)):
            pltpu.make_async_remote_copy(src_ref=s, dst_ref=d, send_sem=send_sems.at[i], recv_sem=recv_sems.at[i],
                                         device_id=dev, device_id_type=MESH_ID).start()
        token[...] = jnp.zeros_like(token)

    bufs = list(srcs) + list(lands)
    out = pl.pallas_call(
        body, name=name,
        out_shape=(pltpu.SemaphoreType.DMA((ncopies,)), pltpu.SemaphoreType.DMA((ncopies,)),
                   *[pltpu.HBM(b.shape, b.dtype) for b in bufs], jax.ShapeDtypeStruct((SUBLANES, LANES), F32)),
        in_specs=[HBM_SPEC] * (n + m),
        out_specs=(SEM_SPEC, SEM_SPEC, *[HBM_SPEC] * (n + m), pl.BlockSpec(memory_space=pltpu.VMEM)),
        input_output_aliases={i: 2 + i for i in range(n + m)},
        compiler_params=pltpu.CompilerParams(has_side_effects=pltpu.SideEffectType.DATAFLOW_SIDE_EFFECTING),
    )(*[pltpu.with_memory_space_constraint(b, pltpu.HBM) for b in bufs])
    return (out[0], out[1], out[2:2 + n], out[2 + n:2 + n + m]), out[-1]


def _remote_wait(handle, plan, after, name):
    send_sems, recv_sems, srcs, lands = handle
    n, m = len(srcs), len(lands)

    def body(*refs):
        src, land = refs[:n], refs[n:n + m]
        ssem, rsem = refs[n + m], refs[n + m + 1]
        x, y, c = _my_xyc()
        for i, (s, d, dev) in enumerate(plan(src, land, x, y, c)):
            cp = pltpu.make_async_remote_copy(src_ref=s, dst_ref=d, send_sem=ssem.at[i], recv_sem=rsem.at[i],
                                              device_id=dev, device_id_type=MESH_ID)
            cp.wait_send()
            cp.wait_recv()

    bufs = list(srcs) + list(lands)
    out = pl.pallas_call(
        body, name=name, out_shape=tuple(pltpu.HBM(b.shape, b.dtype) for b in bufs),
        in_specs=[HBM_SPEC] * (n + m) + [SEM_SPEC, SEM_SPEC, pl.BlockSpec(memory_space=pl.ANY)],
        out_specs=tuple([HBM_SPEC] * (n + m)), input_output_aliases={i: i for i in range(n + m)},
        compiler_params=pltpu.CompilerParams(has_side_effects=pltpu.SideEffectType.DATAFLOW_SIDE_EFFECTING),
    )(*bufs, send_sems, recv_sems, after)
    return out[:n], out[n:]


def _chips_of(x, y):
    return [(1 - x, y), (x, 1 - y), (1 - x, 1 - y)]


def _gather_half_plan(shapes):
    def plan(src, land, x, y, c):
        me = 2 * x + y
        out = []
        for a, shape in enumerate(shapes):
            h = shape[0] // 2
            rows = pl.ds(c * h, h)
            for px, py in _chips_of(x, y):
                out.append((src[a].at[rows], land[a].at[me, rows], (px, py, c)))
        return out

    return plan


def _forward_halves(srcs, lands):
    n = len(srcs)

    def body(*refs):
        src, land = refs[:n], refs[2 * n:3 * n]
        send_sems, recv_sems, loc_sems = refs[3 * n:]
        x, y, c = _my_xyc()
        me = 2 * x + y

        def fwd(a, k, which):
            px, py = _chips_of(x, y)[k]
            h = srcs[a].shape[0] // 2
            rows = land[a].at[2 * px + py, pl.ds(which * h, h)]
            return pltpu.make_async_remote_copy(
                src_ref=rows, dst_ref=rows, send_sem=send_sems.at[a, k], recv_sem=recv_sems.at[a, k],
                device_id=(x, y, 1 - c), device_id_type=MESH_ID)

        local, sends = [], []
        for a in range(n):
            lc = pltpu.make_async_copy(src[a], land[a].at[me], loc_sems.at[a])
            lc.start()
            local.append(lc)
            for k in range(3):
                cp = fwd(a, k, c)
                cp.start()
                sends.append(cp)
        for a in range(n):
            for k in range(3):
                fwd(a, k, 1 - c).wait_recv()
        for cp in sends:
            cp.wait_send()
        for lc in local:
            lc.wait()

    sem = pltpu.SemaphoreType.DMA((n, 3))
    return _pcall(body, name="forward_halves", out_shape=[jax.ShapeDtypeStruct(l.shape, l.dtype) for l in lands],
                  in_specs=[HBM_SPEC] * (2 * n), out_specs=[HBM_SPEC] * n, aliases={n + a: a for a in range(n)},
                  scratch=[sem, sem, pltpu.SemaphoreType.DMA((n,))])(*srcs, *lands)


def _gather8(src, name):
    def body(src_ref, out_ref, send_sems, recv_sems, loc_sem):
        x, y, c = _my_xyc()
        me = 4 * x + 2 * y + c
        lc = pltpu.make_async_copy(src_ref, out_ref.at[me], loc_sem)
        lc.start()
        cps = []
        for k in range(1, N_DEV):
            px = 1 - x if (k >> 2) & 1 else x
            py = 1 - y if (k >> 1) & 1 else y
            pc = 1 - c if k & 1 else c
            cp = pltpu.make_async_remote_copy(
                src_ref=src_ref, dst_ref=out_ref.at[me], send_sem=send_sems.at[k - 1], recv_sem=recv_sems.at[k - 1],
                device_id=(px, py, pc), device_id_type=MESH_ID)
            cp.start()
            cps.append(cp)
        for cp in cps:
            cp.wait()
        lc.wait()

    return _pcall(body, name=name, out_shape=jax.ShapeDtypeStruct((N_DEV,) + src.shape, src.dtype),
                  in_specs=[HBM_SPEC], out_specs=HBM_SPEC,
                  scratch=[pltpu.SemaphoreType.DMA((N_DEV - 1,)), pltpu.SemaphoreType.DMA((N_DEV - 1,)),
                           pltpu.SemaphoreType.DMA])(src)


def _cast_shards(arrs, name, after=None):
    n = len(arrs)
    extra = [] if after is None else [after]

    def body(*refs):
        ins, outs = refs[:n], refs[n + len(extra):]
        for a in range(n):
            outs[a][...] = ins[a][...].astype(BF)

    specs = [pl.BlockSpec((s.shape[0] // 4, s.shape[1]), lambda i: (i, 0)) for s in arrs]
    return _pcall(body, name=name, grid=(4,), in_specs=specs + [pl.BlockSpec(memory_space=pl.ANY)] * len(extra),
                  out_specs=specs, out_shape=[jax.ShapeDtypeStruct(s.shape, BF) for s in arrs])(*arrs, *extra)


def _row_tile(rows, cols):
    t = rows
    while t * cols * 4 > (3 << 19) and t % 16 == 0:
        t //= 2
    return t


def _sum_parts(parts, name):
    p, rows, cols = parts.shape
    tr = _row_tile(rows, cols * p // 2)

    def body(p_ref, o_ref):
        acc = p_ref[0].astype(F32)
        for k in range(1, p):
            acc = acc + p_ref[k].astype(F32)
        o_ref[...] = acc

    return _pcall(body, name=name, grid=(rows // tr,),
                  in_specs=[pl.BlockSpec((p, tr, cols), lambda i: (0, i, 0))],
                  out_specs=pl.BlockSpec((tr, cols), lambda i: (i, 0)),
                  out_shape=jax.ShapeDtypeStruct((rows, cols), F32), vmem_mb=48)(parts)


def _sum_own_and_landed(chip, sums, landed, name):
    _, rows, cols = sums.shape
    tr = _row_tile(rows, 2 * cols)

    def body(chip_ref, own_ref, land_ref, o_ref):
        del chip_ref
        acc = own_ref[0].astype(F32)
        for k in range(3):
            acc = acc + land_ref[k].astype(F32)
        o_ref[...] = acc

    grid_spec = pltpu.PrefetchScalarGridSpec(
        num_scalar_prefetch=1, grid=(rows // tr,),
        in_specs=[pl.BlockSpec((1, tr, cols), lambda i, chip_ref: (chip_ref[0], i, 0)),
                  pl.BlockSpec((3, tr, cols), lambda i, chip_ref: (0, i, 0))],
        out_specs=pl.BlockSpec((tr, cols), lambda i, chip_ref: (i, 0)))
    return _pcall(body, name=name, grid_spec=grid_spec, out_shape=jax.ShapeDtypeStruct((rows, cols), F32),
                  vmem_mb=48)(chip, sums, landed)


def _swap_cores(arrs, name):
    n = len(arrs)

    def body(*refs):
        src, out = refs[:n], refs[n:2 * n]
        send_sems, recv_sems = refs[2 * n:]
        x, y, c = _my_xyc()
        cps = []
        for a in range(n):
            cp = pltpu.make_async_remote_copy(
                src_ref=src[a], dst_ref=out[a], send_sem=send_sems.at[a], recv_sem=recv_sems.at[a],
                device_id=(x, y, 1 - c), device_id_type=MESH_ID)
            cp.start()
            cps.append(cp)
        for cp in cps:
            cp.wait()

    sem = pltpu.SemaphoreType.DMA((n,))
    return _pcall(body, name=name, out_shape=[jax.ShapeDtypeStruct(a.shape, a.dtype) for a in arrs],
                  in_specs=[HBM_SPEC] * n, out_specs=[HBM_SPEC] * n, scratch=[sem, sem])(*arrs)


def _add_pair(a, b, name):
    rows, cols = a.shape
    tr = _row_tile(rows, 2 * cols)

    def body(a_ref, b_ref, o_ref):
        o_ref[...] = a_ref[...] + b_ref[...]

    spec = pl.BlockSpec((tr, cols), lambda i: (i, 0))
    return _pcall(body, name=name, grid=(rows // tr,), in_specs=[spec, spec], out_specs=spec,
                  out_shape=jax.ShapeDtypeStruct((rows, cols), F32))(a, b)


def _sum_chips_in_order(chip, own, landed, name):
    rows, cols = own.shape
    tr = _row_tile(rows, 4 * cols)

    def body(chip_ref, own_ref, land_ref, o_ref):
        me = chip_ref[0]
        acc = None
        for p in range(N_CHIPS):
            q = p ^ me
            k = jnp.where(q == 2, 0, jnp.where(q == 1, 1, 2))
            term = jnp.where(q == 0, own_ref[...], land_ref[k])
            acc = term if acc is None else acc + term
        o_ref[...] = acc

    grid_spec = pltpu.PrefetchScalarGridSpec(
        num_scalar_prefetch=1, grid=(rows // tr,),
        in_specs=[pl.BlockSpec((tr, cols), lambda i, chip_ref: (i, 0)),
                  pl.BlockSpec((3, tr, cols), lambda i, chip_ref: (0, i, 0))],
        out_specs=pl.BlockSpec((tr, cols), lambda i, chip_ref: (i, 0)))
    return _pcall(body, name=name, grid_spec=grid_spec, out_shape=jax.ShapeDtypeStruct((rows, cols), F32))(
        chip, own, landed)


def _bcast_plan(src, land, x, y, c):
    return [(src[0], land[0].at[k], (px, py, c)) for k, (px, py) in enumerate(_chips_of(x, y))]


def _scatter_plan(count):
    def plan(src, land, x, y, c):
        out = []
        for a in range(count):
            for k, (px, py) in enumerate(_chips_of(x, y)):
                out.append((src[a].at[2 * px + py], land[a].at[k], (px, py, c)))
        return out

    return plan


def _adamw_math(w, g, m, v):
    m2 = ADAM_B1 * m + (1.0 - ADAM_B1) * g
    v2 = ADAM_B2 * v + (1.0 - ADAM_B2) * (g * g)
    m_hat = m2 / (1.0 - ADAM_B1 ** ADAM_STEP)
    v_hat = v2 / (1.0 - ADAM_B2 ** ADAM_STEP)
    delta = -ADAM_LR * (m_hat / (jnp.sqrt(v_hat) + ADAM_EPS) + ADAM_WD * w)
    return delta, m2, v2


def _adamw(w, m, v, grads, name):
    rows, cols = w.shape
    tr = _row_tile(rows, cols)
    ng = len(grads)

    def body(*refs):
        w_ref, m_ref, v_ref = refs[:3]
        g = refs[3][...]
        for k in range(1, ng):
            g = g + refs[3 + k][...]
        g_ref, d_ref, m2_ref, v2_ref = refs[3 + ng:]
        delta, m2, v2 = _adamw_math(w_ref[...], g, m_ref[...], v_ref[...])
        g_ref[...] = g
        d_ref[...] = delta
        m2_ref[...] = m2
        v2_ref[...] = v2

    spec = pl.BlockSpec((tr, cols), lambda i: (i, 0))
    return _pcall(body, name=name, grid=(rows // tr,), in_specs=[spec] * (3 + ng), out_specs=[spec] * 4,
                  out_shape=[jax.ShapeDtypeStruct((rows, cols), F32)] * 4, vmem_mb=48)(w, m, v, *grads)


def _ada_adamw(ct, dmod, w, m, v):
    rows, cols = w.shape
    tr = _row_tile(rows, cols)

    def body(ct_ref, dm_ref, w_ref, m_ref, v_ref, g_ref, d_ref, m2_ref, v2_ref):
        cv = ct_ref[...]
        ca = cv * _sigmoid(cv)
        g = ca[:, 0:1] * dm_ref[0:1, :]
        for b in range(1, N_DEV):
            g = g + ca[:, b:b + 1] * dm_ref[b:b + 1, :]
        delta, m2, v2 = _adamw_math(w_ref[...], g, m_ref[...], v_ref[...])
        g_ref[...] = g
        d_ref[...] = delta
        m2_ref[...] = m2
        v2_ref[...] = v2

    spec = pl.BlockSpec((tr, cols), lambda i: (i, 0))
    return _pcall(body, name="ada_adamw", grid=(rows // tr,),
                  in_specs=[pl.BlockSpec((tr, N_DEV), lambda i: (i, 0)), pl.BlockSpec((N_DEV, cols), lambda i: (0, 0)),
                            spec, spec, spec],
                  out_specs=[spec] * 4, out_shape=[jax.ShapeDtypeStruct((rows, cols), F32)] * 4,
                  vmem_mb=48)(ct, dmod, w, m, v)


def _mod_fwd(c_all, w, b):
    cols = w.shape[1]
    tn = cols // 3

    def body(c_ref, w_ref, b_ref, o_ref):
        cv = c_ref[...]
        ca = (cv * _sigmoid(cv)).astype(BF)
        o_ref[...] = _dot(ca, w_ref[...].astype(BF)) + b_ref[...]

    return _pcall(body, name="mod_fwd", grid=(3,),
                  in_specs=[pl.BlockSpec((N_DEV, D), lambda j: (0, 0)), pl.BlockSpec((D, tn), lambda j: (0, j)),
                            pl.BlockSpec((1, tn), lambda j: (0, j))],
                  out_specs=pl.BlockSpec((N_DEV, tn), lambda j: (0, j)),
                  out_shape=jax.ShapeDtypeStruct((N_DEV, cols), F32))(c_all, w, b)


def _resident(shape):
    zeros = (0,) * len(shape)
    return pl.BlockSpec(shape, lambda *_: zeros, pipeline_mode=pl.Buffered(1))


def _mixer_fwd(x, g, scale, shift, gate1, w_in4, cw, cb, wa, ba, wx, bx, lam, lg, lb, ws, bst, wba, wbb, wo,
               tm=256, chunk=256, piece=512):
    T = x.shape[0]
    tm = min(tm, T)
    ns = w_in4.shape[2]
    per = ns // piece

    def body(x_ref, g_ref, sc_ref, sh_ref, g1_ref, w_ref, cw_ref, cb_ref, wa_ref, ba_ref, wx_ref, bx_ref, lam_ref,
             lg_ref, lb_ref, ws_ref, bst_ref, wba_ref, wbb_ref, wo_ref,
             h1_ref, z_ref, hl_ref, yap_ref, ybp_ref, mg_ref, ya_ref, yb_ref, o_ref, x2_ref, prev, hc):
        i = pl.program_id(0)

        @pl.when(i == 0)
        def _():
            prev[...] = jnp.zeros_like(prev)
            hc[...] = jnp.zeros_like(hc)

        xv = x_ref[...]
        _, xh = _rms_stats(xv)
        h = ((xh * g_ref[...]) * (1.0 + sc_ref[...]) + sh_ref[...]).astype(BF)
        h1_ref[...] = h

        def proj(col, width):
            for c0 in range(col, col + width, piece):
                w = min(piece, col + width - c0)
                j, off = c0 // ns, c0 % ns
                z_ref[:, c0:c0 + w] = _dot(h, w_ref[j, :, off:off + w])

        def lru_chunk(c0):
            cs = slice(c0, c0 + chunk)
            xr = z_ref[:, cs]
            pv = prev[:, cs]
            xc = (cb_ref[:, cs] + cw_ref[3:4, cs] * xr + cw_ref[2:3, cs] * _shift_down(xr, pv, 1)
                  + cw_ref[1:2, cs] * _shift_down(xr, pv, 2) + cw_ref[0:1, cs] * _shift_down(xr, pv, 3))
            prev[:, cs] = xr[tm - SUBLANES:tm]
            _, ig, _, a, mult = _lru_gates(xc, wa_ref, ba_ref[:, cs], wx_ref, bx_ref[:, cs], lam_ref[:, cs],
                                           head0=c0 // HD)
            a, u = _scan_rows(a, mult * (ig * xc), reverse=False)
            hv = u + a * hc[SUBLANES - 1:SUBLANES, cs]
            hc[:, cs] = hv[tm - SUBLANES:tm]
            hl_ref[:, cs] = hv
            yap_ref[:, cs] = (hv * _gelu(z_ref[:, D + c0:D + c0 + chunk])).astype(BF)

        proj(0, chunk)
        proj(D, chunk)
        for c0 in range(0, D, chunk):
            if c0 + chunk < D:
                proj(c0 + chunk, chunk)
                proj(D + c0 + chunk, chunk)
            else:
                proj(2 * D, 2 * D)
            lru_chunk(c0)
        proj(4 * D, 2 * D)
        _, xhn = _layernorm_stats(_gelu(z_ref[:, 3 * D:4 * D]))
        vln = xhn * lg_ref[...] + lb_ref[...]
        _, mixed = _sgu_mix(vln, ws_ref, bst_ref, tm)
        ybp = (_gelu(z_ref[:, 2 * D:3 * D]) * mixed).astype(BF)
        ybp_ref[...] = ybp
        ya = _dot(yap_ref[...], wba_ref[...])
        yb = _dot(ybp, wbb_ref[...])
        merged = (_sigmoid_t(z_ref[:, 4 * D:5 * D]) * ya + _sigmoid_t(z_ref[:, 5 * D:6 * D]) * yb).astype(BF)
        o = _dot(merged, wo_ref[...])
        x2_ref[...] = xv + g1_ref[...] * o
        mg_ref[...] = merged
        ya_ref[...] = ya.astype(BF)
        yb_ref[...] = yb.astype(BF)
        o_ref[...] = o.astype(BF)

    row = pl.BlockSpec((tm, D), lambda i: (i, 0))
    vec = pl.BlockSpec((1, D), lambda i: (0, 0))
    bf_row = jax.ShapeDtypeStruct((T, D), BF)
    f32_row = jax.ShapeDtypeStruct((T, D), F32)
    return _pcall(body, name="mixer_fwd", grid=(T // tm,),
                  in_specs=[row, vec, vec, vec, vec, _resident(w_in4.shape), _resident(cw.shape), vec,
                            _resident(wa.shape), vec, _resident(wx.shape), vec, vec, vec, vec,
                            _resident(ws.shape), _resident(bst.shape),
                            _resident(wba.shape), _resident(wbb.shape), _resident(wo.shape)],
                  out_specs=[row, pl.BlockSpec((tm, 6 * D), lambda i: (i, 0)), row, row, row, row, row, row, row, row],
                  out_shape=[bf_row, jax.ShapeDtypeStruct((T, 6 * D), F32), f32_row, bf_row, bf_row, bf_row, bf_row,
                             bf_row, bf_row, f32_row],
                  scratch=[pltpu.VMEM((SUBLANES, D), F32), pltpu.VMEM((SUBLANES, D), F32)], vmem_mb=60)(
        x, g, scale, shift, gate1, w_in4, cw, cb, wa, ba, wx, bx, lam, lg, lb, ws, bst, wba, wbb, wo)


def _ffn_fwd(x2, g, scale, shift, gate2, gf, w_up4, wd, cw, cb, target, tm=256, chunk=768):
    T = x2.shape[0]
    tm = min(tm, T)
    ns = w_up4.shape[2]
    dff = wd.shape[0]
    nchunk = dff // chunk
    per = ns // chunk

    def body(x2_ref, g_ref, sc_ref, sh_ref, g2_ref, gf_ref, wu_ref, wd_ref, cw_ref, cb_ref, t_ref,
             h2_ref, up_ref, f_ref, ga_ref, vd_ref, loss_ref, dx3_ref, dfo_ref, dgf_ref, dg2_ref, prev):
        i = pl.program_id(0)

        @pl.when(i == 0)
        def _():
            prev[...] = jnp.zeros_like(prev)
            loss_ref[...] = jnp.zeros_like(loss_ref)
            dgf_ref[...] = jnp.zeros_like(dgf_ref)
            dg2_ref[...] = jnp.zeros_like(dg2_ref)

        x2v = x2_ref[...]
        _, xh2 = _rms_stats(x2v)
        h2 = ((xh2 * g_ref[...]) * (1.0 + sc_ref[...]) + sh_ref[...]).astype(BF)
        h2_ref[...] = h2

        def conv(u, col):
            cs = slice(col, col + chunk)
            p = prev[:, cs]
            hid = (cb_ref[:, cs] + cw_ref[2:3, cs] * u + cw_ref[1:2, cs] * _shift_down(u, p, 1)
                   + cw_ref[0:1, cs] * _shift_down(u, p, 2))
            prev[:, cs] = u[tm - SUBLANES:tm]
            up_ref[:, cs] = u.astype(BF)
            return hid

        def up_proj(k):
            off = (k % per) * chunk
            return (_dot(h2, wu_ref[k // per, :, off:off + chunk]),
                    _dot(h2, wu_ref[N_CHIPS // 2 + k // per, :, off:off + chunk]))

        fo = None
        nxt = up_proj(0)
        for k in range(nchunk):
            col = k * chunk
            ua, uv = nxt
            if k + 1 < nchunk:
                nxt = up_proj(k + 1)
            act = conv(ua, col)
            val = conv(uv, dff + col)
            ga, dga = _gelu_and_grad(act)
            fk = (ga * val).astype(BF)
            f_ref[:, col:col + chunk] = fk
            ga_ref[:, col:col + chunk] = ga.astype(BF)
            vd_ref[:, col:col + chunk] = (val * dga).astype(BF)
            part = _dot(fk, wd_ref[col:col + chunk, :])
            fo = part if fo is None else fo + part

        x3 = x2v + g2_ref[...] * fo
        rstd, xh = _rms_stats(x3)
        err = xh * gf_ref[...] - t_ref[...]
        loss_ref[...] += 0.5 * jnp.sum(jnp.mean(err * err, axis=-1, keepdims=True), axis=0, keepdims=True)
        dy = err * (1.0 / D)
        dgf_ref[...] += _colsum(dy * xh)
        dxh = dy * gf_ref[...]
        dx3 = rstd * (dxh - xh * jnp.mean(dxh * xh, axis=-1, keepdims=True))
        dg2_ref[...] += _colsum(dx3 * fo)
        dx3_ref[...] = dx3
        dfo_ref[...] = (g2_ref[...] * dx3).astype(BF)

    row = pl.BlockSpec((tm, D), lambda i: (i, 0))
    vec = pl.BlockSpec((1, D), lambda i: (0, 0))
    wide = pl.BlockSpec((tm, 2 * dff), lambda i: (i, 0))
    half = pl.BlockSpec((tm, dff), lambda i: (i, 0))
    return _pcall(body, name="ffn_fwd", grid=(T // tm,),
                  in_specs=[row, vec, vec, vec, vec, vec, _resident(w_up4.shape), _resident(wd.shape),
                            _resident(cw.shape), _resident(cb.shape), row],
                  out_specs=[row, wide, half, half, half, pl.BlockSpec((1, LANES), lambda i: (0, 0)), row, row, vec, vec],
                  out_shape=[jax.ShapeDtypeStruct((T, D), BF), jax.ShapeDtypeStruct((T, 2 * dff), BF),
                             jax.ShapeDtypeStruct((T, dff), BF), jax.ShapeDtypeStruct((T, dff), BF),
                             jax.ShapeDtypeStruct((T, dff), BF), jax.ShapeDtypeStruct((1, LANES), F32),
                             jax.ShapeDtypeStruct((T, D), F32), jax.ShapeDtypeStruct((T, D), BF),
                             jax.ShapeDtypeStruct((1, D), F32), jax.ShapeDtypeStruct((1, D), F32)],
                  scratch=[pltpu.VMEM((SUBLANES, 2 * dff), F32)], vmem_mb=56)(
        x2, g, scale, shift, gate2, gf, w_up4, wd, cw, cb, target)


def _ffn_bwd(dfo, wd, up, ga, vd, cw, w_up4, x2, resid, g, scale, gate, o, tm=256, chunk=768):
    T = up.shape[0]
    tm = min(tm, T)
    dff = wd.shape[0]
    ns = w_up4.shape[2]
    nchunk = dff // chunk
    per = ns // chunk
    nrow = T // tm

    def body(dfo_ref, wd_ref, up_ref, ga_ref, vd_ref, cw_ref, wu_ref, x_ref, r_ref, g_ref, sc_ref, gt_ref, o_ref,
             du_ref, dcw_ref, dcb_ref, dx_ref, dsh_ref, dsc_ref, dg_ref, do_ref, dgt_ref, nxt):
        i = pl.program_id(0)

        @pl.when(i == 0)
        def _():
            nxt[...] = jnp.zeros_like(nxt)
            for ref in (dcw_ref, dcb_ref, dsh_ref, dsc_ref, dg_ref, dgt_ref):
                ref[...] = jnp.zeros_like(ref)

        dfo_t = dfo_ref[...]

        def conv_bwd(dh, col):
            cs = slice(col, col + chunk)
            n8 = nxt[:, cs]
            dh1 = _shift_up(dh, n8, 1)
            dh2 = _shift_up(dh, n8, 2)
            nxt[:, cs] = dh[0:SUBLANES]
            du = (cw_ref[2:3, cs] * dh + cw_ref[1:2, cs] * dh1 + cw_ref[0:1, cs] * dh2).astype(BF)
            du_ref[:, cs] = du
            u = up_ref[:, cs].astype(F32)
            dcw_ref[2:3, cs] += _colsum(dh * u)
            dcw_ref[1:2, cs] += _colsum(dh1 * u)
            dcw_ref[0:1, cs] += _colsum(dh2 * u)
            dcb_ref[:, cs] += _colsum(dh)
            return du

        def down_bwd(k):
            return _dot_nt(dfo_t, wd_ref[k * chunk:(k + 1) * chunk, :])

        dh = None
        df_next = down_bwd(0)
        for k in range(nchunk):
            col = k * chunk
            off = (k % per) * chunk
            df = df_next
            if k + 1 < nchunk:
                df_next = down_bwd(k + 1)
            du_a = conv_bwd(df * vd_ref[:, col:col + chunk].astype(F32), col)
            du_v = conv_bwd(df * ga_ref[:, col:col + chunk].astype(F32), dff + col)
            part = (_dot_nt(du_a, wu_ref[k // per, :, off:off + chunk])
                    + _dot_nt(du_v, wu_ref[N_CHIPS // 2 + k // per, :, off:off + chunk]))
            dh = part if dh is None else dh + part

        rstd, xh = _rms_stats(x_ref[...])
        dsh_ref[...] += _colsum(dh)
        dsc_ref[...] += _colsum(dh * (xh * g_ref[...]))
        dn = dh * (1.0 + sc_ref[...])
        dg_ref[...] += _colsum(dn * xh)
        dxh = dn * g_ref[...]
        dx = r_ref[...] + rstd * (dxh - xh * jnp.mean(dxh * xh, axis=-1, keepdims=True))
        dx_ref[...] = dx
        do_ref[...] = (gt_ref[...] * dx).astype(BF)
        dgt_ref[...] += _colsum(dx * o_ref[...].astype(F32))

    rev = lambda i: (nrow - 1 - i, 0)
    row = pl.BlockSpec((tm, D), rev)
    vec = pl.BlockSpec((1, D), lambda i: (0, 0))
    wide = pl.BlockSpec((tm, 2 * dff), rev)
    half = pl.BlockSpec((tm, dff), rev)
    cw3 = pl.BlockSpec((3, 2 * dff), lambda i: (0, 0))
    cb1 = pl.BlockSpec((1, 2 * dff), lambda i: (0, 0))
    vshape = jax.ShapeDtypeStruct((1, D), F32)
    return _pcall(body, name="ffn_bwd", grid=(nrow,),
                  in_specs=[row, _resident(wd.shape), wide, half, half, _resident(cw.shape), _resident(w_up4.shape),
                            row, row, vec, vec, vec, row],
                  out_specs=[wide, cw3, cb1, row, vec, vec, vec, row, vec],
                  out_shape=[jax.ShapeDtypeStruct((T, 2 * dff), BF), jax.ShapeDtypeStruct((3, 2 * dff), F32),
                             jax.ShapeDtypeStruct((1, 2 * dff), F32), jax.ShapeDtypeStruct((T, D), F32),
                             vshape, vshape, vshape, jax.ShapeDtypeStruct((T, D), BF), vshape],
                  scratch=[pltpu.VMEM((SUBLANES, 2 * dff), F32)], vmem_mb=60)(
        dfo, wd, up, ga, vd, cw, w_up4, x2, resid, g, scale, gate, o)


def _mm_tn_cols(a, b, name, nshard, nb, mb=None, tm=1024):
    T, M = a.shape
    tm = min(tm, T)
    mb = M if mb is None else mb
    ns = b.shape[1] // nshard
    per = ns // nb
    nk = T // tm

    def body(a_ref, b_ref, o_ref, c_ref):
        k = pl.program_id(2)

        @pl.when(k == 0)
        def _():
            o_ref[...] = jnp.zeros_like(o_ref)

        o_ref[0] += _dot_tn(a_ref[...], b_ref[...])

        @pl.when(k == nk - 1)
        def _():
            c_ref[...] = o_ref[...].astype(BF)

    out_spec = pl.BlockSpec((1, mb, nb), lambda m, t, k: (t // per, m, t % per))
    return _pcall(body, name=name, grid=(M // mb, nshard * per, nk),
                  in_specs=[pl.BlockSpec((tm, mb), lambda m, t, k: (k, m)),
                            pl.BlockSpec((tm, nb), lambda m, t, k: (k, t))],
                  out_specs=[out_spec, out_spec],
                  out_shape=[jax.ShapeDtypeStruct((nshard, M, ns), F32), jax.ShapeDtypeStruct((nshard, M, ns), BF)],
                  vmem_mb=48)(a, b)


def _mm_nt_normbwd(dz, w4, x, resid, g, scale, name, tm=256):
    T = x.shape[0]
    tm = min(tm, T)
    ns = w4.shape[2]

    def body(dz_ref, w_ref, x_ref, r_ref, g_ref, sc_ref, dx_ref, dsh_ref, dsc_ref, dg_ref):
        i = pl.program_id(0)

        @pl.when(i == 0)
        def _():
            dsh_ref[...] = jnp.zeros_like(dsh_ref)
            dsc_ref[...] = jnp.zeros_like(dsc_ref)
            dg_ref[...] = jnp.zeros_like(dg_ref)

        dh = None
        for j in range(N_CHIPS):
            part = _dot_nt(dz_ref[:, j * ns:(j + 1) * ns], w_ref[j])
            dh = part if dh is None else dh + part
        rstd, xh = _rms_stats(x_ref[...])
        dsh_ref[...] += _colsum(dh)
        dsc_ref[...] += _colsum(dh * (xh * g_ref[...]))
        dn = dh * (1.0 + sc_ref[...])
        dg_ref[...] += _colsum(dn * xh)
        dxh = dn * g_ref[...]
        dx_ref[...] = r_ref[...] + rstd * (dxh - xh * jnp.mean(dxh * xh, axis=-1, keepdims=True))

    row = pl.BlockSpec((tm, D), lambda i: (i, 0))
    vec = pl.BlockSpec((1, D), lambda i: (0, 0))
    return _pcall(body, name=name, grid=(T // tm,),
                  in_specs=[pl.BlockSpec((tm, N_CHIPS * ns), lambda i: (i, 0)), _resident(w4.shape), row, row, vec, vec],
                  out_specs=[row, vec, vec, vec],
                  out_shape=[jax.ShapeDtypeStruct((T, D), F32)] + [jax.ShapeDtypeStruct((1, D), F32)] * 3,
                  vmem_mb=48)(dz, w4, x, resid, g, scale)


def _mix_bwd(do, ya, yb, z, wo, wba, wbb, tm=256):
    T = do.shape[0]
    tm = min(tm, T)

    def body(do_ref, ya_ref, yb_ref, ga_ref, gb_ref, wo_ref, wa_ref, wb_ref,
             dz_ref, dya_ref, dyb_ref, dyap_ref, dybp_ref):
        dm = _dot_nt(do_ref[...], wo_ref[...])
        sa = _sigmoid_t(ga_ref[...])
        sb = _sigmoid_t(gb_ref[...])
        dya = (sa * dm).astype(BF)
        dyb = (sb * dm).astype(BF)
        dz_ref[:, 0:D] = (dm * ya_ref[...].astype(F32) * sa * (1.0 - sa)).astype(BF)
        dz_ref[:, D:2 * D] = (dm * yb_ref[...].astype(F32) * sb * (1.0 - sb)).astype(BF)
        dya_ref[...] = dya
        dyb_ref[...] = dyb
        dyap_ref[...] = _dot_nt(dya, wa_ref[...]).astype(BF)
        dybp_ref[...] = _dot_nt(dyb, wb_ref[...]).astype(BF)

    row = pl.BlockSpec((tm, D), lambda i: (i, 0))
    wspec = pl.BlockSpec((D, D), lambda i: (0, 0))
    return _pcall(body, name="mix_bwd", grid=(T // tm,),
                  in_specs=[row, row, row, pl.BlockSpec((tm, D), lambda i: (i, 4)),
                            pl.BlockSpec((tm, D), lambda i: (i, 5)), wspec, wspec, wspec],
                  out_specs=[pl.BlockSpec((tm, 2 * D), lambda i: (i, 2)), row, row, row, row],
                  out_shape=[jax.ShapeDtypeStruct((T, 6 * D), BF)] + [jax.ShapeDtypeStruct((T, D), BF)] * 4,
                  vmem_mb=48)(do, ya, yb, z, z, wo, wba, wbb)


def _sgu_bwd(dz, dyb_pre, z, lg, lb, ws, bst, tb=256):
    T = z.shape[0]
    tb = min(tb, T)

    def body(dz_in, dy_ref, zu_ref, zv_ref, lg_ref, lb_ref, ws_ref, bst_ref,
             dz_ref, dws_ref, dbst_ref, dlg_ref, dlb_ref):
        del dz_in
        i = pl.program_id(0)

        @pl.when(i == 0)
        def _():
            dws_ref[...] = jnp.zeros_like(dws_ref)
            dbst_ref[...] = jnp.zeros_like(dbst_ref)
            dlg_ref[...] = jnp.zeros_like(dlg_ref)
            dlb_ref[...] = jnp.zeros_like(dlb_ref)

        gu, dgu = _gelu_and_grad(zu_ref[...])
        gv, dgv = _gelu_and_grad(zv_ref[...])
        rstd, xh = _layernorm_stats(gv)
        vln = xh * lg_ref[...] + lb_ref[...]
        wm, mixed = _sgu_mix(vln, ws_ref, bst_ref, tb)
        dy = dy_ref[...].astype(F32)
        dz_ref[:, 0:D] = (dy * mixed * dgu).astype(BF)
        dmixed = dy * gu
        ri = lax.broadcasted_iota(jnp.int32, (SGU_BLOCK, SGU_BLOCK), 0)
        ci = lax.broadcasted_iota(jnp.int32, (SGU_BLOCK, SGU_BLOCK), 1)
        blocks = []
        for blk in range(tb // SGU_BLOCK):
            rs = slice(blk * SGU_BLOCK, (blk + 1) * SGU_BLOCK)
            cols = []
            for g in range(HEADS):
                cs = slice(g * HD, (g + 1) * HD)
                dmg = dmixed[rs, cs]
                dmb = dmg.astype(BF)
                dbst_ref[:, g:g + 1] += jnp.sum(dmg, axis=1, keepdims=True)
                dws_ref[g] += jnp.where(ri >= ci, _dot_nt(dmb, vln[rs, cs].astype(BF)), 0.0)
                cols.append(_dot_tn(wm[g], dmb))
            blocks.append(jnp.concatenate(cols, axis=1))
        dvln = blocks[0] if len(blocks) == 1 else jnp.concatenate(blocks, axis=0)
        dlg_ref[...] += _colsum(dvln * xh)
        dlb_ref[...] += _colsum(dvln)
        dxh = dvln * lg_ref[...]
        dgv_in = rstd * (dxh - jnp.mean(dxh, axis=-1, keepdims=True)
                         - xh * jnp.mean(dxh * xh, axis=-1, keepdims=True))
        dz_ref[:, D:2 * D] = (dgv_in * dgv).astype(BF)

    row = pl.BlockSpec((tb, D), lambda i: (i, 0))
    vec = pl.BlockSpec((1, D), lambda i: (0, 0))
    wspec = pl.BlockSpec((HEADS, SGU_BLOCK, SGU_BLOCK), lambda i: (0, 0, 0))
    bspec = pl.BlockSpec((SGU_BLOCK, HEADS), lambda i: (0, 0))
    return _pcall(body, name="sgu_bwd", grid=(T // tb,),
                  in_specs=[HBM_SPEC, row, pl.BlockSpec((tb, D), lambda i: (i, 2)),
                            pl.BlockSpec((tb, D), lambda i: (i, 3)), vec, vec, wspec, bspec],
                  out_specs=[pl.BlockSpec((tb, 2 * D), lambda i: (i, 1)), wspec, bspec, vec, vec],
                  out_shape=[jax.ShapeDtypeStruct(dz.shape, BF),
                             jax.ShapeDtypeStruct((HEADS, SGU_BLOCK, SGU_BLOCK), F32),
                             jax.ShapeDtypeStruct((SGU_BLOCK, HEADS), F32),
                             jax.ShapeDtypeStruct((1, D), F32), jax.ShapeDtypeStruct((1, D), F32)],
                  aliases={0: 0}, vmem_mb=48)(dz, dyb_pre, z, z, lg, lb, ws, bst)


def _rglru_bwd(dz, dya_pre, z, h, cw, cb, wa, ba, wx, bx, lam, tb=256):
    T = z.shape[0]
    tb = min(tb, T)
    nrow = T // tb
    per = tb // SUBLANES

    def body(dz_in, dy_ref, xr_ref, xh_ref, gr_ref, h_ref, hh_ref, cw_ref, cb_ref, wa_ref, ba_ref, wx_ref, bx_ref,
             lam_ref, dz_ref, dcw_ref, dcb_ref, dwa_ref, dba_ref, dwx_ref, dbx_ref, dlam_ref, carry, nxt):
        del dz_in
        i = pl.program_id(0)
        first_block = i == nrow - 1

        @pl.when(i == 0)
        def _():
            carry[...] = jnp.zeros_like(carry)
            nxt[...] = jnp.zeros_like(nxt)
            for ref in (dcw_ref, dcb_ref, dwa_ref, dba_ref, dwx_ref, dbx_ref, dlam_ref):
                ref[...] = jnp.zeros_like(ref)

        xr = xr_ref[...]
        pv = jnp.where(first_block, 0.0, xh_ref[...])
        s1 = _shift_down(xr, pv, 1)
        s2 = _shift_down(xr, pv, 2)
        s3 = _shift_down(xr, pv, 3)
        xc = cb_ref[...] + cw_ref[3:4, :] * xr + cw_ref[2:3, :] * s1 + cw_ref[1:2, :] * s2 + cw_ref[0:1, :] * s3
        lam = lam_ref[...]
        r, ig, ls, a, mult = _lru_gates(xc, wa_ref, ba_ref[...], wx_ref, bx_ref[...], lam)
        hv = h_ref[...]
        hprev = _shift_down(hv, jnp.where(first_block, 0.0, hh_ref[...]), 1)
        gg, dgg = _gelu_and_grad(gr_ref[...])
        dy = dy_ref[...].astype(F32)
        dz_ref[:, D:2 * D] = (dy * hv * dgg).astype(BF)

        rows = lax.broadcasted_iota(jnp.int32, (tb, D), 0)
        v = dy * gg + jnp.where(rows == tb - 1, carry[0:1, :], 0.0)
        q = jnp.where(rows < tb - 1, pltpu.roll(a, tb - 1, 0), 0.0)
        _, gsc = _scan_rows(q, v, reverse=True)
        carry[...] = (a * gsc)[0:SUBLANES]

        xi = ig * xc
        dmult = gsc * xi
        dxi = gsc * mult
        dig = dxi * xc
        dxc = dxi * ig
        dlog_a = gsc * hprev * a - dmult * (a * a) * pl.reciprocal(mult, approx=True)
        dlam_ref[...] += _colsum(dlog_a * r) * (LRU_C * _sigmoid(-lam))
        dpr = dlog_a * (LRU_C * ls) * r * (1.0 - r)
        dpi = dig * ig * (1.0 - ig)
        dba_ref[...] += _colsum(dpr)
        dbx_ref[...] += _colsum(dpi)
        back = []
        for hh in range(HEADS):
            cs = slice(hh * HD, (hh + 1) * HD)
            xh = xc[:, cs].astype(BF)
            dprh = dpr[:, cs].astype(BF)
            dpih = dpi[:, cs].astype(BF)
            dwa_ref[hh] += _dot_tn(xh, dprh)
            dwx_ref[hh] += _dot_tn(xh, dpih)
            back.append(_dot_nt(dprh, wa_ref[hh].astype(BF)) + _dot_nt(dpih, wx_ref[hh].astype(BF)))
        dxc = dxc + jnp.concatenate(back, axis=1)

        n8 = nxt[...]
        dxr = (cw_ref[3:4, :] * dxc + cw_ref[2:3, :] * _shift_up(dxc, n8, 1)
               + cw_ref[1:2, :] * _shift_up(dxc, n8, 2) + cw_ref[0:1, :] * _shift_up(dxc, n8, 3))
        nxt[...] = dxc[0:SUBLANES]
        dz_ref[:, 0:D] = dxr.astype(BF)
        dcw_ref[3:4, :] += _colsum(dxc * xr)
        dcw_ref[2:3, :] += _colsum(dxc * s1)
        dcw_ref[1:2, :] += _colsum(dxc * s2)
        dcw_ref[0:1, :] += _colsum(dxc * s3)
        dcb_ref[...] += _colsum(dxc)

    rev = lambda col: (lambda i: (nrow - 1 - i, col))
    halo = lambda col: pl.BlockSpec((SUBLANES, D), lambda i: (jnp.maximum((nrow - 1 - i) * per - 1, 0), col))
    vec = pl.BlockSpec((1, D), lambda i: (0, 0))
    wspec = pl.BlockSpec((HEADS, HD, HD), lambda i: (0, 0, 0))
    c4 = pl.BlockSpec((4, D), lambda i: (0, 0))
    wshape = jax.ShapeDtypeStruct((HEADS, HD, HD), F32)
    vshape = jax.ShapeDtypeStruct((1, D), F32)
    return _pcall(body, name="rglru_bwd", grid=(nrow,),
                  in_specs=[HBM_SPEC, pl.BlockSpec((tb, D), rev(0)), pl.BlockSpec((tb, D), rev(0)), halo(0),
                            pl.BlockSpec((tb, D), rev(1)), pl.BlockSpec((tb, D), rev(0)), halo(0),
                            c4, vec, wspec, vec, wspec, vec, vec],
                  out_specs=[pl.BlockSpec((tb, 2 * D), rev(0)), c4, vec, wspec, vec, wspec, vec, vec],
                  out_shape=[jax.ShapeDtypeStruct(dz.shape, BF), jax.ShapeDtypeStruct((4, D), F32), vshape,
                             wshape, vshape, wshape, vshape, vshape],
                  scratch=[pltpu.VMEM((SUBLANES, D), F32), pltpu.VMEM((SUBLANES, D), F32)],
                  aliases={0: 0}, vmem_mb=56)(dz, dya_pre, z, z, z, h, h, cw, cb, wa, ba, wx, bx, lam)


def _pack_rows(parts):
    out = []
    for p in parts:
        q = p.reshape(-1, LANES)
        pad = (-q.shape[0]) % SUBLANES
        if pad:
            q = jnp.concatenate([q, jnp.zeros((pad, LANES), q.dtype)], axis=0)
        out.append(q)
    return jnp.concatenate(out, axis=0)


def _rows_of(shape):
    n = 1
    for s in shape:
        n *= s
    rows = n // LANES
    return rows + (-rows) % SUBLANES


def kernel(x, c, w_ada, b_ada, norm_mix_g, w_in, rnn_conv_w, rnn_conv_b, lru_w_a, lru_b_a, lru_w_x, lru_b_x, lru_lambda, sgu_ln_g, sgu_ln_b, sgu_w_s, sgu_b_s, w_branch_a, w_branch_b, w_out, norm_ffn_g, w_up, ffn_conv_w, ffn_conv_b, w_down, norm_final_g, loss_target, m_w_ada, m_b_ada, m_norm_mix_g, m_w_in, m_rnn_conv_w, m_rnn_conv_b, m_lru_w_a, m_lru_b_a, m_lru_w_x, m_lru_b_x, m_lru_lambda, m_sgu_ln_g, m_sgu_ln_b, m_sgu_w_s, m_sgu_b_s, m_w_branch_a, m_w_branch_b, m_w_out, m_norm_ffn_g, m_w_up, m_ffn_conv_w, m_ffn_conv_b, m_w_down, m_norm_final_g, v_w_ada, v_b_ada, v_norm_mix_g, v_w_in, v_rnn_conv_w, v_rnn_conv_b, v_lru_w_a, v_lru_b_a, v_lru_w_x, v_lru_b_x, v_lru_lambda, v_sgu_ln_g, v_sgu_ln_b, v_sgu_w_s, v_sgu_b_s, v_w_branch_a, v_w_branch_b, v_w_out, v_norm_ffn_g, v_w_up, v_ffn_conv_w, v_ffn_conv_b, v_w_down, v_norm_final_g):
    args = dict(locals())
    T = x.shape[1]
    mx, my, mc = lax.axis_index("x"), lax.axis_index("y"), lax.axis_index("c")
    chip = 2 * mx + my
    dev = 2 * chip + mc
    vec = lambda a: a.reshape(1, -1)

    xt = x.reshape(T, D)
    tgt = loss_target.reshape(T, D)
    ns = w_in.shape[2]
    dff = w_down.shape[1] * N_CHIPS

    c_all = _gather8(c.reshape(SUBLANES, LANES), "gather_c").reshape(N_DEV, D)
    b_ada_sh = lax.dynamic_slice(b_ada, (0, chip * ns), (1, ns))
    mod_sh = _mod_fwd(c_all, w_ada[0], b_ada_sh)

    mixer_w = _cast_shards([w_in[0], w_branch_a[0], w_branch_b[0], w_out[0]], "cast_mixer_weights")
    w_in4, wba4, wbb4, wo4, rcw4, fcw4, mod4 = _gather_weights(
        list(mixer_w) + [rnn_conv_w[0], ffn_conv_w[0], mod_sh], [True] * 4 + [False] * 3)
    late = _cast_shards([w_up[0], w_down[0]], "cast_late", after=mod4)
    late_plan = _gather_half_plan([w.shape for w in late])
    late_handle, late_token = _remote_start(
        late, [lax.empty((N_CHIPS,) + w.shape, w.dtype) for w in late], late_plan, 3 * len(late), "gather_late_start")
    rcw_full = jnp.transpose(rcw4, (1, 0, 2)).reshape(4, D)
    fcw_full = jnp.transpose(fcw4, (1, 0, 2)).reshape(3, 2 * dff)
    mod = lax.dynamic_index_in_dim(mod4, dev, axis=1, keepdims=False).reshape(1, 6 * D)
    shift1, scale1, gate1, shift2, scale2, gate2 = [mod[:, k * D:(k + 1) * D] for k in range(6)]

    bst = jnp.transpose(sgu_b_s[0])
    wba_full = wba4.reshape(D, D)
    wbb_full = wbb4.reshape(D, D)
    wo_full = wo4.reshape(D, D)
    h1, z, h_lru, ya_pre, yb_pre, merged, ya, yb, o1, x2 = _mixer_fwd(
        xt, norm_mix_g, scale1 + late_token[0:1, 0:1], shift1, gate1, w_in4, rcw_full, rnn_conv_b,
        lru_w_a[0], lru_b_a, lru_w_x[0], lru_b_x, lru_lambda, sgu_ln_g, sgu_ln_b, sgu_w_s[0], bst,
        wba_full, wbb_full, wo_full)
    late, late_lands = _remote_wait(late_handle, late_plan, o1, "gather_late_wait")
    w_up4, w_down4 = _forward_halves(late, late_lands)
    wd_full = w_down4.reshape(dff, D)
    h2, up, f, ffn_ga, ffn_vd, loss_part, dx3, dfo, dgf, dgate2 = _ffn_fwd(
        x2, norm_ffn_g, scale2, shift2, gate2, vec(norm_final_g), w_up4, wd_full, fcw_full, ffn_conv_b, tgt)

    dup, dfcw, dfcb, dx2, dshift2, dscale2, dg_ffn, do1, dgate1 = _ffn_bwd(
        dfo, wd_full, up, ffn_ga, ffn_vd, fcw_full, w_up4, x2, dx3, norm_ffn_g, scale2, gate1, o1)
    dwd = _mm_tn_cols(f, dfo, "dw_down", 1, D, mb=D)
    dw_up4 = _mm_tn_cols(h2, dup, "dw_up", N_CHIPS, ns)
    dz, dya, dyb, dya_pre, dyb_pre = _mix_bwd(do1, ya, yb, z, wo_full, wba_full, wbb_full)
    dwo = _mm_tn_cols(merged, do1, "dw_out", 1, D)
    dwba = _mm_tn_cols(ya_pre, dya, "dw_branch_a", 1, D)
    dwbb = _mm_tn_cols(yb_pre, dyb, "dw_branch_b", 1, D)

    chip_id = chip.astype(jnp.int32).reshape(1)

    def reduce_start(group, name):
        wire = [g16.reshape(N_CHIPS, -1, g16.shape[-1]) for _, (_, g16) in group]
        lands = [lax.empty((3,) + w.shape[1:], w.dtype) for w in wire]
        return _remote_start(wire, lands, _scatter_plan(len(group)), 3 * len(group), "scatter_start_" + name)

    def reduce_finish(group, handle, after, name):
        _, landed = _remote_wait(handle, _scatter_plan(len(group)), after, "scatter_wait_" + name)
        return [_sum_own_and_landed(chip_id, g32.reshape(N_CHIPS, -1, g32.shape[-1]), l, "sum_chips_" + n)
                for (n, (g32, _)), l in zip(group, landed)]

    group1 = [("w_up", dw_up4), ("w_down", dwd), ("w_branch_a", dwba), ("w_branch_b", dwbb), ("w_out", dwo)]
    handle1, token1 = reduce_start(group1, "late")
    dz, dws, dbst, dlg, dlb = _sgu_bwd(dz, dyb_pre, z, sgu_ln_g + token1[0:1, 0:1], sgu_ln_b, sgu_w_s[0], bst)
    dz, drcw, drcb, dwa, dba, dwx, dbx, dlam = _rglru_bwd(
        dz, dya_pre, z, h_lru, rcw_full, rnn_conv_b, lru_w_a[0], lru_b_a, lru_w_x[0], lru_b_x, lru_lambda)
    early_small = [("rnn_conv_b", drcb), ("lru_w_a", dwa), ("lru_b_a", dba), ("lru_w_x", dwx), ("lru_b_x", dbx),
                   ("lru_lambda", dlam), ("sgu_ln_g", dlg), ("sgu_ln_b", dlb), ("sgu_w_s", dws),
                   ("sgu_b_s", jnp.transpose(dbst)), ("norm_ffn_g", dg_ffn),
                   ("ffn_conv_b", dfcb), ("norm_final_g", dgf)]
    r_early = sum(_rows_of(args[n].shape) for n, _ in early_small)
    early_pack = _pack_rows([g for _, g in early_small] + [drcw, dfcw])
    early_pack = jnp.concatenate(
        [early_pack, jnp.zeros(((-early_pack.shape[0]) % 256, LANES), F32)], axis=0)
    early_chip = _add_pair(early_pack, _swap_cores([early_pack], "swap_small_grads")[0], "sum_cores_small_grads")
    early_handle, token3 = _remote_start([early_chip], [lax.empty((3,) + early_chip.shape, F32)], _bcast_plan, 3,
                                         "small_grads_start")
    totals1 = reduce_finish(group1, handle1, drcb, "late")
    group2 = [("w_in", _mm_tn_cols(h1, dz, "dw_in", N_CHIPS, ns))]
    handle2, token2 = reduce_start(group2, "in")
    grad_x, dshift1, dscale1, dg_mix = _mm_nt_normbwd(
        dz, w_in4, xt, dx2, norm_mix_g + (token2[0:1, 0:1] + token3[0:1, 0:1]), scale1, "dh1_norm_bwd")
    totals2 = reduce_finish(group2, handle2, dg_mix, "in")
    dmod = jnp.concatenate([dshift1, dscale1, dgate1, dshift2, dscale2, dgate2], axis=1)

    big = group1 + group2
    mine = totals1 + totals2
    theirs = _swap_cores(mine, "swap_core_sums")
    out = {}
    for (n, _), a, b in zip(big, mine, theirs):
        shape = args[n].shape
        res = _adamw(args[n][0], args["m_" + n][0], args["v_" + n][0], [a, b], "adamw_" + n)
        for kind, r in zip(("grad_", "delta_", "new_m_", "new_v_"), res):
            out[kind + n] = r.reshape(shape)

    late_small = [("b_ada", dmod), ("norm_mix_g", dg_mix)]
    small = late_small + early_small
    late_all = _gather8(_pack_rows([g for _, g in late_small]), "gather_late_small_grads")
    late_sum = _sum_parts(late_all, "sum_late_small_grads")
    _, (early_landed,) = _remote_wait(early_handle, _bcast_plan, dg_mix, "small_grads_wait")
    early_sum = _sum_chips_in_order(chip_id, early_chip, early_landed, "sum_early_small_grads")
    r_small = sum(_rows_of(args[n].shape) for n, _ in small)
    r_pad = r_small + (-r_small) % 256
    fill = jnp.zeros((r_pad - r_small, LANES), F32)
    g_small = jnp.concatenate([late_sum, early_sum[:r_early], fill], axis=0)

    def pack_small(prefix):
        return jnp.concatenate([_pack_rows([args[prefix + n] for n, _ in small]), fill], axis=0)

    res = _adamw(pack_small(""), pack_small("m_"), pack_small("v_"), [g_small], "adamw_small")
    off = 0
    for n, _ in small:
        shape = args[n].shape
        rows = _rows_of(shape)
        for kind, r in zip(("grad_", "delta_", "new_m_", "new_v_"), res):
            out[kind + n] = r[off:off + rows].reshape(shape)
        off += rows

    rcw_cols = rnn_conv_w.shape[2]
    g_rcw = lax.dynamic_slice(early_sum[r_early:r_early + 32].reshape(4, D), (0, chip * rcw_cols), (4, rcw_cols))
    g_fcw = lax.dynamic_slice(early_sum[r_early + 32:r_early + 32 + 144].reshape(3, 2 * dff), (0, chip * ns), (3, ns))
    conv = [("rnn_conv_w", g_rcw), ("ffn_conv_w", g_fcw)]
    res = _adamw(_pack_rows([args[n] for n, _ in conv]), _pack_rows([args["m_" + n] for n, _ in conv]),
                 _pack_rows([args["v_" + n] for n, _ in conv]), [_pack_rows([g for _, g in conv])], "adamw_conv")
    off = 0
    for n, _ in conv:
        shape = args[n].shape
        cnt = shape[1] * shape[2] // LANES
        for kind, r in zip(("grad_", "delta_", "new_m_", "new_v_"), res):
            out[kind + n] = r[off:off + cnt].reshape(shape)
        off += _rows_of(shape)

    dmod_all = late_all[:, 0:6 * D // LANES, :].reshape(N_DEV, 6 * D)
    dmod_sh = lax.dynamic_slice(dmod_all, (0, chip * ns), (N_DEV, ns))
    res = _ada_adamw(jnp.transpose(c_all), dmod_sh, w_ada[0], m_w_ada[0], v_w_ada[0])
    for kind, r in zip(("grad_", "delta_", "new_m_", "new_v_"), res):
        out[kind + "w_ada"] = r.reshape(w_ada.shape)

    loss = lax.psum(loss_part[0, 0], ("x", "y", "c"))
    names = ["w_ada", "b_ada", "norm_mix_g", "w_in", "rnn_conv_w", "rnn_conv_b", "lru_w_a", "lru_b_a", "lru_w_x",
             "lru_b_x", "lru_lambda", "sgu_ln_g", "sgu_ln_b", "sgu_w_s", "sgu_b_s", "w_branch_a", "w_branch_b",
             "w_out", "norm_ffn_g", "w_up", "ffn_conv_w", "ffn_conv_b", "w_down", "norm_final_g"]
    result = [loss, grad_x.reshape(x.shape)]
    for kind in ("grad_", "delta_", "new_m_", "new_v_"):
        result += [out[kind + n] for n in names]
    return tuple(result)
```

```python
import jax
import jax.numpy as jnp
from jax import lax
from jax.experimental import pallas as pl
from jax.experimental.pallas import tpu as pltpu

F32 = jnp.float32
BF = jnp.bfloat16

D = 1024
HEADS = 8
HD = D // HEADS
SGU_BLOCK = 128
N_CHIPS = 4
N_DEV = 8
EPS = 1e-6
LRU_C = 8.0
LANES = 128
SUBLANES = 8

ADAM_LR = 0.001
ADAM_B1 = 0.9
ADAM_B2 = 0.999
ADAM_EPS = 1e-08
ADAM_WD = 0.01
ADAM_STEP = 10

GELU_K0 = 0.7978845608028654
GELU_K1 = 0.044715

HBM_SPEC = pl.BlockSpec(memory_space=pltpu.HBM)
MESH_ID = pl.DeviceIdType.MESH


def _pcall(body, *, name, out_shape, grid=(), in_specs=None, out_specs=None, scratch=(), vmem_mb=32, aliases=None,
           grid_spec=None):
    kw = {}
    if aliases:
        kw["input_output_aliases"] = aliases
    if grid_spec is not None:
        kw["grid_spec"] = grid_spec
        ndim = len(grid_spec.grid)
    else:
        kw.update(grid=grid, in_specs=in_specs, out_specs=out_specs, scratch_shapes=list(scratch))
        ndim = len(grid)
    if ndim:
        params = pltpu.CompilerParams(dimension_semantics=("arbitrary",) * ndim, vmem_limit_bytes=vmem_mb * 2 ** 20)
    else:
        params = pltpu.CompilerParams(vmem_limit_bytes=vmem_mb * 2 ** 20)
    return pl.pallas_call(body, name=name, out_shape=out_shape, compiler_params=params, **kw)


def _gelu_cdf(x, x2):
    return 0.5 * jnp.tanh(x * (GELU_K0 + (GELU_K0 * GELU_K1) * x2)) + 0.5


def _gelu(x):
    return x * _gelu_cdf(x, x * x)


def _gelu_and_grad(x):
    x2 = x * x
    s = _gelu_cdf(x, x2)
    g = x * s
    dg = s * (1.0 + (x - g) * ((2.0 * GELU_K0) + (6.0 * GELU_K0 * GELU_K1) * x2))
    return g, dg


def _sigmoid(x):
    return 1.0 / (1.0 + jnp.exp(-x))


def _sigmoid_t(x):
    return 0.5 * jnp.tanh(0.5 * x) + 0.5


def _log_sigmoid(x):
    e = jnp.exp(-jnp.abs(x))
    u = 1.0 + e
    d = u - 1.0
    l1p = jnp.where(d == 0.0, e, jnp.log(u) * (e / jnp.where(d == 0.0, 1.0, d)))
    return jnp.minimum(x, 0.0) - l1p


def _dot(a, b):
    return jnp.dot(a, b, preferred_element_type=F32)


def _dot_nt(a, b):
    return lax.dot_general(a, b, (((1,), (1,)), ((), ())), preferred_element_type=F32)


def _dot_tn(a, b):
    return lax.dot_general(a, b, (((0,), (0,)), ((), ())), preferred_element_type=F32)


def _shift_down(x, halo, s):
    r = pltpu.roll(x, s, 0)
    rows = lax.broadcasted_iota(jnp.int32, (SUBLANES, x.shape[1]), 0)
    head = jnp.where(rows < s, pltpu.roll(halo, s, 0), r[0:SUBLANES])
    return jnp.concatenate([head, r[SUBLANES:]], axis=0)


def _shift_up(x, halo, s):
    n = x.shape[0]
    r = pltpu.roll(x, n - s, 0)
    rows = lax.broadcasted_iota(jnp.int32, (SUBLANES, x.shape[1]), 0)
    tail = jnp.where(rows >= SUBLANES - s, pltpu.roll(halo, SUBLANES - s, 0), r[n - SUBLANES:n])
    return jnp.concatenate([r[:n - SUBLANES], tail], axis=0)


def _scan_rows(a, u, reverse):
    n, width = a.shape
    rows = lax.broadcasted_iota(jnp.int32, (n, width), 0)
    d = 1
    while d < n:
        if d < SUBLANES:
            keep = rows < n - d if reverse else rows >= d
            shift = n - d if reverse else d
            a_s = jnp.where(keep, pltpu.roll(a, shift, 0), 1.0)
            u_s = jnp.where(keep, pltpu.roll(u, shift, 0), 0.0)
        elif reverse:
            a_s = jnp.concatenate([a[d:], jnp.ones((d, width), a.dtype)], axis=0)
            u_s = jnp.concatenate([u[d:], jnp.zeros((d, width), u.dtype)], axis=0)
        else:
            a_s = jnp.concatenate([jnp.ones((d, width), a.dtype), a[:n - d]], axis=0)
            u_s = jnp.concatenate([jnp.zeros((d, width), u.dtype), u[:n - d]], axis=0)
        u = a * u_s + u
        a = a * a_s
        d *= 2
    return a, u


def _colsum(x):
    return jnp.sum(x, axis=0, keepdims=True)


def _rms_stats(x):
    r = lax.rsqrt(jnp.mean(x * x, axis=-1, keepdims=True) + EPS)
    return r, x * r


def _lru_gates(xc, wa_ref, ba, wx_ref, bx, lam, head0=0):
    pr, pi = [], []
    for hh in range(xc.shape[1] // HD):
        xh = xc[:, hh * HD:(hh + 1) * HD].astype(BF)
        pr.append(_dot(xh, wa_ref[head0 + hh].astype(BF)))
        pi.append(_dot(xh, wx_ref[head0 + hh].astype(BF)))
    r = _sigmoid_t((pr[0] if len(pr) == 1 else jnp.concatenate(pr, axis=1)) + ba)
    ig = _sigmoid_t((pi[0] if len(pi) == 1 else jnp.concatenate(pi, axis=1)) + bx)
    ls = _log_sigmoid(lam)
    log_a = LRU_C * r * ls
    a = jnp.exp(log_a)
    x2 = 2.0 * log_a
    u = a * a
    lu = jnp.log(jnp.maximum(u, 1e-37))
    ratio = x2 * pl.reciprocal(jnp.where(lu == 0.0, 1.0, lu), approx=True)
    em1 = jnp.where(lu == 0.0, x2, jnp.where(u < 1e-30, -1.0, (u - 1.0) * ratio))
    mult = jnp.sqrt(-em1)
    return r, ig, ls, a, mult


def _sgu_mix(vln, ws_ref, bst_ref, tb):
    ri = lax.broadcasted_iota(jnp.int32, (SGU_BLOCK, SGU_BLOCK), 0)
    ci = lax.broadcasted_iota(jnp.int32, (SGU_BLOCK, SGU_BLOCK), 1)
    wm = [jnp.where(ri >= ci, ws_ref[g], 0.0).astype(BF) for g in range(HEADS)]
    blocks = []
    for blk in range(tb // SGU_BLOCK):
        cols = []
        for g in range(HEADS):
            vb = vln[blk * SGU_BLOCK:(blk + 1) * SGU_BLOCK, g * HD:(g + 1) * HD].astype(BF)
            cols.append(_dot(wm[g], vb) + bst_ref[:, g:g + 1])
        blocks.append(jnp.concatenate(cols, axis=1))
    mixed = blocks[0] if len(blocks) == 1 else jnp.concatenate(blocks, axis=0)
    return wm, mixed


def _layernorm_stats(v):
    mu = jnp.mean(v, axis=-1, keepdims=True)
    vc = v - mu
    rstd = lax.rsqrt(jnp.mean(vc * vc, axis=-1, keepdims=True) + EPS)
    return rstd, vc * rstd


def _my_xyc():
    return lax.axis_index("x"), lax.axis_index("y"), lax.axis_index("c")


def _gather_weights(srcs, halve):
    n = len(srcs)
    out_shape = [jax.ShapeDtypeStruct((N_CHIPS,) + s.shape, s.dtype) for s in srcs]

    def body(*refs):
        src, out = refs[:n], refs[n:2 * n]
        send_sems, recv_sems, fwd_send, fwd_recv, loc_sems = refs[2 * n:]
        x, y, c = _my_xyc()
        me = 2 * x + y
        chips = [(1 - x, y), (x, 1 - y), (1 - x, 1 - y)]

        def half(ref, a, which):
            if not halve[a]:
                return ref
            h = srcs[a].shape[0] // 2
            return ref.at[pl.ds(which * h, h)]

        def ici(a, k, frm):
            px, py = chips[k]
            return pltpu.make_async_remote_copy(
                src_ref=half(src[a], a, c), dst_ref=half(out[a].at[frm], a, c),
                send_sem=send_sems.at[a, k], recv_sem=recv_sems.at[a, k],
                device_id=(px, py, c), device_id_type=MESH_ID)

        def d2d(a, k, which):
            px, py = chips[k]
            rows = half(out[a].at[2 * px + py], a, which)
            return pltpu.make_async_remote_copy(
                src_ref=rows, dst_ref=rows, send_sem=fwd_send.at[a, k], recv_sem=fwd_recv.at[a, k],
                device_id=(x, y, 1 - c), device_id_type=MESH_ID)

        local, sends = [], []
        for a in range(n):
            lc = pltpu.make_async_copy(src[a], out[a].at[me], loc_sems.at[a])
            lc.start()
            local.append(lc)
            for k in range(3):
                cp = ici(a, k, me)
                cp.start()
                sends.append(cp)
        for a in range(n):
            for k in range(3):
                px, py = chips[k]
                ici(a, k, 2 * px + py).wait_recv()
                if halve[a]:
                    fw = d2d(a, k, c)
                    fw.start()
                    sends.append(fw)
        for a in range(n):
            if halve[a]:
                for k in range(3):
                    d2d(a, k, 1 - c).wait_recv()
        for cp in sends:
            cp.wait_send()
        for lc in local:
            lc.wait()

    sem = pltpu.SemaphoreType.DMA((n, 3))
    return _pcall(body, name="gather_weights", out_shape=out_shape, in_specs=[HBM_SPEC] * n,
                  out_specs=[HBM_SPEC] * n, scratch=[sem, sem, sem, sem, pltpu.SemaphoreType.DMA((n,))])(*srcs)


SEM_SPEC = pl.BlockSpec(memory_space=pltpu.SEMAPHORE)


def _remote_start(srcs, lands, plan, ncopies, name):
    n, m = len(srcs), len(lands)

    def body(*refs):
        src, land = refs[:n], refs[n:n + m]
        send_sems, recv_sems = refs[n + m], refs[n + m + 1]
        token = refs[-1]
        x, y, c = _my_xyc()
        for i, (s, d, dev) in enumerate(plan(src, land, x, y, c)):
            pltpu.make_async_remote_copy(src_ref=s, dst_ref=d, send_sem=send_sems.at[i], recv_sem=recv_sems.at[i],
                                         device_id=dev, device_id_type=MESH_ID).start()
        token[...] = jnp.zeros_like(token)

    bufs = list(srcs) + list(lands)
    out = pl.pallas_call(
        body, name=name,
        out_shape=(pltpu.SemaphoreType.DMA((ncopies,)), pltpu.SemaphoreType.DMA((ncopies,)),
                   *[pltpu.HBM(b.shape, b.dtype) for b in bufs], jax.ShapeDtypeStruct((SUBLANES, LANES), F32)),
        in_specs=[HBM_SPEC] * (n + m),
        out_specs=(SEM_SPEC, SEM_SPEC, *[HBM_SPEC] * (n + m), pl.BlockSpec(memory_space=pltpu.VMEM)),
        input_output_aliases={i: 2 + i for i in range(n + m)},
        compiler_params=pltpu.CompilerParams(has_side_effects=pltpu.SideEffectType.DATAFLOW_SIDE_EFFECTING),
    )(*[pltpu.with_memory_space_constraint(b, pltpu.HBM) for b in bufs])
    return (out[0], out[1], out[2:2 + n], out[2 + n:2 + n + m]), out[-1]


def _remote_wait(handle, plan, after, name):
    send_sems, recv_sems, srcs, lands = handle
    n, m = len(srcs), len(lands)

    def body(*refs):
        src, land = refs[:n], refs[n:n + m]
        ssem, rsem = refs[n + m], refs[n + m + 1]
        x, y, c = _my_xyc()
        for i, (s, d, dev) in enumerate(plan(src, land, x, y, c)):
            cp = pltpu.make_async_remote_copy(src_ref=s, dst_ref=d, send_sem=ssem.at[i], recv_sem=rsem.at[i],
                                              device_id=dev, device_id_type=MESH_ID)
            cp.wait_send()
            cp.wait_recv()

    bufs = list(srcs) + list(lands)
    out = pl.pallas_call(
        body, name=name, out_shape=tuple(pltpu.HBM(b.shape, b.dtype) for b in bufs),
        in_specs=[HBM_SPEC] * (n + m) + [SEM_SPEC, SEM_SPEC, pl.BlockSpec(memory_space=pl.ANY)],
        out_specs=tuple([HBM_SPEC] * (n + m)), input_output_aliases={i: i for i in range(n + m)},
        compiler_params=pltpu.CompilerParams(has_side_effects=pltpu.SideEffectType.DATAFLOW_SIDE_EFFECTING),
    )(*bufs, send_sems, recv_sems, after)
    return out[:n], out[n:]


def _chips_of(x, y):
    return [(1 - x, y), (x, 1 - y), (1 - x, 1 - y)]


def _gather_plan(count):
    def plan(src, land, x, y, c):
        me = 2 * x + y
        return [(src[a], land[a].at[me], (px, py, c)) for a in range(count) for px, py in _chips_of(x, y)]

    return plan


def _place_own(srcs, lands):
    n = len(srcs)

    def body(*refs):
        src, land, sems = refs[:n], refs[2 * n:3 * n], refs[3 * n]
        me = 2 * lax.axis_index("x") + lax.axis_index("y")
        copies = [pltpu.make_async_copy(src[a], land[a].at[me], sems.at[a]) for a in range(n)]
        for cp in copies:
            cp.start()
        for cp in copies:
            cp.wait()

    return _pcall(body, name="place_own_shards", out_shape=[jax.ShapeDtypeStruct(l.shape, l.dtype) for l in lands],
                  in_specs=[HBM_SPEC] * (2 * n), out_specs=[HBM_SPEC] * n, aliases={n + a: a for a in range(n)},
                  scratch=[pltpu.SemaphoreType.DMA((n,))])(*srcs, *lands)


def _gather8(src, name):
    def body(src_ref, out_ref, send_sems, recv_sems, loc_sem):
        x, y, c = _my_xyc()
        me = 4 * x + 2 * y + c
        lc = pltpu.make_async_copy(src_ref, out_ref.at[me], loc_sem)
        lc.start()
        cps = []
        for k in range(1, N_DEV):
            px = 1 - x if (k >> 2) & 1 else x
            py = 1 - y if (k >> 1) & 1 else y
            pc = 1 - c if k & 1 else c
            cp = pltpu.make_async_remote_copy(
                src_ref=src_ref, dst_ref=out_ref.at[me], send_sem=send_sems.at[k - 1], recv_sem=recv_sems.at[k - 1],
                device_id=(px, py, pc), device_id_type=MESH_ID)
            cp.start()
            cps.append(cp)
        for cp in cps:
            cp.wait()
        lc.wait()

    return _pcall(body, name=name, out_shape=jax.ShapeDtypeStruct((N_DEV,) + src.shape, src.dtype),
                  in_specs=[HBM_SPEC], out_specs=HBM_SPEC,
                  scratch=[pltpu.SemaphoreType.DMA((N_DEV - 1,)), pltpu.SemaphoreType.DMA((N_DEV - 1,)),
                           pltpu.SemaphoreType.DMA])(src)


def _cast_shards(arrs, name, after=None):
    n = len(arrs)
    extra = [] if after is None else [after]

    def body(*refs):
        ins, outs = refs[:n], refs[n + len(extra):]
        for a in range(n):
            outs[a][...] = ins[a][...].astype(BF)

    specs = [pl.BlockSpec((s.shape[0] // 4, s.shape[1]), lambda i: (i, 0)) for s in arrs]
    return _pcall(body, name=name, grid=(4,), in_specs=specs + [pl.BlockSpec(memory_space=pl.ANY)] * len(extra),
                  out_specs=specs, out_shape=[jax.ShapeDtypeStruct(s.shape, BF) for s in arrs])(*arrs, *extra)


def _row_tile(rows, cols):
    t = rows
    while t * cols * 4 > (3 << 19) and t % 16 == 0:
        t //= 2
    return t


def _sum_parts(parts, name):
    p, rows, cols = parts.shape
    tr = _row_tile(rows, cols * p // 2)

    def body(p_ref, o_ref):
        acc = p_ref[0].astype(F32)
        for k in range(1, p):
            acc = acc + p_ref[k].astype(F32)
        o_ref[...] = acc

    return _pcall(body, name=name, grid=(rows // tr,),
                  in_specs=[pl.BlockSpec((p, tr, cols), lambda i: (0, i, 0))],
                  out_specs=pl.BlockSpec((tr, cols), lambda i: (i, 0)),
                  out_shape=jax.ShapeDtypeStruct((rows, cols), F32), vmem_mb=48)(parts)


def _sum_own_and_landed(chip, sums, landed, name):
    _, rows, cols = sums.shape
    tr = _row_tile(rows, 2 * cols)

    def body(chip_ref, own_ref, land_ref, o_ref):
        del chip_ref
        acc = own_ref[0].astype(F32)
        for k in range(3):
            acc = acc + land_ref[k].astype(F32)
        o_ref[...] = acc

    grid_spec = pltpu.PrefetchScalarGridSpec(
        num_scalar_prefetch=1, grid=(rows // tr,),
        in_specs=[pl.BlockSpec((1, tr, cols), lambda i, chip_ref: (chip_ref[0], i, 0)),
                  pl.BlockSpec((3, tr, cols), lambda i, chip_ref: (0, i, 0))],
        out_specs=pl.BlockSpec((tr, cols), lambda i, chip_ref: (i, 0)))
    return _pcall(body, name=name, grid_spec=grid_spec, out_shape=jax.ShapeDtypeStruct((rows, cols), F32),
                  vmem_mb=48)(chip, sums, landed)


def _swap_cores(arrs, name):
    n = len(arrs)

    def body(*refs):
        src, out = refs[:n], refs[n:2 * n]
        send_sems, recv_sems = refs[2 * n:]
        x, y, c = _my_xyc()
        cps = []
        for a in range(n):
            cp = pltpu.make_async_remote_copy(
                src_ref=src[a], dst_ref=out[a], send_sem=send_sems.at[a], recv_sem=recv_sems.at[a],
                device_id=(x, y, 1 - c), device_id_type=MESH_ID)
            cp.start()
            cps.append(cp)
        for cp in cps:
            cp.wait()

    sem = pltpu.SemaphoreType.DMA((n,))
    return _pcall(body, name=name, out_shape=[jax.ShapeDtypeStruct(a.shape, a.dtype) for a in arrs],
                  in_specs=[HBM_SPEC] * n, out_specs=[HBM_SPEC] * n, scratch=[sem, sem])(*arrs)


def _add_pair(a, b, name):
    rows, cols = a.shape
    tr = _row_tile(rows, 2 * cols)

    def body(a_ref, b_ref, o_ref):
        o_ref[...] = a_ref[...] + b_ref[...]

    spec = pl.BlockSpec((tr, cols), lambda i: (i, 0))
    return _pcall(body, name=name, grid=(rows // tr,), in_specs=[spec, spec], out_specs=spec,
                  out_shape=jax.ShapeDtypeStruct((rows, cols), F32))(a, b)


def _sum_chips_in_order(chip, own, landed, name):
    rows, cols = own.shape
    tr = _row_tile(rows, 4 * cols)

    def body(chip_ref, own_ref, land_ref, o_ref):
        me = chip_ref[0]
        acc = None
        for p in range(N_CHIPS):
            q = p ^ me
            k = jnp.where(q == 2, 0, jnp.where(q == 1, 1, 2))
            term = jnp.where(q == 0, own_ref[...], land_ref[k])
            acc = term if acc is None else acc + term
        o_ref[...] = acc

    grid_spec = pltpu.PrefetchScalarGridSpec(
        num_scalar_prefetch=1, grid=(rows // tr,),
        in_specs=[pl.BlockSpec((tr, cols), lambda i, chip_ref: (i, 0)),
                  pl.BlockSpec((3, tr, cols), lambda i, chip_ref: (0, i, 0))],
        out_specs=pl.BlockSpec((tr, cols), lambda i, chip_ref: (i, 0)))
    return _pcall(body, name=name, grid_spec=grid_spec, out_shape=jax.ShapeDtypeStruct((rows, cols), F32))(
        chip, own, landed)


def _bcast_plan(src, land, x, y, c):
    return [(src[0], land[0].at[k], (px, py, c)) for k, (px, py) in enumerate(_chips_of(x, y))]


def _scatter_plan(count):
    def plan(src, land, x, y, c):
        out = []
        for a in range(count):
            for k, (px, py) in enumerate(_chips_of(x, y)):
                out.append((src[a].at[2 * px + py], land[a].at[k], (px, py, c)))
        return out

    return plan


def _adamw_math(w, g, m, v):
    m2 = ADAM_B1 * m + (1.0 - ADAM_B1) * g
    v2 = ADAM_B2 * v + (1.0 - ADAM_B2) * (g * g)
    m_hat = m2 / (1.0 - ADAM_B1 ** ADAM_STEP)
    v_hat = v2 / (1.0 - ADAM_B2 ** ADAM_STEP)
    delta = -ADAM_LR * (m_hat / (jnp.sqrt(v_hat) + ADAM_EPS) + ADAM_WD * w)
    return delta, m2, v2


def _adamw(w, m, v, grads, name):
    rows, cols = w.shape
    tr = _row_tile(rows, cols)
    ng = len(grads)

    def body(*refs):
        w_ref, m_ref, v_ref = refs[:3]
        g = refs[3][...]
        for k in range(1, ng):
            g = g + refs[3 + k][...]
        g_ref, d_ref, m2_ref, v2_ref = refs[3 + ng:]
        delta, m2, v2 = _adamw_math(w_ref[...], g, m_ref[...], v_ref[...])
        g_ref[...] = g
        d_ref[...] = delta
        m2_ref[...] = m2
        v2_ref[...] = v2

    spec = pl.BlockSpec((tr, cols), lambda i: (i, 0))
    return _pcall(body, name=name, grid=(rows // tr,), in_specs=[spec] * (3 + ng), out_specs=[spec] * 4,
                  out_shape=[jax.ShapeDtypeStruct((rows, cols), F32)] * 4, vmem_mb=48)(w, m, v, *grads)


def _ada_adamw(ct, dmod, w, m, v):
    rows, cols = w.shape
    tr = _row_tile(rows, cols)

    def body(ct_ref, dm_ref, w_ref, m_ref, v_ref, g_ref, d_ref, m2_ref, v2_ref):
        cv = ct_ref[...]
        ca = cv * _sigmoid(cv)
        g = ca[:, 0:1] * dm_ref[0:1, :]
        for b in range(1, N_DEV):
            g = g + ca[:, b:b + 1] * dm_ref[b:b + 1, :]
        delta, m2, v2 = _adamw_math(w_ref[...], g, m_ref[...], v_ref[...])
        g_ref[...] = g
        d_ref[...] = delta
        m2_ref[...] = m2
        v2_ref[...] = v2

    spec = pl.BlockSpec((tr, cols), lambda i: (i, 0))
    return _pcall(body, name="ada_adamw", grid=(rows // tr,),
                  in_specs=[pl.BlockSpec((tr, N_DEV), lambda i: (i, 0)), pl.BlockSpec((N_DEV, cols), lambda i: (0, 0)),
                            spec, spec, spec],
                  out_specs=[spec] * 4, out_shape=[jax.ShapeDtypeStruct((rows, cols), F32)] * 4,
                  vmem_mb=48)(ct, dmod, w, m, v)


def _mod_fwd(c_all, w, b):
    cols = w.shape[1]
    tn = cols // 3

    def body(c_ref, w_ref, b_ref, o_ref):
        cv = c_ref[...]
        ca = (cv * _sigmoid(cv)).astype(BF)
        o_ref[...] = _dot(ca, w_ref[...].astype(BF)) + b_ref[...]

    return _pcall(body, name="mod_fwd", grid=(3,),
                  in_specs=[pl.BlockSpec((N_DEV, D), lambda j: (0, 0)), pl.BlockSpec((D, tn), lambda j: (0, j)),
                            pl.BlockSpec((1, tn), lambda j: (0, j))],
                  out_specs=pl.BlockSpec((N_DEV, tn), lambda j: (0, j)),
                  out_shape=jax.ShapeDtypeStruct((N_DEV, cols), F32))(c_all, w, b)


def _resident(shape):
    zeros = (0,) * len(shape)
    return pl.BlockSpec(shape, lambda *_: zeros, pipeline_mode=pl.Buffered(1))


def _mixer_fwd(x, g, scale, shift, gate1, w_in4, cw, cb, wa, ba, wx, bx, lam, lg, lb, ws, bst, wba, wbb, wo,
               tm=256, chunk=256, piece=512):
    T = x.shape[0]
    tm = min(tm, T)
    ns = w_in4.shape[2]
    per = ns // piece

    def body(x_ref, g_ref, sc_ref, sh_ref, g1_ref, w_ref, cw_ref, cb_ref, wa_ref, ba_ref, wx_ref, bx_ref, lam_ref,
             lg_ref, lb_ref, ws_ref, bst_ref, wba_ref, wbb_ref, wo_ref,
             h1_ref, z_ref, hl_ref, yap_ref, ybp_ref, mg_ref, ya_ref, yb_ref, o_ref, x2_ref, prev, hc):
        i = pl.program_id(0)

        @pl.when(i == 0)
        def _():
            prev[...] = jnp.zeros_like(prev)
            hc[...] = jnp.zeros_like(hc)

        xv = x_ref[...]
        _, xh = _rms_stats(xv)
        h = ((xh * g_ref[...]) * (1.0 + sc_ref[...]) + sh_ref[...]).astype(BF)
        h1_ref[...] = h

        def proj(col, width):
            for c0 in range(col, col + width, piece):
                w = min(piece, col + width - c0)
                j, off = c0 // ns, c0 % ns
                z_ref[:, c0:c0 + w] = _dot(h, w_ref[j, :, off:off + w])

        def lru_chunk(c0):
            cs = slice(c0, c0 + chunk)
            xr = z_ref[:, cs]
            pv = prev[:, cs]
            xc = (cb_ref[:, cs] + cw_ref[3:4, cs] * xr + cw_ref[2:3, cs] * _shift_down(xr, pv, 1)
                  + cw_ref[1:2, cs] * _shift_down(xr, pv, 2) + cw_ref[0:1, cs] * _shift_down(xr, pv, 3))
            prev[:, cs] = xr[tm - SUBLANES:tm]
            _, ig, _, a, mult = _lru_gates(xc, wa_ref, ba_ref[:, cs], wx_ref, bx_ref[:, cs], lam_ref[:, cs],
                                           head0=c0 // HD)
            a, u = _scan_rows(a, mult * (ig * xc), reverse=False)
            hv = u + a * hc[SUBLANES - 1:SUBLANES, cs]
            hc[:, cs] = hv[tm - SUBLANES:tm]
            hl_ref[:, cs] = hv
            yap_ref[:, cs] = (hv * _gelu(z_ref[:, D + c0:D + c0 + chunk])).astype(BF)

        proj(0, chunk)
        proj(D, chunk)
        for c0 in range(0, D, chunk):
            if c0 + chunk < D:
                proj(c0 + chunk, chunk)
                proj(D + c0 + chunk, chunk)
            else:
                proj(2 * D, 2 * D)
            lru_chunk(c0)
        proj(4 * D, 2 * D)
        _, xhn = _layernorm_stats(_gelu(z_ref[:, 3 * D:4 * D]))
        vln = xhn * lg_ref[...] + lb_ref[...]
        _, mixed = _sgu_mix(vln, ws_ref, bst_ref, tm)
        ybp = (_gelu(z_ref[:, 2 * D:3 * D]) * mixed).astype(BF)
        ybp_ref[...] = ybp
        ya = _dot(yap_ref[...], wba_ref[...])
        yb = _dot(ybp, wbb_ref[...])
        merged = (_sigmoid_t(z_ref[:, 4 * D:5 * D]) * ya + _sigmoid_t(z_ref[:, 5 * D:6 * D]) * yb).astype(BF)
        o = _dot(merged, wo_ref[...])
        x2_ref[...] = xv + g1_ref[...] * o
        mg_ref[...] = merged
        ya_ref[...] = ya.astype(BF)
        yb_ref[...] = yb.astype(BF)
        o_ref[...] = o.astype(BF)

    row = pl.BlockSpec((tm, D), lambda i: (i, 0))
    vec = pl.BlockSpec((1, D), lambda i: (0, 0))
    bf_row = jax.ShapeDtypeStruct((T, D), BF)
    f32_row = jax.ShapeDtypeStruct((T, D), F32)
    return _pcall(body, name="mixer_fwd", grid=(T // tm,),
                  in_specs=[row, vec, vec, vec, vec, _resident(w_in4.shape), _resident(cw.shape), vec,
                            _resident(wa.shape), vec, _resident(wx.shape), vec, vec, vec, vec,
                            _resident(ws.shape), _resident(bst.shape),
                            _resident(wba.shape), _resident(wbb.shape), _resident(wo.shape)],
                  out_specs=[row, pl.BlockSpec((tm, 6 * D), lambda i: (i, 0)), row, row, row, row, row, row, row, row],
                  out_shape=[bf_row, jax.ShapeDtypeStruct((T, 6 * D), F32), f32_row, bf_row, bf_row, bf_row, bf_row,
                             bf_row, bf_row, f32_row],
                  scratch=[pltpu.VMEM((SUBLANES, D), F32), pltpu.VMEM((SUBLANES, D), F32)], vmem_mb=60)(
        x, g, scale, shift, gate1, w_in4, cw, cb, wa, ba, wx, bx, lam, lg, lb, ws, bst, wba, wbb, wo)


def _ffn_fwd(x2, g, scale, shift, gate2, gf, w_up4, wd, cw, cb, target, tm=256, chunk=768):
    T = x2.shape[0]
    tm = min(tm, T)
    ns = w_up4.shape[2]
    dff = wd.shape[0]
    nchunk = dff // chunk
    per = ns // chunk

    def body(x2_ref, g_ref, sc_ref, sh_ref, g2_ref, gf_ref, wu_ref, wd_ref, cw_ref, cb_ref, t_ref,
             h2_ref, up_ref, f_ref, ga_ref, vd_ref, loss_ref, dx3_ref, dfo_ref, dgf_ref, dg2_ref, prev):
        i = pl.program_id(0)

        @pl.when(i == 0)
        def _():
            prev[...] = jnp.zeros_like(prev)
            loss_ref[...] = jnp.zeros_like(loss_ref)
            dgf_ref[...] = jnp.zeros_like(dgf_ref)
            dg2_ref[...] = jnp.zeros_like(dg2_ref)

        x2v = x2_ref[...]
        _, xh2 = _rms_stats(x2v)
        h2 = ((xh2 * g_ref[...]) * (1.0 + sc_ref[...]) + sh_ref[...]).astype(BF)
        h2_ref[...] = h2

        def conv(u, col):
            cs = slice(col, col + chunk)
            p = prev[:, cs]
            hid = (cb_ref[:, cs] + cw_ref[2:3, cs] * u + cw_ref[1:2, cs] * _shift_down(u, p, 1)
                   + cw_ref[0:1, cs] * _shift_down(u, p, 2))
            prev[:, cs] = u[tm - SUBLANES:tm]
            up_ref[:, cs] = u.astype(BF)
            return hid

        def up_proj(k):
            off = (k % per) * chunk
            return (_dot(h2, wu_ref[k // per, :, off:off + chunk]),
                    _dot(h2, wu_ref[N_CHIPS // 2 + k // per, :, off:off + chunk]))

        fo = None
        nxt = up_proj(0)
        for k in range(nchunk):
            col = k * chunk
            ua, uv = nxt
            if k + 1 < nchunk:
                nxt = up_proj(k + 1)
            act = conv(ua, col)
            val = conv(uv, dff + col)
            ga, dga = _gelu_and_grad(act)
            fk = (ga * val).astype(BF)
            f_ref[:, col:col + chunk] = fk
            ga_ref[:, col:col + chunk] = ga.astype(BF)
            vd_ref[:, col:col + chunk] = (val * dga).astype(BF)
            part = _dot(fk, wd_ref[col:col + chunk, :])
            fo = part if fo is None else fo + part

        x3 = x2v + g2_ref[...] * fo
        rstd, xh = _rms_stats(x3)
        err = xh * gf_ref[...] - t_ref[...]
        loss_ref[...] += 0.5 * jnp.sum(jnp.mean(err * err, axis=-1, keepdims=True), axis=0, keepdims=True)
        dy = err * (1.0 / D)
        dgf_ref[...] += _colsum(dy * xh)
        dxh = dy * gf_ref[...]
        dx3 = rstd * (dxh - xh * jnp.mean(dxh * xh, axis=-1, keepdims=True))
        dg2_ref[...] += _colsum(dx3 * fo)
        dx3_ref[...] = dx3
        dfo_ref[...] = (g2_ref[...] * dx3).astype(BF)

    row = pl.BlockSpec((tm, D), lambda i: (i, 0))
    vec = pl.BlockSpec((1, D), lambda i: (0, 0))
    wide = pl.BlockSpec((tm, 2 * dff), lambda i: (i, 0))
    half = pl.BlockSpec((tm, dff), lambda i: (i, 0))
    return _pcall(body, name="ffn_fwd", grid=(T // tm,),
                  in_specs=[row, vec, vec, vec, vec, vec, _resident(w_up4.shape), _resident(wd.shape),
                            _resident(cw.shape), _resident(cb.shape), row],
                  out_specs=[row, wide, half, half, half, pl.BlockSpec((1, LANES), lambda i: (0, 0)), row, row, vec, vec],
                  out_shape=[jax.ShapeDtypeStruct((T, D), BF), jax.ShapeDtypeStruct((T, 2 * dff), BF),
                             jax.ShapeDtypeStruct((T, dff), BF), jax.ShapeDtypeStruct((T, dff), BF),
                             jax.ShapeDtypeStruct((T, dff), BF), jax.ShapeDtypeStruct((1, LANES), F32),
                             jax.ShapeDtypeStruct((T, D), F32), jax.ShapeDtypeStruct((T, D), BF),
                             jax.ShapeDtypeStruct((1, D), F32), jax.ShapeDtypeStruct((1, D), F32)],
                  scratch=[pltpu.VMEM((SUBLANES, 2 * dff), F32)], vmem_mb=56)(
        x2, g, scale, shift, gate2, gf, w_up4, wd, cw, cb, target)


def _ffn_bwd(dfo, wd, up, ga, vd, cw, w_up4, x2, resid, g, scale, gate, o, tm=256, chunk=1536):
    T = up.shape[0]
    tm = min(tm, T)
    dff = wd.shape[0]
    ns = w_up4.shape[2]
    nchunk = dff // chunk
    per = ns // chunk
    nrow = T // tm

    def body(dfo_ref, wd_ref, up_ref, ga_ref, vd_ref, cw_ref, wu_ref, x_ref, r_ref, g_ref, sc_ref, gt_ref, o_ref,
             du_ref, dcw_ref, dcb_ref, dx_ref, dsh_ref, dsc_ref, dg_ref, do_ref, dgt_ref, nxt):
        i = pl.program_id(0)

        @pl.when(i == 0)
        def _():
            nxt[...] = jnp.zeros_like(nxt)
            for ref in (dcw_ref, dcb_ref, dsh_ref, dsc_ref, dg_ref, dgt_ref):
                ref[...] = jnp.zeros_like(ref)

        dfo_t = dfo_ref[...]

        def conv_bwd(dh, col):
            cs = slice(col, col + chunk)
            n8 = nxt[:, cs]
            dh1 = _shift_up(dh, n8, 1)
            dh2 = _shift_up(dh, n8, 2)
            nxt[:, cs] = dh[0:SUBLANES]
            du = (cw_ref[2:3, cs] * dh + cw_ref[1:2, cs] * dh1 + cw_ref[0:1, cs] * dh2).astype(BF)
            du_ref[:, cs] = du
            u = up_ref[:, cs].astype(F32)
            dcw_ref[2:3, cs] += _colsum(dh * u)
            dcw_ref[1:2, cs] += _colsum(dh1 * u)
            dcw_ref[0:1, cs] += _colsum(dh2 * u)
            dcb_ref[:, cs] += _colsum(dh)
            return du

        def down_bwd(k):
            return _dot_nt(dfo_t, wd_ref[k * chunk:(k + 1) * chunk, :])

        dh = None
        df_next = down_bwd(0)
        for k in range(nchunk):
            col = k * chunk
            off = (k % per) * chunk
            df = df_next
            if k + 1 < nchunk:
                df_next = down_bwd(k + 1)
            du_a = conv_bwd(df * vd_ref[:, col:col + chunk].astype(F32), col)
            du_v = conv_bwd(df * ga_ref[:, col:col + chunk].astype(F32), dff + col)
            part = (_dot_nt(du_a, wu_ref[k // per, :, off:off + chunk])
                    + _dot_nt(du_v, wu_ref[N_CHIPS // 2 + k // per, :, off:off + chunk]))
            dh = part if dh is None else dh + part

        rstd, xh = _rms_stats(x_ref[...])
        dsh_ref[...] += _colsum(dh)
        dsc_ref[...] += _colsum(dh * (xh * g_ref[...]))
        dn = dh * (1.0 + sc_ref[...])
        dg_ref[...] += _colsum(dn * xh)
        dxh = dn * g_ref[...]
        dx = r_ref[...] + rstd * (dxh - xh * jnp.mean(dxh * xh, axis=-1, keepdims=True))
        dx_ref[...] = dx
        do_ref[...] = (gt_ref[...] * dx).astype(BF)
        dgt_ref[...] += _colsum(dx * o_ref[...].astype(F32))

    rev = lambda i: (nrow - 1 - i, 0)
    row = pl.BlockSpec((tm, D), rev)
    vec = pl.BlockSpec((1, D), lambda i: (0, 0))
    wide = pl.BlockSpec((tm, 2 * dff), rev)
    half = pl.BlockSpec((tm, dff), rev)
    cw3 = pl.BlockSpec((3, 2 * dff), lambda i: (0, 0))
    cb1 = pl.BlockSpec((1, 2 * dff), lambda i: (0, 0))
    vshape = jax.ShapeDtypeStruct((1, D), F32)
    return _pcall(body, name="ffn_bwd", grid=(nrow,),
                  in_specs=[row, _resident(wd.shape), wide, half, half, _resident(cw.shape), _resident(w_up4.shape),
                            row, row, vec, vec, vec, row],
                  out_specs=[wide, cw3, cb1, row, vec, vec, vec, row, vec],
                  out_shape=[jax.ShapeDtypeStruct((T, 2 * dff), BF), jax.ShapeDtypeStruct((3, 2 * dff), F32),
                             jax.ShapeDtypeStruct((1, 2 * dff), F32), jax.ShapeDtypeStruct((T, D), F32),
                             vshape, vshape, vshape, jax.ShapeDtypeStruct((T, D), BF), vshape],
                  scratch=[pltpu.VMEM((SUBLANES, 2 * dff), F32)], vmem_mb=60)(
        dfo, wd, up, ga, vd, cw, w_up4, x2, resid, g, scale, gate, o)


def _mm_tn_cols(a, b, name, nshard, nb, mb=None, tm=1024):
    T, M = a.shape
    tm = min(tm, T)
    mb = M if mb is None else mb
    ns = b.shape[1] // nshard
    per = ns // nb
    nk = T // tm

    def body(a_ref, b_ref, o_ref, c_ref):
        k = pl.program_id(2)

        @pl.when(k == 0)
        def _():
            o_ref[...] = jnp.zeros_like(o_ref)

        o_ref[0] += _dot_tn(a_ref[...], b_ref[...])

        @pl.when(k == nk - 1)
        def _():
            c_ref[...] = o_ref[...].astype(BF)

    out_spec = pl.BlockSpec((1, mb, nb), lambda m, t, k: (t // per, m, t % per))
    return _pcall(body, name=name, grid=(M // mb, nshard * per, nk),
                  in_specs=[pl.BlockSpec((tm, mb), lambda m, t, k: (k, m)),
                            pl.BlockSpec((tm, nb), lambda m, t, k: (k, t))],
                  out_specs=[out_spec, out_spec],
                  out_shape=[jax.ShapeDtypeStruct((nshard, M, ns), F32), jax.ShapeDtypeStruct((nshard, M, ns), BF)],
                  vmem_mb=48)(a, b)


def _mm_nt_normbwd(dz, w4, x, resid, g, scale, name, tm=256):
    T = x.shape[0]
    tm = min(tm, T)
    ns = w4.shape[2]

    def body(dz_ref, w_ref, x_ref, r_ref, g_ref, sc_ref, dx_ref, dsh_ref, dsc_ref, dg_ref):
        i = pl.program_id(0)

        @pl.when(i == 0)
        def _():
            dsh_ref[...] = jnp.zeros_like(dsh_ref)
            dsc_ref[...] = jnp.zeros_like(dsc_ref)
            dg_ref[...] = jnp.zeros_like(dg_ref)

        dh = None
        for j in range(N_CHIPS):
            part = _dot_nt(dz_ref[:, j * ns:(j + 1) * ns], w_ref[j])
            dh = part if dh is None else dh + part
        rstd, xh = _rms_stats(x_ref[...])
        dsh_ref[...] += _colsum(dh)
        dsc_ref[...] += _colsum(dh * (xh * g_ref[...]))
        dn = dh * (1.0 + sc_ref[...])
        dg_ref[...] += _colsum(dn * xh)
        dxh = dn * g_ref[...]
        dx_ref[...] = r_ref[...] + rstd * (dxh - xh * jnp.mean(dxh * xh, axis=-1, keepdims=True))

    row = pl.BlockSpec((tm, D), lambda i: (i, 0))
    vec = pl.BlockSpec((1, D), lambda i: (0, 0))
    return _pcall(body, name=name, grid=(T // tm,),
                  in_specs=[pl.BlockSpec((tm, N_CHIPS * ns), lambda i: (i, 0)), _resident(w4.shape), row, row, vec, vec],
                  out_specs=[row, vec, vec, vec],
                  out_shape=[jax.ShapeDtypeStruct((T, D), F32)] + [jax.ShapeDtypeStruct((1, D), F32)] * 3,
                  vmem_mb=48)(dz, w4, x, resid, g, scale)


def _mix_bwd(do, ya, yb, z, wo, wba, wbb, tm=256):
    T = do.shape[0]
    tm = min(tm, T)

    def body(do_ref, ya_ref, yb_ref, ga_ref, gb_ref, wo_ref, wa_ref, wb_ref,
             dz_ref, dya_ref, dyb_ref, dyap_ref, dybp_ref):
        dm = _dot_nt(do_ref[...], wo_ref[...])
        sa = _sigmoid_t(ga_ref[...])
        sb = _sigmoid_t(gb_ref[...])
        dya = (sa * dm).astype(BF)
        dyb = (sb * dm).astype(BF)
        dz_ref[:, 0:D] = (dm * ya_ref[...].astype(F32) * sa * (1.0 - sa)).astype(BF)
        dz_ref[:, D:2 * D] = (dm * yb_ref[...].astype(F32) * sb * (1.0 - sb)).astype(BF)
        dya_ref[...] = dya
        dyb_ref[...] = dyb
        dyap_ref[...] = _dot_nt(dya, wa_ref[...]).astype(BF)
        dybp_ref[...] = _dot_nt(dyb, wb_ref[...]).astype(BF)

    row = pl.BlockSpec((tm, D), lambda i: (i, 0))
    wspec = pl.BlockSpec((D, D), lambda i: (0, 0))
    return _pcall(body, name="mix_bwd", grid=(T // tm,),
                  in_specs=[row, row, row, pl.BlockSpec((tm, D), lambda i: (i, 4)),
                            pl.BlockSpec((tm, D), lambda i: (i, 5)), wspec, wspec, wspec],
                  out_specs=[pl.BlockSpec((tm, 2 * D), lambda i: (i, 2)), row, row, row, row],
                  out_shape=[jax.ShapeDtypeStruct((T, 6 * D), BF)] + [jax.ShapeDtypeStruct((T, D), BF)] * 4,
                  vmem_mb=48)(do, ya, yb, z, z, wo, wba, wbb)


def _sgu_bwd(dz, dyb_pre, z, lg, lb, ws, bst, tb=256):
    T = z.shape[0]
    tb = min(tb, T)

    def body(dz_in, dy_ref, zu_ref, zv_ref, lg_ref, lb_ref, ws_ref, bst_ref,
             dz_ref, dws_ref, dbst_ref, dlg_ref, dlb_ref):
        del dz_in
        i = pl.program_id(0)

        @pl.when(i == 0)
        def _():
            dws_ref[...] = jnp.zeros_like(dws_ref)
            dbst_ref[...] = jnp.zeros_like(dbst_ref)
            dlg_ref[...] = jnp.zeros_like(dlg_ref)
            dlb_ref[...] = jnp.zeros_like(dlb_ref)

        gu, dgu = _gelu_and_grad(zu_ref[...])
        gv, dgv = _gelu_and_grad(zv_ref[...])
        rstd, xh = _layernorm_stats(gv)
        vln = xh * lg_ref[...] + lb_ref[...]
        wm, mixed = _sgu_mix(vln, ws_ref, bst_ref, tb)
        dy = dy_ref[...].astype(F32)
        dz_ref[:, 0:D] = (dy * mixed * dgu).astype(BF)
        dmixed = dy * gu
        ri = lax.broadcasted_iota(jnp.int32, (SGU_BLOCK, SGU_BLOCK), 0)
        ci = lax.broadcasted_iota(jnp.int32, (SGU_BLOCK, SGU_BLOCK), 1)
        blocks = []
        for blk in range(tb // SGU_BLOCK):
            rs = slice(blk * SGU_BLOCK, (blk + 1) * SGU_BLOCK)
            cols = []
            for g in range(HEADS):
                cs = slice(g * HD, (g + 1) * HD)
                dmg = dmixed[rs, cs]
                dmb = dmg.astype(BF)
                dbst_ref[:, g:g + 1] += jnp.sum(dmg, axis=1, keepdims=True)
                dws_ref[g] += jnp.where(ri >= ci, _dot_nt(dmb, vln[rs, cs].astype(BF)), 0.0)
                cols.append(_dot_tn(wm[g], dmb))
            blocks.append(jnp.concatenate(cols, axis=1))
        dvln = blocks[0] if len(blocks) == 1 else jnp.concatenate(blocks, axis=0)
        dlg_ref[...] += _colsum(dvln * xh)
        dlb_ref[...] += _colsum(dvln)
        dxh = dvln * lg_ref[...]
        dgv_in = rstd * (dxh - jnp.mean(dxh, axis=-1, keepdims=True)
                         - xh * jnp.mean(dxh * xh, axis=-1, keepdims=True))
        dz_ref[:, D:2 * D] = (dgv_in * dgv).astype(BF)

    row = pl.BlockSpec((tb, D), lambda i: (i, 0))
    vec = pl.BlockSpec((1, D), lambda i: (0, 0))
    wspec = pl.BlockSpec((HEADS, SGU_BLOCK, SGU_BLOCK), lambda i: (0, 0, 0))
    bspec = pl.BlockSpec((SGU_BLOCK, HEADS), lambda i: (0, 0))
    return _pcall(body, name="sgu_bwd", grid=(T // tb,),
                  in_specs=[HBM_SPEC, row, pl.BlockSpec((tb, D), lambda i: (i, 2)),
                            pl.BlockSpec((tb, D), lambda i: (i, 3)), vec, vec, wspec, bspec],
                  out_specs=[pl.BlockSpec((tb, 2 * D), lambda i: (i, 1)), wspec, bspec, vec, vec],
                  out_shape=[jax.ShapeDtypeStruct(dz.shape, BF),
                             jax.ShapeDtypeStruct((HEADS, SGU_BLOCK, SGU_BLOCK), F32),
                             jax.ShapeDtypeStruct((SGU_BLOCK, HEADS), F32),
                             jax.ShapeDtypeStruct((1, D), F32), jax.ShapeDtypeStruct((1, D), F32)],
                  aliases={0: 0}, vmem_mb=48)(dz, dyb_pre, z, z, lg, lb, ws, bst)


def _rglru_bwd(dz, dya_pre, z, h, cw, cb, wa, ba, wx, bx, lam, tb=256):
    T = z.shape[0]
    tb = min(tb, T)
    nrow = T // tb
    per = tb // SUBLANES

    def body(dz_in, dy_ref, xr_ref, xh_ref, gr_ref, h_ref, hh_ref, cw_ref, cb_ref, wa_ref, ba_ref, wx_ref, bx_ref,
             lam_ref, dz_ref, dcw_ref, dcb_ref, dwa_ref, dba_ref, dwx_ref, dbx_ref, dlam_ref, carry, nxt):
        del dz_in
        i = pl.program_id(0)
        first_block = i == nrow - 1

        @pl.when(i == 0)
        def _():
            carry[...] = jnp.zeros_like(carry)
            nxt[...] = jnp.zeros_like(nxt)
            for ref in (dcw_ref, dcb_ref, dwa_ref, dba_ref, dwx_ref, dbx_ref, dlam_ref):
                ref[...] = jnp.zeros_like(ref)

        xr = xr_ref[...]
        pv = jnp.where(first_block, 0.0, xh_ref[...])
        s1 = _shift_down(xr, pv, 1)
        s2 = _shift_down(xr, pv, 2)
        s3 = _shift_down(xr, pv, 3)
        xc = cb_ref[...] + cw_ref[3:4, :] * xr + cw_ref[2:3, :] * s1 + cw_ref[1:2, :] * s2 + cw_ref[0:1, :] * s3
        lam = lam_ref[...]
        r, ig, ls, a, mult = _lru_gates(xc, wa_ref, ba_ref[...], wx_ref, bx_ref[...], lam)
        hv = h_ref[...]
        hprev = _shift_down(hv, jnp.where(first_block, 0.0, hh_ref[...]), 1)
        gg, dgg = _gelu_and_grad(gr_ref[...])
        dy = dy_ref[...].astype(F32)
        dz_ref[:, D:2 * D] = (dy * hv * dgg).astype(BF)

        rows = lax.broadcasted_iota(jnp.int32, (tb, D), 0)
        v = dy * gg + jnp.where(rows == tb - 1, carry[0:1, :], 0.0)
        q = jnp.where(rows < tb - 1, pltpu.roll(a, tb - 1, 0), 0.0)
        _, gsc = _scan_rows(q, v, reverse=True)
        carry[...] = (a * gsc)[0:SUBLANES]

        xi = ig * xc
        dmult = gsc * xi
        dxi = gsc * mult
        dig = dxi * xc
        dxc = dxi * ig
        dlog_a = gsc * hprev * a - dmult * (a * a) * pl.reciprocal(mult, approx=True)
        dlam_ref[...] += _colsum(dlog_a * r) * (LRU_C * _sigmoid(-lam))
        dpr = dlog_a * (LRU_C * ls) * r * (1.0 - r)
        dpi = dig * ig * (1.0 - ig)
        dba_ref[...] += _colsum(dpr)
        dbx_ref[...] += _colsum(dpi)
        back = []
        for hh in range(HEADS):
            cs = slice(hh * HD, (hh + 1) * HD)
            xh = xc[:, cs].astype(BF)
            dprh = dpr[:, cs].astype(BF)
            dpih = dpi[:, cs].astype(BF)
            dwa_ref[hh] += _dot_tn(xh, dprh)
            dwx_ref[hh] += _dot_tn(xh, dpih)
            back.append(_dot_nt(dprh, wa_ref[hh].astype(BF)) + _dot_nt(dpih, wx_ref[hh].astype(BF)))
        dxc = dxc + jnp.concatenate(back, axis=1)

        n8 = nxt[...]
        dxr = (cw_ref[3:4, :] * dxc + cw_ref[2:3, :] * _shift_up(dxc, n8, 1)
               + cw_ref[1:2, :] * _shift_up(dxc, n8, 2) + cw_ref[0:1, :] * _shift_up(dxc, n8, 3))
        nxt[...] = dxc[0:SUBLANES]
        dz_ref[:, 0:D] = dxr.astype(BF)
        dcw_ref[3:4, :] += _colsum(dxc * xr)
        dcw_ref[2:3, :] += _colsum(dxc * s1)
        dcw_ref[1:2, :] += _colsum(dxc * s2)
        dcw_ref[0:1, :] += _colsum(dxc * s3)
        dcb_ref[...] += _colsum(dxc)

    rev = lambda col: (lambda i: (nrow - 1 - i, col))
    halo = lambda col: pl.BlockSpec((SUBLANES, D), lambda i: (jnp.maximum((nrow - 1 - i) * per - 1, 0), col))
    vec = pl.BlockSpec((1, D), lambda i: (0, 0))
    wspec = pl.BlockSpec((HEADS, HD, HD), lambda i: (0, 0, 0))
    c4 = pl.BlockSpec((4, D), lambda i: (0, 0))
    wshape = jax.ShapeDtypeStruct((HEADS, HD, HD), F32)
    vshape = jax.ShapeDtypeStruct((1, D), F32)
    return _pcall(body, name="rglru_bwd", grid=(nrow,),
                  in_specs=[HBM_SPEC, pl.BlockSpec((tb, D), rev(0)), pl.BlockSpec((tb, D), rev(0)), halo(0),
                            pl.BlockSpec((tb, D), rev(1)), pl.BlockSpec((tb, D), rev(0)), halo(0),
                            c4, vec, wspec, vec, wspec, vec, vec],
                  out_specs=[pl.BlockSpec((tb, 2 * D), rev(0)), c4, vec, wspec, vec, wspec, vec, vec],
                  out_shape=[jax.ShapeDtypeStruct(dz.shape, BF), jax.ShapeDtypeStruct((4, D), F32), vshape,
                             wshape, vshape, wshape, vshape, vshape],
                  scratch=[pltpu.VMEM((SUBLANES, D), F32), pltpu.VMEM((SUBLANES, D), F32)],
                  aliases={0: 0}, vmem_mb=56)(dz, dya_pre, z, z, z, h, h, cw, cb, wa, ba, wx, bx, lam)


def _pack_rows(parts):
    out = []
    for p in parts:
        q = p.reshape(-1, LANES)
        pad = (-q.shape[0]) % SUBLANES
        if pad:
            q = jnp.concatenate([q, jnp.zeros((pad, LANES), q.dtype)], axis=0)
        out.append(q)
    return jnp.concatenate(out, axis=0)


def _rows_of(shape):
    n = 1
    for s in shape:
        n *= s
    rows = n // LANES
    return rows + (-rows) % SUBLANES


def kernel(x, c, w_ada, b_ada, norm_mix_g, w_in, rnn_conv_w, rnn_conv_b, lru_w_a, lru_b_a, lru_w_x, lru_b_x, lru_lambda, sgu_ln_g, sgu_ln_b, sgu_w_s, sgu_b_s, w_branch_a, w_branch_b, w_out, norm_ffn_g, w_up, ffn_conv_w, ffn_conv_b, w_down, norm_final_g, loss_target, m_w_ada, m_b_ada, m_norm_mix_g, m_w_in, m_rnn_conv_w, m_rnn_conv_b, m_lru_w_a, m_lru_b_a, m_lru_w_x, m_lru_b_x, m_lru_lambda, m_sgu_ln_g, m_sgu_ln_b, m_sgu_w_s, m_sgu_b_s, m_w_branch_a, m_w_branch_b, m_w_out, m_norm_ffn_g, m_w_up, m_ffn_conv_w, m_ffn_conv_b, m_w_down, m_norm_final_g, v_w_ada, v_b_ada, v_norm_mix_g, v_w_in, v_rnn_conv_w, v_rnn_conv_b, v_lru_w_a, v_lru_b_a, v_lru_w_x, v_lru_b_x, v_lru_lambda, v_sgu_ln_g, v_sgu_ln_b, v_sgu_w_s, v_sgu_b_s, v_w_branch_a, v_w_branch_b, v_w_out, v_norm_ffn_g, v_w_up, v_ffn_conv_w, v_ffn_conv_b, v_w_down, v_norm_final_g):
    args = dict(locals())
    T = x.shape[1]
    mx, my, mc = lax.axis_index("x"), lax.axis_index("y"), lax.axis_index("c")
    chip = 2 * mx + my
    dev = 2 * chip + mc
    vec = lambda a: a.reshape(1, -1)

    xt = x.reshape(T, D)
    tgt = loss_target.reshape(T, D)
    ns = w_in.shape[2]
    dff = w_down.shape[1] * N_CHIPS

    c_all = _gather8(c.reshape(SUBLANES, LANES), "gather_c").reshape(N_DEV, D)
    b_ada_sh = lax.dynamic_slice(b_ada, (0, chip * ns), (1, ns))
    mod_sh = _mod_fwd(c_all, w_ada[0], b_ada_sh)

    mixer_w = _cast_shards([w_in[0], w_branch_a[0], w_branch_b[0], w_out[0]], "cast_mixer_weights")
    w_in4, wba4, wbb4, wo4, rcw4, fcw4, mod4 = _gather_weights(
        list(mixer_w) + [rnn_conv_w[0], ffn_conv_w[0], mod_sh], [True] * 4 + [False] * 3)
    late = _cast_shards([w_up[0], w_down[0]], "cast_late", after=mod4)
    late_plan = _gather_plan(len(late))
    late_handle, late_token = _remote_start(
        late, [lax.empty((N_CHIPS,) + w.shape, w.dtype) for w in late], late_plan, 3 * len(late), "gather_late_start")
    rcw_full = jnp.transpose(rcw4, (1, 0, 2)).reshape(4, D)
    fcw_full = jnp.transpose(fcw4, (1, 0, 2)).reshape(3, 2 * dff)
    mod = lax.dynamic_index_in_dim(mod4, dev, axis=1, keepdims=False).reshape(1, 6 * D)
    shift1, scale1, gate1, shift2, scale2, gate2 = [mod[:, k * D:(k + 1) * D] for k in range(6)]

    bst = jnp.transpose(sgu_b_s[0])
    wba_full = wba4.reshape(D, D)
    wbb_full = wbb4.reshape(D, D)
    wo_full = wo4.reshape(D, D)
    h1, z, h_lru, ya_pre, yb_pre, merged, ya, yb, o1, x2 = _mixer_fwd(
        xt, norm_mix_g, scale1 + late_token[0:1, 0:1], shift1, gate1, w_in4, rcw_full, rnn_conv_b,
        lru_w_a[0], lru_b_a, lru_w_x[0], lru_b_x, lru_lambda, sgu_ln_g, sgu_ln_b, sgu_w_s[0], bst,
        wba_full, wbb_full, wo_full)
    late, late_lands = _remote_wait(late_handle, late_plan, o1, "gather_late_wait")
    w_up4, w_down4 = _place_own(late, late_lands)
    wd_full = w_down4.reshape(dff, D)
    h2, up, f, ffn_ga, ffn_vd, loss_part, dx3, dfo, dgf, dgate2 = _ffn_fwd(
        x2, norm_ffn_g, scale2, shift2, gate2, vec(norm_final_g), w_up4, wd_full, fcw_full, ffn_conv_b, tgt)

    dup, dfcw, dfcb, dx2, dshift2, dscale2, dg_ffn, do1, dgate1 = _ffn_bwd(
        dfo, wd_full, up, ffn_ga, ffn_vd, fcw_full, w_up4, x2, dx3, norm_ffn_g, scale2, gate1, o1)
    dwd = _mm_tn_cols(f, dfo, "dw_down", 1, D, mb=D)
    dw_up4 = _mm_tn_cols(h2, dup, "dw_up", N_CHIPS, ns)
    dz, dya, dyb, dya_pre, dyb_pre = _mix_bwd(do1, ya, yb, z, wo_full, wba_full, wbb_full)
    dwo = _mm_tn_cols(merged, do1, "dw_out", 1, D)
    dwba = _mm_tn_cols(ya_pre, dya, "dw_branch_a", 1, D)
    dwbb = _mm_tn_cols(yb_pre, dyb, "dw_branch_b", 1, D)

    chip_id = chip.astype(jnp.int32).reshape(1)

    def reduce_start(group, name):
        wire = [g16.reshape(N_CHIPS, -1, g16.shape[-1]) for _, (_, g16) in group]
        lands = [lax.empty((3,) + w.shape[1:], w.dtype) for w in wire]
        return _remote_start(wire, lands, _scatter_plan(len(group)), 3 * len(group), "scatter_start_" + name)

    def reduce_finish(group, handle, after, name):
        _, landed = _remote_wait(handle, _scatter_plan(len(group)), after, "scatter_wait_" + name)
        return [_sum_own_and_landed(chip_id, g32.reshape(N_CHIPS, -1, g32.shape[-1]), l, "sum_chips_" + n)
                for (n, (g32, _)), l in zip(group, landed)]

    group1 = [("w_up", dw_up4), ("w_down", dwd), ("w_branch_a", dwba), ("w_branch_b", dwbb), ("w_out", dwo)]
    handle1, token1 = reduce_start(group1, "late")
    dz, dws, dbst, dlg, dlb = _sgu_bwd(dz, dyb_pre, z, sgu_ln_g + token1[0:1, 0:1], sgu_ln_b, sgu_w_s[0], bst)
    dz, drcw, drcb, dwa, dba, dwx, dbx, dlam = _rglru_bwd(
        dz, dya_pre, z, h_lru, rcw_full, rnn_conv_b, lru_w_a[0], lru_b_a, lru_w_x[0], lru_b_x, lru_lambda)
    early_small = [("rnn_conv_b", drcb), ("lru_w_a", dwa), ("lru_b_a", dba), ("lru_w_x", dwx), ("lru_b_x", dbx),
                   ("lru_lambda", dlam), ("sgu_ln_g", dlg), ("sgu_ln_b", dlb), ("sgu_w_s", dws),
                   ("sgu_b_s", jnp.transpose(dbst)), ("norm_ffn_g", dg_ffn),
                   ("ffn_conv_b", dfcb), ("norm_final_g", dgf)]
    r_early = sum(_rows_of(args[n].shape) for n, _ in early_small)
    early_pack = _pack_rows([g for _, g in early_small] + [drcw, dfcw])
    early_pack = jnp.concatenate(
        [early_pack, jnp.zeros(((-early_pack.shape[0]) % 256, LANES), F32)], axis=0)
    early_chip = _add_pair(early_pack, _swap_cores([early_pack], "swap_small_grads")[0], "sum_cores_small_grads")
    early_handle, token3 = _remote_start([early_chip], [lax.empty((3,) + early_chip.shape, F32)], _bcast_plan, 3,
                                         "small_grads_start")
    totals1 = reduce_finish(group1, handle1, drcb, "late")
    group2 = [("w_in", _mm_tn_cols(h1, dz, "dw_in", N_CHIPS, ns))]
    handle2, token2 = reduce_start(group2, "in")
    grad_x, dshift1, dscale1, dg_mix = _mm_nt_normbwd(
        dz, w_in4, xt, dx2, norm_mix_g + (token2[0:1, 0:1] + token3[0:1, 0:1]), scale1, "dh1_norm_bwd")
    totals2 = reduce_finish(group2, handle2, dg_mix, "in")
    dmod = jnp.concatenate([dshift1, dscale1, dgate1, dshift2, dscale2, dgate2], axis=1)

    big = group1 + group2
    mine = totals1 + totals2
    theirs = _swap_cores(mine, "swap_core_sums")
    out = {}
    for (n, _), a, b in zip(big, mine, theirs):
        shape = args[n].shape
        res = _adamw(args[n][0], args["m_" + n][0], args["v_" + n][0], [a, b], "adamw_" + n)
        for kind, r in zip(("grad_", "delta_", "new_m_", "new_v_"), res):
            out[kind + n] = r.reshape(shape)

    late_small = [("b_ada", dmod), ("norm_mix_g", dg_mix)]
    small = late_small + early_small
    late_all = _gather8(_pack_rows([g for _, g in late_small] + [loss_part]), "gather_late_small_grads")
    late_sum = _sum_parts(late_all, "sum_late_small_grads")
    r_late = sum(_rows_of(args[n].shape) for n, _ in late_small)
    loss = late_sum[r_late, 0]
    late_sum = late_sum[:r_late]
    _, (early_landed,) = _remote_wait(early_handle, _bcast_plan, dg_mix, "small_grads_wait")
    early_sum = _sum_chips_in_order(chip_id, early_chip, early_landed, "sum_early_small_grads")
    r_small = sum(_rows_of(args[n].shape) for n, _ in small)
    r_pad = r_small + (-r_small) % 256
    fill = jnp.zeros((r_pad - r_small, LANES), F32)
    g_small = jnp.concatenate([late_sum, early_sum[:r_early], fill], axis=0)

    def pack_small(prefix):
        return jnp.concatenate([_pack_rows([args[prefix + n] for n, _ in small]), fill], axis=0)

    res = _adamw(pack_small(""), pack_small("m_"), pack_small("v_"), [g_small], "adamw_small")
    off = 0
    for n, _ in small:
        shape = args[n].shape
        rows = _rows_of(shape)
        for kind, r in zip(("grad_", "delta_", "new_m_", "new_v_"), res):
            out[kind + n] = r[off:off + rows].reshape(shape)
        off += rows

    rcw_cols = rnn_conv_w.shape[2]
    g_rcw = lax.dynamic_slice(early_sum[r_early:r_early + 32].reshape(4, D), (0, chip * rcw_cols), (4, rcw_cols))
    g_fcw = lax.dynamic_slice(early_sum[r_early + 32:r_early + 32 + 144].reshape(3, 2 * dff), (0, chip * ns), (3, ns))
    conv = [("rnn_conv_w", g_rcw), ("ffn_conv_w", g_fcw)]
    res = _adamw(_pack_rows([args[n] for n, _ in conv]), _pack_rows([args["m_" + n] for n, _ in conv]),
                 _pack_rows([args["v_" + n] for n, _ in conv]), [_pack_rows([g for _, g in conv])], "adamw_conv")
    off = 0
    for n, _ in conv:
        shape = args[n].shape
        cnt = shape[1] * shape[2] // LANES
        for kind, r in zip(("grad_", "delta_", "new_m_", "new_v_"), res):
            out[kind + n] = r[off:off + cnt].reshape(shape)
        off += _rows_of(shape)

    dmod_all = late_all[:, 0:6 * D // LANES, :].reshape(N_DEV, 6 * D)
    dmod_sh = lax.dynamic_slice(dmod_all, (0, chip * ns), (N_DEV, ns))
    res = _ada_adamw(jnp.transpose(c_all), dmod_sh, w_ada[0], m_w_ada[0], v_w_ada[0])
    for kind, r in zip(("grad_", "delta_", "new_m_", "new_v_"), res):
        out[kind + "w_ada"] = r.reshape(w_ada.shape)

    names = ["w_ada", "b_ada", "norm_mix_g", "w_in", "rnn_conv_w", "rnn_conv_b", "lru_w_a", "lru_b_a", "lru_w_x",
             "lru_b_x", "lru_lambda", "sgu_ln_g", "sgu_ln_b", "sgu_w_s", "sgu_b_s", "w_branch_a", "w_branch_b",
             "w_out", "norm_ffn_g", "w_up", "ffn_conv_w", "ffn_conv_b", "w_down", "norm_final_g"]
    result = [loss, grad_x.reshape(x.shape)]
    for kind in ("grad_", "delta_", "new_m_", "new_v_"):
        result += [out[kind + n] for n in names]
    return tuple(result)
```

```python
import jax
import jax.numpy as jnp
from jax import lax
from jax.experimental import pallas as pl
from jax.experimental.pallas import tpu as pltpu

F32 = jnp.float32
BF = jnp.bfloat16

D = 1024
HEADS = 8
HD = D // HEADS
SGU_BLOCK = 128
N_CHIPS = 4
N_DEV = 8
EPS = 1e-6
LRU_C = 8.0
LANES = 128
SUBLANES = 8

ADAM_LR = 0.001
ADAM_B1 = 0.9
ADAM_B2 = 0.999
ADAM_EPS = 1e-08
ADAM_WD = 0.01
ADAM_STEP = 10

GELU_K0 = 0.7978845608028654
GELU_K1 = 0.044715

HBM_SPEC = pl.BlockSpec(memory_space=pltpu.HBM)
MESH_ID = pl.DeviceIdType.MESH


def _pcall(body, *, name, out_shape, grid=(), in_specs=None, out_specs=None, scratch=(), vmem_mb=32, aliases=None,
           grid_spec=None):
    kw = {}
    if aliases:
        kw["input_output_aliases"] = aliases
    if grid_spec is not None:
        kw["grid_spec"] = grid_spec
        ndim = len(grid_spec.grid)
    else:
        kw.update(grid=grid, in_specs=in_specs, out_specs=out_specs, scratch_shapes=list(scratch))
        ndim = len(grid)
    if ndim:
        params = pltpu.CompilerParams(dimension_semantics=("arbitrary",) * ndim, vmem_limit_bytes=vmem_mb * 2 ** 20)
    else:
        params = pltpu.CompilerParams(vmem_limit_bytes=vmem_mb * 2 ** 20)
    return pl.pallas_call(body, name=name, out_shape=out_shape, compiler_params=params, **kw)


def _gelu_cdf(x, x2):
    return 0.5 * jnp.tanh(x * (GELU_K0 + (GELU_K0 * GELU_K1) * x2)) + 0.5


def _gelu(x):
    return x * _gelu_cdf(x, x * x)


def _gelu_and_grad(x):
    x2 = x * x
    s = _gelu_cdf(x, x2)
    g = x * s
    dg = s * (1.0 + (x - g) * ((2.0 * GELU_K0) + (6.0 * GELU_K0 * GELU_K1) * x2))
    return g, dg


def _sigmoid(x):
    return 1.0 / (1.0 + jnp.exp(-x))


def _sigmoid_t(x):
    return 0.5 * jnp.tanh(0.5 * x) + 0.5


def _log_sigmoid(x):
    e = jnp.exp(-jnp.abs(x))
    u = 1.0 + e
    d = u - 1.0
    l1p = jnp.where(d == 0.0, e, jnp.log(u) * (e / jnp.where(d == 0.0, 1.0, d)))
    return jnp.minimum(x, 0.0) - l1p


def _dot(a, b):
    return jnp.dot(a, b, preferred_element_type=F32)


def _dot_nt(a, b):
    return lax.dot_general(a, b, (((1,), (1,)), ((), ())), preferred_element_type=F32)


def _dot_tn(a, b):
    return lax.dot_general(a, b, (((0,), (0,)), ((), ())), preferred_element_type=F32)


def _shift_down(x, halo, s):
    r = pltpu.roll(x, s, 0)
    rows = lax.broadcasted_iota(jnp.int32, (SUBLANES, x.shape[1]), 0)
    head = jnp.where(rows < s, pltpu.roll(halo, s, 0), r[0:SUBLANES])
    return jnp.concatenate([head, r[SUBLANES:]], axis=0)


def _shift_up(x, halo, s):
    n = x.shape[0]
    r = pltpu.roll(x, n - s, 0)
    rows = lax.broadcasted_iota(jnp.int32, (SUBLANES, x.shape[1]), 0)
    tail = jnp.where(rows >= SUBLANES - s, pltpu.roll(halo, SUBLANES - s, 0), r[n - SUBLANES:n])
    return jnp.concatenate([r[:n - SUBLANES], tail], axis=0)


def _scan_rows(a, u, reverse):
    n, width = a.shape
    rows = lax.broadcasted_iota(jnp.int32, (n, width), 0)
    d = 1
    while d < n:
        if d < SUBLANES:
            keep = rows < n - d if reverse else rows >= d
            shift = n - d if reverse else d
            a_s = jnp.where(keep, pltpu.roll(a, shift, 0), 1.0)
            u_s = jnp.where(keep, pltpu.roll(u, shift, 0), 0.0)
        elif reverse:
            a_s = jnp.concatenate([a[d:], jnp.ones((d, width), a.dtype)], axis=0)
            u_s = jnp.concatenate([u[d:], jnp.zeros((d, width), u.dtype)], axis=0)
        else:
            a_s = jnp.concatenate([jnp.ones((d, width), a.dtype), a[:n - d]], axis=0)
            u_s = jnp.concatenate([jnp.zeros((d, width), u.dtype), u[:n - d]], axis=0)
        u = a * u_s + u
        a = a * a_s
        d *= 2
    return a, u


def _colsum(x):
    return jnp.sum(x, axis=0, keepdims=True)


def _rms_stats(x):
    r = lax.rsqrt(jnp.mean(x * x, axis=-1, keepdims=True) + EPS)
    return r, x * r


def _lru_gates(xc, wa_ref, ba, wx_ref, bx, lam, head0=0):
    pr, pi = [], []
    for hh in range(xc.shape[1] // HD):
        xh = xc[:, hh * HD:(hh + 1) * HD].astype(BF)
        pr.append(_dot(xh, wa_ref[head0 + hh].astype(BF)))
        pi.append(_dot(xh, wx_ref[head0 + hh].astype(BF)))
    r = _sigmoid_t((pr[0] if len(pr) == 1 else jnp.concatenate(pr, axis=1)) + ba)
    ig = _sigmoid_t((pi[0] if len(pi) == 1 else jnp.concatenate(pi, axis=1)) + bx)
    ls = _log_sigmoid(lam)
    log_a = LRU_C * r * ls
    a = jnp.exp(log_a)
    x2 = 2.0 * log_a
    u = a * a
    lu = jnp.log(jnp.maximum(u, 1e-37))
    ratio = x2 * pl.reciprocal(jnp.where(lu == 0.0, 1.0, lu), approx=True)
    em1 = jnp.where(lu == 0.0, x2, jnp.where(u < 1e-30, -1.0, (u - 1.0) * ratio))
    mult = jnp.sqrt(-em1)
    return r, ig, ls, a, mult


def _sgu_mix(vln, ws_ref, bst_ref, tb):
    ri = lax.broadcasted_iota(jnp.int32, (SGU_BLOCK, SGU_BLOCK), 0)
    ci = lax.broadcasted_iota(jnp.int32, (SGU_BLOCK, SGU_BLOCK), 1)
    wm = [jnp.where(ri >= ci, ws_ref[g], 0.0).astype(BF) for g in range(HEADS)]
    blocks = []
    for blk in range(tb // SGU_BLOCK):
        cols = []
        for g in range(HEADS):
            vb = vln[blk * SGU_BLOCK:(blk + 1) * SGU_BLOCK, g * HD:(g + 1) * HD].astype(BF)
            cols.append(_dot(wm[g], vb) + bst_ref[:, g:g + 1])
        blocks.append(jnp.concatenate(cols, axis=1))
    mixed = blocks[0] if len(blocks) == 1 else jnp.concatenate(blocks, axis=0)
    return wm, mixed


def _layernorm_stats(v):
    mu = jnp.mean(v, axis=-1, keepdims=True)
    vc = v - mu
    rstd = lax.rsqrt(jnp.mean(vc * vc, axis=-1, keepdims=True) + EPS)
    return rstd, vc * rstd


def _my_xyc():
    return lax.axis_index("x"), lax.axis_index("y"), lax.axis_index("c")


def _gather_weights(srcs, halve):
    n = len(srcs)
    out_shape = [jax.ShapeDtypeStruct((N_CHIPS,) + s.shape, s.dtype) for s in srcs]

    def body(*refs):
        src, out = refs[:n], refs[n:2 * n]
        send_sems, recv_sems, fwd_send, fwd_recv, loc_sems = refs[2 * n:]
        x, y, c = _my_xyc()
        me = 2 * x + y
        chips = [(1 - x, y), (x, 1 - y), (1 - x, 1 - y)]

        def half(ref, a, which):
            if not halve[a]:
                return ref
            h = srcs[a].shape[0] // 2
            return ref.at[pl.ds(which * h, h)]

        def ici(a, k, frm):
            px, py = chips[k]
            return pltpu.make_async_remote_copy(
                src_ref=half(src[a], a, c), dst_ref=half(out[a].at[frm], a, c),
                send_sem=send_sems.at[a, k], recv_sem=recv_sems.at[a, k],
                device_id=(px, py, c), device_id_type=MESH_ID)

        def d2d(a, k, which):
            px, py = chips[k]
            rows = half(out[a].at[2 * px + py], a, which)
            return pltpu.make_async_remote_copy(
                src_ref=rows, dst_ref=rows, send_sem=fwd_send.at[a, k], recv_sem=fwd_recv.at[a, k],
                device_id=(x, y, 1 - c), device_id_type=MESH_ID)

        local, sends = [], []
        for a in range(n):
            lc = pltpu.make_async_copy(src[a], out[a].at[me], loc_sems.at[a])
            lc.start()
            local.append(lc)
            for k in range(3):
                cp = ici(a, k, me)
                cp.start()
                sends.append(cp)
        for a in range(n):
            for k in range(3):
                px, py = chips[k]
                ici(a, k, 2 * px + py).wait_recv()
                if halve[a]:
                    fw = d2d(a, k, c)
                    fw.start()
                    sends.append(fw)
        for a in range(n):
            if halve[a]:
                for k in range(3):
                    d2d(a, k, 1 - c).wait_recv()
        for cp in sends:
            cp.wait_send()
        for lc in local:
            lc.wait()

    sem = pltpu.SemaphoreType.DMA((n, 3))
    return _pcall(body, name="gather_weights", out_shape=out_shape, in_specs=[HBM_SPEC] * n,
                  out_specs=[HBM_SPEC] * n, scratch=[sem, sem, sem, sem, pltpu.SemaphoreType.DMA((n,))])(*srcs)


SEM_SPEC = pl.BlockSpec(memory_space=pltpu.SEMAPHORE)


def _remote_start(srcs, lands, plan, ncopies, name):
    n, m = len(srcs), len(lands)

    def body(*refs):
        src, land = refs[:n], refs[n:n + m]
        send_sems, recv_sems = refs[n + m], refs[n + m + 1]
        token = refs[-1]
        x, y, c = _my_xyc()
        for i, (s, d, dev) in enumerate(plan(src, land, x, y, c)):
            pltpu.make_async_remote_copy(src_ref=s, dst_ref=d, send_sem=send_sems.at[i], recv_sem=recv_sems.at[i],
                                         device_id=dev, device_id_type=MESH_ID).start()
        token[...] = jnp.zeros_like(token)

    bufs = list(srcs) + list(lands)
    out = pl.pallas_call(
        body, name=name,
        out_shape=(pltpu.SemaphoreType.DMA((ncopies,)), pltpu.SemaphoreType.DMA((ncopies,)),
                   *[pltpu.HBM(b.shape, b.dtype) for b in bufs], jax.ShapeDtypeStruct((SUBLANES, LANES), F32)),
        in_specs=[HBM_SPEC] * (n + m),
        out_specs=(SEM_SPEC, SEM_SPEC, *[HBM_SPEC] * (n + m), pl.BlockSpec(memory_space=pltpu.VMEM)),
        input_output_aliases={i: 2 + i for i in range(n + m)},
        compiler_params=pltpu.CompilerParams(has_side_effects=pltpu.SideEffectType.DATAFLOW_SIDE_EFFECTING),
    )(*[pltpu.with_memory_space_constraint(b, pltpu.HBM) for b in bufs])
    return (out[0], out[1], out[2:2 + n], out[2 + n:2 + n + m]), out[-1]


def _remote_wait(handle, plan, after, name):
    send_sems, recv_sems, srcs, lands = handle
    n, m = len(srcs), len(lands)

    def body(*refs):
        src, land = refs[:n], refs[n:n + m]
        ssem, rsem = refs[n + m], refs[n + m + 1]
        x, y, c = _my_xyc()
        for i, (s, d, dev) in enumerate(plan(src, land, x, y, c)):
            cp = pltpu.make_async_remote_copy(src_ref=s, dst_ref=d, send_sem=ssem.at[i], recv_sem=rsem.at[i],
                                              device_id=dev, device_id_type=MESH_ID)
            cp.wait_send()
            cp.wait_recv()

    bufs = list(srcs) + list(lands)
    out = pl.pallas_call(
        body, name=name, out_shape=tuple(pltpu.HBM(b.shape, b.dtype) for b in bufs),
        in_specs=[HBM_SPEC] * (n + m) + [SEM_SPEC, SEM_SPEC, pl.BlockSpec(memory_space=pl.ANY)],
        out_specs=tuple([HBM_SPEC] * (n + m)), input_output_aliases={i: i for i in range(n + m)},
        compiler_params=pltpu.CompilerParams(has_side_effects=pltpu.SideEffectType.DATAFLOW_SIDE_EFFECTING),
    )(*bufs, send_sems, recv_sems, after)
    return out[:n], out[n:]


def _chips_of(x, y):
    return [(1 - x, y), (x, 1 - y), (1 - x, 1 - y)]


def _gather_plan(count):
    def plan(src, land, x, y, c):
        me = 2 * x + y
        return [(src[a], land[a].at[me], (px, py, c)) for a in range(count) for px, py in _chips_of(x, y)]

    return plan


def _place_own(srcs, lands):
    n = len(srcs)

    def body(*refs):
        src, land, sems = refs[:n], refs[2 * n:3 * n], refs[3 * n]
        me = 2 * lax.axis_index("x") + lax.axis_index("y")
        copies = [pltpu.make_async_copy(src[a], land[a].at[me], sems.at[a]) for a in range(n)]
        for cp in copies:
            cp.start()
        for cp in copies:
            cp.wait()

    return _pcall(body, name="place_own_shards", out_shape=[jax.ShapeDtypeStruct(l.shape, l.dtype) for l in lands],
                  in_specs=[HBM_SPEC] * (2 * n), out_specs=[HBM_SPEC] * n, aliases={n + a: a for a in range(n)},
                  scratch=[pltpu.SemaphoreType.DMA((n,))])(*srcs, *lands)


def _gather8(src, name):
    def body(src_ref, out_ref, send_sems, recv_sems, loc_sem):
        x, y, c = _my_xyc()
        me = 4 * x + 2 * y + c
        lc = pltpu.make_async_copy(src_ref, out_ref.at[me], loc_sem)
        lc.start()
        cps = []
        for k in range(1, N_DEV):
            px = 1 - x if (k >> 2) & 1 else x
            py = 1 - y if (k >> 1) & 1 else y
            pc = 1 - c if k & 1 else c
            cp = pltpu.make_async_remote_copy(
                src_ref=src_ref, dst_ref=out_ref.at[me], send_sem=send_sems.at[k - 1], recv_sem=recv_sems.at[k - 1],
                device_id=(px, py, pc), device_id_type=MESH_ID)
            cp.start()
            cps.append(cp)
        for cp in cps:
            cp.wait()
        lc.wait()

    return _pcall(body, name=name, out_shape=jax.ShapeDtypeStruct((N_DEV,) + src.shape, src.dtype),
                  in_specs=[HBM_SPEC], out_specs=HBM_SPEC,
                  scratch=[pltpu.SemaphoreType.DMA((N_DEV - 1,)), pltpu.SemaphoreType.DMA((N_DEV - 1,)),
                           pltpu.SemaphoreType.DMA])(src)


def _cast_shards(arrs, name, after=None):
    n = len(arrs)
    extra = [] if after is None else [after]

    def body(*refs):
        ins, outs = refs[:n], refs[n + len(extra):]
        for a in range(n):
            outs[a][...] = ins[a][...].astype(BF)

    specs = [pl.BlockSpec((s.shape[0] // 4, s.shape[1]), lambda i: (i, 0)) for s in arrs]
    return _pcall(body, name=name, grid=(4,), in_specs=specs + [pl.BlockSpec(memory_space=pl.ANY)] * len(extra),
                  out_specs=specs, out_shape=[jax.ShapeDtypeStruct(s.shape, BF) for s in arrs])(*arrs, *extra)


def _row_tile(rows, cols):
    t = rows
    while t * cols * 4 > (3 << 19) and t % 16 == 0:
        t //= 2
    return t


def _sum_parts(parts, name):
    p, rows, cols = parts.shape
    tr = _row_tile(rows, cols * p // 2)

    def body(p_ref, o_ref):
        acc = p_ref[0].astype(F32)
        for k in range(1, p):
            acc = acc + p_ref[k].astype(F32)
        o_ref[...] = acc

    return _pcall(body, name=name, grid=(rows // tr,),
                  in_specs=[pl.BlockSpec((p, tr, cols), lambda i: (0, i, 0))],
                  out_specs=pl.BlockSpec((tr, cols), lambda i: (i, 0)),
                  out_shape=jax.ShapeDtypeStruct((rows, cols), F32), vmem_mb=48)(parts)


def _sum_own_and_landed(chip, sums, landed, name):
    _, rows, cols = sums.shape
    tr = _row_tile(rows, 2 * cols)

    def body(chip_ref, own_ref, land_ref, o_ref):
        del chip_ref
        acc = own_ref[0].astype(F32)
        for k in range(3):
            acc = acc + land_ref[k].astype(F32)
        o_ref[...] = acc

    grid_spec = pltpu.PrefetchScalarGridSpec(
        num_scalar_prefetch=1, grid=(rows // tr,),
        in_specs=[pl.BlockSpec((1, tr, cols), lambda i, chip_ref: (chip_ref[0], i, 0)),
                  pl.BlockSpec((3, tr, cols), lambda i, chip_ref: (0, i, 0))],
        out_specs=pl.BlockSpec((tr, cols), lambda i, chip_ref: (i, 0)))
    return _pcall(body, name=name, grid_spec=grid_spec, out_shape=jax.ShapeDtypeStruct((rows, cols), F32),
                  vmem_mb=48)(chip, sums, landed)


def _swap_cores(arrs, name):
    n = len(arrs)

    def body(*refs):
        src, out = refs[:n], refs[n:2 * n]
        send_sems, recv_sems = refs[2 * n:]
        x, y, c = _my_xyc()
        cps = []
        for a in range(n):
            cp = pltpu.make_async_remote_copy(
                src_ref=src[a], dst_ref=out[a], send_sem=send_sems.at[a], recv_sem=recv_sems.at[a],
                device_id=(x, y, 1 - c), device_id_type=MESH_ID)
            cp.start()
            cps.append(cp)
        for cp in cps:
            cp.wait()

    sem = pltpu.SemaphoreType.DMA((n,))
    return _pcall(body, name=name, out_shape=[jax.ShapeDtypeStruct(a.shape, a.dtype) for a in arrs],
                  in_specs=[HBM_SPEC] * n, out_specs=[HBM_SPEC] * n, scratch=[sem, sem])(*arrs)


def _add_pair(a, b, name):
    rows, cols = a.shape
    tr = _row_tile(rows, 2 * cols)

    def body(a_ref, b_ref, o_ref):
        o_ref[...] = a_ref[...] + b_ref[...]

    spec = pl.BlockSpec((tr, cols), lambda i: (i, 0))
    return _pcall(body, name=name, grid=(rows // tr,), in_specs=[spec, spec], out_specs=spec,
                  out_shape=jax.ShapeDtypeStruct((rows, cols), F32))(a, b)


def _sum_chips_in_order(chip, own, landed, name):
    rows, cols = own.shape
    tr = _row_tile(rows, 4 * cols)

    def body(chip_ref, own_ref, land_ref, o_ref):
        me = chip_ref[0]
        acc = None
        for p in range(N_CHIPS):
            q = p ^ me
            k = jnp.where(q == 2, 0, jnp.where(q == 1, 1, 2))
            term = jnp.where(q == 0, own_ref[...], land_ref[k])
            acc = term if acc is None else acc + term
        o_ref[...] = acc

    grid_spec = pltpu.PrefetchScalarGridSpec(
        num_scalar_prefetch=1, grid=(rows // tr,),
        in_specs=[pl.BlockSpec((tr, cols), lambda i, chip_ref: (i, 0)),
                  pl.BlockSpec((3, tr, cols), lambda i, chip_ref: (0, i, 0))],
        out_specs=pl.BlockSpec((tr, cols), lambda i, chip_ref: (i, 0)))
    return _pcall(body, name=name, grid_spec=grid_spec, out_shape=jax.ShapeDtypeStruct((rows, cols), F32))(
        chip, own, landed)


def _bcast_plan(src, land, x, y, c):
    return [(src[0], land[0].at[k], (px, py, c)) for k, (px, py) in enumerate(_chips_of(x, y))]


def _scatter_plan(count):
    def plan(src, land, x, y, c):
        out = []
        for a in range(count):
            for k, (px, py) in enumerate(_chips_of(x, y)):
                out.append((src[a].at[2 * px + py], land[a].at[k], (px, py, c)))
        return out

    return plan


def _adamw_math(w, g, m, v):
    m2 = ADAM_B1 * m + (1.0 - ADAM_B1) * g
    v2 = ADAM_B2 * v + (1.0 - ADAM_B2) * (g * g)
    m_hat = m2 / (1.0 - ADAM_B1 ** ADAM_STEP)
    v_hat = v2 / (1.0 - ADAM_B2 ** ADAM_STEP)
    delta = -ADAM_LR * (m_hat / (jnp.sqrt(v_hat) + ADAM_EPS) + ADAM_WD * w)
    return delta, m2, v2


def _adamw(w, m, v, grads, name):
    rows, cols = w.shape
    tr = _row_tile(rows, cols)
    ng = len(grads)

    def body(*refs):
        w_ref, m_ref, v_ref = refs[:3]
        g = refs[3][...]
        for k in range(1, ng):
            g = g + refs[3 + k][...]
        g_ref, d_ref, m2_ref, v2_ref = refs[3 + ng:]
        delta, m2, v2 = _adamw_math(w_ref[...], g, m_ref[...], v_ref[...])
        g_ref[...] = g
        d_ref[...] = delta
        m2_ref[...] = m2
        v2_ref[...] = v2

    spec = pl.BlockSpec((tr, cols), lambda i: (i, 0))
    return _pcall(body, name=name, grid=(rows // tr,), in_specs=[spec] * (3 + ng), out_specs=[spec] * 4,
                  out_shape=[jax.ShapeDtypeStruct((rows, cols), F32)] * 4, vmem_mb=48)(w, m, v, *grads)


def _ada_adamw(ct, dmod, w, m, v):
    rows, cols = w.shape
    tr = _row_tile(rows, cols)

    def body(ct_ref, dm_ref, w_ref, m_ref, v_ref, g_ref, d_ref, m2_ref, v2_ref):
        cv = ct_ref[...]
        ca = cv * _sigmoid(cv)
        g = ca[:, 0:1] * dm_ref[0:1, :]
        for b in range(1, N_DEV):
            g = g + ca[:, b:b + 1] * dm_ref[b:b + 1, :]
        delta, m2, v2 = _adamw_math(w_ref[...], g, m_ref[...], v_ref[...])
        g_ref[...] = g
        d_ref[...] = delta
        m2_ref[...] = m2
        v2_ref[...] = v2

    spec = pl.BlockSpec((tr, cols), lambda i: (i, 0))
    return _pcall(body, name="ada_adamw", grid=(rows // tr,),
                  in_specs=[pl.BlockSpec((tr, N_DEV), lambda i: (i, 0)), pl.BlockSpec((N_DEV, cols), lambda i: (0, 0)),
                            spec, spec, spec],
                  out_specs=[spec] * 4, out_shape=[jax.ShapeDtypeStruct((rows, cols), F32)] * 4,
                  vmem_mb=48)(ct, dmod, w, m, v)


def _mod_fwd(c_all, w, b):
    cols = w.shape[1]
    tn = cols // 3

    def body(c_ref, w_ref, b_ref, o_ref):
        cv = c_ref[...]
        ca = (cv * _sigmoid(cv)).astype(BF)
        o_ref[...] = _dot(ca, w_ref[...].astype(BF)) + b_ref[...]

    return _pcall(body, name="mod_fwd", grid=(3,),
                  in_specs=[pl.BlockSpec((N_DEV, D), lambda j: (0, 0)), pl.BlockSpec((D, tn), lambda j: (0, j)),
                            pl.BlockSpec((1, tn), lambda j: (0, j))],
                  out_specs=pl.BlockSpec((N_DEV, tn), lambda j: (0, j)),
                  out_shape=jax.ShapeDtypeStruct((N_DEV, cols), F32))(c_all, w, b)


def _resident(shape):
    zeros = (0,) * len(shape)
    return pl.BlockSpec(shape, lambda *_: zeros, pipeline_mode=pl.Buffered(1))


def _mixer_fwd(x, g, scale, shift, gate1, w_in4, cw, cb, wa, ba, wx, bx, lam, lg, lb, ws, bst, wba, wbb, wo,
               tm=256, chunk=256, piece=512):
    T = x.shape[0]
    tm = min(tm, T)
    ns = w_in4.shape[2]
    per = ns // piece

    def body(x_ref, g_ref, sc_ref, sh_ref, g1_ref, w_ref, cw_ref, cb_ref, wa_ref, ba_ref, wx_ref, bx_ref, lam_ref,
             lg_ref, lb_ref, ws_ref, bst_ref, wba_ref, wbb_ref, wo_ref,
             h1_ref, z_ref, hl_ref, yap_ref, ybp_ref, mg_ref, ya_ref, yb_ref, o_ref, x2_ref,
             xc_ref, r_ref, ig_ref, mu_ref, a_ref, prev, hc):
        i = pl.program_id(0)

        @pl.when(i == 0)
        def _():
            prev[...] = jnp.zeros_like(prev)
            hc[...] = jnp.zeros_like(hc)

        xv = x_ref[...]
        _, xh = _rms_stats(xv)
        h = ((xh * g_ref[...]) * (1.0 + sc_ref[...]) + sh_ref[...]).astype(BF)
        h1_ref[...] = h

        def proj(col, width):
            for c0 in range(col, col + width, piece):
                w = min(piece, col + width - c0)
                j, off = c0 // ns, c0 % ns
                z_ref[:, c0:c0 + w] = _dot(h, w_ref[j, :, off:off + w])

        def lru_chunk(c0):
            cs = slice(c0, c0 + chunk)
            xr = z_ref[:, cs]
            pv = prev[:, cs]
            xc = (cb_ref[:, cs] + cw_ref[3:4, cs] * xr + cw_ref[2:3, cs] * _shift_down(xr, pv, 1)
                  + cw_ref[1:2, cs] * _shift_down(xr, pv, 2) + cw_ref[0:1, cs] * _shift_down(xr, pv, 3))
            prev[:, cs] = xr[tm - SUBLANES:tm]
            r, ig, _, a, mult = _lru_gates(xc, wa_ref, ba_ref[:, cs], wx_ref, bx_ref[:, cs], lam_ref[:, cs],
                                           head0=c0 // HD)
            xc_ref[:, cs] = xc.astype(BF)
            r_ref[:, cs] = r.astype(BF)
            ig_ref[:, cs] = ig.astype(BF)
            a_ref[:, cs] = a
            mu_ref[:, cs] = mult.astype(BF)
            a, u = _scan_rows(a, mult * (ig * xc), reverse=False)
            hv = u + a * hc[SUBLANES - 1:SUBLANES, cs]
            hc[:, cs] = hv[tm - SUBLANES:tm]
            hl_ref[:, cs] = hv
            yap_ref[:, cs] = (hv * _gelu(z_ref[:, D + c0:D + c0 + chunk])).astype(BF)

        proj(0, chunk)
        proj(D, chunk)
        for c0 in range(0, D, chunk):
            if c0 + chunk < D:
                proj(c0 + chunk, chunk)
                proj(D + c0 + chunk, chunk)
            else:
                proj(2 * D, 2 * D)
            lru_chunk(c0)
        proj(4 * D, 2 * D)
        _, xhn = _layernorm_stats(_gelu(z_ref[:, 3 * D:4 * D]))
        vln = xhn * lg_ref[...] + lb_ref[...]
        _, mixed = _sgu_mix(vln, ws_ref, bst_ref, tm)
        ybp = (_gelu(z_ref[:, 2 * D:3 * D]) * mixed).astype(BF)
        ybp_ref[...] = ybp
        ya = _dot(yap_ref[...], wba_ref[...])
        yb = _dot(ybp, wbb_ref[...])
        merged = (_sigmoid_t(z_ref[:, 4 * D:5 * D]) * ya + _sigmoid_t(z_ref[:, 5 * D:6 * D]) * yb).astype(BF)
        o = _dot(merged, wo_ref[...])
        x2_ref[...] = xv + g1_ref[...] * o
        mg_ref[...] = merged
        ya_ref[...] = ya.astype(BF)
        yb_ref[...] = yb.astype(BF)
        o_ref[...] = o.astype(BF)

    row = pl.BlockSpec((tm, D), lambda i: (i, 0))
    vec = pl.BlockSpec((1, D), lambda i: (0, 0))
    bf_row = jax.ShapeDtypeStruct((T, D), BF)
    f32_row = jax.ShapeDtypeStruct((T, D), F32)
    return _pcall(body, name="mixer_fwd", grid=(T // tm,),
                  in_specs=[row, vec, vec, vec, vec, _resident(w_in4.shape), _resident(cw.shape), vec,
                            _resident(wa.shape), vec, _resident(wx.shape), vec, vec, vec, vec,
                            _resident(ws.shape), _resident(bst.shape),
                            _resident(wba.shape), _resident(wbb.shape), _resident(wo.shape)],
                  out_specs=[row, pl.BlockSpec((tm, 6 * D), lambda i: (i, 0))] + [row] * 13,
                  out_shape=[bf_row, jax.ShapeDtypeStruct((T, 6 * D), F32), f32_row, bf_row, bf_row, bf_row, bf_row,
                             bf_row, bf_row, f32_row, bf_row, bf_row, bf_row, bf_row, f32_row],
                  scratch=[pltpu.VMEM((SUBLANES, D), F32), pltpu.VMEM((SUBLANES, D), F32)], vmem_mb=60)(
        x, g, scale, shift, gate1, w_in4, cw, cb, wa, ba, wx, bx, lam, lg, lb, ws, bst, wba, wbb, wo)


def _ffn_fwd(x2, g, scale, shift, gate2, gf, w_up4, wd, cw, cb, target, tm=256, chunk=768):
    T = x2.shape[0]
    tm = min(tm, T)
    ns = w_up4.shape[2]
    dff = wd.shape[0]
    nchunk = dff // chunk
    per = ns // chunk

    def body(x2_ref, g_ref, sc_ref, sh_ref, g2_ref, gf_ref, wu_ref, wd_ref, cw_ref, cb_ref, t_ref,
             h2_ref, up_ref, f_ref, ga_ref, vd_ref, loss_ref, dx3_ref, dfo_ref, dgf_ref, dg2_ref, prev):
        i = pl.program_id(0)

        @pl.when(i == 0)
        def _():
            prev[...] = jnp.zeros_like(prev)
            loss_ref[...] = jnp.zeros_like(loss_ref)
            dgf_ref[...] = jnp.zeros_like(dgf_ref)
            dg2_ref[...] = jnp.zeros_like(dg2_ref)

        x2v = x2_ref[...]
        _, xh2 = _rms_stats(x2v)
        h2 = ((xh2 * g_ref[...]) * (1.0 + sc_ref[...]) + sh_ref[...]).astype(BF)
        h2_ref[...] = h2

        def conv(u, col):
            cs = slice(col, col + chunk)
            p = prev[:, cs]
            hid = (cb_ref[:, cs] + cw_ref[2:3, cs] * u + cw_ref[1:2, cs] * _shift_down(u, p, 1)
                   + cw_ref[0:1, cs] * _shift_down(u, p, 2))
            prev[:, cs] = u[tm - SUBLANES:tm]
            up_ref[:, cs] = u.astype(BF)
            return hid

        def up_proj(k):
            off = (k % per) * chunk
            return (_dot(h2, wu_ref[k // per, :, off:off + chunk]),
                    _dot(h2, wu_ref[N_CHIPS // 2 + k // per, :, off:off + chunk]))

        fo = None
        nxt = up_proj(0)
        for k in range(nchunk):
            col = k * chunk
            ua, uv = nxt
            if k + 1 < nchunk:
                nxt = up_proj(k + 1)
            act = conv(ua, col)
            val = conv(uv, dff + col)
            ga, dga = _gelu_and_grad(act)
            fk = (ga * val).astype(BF)
            f_ref[:, col:col + chunk] = fk
            ga_ref[:, col:col + chunk] = ga.astype(BF)
            vd_ref[:, col:col + chunk] = (val * dga).astype(BF)
            part = _dot(fk, wd_ref[col:col + chunk, :])
            fo = part if fo is None else fo + part

        x3 = x2v + g2_ref[...] * fo
        rstd, xh = _rms_stats(x3)
        err = xh * gf_ref[...] - t_ref[...]
        loss_ref[...] += 0.5 * jnp.sum(jnp.mean(err * err, axis=-1, keepdims=True), axis=0, keepdims=True)
        dy = err * (1.0 / D)
        dgf_ref[...] += _colsum(dy * xh)
        dxh = dy * gf_ref[...]
        dx3 = rstd * (dxh - xh * jnp.mean(dxh * xh, axis=-1, keepdims=True))
        dg2_ref[...] += _colsum(dx3 * fo)
        dx3_ref[...] = dx3
        dfo_ref[...] = (g2_ref[...] * dx3).astype(BF)

    row = pl.BlockSpec((tm, D), lambda i: (i, 0))
    vec = pl.BlockSpec((1, D), lambda i: (0, 0))
    wide = pl.BlockSpec((tm, 2 * dff), lambda i: (i, 0))
    half = pl.BlockSpec((tm, dff), lambda i: (i, 0))
    return _pcall(body, name="ffn_fwd", grid=(T // tm,),
                  in_specs=[row, vec, vec, vec, vec, vec, _resident(w_up4.shape), _resident(wd.shape),
                            _resident(cw.shape), _resident(cb.shape), row],
                  out_specs=[row, wide, half, half, half, pl.BlockSpec((1, LANES), lambda i: (0, 0)), row, row, vec, vec],
                  out_shape=[jax.ShapeDtypeStruct((T, D), BF), jax.ShapeDtypeStruct((T, 2 * dff), BF),
                             jax.ShapeDtypeStruct((T, dff), BF), jax.ShapeDtypeStruct((T, dff), BF),
                             jax.ShapeDtypeStruct((T, dff), BF), jax.ShapeDtypeStruct((1, LANES), F32),
                             jax.ShapeDtypeStruct((T, D), F32), jax.ShapeDtypeStruct((T, D), BF),
                             jax.ShapeDtypeStruct((1, D), F32), jax.ShapeDtypeStruct((1, D), F32)],
                  scratch=[pltpu.VMEM((SUBLANES, 2 * dff), F32)], vmem_mb=56)(
        x2, g, scale, shift, gate2, gf, w_up4, wd, cw, cb, target)


def _ffn_bwd(dfo, wd, up, ga, vd, cw, w_up4, x2, resid, g, scale, gate, o, tm=256, chunk=1536):
    T = up.shape[0]
    tm = min(tm, T)
    dff = wd.shape[0]
    ns = w_up4.shape[2]
    nchunk = dff // chunk
    per = ns // chunk
    nrow = T // tm

    def body(dfo_ref, wd_ref, up_ref, ga_ref, vd_ref, cw_ref, wu_ref, x_ref, r_ref, g_ref, sc_ref, gt_ref, o_ref,
             du_ref, dcw_ref, dcb_ref, dx_ref, dsh_ref, dsc_ref, dg_ref, do_ref, dgt_ref, nxt):
        i = pl.program_id(0)

        @pl.when(i == 0)
        def _():
            nxt[...] = jnp.zeros_like(nxt)
            for ref in (dcw_ref, dcb_ref, dsh_ref, dsc_ref, dg_ref, dgt_ref):
                ref[...] = jnp.zeros_like(ref)

        dfo_t = dfo_ref[...]

        def conv_bwd(dh, col):
            cs = slice(col, col + chunk)
            n8 = nxt[:, cs]
            dh1 = _shift_up(dh, n8, 1)
            dh2 = _shift_up(dh, n8, 2)
            nxt[:, cs] = dh[0:SUBLANES]
            du = (cw_ref[2:3, cs] * dh + cw_ref[1:2, cs] * dh1 + cw_ref[0:1, cs] * dh2).astype(BF)
            du_ref[:, cs] = du
            u = up_ref[:, cs].astype(F32)
            dcw_ref[2:3, cs] += _colsum(dh * u)
            dcw_ref[1:2, cs] += _colsum(dh1 * u)
            dcw_ref[0:1, cs] += _colsum(dh2 * u)
            dcb_ref[:, cs] += _colsum(dh)
            return du

        def down_bwd(k):
            return _dot_nt(dfo_t, wd_ref[k * chunk:(k + 1) * chunk, :])

        dh = None
        df_next = down_bwd(0)
        for k in range(nchunk):
            col = k * chunk
            off = (k % per) * chunk
            df = df_next
            if k + 1 < nchunk:
                df_next = down_bwd(k + 1)
            du_a = conv_bwd(df * vd_ref[:, col:col + chunk].astype(F32), col)
            du_v = conv_bwd(df * ga_ref[:, col:col + chunk].astype(F32), dff + col)
            part = (_dot_nt(du_a, wu_ref[k // per, :, off:off + chunk])
                    + _dot_nt(du_v, wu_ref[N_CHIPS // 2 + k // per, :, off:off + chunk]))
            dh = part if dh is None else dh + part

        rstd, xh = _rms_stats(x_ref[...])
        dsh_ref[...] += _colsum(dh)
        dsc_ref[...] += _colsum(dh * (xh * g_ref[...]))
        dn = dh * (1.0 + sc_ref[...])
        dg_ref[...] += _colsum(dn * xh)
        dxh = dn * g_ref[...]
        dx = r_ref[...] + rstd * (dxh - xh * jnp.mean(dxh * xh, axis=-1, keepdims=True))
        dx_ref[...] = dx
        do_ref[...] = (gt_ref[...] * dx).astype(BF)
        dgt_ref[...] += _colsum(dx * o_ref[...].astype(F32))

    rev = lambda i: (nrow - 1 - i, 0)
    row = pl.BlockSpec((tm, D), rev)
    vec = pl.BlockSpec((1, D), lambda i: (0, 0))
    wide = pl.BlockSpec((tm, 2 * dff), rev)
    half = pl.BlockSpec((tm, dff), rev)
    cw3 = pl.BlockSpec((3, 2 * dff), lambda i: (0, 0))
    cb1 = pl.BlockSpec((1, 2 * dff), lambda i: (0, 0))
    vshape = jax.ShapeDtypeStruct((1, D), F32)
    return _pcall(body, name="ffn_bwd", grid=(nrow,),
                  in_specs=[row, _resident(wd.shape), wide, half, half, _resident(cw.shape), _resident(w_up4.shape),
                            row, row, vec, vec, vec, row],
                  out_specs=[wide, cw3, cb1, row, vec, vec, vec, row, vec],
                  out_shape=[jax.ShapeDtypeStruct((T, 2 * dff), BF), jax.ShapeDtypeStruct((3, 2 * dff), F32),
                             jax.ShapeDtypeStruct((1, 2 * dff), F32), jax.ShapeDtypeStruct((T, D), F32),
                             vshape, vshape, vshape, jax.ShapeDtypeStruct((T, D), BF), vshape],
                  scratch=[pltpu.VMEM((SUBLANES, 2 * dff), F32)], vmem_mb=60)(
        dfo, wd, up, ga, vd, cw, w_up4, x2, resid, g, scale, gate, o)


def _mm_tn_cols(a, b, name, nshard, nb, mb=None, tm=2048):
    T, M = a.shape
    tm = min(tm, T)
    mb = M if mb is None else mb
    ns = b.shape[1] // nshard
    per = ns // nb
    nk = T // tm

    def body(a_ref, b_ref, o_ref, c_ref):
        k = pl.program_id(2)

        @pl.when(k == 0)
        def _():
            o_ref[...] = jnp.zeros_like(o_ref)

        o_ref[0] += _dot_tn(a_ref[...], b_ref[...])

        @pl.when(k == nk - 1)
        def _():
            c_ref[...] = o_ref[...].astype(BF)

    out_spec = pl.BlockSpec((1, mb, nb), lambda m, t, k: (t // per, m, t % per))
    return _pcall(body, name=name, grid=(M // mb, nshard * per, nk),
                  in_specs=[pl.BlockSpec((tm, mb), lambda m, t, k: (k, m)),
                            pl.BlockSpec((tm, nb), lambda m, t, k: (k, t))],
                  out_specs=[out_spec, out_spec],
                  out_shape=[jax.ShapeDtypeStruct((nshard, M, ns), F32), jax.ShapeDtypeStruct((nshard, M, ns), BF)],
                  vmem_mb=48)(a, b)


def _mm_nt_normbwd(dz, w4, x, resid, g, scale, name, tm=256):
    T = x.shape[0]
    tm = min(tm, T)
    ns = w4.shape[2]

    def body(dz_ref, w_ref, x_ref, r_ref, g_ref, sc_ref, dx_ref, dsh_ref, dsc_ref, dg_ref):
        i = pl.program_id(0)

        @pl.when(i == 0)
        def _():
            dsh_ref[...] = jnp.zeros_like(dsh_ref)
            dsc_ref[...] = jnp.zeros_like(dsc_ref)
            dg_ref[...] = jnp.zeros_like(dg_ref)

        dh = None
        for j in range(N_CHIPS):
            part = _dot_nt(dz_ref[:, j * ns:(j + 1) * ns], w_ref[j])
            dh = part if dh is None else dh + part
        rstd, xh = _rms_stats(x_ref[...])
        dsh_ref[...] += _colsum(dh)
        dsc_ref[...] += _colsum(dh * (xh * g_ref[...]))
        dn = dh * (1.0 + sc_ref[...])
        dg_ref[...] += _colsum(dn * xh)
        dxh = dn * g_ref[...]
        dx_ref[...] = r_ref[...] + rstd * (dxh - xh * jnp.mean(dxh * xh, axis=-1, keepdims=True))

    row = pl.BlockSpec((tm, D), lambda i: (i, 0))
    vec = pl.BlockSpec((1, D), lambda i: (0, 0))
    return _pcall(body, name=name, grid=(T // tm,),
                  in_specs=[pl.BlockSpec((tm, N_CHIPS * ns), lambda i: (i, 0)), _resident(w4.shape), row, row, vec, vec],
                  out_specs=[row, vec, vec, vec],
                  out_shape=[jax.ShapeDtypeStruct((T, D), F32)] + [jax.ShapeDtypeStruct((1, D), F32)] * 3,
                  vmem_mb=48)(dz, w4, x, resid, g, scale)


def _mix_bwd(do, ya, yb, z, wo, wba, wbb, tm=256):
    T = do.shape[0]
    tm = min(tm, T)

    def body(do_ref, ya_ref, yb_ref, ga_ref, gb_ref, wo_ref, wa_ref, wb_ref,
             dz_ref, dya_ref, dyb_ref, dyap_ref, dybp_ref):
        dm = _dot_nt(do_ref[...], wo_ref[...])
        sa = _sigmoid_t(ga_ref[...])
        sb = _sigmoid_t(gb_ref[...])
        dya = (sa * dm).astype(BF)
        dyb = (sb * dm).astype(BF)
        dz_ref[:, 0:D] = (dm * ya_ref[...].astype(F32) * sa * (1.0 - sa)).astype(BF)
        dz_ref[:, D:2 * D] = (dm * yb_ref[...].astype(F32) * sb * (1.0 - sb)).astype(BF)
        dya_ref[...] = dya
        dyb_ref[...] = dyb
        dyap_ref[...] = _dot_nt(dya, wa_ref[...]).astype(BF)
        dybp_ref[...] = _dot_nt(dyb, wb_ref[...]).astype(BF)

    row = pl.BlockSpec((tm, D), lambda i: (i, 0))
    wspec = pl.BlockSpec((D, D), lambda i: (0, 0))
    return _pcall(body, name="mix_bwd", grid=(T // tm,),
                  in_specs=[row, row, row, pl.BlockSpec((tm, D), lambda i: (i, 4)),
                            pl.BlockSpec((tm, D), lambda i: (i, 5)), wspec, wspec, wspec],
                  out_specs=[pl.BlockSpec((tm, 2 * D), lambda i: (i, 2)), row, row, row, row],
                  out_shape=[jax.ShapeDtypeStruct((T, 6 * D), BF)] + [jax.ShapeDtypeStruct((T, D), BF)] * 4,
                  vmem_mb=48)(do, ya, yb, z, z, wo, wba, wbb)


def _sgu_bwd(dz, dyb_pre, z, lg, lb, ws, bst, tb=256):
    T = z.shape[0]
    tb = min(tb, T)

    def body(dz_in, dy_ref, zu_ref, zv_ref, lg_ref, lb_ref, ws_ref, bst_ref,
             dz_ref, dws_ref, dbst_ref, dlg_ref, dlb_ref):
        del dz_in
        i = pl.program_id(0)

        @pl.when(i == 0)
        def _():
            dws_ref[...] = jnp.zeros_like(dws_ref)
            dbst_ref[...] = jnp.zeros_like(dbst_ref)
            dlg_ref[...] = jnp.zeros_like(dlg_ref)
            dlb_ref[...] = jnp.zeros_like(dlb_ref)

        gu, dgu = _gelu_and_grad(zu_ref[...])
        gv, dgv = _gelu_and_grad(zv_ref[...])
        rstd, xh = _layernorm_stats(gv)
        vln = xh * lg_ref[...] + lb_ref[...]
        wm, mixed = _sgu_mix(vln, ws_ref, bst_ref, tb)
        dy = dy_ref[...].astype(F32)
        dz_ref[:, 0:D] = (dy * mixed * dgu).astype(BF)
        dmixed = dy * gu
        ri = lax.broadcasted_iota(jnp.int32, (SGU_BLOCK, SGU_BLOCK), 0)
        ci = lax.broadcasted_iota(jnp.int32, (SGU_BLOCK, SGU_BLOCK), 1)
        blocks = []
        for blk in range(tb // SGU_BLOCK):
            rs = slice(blk * SGU_BLOCK, (blk + 1) * SGU_BLOCK)
            cols = []
            for g in range(HEADS):
                cs = slice(g * HD, (g + 1) * HD)
                dmg = dmixed[rs, cs]
                dmb = dmg.astype(BF)
                dbst_ref[:, g:g + 1] += jnp.sum(dmg, axis=1, keepdims=True)
                dws_ref[g] += jnp.where(ri >= ci, _dot_nt(dmb, vln[rs, cs].astype(BF)), 0.0)
                cols.append(_dot_tn(wm[g], dmb))
            blocks.append(jnp.concatenate(cols, axis=1))
        dvln = blocks[0] if len(blocks) == 1 else jnp.concatenate(blocks, axis=0)
        dlg_ref[...] += _colsum(dvln * xh)
        dlb_ref[...] += _colsum(dvln)
        dxh = dvln * lg_ref[...]
        dgv_in = rstd * (dxh - jnp.mean(dxh, axis=-1, keepdims=True)
                         - xh * jnp.mean(dxh * xh, axis=-1, keepdims=True))
        dz_ref[:, D:2 * D] = (dgv_in * dgv).astype(BF)

    row = pl.BlockSpec((tb, D), lambda i: (i, 0))
    vec = pl.BlockSpec((1, D), lambda i: (0, 0))
    wspec = pl.BlockSpec((HEADS, SGU_BLOCK, SGU_BLOCK), lambda i: (0, 0, 0))
    bspec = pl.BlockSpec((SGU_BLOCK, HEADS), lambda i: (0, 0))
    return _pcall(body, name="sgu_bwd", grid=(T // tb,),
                  in_specs=[HBM_SPEC, row, pl.BlockSpec((tb, D), lambda i: (i, 2)),
                            pl.BlockSpec((tb, D), lambda i: (i, 3)), vec, vec, wspec, bspec],
                  out_specs=[pl.BlockSpec((tb, 2 * D), lambda i: (i, 1)), wspec, bspec, vec, vec],
                  out_shape=[jax.ShapeDtypeStruct(dz.shape, BF),
                             jax.ShapeDtypeStruct((HEADS, SGU_BLOCK, SGU_BLOCK), F32),
                             jax.ShapeDtypeStruct((SGU_BLOCK, HEADS), F32),
                             jax.ShapeDtypeStruct((1, D), F32), jax.ShapeDtypeStruct((1, D), F32)],
                  aliases={0: 0}, vmem_mb=48)(dz, dyb_pre, z, z, lg, lb, ws, bst)


def _rglru_bwd(dz, dya_pre, z, h, xc_s, r_s, ig_s, mult_s, a_s, cw, wa, wx, lam, tb=256):
    T = z.shape[0]
    tb = min(tb, T)
    nrow = T // tb
    per = tb // SUBLANES

    def body(dz_in, dy_ref, xr_ref, gr_ref, h_ref, hh_ref, xc_ref, r_ref, ig_ref, mu_ref, a_ref, cw_ref, wa_ref,
             wx_ref, lam_ref, dz_ref, dcw_ref, dcb_ref, dwa_ref, dba_ref, dwx_ref, dbx_ref, dlam_ref, carry, nxt):
        del dz_in
        i = pl.program_id(0)
        first_block = i == nrow - 1

        @pl.when(i == 0)
        def _():
            carry[...] = jnp.zeros_like(carry)
            nxt[...] = jnp.zeros_like(nxt)
            for ref in (dcw_ref, dcb_ref, dwa_ref, dba_ref, dwx_ref, dbx_ref, dlam_ref):
                ref[...] = jnp.zeros_like(ref)

        xc = xc_ref[...].astype(F32)
        r = r_ref[...].astype(F32)
        ig = ig_ref[...].astype(F32)
        mult = mu_ref[...].astype(F32)
        a = a_ref[...]
        lam = lam_ref[...]
        ls = _log_sigmoid(lam)
        hv = h_ref[...]
        hprev = _shift_down(hv, jnp.where(first_block, 0.0, hh_ref[...]), 1)
        gg, dgg = _gelu_and_grad(gr_ref[...])
        dy = dy_ref[...].astype(F32)
        dz_ref[:, D:2 * D] = (dy * hv * dgg).astype(BF)

        rows = lax.broadcasted_iota(jnp.int32, (tb, D), 0)
        v = dy * gg + jnp.where(rows == tb - 1, carry[0:1, :], 0.0)
        q = jnp.where(rows < tb - 1, pltpu.roll(a, tb - 1, 0), 0.0)
        _, gsc = _scan_rows(q, v, reverse=True)
        carry[...] = (a * gsc)[0:SUBLANES]

        xi = ig * xc
        dmult = gsc * xi
        dxi = gsc * mult
        dig = dxi * xc
        dxc = dxi * ig
        dlog_a = gsc * hprev * a - dmult * (a * a) * pl.reciprocal(mult, approx=True)
        dlam_ref[...] += _colsum(dlog_a * r) * (LRU_C * _sigmoid(-lam))
        dpr = dlog_a * (LRU_C * ls) * r * (1.0 - r)
        dpi = dig * ig * (1.0 - ig)
        dba_ref[...] += _colsum(dpr)
        dbx_ref[...] += _colsum(dpi)
        back = []
        for hh in range(HEADS):
            cs = slice(hh * HD, (hh + 1) * HD)
            xh = xc[:, cs].astype(BF)
            dprh = dpr[:, cs].astype(BF)
            dpih = dpi[:, cs].astype(BF)
            dwa_ref[hh] += _dot_tn(xh, dprh)
            dwx_ref[hh] += _dot_tn(xh, dpih)
            back.append(_dot_nt(dprh, wa_ref[hh].astype(BF)) + _dot_nt(dpih, wx_ref[hh].astype(BF)))
        dxc = dxc + jnp.concatenate(back, axis=1)

        n8 = nxt[...]
        d1 = _shift_up(dxc, n8, 1)
        d2 = _shift_up(dxc, n8, 2)
        d3 = _shift_up(dxc, n8, 3)
        nxt[...] = dxc[0:SUBLANES]
        dz_ref[:, 0:D] = (cw_ref[3:4, :] * dxc + cw_ref[2:3, :] * d1 + cw_ref[1:2, :] * d2
                          + cw_ref[0:1, :] * d3).astype(BF)
        xr = xr_ref[...]
        dcw_ref[3:4, :] += _colsum(dxc * xr)
        dcw_ref[2:3, :] += _colsum(d1 * xr)
        dcw_ref[1:2, :] += _colsum(d2 * xr)
        dcw_ref[0:1, :] += _colsum(d3 * xr)
        dcb_ref[...] += _colsum(dxc)

    rev = lambda col: (lambda i: (nrow - 1 - i, col))
    row = pl.BlockSpec((tb, D), rev(0))
    halo = pl.BlockSpec((SUBLANES, D), lambda i: (jnp.maximum((nrow - 1 - i) * per - 1, 0), 0))
    vec = pl.BlockSpec((1, D), lambda i: (0, 0))
    wspec = pl.BlockSpec((HEADS, HD, HD), lambda i: (0, 0, 0))
    c4 = pl.BlockSpec((4, D), lambda i: (0, 0))
    wshape = jax.ShapeDtypeStruct((HEADS, HD, HD), F32)
    vshape = jax.ShapeDtypeStruct((1, D), F32)
    return _pcall(body, name="rglru_bwd", grid=(nrow,),
                  in_specs=[HBM_SPEC, row, row, pl.BlockSpec((tb, D), rev(1)), row, halo,
                            row, row, row, row, row, c4, wspec, wspec, vec],
                  out_specs=[pl.BlockSpec((tb, 2 * D), rev(0)), c4, vec, wspec, vec, wspec, vec, vec],
                  out_shape=[jax.ShapeDtypeStruct(dz.shape, BF), jax.ShapeDtypeStruct((4, D), F32), vshape,
                             wshape, vshape, wshape, vshape, vshape],
                  scratch=[pltpu.VMEM((SUBLANES, D), F32), pltpu.VMEM((SUBLANES, D), F32)],
                  aliases={0: 0}, vmem_mb=56)(dz, dya_pre, z, z, h, h, xc_s, r_s, ig_s, mult_s, a_s, cw, wa, wx, lam)


def _pack_rows(parts):
    out = []
    for p in parts:
        q = p.reshape(-1, LANES)
        pad = (-q.shape[0]) % SUBLANES
        if pad:
            q = jnp.concatenate([q, jnp.zeros((pad, LANES), q.dtype)], axis=0)
        out.append(q)
    return jnp.concatenate(out, axis=0)


def _rows_of(shape):
    n = 1
    for s in shape:
        n *= s
    rows = n // LANES
    return rows + (-rows) % SUBLANES


def kernel(x, c, w_ada, b_ada, norm_mix_g, w_in, rnn_conv_w, rnn_conv_b, lru_w_a, lru_b_a, lru_w_x, lru_b_x, lru_lambda, sgu_ln_g, sgu_ln_b, sgu_w_s, sgu_b_s, w_branch_a, w_branch_b, w_out, norm_ffn_g, w_up, ffn_conv_w, ffn_conv_b, w_down, norm_final_g, loss_target, m_w_ada, m_b_ada, m_norm_mix_g, m_w_in, m_rnn_conv_w, m_rnn_conv_b, m_lru_w_a, m_lru_b_a, m_lru_w_x, m_lru_b_x, m_lru_lambda, m_sgu_ln_g, m_sgu_ln_b, m_sgu_w_s, m_sgu_b_s, m_w_branch_a, m_w_branch_b, m_w_out, m_norm_ffn_g, m_w_up, m_ffn_conv_w, m_ffn_conv_b, m_w_down, m_norm_final_g, v_w_ada, v_b_ada, v_norm_mix_g, v_w_in, v_rnn_conv_w, v_rnn_conv_b, v_lru_w_a, v_lru_b_a, v_lru_w_x, v_lru_b_x, v_lru_lambda, v_sgu_ln_g, v_sgu_ln_b, v_sgu_w_s, v_sgu_b_s, v_w_branch_a, v_w_branch_b, v_w_out, v_norm_ffn_g, v_w_up, v_ffn_conv_w, v_ffn_conv_b, v_w_down, v_norm_final_g):
    args = dict(locals())
    T = x.shape[1]
    mx, my, mc = lax.axis_index("x"), lax.axis_index("y"), lax.axis_index("c")
    chip = 2 * mx + my
    dev = 2 * chip + mc
    vec = lambda a: a.reshape(1, -1)

    xt = x.reshape(T, D)
    tgt = loss_target.reshape(T, D)
    ns = w_in.shape[2]
    dff = w_down.shape[1] * N_CHIPS

    c_all = _gather8(c.reshape(SUBLANES, LANES), "gather_c").reshape(N_DEV, D)
    b_ada_sh = lax.dynamic_slice(b_ada, (0, chip * ns), (1, ns))
    mod_sh = _mod_fwd(c_all, w_ada[0], b_ada_sh)

    mixer_w = _cast_shards([w_in[0], w_branch_a[0], w_branch_b[0], w_out[0]], "cast_mixer_weights")
    w_in4, wba4, wbb4, wo4, rcw4, fcw4, mod4 = _gather_weights(
        list(mixer_w) + [rnn_conv_w[0], ffn_conv_w[0], mod_sh], [True] * 4 + [False] * 3)
    late = _cast_shards([w_up[0], w_down[0]], "cast_late", after=mod4)
    late_plan = _gather_plan(len(late))
    late_handle, late_token = _remote_start(
        late, [lax.empty((N_CHIPS,) + w.shape, w.dtype) for w in late], late_plan, 3 * len(late), "gather_late_start")
    rcw_full = jnp.transpose(rcw4, (1, 0, 2)).reshape(4, D)
    fcw_full = jnp.transpose(fcw4, (1, 0, 2)).reshape(3, 2 * dff)
    mod = lax.dynamic_index_in_dim(mod4, dev, axis=1, keepdims=False).reshape(1, 6 * D)
    shift1, scale1, gate1, shift2, scale2, gate2 = [mod[:, k * D:(k + 1) * D] for k in range(6)]

    bst = jnp.transpose(sgu_b_s[0])
    wba_full = wba4.reshape(D, D)
    wbb_full = wbb4.reshape(D, D)
    wo_full = wo4.reshape(D, D)
    h1, z, h_lru, ya_pre, yb_pre, merged, ya, yb, o1, x2, lru_xc, lru_r, lru_i, lru_mult, lru_a = _mixer_fwd(
        xt, norm_mix_g, scale1 + late_token[0:1, 0:1], shift1, gate1, w_in4, rcw_full, rnn_conv_b,
        lru_w_a[0], lru_b_a, lru_w_x[0], lru_b_x, lru_lambda, sgu_ln_g, sgu_ln_b, sgu_w_s[0], bst,
        wba_full, wbb_full, wo_full)
    late, late_lands = _remote_wait(late_handle, late_plan, o1, "gather_late_wait")
    w_up4, w_down4 = _place_own(late, late_lands)
    wd_full = w_down4.reshape(dff, D)
    h2, up, f, ffn_ga, ffn_vd, loss_part, dx3, dfo, dgf, dgate2 = _ffn_fwd(
        x2, norm_ffn_g, scale2, shift2, gate2, vec(norm_final_g), w_up4, wd_full, fcw_full, ffn_conv_b, tgt)

    dup, dfcw, dfcb, dx2, dshift2, dscale2, dg_ffn, do1, dgate1 = _ffn_bwd(
        dfo, wd_full, up, ffn_ga, ffn_vd, fcw_full, w_up4, x2, dx3, norm_ffn_g, scale2, gate1, o1)
    dwd = _mm_tn_cols(f, dfo, "dw_down", 1, D, mb=D)
    dw_up4 = _mm_tn_cols(h2, dup, "dw_up", N_CHIPS, ns)
    dz, dya, dyb, dya_pre, dyb_pre = _mix_bwd(do1, ya, yb, z, wo_full, wba_full, wbb_full)
    dwo = _mm_tn_cols(merged, do1, "dw_out", 1, D)
    dwba = _mm_tn_cols(ya_pre, dya, "dw_branch_a", 1, D)
    dwbb = _mm_tn_cols(yb_pre, dyb, "dw_branch_b", 1, D)

    chip_id = chip.astype(jnp.int32).reshape(1)

    def reduce_start(group, name):
        wire = [g16.reshape(N_CHIPS, -1, g16.shape[-1]) for _, (_, g16) in group]
        lands = [lax.empty((3,) + w.shape[1:], w.dtype) for w in wire]
        return _remote_start(wire, lands, _scatter_plan(len(group)), 3 * len(group), "scatter_start_" + name)

    def reduce_finish(group, handle, after, name):
        _, landed = _remote_wait(handle, _scatter_plan(len(group)), after, "scatter_wait_" + name)
        return [_sum_own_and_landed(chip_id, g32.reshape(N_CHIPS, -1, g32.shape[-1]), l, "sum_chips_" + n)
                for (n, (g32, _)), l in zip(group, landed)]

    group1 = [("w_up", dw_up4), ("w_down", dwd), ("w_branch_a", dwba), ("w_branch_b", dwbb), ("w_out", dwo)]
    handle1, token1 = reduce_start(group1, "late")
    dz, dws, dbst, dlg, dlb = _sgu_bwd(dz, dyb_pre, z, sgu_ln_g + token1[0:1, 0:1], sgu_ln_b, sgu_w_s[0], bst)
    dz, drcw, drcb, dwa, dba, dwx, dbx, dlam = _rglru_bwd(
        dz, dya_pre, z, h_lru, lru_xc, lru_r, lru_i, lru_mult, lru_a, rcw_full, lru_w_a[0], lru_w_x[0], lru_lambda)
    early_small = [("rnn_conv_b", drcb), ("lru_w_a", dwa), ("lru_b_a", dba), ("lru_w_x", dwx), ("lru_b_x", dbx),
                   ("lru_lambda", dlam), ("sgu_ln_g", dlg), ("sgu_ln_b", dlb), ("sgu_w_s", dws),
                   ("sgu_b_s", jnp.transpose(dbst)), ("norm_ffn_g", dg_ffn),
                   ("ffn_conv_b", dfcb), ("norm_final_g", dgf)]
    r_early = sum(_rows_of(args[n].shape) for n, _ in early_small)
    early_pack = _pack_rows([g for _, g in early_small] + [drcw, dfcw])
    early_pack = jnp.concatenate(
        [early_pack, jnp.zeros(((-early_pack.shape[0]) % 256, LANES), F32)], axis=0)
    early_chip = _add_pair(early_pack, _swap_cores([early_pack], "swap_small_grads")[0], "sum_cores_small_grads")
    early_handle, token3 = _remote_start([early_chip], [lax.empty((3,) + early_chip.shape, F32)], _bcast_plan, 3,
                                         "small_grads_start")
    totals1 = reduce_finish(group1, handle1, drcb, "late")
    group2 = [("w_in", _mm_tn_cols(h1, dz, "dw_in", N_CHIPS, ns))]
    handle2, token2 = reduce_start(group2, "in")
    grad_x, dshift1, dscale1, dg_mix = _mm_nt_normbwd(
        dz, w_in4, xt, dx2, norm_mix_g + (token2[0:1, 0:1] + token3[0:1, 0:1]), scale1, "dh1_norm_bwd")
    totals2 = reduce_finish(group2, handle2, dg_mix, "in")
    dmod = jnp.concatenate([dshift1, dscale1, dgate1, dshift2, dscale2, dgate2], axis=1)

    big = group1 + group2
    mine = totals1 + totals2
    theirs = _swap_cores(mine, "swap_core_sums")
    out = {}
    for (n, _), a, b in zip(big, mine, theirs):
        shape = args[n].shape
        res = _adamw(args[n][0], args["m_" + n][0], args["v_" + n][0], [a, b], "adamw_" + n)
        for kind, r in zip(("grad_", "delta_", "new_m_", "new_v_"), res):
            out[kind + n] = r.reshape(shape)

    late_small = [("b_ada", dmod), ("norm_mix_g", dg_mix)]
    small = late_small + early_small
    late_all = _gather8(_pack_rows([g for _, g in late_small] + [loss_part]), "gather_late_small_grads")
    late_sum = _sum_parts(late_all, "sum_late_small_grads")
    r_late = sum(_rows_of(args[n].shape) for n, _ in late_small)
    loss = late_sum[r_late, 0]
    late_sum = late_sum[:r_late]
    _, (early_landed,) = _remote_wait(early_handle, _bcast_plan, dg_mix, "small_grads_wait")
    early_sum = _sum_chips_in_order(chip_id, early_chip, early_landed, "sum_early_small_grads")
    r_small = sum(_rows_of(args[n].shape) for n, _ in small)
    r_pad = r_small + (-r_small) % 256
    fill = jnp.zeros((r_pad - r_small, LANES), F32)
    g_small = jnp.concatenate([late_sum, early_sum[:r_early], fill], axis=0)

    def pack_small(prefix):
        return jnp.concatenate([_pack_rows([args[prefix + n] for n, _ in small]), fill], axis=0)

    res = _adamw(pack_small(""), pack_small("m_"), pack_small("v_"), [g_small], "adamw_small")
    off = 0
    for n, _ in small:
        shape = args[n].shape
        rows = _rows_of(shape)
        for kind, r in zip(("grad_", "delta_", "new_m_", "new_v_"), res):
            out[kind + n] = r[off:off + rows].reshape(shape)
        off += rows

    rcw_cols = rnn_conv_w.shape[2]
    g_rcw = lax.dynamic_slice(early_sum[r_early:r_early + 32].reshape(4, D), (0, chip * rcw_cols), (4, rcw_cols))
    g_fcw = lax.dynamic_slice(early_sum[r_early + 32:r_early + 32 + 144].reshape(3, 2 * dff), (0, chip * ns), (3, ns))
    conv = [("rnn_conv_w", g_rcw), ("ffn_conv_w", g_fcw)]
    res = _adamw(_pack_rows([args[n] for n, _ in conv]), _pack_rows([args["m_" + n] for n, _ in conv]),
                 _pack_rows([args["v_" + n] for n, _ in conv]), [_pack_rows([g for _, g in conv])], "adamw_conv")
    off = 0
    for n, _ in conv:
        shape = args[n].shape
        cnt = shape[1] * shape[2] // LANES
        for kind, r in zip(("grad_", "delta_", "new_m_", "new_v_"), res):
            out[kind + n] = r[off:off + cnt].reshape(shape)
        off += _rows_of(shape)

    dmod_all = late_all[:, 0:6 * D // LANES, :].reshape(N_DEV, 6 * D)
    dmod_sh = lax.dynamic_slice(dmod_all, (0, chip * ns), (N_DEV, ns))
    res = _ada_adamw(jnp.transpose(c_all), dmod_sh, w_ada[0], m_w_ada[0], v_w_ada[0])
    for kind, r in zip(("grad_", "delta_", "new_m_", "new_v_"), res):
        out[kind + "w_ada"] = r.reshape(w_ada.shape)

    names = ["w_ada", "b_ada", "norm_mix_g", "w_in", "rnn_conv_w", "rnn_conv_b", "lru_w_a", "lru_b_a", "lru_w_x",
             "lru_b_x", "lru_lambda", "sgu_ln_g", "sgu_ln_b", "sgu_w_s", "sgu_b_s", "w_branch_a", "w_branch_b",
             "w_out", "norm_ffn_g", "w_up", "ffn_conv_w", "ffn_conv_b", "w_down", "norm_final_g"]
    result = [loss, grad_x.reshape(x.shape)]
    for kind in ("grad_", "delta_", "new_m_", "new_v_"):
        result += [out[kind + n] for n in names]
    return tuple(result)
```

```python
import jax
import jax.numpy as jnp
from jax import lax
from jax.experimental import pallas as pl
from jax.experimental.pallas import tpu as pltpu

F32 = jnp.float32
BF = jnp.bfloat16

D = 1024
HEADS = 8
HD = D // HEADS
SGU_BLOCK = 128
N_CHIPS = 4
N_DEV = 8
EPS = 1e-6
LRU_C = 8.0
LANES = 128
SUBLANES = 8

ADAM_LR = 0.001
ADAM_B1 = 0.9
ADAM_B2 = 0.999
ADAM_EPS = 1e-08
ADAM_WD = 0.01
ADAM_STEP = 10

GELU_K0 = 0.7978845608028654
GELU_K1 = 0.044715

HBM_SPEC = pl.BlockSpec(memory_space=pltpu.HBM)
MESH_ID = pl.DeviceIdType.MESH


def _pcall(body, *, name, out_shape, grid=(), in_specs=None, out_specs=None, scratch=(), vmem_mb=32, aliases=None,
           grid_spec=None):
    kw = {}
    if aliases:
        kw["input_output_aliases"] = aliases
    if grid_spec is not None:
        kw["grid_spec"] = grid_spec
        ndim = len(grid_spec.grid)
    else:
        kw.update(grid=grid, in_specs=in_specs, out_specs=out_specs, scratch_shapes=list(scratch))
        ndim = len(grid)
    if ndim:
        params = pltpu.CompilerParams(dimension_semantics=("arbitrary",) * ndim, vmem_limit_bytes=vmem_mb * 2 ** 20)
    else:
        params = pltpu.CompilerParams(vmem_limit_bytes=vmem_mb * 2 ** 20)
    return pl.pallas_call(body, name=name, out_shape=out_shape, compiler_params=params, **kw)


def _gelu_cdf(x, x2):
    return 0.5 * jnp.tanh(x * (GELU_K0 + (GELU_K0 * GELU_K1) * x2)) + 0.5


def _gelu(x):
    return x * _gelu_cdf(x, x * x)


def _gelu_and_grad(x):
    x2 = x * x
    s = _gelu_cdf(x, x2)
    g = x * s
    dg = s * (1.0 + (x - g) * ((2.0 * GELU_K0) + (6.0 * GELU_K0 * GELU_K1) * x2))
    return g, dg


def _sigmoid(x):
    return 1.0 / (1.0 + jnp.exp(-x))


def _sigmoid_t(x):
    return 0.5 * jnp.tanh(0.5 * x) + 0.5


def _log_sigmoid(x):
    e = jnp.exp(-jnp.abs(x))
    u = 1.0 + e
    d = u - 1.0
    l1p = jnp.where(d == 0.0, e, jnp.log(u) * (e / jnp.where(d == 0.0, 1.0, d)))
    return jnp.minimum(x, 0.0) - l1p


def _dot(a, b):
    return jnp.dot(a, b, preferred_element_type=F32)


def _dot_nt(a, b):
    return lax.dot_general(a, b, (((1,), (1,)), ((), ())), preferred_element_type=F32)


def _dot_tn(a, b):
    return lax.dot_general(a, b, (((0,), (0,)), ((), ())), preferred_element_type=F32)


def _shift_down(x, halo, s):
    r = pltpu.roll(x, s, 0)
    rows = lax.broadcasted_iota(jnp.int32, (SUBLANES, x.shape[1]), 0)
    head = jnp.where(rows < s, pltpu.roll(halo, s, 0), r[0:SUBLANES])
    return jnp.concatenate([head, r[SUBLANES:]], axis=0)


def _shift_up(x, halo, s):
    n = x.shape[0]
    r = pltpu.roll(x, n - s, 0)
    rows = lax.broadcasted_iota(jnp.int32, (SUBLANES, x.shape[1]), 0)
    tail = jnp.where(rows >= SUBLANES - s, pltpu.roll(halo, SUBLANES - s, 0), r[n - SUBLANES:n])
    return jnp.concatenate([r[:n - SUBLANES], tail], axis=0)


def _scan_rows(a, u, reverse):
    n, width = a.shape
    rows = lax.broadcasted_iota(jnp.int32, (n, width), 0)
    d = 1
    while d < n:
        if d < SUBLANES:
            keep = rows < n - d if reverse else rows >= d
            shift = n - d if reverse else d
            a_s = jnp.where(keep, pltpu.roll(a, shift, 0), 1.0)
            u_s = jnp.where(keep, pltpu.roll(u, shift, 0), 0.0)
        elif reverse:
            a_s = jnp.concatenate([a[d:], jnp.ones((d, width), a.dtype)], axis=0)
            u_s = jnp.concatenate([u[d:], jnp.zeros((d, width), u.dtype)], axis=0)
        else:
            a_s = jnp.concatenate([jnp.ones((d, width), a.dtype), a[:n - d]], axis=0)
            u_s = jnp.concatenate([jnp.zeros((d, width), u.dtype), u[:n - d]], axis=0)
        u = a * u_s + u
        a = a * a_s
        d *= 2
    return a, u


def _colsum(x):
    return jnp.sum(x, axis=0, keepdims=True)


def _rms_stats(x):
    r = lax.rsqrt(jnp.mean(x * x, axis=-1, keepdims=True) + EPS)
    return r, x * r


def _lru_gates(xc, wa_ref, ba, wx_ref, bx, lam, head0=0):
    pr, pi = [], []
    for hh in range(xc.shape[1] // HD):
        xh = xc[:, hh * HD:(hh + 1) * HD].astype(BF)
        pr.append(_dot(xh, wa_ref[head0 + hh].astype(BF)))
        pi.append(_dot(xh, wx_ref[head0 + hh].astype(BF)))
    r = _sigmoid_t((pr[0] if len(pr) == 1 else jnp.concatenate(pr, axis=1)) + ba)
    ig = _sigmoid_t((pi[0] if len(pi) == 1 else jnp.concatenate(pi, axis=1)) + bx)
    ls = _log_sigmoid(lam)
    log_a = LRU_C * r * ls
    a = jnp.exp(log_a)
    x2 = 2.0 * log_a
    u = a * a
    lu = jnp.log(jnp.maximum(u, 1e-37))
    ratio = x2 * pl.reciprocal(jnp.where(lu == 0.0, 1.0, lu), approx=True)
    em1 = jnp.where(lu == 0.0, x2, jnp.where(u < 1e-30, -1.0, (u - 1.0) * ratio))
    mult = jnp.sqrt(-em1)
    return r, ig, ls, a, mult


def _sgu_mix(vln, ws_ref, bst_ref, tb):
    ri = lax.broadcasted_iota(jnp.int32, (SGU_BLOCK, SGU_BLOCK), 0)
    ci = lax.broadcasted_iota(jnp.int32, (SGU_BLOCK, SGU_BLOCK), 1)
    wm = [jnp.where(ri >= ci, ws_ref[g], 0.0).astype(BF) for g in range(HEADS)]
    blocks = []
    for blk in range(tb // SGU_BLOCK):
        cols = []
        for g in range(HEADS):
            vb = vln[blk * SGU_BLOCK:(blk + 1) * SGU_BLOCK, g * HD:(g + 1) * HD].astype(BF)
            cols.append(_dot(wm[g], vb) + bst_ref[:, g:g + 1])
        blocks.append(jnp.concatenate(cols, axis=1))
    mixed = blocks[0] if len(blocks) == 1 else jnp.concatenate(blocks, axis=0)
    return wm, mixed


def _layernorm_stats(v):
    mu = jnp.mean(v, axis=-1, keepdims=True)
    vc = v - mu
    rstd = lax.rsqrt(jnp.mean(vc * vc, axis=-1, keepdims=True) + EPS)
    return rstd, vc * rstd


def _my_xyc():
    return lax.axis_index("x"), lax.axis_index("y"), lax.axis_index("c")


def _gather_weights(srcs, halve):
    n = len(srcs)
    out_shape = [jax.ShapeDtypeStruct((N_CHIPS,) + s.shape, s.dtype) for s in srcs]

    def body(*refs):
        src, out = refs[:n], refs[n:2 * n]
        send_sems, recv_sems, fwd_send, fwd_recv, loc_sems = refs[2 * n:]
        x, y, c = _my_xyc()
        me = 2 * x + y
        chips = [(1 - x, y), (x, 1 - y), (1 - x, 1 - y)]

        def half(ref, a, which):
            if not halve[a]:
                return ref
            h = srcs[a].shape[0] // 2
            return ref.at[pl.ds(which * h, h)]

        def ici(a, k, frm):
            px, py = chips[k]
            return pltpu.make_async_remote_copy(
                src_ref=half(src[a], a, c), dst_ref=half(out[a].at[frm], a, c),
                send_sem=send_sems.at[a, k], recv_sem=recv_sems.at[a, k],
                device_id=(px, py, c), device_id_type=MESH_ID)

        def d2d(a, k, which):
            px, py = chips[k]
            rows = half(out[a].at[2 * px + py], a, which)
            return pltpu.make_async_remote_copy(
                src_ref=rows, dst_ref=rows, send_sem=fwd_send.at[a, k], recv_sem=fwd_recv.at[a, k],
                device_id=(x, y, 1 - c), device_id_type=MESH_ID)

        local, sends = [], []
        for a in range(n):
            lc = pltpu.make_async_copy(src[a], out[a].at[me], loc_sems.at[a])
            lc.start()
            local.append(lc)
            for k in range(3):
                cp = ici(a, k, me)
                cp.start()
                sends.append(cp)
        for a in range(n):
            for k in range(3):
                px, py = chips[k]
                ici(a, k, 2 * px + py).wait_recv()
                if halve[a]:
                    fw = d2d(a, k, c)
                    fw.start()
                    sends.append(fw)
        for a in range(n):
            if halve[a]:
                for k in range(3):
                    d2d(a, k, 1 - c).wait_recv()
        for cp in sends:
            cp.wait_send()
        for lc in local:
            lc.wait()

    sem = pltpu.SemaphoreType.DMA((n, 3))
    return _pcall(body, name="gather_weights", out_shape=out_shape, in_specs=[HBM_SPEC] * n,
                  out_specs=[HBM_SPEC] * n, scratch=[sem, sem, sem, sem, pltpu.SemaphoreType.DMA((n,))])(*srcs)


SEM_SPEC = pl.BlockSpec(memory_space=pltpu.SEMAPHORE)


def _remote_start(srcs, lands, plan, ncopies, name):
    n, m = len(srcs), len(lands)

    def body(*refs):
        src, land = refs[:n], refs[n:n + m]
        send_sems, recv_sems = refs[n + m], refs[n + m + 1]
        token = refs[-1]
        x, y, c = _my_xyc()
        for i, (s, d, dev) in enumerate(plan(src, land, x, y, c)):
            pltpu.make_async_remote_copy(src_ref=s, dst_ref=d, send_sem=send_sems.at[i], recv_sem=recv_sems.at[i],
                                         device_id=dev, device_id_type=MESH_ID).start()
        token[...] = jnp.zeros_like(token)

    bufs = list(srcs) + list(lands)
    out = pl.pallas_call(
        body, name=name,
        out_shape=(pltpu.SemaphoreType.DMA((ncopies,)), pltpu.SemaphoreType.DMA((ncopies,)),
                   *[pltpu.HBM(b.shape, b.dtype) for b in bufs], jax.ShapeDtypeStruct((SUBLANES, LANES), F32)),
        in_specs=[HBM_SPEC] * (n + m),
        out_specs=(SEM_SPEC, SEM_SPEC, *[HBM_SPEC] * (n + m), pl.BlockSpec(memory_space=pltpu.VMEM)),
        input_output_aliases={i: 2 + i for i in range(n + m)},
        compiler_params=pltpu.CompilerParams(has_side_effects=pltpu.SideEffectType.DATAFLOW_SIDE_EFFECTING),
    )(*[pltpu.with_memory_space_constraint(b, pltpu.HBM) for b in bufs])
    return (out[0], out[1], out[2:2 + n], out[2 + n:2 + n + m]), out[-1]


def _remote_wait(handle, plan, after, name):
    send_sems, recv_sems, srcs, lands = handle
    n, m = len(srcs), len(lands)

    def body(*refs):
        src, land = refs[:n], refs[n:n + m]
        ssem, rsem = refs[n + m], refs[n + m + 1]
        x, y, c = _my_xyc()
        for i, (s, d, dev) in enumerate(plan(src, land, x, y, c)):
            cp = pltpu.make_async_remote_copy(src_ref=s, dst_ref=d, send_sem=ssem.at[i], recv_sem=rsem.at[i],
                                              device_id=dev, device_id_type=MESH_ID)
            cp.wait_send()
            cp.wait_recv()

    bufs = list(srcs) + list(lands)
    out = pl.pallas_call(
        body, name=name, out_shape=tuple(pltpu.HBM(b.shape, b.dtype) for b in bufs),
        in_specs=[HBM_SPEC] * (n + m) + [SEM_SPEC, SEM_SPEC, pl.BlockSpec(memory_space=pl.ANY)],
        out_specs=tuple([HBM_SPEC] * (n + m)), input_output_aliases={i: i for i in range(n + m)},
        compiler_params=pltpu.CompilerParams(has_side_effects=pltpu.SideEffectType.DATAFLOW_SIDE_EFFECTING),
    )(*bufs, send_sems, recv_sems, after)
    return out[:n], out[n:]


def _chips_of(x, y):
    return [(1 - x, y), (x, 1 - y), (1 - x, 1 - y)]


def _gather_plan(count):
    def plan(src, land, x, y, c):
        me = 2 * x + y
        return [(src[a], land[a].at[me], (px, py, c)) for a in range(count) for px, py in _chips_of(x, y)]

    return plan


def _place_own(srcs, lands):
    n = len(srcs)

    def body(*refs):
        src, land, sems = refs[:n], refs[2 * n:3 * n], refs[3 * n]
        me = 2 * lax.axis_index("x") + lax.axis_index("y")
        copies = [pltpu.make_async_copy(src[a], land[a].at[me], sems.at[a]) for a in range(n)]
        for cp in copies:
            cp.start()
        for cp in copies:
            cp.wait()

    return _pcall(body, name="place_own_shards", out_shape=[jax.ShapeDtypeStruct(l.shape, l.dtype) for l in lands],
                  in_specs=[HBM_SPEC] * (2 * n), out_specs=[HBM_SPEC] * n, aliases={n + a: a for a in range(n)},
                  scratch=[pltpu.SemaphoreType.DMA((n,))])(*srcs, *lands)


def _gather8(src, name):
    def body(src_ref, out_ref, send_sems, recv_sems, loc_sem):
        x, y, c = _my_xyc()
        me = 4 * x + 2 * y + c
        lc = pltpu.make_async_copy(src_ref, out_ref.at[me], loc_sem)
        lc.start()
        cps = []
        for k in range(1, N_DEV):
            px = 1 - x if (k >> 2) & 1 else x
            py = 1 - y if (k >> 1) & 1 else y
            pc = 1 - c if k & 1 else c
            cp = pltpu.make_async_remote_copy(
                src_ref=src_ref, dst_ref=out_ref.at[me], send_sem=send_sems.at[k - 1], recv_sem=recv_sems.at[k - 1],
                device_id=(px, py, pc), device_id_type=MESH_ID)
            cp.start()
            cps.append(cp)
        for cp in cps:
            cp.wait()
        lc.wait()

    return _pcall(body, name=name, out_shape=jax.ShapeDtypeStruct((N_DEV,) + src.shape, src.dtype),
                  in_specs=[HBM_SPEC], out_specs=HBM_SPEC,
                  scratch=[pltpu.SemaphoreType.DMA((N_DEV - 1,)), pltpu.SemaphoreType.DMA((N_DEV - 1,)),
                           pltpu.SemaphoreType.DMA])(src)


def _cast_shards(arrs, name, after=None):
    n = len(arrs)
    extra = [] if after is None else [after]

    def body(*refs):
        ins, outs = refs[:n], refs[n + len(extra):]
        for a in range(n):
            outs[a][...] = ins[a][...].astype(BF)

    specs = [pl.BlockSpec((s.shape[0] // 4, s.shape[1]), lambda i: (i, 0)) for s in arrs]
    return _pcall(body, name=name, grid=(4,), in_specs=specs + [pl.BlockSpec(memory_space=pl.ANY)] * len(extra),
                  out_specs=specs, out_shape=[jax.ShapeDtypeStruct(s.shape, BF) for s in arrs])(*arrs, *extra)


def _row_tile(rows, cols):
    t = rows
    while t * cols * 4 > (3 << 19) and t % 16 == 0:
        t //= 2
    return t


def _sum_parts(parts, name):
    p, rows, cols = parts.shape
    tr = _row_tile(rows, cols * p // 2)

    def body(p_ref, o_ref):
        acc = p_ref[0].astype(F32)
        for k in range(1, p):
            acc = acc + p_ref[k].astype(F32)
        o_ref[...] = acc

    return _pcall(body, name=name, grid=(rows // tr,),
                  in_specs=[pl.BlockSpec((p, tr, cols), lambda i: (0, i, 0))],
                  out_specs=pl.BlockSpec((tr, cols), lambda i: (i, 0)),
                  out_shape=jax.ShapeDtypeStruct((rows, cols), F32), vmem_mb=48)(parts)


def _sum_own_and_landed(chip, sums, landed, name):
    _, rows, cols = sums.shape
    tr = _row_tile(rows, 2 * cols)

    def body(chip_ref, own_ref, land_ref, o_ref):
        del chip_ref
        acc = own_ref[0].astype(F32)
        for k in range(3):
            acc = acc + land_ref[k].astype(F32)
        o_ref[...] = acc

    grid_spec = pltpu.PrefetchScalarGridSpec(
        num_scalar_prefetch=1, grid=(rows // tr,),
        in_specs=[pl.BlockSpec((1, tr, cols), lambda i, chip_ref: (chip_ref[0], i, 0)),
                  pl.BlockSpec((3, tr, cols), lambda i, chip_ref: (0, i, 0))],
        out_specs=pl.BlockSpec((tr, cols), lambda i, chip_ref: (i, 0)))
    return _pcall(body, name=name, grid_spec=grid_spec, out_shape=jax.ShapeDtypeStruct((rows, cols), F32),
                  vmem_mb=48)(chip, sums, landed)


def _swap_cores(arrs, name):
    n = len(arrs)

    def body(*refs):
        src, out = refs[:n], refs[n:2 * n]
        send_sems, recv_sems = refs[2 * n:]
        x, y, c = _my_xyc()
        cps = []
        for a in range(n):
            cp = pltpu.make_async_remote_copy(
                src_ref=src[a], dst_ref=out[a], send_sem=send_sems.at[a], recv_sem=recv_sems.at[a],
                device_id=(x, y, 1 - c), device_id_type=MESH_ID)
            cp.start()
            cps.append(cp)
        for cp in cps:
            cp.wait()

    sem = pltpu.SemaphoreType.DMA((n,))
    return _pcall(body, name=name, out_shape=[jax.ShapeDtypeStruct(a.shape, a.dtype) for a in arrs],
                  in_specs=[HBM_SPEC] * n, out_specs=[HBM_SPEC] * n, scratch=[sem, sem])(*arrs)


def _add_pair(a, b, name):
    rows, cols = a.shape
    tr = _row_tile(rows, 2 * cols)

    def body(a_ref, b_ref, o_ref):
        o_ref[...] = a_ref[...] + b_ref[...]

    spec = pl.BlockSpec((tr, cols), lambda i: (i, 0))
    return _pcall(body, name=name, grid=(rows // tr,), in_specs=[spec, spec], out_specs=spec,
                  out_shape=jax.ShapeDtypeStruct((rows, cols), F32))(a, b)


def _sum_chips_in_order(chip, own, landed, name):
    rows, cols = own.shape
    tr = _row_tile(rows, 4 * cols)

    def body(chip_ref, own_ref, land_ref, o_ref):
        me = chip_ref[0]
        acc = None
        for p in range(N_CHIPS):
            q = p ^ me
            k = jnp.where(q == 2, 0, jnp.where(q == 1, 1, 2))
            term = jnp.where(q == 0, own_ref[...], land_ref[k])
            acc = term if acc is None else acc + term
        o_ref[...] = acc

    grid_spec = pltpu.PrefetchScalarGridSpec(
        num_scalar_prefetch=1, grid=(rows // tr,),
        in_specs=[pl.BlockSpec((tr, cols), lambda i, chip_ref: (i, 0)),
                  pl.BlockSpec((3, tr, cols), lambda i, chip_ref: (0, i, 0))],
        out_specs=pl.BlockSpec((tr, cols), lambda i, chip_ref: (i, 0)))
    return _pcall(body, name=name, grid_spec=grid_spec, out_shape=jax.ShapeDtypeStruct((rows, cols), F32))(
        chip, own, landed)


def _bcast_plan(src, land, x, y, c):
    return [(src[0], land[0].at[k], (px, py, c)) for k, (px, py) in enumerate(_chips_of(x, y))]


def _sibling_plan(count):
    def plan(src, land, x, y, c):
        return [(src[a], land[a], (x, y, 1 - c)) for a in range(count)]

    return plan


def _scatter_plan(count):
    def plan(src, land, x, y, c):
        out = []
        for a in range(count):
            for k, (px, py) in enumerate(_chips_of(x, y)):
                out.append((src[a].at[2 * px + py], land[a].at[k], (px, py, c)))
        return out

    return plan


def _adamw_math(w, g, m, v):
    m2 = ADAM_B1 * m + (1.0 - ADAM_B1) * g
    v2 = ADAM_B2 * v + (1.0 - ADAM_B2) * (g * g)
    m_hat = m2 / (1.0 - ADAM_B1 ** ADAM_STEP)
    v_hat = v2 / (1.0 - ADAM_B2 ** ADAM_STEP)
    delta = -ADAM_LR * (m_hat / (jnp.sqrt(v_hat) + ADAM_EPS) + ADAM_WD * w)
    return delta, m2, v2


def _adamw(w, m, v, grads, name):
    rows, cols = w.shape
    tr = _row_tile(rows, cols)
    ng = len(grads)

    def body(*refs):
        w_ref, m_ref, v_ref = refs[:3]
        g = refs[3][...]
        for k in range(1, ng):
            g = g + refs[3 + k][...]
        g_ref, d_ref, m2_ref, v2_ref = refs[3 + ng:]
        delta, m2, v2 = _adamw_math(w_ref[...], g, m_ref[...], v_ref[...])
        g_ref[...] = g
        d_ref[...] = delta
        m2_ref[...] = m2
        v2_ref[...] = v2

    spec = pl.BlockSpec((tr, cols), lambda i: (i, 0))
    return _pcall(body, name=name, grid=(rows // tr,), in_specs=[spec] * (3 + ng), out_specs=[spec] * 4,
                  out_shape=[jax.ShapeDtypeStruct((rows, cols), F32)] * 4, vmem_mb=48)(w, m, v, *grads)


def _ada_adamw(ct, dmod, w, m, v):
    rows, cols = w.shape
    tr = _row_tile(rows, cols)

    def body(ct_ref, dm_ref, w_ref, m_ref, v_ref, g_ref, d_ref, m2_ref, v2_ref):
        cv = ct_ref[...]
        ca = cv * _sigmoid(cv)
        g = ca[:, 0:1] * dm_ref[0:1, :]
        for b in range(1, N_DEV):
            g = g + ca[:, b:b + 1] * dm_ref[b:b + 1, :]
        delta, m2, v2 = _adamw_math(w_ref[...], g, m_ref[...], v_ref[...])
        g_ref[...] = g
        d_ref[...] = delta
        m2_ref[...] = m2
        v2_ref[...] = v2

    spec = pl.BlockSpec((tr, cols), lambda i: (i, 0))
    return _pcall(body, name="ada_adamw", grid=(rows // tr,),
                  in_specs=[pl.BlockSpec((tr, N_DEV), lambda i: (i, 0)), pl.BlockSpec((N_DEV, cols), lambda i: (0, 0)),
                            spec, spec, spec],
                  out_specs=[spec] * 4, out_shape=[jax.ShapeDtypeStruct((rows, cols), F32)] * 4,
                  vmem_mb=48)(ct, dmod, w, m, v)


def _mod_fwd(c_all, w, b):
    cols = w.shape[1]
    tn = cols // 3

    def body(c_ref, w_ref, b_ref, o_ref):
        cv = c_ref[...]
        ca = (cv * _sigmoid(cv)).astype(BF)
        o_ref[...] = _dot(ca, w_ref[...].astype(BF)) + b_ref[...]

    return _pcall(body, name="mod_fwd", grid=(3,),
                  in_specs=[pl.BlockSpec((N_DEV, D), lambda j: (0, 0)), pl.BlockSpec((D, tn), lambda j: (0, j)),
                            pl.BlockSpec((1, tn), lambda j: (0, j))],
                  out_specs=pl.BlockSpec((N_DEV, tn), lambda j: (0, j)),
                  out_shape=jax.ShapeDtypeStruct((N_DEV, cols), F32))(c_all, w, b)


def _resident(shape):
    zeros = (0,) * len(shape)
    return pl.BlockSpec(shape, lambda *_: zeros, pipeline_mode=pl.Buffered(1))


def _mixer_fwd(x, g, scale, shift, gate1, w_in4, cw, cb, wa, ba, wx, bx, lam, lg, lb, ws, bst, wba, wbb, wo,
               tm=256, chunk=256, piece=512):
    T = x.shape[0]
    tm = min(tm, T)
    ns = w_in4.shape[2]
    per = ns // piece

    def body(x_ref, g_ref, sc_ref, sh_ref, g1_ref, w_ref, cw_ref, cb_ref, wa_ref, ba_ref, wx_ref, bx_ref, lam_ref,
             lg_ref, lb_ref, ws_ref, bst_ref, wba_ref, wbb_ref, wo_ref,
             h1_ref, z_ref, hl_ref, yap_ref, ybp_ref, mg_ref, ya_ref, yb_ref, o_ref, x2_ref,
             xc_ref, r_ref, ig_ref, mu_ref, a_ref, prev, hc):
        i = pl.program_id(0)

        @pl.when(i == 0)
        def _():
            prev[...] = jnp.zeros_like(prev)
            hc[...] = jnp.zeros_like(hc)

        xv = x_ref[...]
        _, xh = _rms_stats(xv)
        h = ((xh * g_ref[...]) * (1.0 + sc_ref[...]) + sh_ref[...]).astype(BF)
        h1_ref[...] = h

        def proj(col, width):
            for c0 in range(col, col + width, piece):
                w = min(piece, col + width - c0)
                j, off = c0 // ns, c0 % ns
                z_ref[:, c0:c0 + w] = _dot(h, w_ref[j, :, off:off + w])

        def lru_chunk(c0):
            cs = slice(c0, c0 + chunk)
            xr = z_ref[:, cs]
            pv = prev[:, cs]
            xc = (cb_ref[:, cs] + cw_ref[3:4, cs] * xr + cw_ref[2:3, cs] * _shift_down(xr, pv, 1)
                  + cw_ref[1:2, cs] * _shift_down(xr, pv, 2) + cw_ref[0:1, cs] * _shift_down(xr, pv, 3))
            prev[:, cs] = xr[tm - SUBLANES:tm]
            r, ig, _, a, mult = _lru_gates(xc, wa_ref, ba_ref[:, cs], wx_ref, bx_ref[:, cs], lam_ref[:, cs],
                                           head0=c0 // HD)
            xc_ref[:, cs] = xc.astype(BF)
            r_ref[:, cs] = r.astype(BF)
            ig_ref[:, cs] = ig.astype(BF)
            a_ref[:, cs] = a
            mu_ref[:, cs] = mult.astype(BF)
            a, u = _scan_rows(a, mult * (ig * xc), reverse=False)
            hv = u + a * hc[SUBLANES - 1:SUBLANES, cs]
            hc[:, cs] = hv[tm - SUBLANES:tm]
            hl_ref[:, cs] = hv
            yap_ref[:, cs] = (hv * _gelu(z_ref[:, D + c0:D + c0 + chunk])).astype(BF)

        proj(0, chunk)
        proj(D, chunk)
        for c0 in range(0, D, chunk):
            if c0 + chunk < D:
                proj(c0 + chunk, chunk)
                proj(D + c0 + chunk, chunk)
            else:
                proj(2 * D, 2 * D)
            lru_chunk(c0)
        proj(4 * D, 2 * D)
        _, xhn = _layernorm_stats(_gelu(z_ref[:, 3 * D:4 * D]))
        vln = xhn * lg_ref[...] + lb_ref[...]
        _, mixed = _sgu_mix(vln, ws_ref, bst_ref, tm)
        ybp = (_gelu(z_ref[:, 2 * D:3 * D]) * mixed).astype(BF)
        ybp_ref[...] = ybp
        ya = _dot(yap_ref[...], wba_ref[...])
        yb = _dot(ybp, wbb_ref[...])
        merged = (_sigmoid_t(z_ref[:, 4 * D:5 * D]) * ya + _sigmoid_t(z_ref[:, 5 * D:6 * D]) * yb).astype(BF)
        o = _dot(merged, wo_ref[...])
        x2_ref[...] = xv + g1_ref[...] * o
        mg_ref[...] = merged
        ya_ref[...] = ya.astype(BF)
        yb_ref[...] = yb.astype(BF)
        o_ref[...] = o.astype(BF)

    row = pl.BlockSpec((tm, D), lambda i: (i, 0))
    vec = pl.BlockSpec((1, D), lambda i: (0, 0))
    bf_row = jax.ShapeDtypeStruct((T, D), BF)
    f32_row = jax.ShapeDtypeStruct((T, D), F32)
    return _pcall(body, name="mixer_fwd", grid=(T // tm,),
                  in_specs=[row, vec, vec, vec, vec, _resident(w_in4.shape), _resident(cw.shape), vec,
                            _resident(wa.shape), vec, _resident(wx.shape), vec, vec, vec, vec,
                            _resident(ws.shape), _resident(bst.shape),
                            _resident(wba.shape), _resident(wbb.shape), _resident(wo.shape)],
                  out_specs=[row, pl.BlockSpec((tm, 6 * D), lambda i: (i, 0))] + [row] * 13,
                  out_shape=[bf_row, jax.ShapeDtypeStruct((T, 6 * D), F32), f32_row, bf_row, bf_row, bf_row, bf_row,
                             bf_row, bf_row, f32_row, bf_row, bf_row, bf_row, bf_row, f32_row],
                  scratch=[pltpu.VMEM((SUBLANES, D), F32), pltpu.VMEM((SUBLANES, D), F32)], vmem_mb=60)(
        x, g, scale, shift, gate1, w_in4, cw, cb, wa, ba, wx, bx, lam, lg, lb, ws, bst, wba, wbb, wo)


def _ffn_fwd(x2, g, scale, shift, gate2, gf, w_up4, wd, cw, cb, target, tm=256, chunk=768):
    T = x2.shape[0]
    tm = min(tm, T)
    ns = w_up4.shape[2]
    dff = wd.shape[0]
    nchunk = dff // chunk
    per = ns // chunk

    def body(x2_ref, g_ref, sc_ref, sh_ref, g2_ref, gf_ref, wu_ref, wd_ref, cw_ref, cb_ref, t_ref,
             h2_ref, up_ref, f_ref, ga_ref, vd_ref, loss_ref, dx3_ref, dfo_ref, dgf_ref, dg2_ref, prev):
        i = pl.program_id(0)

        @pl.when(i == 0)
        def _():
            prev[...] = jnp.zeros_like(prev)
            loss_ref[...] = jnp.zeros_like(loss_ref)
            dgf_ref[...] = jnp.zeros_like(dgf_ref)
            dg2_ref[...] = jnp.zeros_like(dg2_ref)

        x2v = x2_ref[...]
        _, xh2 = _rms_stats(x2v)
        h2 = ((xh2 * g_ref[...]) * (1.0 + sc_ref[...]) + sh_ref[...]).astype(BF)
        h2_ref[...] = h2

        def conv(u, col):
            cs = slice(col, col + chunk)
            p = prev[:, cs]
            hid = (cb_ref[:, cs] + cw_ref[2:3, cs] * u + cw_ref[1:2, cs] * _shift_down(u, p, 1)
                   + cw_ref[0:1, cs] * _shift_down(u, p, 2))
            prev[:, cs] = u[tm - SUBLANES:tm]
            up_ref[:, cs] = u.astype(BF)
            return hid

        def up_proj(k):
            off = (k % per) * chunk
            return (_dot(h2, wu_ref[k // per, :, off:off + chunk]),
                    _dot(h2, wu_ref[N_CHIPS // 2 + k // per, :, off:off + chunk]))

        fo = None
        nxt = up_proj(0)
        for k in range(nchunk):
            col = k * chunk
            ua, uv = nxt
            if k + 1 < nchunk:
                nxt = up_proj(k + 1)
            act = conv(ua, col)
            val = conv(uv, dff + col)
            ga, dga = _gelu_and_grad(act)
            fk = (ga * val).astype(BF)
            f_ref[:, col:col + chunk] = fk
            ga_ref[:, col:col + chunk] = ga.astype(BF)
            vd_ref[:, col:col + chunk] = (val * dga).astype(BF)
            part = _dot(fk, wd_ref[col:col + chunk, :])
            fo = part if fo is None else fo + part

        x3 = x2v + g2_ref[...] * fo
        rstd, xh = _rms_stats(x3)
        err = xh * gf_ref[...] - t_ref[...]
        loss_ref[...] += 0.5 * jnp.sum(jnp.mean(err * err, axis=-1, keepdims=True), axis=0, keepdims=True)
        dy = err * (1.0 / D)
        dgf_ref[...] += _colsum(dy * xh)
        dxh = dy * gf_ref[...]
        dx3 = rstd * (dxh - xh * jnp.mean(dxh * xh, axis=-1, keepdims=True))
        dg2_ref[...] += _colsum(dx3 * fo)
        dx3_ref[...] = dx3
        dfo_ref[...] = (g2_ref[...] * dx3).astype(BF)

    row = pl.BlockSpec((tm, D), lambda i: (i, 0))
    vec = pl.BlockSpec((1, D), lambda i: (0, 0))
    wide = pl.BlockSpec((tm, 2 * dff), lambda i: (i, 0))
    half = pl.BlockSpec((tm, dff), lambda i: (i, 0))
    return _pcall(body, name="ffn_fwd", grid=(T // tm,),
                  in_specs=[row, vec, vec, vec, vec, vec, _resident(w_up4.shape), _resident(wd.shape),
                            _resident(cw.shape), _resident(cb.shape), row],
                  out_specs=[row, wide, half, half, half, pl.BlockSpec((1, LANES), lambda i: (0, 0)), row, row, vec, vec],
                  out_shape=[jax.ShapeDtypeStruct((T, D), BF), jax.ShapeDtypeStruct((T, 2 * dff), BF),
                             jax.ShapeDtypeStruct((T, dff), BF), jax.ShapeDtypeStruct((T, dff), BF),
                             jax.ShapeDtypeStruct((T, dff), BF), jax.ShapeDtypeStruct((1, LANES), F32),
                             jax.ShapeDtypeStruct((T, D), F32), jax.ShapeDtypeStruct((T, D), BF),
                             jax.ShapeDtypeStruct((1, D), F32), jax.ShapeDtypeStruct((1, D), F32)],
                  scratch=[pltpu.VMEM((SUBLANES, 2 * dff), F32)], vmem_mb=56)(
        x2, g, scale, shift, gate2, gf, w_up4, wd, cw, cb, target)


def _ffn_bwd(dfo, wd, up, ga, vd, cw, w_up4, x2, resid, g, scale, gate, o, tm=256, chunk=1536):
    T = up.shape[0]
    tm = min(tm, T)
    dff = wd.shape[0]
    ns = w_up4.shape[2]
    nchunk = dff // chunk
    per = ns // chunk
    nrow = T // tm

    def body(dfo_ref, wd_ref, up_ref, ga_ref, vd_ref, cw_ref, wu_ref, x_ref, r_ref, g_ref, sc_ref, gt_ref, o_ref,
             du_ref, dcw_ref, dcb_ref, dx_ref, dsh_ref, dsc_ref, dg_ref, do_ref, dgt_ref, nxt):
        i = pl.program_id(0)

        @pl.when(i == 0)
        def _():
            nxt[...] = jnp.zeros_like(nxt)
            for ref in (dcw_ref, dcb_ref, dsh_ref, dsc_ref, dg_ref, dgt_ref):
                ref[...] = jnp.zeros_like(ref)

        dfo_t = dfo_ref[...]

        def conv_bwd(dh, col):
            cs = slice(col, col + chunk)
            n8 = nxt[:, cs]
            dh1 = _shift_up(dh, n8, 1)
            dh2 = _shift_up(dh, n8, 2)
            nxt[:, cs] = dh[0:SUBLANES]
            du = (cw_ref[2:3, cs] * dh + cw_ref[1:2, cs] * dh1 + cw_ref[0:1, cs] * dh2).astype(BF)
            du_ref[:, cs] = du
            u = up_ref[:, cs].astype(F32)
            dcw_ref[2:3, cs] += _colsum(dh * u)
            dcw_ref[1:2, cs] += _colsum(dh1 * u)
            dcw_ref[0:1, cs] += _colsum(dh2 * u)
            dcb_ref[:, cs] += _colsum(dh)
            return du

        def down_bwd(k):
            return _dot_nt(dfo_t, wd_ref[k * chunk:(k + 1) * chunk, :])

        dh = None
        df_next = down_bwd(0)
        for k in range(nchunk):
            col = k * chunk
            off = (k % per) * chunk
            df = df_next
            if k + 1 < nchunk:
                df_next = down_bwd(k + 1)
            du_a = conv_bwd(df * vd_ref[:, col:col + chunk].astype(F32), col)
            du_v = conv_bwd(df * ga_ref[:, col:col + chunk].astype(F32), dff + col)
            part = (_dot_nt(du_a, wu_ref[k // per, :, off:off + chunk])
                    + _dot_nt(du_v, wu_ref[N_CHIPS // 2 + k // per, :, off:off + chunk]))
            dh = part if dh is None else dh + part

        rstd, xh = _rms_stats(x_ref[...])
        dsh_ref[...] += _colsum(dh)
        dsc_ref[...] += _colsum(dh * (xh * g_ref[...]))
        dn = dh * (1.0 + sc_ref[...])
        dg_ref[...] += _colsum(dn * xh)
        dxh = dn * g_ref[...]
        dx = r_ref[...] + rstd * (dxh - xh * jnp.mean(dxh * xh, axis=-1, keepdims=True))
        dx_ref[...] = dx
        do_ref[...] = (gt_ref[...] * dx).astype(BF)
        dgt_ref[...] += _colsum(dx * o_ref[...].astype(F32))

    rev = lambda i: (nrow - 1 - i, 0)
    row = pl.BlockSpec((tm, D), rev)
    vec = pl.BlockSpec((1, D), lambda i: (0, 0))
    wide = pl.BlockSpec((tm, 2 * dff), rev)
    half = pl.BlockSpec((tm, dff), rev)
    cw3 = pl.BlockSpec((3, 2 * dff), lambda i: (0, 0))
    cb1 = pl.BlockSpec((1, 2 * dff), lambda i: (0, 0))
    vshape = jax.ShapeDtypeStruct((1, D), F32)
    return _pcall(body, name="ffn_bwd", grid=(nrow,),
                  in_specs=[row, _resident(wd.shape), wide, half, half, _resident(cw.shape), _resident(w_up4.shape),
                            row, row, vec, vec, vec, row],
                  out_specs=[wide, cw3, cb1, row, vec, vec, vec, row, vec],
                  out_shape=[jax.ShapeDtypeStruct((T, 2 * dff), BF), jax.ShapeDtypeStruct((3, 2 * dff), F32),
                             jax.ShapeDtypeStruct((1, 2 * dff), F32), jax.ShapeDtypeStruct((T, D), F32),
                             vshape, vshape, vshape, jax.ShapeDtypeStruct((T, D), BF), vshape],
                  scratch=[pltpu.VMEM((SUBLANES, 2 * dff), F32)], vmem_mb=60)(
        dfo, wd, up, ga, vd, cw, w_up4, x2, resid, g, scale, gate, o)


def _mm_tn_cols(a, b, name, nshard, nb, mb=None, tm=2048):
    T, M = a.shape
    tm = min(tm, T)
    mb = M if mb is None else mb
    ns = b.shape[1] // nshard
    per = ns // nb
    nk = T // tm

    def body(a_ref, b_ref, o_ref, c_ref):
        k = pl.program_id(2)

        @pl.when(k == 0)
        def _():
            o_ref[...] = jnp.zeros_like(o_ref)

        o_ref[0] += _dot_tn(a_ref[...], b_ref[...])

        @pl.when(k == nk - 1)
        def _():
            c_ref[...] = o_ref[...].astype(BF)

    out_spec = pl.BlockSpec((1, mb, nb), lambda m, t, k: (t // per, m, t % per))
    return _pcall(body, name=name, grid=(M // mb, nshard * per, nk),
                  in_specs=[pl.BlockSpec((tm, mb), lambda m, t, k: (k, m)),
                            pl.BlockSpec((tm, nb), lambda m, t, k: (k, t))],
                  out_specs=[out_spec, out_spec],
                  out_shape=[jax.ShapeDtypeStruct((nshard, M, ns), F32), jax.ShapeDtypeStruct((nshard, M, ns), BF)],
                  vmem_mb=48)(a, b)


def _mm_nt_normbwd(dz, w4, x, resid, g, scale, name, tm=256):
    T = x.shape[0]
    tm = min(tm, T)
    ns = w4.shape[2]

    def body(dz_ref, w_ref, x_ref, r_ref, g_ref, sc_ref, dx_ref, dsh_ref, dsc_ref, dg_ref):
        i = pl.program_id(0)

        @pl.when(i == 0)
        def _():
            dsh_ref[...] = jnp.zeros_like(dsh_ref)
            dsc_ref[...] = jnp.zeros_like(dsc_ref)
            dg_ref[...] = jnp.zeros_like(dg_ref)

        dh = None
        for j in range(N_CHIPS):
            part = _dot_nt(dz_ref[:, j * ns:(j + 1) * ns], w_ref[j])
            dh = part if dh is None else dh + part
        rstd, xh = _rms_stats(x_ref[...])
        dsh_ref[...] += _colsum(dh)
        dsc_ref[...] += _colsum(dh * (xh * g_ref[...]))
        dn = dh * (1.0 + sc_ref[...])
        dg_ref[...] += _colsum(dn * xh)
        dxh = dn * g_ref[...]
        dx_ref[...] = r_ref[...] + rstd * (dxh - xh * jnp.mean(dxh * xh, axis=-1, keepdims=True))

    row = pl.BlockSpec((tm, D), lambda i: (i, 0))
    vec = pl.BlockSpec((1, D), lambda i: (0, 0))
    return _pcall(body, name=name, grid=(T // tm,),
                  in_specs=[pl.BlockSpec((tm, N_CHIPS * ns), lambda i: (i, 0)), _resident(w4.shape), row, row, vec, vec],
                  out_specs=[row, vec, vec, vec],
                  out_shape=[jax.ShapeDtypeStruct((T, D), F32)] + [jax.ShapeDtypeStruct((1, D), F32)] * 3,
                  vmem_mb=48)(dz, w4, x, resid, g, scale)


def _mix_bwd(do, ya, yb, z, wo, wba, wbb, tm=256):
    T = do.shape[0]
    tm = min(tm, T)

    def body(do_ref, ya_ref, yb_ref, ga_ref, gb_ref, wo_ref, wa_ref, wb_ref,
             dz_ref, dya_ref, dyb_ref, dyap_ref, dybp_ref):
        dm = _dot_nt(do_ref[...], wo_ref[...])
        sa = _sigmoid_t(ga_ref[...])
        sb = _sigmoid_t(gb_ref[...])
        dya = (sa * dm).astype(BF)
        dyb = (sb * dm).astype(BF)
        dz_ref[:, 0:D] = (dm * ya_ref[...].astype(F32) * sa * (1.0 - sa)).astype(BF)
        dz_ref[:, D:2 * D] = (dm * yb_ref[...].astype(F32) * sb * (1.0 - sb)).astype(BF)
        dya_ref[...] = dya
        dyb_ref[...] = dyb
        dyap_ref[...] = _dot_nt(dya, wa_ref[...]).astype(BF)
        dybp_ref[...] = _dot_nt(dyb, wb_ref[...]).astype(BF)

    row = pl.BlockSpec((tm, D), lambda i: (i, 0))
    wspec = pl.BlockSpec((D, D), lambda i: (0, 0))
    return _pcall(body, name="mix_bwd", grid=(T // tm,),
                  in_specs=[row, row, row, pl.BlockSpec((tm, D), lambda i: (i, 4)),
                            pl.BlockSpec((tm, D), lambda i: (i, 5)), wspec, wspec, wspec],
                  out_specs=[pl.BlockSpec((tm, 2 * D), lambda i: (i, 2)), row, row, row, row],
                  out_shape=[jax.ShapeDtypeStruct((T, 6 * D), BF)] + [jax.ShapeDtypeStruct((T, D), BF)] * 4,
                  vmem_mb=48)(do, ya, yb, z, z, wo, wba, wbb)


def _sgu_bwd(dz, dyb_pre, z, lg, lb, ws, bst, tb=256):
    T = z.shape[0]
    tb = min(tb, T)

    def body(dz_in, dy_ref, zu_ref, zv_ref, lg_ref, lb_ref, ws_ref, bst_ref,
             dz_ref, dws_ref, dbst_ref, dlg_ref, dlb_ref):
        del dz_in
        i = pl.program_id(0)

        @pl.when(i == 0)
        def _():
            dws_ref[...] = jnp.zeros_like(dws_ref)
            dbst_ref[...] = jnp.zeros_like(dbst_ref)
            dlg_ref[...] = jnp.zeros_like(dlg_ref)
            dlb_ref[...] = jnp.zeros_like(dlb_ref)

        gu, dgu = _gelu_and_grad(zu_ref[...])
        gv, dgv = _gelu_and_grad(zv_ref[...])
        rstd, xh = _layernorm_stats(gv)
        vln = xh * lg_ref[...] + lb_ref[...]
        wm, mixed = _sgu_mix(vln, ws_ref, bst_ref, tb)
        dy = dy_ref[...].astype(F32)
        dz_ref[:, 0:D] = (dy * mixed * dgu).astype(BF)
        dmixed = dy * gu
        ri = lax.broadcasted_iota(jnp.int32, (SGU_BLOCK, SGU_BLOCK), 0)
        ci = lax.broadcasted_iota(jnp.int32, (SGU_BLOCK, SGU_BLOCK), 1)
        blocks = []
        for blk in range(tb // SGU_BLOCK):
            rs = slice(blk * SGU_BLOCK, (blk + 1) * SGU_BLOCK)
            cols = []
            for g in range(HEADS):
                cs = slice(g * HD, (g + 1) * HD)
                dmg = dmixed[rs, cs]
                dmb = dmg.astype(BF)
                dbst_ref[:, g:g + 1] += jnp.sum(dmg, axis=1, keepdims=True)
                dws_ref[g] += jnp.where(ri >= ci, _dot_nt(dmb, vln[rs, cs].astype(BF)), 0.0)
                cols.append(_dot_tn(wm[g], dmb))
            blocks.append(jnp.concatenate(cols, axis=1))
        dvln = blocks[0] if len(blocks) == 1 else jnp.concatenate(blocks, axis=0)
        dlg_ref[...] += _colsum(dvln * xh)
        dlb_ref[...] += _colsum(dvln)
        dxh = dvln * lg_ref[...]
        dgv_in = rstd * (dxh - jnp.mean(dxh, axis=-1, keepdims=True)
                         - xh * jnp.mean(dxh * xh, axis=-1, keepdims=True))
        dz_ref[:, D:2 * D] = (dgv_in * dgv).astype(BF)

    row = pl.BlockSpec((tb, D), lambda i: (i, 0))
    vec = pl.BlockSpec((1, D), lambda i: (0, 0))
    wspec = pl.BlockSpec((HEADS, SGU_BLOCK, SGU_BLOCK), lambda i: (0, 0, 0))
    bspec = pl.BlockSpec((SGU_BLOCK, HEADS), lambda i: (0, 0))
    return _pcall(body, name="sgu_bwd", grid=(T // tb,),
                  in_specs=[HBM_SPEC, row, pl.BlockSpec((tb, D), lambda i: (i, 2)),
                            pl.BlockSpec((tb, D), lambda i: (i, 3)), vec, vec, wspec, bspec],
                  out_specs=[pl.BlockSpec((tb, 2 * D), lambda i: (i, 1)), wspec, bspec, vec, vec],
                  out_shape=[jax.ShapeDtypeStruct(dz.shape, BF),
                             jax.ShapeDtypeStruct((HEADS, SGU_BLOCK, SGU_BLOCK), F32),
                             jax.ShapeDtypeStruct((SGU_BLOCK, HEADS), F32),
                             jax.ShapeDtypeStruct((1, D), F32), jax.ShapeDtypeStruct((1, D), F32)],
                  aliases={0: 0}, vmem_mb=48)(dz, dyb_pre, z, z, lg, lb, ws, bst)


def _rglru_bwd(dz, dya_pre, z, h, xc_s, r_s, ig_s, mult_s, a_s, cw, wa, wx, lam, tb=256):
    T = z.shape[0]
    tb = min(tb, T)
    nrow = T // tb
    per = tb // SUBLANES

    def body(dz_in, dy_ref, xr_ref, gr_ref, h_ref, hh_ref, xc_ref, r_ref, ig_ref, mu_ref, a_ref, cw_ref, wa_ref,
             wx_ref, lam_ref, dz_ref, dcw_ref, dcb_ref, dwa_ref, dba_ref, dwx_ref, dbx_ref, dlam_ref, carry, nxt):
        del dz_in
        i = pl.program_id(0)
        first_block = i == nrow - 1

        @pl.when(i == 0)
        def _():
            carry[...] = jnp.zeros_like(carry)
            nxt[...] = jnp.zeros_like(nxt)
            for ref in (dcw_ref, dcb_ref, dwa_ref, dba_ref, dwx_ref, dbx_ref, dlam_ref):
                ref[...] = jnp.zeros_like(ref)

        xc = xc_ref[...].astype(F32)
        r = r_ref[...].astype(F32)
        ig = ig_ref[...].astype(F32)
        mult = mu_ref[...].astype(F32)
        a = a_ref[...]
        lam = lam_ref[...]
        ls = _log_sigmoid(lam)
        hv = h_ref[...]
        hprev = _shift_down(hv, jnp.where(first_block, 0.0, hh_ref[...]), 1)
        gg, dgg = _gelu_and_grad(gr_ref[...])
        dy = dy_ref[...].astype(F32)
        dz_ref[:, D:2 * D] = (dy * hv * dgg).astype(BF)

        rows = lax.broadcasted_iota(jnp.int32, (tb, D), 0)
        v = dy * gg + jnp.where(rows == tb - 1, carry[0:1, :], 0.0)
        q = jnp.where(rows < tb - 1, pltpu.roll(a, tb - 1, 0), 0.0)
        _, gsc = _scan_rows(q, v, reverse=True)
        carry[...] = (a * gsc)[0:SUBLANES]

        xi = ig * xc
        dmult = gsc * xi
        dxi = gsc * mult
        dig = dxi * xc
        dxc = dxi * ig
        dlog_a = gsc * hprev * a - dmult * (a * a) * pl.reciprocal(mult, approx=True)
        dlam_ref[...] += _colsum(dlog_a * r) * (LRU_C * _sigmoid(-lam))
        dpr = dlog_a * (LRU_C * ls) * r * (1.0 - r)
        dpi = dig * ig * (1.0 - ig)
        dba_ref[...] += _colsum(dpr)
        dbx_ref[...] += _colsum(dpi)
        back = []
        for hh in range(HEADS):
            cs = slice(hh * HD, (hh + 1) * HD)
            xh = xc[:, cs].astype(BF)
            dprh = dpr[:, cs].astype(BF)
            dpih = dpi[:, cs].astype(BF)
            dwa_ref[hh] += _dot_tn(xh, dprh)
            dwx_ref[hh] += _dot_tn(xh, dpih)
            back.append(_dot_nt(dprh, wa_ref[hh].astype(BF)) + _dot_nt(dpih, wx_ref[hh].astype(BF)))
        dxc = dxc + jnp.concatenate(back, axis=1)

        n8 = nxt[...]
        d1 = _shift_up(dxc, n8, 1)
        d2 = _shift_up(dxc, n8, 2)
        d3 = _shift_up(dxc, n8, 3)
        nxt[...] = dxc[0:SUBLANES]
        dz_ref[:, 0:D] = (cw_ref[3:4, :] * dxc + cw_ref[2:3, :] * d1 + cw_ref[1:2, :] * d2
                          + cw_ref[0:1, :] * d3).astype(BF)
        xr = xr_ref[...]
        dcw_ref[3:4, :] += _colsum(dxc * xr)
        dcw_ref[2:3, :] += _colsum(d1 * xr)
        dcw_ref[1:2, :] += _colsum(d2 * xr)
        dcw_ref[0:1, :] += _colsum(d3 * xr)
        dcb_ref[...] += _colsum(dxc)

    rev = lambda col: (lambda i: (nrow - 1 - i, col))
    row = pl.BlockSpec((tb, D), rev(0))
    halo = pl.BlockSpec((SUBLANES, D), lambda i: (jnp.maximum((nrow - 1 - i) * per - 1, 0), 0))
    vec = pl.BlockSpec((1, D), lambda i: (0, 0))
    wspec = pl.BlockSpec((HEADS, HD, HD), lambda i: (0, 0, 0))
    c4 = pl.BlockSpec((4, D), lambda i: (0, 0))
    wshape = jax.ShapeDtypeStruct((HEADS, HD, HD), F32)
    vshape = jax.ShapeDtypeStruct((1, D), F32)
    return _pcall(body, name="rglru_bwd", grid=(nrow,),
                  in_specs=[HBM_SPEC, row, row, pl.BlockSpec((tb, D), rev(1)), row, halo,
                            row, row, row, row, row, c4, wspec, wspec, vec],
                  out_specs=[pl.BlockSpec((tb, 2 * D), rev(0)), c4, vec, wspec, vec, wspec, vec, vec],
                  out_shape=[jax.ShapeDtypeStruct(dz.shape, BF), jax.ShapeDtypeStruct((4, D), F32), vshape,
                             wshape, vshape, wshape, vshape, vshape],
                  scratch=[pltpu.VMEM((SUBLANES, D), F32), pltpu.VMEM((SUBLANES, D), F32)],
                  aliases={0: 0}, vmem_mb=56)(dz, dya_pre, z, z, h, h, xc_s, r_s, ig_s, mult_s, a_s, cw, wa, wx, lam)


def _pack_rows(parts):
    out = []
    for p in parts:
        q = p.reshape(-1, LANES)
        pad = (-q.shape[0]) % SUBLANES
        if pad:
            q = jnp.concatenate([q, jnp.zeros((pad, LANES), q.dtype)], axis=0)
        out.append(q)
    return jnp.concatenate(out, axis=0)


def _rows_of(shape):
    n = 1
    for s in shape:
        n *= s
    rows = n // LANES
    return rows + (-rows) % SUBLANES


def kernel(x, c, w_ada, b_ada, norm_mix_g, w_in, rnn_conv_w, rnn_conv_b, lru_w_a, lru_b_a, lru_w_x, lru_b_x, lru_lambda, sgu_ln_g, sgu_ln_b, sgu_w_s, sgu_b_s, w_branch_a, w_branch_b, w_out, norm_ffn_g, w_up, ffn_conv_w, ffn_conv_b, w_down, norm_final_g, loss_target, m_w_ada, m_b_ada, m_norm_mix_g, m_w_in, m_rnn_conv_w, m_rnn_conv_b, m_lru_w_a, m_lru_b_a, m_lru_w_x, m_lru_b_x, m_lru_lambda, m_sgu_ln_g, m_sgu_ln_b, m_sgu_w_s, m_sgu_b_s, m_w_branch_a, m_w_branch_b, m_w_out, m_norm_ffn_g, m_w_up, m_ffn_conv_w, m_ffn_conv_b, m_w_down, m_norm_final_g, v_w_ada, v_b_ada, v_norm_mix_g, v_w_in, v_rnn_conv_w, v_rnn_conv_b, v_lru_w_a, v_lru_b_a, v_lru_w_x, v_lru_b_x, v_lru_lambda, v_sgu_ln_g, v_sgu_ln_b, v_sgu_w_s, v_sgu_b_s, v_w_branch_a, v_w_branch_b, v_w_out, v_norm_ffn_g, v_w_up, v_ffn_conv_w, v_ffn_conv_b, v_w_down, v_norm_final_g):
    args = dict(locals())
    T = x.shape[1]
    mx, my, mc = lax.axis_index("x"), lax.axis_index("y"), lax.axis_index("c")
    chip = 2 * mx + my
    dev = 2 * chip + mc
    vec = lambda a: a.reshape(1, -1)

    xt = x.reshape(T, D)
    tgt = loss_target.reshape(T, D)
    ns = w_in.shape[2]
    dff = w_down.shape[1] * N_CHIPS

    c_all = _gather8(c.reshape(SUBLANES, LANES), "gather_c").reshape(N_DEV, D)
    b_ada_sh = lax.dynamic_slice(b_ada, (0, chip * ns), (1, ns))
    mod_sh = _mod_fwd(c_all, w_ada[0], b_ada_sh)

    mixer_w = _cast_shards([w_in[0], w_branch_a[0], w_branch_b[0], w_out[0]], "cast_mixer_weights")
    w_in4, wba4, wbb4, wo4, rcw4, fcw4, mod4 = _gather_weights(
        list(mixer_w) + [rnn_conv_w[0], ffn_conv_w[0], mod_sh], [True] * 4 + [False] * 3)
    late = _cast_shards([w_up[0], w_down[0]], "cast_late", after=mod4)
    late_plan = _gather_plan(len(late))
    late_handle, late_token = _remote_start(
        late, [lax.empty((N_CHIPS,) + w.shape, w.dtype) for w in late], late_plan, 3 * len(late), "gather_late_start")
    rcw_full = jnp.transpose(rcw4, (1, 0, 2)).reshape(4, D)
    fcw_full = jnp.transpose(fcw4, (1, 0, 2)).reshape(3, 2 * dff)
    mod = lax.dynamic_index_in_dim(mod4, dev, axis=1, keepdims=False).reshape(1, 6 * D)
    shift1, scale1, gate1, shift2, scale2, gate2 = [mod[:, k * D:(k + 1) * D] for k in range(6)]

    bst = jnp.transpose(sgu_b_s[0])
    wba_full = wba4.reshape(D, D)
    wbb_full = wbb4.reshape(D, D)
    wo_full = wo4.reshape(D, D)
    h1, z, h_lru, ya_pre, yb_pre, merged, ya, yb, o1, x2, lru_xc, lru_r, lru_i, lru_mult, lru_a = _mixer_fwd(
        xt, norm_mix_g, scale1 + late_token[0:1, 0:1], shift1, gate1, w_in4, rcw_full, rnn_conv_b,
        lru_w_a[0], lru_b_a, lru_w_x[0], lru_b_x, lru_lambda, sgu_ln_g, sgu_ln_b, sgu_w_s[0], bst,
        wba_full, wbb_full, wo_full)
    late, late_lands = _remote_wait(late_handle, late_plan, o1, "gather_late_wait")
    w_up4, w_down4 = _place_own(late, late_lands)
    wd_full = w_down4.reshape(dff, D)
    h2, up, f, ffn_ga, ffn_vd, loss_part, dx3, dfo, dgf, dgate2 = _ffn_fwd(
        x2, norm_ffn_g, scale2, shift2, gate2, vec(norm_final_g), w_up4, wd_full, fcw_full, ffn_conv_b, tgt)

    dup, dfcw, dfcb, dx2, dshift2, dscale2, dg_ffn, do1, dgate1 = _ffn_bwd(
        dfo, wd_full, up, ffn_ga, ffn_vd, fcw_full, w_up4, x2, dx3, norm_ffn_g, scale2, gate1, o1)
    dwd = _mm_tn_cols(f, dfo, "dw_down", 1, D, mb=D)
    dw_up4 = _mm_tn_cols(h2, dup, "dw_up", N_CHIPS, ns)
    dz, dya, dyb, dya_pre, dyb_pre = _mix_bwd(do1, ya, yb, z, wo_full, wba_full, wbb_full)
    dwo = _mm_tn_cols(merged, do1, "dw_out", 1, D)
    dwba = _mm_tn_cols(ya_pre, dya, "dw_branch_a", 1, D)
    dwbb = _mm_tn_cols(yb_pre, dyb, "dw_branch_b", 1, D)

    chip_id = chip.astype(jnp.int32).reshape(1)

    def reduce_start(group, name):
        wire = [g16.reshape(N_CHIPS, -1, g16.shape[-1]) for _, (_, g16) in group]
        lands = [lax.empty((3,) + w.shape[1:], w.dtype) for w in wire]
        return _remote_start(wire, lands, _scatter_plan(len(group)), 3 * len(group), "scatter_start_" + name)

    def reduce_finish(group, handle, after, name):
        _, landed = _remote_wait(handle, _scatter_plan(len(group)), after, "scatter_wait_" + name)
        return [_sum_own_and_landed(chip_id, g32.reshape(N_CHIPS, -1, g32.shape[-1]), l, "sum_chips_" + n)
                for (n, (g32, _)), l in zip(group, landed)]

    group1 = [("w_up", dw_up4), ("w_down", dwd), ("w_branch_a", dwba), ("w_branch_b", dwbb), ("w_out", dwo)]
    handle1, token1 = reduce_start(group1, "late")
    dz, dws, dbst, dlg, dlb = _sgu_bwd(dz, dyb_pre, z, sgu_ln_g + token1[0:1, 0:1], sgu_ln_b, sgu_w_s[0], bst)
    dz, drcw, drcb, dwa, dba, dwx, dbx, dlam = _rglru_bwd(
        dz, dya_pre, z, h_lru, lru_xc, lru_r, lru_i, lru_mult, lru_a, rcw_full, lru_w_a[0], lru_w_x[0], lru_lambda)
    early_small = [("rnn_conv_b", drcb), ("lru_w_a", dwa), ("lru_b_a", dba), ("lru_w_x", dwx), ("lru_b_x", dbx),
                   ("lru_lambda", dlam), ("sgu_ln_g", dlg), ("sgu_ln_b", dlb), ("sgu_w_s", dws),
                   ("sgu_b_s", jnp.transpose(dbst)), ("norm_ffn_g", dg_ffn),
                   ("ffn_conv_b", dfcb), ("norm_final_g", dgf)]
    r_early = sum(_rows_of(args[n].shape) for n, _ in early_small)
    early_pack = _pack_rows([g for _, g in early_small] + [drcw, dfcw])
    early_pack = jnp.concatenate(
        [early_pack, jnp.zeros(((-early_pack.shape[0]) % 256, LANES), F32)], axis=0)
    early_chip = _add_pair(early_pack, _swap_cores([early_pack], "swap_small_grads")[0], "sum_cores_small_grads")
    early_handle, token3 = _remote_start([early_chip], [lax.empty((3,) + early_chip.shape, F32)], _bcast_plan, 3,
                                         "small_grads_start")
    def swap_start(totals, name):
        lands = [lax.empty(t.shape, t.dtype) for t in totals]
        return _remote_start(totals, lands, _sibling_plan(len(totals)), len(totals), "swap_sums_start_" + name)

    out = {}

    def swap_finish(group, handle, after, name):
        mine, theirs = _remote_wait(handle, _sibling_plan(len(group)), after, "swap_sums_wait_" + name)
        for (n, _), a, b in zip(group, mine, theirs):
            shape = args[n].shape
            res = _adamw(args[n][0], args["m_" + n][0], args["v_" + n][0], [a, b], "adamw_" + n)
            for kind, r in zip(("grad_", "delta_", "new_m_", "new_v_"), res):
                out[kind + n] = r.reshape(shape)
        return res[3]

    swap1, token4 = swap_start(reduce_finish(group1, handle1, drcb, "late"), "late")
    group2 = [("w_in", _mm_tn_cols(h1, dz, "dw_in", N_CHIPS, ns))]
    handle2, token2 = reduce_start(group2, "in")
    tokens = token2[0:1, 0:1] + token3[0:1, 0:1] + token4[0:1, 0:1]
    grad_x, dshift1, dscale1, dg_mix = _mm_nt_normbwd(
        dz, w_in4, xt, dx2, norm_mix_g + tokens, scale1, "dh1_norm_bwd")
    swap2, token5 = swap_start(reduce_finish(group2, handle2, dg_mix, "in"), "in")
    dmod = jnp.concatenate([dshift1, dscale1, dgate1, dshift2, dscale2, dgate2], axis=1)
    last = swap_finish(group1, swap1, token5, "late")
    swap_finish(group2, swap2, last, "in")

    late_small = [("b_ada", dmod), ("norm_mix_g", dg_mix)]
    small = late_small + early_small
    late_all = _gather8(_pack_rows([g for _, g in late_small] + [loss_part]), "gather_late_small_grads")
    late_sum = _sum_parts(late_all, "sum_late_small_grads")
    r_late = sum(_rows_of(args[n].shape) for n, _ in late_small)
    loss = late_sum[r_late, 0]
    late_sum = late_sum[:r_late]
    _, (early_landed,) = _remote_wait(early_handle, _bcast_plan, dg_mix, "small_grads_wait")
    early_sum = _sum_chips_in_order(chip_id, early_chip, early_landed, "sum_early_small_grads")
    r_small = sum(_rows_of(args[n].shape) for n, _ in small)
    r_pad = r_small + (-r_small) % 256
    fill = jnp.zeros((r_pad - r_small, LANES), F32)
    g_small = jnp.concatenate([late_sum, early_sum[:r_early], fill], axis=0)

    def pack_small(prefix):
        return jnp.concatenate([_pack_rows([args[prefix + n] for n, _ in small]), fill], axis=0)

    res = _adamw(pack_small(""), pack_small("m_"), pack_small("v_"), [g_small], "adamw_small")
    off = 0
    for n, _ in small:
        shape = args[n].shape
        rows = _rows_of(shape)
        for kind, r in zip(("grad_", "delta_", "new_m_", "new_v_"), res):
            out[kind + n] = r[off:off + rows].reshape(shape)
        off += rows

    rcw_cols = rnn_conv_w.shape[2]
    g_rcw = lax.dynamic_slice(early_sum[r_early:r_early + 32].reshape(4, D), (0, chip * rcw_cols), (4, rcw_cols))
    g_fcw = lax.dynamic_slice(early_sum[r_early + 32:r_early + 32 + 144].reshape(3, 2 * dff), (0, chip * ns), (3, ns))
    conv = [("rnn_conv_w", g_rcw), ("ffn_conv_w", g_fcw)]
    res = _adamw(_pack_rows([args[n] for n, _ in conv]), _pack_rows([args["m_" + n] for n, _ in conv]),
                 _pack_rows([args["v_" + n] for n, _ in conv]), [_pack_rows([g for _, g in conv])], "adamw_conv")
    off = 0
    for n, _ in conv:
        shape = args[n].shape
        cnt = shape[1] * shape[2] // LANES
        for kind, r in zip(("grad_", "delta_", "new_m_", "new_v_"), res):
            out[kind + n] = r[off:off + cnt].reshape(shape)
        off += _rows_of(shape)

    dmod_all = late_all[:, 0:6 * D // LANES, :].reshape(N_DEV, 6 * D)
    dmod_sh = lax.dynamic_slice(dmod_all, (0, chip * ns), (N_DEV, ns))
    res = _ada_adamw(jnp.transpose(c_all), dmod_sh, w_ada[0], m_w_ada[0], v_w_ada[0])
    for kind, r in zip(("grad_", "delta_", "new_m_", "new_v_"), res):
        out[kind + "w_ada"] = r.reshape(w_ada.shape)

    names = ["w_ada", "b_ada", "norm_mix_g", "w_in", "rnn_conv_w", "rnn_conv_b", "lru_w_a", "lru_b_a", "lru_w_x",
             "lru_b_x", "lru_lambda", "sgu_ln_g", "sgu_ln_b", "sgu_w_s", "sgu_b_s", "w_branch_a", "w_branch_b",
             "w_out", "norm_ffn_g", "w_up", "ffn_conv_w", "ffn_conv_b", "w_down", "norm_final_g"]
    result = [loss, grad_x.reshape(x.shape)]
    for kind in ("grad_", "delta_", "new_m_", "new_v_"):
        result += [out[kind + n] for n in names]
    return tuple(result)
```

```python
import jax
import jax.numpy as jnp
from jax import lax
from jax.experimental import pallas as pl
from jax.experimental.pallas import tpu as pltpu

F32 = jnp.float32
BF = jnp.bfloat16

D = 1024
HEADS = 8
HD = D // HEADS
SGU_BLOCK = 128
N_CHIPS = 4
N_DEV = 8
EPS = 1e-6
LRU_C = 8.0
LANES = 128
SUBLANES = 8
ELEMENTWISE_BLOCK_BYTES = 3 << 19
TN_ROWS = 2048
TN_ROWS_SQUARE = 4096

ADAM_LR = 0.001
ADAM_B1 = 0.9
ADAM_B2 = 0.999
ADAM_EPS = 1e-08
ADAM_WD = 0.01
ADAM_STEP = 10

GELU_K0 = 0.7978845608028654
GELU_K1 = 0.044715

HBM_SPEC = pl.BlockSpec(memory_space=pltpu.HBM)
MESH_ID = pl.DeviceIdType.MESH


def _pcall(body, *, name, out_shape, grid=(), in_specs=None, out_specs=None, scratch=(), vmem_mb=32, aliases=None,
           grid_spec=None):
    kw = {}
    if aliases:
        kw["input_output_aliases"] = aliases
    if grid_spec is not None:
        kw["grid_spec"] = grid_spec
        ndim = len(grid_spec.grid)
    else:
        kw.update(grid=grid, in_specs=in_specs, out_specs=out_specs, scratch_shapes=list(scratch))
        ndim = len(grid)
    if ndim:
        params = pltpu.CompilerParams(dimension_semantics=("arbitrary",) * ndim, vmem_limit_bytes=vmem_mb * 2 ** 20)
    else:
        params = pltpu.CompilerParams(vmem_limit_bytes=vmem_mb * 2 ** 20)
    return pl.pallas_call(body, name=name, out_shape=out_shape, compiler_params=params, **kw)


def _gelu_cdf(x, x2):
    return 0.5 * jnp.tanh(x * (GELU_K0 + (GELU_K0 * GELU_K1) * x2)) + 0.5


def _gelu(x):
    return x * _gelu_cdf(x, x * x)


def _gelu_and_grad(x):
    x2 = x * x
    s = _gelu_cdf(x, x2)
    g = x * s
    dg = s * (1.0 + (x - g) * ((2.0 * GELU_K0) + (6.0 * GELU_K0 * GELU_K1) * x2))
    return g, dg


def _sigmoid(x):
    return 1.0 / (1.0 + jnp.exp(-x))


def _sigmoid_t(x):
    return 0.5 * jnp.tanh(0.5 * x) + 0.5


def _log_sigmoid(x):
    e = jnp.exp(-jnp.abs(x))
    u = 1.0 + e
    d = u - 1.0
    l1p = jnp.where(d == 0.0, e, jnp.log(u) * (e / jnp.where(d == 0.0, 1.0, d)))
    return jnp.minimum(x, 0.0) - l1p


def _dot(a, b):
    return jnp.dot(a, b, preferred_element_type=F32)


def _dot_nt(a, b):
    return lax.dot_general(a, b, (((1,), (1,)), ((), ())), preferred_element_type=F32)


def _dot_tn(a, b):
    return lax.dot_general(a, b, (((0,), (0,)), ((), ())), preferred_element_type=F32)


def _shift_down(x, halo, s):
    r = pltpu.roll(x, s, 0)
    rows = lax.broadcasted_iota(jnp.int32, (SUBLANES, x.shape[1]), 0)
    head = jnp.where(rows < s, pltpu.roll(halo, s, 0), r[0:SUBLANES])
    return jnp.concatenate([head, r[SUBLANES:]], axis=0)


def _shift_up(x, halo, s):
    n = x.shape[0]
    r = pltpu.roll(x, n - s, 0)
    rows = lax.broadcasted_iota(jnp.int32, (SUBLANES, x.shape[1]), 0)
    tail = jnp.where(rows >= SUBLANES - s, pltpu.roll(halo, SUBLANES - s, 0), r[n - SUBLANES:n])
    return jnp.concatenate([r[:n - SUBLANES], tail], axis=0)


def _scan_rows(a, u, reverse):
    n, width = a.shape
    rows = lax.broadcasted_iota(jnp.int32, (n, width), 0)
    d = 1
    while d < n:
        if d < SUBLANES:
            keep = rows < n - d if reverse else rows >= d
            shift = n - d if reverse else d
            a_s = jnp.where(keep, pltpu.roll(a, shift, 0), 1.0)
            u_s = jnp.where(keep, pltpu.roll(u, shift, 0), 0.0)
        elif reverse:
            a_s = jnp.concatenate([a[d:], jnp.ones((d, width), a.dtype)], axis=0)
            u_s = jnp.concatenate([u[d:], jnp.zeros((d, width), u.dtype)], axis=0)
        else:
            a_s = jnp.concatenate([jnp.ones((d, width), a.dtype), a[:n - d]], axis=0)
            u_s = jnp.concatenate([jnp.zeros((d, width), u.dtype), u[:n - d]], axis=0)
        u = a * u_s + u
        a = a * a_s
        d *= 2
    return a, u


def _colsum(x):
    return jnp.sum(x, axis=0, keepdims=True)


def _rms_stats(x):
    r = lax.rsqrt(jnp.mean(x * x, axis=-1, keepdims=True) + EPS)
    return r, x * r


def _lru_gates(xc, wa_ref, ba, wx_ref, bx, lam, head0=0):
    pr, pi = [], []
    for hh in range(xc.shape[1] // HD):
        xh = xc[:, hh * HD:(hh + 1) * HD].astype(BF)
        pr.append(_dot(xh, wa_ref[head0 + hh].astype(BF)))
        pi.append(_dot(xh, wx_ref[head0 + hh].astype(BF)))
    r = _sigmoid_t((pr[0] if len(pr) == 1 else jnp.concatenate(pr, axis=1)) + ba)
    ig = _sigmoid_t((pi[0] if len(pi) == 1 else jnp.concatenate(pi, axis=1)) + bx)
    ls = _log_sigmoid(lam)
    log_a = LRU_C * r * ls
    a = jnp.exp(log_a)
    x2 = 2.0 * log_a
    u = a * a
    lu = jnp.log(jnp.maximum(u, 1e-37))
    ratio = x2 * pl.reciprocal(jnp.where(lu == 0.0, 1.0, lu), approx=True)
    em1 = jnp.where(lu == 0.0, x2, jnp.where(u < 1e-30, -1.0, (u - 1.0) * ratio))
    mult = jnp.sqrt(-em1)
    return r, ig, ls, a, mult


def _sgu_mix(vln, ws_ref, bst_ref, tb):
    ri = lax.broadcasted_iota(jnp.int32, (SGU_BLOCK, SGU_BLOCK), 0)
    ci = lax.broadcasted_iota(jnp.int32, (SGU_BLOCK, SGU_BLOCK), 1)
    wm = [jnp.where(ri >= ci, ws_ref[g], 0.0).astype(BF) for g in range(HEADS)]
    blocks = []
    for blk in range(tb // SGU_BLOCK):
        cols = []
        for g in range(HEADS):
            vb = vln[blk * SGU_BLOCK:(blk + 1) * SGU_BLOCK, g * HD:(g + 1) * HD].astype(BF)
            cols.append(_dot(wm[g], vb) + bst_ref[:, g:g + 1])
        blocks.append(jnp.concatenate(cols, axis=1))
    mixed = blocks[0] if len(blocks) == 1 else jnp.concatenate(blocks, axis=0)
    return wm, mixed


def _layernorm_stats(v):
    mu = jnp.mean(v, axis=-1, keepdims=True)
    vc = v - mu
    rstd = lax.rsqrt(jnp.mean(vc * vc, axis=-1, keepdims=True) + EPS)
    return rstd, vc * rstd


def _my_xyc():
    return lax.axis_index("x"), lax.axis_index("y"), lax.axis_index("c")


def _gather_weights(srcs, halve):
    n = len(srcs)
    out_shape = [jax.ShapeDtypeStruct((N_CHIPS,) + s.shape, s.dtype) for s in srcs]

    def body(*refs):
        src, out = refs[:n], refs[n:2 * n]
        send_sems, recv_sems, fwd_send, fwd_recv, loc_sems = refs[2 * n:]
        x, y, c = _my_xyc()
        me = 2 * x + y
        chips = [(1 - x, y), (x, 1 - y), (1 - x, 1 - y)]

        def half(ref, a, which):
            if not halve[a]:
                return ref
            h = srcs[a].shape[0] // 2
            return ref.at[pl.ds(which * h, h)]

        def ici(a, k, frm):
            px, py = chips[k]
            return pltpu.make_async_remote_copy(
                src_ref=half(src[a], a, c), dst_ref=half(out[a].at[frm], a, c),
                send_sem=send_sems.at[a, k], recv_sem=recv_sems.at[a, k],
                device_id=(px, py, c), device_id_type=MESH_ID)

        def d2d(a, k, which):
            px, py = chips[k]
            rows = half(out[a].at[2 * px + py], a, which)
            return pltpu.make_async_remote_copy(
                src_ref=rows, dst_ref=rows, send_sem=fwd_send.at[a, k], recv_sem=fwd_recv.at[a, k],
                device_id=(x, y, 1 - c), device_id_type=MESH_ID)

        local, sends = [], []
        for a in range(n):
            lc = pltpu.make_async_copy(src[a], out[a].at[me], loc_sems.at[a])
            lc.start()
            local.append(lc)
            for k in range(3):
                cp = ici(a, k, me)
                cp.start()
                sends.append(cp)
        for a in range(n):
            for k in range(3):
                px, py = chips[k]
                ici(a, k, 2 * px + py).wait_recv()
                if halve[a]:
                    fw = d2d(a, k, c)
                    fw.start()
                    sends.append(fw)
        for a in range(n):
            if halve[a]:
                for k in range(3):
                    d2d(a, k, 1 - c).wait_recv()
        for cp in sends:
            cp.wait_send()
        for lc in local:
            lc.wait()

    sem = pltpu.SemaphoreType.DMA((n, 3))
    return _pcall(body, name="gather_weights", out_shape=out_shape, in_specs=[HBM_SPEC] * n,
                  out_specs=[HBM_SPEC] * n, scratch=[sem, sem, sem, sem, pltpu.SemaphoreType.DMA((n,))])(*srcs)


SEM_SPEC = pl.BlockSpec(memory_space=pltpu.SEMAPHORE)


def _remote_start(srcs, lands, plan, ncopies, name):
    n, m = len(srcs), len(lands)

    def body(*refs):
        src, land = refs[:n], refs[n:n + m]
        send_sems, recv_sems = refs[n + m], refs[n + m + 1]
        token = refs[-1]
        x, y, c = _my_xyc()
        for i, (s, d, dev) in enumerate(plan(src, land, x, y, c)):
            pltpu.make_async_remote_copy(src_ref=s, dst_ref=d, send_sem=send_sems.at[i], recv_sem=recv_sems.at[i],
                                         device_id=dev, device_id_type=MESH_ID).start()
        token[...] = jnp.zeros_like(token)

    bufs = list(srcs) + list(lands)
    out = pl.pallas_call(
        body, name=name,
        out_shape=(pltpu.SemaphoreType.DMA((ncopies,)), pltpu.SemaphoreType.DMA((ncopies,)),
                   *[pltpu.HBM(b.shape, b.dtype) for b in bufs], jax.ShapeDtypeStruct((SUBLANES, LANES), F32)),
        in_specs=[HBM_SPEC] * (n + m),
        out_specs=(SEM_SPEC, SEM_SPEC, *[HBM_SPEC] * (n + m), pl.BlockSpec(memory_space=pltpu.VMEM)),
        input_output_aliases={i: 2 + i for i in range(n + m)},
        compiler_params=pltpu.CompilerParams(has_side_effects=pltpu.SideEffectType.DATAFLOW_SIDE_EFFECTING),
    )(*[pltpu.with_memory_space_constraint(b, pltpu.HBM) for b in bufs])
    return (out[0], out[1], out[2:2 + n], out[2 + n:2 + n + m]), out[-1]


def _remote_wait(handle, plan, after, name):
    send_sems, recv_sems, srcs, lands = handle
    n, m = len(srcs), len(lands)

    def body(*refs):
        src, land = refs[:n], refs[n:n + m]
        ssem, rsem = refs[n + m], refs[n + m + 1]
        x, y, c = _my_xyc()
        for i, (s, d, dev) in enumerate(plan(src, land, x, y, c)):
            cp = pltpu.make_async_remote_copy(src_ref=s, dst_ref=d, send_sem=ssem.at[i], recv_sem=rsem.at[i],
                                              device_id=dev, device_id_type=MESH_ID)
            cp.wait_send()
            cp.wait_recv()

    bufs = list(srcs) + list(lands)
    out = pl.pallas_call(
        body, name=name, out_shape=tuple(pltpu.HBM(b.shape, b.dtype) for b in bufs),
        in_specs=[HBM_SPEC] * (n + m) + [SEM_SPEC, SEM_SPEC, pl.BlockSpec(memory_space=pl.ANY)],
        out_specs=tuple([HBM_SPEC] * (n + m)), input_output_aliases={i: i for i in range(n + m)},
        compiler_params=pltpu.CompilerParams(has_side_effects=pltpu.SideEffectType.DATAFLOW_SIDE_EFFECTING),
    )(*bufs, send_sems, recv_sems, after)
    return out[:n], out[n:]


def _chips_of(x, y):
    return [(1 - x, y), (x, 1 - y), (1 - x, 1 - y)]


def _gather_plan(count):
    def plan(src, land, x, y, c):
        me = 2 * x + y
        return [(src[a], land[a].at[me], (px, py, c)) for a in range(count) for px, py in _chips_of(x, y)]

    return plan


def _place_own(srcs, lands):
    n = len(srcs)

    def body(*refs):
        src, land, sems = refs[:n], refs[2 * n:3 * n], refs[3 * n]
        me = 2 * lax.axis_index("x") + lax.axis_index("y")
        copies = [pltpu.make_async_copy(src[a], land[a].at[me], sems.at[a]) for a in range(n)]
        for cp in copies:
            cp.start()
        for cp in copies:
            cp.wait()

    return _pcall(body, name="place_own_shards", out_shape=[jax.ShapeDtypeStruct(l.shape, l.dtype) for l in lands],
                  in_specs=[HBM_SPEC] * (2 * n), out_specs=[HBM_SPEC] * n, aliases={n + a: a for a in range(n)},
                  scratch=[pltpu.SemaphoreType.DMA((n,))])(*srcs, *lands)


def _gather8(src, name):
    def body(src_ref, out_ref, send_sems, recv_sems, loc_sem):
        x, y, c = _my_xyc()
        me = 4 * x + 2 * y + c
        lc = pltpu.make_async_copy(src_ref, out_ref.at[me], loc_sem)
        lc.start()
        cps = []
        for k in range(1, N_DEV):
            px = 1 - x if (k >> 2) & 1 else x
            py = 1 - y if (k >> 1) & 1 else y
            pc = 1 - c if k & 1 else c
            cp = pltpu.make_async_remote_copy(
                src_ref=src_ref, dst_ref=out_ref.at[me], send_sem=send_sems.at[k - 1], recv_sem=recv_sems.at[k - 1],
                device_id=(px, py, pc), device_id_type=MESH_ID)
            cp.start()
            cps.append(cp)
        for cp in cps:
            cp.wait()
        lc.wait()

    return _pcall(body, name=name, out_shape=jax.ShapeDtypeStruct((N_DEV,) + src.shape, src.dtype),
                  in_specs=[HBM_SPEC], out_specs=HBM_SPEC,
                  scratch=[pltpu.SemaphoreType.DMA((N_DEV - 1,)), pltpu.SemaphoreType.DMA((N_DEV - 1,)),
                           pltpu.SemaphoreType.DMA])(src)


def _cast_shards(arrs, name, after=None):
    n = len(arrs)
    extra = [] if after is None else [after]

    def body(*refs):
        ins, outs = refs[:n], refs[n + len(extra):]
        for a in range(n):
            outs[a][...] = ins[a][...].astype(BF)

    specs = [pl.BlockSpec((s.shape[0] // 4, s.shape[1]), lambda i: (i, 0)) for s in arrs]
    return _pcall(body, name=name, grid=(4,), in_specs=specs + [pl.BlockSpec(memory_space=pl.ANY)] * len(extra),
                  out_specs=specs, out_shape=[jax.ShapeDtypeStruct(s.shape, BF) for s in arrs])(*arrs, *extra)


def _row_tile(rows, cols):
    t = rows
    while t * cols * 4 > ELEMENTWISE_BLOCK_BYTES and t % (2 * SUBLANES) == 0:
        t //= 2
    return t


def _sum_parts(parts, name):
    p, rows, cols = parts.shape
    tr = _row_tile(rows, cols * p // 2)

    def body(p_ref, o_ref):
        acc = p_ref[0].astype(F32)
        for k in range(1, p):
            acc = acc + p_ref[k].astype(F32)
        o_ref[...] = acc

    return _pcall(body, name=name, grid=(rows // tr,),
                  in_specs=[pl.BlockSpec((p, tr, cols), lambda i: (0, i, 0))],
                  out_specs=pl.BlockSpec((tr, cols), lambda i: (i, 0)),
                  out_shape=jax.ShapeDtypeStruct((rows, cols), F32), vmem_mb=48)(parts)


def _sum_own_and_landed(chip, sums, landed, name):
    _, rows, cols = sums.shape
    tr = _row_tile(rows, 2 * cols)

    def body(chip_ref, own_ref, land_ref, o_ref):
        del chip_ref
        acc = own_ref[0].astype(F32)
        for k in range(3):
            acc = acc + land_ref[k].astype(F32)
        o_ref[...] = acc

    grid_spec = pltpu.PrefetchScalarGridSpec(
        num_scalar_prefetch=1, grid=(rows // tr,),
        in_specs=[pl.BlockSpec((1, tr, cols), lambda i, chip_ref: (chip_ref[0], i, 0)),
                  pl.BlockSpec((3, tr, cols), lambda i, chip_ref: (0, i, 0))],
        out_specs=pl.BlockSpec((tr, cols), lambda i, chip_ref: (i, 0)))
    return _pcall(body, name=name, grid_spec=grid_spec, out_shape=jax.ShapeDtypeStruct((rows, cols), F32),
                  vmem_mb=48)(chip, sums, landed)


def _swap_cores(arrs, name):
    n = len(arrs)

    def body(*refs):
        src, out = refs[:n], refs[n:2 * n]
        send_sems, recv_sems = refs[2 * n:]
        x, y, c = _my_xyc()
        cps = []
        for a in range(n):
            cp = pltpu.make_async_remote_copy(
                src_ref=src[a], dst_ref=out[a], send_sem=send_sems.at[a], recv_sem=recv_sems.at[a],
                device_id=(x, y, 1 - c), device_id_type=MESH_ID)
            cp.start()
            cps.append(cp)
        for cp in cps:
            cp.wait()

    sem = pltpu.SemaphoreType.DMA((n,))
    return _pcall(body, name=name, out_shape=[jax.ShapeDtypeStruct(a.shape, a.dtype) for a in arrs],
                  in_specs=[HBM_SPEC] * n, out_specs=[HBM_SPEC] * n, scratch=[sem, sem])(*arrs)


def _add_pair(a, b, name):
    rows, cols = a.shape
    tr = _row_tile(rows, 2 * cols)

    def body(a_ref, b_ref, o_ref):
        o_ref[...] = a_ref[...] + b_ref[...]

    spec = pl.BlockSpec((tr, cols), lambda i: (i, 0))
    return _pcall(body, name=name, grid=(rows // tr,), in_specs=[spec, spec], out_specs=spec,
                  out_shape=jax.ShapeDtypeStruct((rows, cols), F32))(a, b)


def _sum_chips_in_order(chip, own, landed, name):
    rows, cols = own.shape
    tr = _row_tile(rows, 4 * cols)

    def body(chip_ref, own_ref, land_ref, o_ref):
        me = chip_ref[0]
        acc = None
        for p in range(N_CHIPS):
            q = p ^ me
            k = jnp.where(q == 2, 0, jnp.where(q == 1, 1, 2))
            term = jnp.where(q == 0, own_ref[...], land_ref[k])
            acc = term if acc is None else acc + term
        o_ref[...] = acc

    grid_spec = pltpu.PrefetchScalarGridSpec(
        num_scalar_prefetch=1, grid=(rows // tr,),
        in_specs=[pl.BlockSpec((tr, cols), lambda i, chip_ref: (i, 0)),
                  pl.BlockSpec((3, tr, cols), lambda i, chip_ref: (0, i, 0))],
        out_specs=pl.BlockSpec((tr, cols), lambda i, chip_ref: (i, 0)))
    return _pcall(body, name=name, grid_spec=grid_spec, out_shape=jax.ShapeDtypeStruct((rows, cols), F32))(
        chip, own, landed)


def _bcast_plan(src, land, x, y, c):
    return [(src[0], land[0].at[k], (px, py, c)) for k, (px, py) in enumerate(_chips_of(x, y))]


def _sibling_plan(count):
    def plan(src, land, x, y, c):
        return [(src[a], land[a], (x, y, 1 - c)) for a in range(count)]

    return plan


def _scatter_plan(count):
    def plan(src, land, x, y, c):
        out = []
        for a in range(count):
            for k, (px, py) in enumerate(_chips_of(x, y)):
                out.append((src[a].at[2 * px + py], land[a].at[k], (px, py, c)))
        return out

    return plan


def _adamw_math(w, g, m, v):
    m2 = ADAM_B1 * m + (1.0 - ADAM_B1) * g
    v2 = ADAM_B2 * v + (1.0 - ADAM_B2) * (g * g)
    m_hat = m2 / (1.0 - ADAM_B1 ** ADAM_STEP)
    v_hat = v2 / (1.0 - ADAM_B2 ** ADAM_STEP)
    delta = -ADAM_LR * (m_hat / (jnp.sqrt(v_hat) + ADAM_EPS) + ADAM_WD * w)
    return delta, m2, v2


def _adamw(w, m, v, grads, name):
    rows, cols = w.shape
    tr = _row_tile(rows, cols)
    ng = len(grads)

    def body(*refs):
        w_ref, m_ref, v_ref = refs[:3]
        g = refs[3][...]
        for k in range(1, ng):
            g = g + refs[3 + k][...]
        g_ref, d_ref, m2_ref, v2_ref = refs[3 + ng:]
        delta, m2, v2 = _adamw_math(w_ref[...], g, m_ref[...], v_ref[...])
        g_ref[...] = g
        d_ref[...] = delta
        m2_ref[...] = m2
        v2_ref[...] = v2

    spec = pl.BlockSpec((tr, cols), lambda i: (i, 0))
    return _pcall(body, name=name, grid=(rows // tr,), in_specs=[spec] * (3 + ng), out_specs=[spec] * 4,
                  out_shape=[jax.ShapeDtypeStruct((rows, cols), F32)] * 4, vmem_mb=48)(w, m, v, *grads)


def _ada_adamw(ct, dmod, w, m, v):
    rows, cols = w.shape
    tr = _row_tile(rows, cols)

    def body(ct_ref, dm_ref, w_ref, m_ref, v_ref, g_ref, d_ref, m2_ref, v2_ref):
        cv = ct_ref[...]
        ca = cv * _sigmoid(cv)
        g = ca[:, 0:1] * dm_ref[0:1, :]
        for b in range(1, N_DEV):
            g = g + ca[:, b:b + 1] * dm_ref[b:b + 1, :]
        delta, m2, v2 = _adamw_math(w_ref[...], g, m_ref[...], v_ref[...])
        g_ref[...] = g
        d_ref[...] = delta
        m2_ref[...] = m2
        v2_ref[...] = v2

    spec = pl.BlockSpec((tr, cols), lambda i: (i, 0))
    return _pcall(body, name="ada_adamw", grid=(rows // tr,),
                  in_specs=[pl.BlockSpec((tr, N_DEV), lambda i: (i, 0)), pl.BlockSpec((N_DEV, cols), lambda i: (0, 0)),
                            spec, spec, spec],
                  out_specs=[spec] * 4, out_shape=[jax.ShapeDtypeStruct((rows, cols), F32)] * 4,
                  vmem_mb=48)(ct, dmod, w, m, v)


def _mod_fwd(c_all, w, b):
    cols = w.shape[1]
    tn = cols // 3

    def body(c_ref, w_ref, b_ref, o_ref):
        cv = c_ref[...]
        ca = (cv * _sigmoid(cv)).astype(BF)
        o_ref[...] = _dot(ca, w_ref[...].astype(BF)) + b_ref[...]

    return _pcall(body, name="mod_fwd", grid=(3,),
                  in_specs=[pl.BlockSpec((N_DEV, D), lambda j: (0, 0)), pl.BlockSpec((D, tn), lambda j: (0, j)),
                            pl.BlockSpec((1, tn), lambda j: (0, j))],
                  out_specs=pl.BlockSpec((N_DEV, tn), lambda j: (0, j)),
                  out_shape=jax.ShapeDtypeStruct((N_DEV, cols), F32))(c_all, w, b)


def _resident(shape):
    zeros = (0,) * len(shape)
    return pl.BlockSpec(shape, lambda *_: zeros, pipeline_mode=pl.Buffered(1))


def _mixer_fwd(x, g, scale, shift, gate1, w_in4, cw, cb, wa, ba, wx, bx, lam, lg, lb, ws, bst, wba, wbb, wo,
               tm=256, chunk=256, piece=512):
    T = x.shape[0]
    tm = min(tm, T)
    ns = w_in4.shape[2]
    per = ns // piece

    def body(x_ref, g_ref, sc_ref, sh_ref, g1_ref, w_ref, cw_ref, cb_ref, wa_ref, ba_ref, wx_ref, bx_ref, lam_ref,
             lg_ref, lb_ref, ws_ref, bst_ref, wba_ref, wbb_ref, wo_ref,
             h1_ref, z_ref, hl_ref, yap_ref, ybp_ref, mg_ref, ya_ref, yb_ref, o_ref, x2_ref,
             xc_ref, r_ref, ig_ref, mu_ref, a_ref, prev, hc):
        i = pl.program_id(0)

        @pl.when(i == 0)
        def _():
            prev[...] = jnp.zeros_like(prev)
            hc[...] = jnp.zeros_like(hc)

        xv = x_ref[...]
        _, xh = _rms_stats(xv)
        h = ((xh * g_ref[...]) * (1.0 + sc_ref[...]) + sh_ref[...]).astype(BF)
        h1_ref[...] = h

        def proj(col, width):
            for c0 in range(col, col + width, piece):
                w = min(piece, col + width - c0)
                j, off = c0 // ns, c0 % ns
                z_ref[:, c0:c0 + w] = _dot(h, w_ref[j, :, off:off + w])

        def lru_chunk(c0):
            cs = slice(c0, c0 + chunk)
            xr = z_ref[:, cs]
            pv = prev[:, cs]
            xc = (cb_ref[:, cs] + cw_ref[3:4, cs] * xr + cw_ref[2:3, cs] * _shift_down(xr, pv, 1)
                  + cw_ref[1:2, cs] * _shift_down(xr, pv, 2) + cw_ref[0:1, cs] * _shift_down(xr, pv, 3))
            prev[:, cs] = xr[tm - SUBLANES:tm]
            r, ig, _, a, mult = _lru_gates(xc, wa_ref, ba_ref[:, cs], wx_ref, bx_ref[:, cs], lam_ref[:, cs],
                                           head0=c0 // HD)
            xc_ref[:, cs] = xc.astype(BF)
            r_ref[:, cs] = r.astype(BF)
            ig_ref[:, cs] = ig.astype(BF)
            a_ref[:, cs] = a
            mu_ref[:, cs] = mult.astype(BF)
            a, u = _scan_rows(a, mult * (ig * xc), reverse=False)
            hv = u + a * hc[SUBLANES - 1:SUBLANES, cs]
            hc[:, cs] = hv[tm - SUBLANES:tm]
            hl_ref[:, cs] = hv
            yap_ref[:, cs] = (hv * _gelu(z_ref[:, D + c0:D + c0 + chunk])).astype(BF)

        proj(0, chunk)
        proj(D, chunk)
        for c0 in range(0, D, chunk):
            if c0 + chunk < D:
                proj(c0 + chunk, chunk)
                proj(D + c0 + chunk, chunk)
            else:
                proj(2 * D, 2 * D)
            lru_chunk(c0)
        proj(4 * D, 2 * D)
        _, xhn = _layernorm_stats(_gelu(z_ref[:, 3 * D:4 * D]))
        vln = xhn * lg_ref[...] + lb_ref[...]
        _, mixed = _sgu_mix(vln, ws_ref, bst_ref, tm)
        ybp = (_gelu(z_ref[:, 2 * D:3 * D]) * mixed).astype(BF)
        ybp_ref[...] = ybp
        ya = _dot(yap_ref[...], wba_ref[...])
        yb = _dot(ybp, wbb_ref[...])
        merged = (_sigmoid_t(z_ref[:, 4 * D:5 * D]) * ya + _sigmoid_t(z_ref[:, 5 * D:6 * D]) * yb).astype(BF)
        o = _dot(merged, wo_ref[...])
        x2_ref[...] = xv + g1_ref[...] * o
        mg_ref[...] = merged
        ya_ref[...] = ya.astype(BF)
        yb_ref[...] = yb.astype(BF)
        o_ref[...] = o.astype(BF)

    row = pl.BlockSpec((tm, D), lambda i: (i, 0))
    vec = pl.BlockSpec((1, D), lambda i: (0, 0))
    bf_row = jax.ShapeDtypeStruct((T, D), BF)
    f32_row = jax.ShapeDtypeStruct((T, D), F32)
    return _pcall(body, name="mixer_fwd", grid=(T // tm,),
                  in_specs=[row, vec, vec, vec, vec, _resident(w_in4.shape), _resident(cw.shape), vec,
                            _resident(wa.shape), vec, _resident(wx.shape), vec, vec, vec, vec,
                            _resident(ws.shape), _resident(bst.shape),
                            _resident(wba.shape), _resident(wbb.shape), _resident(wo.shape)],
                  out_specs=[row, pl.BlockSpec((tm, 6 * D), lambda i: (i, 0))] + [row] * 13,
                  out_shape=[bf_row, jax.ShapeDtypeStruct((T, 6 * D), F32), f32_row, bf_row, bf_row, bf_row, bf_row,
                             bf_row, bf_row, f32_row, bf_row, bf_row, bf_row, bf_row, f32_row],
                  scratch=[pltpu.VMEM((SUBLANES, D), F32), pltpu.VMEM((SUBLANES, D), F32)], vmem_mb=60)(
        x, g, scale, shift, gate1, w_in4, cw, cb, wa, ba, wx, bx, lam, lg, lb, ws, bst, wba, wbb, wo)


def _ffn_fwd(x2, g, scale, shift, gate2, gf, w_up4, wd, cw, cb, target, tm=256, chunk=768):
    T = x2.shape[0]
    tm = min(tm, T)
    ns = w_up4.shape[2]
    dff = wd.shape[0]
    nchunk = dff // chunk
    per = ns // chunk

    def body(x2_ref, g_ref, sc_ref, sh_ref, g2_ref, gf_ref, wu_ref, wd_ref, cw_ref, cb_ref, t_ref,
             h2_ref, up_ref, f_ref, ga_ref, vd_ref, loss_ref, dx3_ref, dfo_ref, dgf_ref, dg2_ref, prev):
        i = pl.program_id(0)

        @pl.when(i == 0)
        def _():
            prev[...] = jnp.zeros_like(prev)
            loss_ref[...] = jnp.zeros_like(loss_ref)
            dgf_ref[...] = jnp.zeros_like(dgf_ref)
            dg2_ref[...] = jnp.zeros_like(dg2_ref)

        x2v = x2_ref[...]
        _, xh2 = _rms_stats(x2v)
        h2 = ((xh2 * g_ref[...]) * (1.0 + sc_ref[...]) + sh_ref[...]).astype(BF)
        h2_ref[...] = h2

        def conv(u, col):
            cs = slice(col, col + chunk)
            p = prev[:, cs]
            hid = (cb_ref[:, cs] + cw_ref[2:3, cs] * u + cw_ref[1:2, cs] * _shift_down(u, p, 1)
                   + cw_ref[0:1, cs] * _shift_down(u, p, 2))
            prev[:, cs] = u[tm - SUBLANES:tm]
            up_ref[:, cs] = u.astype(BF)
            return hid

        def up_proj(k):
            off = (k % per) * chunk
            return (_dot(h2, wu_ref[k // per, :, off:off + chunk]),
                    _dot(h2, wu_ref[N_CHIPS // 2 + k // per, :, off:off + chunk]))

        fo = None
        nxt = up_proj(0)
        for k in range(nchunk):
            col = k * chunk
            ua, uv = nxt
            if k + 1 < nchunk:
                nxt = up_proj(k + 1)
            act = conv(ua, col)
            val = conv(uv, dff + col)
            ga, dga = _gelu_and_grad(act)
            fk = (ga * val).astype(BF)
            f_ref[:, col:col + chunk] = fk
            ga_ref[:, col:col + chunk] = ga.astype(BF)
            vd_ref[:, col:col + chunk] = (val * dga).astype(BF)
            part = _dot(fk, wd_ref[col:col + chunk, :])
            fo = part if fo is None else fo + part

        x3 = x2v + g2_ref[...] * fo
        rstd, xh = _rms_stats(x3)
        err = xh * gf_ref[...] - t_ref[...]
        loss_ref[...] += 0.5 * jnp.sum(jnp.mean(err * err, axis=-1, keepdims=True), axis=0, keepdims=True)
        dy = err * (1.0 / D)
        dgf_ref[...] += _colsum(dy * xh)
        dxh = dy * gf_ref[...]
        dx3 = rstd * (dxh - xh * jnp.mean(dxh * xh, axis=-1, keepdims=True))
        dg2_ref[...] += _colsum(dx3 * fo)
        dx3_ref[...] = dx3
        dfo_ref[...] = (g2_ref[...] * dx3).astype(BF)

    row = pl.BlockSpec((tm, D), lambda i: (i, 0))
    vec = pl.BlockSpec((1, D), lambda i: (0, 0))
    wide = pl.BlockSpec((tm, 2 * dff), lambda i: (i, 0))
    half = pl.BlockSpec((tm, dff), lambda i: (i, 0))
    return _pcall(body, name="ffn_fwd", grid=(T // tm,),
                  in_specs=[row, vec, vec, vec, vec, vec, _resident(w_up4.shape), _resident(wd.shape),
                            _resident(cw.shape), _resident(cb.shape), row],
                  out_specs=[row, wide, half, half, half, pl.BlockSpec((1, LANES), lambda i: (0, 0)), row, row, vec, vec],
                  out_shape=[jax.ShapeDtypeStruct((T, D), BF), jax.ShapeDtypeStruct((T, 2 * dff), BF),
                             jax.ShapeDtypeStruct((T, dff), BF), jax.ShapeDtypeStruct((T, dff), BF),
                             jax.ShapeDtypeStruct((T, dff), BF), jax.ShapeDtypeStruct((1, LANES), F32),
                             jax.ShapeDtypeStruct((T, D), F32), jax.ShapeDtypeStruct((T, D), BF),
                             jax.ShapeDtypeStruct((1, D), F32), jax.ShapeDtypeStruct((1, D), F32)],
                  scratch=[pltpu.VMEM((SUBLANES, 2 * dff), F32)], vmem_mb=56)(
        x2, g, scale, shift, gate2, gf, w_up4, wd, cw, cb, target)


def _ffn_bwd(dfo, wd, up, ga, vd, cw, w_up4, x2, resid, g, scale, gate, o, tm=256, chunk=1536):
    T = up.shape[0]
    tm = min(tm, T)
    dff = wd.shape[0]
    ns = w_up4.shape[2]
    nchunk = dff // chunk
    per = ns // chunk
    nrow = T // tm

    def body(dfo_ref, wd_ref, up_ref, ga_ref, vd_ref, cw_ref, wu_ref, x_ref, r_ref, g_ref, sc_ref, gt_ref, o_ref,
             du_ref, dcw_ref, dcb_ref, dx_ref, dsh_ref, dsc_ref, dg_ref, do_ref, dgt_ref, nxt):
        i = pl.program_id(0)

        @pl.when(i == 0)
        def _():
            nxt[...] = jnp.zeros_like(nxt)
            for ref in (dcw_ref, dcb_ref, dsh_ref, dsc_ref, dg_ref, dgt_ref):
                ref[...] = jnp.zeros_like(ref)

        dfo_t = dfo_ref[...]

        def conv_bwd(dh, col):
            cs = slice(col, col + chunk)
            n8 = nxt[:, cs]
            dh1 = _shift_up(dh, n8, 1)
            dh2 = _shift_up(dh, n8, 2)
            nxt[:, cs] = dh[0:SUBLANES]
            du = (cw_ref[2:3, cs] * dh + cw_ref[1:2, cs] * dh1 + cw_ref[0:1, cs] * dh2).astype(BF)
            du_ref[:, cs] = du
            u = up_ref[:, cs].astype(F32)
            dcw_ref[2:3, cs] += _colsum(dh * u)
            dcw_ref[1:2, cs] += _colsum(dh1 * u)
            dcw_ref[0:1, cs] += _colsum(dh2 * u)
            dcb_ref[:, cs] += _colsum(dh)
            return du

        def down_bwd(k):
            return _dot_nt(dfo_t, wd_ref[k * chunk:(k + 1) * chunk, :])

        dh = None
        df_next = down_bwd(0)
        for k in range(nchunk):
            col = k * chunk
            off = (k % per) * chunk
            df = df_next
            if k + 1 < nchunk:
                df_next = down_bwd(k + 1)
            du_a = conv_bwd(df * vd_ref[:, col:col + chunk].astype(F32), col)
            du_v = conv_bwd(df * ga_ref[:, col:col + chunk].astype(F32), dff + col)
            part = (_dot_nt(du_a, wu_ref[k // per, :, off:off + chunk])
                    + _dot_nt(du_v, wu_ref[N_CHIPS // 2 + k // per, :, off:off + chunk]))
            dh = part if dh is None else dh + part

        rstd, xh = _rms_stats(x_ref[...])
        dsh_ref[...] += _colsum(dh)
        dsc_ref[...] += _colsum(dh * (xh * g_ref[...]))
        dn = dh * (1.0 + sc_ref[...])
        dg_ref[...] += _colsum(dn * xh)
        dxh = dn * g_ref[...]
        dx = r_ref[...] + rstd * (dxh - xh * jnp.mean(dxh * xh, axis=-1, keepdims=True))
        dx_ref[...] = dx
        do_ref[...] = (gt_ref[...] * dx).astype(BF)
        dgt_ref[...] += _colsum(dx * o_ref[...].astype(F32))

    rev = lambda i: (nrow - 1 - i, 0)
    row = pl.BlockSpec((tm, D), rev)
    vec = pl.BlockSpec((1, D), lambda i: (0, 0))
    wide = pl.BlockSpec((tm, 2 * dff), rev)
    half = pl.BlockSpec((tm, dff), rev)
    cw3 = pl.BlockSpec((3, 2 * dff), lambda i: (0, 0))
    cb1 = pl.BlockSpec((1, 2 * dff), lambda i: (0, 0))
    vshape = jax.ShapeDtypeStruct((1, D), F32)
    return _pcall(body, name="ffn_bwd", grid=(nrow,),
                  in_specs=[row, _resident(wd.shape), wide, half, half, _resident(cw.shape), _resident(w_up4.shape),
                            row, row, vec, vec, vec, row],
                  out_specs=[wide, cw3, cb1, row, vec, vec, vec, row, vec],
                  out_shape=[jax.ShapeDtypeStruct((T, 2 * dff), BF), jax.ShapeDtypeStruct((3, 2 * dff), F32),
                             jax.ShapeDtypeStruct((1, 2 * dff), F32), jax.ShapeDtypeStruct((T, D), F32),
                             vshape, vshape, vshape, jax.ShapeDtypeStruct((T, D), BF), vshape],
                  scratch=[pltpu.VMEM((SUBLANES, 2 * dff), F32)], vmem_mb=60)(
        dfo, wd, up, ga, vd, cw, w_up4, x2, resid, g, scale, gate, o)


def _mm_tn_cols(a, b, name, nshard, nb, mb=None, tm=TN_ROWS):
    T, M = a.shape
    tm = min(tm, T)
    mb = M if mb is None else mb
    ns = b.shape[1] // nshard
    per = ns // nb
    nk = T // tm
    vmem_mb = (2 * 2 * tm * (mb + nb) + 2 * (4 + 2) * mb * nb) // 2 ** 20 + 8

    def body(a_ref, b_ref, o_ref, c_ref):
        k = pl.program_id(2)

        @pl.when(k == 0)
        def _():
            o_ref[...] = jnp.zeros_like(o_ref)

        o_ref[0] += _dot_tn(a_ref[...], b_ref[...])

        @pl.when(k == nk - 1)
        def _():
            c_ref[...] = o_ref[...].astype(BF)

    out_spec = pl.BlockSpec((1, mb, nb), lambda m, t, k: (t // per, m, t % per))
    return _pcall(body, name=name, grid=(M // mb, nshard * per, nk),
                  in_specs=[pl.BlockSpec((tm, mb), lambda m, t, k: (k, m)),
                            pl.BlockSpec((tm, nb), lambda m, t, k: (k, t))],
                  out_specs=[out_spec, out_spec],
                  out_shape=[jax.ShapeDtypeStruct((nshard, M, ns), F32), jax.ShapeDtypeStruct((nshard, M, ns), BF)],
                  vmem_mb=vmem_mb)(a, b)


def _mm_nt_normbwd(dz, w4, x, resid, g, scale, name, tm=256):
    T = x.shape[0]
    tm = min(tm, T)
    ns = w4.shape[2]

    def body(dz_ref, w_ref, x_ref, r_ref, g_ref, sc_ref, dx_ref, dsh_ref, dsc_ref, dg_ref):
        i = pl.program_id(0)

        @pl.when(i == 0)
        def _():
            dsh_ref[...] = jnp.zeros_like(dsh_ref)
            dsc_ref[...] = jnp.zeros_like(dsc_ref)
            dg_ref[...] = jnp.zeros_like(dg_ref)

        dh = None
        for j in range(N_CHIPS):
            part = _dot_nt(dz_ref[:, j * ns:(j + 1) * ns], w_ref[j])
            dh = part if dh is None else dh + part
        rstd, xh = _rms_stats(x_ref[...])
        dsh_ref[...] += _colsum(dh)
        dsc_ref[...] += _colsum(dh * (xh * g_ref[...]))
        dn = dh * (1.0 + sc_ref[...])
        dg_ref[...] += _colsum(dn * xh)
        dxh = dn * g_ref[...]
        dx_ref[...] = r_ref[...] + rstd * (dxh - xh * jnp.mean(dxh * xh, axis=-1, keepdims=True))

    row = pl.BlockSpec((tm, D), lambda i: (i, 0))
    vec = pl.BlockSpec((1, D), lambda i: (0, 0))
    return _pcall(body, name=name, grid=(T // tm,),
                  in_specs=[pl.BlockSpec((tm, N_CHIPS * ns), lambda i: (i, 0)), _resident(w4.shape), row, row, vec, vec],
                  out_specs=[row, vec, vec, vec],
                  out_shape=[jax.ShapeDtypeStruct((T, D), F32)] + [jax.ShapeDtypeStruct((1, D), F32)] * 3,
                  vmem_mb=48)(dz, w4, x, resid, g, scale)


def _mix_bwd(do, ya, yb, z, wo, wba, wbb, tm=256):
    T = do.shape[0]
    tm = min(tm, T)

    def body(do_ref, ya_ref, yb_ref, ga_ref, gb_ref, wo_ref, wa_ref, wb_ref,
             dz_ref, dya_ref, dyb_ref, dyap_ref, dybp_ref):
        dm = _dot_nt(do_ref[...], wo_ref[...])
        sa = _sigmoid_t(ga_ref[...])
        sb = _sigmoid_t(gb_ref[...])
        dya = (sa * dm).astype(BF)
        dyb = (sb * dm).astype(BF)
        dz_ref[:, 0:D] = (dm * ya_ref[...].astype(F32) * sa * (1.0 - sa)).astype(BF)
        dz_ref[:, D:2 * D] = (dm * yb_ref[...].astype(F32) * sb * (1.0 - sb)).astype(BF)
        dya_ref[...] = dya
        dyb_ref[...] = dyb
        dyap_ref[...] = _dot_nt(dya, wa_ref[...]).astype(BF)
        dybp_ref[...] = _dot_nt(dyb, wb_ref[...]).astype(BF)

    row = pl.BlockSpec((tm, D), lambda i: (i, 0))
    wspec = pl.BlockSpec((D, D), lambda i: (0, 0))
    return _pcall(body, name="mix_bwd", grid=(T // tm,),
                  in_specs=[row, row, row, pl.BlockSpec((tm, D), lambda i: (i, 4)),
                            pl.BlockSpec((tm, D), lambda i: (i, 5)), wspec, wspec, wspec],
                  out_specs=[pl.BlockSpec((tm, 2 * D), lambda i: (i, 2)), row, row, row, row],
                  out_shape=[jax.ShapeDtypeStruct((T, 6 * D), BF)] + [jax.ShapeDtypeStruct((T, D), BF)] * 4,
                  vmem_mb=48)(do, ya, yb, z, z, wo, wba, wbb)


def _sgu_bwd(dz, dyb_pre, z, lg, lb, ws, bst, tb=256):
    T = z.shape[0]
    tb = min(tb, T)

    def body(dz_in, dy_ref, zu_ref, zv_ref, lg_ref, lb_ref, ws_ref, bst_ref,
             dz_ref, dws_ref, dbst_ref, dlg_ref, dlb_ref):
        del dz_in
        i = pl.program_id(0)

        @pl.when(i == 0)
        def _():
            dws_ref[...] = jnp.zeros_like(dws_ref)
            dbst_ref[...] = jnp.zeros_like(dbst_ref)
            dlg_ref[...] = jnp.zeros_like(dlg_ref)
            dlb_ref[...] = jnp.zeros_like(dlb_ref)

        gu, dgu = _gelu_and_grad(zu_ref[...])
        gv, dgv = _gelu_and_grad(zv_ref[...])
        rstd, xh = _layernorm_stats(gv)
        vln = xh * lg_ref[...] + lb_ref[...]
        wm, mixed = _sgu_mix(vln, ws_ref, bst_ref, tb)
        dy = dy_ref[...].astype(F32)
        dz_ref[:, 0:D] = (dy * mixed * dgu).astype(BF)
        dmixed = dy * gu
        ri = lax.broadcasted_iota(jnp.int32, (SGU_BLOCK, SGU_BLOCK), 0)
        ci = lax.broadcasted_iota(jnp.int32, (SGU_BLOCK, SGU_BLOCK), 1)
        blocks = []
        for blk in range(tb // SGU_BLOCK):
            rs = slice(blk * SGU_BLOCK, (blk + 1) * SGU_BLOCK)
            cols = []
            for g in range(HEADS):
                cs = slice(g * HD, (g + 1) * HD)
                dmg = dmixed[rs, cs]
                dmb = dmg.astype(BF)
                dbst_ref[:, g:g + 1] += jnp.sum(dmg, axis=1, keepdims=True)
                dws_ref[g] += jnp.where(ri >= ci, _dot_nt(dmb, vln[rs, cs].astype(BF)), 0.0)
                cols.append(_dot_tn(wm[g], dmb))
            blocks.append(jnp.concatenate(cols, axis=1))
        dvln = blocks[0] if len(blocks) == 1 else jnp.concatenate(blocks, axis=0)
        dlg_ref[...] += _colsum(dvln * xh)
        dlb_ref[...] += _colsum(dvln)
        dxh = dvln * lg_ref[...]
        dgv_in = rstd * (dxh - jnp.mean(dxh, axis=-1, keepdims=True)
                         - xh * jnp.mean(dxh * xh, axis=-1, keepdims=True))
        dz_ref[:, D:2 * D] = (dgv_in * dgv).astype(BF)

    row = pl.BlockSpec((tb, D), lambda i: (i, 0))
    vec = pl.BlockSpec((1, D), lambda i: (0, 0))
    wspec = pl.BlockSpec((HEADS, SGU_BLOCK, SGU_BLOCK), lambda i: (0, 0, 0))
    bspec = pl.BlockSpec((SGU_BLOCK, HEADS), lambda i: (0, 0))
    return _pcall(body, name="sgu_bwd", grid=(T // tb,),
                  in_specs=[HBM_SPEC, row, pl.BlockSpec((tb, D), lambda i: (i, 2)),
                            pl.BlockSpec((tb, D), lambda i: (i, 3)), vec, vec, wspec, bspec],
                  out_specs=[pl.BlockSpec((tb, 2 * D), lambda i: (i, 1)), wspec, bspec, vec, vec],
                  out_shape=[jax.ShapeDtypeStruct(dz.shape, BF),
                             jax.ShapeDtypeStruct((HEADS, SGU_BLOCK, SGU_BLOCK), F32),
                             jax.ShapeDtypeStruct((SGU_BLOCK, HEADS), F32),
                             jax.ShapeDtypeStruct((1, D), F32), jax.ShapeDtypeStruct((1, D), F32)],
                  aliases={0: 0}, vmem_mb=48)(dz, dyb_pre, z, z, lg, lb, ws, bst)


def _rglru_bwd(dz, dya_pre, z, h, xc_s, r_s, ig_s, mult_s, a_s, cw, wa, wx, lam, tb=256):
    T = z.shape[0]
    tb = min(tb, T)
    nrow = T // tb
    per = tb // SUBLANES

    def body(dz_in, dy_ref, xr_ref, gr_ref, h_ref, hh_ref, xc_ref, r_ref, ig_ref, mu_ref, a_ref, cw_ref, wa_ref,
             wx_ref, lam_ref, dz_ref, dcw_ref, dcb_ref, dwa_ref, dba_ref, dwx_ref, dbx_ref, dlam_ref, carry, nxt):
        del dz_in
        i = pl.program_id(0)
        first_block = i == nrow - 1

        @pl.when(i == 0)
        def _():
            carry[...] = jnp.zeros_like(carry)
            nxt[...] = jnp.zeros_like(nxt)
            for ref in (dcw_ref, dcb_ref, dwa_ref, dba_ref, dwx_ref, dbx_ref, dlam_ref):
                ref[...] = jnp.zeros_like(ref)

        xc = xc_ref[...].astype(F32)
        r = r_ref[...].astype(F32)
        ig = ig_ref[...].astype(F32)
        mult = mu_ref[...].astype(F32)
        a = a_ref[...]
        lam = lam_ref[...]
        ls = _log_sigmoid(lam)
        hv = h_ref[...]
        hprev = _shift_down(hv, jnp.where(first_block, 0.0, hh_ref[...]), 1)
        gg, dgg = _gelu_and_grad(gr_ref[...])
        dy = dy_ref[...].astype(F32)
        dz_ref[:, D:2 * D] = (dy * hv * dgg).astype(BF)

        rows = lax.broadcasted_iota(jnp.int32, (tb, D), 0)
        v = dy * gg + jnp.where(rows == tb - 1, carry[0:1, :], 0.0)
        q = jnp.where(rows < tb - 1, pltpu.roll(a, tb - 1, 0), 0.0)
        _, gsc = _scan_rows(q, v, reverse=True)
        carry[...] = (a * gsc)[0:SUBLANES]

        xi = ig * xc
        dmult = gsc * xi
        dxi = gsc * mult
        dig = dxi * xc
        dxc = dxi * ig
        dlog_a = gsc * hprev * a - dmult * (a * a) * pl.reciprocal(mult, approx=True)
        dlam_ref[...] += _colsum(dlog_a * r) * (LRU_C * _sigmoid(-lam))
        dpr = dlog_a * (LRU_C * ls) * r * (1.0 - r)
        dpi = dig * ig * (1.0 - ig)
        dba_ref[...] += _colsum(dpr)
        dbx_ref[...] += _colsum(dpi)
        back = []
        for hh in range(HEADS):
            cs = slice(hh * HD, (hh + 1) * HD)
            xh = xc[:, cs].astype(BF)
            dprh = dpr[:, cs].astype(BF)
            dpih = dpi[:, cs].astype(BF)
            dwa_ref[hh] += _dot_tn(xh, dprh)
            dwx_ref[hh] += _dot_tn(xh, dpih)
            back.append(_dot_nt(dprh, wa_ref[hh].astype(BF)) + _dot_nt(dpih, wx_ref[hh].astype(BF)))
        dxc = dxc + jnp.concatenate(back, axis=1)

        n8 = nxt[...]
        d1 = _shift_up(dxc, n8, 1)
        d2 = _shift_up(dxc, n8, 2)
        d3 = _shift_up(dxc, n8, 3)
        nxt[...] = dxc[0:SUBLANES]
        dz_ref[:, 0:D] = (cw_ref[3:4, :] * dxc + cw_ref[2:3, :] * d1 + cw_ref[1:2, :] * d2
                          + cw_ref[0:1, :] * d3).astype(BF)
        xr = xr_ref[...]
        dcw_ref[3:4, :] += _colsum(dxc * xr)
        dcw_ref[2:3, :] += _colsum(d1 * xr)
        dcw_ref[1:2, :] += _colsum(d2 * xr)
        dcw_ref[0:1, :] += _colsum(d3 * xr)
        dcb_ref[...] += _colsum(dxc)

    rev = lambda col: (lambda i: (nrow - 1 - i, col))
    row = pl.BlockSpec((tb, D), rev(0))
    halo = pl.BlockSpec((SUBLANES, D), lambda i: (jnp.maximum((nrow - 1 - i) * per - 1, 0), 0))
    vec = pl.BlockSpec((1, D), lambda i: (0, 0))
    wspec = pl.BlockSpec((HEADS, HD, HD), lambda i: (0, 0, 0))
    c4 = pl.BlockSpec((4, D), lambda i: (0, 0))
    wshape = jax.ShapeDtypeStruct((HEADS, HD, HD), F32)
    vshape = jax.ShapeDtypeStruct((1, D), F32)
    return _pcall(body, name="rglru_bwd", grid=(nrow,),
                  in_specs=[HBM_SPEC, row, row, pl.BlockSpec((tb, D), rev(1)), row, halo,
                            row, row, row, row, row, c4, wspec, wspec, vec],
                  out_specs=[pl.BlockSpec((tb, 2 * D), rev(0)), c4, vec, wspec, vec, wspec, vec, vec],
                  out_shape=[jax.ShapeDtypeStruct(dz.shape, BF), jax.ShapeDtypeStruct((4, D), F32), vshape,
                             wshape, vshape, wshape, vshape, vshape],
                  scratch=[pltpu.VMEM((SUBLANES, D), F32), pltpu.VMEM((SUBLANES, D), F32)],
                  aliases={0: 0}, vmem_mb=56)(dz, dya_pre, z, z, h, h, xc_s, r_s, ig_s, mult_s, a_s, cw, wa, wx, lam)


def _pack_rows(parts):
    out = []
    for p in parts:
        q = p.reshape(-1, LANES)
        pad = (-q.shape[0]) % SUBLANES
        if pad:
            q = jnp.concatenate([q, jnp.zeros((pad, LANES), q.dtype)], axis=0)
        out.append(q)
    return jnp.concatenate(out, axis=0)


def _rows_of(shape):
    n = 1
    for s in shape:
        n *= s
    rows = n // LANES
    return rows + (-rows) % SUBLANES


def kernel(x, c, w_ada, b_ada, norm_mix_g, w_in, rnn_conv_w, rnn_conv_b, lru_w_a, lru_b_a, lru_w_x, lru_b_x, lru_lambda, sgu_ln_g, sgu_ln_b, sgu_w_s, sgu_b_s, w_branch_a, w_branch_b, w_out, norm_ffn_g, w_up, ffn_conv_w, ffn_conv_b, w_down, norm_final_g, loss_target, m_w_ada, m_b_ada, m_norm_mix_g, m_w_in, m_rnn_conv_w, m_rnn_conv_b, m_lru_w_a, m_lru_b_a, m_lru_w_x, m_lru_b_x, m_lru_lambda, m_sgu_ln_g, m_sgu_ln_b, m_sgu_w_s, m_sgu_b_s, m_w_branch_a, m_w_branch_b, m_w_out, m_norm_ffn_g, m_w_up, m_ffn_conv_w, m_ffn_conv_b, m_w_down, m_norm_final_g, v_w_ada, v_b_ada, v_norm_mix_g, v_w_in, v_rnn_conv_w, v_rnn_conv_b, v_lru_w_a, v_lru_b_a, v_lru_w_x, v_lru_b_x, v_lru_lambda, v_sgu_ln_g, v_sgu_ln_b, v_sgu_w_s, v_sgu_b_s, v_w_branch_a, v_w_branch_b, v_w_out, v_norm_ffn_g, v_w_up, v_ffn_conv_w, v_ffn_conv_b, v_w_down, v_norm_final_g):
    args = dict(locals())
    T = x.shape[1]
    mx, my, mc = lax.axis_index("x"), lax.axis_index("y"), lax.axis_index("c")
    chip = 2 * mx + my
    dev = 2 * chip + mc
    vec = lambda a: a.reshape(1, -1)

    xt = x.reshape(T, D)
    tgt = loss_target.reshape(T, D)
    ns = w_in.shape[2]
    dff = w_down.shape[1] * N_CHIPS

    c_all = _gather8(c.reshape(SUBLANES, LANES), "gather_c").reshape(N_DEV, D)
    b_ada_sh = lax.dynamic_slice(b_ada, (0, chip * ns), (1, ns))
    mod_sh = _mod_fwd(c_all, w_ada[0], b_ada_sh)

    mixer_w = _cast_shards([w_in[0], w_branch_a[0], w_branch_b[0], w_out[0]], "cast_mixer_weights")
    w_in4, wba4, wbb4, wo4, rcw4, fcw4, mod4 = _gather_weights(
        list(mixer_w) + [rnn_conv_w[0], ffn_conv_w[0], mod_sh], [True] * 4 + [False] * 3)
    late = _cast_shards([w_up[0], w_down[0]], "cast_late", after=mod4)
    late_plan = _gather_plan(len(late))
    late_handle, late_token = _remote_start(
        late, [lax.empty((N_CHIPS,) + w.shape, w.dtype) for w in late], late_plan, 3 * len(late), "gather_late_start")
    rcw_full = jnp.transpose(rcw4, (1, 0, 2)).reshape(4, D)
    fcw_full = jnp.transpose(fcw4, (1, 0, 2)).reshape(3, 2 * dff)
    mod = lax.dynamic_index_in_dim(mod4, dev, axis=1, keepdims=False).reshape(1, 6 * D)
    shift1, scale1, gate1, shift2, scale2, gate2 = [mod[:, k * D:(k + 1) * D] for k in range(6)]

    bst = jnp.transpose(sgu_b_s[0])
    wba_full = wba4.reshape(D, D)
    wbb_full = wbb4.reshape(D, D)
    wo_full = wo4.reshape(D, D)
    h1, z, h_lru, ya_pre, yb_pre, merged, ya, yb, o1, x2, lru_xc, lru_r, lru_i, lru_mult, lru_a = _mixer_fwd(
        xt, norm_mix_g, scale1 + late_token[0:1, 0:1], shift1, gate1, w_in4, rcw_full, rnn_conv_b,
        lru_w_a[0], lru_b_a, lru_w_x[0], lru_b_x, lru_lambda, sgu_ln_g, sgu_ln_b, sgu_w_s[0], bst,
        wba_full, wbb_full, wo_full)
    late, late_lands = _remote_wait(late_handle, late_plan, o1, "gather_late_wait")
    w_up4, w_down4 = _place_own(late, late_lands)
    wd_full = w_down4.reshape(dff, D)
    h2, up, f, ffn_ga, ffn_vd, loss_part, dx3, dfo, dgf, dgate2 = _ffn_fwd(
        x2, norm_ffn_g, scale2, shift2, gate2, vec(norm_final_g), w_up4, wd_full, fcw_full, ffn_conv_b, tgt)

    dup, dfcw, dfcb, dx2, dshift2, dscale2, dg_ffn, do1, dgate1 = _ffn_bwd(
        dfo, wd_full, up, ffn_ga, ffn_vd, fcw_full, w_up4, x2, dx3, norm_ffn_g, scale2, gate1, o1)
    dwd = _mm_tn_cols(f, dfo, "dw_down", 1, D, mb=D, tm=TN_ROWS_SQUARE)
    dw_up4 = _mm_tn_cols(h2, dup, "dw_up", N_CHIPS, ns)
    dz, dya, dyb, dya_pre, dyb_pre = _mix_bwd(do1, ya, yb, z, wo_full, wba_full, wbb_full)
    dwo = _mm_tn_cols(merged, do1, "dw_out", 1, D, tm=TN_ROWS_SQUARE)
    dwba = _mm_tn_cols(ya_pre, dya, "dw_branch_a", 1, D, tm=TN_ROWS_SQUARE)
    dwbb = _mm_tn_cols(yb_pre, dyb, "dw_branch_b", 1, D, tm=TN_ROWS_SQUARE)

    chip_id = chip.astype(jnp.int32).reshape(1)

    def reduce_start(group, name):
        wire = [g16.reshape(N_CHIPS, -1, g16.shape[-1]) for _, (_, g16) in group]
        lands = [lax.empty((3,) + w.shape[1:], w.dtype) for w in wire]
        return _remote_start(wire, lands, _scatter_plan(len(group)), 3 * len(group), "scatter_start_" + name)

    def reduce_finish(group, handle, after, name):
        _, landed = _remote_wait(handle, _scatter_plan(len(group)), after, "scatter_wait_" + name)
        return [_sum_own_and_landed(chip_id, g32.reshape(N_CHIPS, -1, g32.shape[-1]), l, "sum_chips_" + n)
                for (n, (g32, _)), l in zip(group, landed)]

    group1 = [("w_up", dw_up4), ("w_down", dwd), ("w_branch_a", dwba), ("w_branch_b", dwbb), ("w_out", dwo)]
    handle1, token1 = reduce_start(group1, "late")
    dz, dws, dbst, dlg, dlb = _sgu_bwd(dz, dyb_pre, z, sgu_ln_g + token1[0:1, 0:1], sgu_ln_b, sgu_w_s[0], bst)
    dz, drcw, drcb, dwa, dba, dwx, dbx, dlam = _rglru_bwd(
        dz, dya_pre, z, h_lru, lru_xc, lru_r, lru_i, lru_mult, lru_a, rcw_full, lru_w_a[0], lru_w_x[0], lru_lambda)
    early_small = [("rnn_conv_b", drcb), ("lru_w_a", dwa), ("lru_b_a", dba), ("lru_w_x", dwx), ("lru_b_x", dbx),
                   ("lru_lambda", dlam), ("sgu_ln_g", dlg), ("sgu_ln_b", dlb), ("sgu_w_s", dws),
                   ("sgu_b_s", jnp.transpose(dbst)), ("norm_ffn_g", dg_ffn),
                   ("ffn_conv_b", dfcb), ("norm_final_g", dgf)]
    r_early = sum(_rows_of(args[n].shape) for n, _ in early_small)
    early_pack = _pack_rows([g for _, g in early_small] + [drcw, dfcw])
    early_pack = jnp.concatenate(
        [early_pack, jnp.zeros(((-early_pack.shape[0]) % 256, LANES), F32)], axis=0)
    early_chip = _add_pair(early_pack, _swap_cores([early_pack], "swap_small_grads")[0], "sum_cores_small_grads")
    early_handle, token3 = _remote_start([early_chip], [lax.empty((3,) + early_chip.shape, F32)], _bcast_plan, 3,
                                         "small_grads_start")
    def swap_start(totals, name):
        lands = [lax.empty(t.shape, t.dtype) for t in totals]
        return _remote_start(totals, lands, _sibling_plan(len(totals)), len(totals), "swap_sums_start_" + name)

    out = {}

    def swap_finish(group, handle, after, name):
        mine, theirs = _remote_wait(handle, _sibling_plan(len(group)), after, "swap_sums_wait_" + name)
        for (n, _), a, b in zip(group, mine, theirs):
            shape = args[n].shape
            res = _adamw(args[n][0], args["m_" + n][0], args["v_" + n][0], [a, b], "adamw_" + n)
            for kind, r in zip(("grad_", "delta_", "new_m_", "new_v_"), res):
                out[kind + n] = r.reshape(shape)
        return res[3]

    swap1, token4 = swap_start(reduce_finish(group1, handle1, drcb, "late"), "late")
    group2 = [("w_in", _mm_tn_cols(h1, dz, "dw_in", N_CHIPS, ns))]
    handle2, token2 = reduce_start(group2, "in")
    tokens = token2[0:1, 0:1] + token3[0:1, 0:1] + token4[0:1, 0:1]
    grad_x, dshift1, dscale1, dg_mix = _mm_nt_normbwd(
        dz, w_in4, xt, dx2, norm_mix_g + tokens, scale1, "dh1_norm_bwd")
    swap2, token5 = swap_start(reduce_finish(group2, handle2, dg_mix, "in"), "in")
    dmod = jnp.concatenate([dshift1, dscale1, dgate1, dshift2, dscale2, dgate2], axis=1)
    last = swap_finish(group1, swap1, token5, "late")
    swap_finish(group2, swap2, last, "in")

    late_small = [("b_ada", dmod), ("norm_mix_g", dg_mix)]
    small = late_small + early_small
    late_all = _gather8(_pack_rows([g for _, g in late_small] + [loss_part]), "gather_late_small_grads")
    late_sum = _sum_parts(late_all, "sum_late_small_grads")
    r_late = sum(_rows_of(args[n].shape) for n, _ in late_small)
    loss = late_sum[r_late, 0]
    late_sum = late_sum[:r_late]
    _, (early_landed,) = _remote_wait(early_handle, _bcast_plan, dg_mix, "small_grads_wait")
    early_sum = _sum_chips_in_order(chip_id, early_chip, early_landed, "sum_early_small_grads")
    r_small = sum(_rows_of(args[n].shape) for n, _ in small)
    r_pad = r_small + (-r_small) % 256
    fill = jnp.zeros((r_pad - r_small, LANES), F32)
    g_small = jnp.concatenate([late_sum, early_sum[:r_early], fill], axis=0)

    def pack_small(prefix):
        return jnp.concatenate([_pack_rows([args[prefix + n] for n, _ in small]), fill], axis=0)

    res = _adamw(pack_small(""), pack_small("m_"), pack_small("v_"), [g_small], "adamw_small")
    off = 0
    for n, _ in small:
        shape = args[n].shape
        rows = _rows_of(shape)
        for kind, r in zip(("grad_", "delta_", "new_m_", "new_v_"), res):
            out[kind + n] = r[off:off + rows].reshape(shape)
        off += rows

    rcw_cols = rnn_conv_w.shape[2]
    g_rcw = lax.dynamic_slice(early_sum[r_early:r_early + 32].reshape(4, D), (0, chip * rcw_cols), (4, rcw_cols))
    g_fcw = lax.dynamic_slice(early_sum[r_early + 32:r_early + 32 + 144].reshape(3, 2 * dff), (0, chip * ns), (3, ns))
    conv = [("rnn_conv_w", g_rcw), ("ffn_conv_w", g_fcw)]
    res = _adamw(_pack_rows([args[n] for n, _ in conv]), _pack_rows([args["m_" + n] for n, _ in conv]),
                 _pack_rows([args["v_" + n] for n, _ in conv]), [_pack_rows([g for _, g in conv])], "adamw_conv")
    off = 0
    for n, _ in conv:
        shape = args[n].shape
        cnt = shape[1] * shape[2] // LANES
        for kind, r in zip(("grad_", "delta_", "new_m_", "new_v_"), res):
            out[kind + n] = r[off:off + cnt].reshape(shape)
        off += _rows_of(shape)

    dmod_all = late_all[:, 0:6 * D // LANES, :].reshape(N_DEV, 6 * D)
    dmod_sh = lax.dynamic_slice(dmod_all, (0, chip * ns), (N_DEV, ns))
    res = _ada_adamw(jnp.transpose(c_all), dmod_sh, w_ada[0], m_w_ada[0], v_w_ada[0])
    for kind, r in zip(("grad_", "delta_", "new_m_", "new_v_"), res):
        out[kind + "w_ada"] = r.reshape(w_ada.shape)

    names = ["w_ada", "b_ada", "norm_mix_g", "w_in", "rnn_conv_w", "rnn_conv_b", "lru_w_a", "lru_b_a", "lru_w_x",
             "lru_b_x", "lru_lambda", "sgu_ln_g", "sgu_ln_b", "sgu_w_s", "sgu_b_s", "w_branch_a", "w_branch_b",
             "w_out", "norm_ffn_g", "w_up", "ffn_conv_w", "ffn_conv_b", "w_down", "norm_final_g"]
    result = [loss, grad_x.reshape(x.shape)]
    for kind in ("grad_", "delta_", "new_m_", "new_v_"):
        result += [out[kind + n] for n in names]
    return tuple(result)
```

```python
import jax
import jax.numpy as jnp
from jax import lax
from jax.experimental import pallas as pl
from jax.experimental.pallas import tpu as pltpu

F32 = jnp.float32
BF = jnp.bfloat16

D = 1024
HEADS = 8
HD = D // HEADS
SGU_BLOCK = 128
N_CHIPS = 4
N_DEV = 8
EPS = 1e-6
LRU_C = 8.0
LANES = 128
SUBLANES = 8
ELEMENTWISE_BLOCK_BYTES = 3 << 19
TN_ROWS = 2048
TN_ROWS_SQUARE = 4096

ADAM_LR = 0.001
ADAM_B1 = 0.9
ADAM_B2 = 0.999
ADAM_EPS = 1e-08
ADAM_WD = 0.01
ADAM_STEP = 10

GELU_K0 = 0.7978845608028654
GELU_K1 = 0.044715

HBM_SPEC = pl.BlockSpec(memory_space=pltpu.HBM)
MESH_ID = pl.DeviceIdType.MESH


def _pcall(body, *, name, out_shape, grid=(), in_specs=None, out_specs=None, scratch=(), vmem_mb=32, aliases=None,
           grid_spec=None):
    kw = {}
    if aliases:
        kw["input_output_aliases"] = aliases
    if grid_spec is not None:
        kw["grid_spec"] = grid_spec
        ndim = len(grid_spec.grid)
    else:
        kw.update(grid=grid, in_specs=in_specs, out_specs=out_specs, scratch_shapes=list(scratch))
        ndim = len(grid)
    if ndim:
        params = pltpu.CompilerParams(dimension_semantics=("arbitrary",) * ndim, vmem_limit_bytes=vmem_mb * 2 ** 20)
    else:
        params = pltpu.CompilerParams(vmem_limit_bytes=vmem_mb * 2 ** 20)
    return pl.pallas_call(body, name=name, out_shape=out_shape, compiler_params=params, **kw)


def _gelu_cdf(x, x2):
    return 0.5 * jnp.tanh(x * (GELU_K0 + (GELU_K0 * GELU_K1) * x2)) + 0.5


def _gelu(x):
    return x * _gelu_cdf(x, x * x)


def _gelu_and_grad(x):
    x2 = x * x
    s = _gelu_cdf(x, x2)
    g = x * s
    dg = s * (1.0 + (x - g) * ((2.0 * GELU_K0) + (6.0 * GELU_K0 * GELU_K1) * x2))
    return g, dg


def _sigmoid(x):
    return 1.0 / (1.0 + jnp.exp(-x))


def _sigmoid_t(x):
    return 0.5 * jnp.tanh(0.5 * x) + 0.5


def _log_sigmoid(x):
    e = jnp.exp(-jnp.abs(x))
    u = 1.0 + e
    d = u - 1.0
    l1p = jnp.where(d == 0.0, e, jnp.log(u) * (e / jnp.where(d == 0.0, 1.0, d)))
    return jnp.minimum(x, 0.0) - l1p


def _dot(a, b):
    return jnp.dot(a, b, preferred_element_type=F32)


def _dot_nt(a, b):
    return lax.dot_general(a, b, (((1,), (1,)), ((), ())), preferred_element_type=F32)


def _dot_tn(a, b):
    return lax.dot_general(a, b, (((0,), (0,)), ((), ())), preferred_element_type=F32)


def _shift_down(x, halo, s):
    r = pltpu.roll(x, s, 0)
    rows = lax.broadcasted_iota(jnp.int32, (SUBLANES, x.shape[1]), 0)
    head = jnp.where(rows < s, pltpu.roll(halo, s, 0), r[0:SUBLANES])
    return jnp.concatenate([head, r[SUBLANES:]], axis=0)


def _shift_up(x, halo, s):
    n = x.shape[0]
    r = pltpu.roll(x, n - s, 0)
    rows = lax.broadcasted_iota(jnp.int32, (SUBLANES, x.shape[1]), 0)
    tail = jnp.where(rows >= SUBLANES - s, pltpu.roll(halo, SUBLANES - s, 0), r[n - SUBLANES:n])
    return jnp.concatenate([r[:n - SUBLANES], tail], axis=0)


def _scan_rows(a, u, reverse):
    n, width = a.shape
    rows = lax.broadcasted_iota(jnp.int32, (n, width), 0)
    d = 1
    while d < n:
        if d < SUBLANES:
            keep = rows < n - d if reverse else rows >= d
            shift = n - d if reverse else d
            a_s = jnp.where(keep, pltpu.roll(a, shift, 0), 1.0)
            u_s = jnp.where(keep, pltpu.roll(u, shift, 0), 0.0)
        elif reverse:
            a_s = jnp.concatenate([a[d:], jnp.ones((d, width), a.dtype)], axis=0)
            u_s = jnp.concatenate([u[d:], jnp.zeros((d, width), u.dtype)], axis=0)
        else:
            a_s = jnp.concatenate([jnp.ones((d, width), a.dtype), a[:n - d]], axis=0)
            u_s = jnp.concatenate([jnp.zeros((d, width), u.dtype), u[:n - d]], axis=0)
        u = a * u_s + u
        a = a * a_s
        d *= 2
    return a, u


def _colsum(x):
    return jnp.sum(x, axis=0, keepdims=True)


def _rms_stats(x):
    r = lax.rsqrt(jnp.mean(x * x, axis=-1, keepdims=True) + EPS)
    return r, x * r


def _lru_gates(xc, wa_ref, ba, wx_ref, bx, lam, head0=0):
    pr, pi = [], []
    for hh in range(xc.shape[1] // HD):
        xh = xc[:, hh * HD:(hh + 1) * HD].astype(BF)
        pr.append(_dot(xh, wa_ref[head0 + hh].astype(BF)))
        pi.append(_dot(xh, wx_ref[head0 + hh].astype(BF)))
    r = _sigmoid_t((pr[0] if len(pr) == 1 else jnp.concatenate(pr, axis=1)) + ba)
    ig = _sigmoid_t((pi[0] if len(pi) == 1 else jnp.concatenate(pi, axis=1)) + bx)
    ls = _log_sigmoid(lam)
    log_a = LRU_C * r * ls
    a = jnp.exp(log_a)
    x2 = 2.0 * log_a
    u = a * a
    lu = jnp.log(jnp.maximum(u, 1e-37))
    ratio = x2 * pl.reciprocal(jnp.where(lu == 0.0, 1.0, lu), approx=True)
    em1 = jnp.where(lu == 0.0, x2, jnp.where(u < 1e-30, -1.0, (u - 1.0) * ratio))
    mult = jnp.sqrt(-em1)
    return r, ig, ls, a, mult


def _sgu_mix(vln, ws_ref, bst_ref, tb):
    ri = lax.broadcasted_iota(jnp.int32, (SGU_BLOCK, SGU_BLOCK), 0)
    ci = lax.broadcasted_iota(jnp.int32, (SGU_BLOCK, SGU_BLOCK), 1)
    wm = [jnp.where(ri >= ci, ws_ref[g], 0.0).astype(BF) for g in range(HEADS)]
    blocks = []
    for blk in range(tb // SGU_BLOCK):
        cols = []
        for g in range(HEADS):
            vb = vln[blk * SGU_BLOCK:(blk + 1) * SGU_BLOCK, g * HD:(g + 1) * HD].astype(BF)
            cols.append(_dot(wm[g], vb) + bst_ref[:, g:g + 1])
        blocks.append(jnp.concatenate(cols, axis=1))
    mixed = blocks[0] if len(blocks) == 1 else jnp.concatenate(blocks, axis=0)
    return wm, mixed


def _layernorm_stats(v):
    mu = jnp.mean(v, axis=-1, keepdims=True)
    vc = v - mu
    rstd = lax.rsqrt(jnp.mean(vc * vc, axis=-1, keepdims=True) + EPS)
    return rstd, vc * rstd


def _my_xyc():
    return lax.axis_index("x"), lax.axis_index("y"), lax.axis_index("c")


def _gather_weights(srcs, halve):
    n = len(srcs)
    out_shape = [jax.ShapeDtypeStruct((N_CHIPS,) + s.shape, s.dtype) for s in srcs]

    def body(*refs):
        src, out = refs[:n], refs[n:2 * n]
        send_sems, recv_sems, fwd_send, fwd_recv, loc_sems = refs[2 * n:]
        x, y, c = _my_xyc()
        me = 2 * x + y
        chips = [(1 - x, y), (x, 1 - y), (1 - x, 1 - y)]

        def half(ref, a, which):
            if not halve[a]:
                return ref
            h = srcs[a].shape[0] // 2
            return ref.at[pl.ds(which * h, h)]

        def ici(a, k, frm):
            px, py = chips[k]
            return pltpu.make_async_remote_copy(
                src_ref=half(src[a], a, c), dst_ref=half(out[a].at[frm], a, c),
                send_sem=send_sems.at[a, k], recv_sem=recv_sems.at[a, k],
                device_id=(px, py, c), device_id_type=MESH_ID)

        def d2d(a, k, which):
            px, py = chips[k]
            rows = half(out[a].at[2 * px + py], a, which)
            return pltpu.make_async_remote_copy(
                src_ref=rows, dst_ref=rows, send_sem=fwd_send.at[a, k], recv_sem=fwd_recv.at[a, k],
                device_id=(x, y, 1 - c), device_id_type=MESH_ID)

        local, sends = [], []
        for a in range(n):
            lc = pltpu.make_async_copy(src[a], out[a].at[me], loc_sems.at[a])
            lc.start()
            local.append(lc)
            for k in range(3):
                cp = ici(a, k, me)
                cp.start()
                sends.append(cp)
        for a in range(n):
            for k in range(3):
                px, py = chips[k]
                ici(a, k, 2 * px + py).wait_recv()
                if halve[a]:
                    fw = d2d(a, k, c)
                    fw.start()
                    sends.append(fw)
        for a in range(n):
            if halve[a]:
                for k in range(3):
                    d2d(a, k, 1 - c).wait_recv()
        for cp in sends:
            cp.wait_send()
        for lc in local:
            lc.wait()

    sem = pltpu.SemaphoreType.DMA((n, 3))
    return _pcall(body, name="gather_weights", out_shape=out_shape, in_specs=[HBM_SPEC] * n,
                  out_specs=[HBM_SPEC] * n, scratch=[sem, sem, sem, sem, pltpu.SemaphoreType.DMA((n,))])(*srcs)


SEM_SPEC = pl.BlockSpec(memory_space=pltpu.SEMAPHORE)


def _remote_start(srcs, lands, plan, ncopies, name):
    n, m = len(srcs), len(lands)

    def body(*refs):
        src, land = refs[:n], refs[n:n + m]
        send_sems, recv_sems = refs[n + m], refs[n + m + 1]
        token = refs[-1]
        x, y, c = _my_xyc()
        for i, (s, d, dev) in enumerate(plan(src, land, x, y, c)):
            pltpu.make_async_remote_copy(src_ref=s, dst_ref=d, send_sem=send_sems.at[i], recv_sem=recv_sems.at[i],
                                         device_id=dev, device_id_type=MESH_ID).start()
        token[...] = jnp.zeros_like(token)

    bufs = list(srcs) + list(lands)
    out = pl.pallas_call(
        body, name=name,
        out_shape=(pltpu.SemaphoreType.DMA((ncopies,)), pltpu.SemaphoreType.DMA((ncopies,)),
                   *[pltpu.HBM(b.shape, b.dtype) for b in bufs], jax.ShapeDtypeStruct((SUBLANES, LANES), F32)),
        in_specs=[HBM_SPEC] * (n + m),
        out_specs=(SEM_SPEC, SEM_SPEC, *[HBM_SPEC] * (n + m), pl.BlockSpec(memory_space=pltpu.VMEM)),
        input_output_aliases={i: 2 + i for i in range(n + m)},
        compiler_params=pltpu.CompilerParams(has_side_effects=pltpu.SideEffectType.DATAFLOW_SIDE_EFFECTING),
    )(*[pltpu.with_memory_space_constraint(b, pltpu.HBM) for b in bufs])
    return (out[0], out[1], out[2:2 + n], out[2 + n:2 + n + m]), out[-1]


def _remote_wait(handle, plan, after, name):
    send_sems, recv_sems, srcs, lands = handle
    n, m = len(srcs), len(lands)

    def body(*refs):
        src, land = refs[:n], refs[n:n + m]
        ssem, rsem = refs[n + m], refs[n + m + 1]
        x, y, c = _my_xyc()
        for i, (s, d, dev) in enumerate(plan(src, land, x, y, c)):
            cp = pltpu.make_async_remote_copy(src_ref=s, dst_ref=d, send_sem=ssem.at[i], recv_sem=rsem.at[i],
                                              device_id=dev, device_id_type=MESH_ID)
            cp.wait_send()
            cp.wait_recv()

    bufs = list(srcs) + list(lands)
    out = pl.pallas_call(
        body, name=name, out_shape=tuple(pltpu.HBM(b.shape, b.dtype) for b in bufs),
        in_specs=[HBM_SPEC] * (n + m) + [SEM_SPEC, SEM_SPEC, pl.BlockSpec(memory_space=pl.ANY)],
        out_specs=tuple([HBM_SPEC] * (n + m)), input_output_aliases={i: i for i in range(n + m)},
        compiler_params=pltpu.CompilerParams(has_side_effects=pltpu.SideEffectType.DATAFLOW_SIDE_EFFECTING),
    )(*bufs, send_sems, recv_sems, after)
    return out[:n], out[n:]


def _chips_of(x, y):
    return [(1 - x, y), (x, 1 - y), (1 - x, 1 - y)]


def _gather_plan(count):
    def plan(src, land, x, y, c):
        me = 2 * x + y
        return [(src[a], land[a].at[me], (px, py, c)) for a in range(count) for px, py in _chips_of(x, y)]

    return plan


def _place_own(srcs, lands):
    n = len(srcs)

    def body(*refs):
        src, land, sems = refs[:n], refs[2 * n:3 * n], refs[3 * n]
        me = 2 * lax.axis_index("x") + lax.axis_index("y")
        copies = [pltpu.make_async_copy(src[a], land[a].at[me], sems.at[a]) for a in range(n)]
        for cp in copies:
            cp.start()
        for cp in copies:
            cp.wait()

    return _pcall(body, name="place_own_shards", out_shape=[jax.ShapeDtypeStruct(l.shape, l.dtype) for l in lands],
                  in_specs=[HBM_SPEC] * (2 * n), out_specs=[HBM_SPEC] * n, aliases={n + a: a for a in range(n)},
                  scratch=[pltpu.SemaphoreType.DMA((n,))])(*srcs, *lands)


def _gather8(src, name):
    def body(src_ref, out_ref, send_sems, recv_sems, loc_sem):
        x, y, c = _my_xyc()
        me = 4 * x + 2 * y + c
        lc = pltpu.make_async_copy(src_ref, out_ref.at[me], loc_sem)
        lc.start()
        cps = []
        for k in range(1, N_DEV):
            px = 1 - x if (k >> 2) & 1 else x
            py = 1 - y if (k >> 1) & 1 else y
            pc = 1 - c if k & 1 else c
            cp = pltpu.make_async_remote_copy(
                src_ref=src_ref, dst_ref=out_ref.at[me], send_sem=send_sems.at[k - 1], recv_sem=recv_sems.at[k - 1],
                device_id=(px, py, pc), device_id_type=MESH_ID)
            cp.start()
            cps.append(cp)
        for cp in cps:
            cp.wait()
        lc.wait()

    return _pcall(body, name=name, out_shape=jax.ShapeDtypeStruct((N_DEV,) + src.shape, src.dtype),
                  in_specs=[HBM_SPEC], out_specs=HBM_SPEC,
                  scratch=[pltpu.SemaphoreType.DMA((N_DEV - 1,)), pltpu.SemaphoreType.DMA((N_DEV - 1,)),
                           pltpu.SemaphoreType.DMA])(src)


def _cast_shards(arrs, name, after=None):
    n = len(arrs)
    extra = [] if after is None else [after]

    def body(*refs):
        ins, outs = refs[:n], refs[n + len(extra):]
        for a in range(n):
            outs[a][...] = ins[a][...].astype(BF)

    specs = [pl.BlockSpec((s.shape[0] // 4, s.shape[1]), lambda i: (i, 0)) for s in arrs]
    return _pcall(body, name=name, grid=(4,), in_specs=specs + [pl.BlockSpec(memory_space=pl.ANY)] * len(extra),
                  out_specs=specs, out_shape=[jax.ShapeDtypeStruct(s.shape, BF) for s in arrs])(*arrs, *extra)


def _row_tile(rows, cols):
    t = rows
    while t * cols * 4 > ELEMENTWISE_BLOCK_BYTES and t % (2 * SUBLANES) == 0:
        t //= 2
    return t


def _sum_parts(parts, name):
    p, rows, cols = parts.shape
    tr = _row_tile(rows, cols * p // 2)

    def body(p_ref, o_ref):
        acc = p_ref[0].astype(F32)
        for k in range(1, p):
            acc = acc + p_ref[k].astype(F32)
        o_ref[...] = acc

    return _pcall(body, name=name, grid=(rows // tr,),
                  in_specs=[pl.BlockSpec((p, tr, cols), lambda i: (0, i, 0))],
                  out_specs=pl.BlockSpec((tr, cols), lambda i: (i, 0)),
                  out_shape=jax.ShapeDtypeStruct((rows, cols), F32), vmem_mb=48)(parts)


def _sum_own_and_landed(chip, sums, landed, name):
    _, rows, cols = sums.shape
    tr = _row_tile(rows, 2 * cols)

    def body(chip_ref, own_ref, land_ref, o_ref):
        del chip_ref
        acc = own_ref[0].astype(F32)
        for k in range(3):
            acc = acc + land_ref[k].astype(F32)
        o_ref[...] = acc

    grid_spec = pltpu.PrefetchScalarGridSpec(
        num_scalar_prefetch=1, grid=(rows // tr,),
        in_specs=[pl.BlockSpec((1, tr, cols), lambda i, chip_ref: (chip_ref[0], i, 0)),
                  pl.BlockSpec((3, tr, cols), lambda i, chip_ref: (0, i, 0))],
        out_specs=pl.BlockSpec((tr, cols), lambda i, chip_ref: (i, 0)))
    return _pcall(body, name=name, grid_spec=grid_spec, out_shape=jax.ShapeDtypeStruct((rows, cols), F32),
                  vmem_mb=48)(chip, sums, landed)


def _swap_cores(arrs, name):
    n = len(arrs)

    def body(*refs):
        src, out = refs[:n], refs[n:2 * n]
        send_sems, recv_sems = refs[2 * n:]
        x, y, c = _my_xyc()
        cps = []
        for a in range(n):
            cp = pltpu.make_async_remote_copy(
                src_ref=src[a], dst_ref=out[a], send_sem=send_sems.at[a], recv_sem=recv_sems.at[a],
                device_id=(x, y, 1 - c), device_id_type=MESH_ID)
            cp.start()
            cps.append(cp)
        for cp in cps:
            cp.wait()

    sem = pltpu.SemaphoreType.DMA((n,))
    return _pcall(body, name=name, out_shape=[jax.ShapeDtypeStruct(a.shape, a.dtype) for a in arrs],
                  in_specs=[HBM_SPEC] * n, out_specs=[HBM_SPEC] * n, scratch=[sem, sem])(*arrs)


def _add_pair(a, b, name):
    rows, cols = a.shape
    tr = _row_tile(rows, 2 * cols)

    def body(a_ref, b_ref, o_ref):
        o_ref[...] = a_ref[...] + b_ref[...]

    spec = pl.BlockSpec((tr, cols), lambda i: (i, 0))
    return _pcall(body, name=name, grid=(rows // tr,), in_specs=[spec, spec], out_specs=spec,
                  out_shape=jax.ShapeDtypeStruct((rows, cols), F32))(a, b)


def _sum_chips_in_order(chip, own, landed, name):
    rows, cols = own.shape
    tr = _row_tile(rows, 4 * cols)

    def body(chip_ref, own_ref, land_ref, o_ref):
        me = chip_ref[0]
        acc = None
        for p in range(N_CHIPS):
            q = p ^ me
            k = jnp.where(q == 2, 0, jnp.where(q == 1, 1, 2))
            term = jnp.where(q == 0, own_ref[...], land_ref[k])
            acc = term if acc is None else acc + term
        o_ref[...] = acc

    grid_spec = pltpu.PrefetchScalarGridSpec(
        num_scalar_prefetch=1, grid=(rows // tr,),
        in_specs=[pl.BlockSpec((tr, cols), lambda i, chip_ref: (i, 0)),
                  pl.BlockSpec((3, tr, cols), lambda i, chip_ref: (0, i, 0))],
        out_specs=pl.BlockSpec((tr, cols), lambda i, chip_ref: (i, 0)))
    return _pcall(body, name=name, grid_spec=grid_spec, out_shape=jax.ShapeDtypeStruct((rows, cols), F32))(
        chip, own, landed)


def _bcast_plan(src, land, x, y, c):
    return [(src[0], land[0].at[k], (px, py, c)) for k, (px, py) in enumerate(_chips_of(x, y))]


def _sibling_plan(count):
    def plan(src, land, x, y, c):
        return [(src[a], land[a], (x, y, 1 - c)) for a in range(count)]

    return plan


def _scatter_plan(count):
    def plan(src, land, x, y, c):
        out = []
        for a in range(count):
            for k, (px, py) in enumerate(_chips_of(x, y)):
                out.append((src[a].at[2 * px + py], land[a].at[k], (px, py, c)))
        return out

    return plan


def _adamw_math(w, g, m, v):
    m2 = ADAM_B1 * m + (1.0 - ADAM_B1) * g
    v2 = ADAM_B2 * v + (1.0 - ADAM_B2) * (g * g)
    m_hat = m2 / (1.0 - ADAM_B1 ** ADAM_STEP)
    v_hat = v2 / (1.0 - ADAM_B2 ** ADAM_STEP)
    delta = -ADAM_LR * (m_hat / (jnp.sqrt(v_hat) + ADAM_EPS) + ADAM_WD * w)
    return delta, m2, v2


def _adamw(w, m, v, grads, name):
    rows, cols = w.shape
    tr = _row_tile(rows, cols)
    ng = len(grads)

    def body(*refs):
        w_ref, m_ref, v_ref = refs[:3]
        g = refs[3][...]
        for k in range(1, ng):
            g = g + refs[3 + k][...]
        g_ref, d_ref, m2_ref, v2_ref = refs[3 + ng:]
        delta, m2, v2 = _adamw_math(w_ref[...], g, m_ref[...], v_ref[...])
        g_ref[...] = g
        d_ref[...] = delta
        m2_ref[...] = m2
        v2_ref[...] = v2

    spec = pl.BlockSpec((tr, cols), lambda i: (i, 0))
    return _pcall(body, name=name, grid=(rows // tr,), in_specs=[spec] * (3 + ng), out_specs=[spec] * 4,
                  out_shape=[jax.ShapeDtypeStruct((rows, cols), F32)] * 4, vmem_mb=48)(w, m, v, *grads)


def _ada_adamw(ct, dmod, w, m, v):
    rows, cols = w.shape
    tr = _row_tile(rows, cols)

    def body(ct_ref, dm_ref, w_ref, m_ref, v_ref, g_ref, d_ref, m2_ref, v2_ref):
        cv = ct_ref[...]
        ca = cv * _sigmoid(cv)
        g = ca[:, 0:1] * dm_ref[0:1, :]
        for b in range(1, N_DEV):
            g = g + ca[:, b:b + 1] * dm_ref[b:b + 1, :]
        delta, m2, v2 = _adamw_math(w_ref[...], g, m_ref[...], v_ref[...])
        g_ref[...] = g
        d_ref[...] = delta
        m2_ref[...] = m2
        v2_ref[...] = v2

    spec = pl.BlockSpec((tr, cols), lambda i: (i, 0))
    return _pcall(body, name="ada_adamw", grid=(rows // tr,),
                  in_specs=[pl.BlockSpec((tr, N_DEV), lambda i: (i, 0)), pl.BlockSpec((N_DEV, cols), lambda i: (0, 0)),
                            spec, spec, spec],
                  out_specs=[spec] * 4, out_shape=[jax.ShapeDtypeStruct((rows, cols), F32)] * 4,
                  vmem_mb=48)(ct, dmod, w, m, v)


def _mod_fwd(c_all, w, b):
    cols = w.shape[1]
    tn = cols // 3

    def body(c_ref, w_ref, b_ref, o_ref):
        cv = c_ref[...]
        ca = (cv * _sigmoid(cv)).astype(BF)
        o_ref[...] = _dot(ca, w_ref[...].astype(BF)) + b_ref[...]

    return _pcall(body, name="mod_fwd", grid=(3,),
                  in_specs=[pl.BlockSpec((N_DEV, D), lambda j: (0, 0)), pl.BlockSpec((D, tn), lambda j: (0, j)),
                            pl.BlockSpec((1, tn), lambda j: (0, j))],
                  out_specs=pl.BlockSpec((N_DEV, tn), lambda j: (0, j)),
                  out_shape=jax.ShapeDtypeStruct((N_DEV, cols), F32))(c_all, w, b)


def _resident(shape):
    zeros = (0,) * len(shape)
    return pl.BlockSpec(shape, lambda *_: zeros, pipeline_mode=pl.Buffered(1))


def _mixer_fwd(x, g, scale, shift, gate1, w_in4, cw, cb, wa, ba, wx, bx, lam, lg, lb, ws, bst, wba, wbb, wo,
               tm=256, chunk=256, piece=512):
    T = x.shape[0]
    tm = min(tm, T)
    ns = w_in4.shape[2]
    per = ns // piece

    def body(x_ref, g_ref, sc_ref, sh_ref, g1_ref, w_ref, cw_ref, cb_ref, wa_ref, ba_ref, wx_ref, bx_ref, lam_ref,
             lg_ref, lb_ref, ws_ref, bst_ref, wba_ref, wbb_ref, wo_ref,
             h1_ref, z_ref, hl_ref, yap_ref, ybp_ref, mg_ref, ya_ref, yb_ref, o_ref, x2_ref,
             xc_ref, r_ref, ig_ref, mu_ref, a_ref, prev, hc):
        i = pl.program_id(0)

        @pl.when(i == 0)
        def _():
            prev[...] = jnp.zeros_like(prev)
            hc[...] = jnp.zeros_like(hc)

        xv = x_ref[...]
        _, xh = _rms_stats(xv)
        h = ((xh * g_ref[...]) * (1.0 + sc_ref[...]) + sh_ref[...]).astype(BF)
        h1_ref[...] = h

        def proj(col, width):
            for c0 in range(col, col + width, piece):
                w = min(piece, col + width - c0)
                j, off = c0 // ns, c0 % ns
                z_ref[:, c0:c0 + w] = _dot(h, w_ref[j, :, off:off + w])

        def lru_chunk(c0):
            cs = slice(c0, c0 + chunk)
            xr = z_ref[:, cs]
            pv = prev[:, cs]
            xc = (cb_ref[:, cs] + cw_ref[3:4, cs] * xr + cw_ref[2:3, cs] * _shift_down(xr, pv, 1)
                  + cw_ref[1:2, cs] * _shift_down(xr, pv, 2) + cw_ref[0:1, cs] * _shift_down(xr, pv, 3))
            prev[:, cs] = xr[tm - SUBLANES:tm]
            r, ig, _, a, mult = _lru_gates(xc, wa_ref, ba_ref[:, cs], wx_ref, bx_ref[:, cs], lam_ref[:, cs],
                                           head0=c0 // HD)
            xc_ref[:, cs] = xc.astype(BF)
            r_ref[:, cs] = r.astype(BF)
            ig_ref[:, cs] = ig.astype(BF)
            a_ref[:, cs] = a
            mu_ref[:, cs] = mult.astype(BF)
            a, u = _scan_rows(a, mult * (ig * xc), reverse=False)
            hv = u + a * hc[SUBLANES - 1:SUBLANES, cs]
            hc[:, cs] = hv[tm - SUBLANES:tm]
            hl_ref[:, cs] = hv
            yap_ref[:, cs] = (hv * _gelu(z_ref[:, D + c0:D + c0 + chunk])).astype(BF)

        proj(0, chunk)
        proj(D, chunk)
        for c0 in range(0, D, chunk):
            if c0 + chunk < D:
                proj(c0 + chunk, chunk)
                proj(D + c0 + chunk, chunk)
            else:
                proj(2 * D, 2 * D)
            lru_chunk(c0)
        proj(4 * D, 2 * D)
        _, xhn = _layernorm_stats(_gelu(z_ref[:, 3 * D:4 * D]))
        vln = xhn * lg_ref[...] + lb_ref[...]
        _, mixed = _sgu_mix(vln, ws_ref, bst_ref, tm)
        ybp = (_gelu(z_ref[:, 2 * D:3 * D]) * mixed).astype(BF)
        ybp_ref[...] = ybp
        ya = _dot(yap_ref[...], wba_ref[...])
        yb = _dot(ybp, wbb_ref[...])
        merged = (_sigmoid_t(z_ref[:, 4 * D:5 * D]) * ya + _sigmoid_t(z_ref[:, 5 * D:6 * D]) * yb).astype(BF)
        o = _dot(merged, wo_ref[...])
        x2_ref[...] = xv + g1_ref[...] * o
        mg_ref[...] = merged
        ya_ref[...] = ya.astype(BF)
        yb_ref[...] = yb.astype(BF)
        o_ref[...] = o.astype(BF)

    row = pl.BlockSpec((tm, D), lambda i: (i, 0))
    vec = pl.BlockSpec((1, D), lambda i: (0, 0))
    bf_row = jax.ShapeDtypeStruct((T, D), BF)
    f32_row = jax.ShapeDtypeStruct((T, D), F32)
    return _pcall(body, name="mixer_fwd", grid=(T // tm,),
                  in_specs=[row, vec, vec, vec, vec, _resident(w_in4.shape), _resident(cw.shape), vec,
                            _resident(wa.shape), vec, _resident(wx.shape), vec, vec, vec, vec,
                            _resident(ws.shape), _resident(bst.shape),
                            _resident(wba.shape), _resident(wbb.shape), _resident(wo.shape)],
                  out_specs=[row, pl.BlockSpec((tm, 6 * D), lambda i: (i, 0))] + [row] * 13,
                  out_shape=[bf_row, jax.ShapeDtypeStruct((T, 6 * D), F32), f32_row, bf_row, bf_row, bf_row, bf_row,
                             bf_row, bf_row, f32_row, bf_row, bf_row, bf_row, bf_row, f32_row],
                  scratch=[pltpu.VMEM((SUBLANES, D), F32), pltpu.VMEM((SUBLANES, D), F32)], vmem_mb=60)(
        x, g, scale, shift, gate1, w_in4, cw, cb, wa, ba, wx, bx, lam, lg, lb, ws, bst, wba, wbb, wo)


def _ffn_fwd(x2, g, scale, shift, gate2, gf, w_up4, wd, cw, cb, target, tm=256, chunk=768):
    T = x2.shape[0]
    tm = min(tm, T)
    ns = w_up4.shape[2]
    dff = wd.shape[0]
    nchunk = dff // chunk
    per = ns // chunk

    def body(x2_ref, g_ref, sc_ref, sh_ref, g2_ref, gf_ref, wu_ref, wd_ref, cw_ref, cb_ref, t_ref,
             h2_ref, up_ref, f_ref, ga_ref, vd_ref, loss_ref, dx3_ref, dfo_ref, dgf_ref, dg2_ref, prev):
        i = pl.program_id(0)

        @pl.when(i == 0)
        def _():
            prev[...] = jnp.zeros_like(prev)
            loss_ref[...] = jnp.zeros_like(loss_ref)
            dgf_ref[...] = jnp.zeros_like(dgf_ref)
            dg2_ref[...] = jnp.zeros_like(dg2_ref)

        x2v = x2_ref[...]
        _, xh2 = _rms_stats(x2v)
        h2 = ((xh2 * g_ref[...]) * (1.0 + sc_ref[...]) + sh_ref[...]).astype(BF)
        h2_ref[...] = h2

        def conv(u, col):
            cs = slice(col, col + chunk)
            p = prev[:, cs]
            hid = (cb_ref[:, cs] + cw_ref[2:3, cs] * u + cw_ref[1:2, cs] * _shift_down(u, p, 1)
                   + cw_ref[0:1, cs] * _shift_down(u, p, 2))
            prev[:, cs] = u[tm - SUBLANES:tm]
            up_ref[:, cs] = u.astype(BF)
            return hid

        def up_proj(k):
            off = (k % per) * chunk
            return (_dot(h2, wu_ref[k // per, :, off:off + chunk]),
                    _dot(h2, wu_ref[N_CHIPS // 2 + k // per, :, off:off + chunk]))

        fo = None
        nxt = up_proj(0)
        for k in range(nchunk):
            col = k * chunk
            ua, uv = nxt
            if k + 1 < nchunk:
                nxt = up_proj(k + 1)
            act = conv(ua, col)
            val = conv(uv, dff + col)
            ga, dga = _gelu_and_grad(act)
            fk = (ga * val).astype(BF)
            f_ref[:, col:col + chunk] = fk
            ga_ref[:, col:col + chunk] = ga.astype(BF)
            vd_ref[:, col:col + chunk] = (val * dga).astype(BF)
            part = _dot(fk, wd_ref[col:col + chunk, :])
            fo = part if fo is None else fo + part

        x3 = x2v + g2_ref[...] * fo
        rstd, xh = _rms_stats(x3)
        err = xh * gf_ref[...] - t_ref[...]
        loss_ref[...] += 0.5 * jnp.sum(jnp.mean(err * err, axis=-1, keepdims=True), axis=0, keepdims=True)
        dy = err * (1.0 / D)
        dgf_ref[...] += _colsum(dy * xh)
        dxh = dy * gf_ref[...]
        dx3 = rstd * (dxh - xh * jnp.mean(dxh * xh, axis=-1, keepdims=True))
        dg2_ref[...] += _colsum(dx3 * fo)
        dx3_ref[...] = dx3
        dfo_ref[...] = (g2_ref[...] * dx3).astype(BF)

    row = pl.BlockSpec((tm, D), lambda i: (i, 0))
    vec = pl.BlockSpec((1, D), lambda i: (0, 0))
    wide = pl.BlockSpec((tm, 2 * dff), lambda i: (i, 0))
    half = pl.BlockSpec((tm, dff), lambda i: (i, 0))
    return _pcall(body, name="ffn_fwd", grid=(T // tm,),
                  in_specs=[row, vec, vec, vec, vec, vec, _resident(w_up4.shape), _resident(wd.shape),
                            _resident(cw.shape), _resident(cb.shape), row],
                  out_specs=[row, wide, half, half, half, pl.BlockSpec((1, LANES), lambda i: (0, 0)), row, row, vec, vec],
                  out_shape=[jax.ShapeDtypeStruct((T, D), BF), jax.ShapeDtypeStruct((T, 2 * dff), BF),
                             jax.ShapeDtypeStruct((T, dff), BF), jax.ShapeDtypeStruct((T, dff), BF),
                             jax.ShapeDtypeStruct((T, dff), BF), jax.ShapeDtypeStruct((1, LANES), F32),
                             jax.ShapeDtypeStruct((T, D), F32), jax.ShapeDtypeStruct((T, D), BF),
                             jax.ShapeDtypeStruct((1, D), F32), jax.ShapeDtypeStruct((1, D), F32)],
                  scratch=[pltpu.VMEM((SUBLANES, 2 * dff), F32)], vmem_mb=56)(
        x2, g, scale, shift, gate2, gf, w_up4, wd, cw, cb, target)


def _ffn_bwd(dfo, wd, up, ga, vd, cw, w_up4, x2, resid, g, scale, gate, o, tm=256, chunk=1536):
    T = up.shape[0]
    tm = min(tm, T)
    dff = wd.shape[0]
    ns = w_up4.shape[2]
    nchunk = dff // chunk
    per = ns // chunk
    nrow = T // tm

    def body(dfo_ref, wd_ref, up_ref, ga_ref, vd_ref, cw_ref, wu_ref, x_ref, r_ref, g_ref, sc_ref, gt_ref, o_ref,
             du_ref, dcw_ref, dcb_ref, dx_ref, dsh_ref, dsc_ref, dg_ref, do_ref, dgt_ref, nxt):
        i = pl.program_id(0)

        @pl.when(i == 0)
        def _():
            nxt[...] = jnp.zeros_like(nxt)
            for ref in (dcw_ref, dcb_ref, dsh_ref, dsc_ref, dg_ref, dgt_ref):
                ref[...] = jnp.zeros_like(ref)

        dfo_t = dfo_ref[...]

        def conv_bwd(dh, col):
            cs = slice(col, col + chunk)
            n8 = nxt[:, cs]
            dh1 = _shift_up(dh, n8, 1)
            dh2 = _shift_up(dh, n8, 2)
            nxt[:, cs] = dh[0:SUBLANES]
            du = (cw_ref[2:3, cs] * dh + cw_ref[1:2, cs] * dh1 + cw_ref[0:1, cs] * dh2).astype(BF)
            du_ref[:, cs] = du
            u = up_ref[:, cs].astype(F32)
            dcw_ref[2:3, cs] += _colsum(dh * u)
            dcw_ref[1:2, cs] += _colsum(dh1 * u)
            dcw_ref[0:1, cs] += _colsum(dh2 * u)
            dcb_ref[:, cs] += _colsum(dh)
            return du

        def down_bwd(k):
            return _dot_nt(dfo_t, wd_ref[k * chunk:(k + 1) * chunk, :])

        dh = None
        df_next = down_bwd(0)
        for k in range(nchunk):
            col = k * chunk
            off = (k % per) * chunk
            df = df_next
            if k + 1 < nchunk:
                df_next = down_bwd(k + 1)
            du_a = conv_bwd(df * vd_ref[:, col:col + chunk].astype(F32), col)
            du_v = conv_bwd(df * ga_ref[:, col:col + chunk].astype(F32), dff + col)
            part = (_dot_nt(du_a, wu_ref[k // per, :, off:off + chunk])
                    + _dot_nt(du_v, wu_ref[N_CHIPS // 2 + k // per, :, off:off + chunk]))
            dh = part if dh is None else dh + part

        rstd, xh = _rms_stats(x_ref[...])
        dsh_ref[...] += _colsum(dh)
        dsc_ref[...] += _colsum(dh * (xh * g_ref[...]))
        dn = dh * (1.0 + sc_ref[...])
        dg_ref[...] += _colsum(dn * xh)
        dxh = dn * g_ref[...]
        dx = r_ref[...] + rstd * (dxh - xh * jnp.mean(dxh * xh, axis=-1, keepdims=True))
        dx_ref[...] = dx
        do_ref[...] = (gt_ref[...] * dx).astype(BF)
        dgt_ref[...] += _colsum(dx * o_ref[...].astype(F32))

    rev = lambda i: (nrow - 1 - i, 0)
    row = pl.BlockSpec((tm, D), rev)
    vec = pl.BlockSpec((1, D), lambda i: (0, 0))
    wide = pl.BlockSpec((tm, 2 * dff), rev)
    half = pl.BlockSpec((tm, dff), rev)
    cw3 = pl.BlockSpec((3, 2 * dff), lambda i: (0, 0))
    cb1 = pl.BlockSpec((1, 2 * dff), lambda i: (0, 0))
    vshape = jax.ShapeDtypeStruct((1, D), F32)
    return _pcall(body, name="ffn_bwd", grid=(nrow,),
                  in_specs=[row, _resident(wd.shape), wide, half, half, _resident(cw.shape), _resident(w_up4.shape),
                            row, row, vec, vec, vec, row],
                  out_specs=[wide, cw3, cb1, row, vec, vec, vec, row, vec],
                  out_shape=[jax.ShapeDtypeStruct((T, 2 * dff), BF), jax.ShapeDtypeStruct((3, 2 * dff), F32),
                             jax.ShapeDtypeStruct((1, 2 * dff), F32), jax.ShapeDtypeStruct((T, D), F32),
                             vshape, vshape, vshape, jax.ShapeDtypeStruct((T, D), BF), vshape],
                  scratch=[pltpu.VMEM((SUBLANES, 2 * dff), F32)], vmem_mb=60)(
        dfo, wd, up, ga, vd, cw, w_up4, x2, resid, g, scale, gate, o)


def _mm_tn_cols(a, b, name, nshard, nb, mb=None, tm=TN_ROWS):
    T, M = a.shape
    tm = min(tm, T)
    mb = M if mb is None else mb
    ns = b.shape[1] // nshard
    per = ns // nb
    nk = T // tm
    vmem_mb = (2 * 2 * tm * (mb + nb) + 2 * (4 + 2) * mb * nb) // 2 ** 20 + 8

    def body(a_ref, b_ref, o_ref, c_ref):
        k = pl.program_id(2)

        @pl.when(k == 0)
        def _():
            o_ref[...] = jnp.zeros_like(o_ref)

        o_ref[0] += _dot_tn(a_ref[...], b_ref[...])

        @pl.when(k == nk - 1)
        def _():
            c_ref[...] = o_ref[...].astype(BF)

    out_spec = pl.BlockSpec((1, mb, nb), lambda m, t, k: (t // per, m, t % per))
    return _pcall(body, name=name, grid=(M // mb, nshard * per, nk),
                  in_specs=[pl.BlockSpec((tm, mb), lambda m, t, k: (k, m)),
                            pl.BlockSpec((tm, nb), lambda m, t, k: (k, t))],
                  out_specs=[out_spec, out_spec],
                  out_shape=[jax.ShapeDtypeStruct((nshard, M, ns), F32), jax.ShapeDtypeStruct((nshard, M, ns), BF)],
                  vmem_mb=vmem_mb)(a, b)


def _mm_nt_normbwd(dz, w4, x, resid, g, scale, name, tm=256):
    T = x.shape[0]
    tm = min(tm, T)
    ns = w4.shape[2]

    def body(dz_ref, w_ref, x_ref, r_ref, g_ref, sc_ref, dx_ref, dsh_ref, dsc_ref, dg_ref):
        i = pl.program_id(0)

        @pl.when(i == 0)
        def _():
            dsh_ref[...] = jnp.zeros_like(dsh_ref)
            dsc_ref[...] = jnp.zeros_like(dsc_ref)
            dg_ref[...] = jnp.zeros_like(dg_ref)

        dh = None
        for j in range(N_CHIPS):
            part = _dot_nt(dz_ref[:, j * ns:(j + 1) * ns], w_ref[j])
            dh = part if dh is None else dh + part
        rstd, xh = _rms_stats(x_ref[...])
        dsh_ref[...] += _colsum(dh)
        dsc_ref[...] += _colsum(dh * (xh * g_ref[...]))
        dn = dh * (1.0 + sc_ref[...])
        dg_ref[...] += _colsum(dn * xh)
        dxh = dn * g_ref[...]
        dx_ref[...] = r_ref[...] + rstd * (dxh - xh * jnp.mean(dxh * xh, axis=-1, keepdims=True))

    row = pl.BlockSpec((tm, D), lambda i: (i, 0))
    vec = pl.BlockSpec((1, D), lambda i: (0, 0))
    return _pcall(body, name=name, grid=(T // tm,),
                  in_specs=[pl.BlockSpec((tm, N_CHIPS * ns), lambda i: (i, 0)), _resident(w4.shape), row, row, vec, vec],
                  out_specs=[row, vec, vec, vec],
                  out_shape=[jax.ShapeDtypeStruct((T, D), F32)] + [jax.ShapeDtypeStruct((1, D), F32)] * 3,
                  vmem_mb=48)(dz, w4, x, resid, g, scale)


def _mix_bwd(do, ya, yb, z, wo, wba, wbb, tm=256):
    T = do.shape[0]
    tm = min(tm, T)

    def body(do_ref, ya_ref, yb_ref, ga_ref, gb_ref, wo_ref, wa_ref, wb_ref,
             dz_ref, dya_ref, dyb_ref, dyap_ref, dybp_ref):
        dm = _dot_nt(do_ref[...], wo_ref[...])
        sa = _sigmoid_t(ga_ref[...])
        sb = _sigmoid_t(gb_ref[...])
        dya = (sa * dm).astype(BF)
        dyb = (sb * dm).astype(BF)
        dz_ref[:, 0:D] = (dm * ya_ref[...].astype(F32) * sa * (1.0 - sa)).astype(BF)
        dz_ref[:, D:2 * D] = (dm * yb_ref[...].astype(F32) * sb * (1.0 - sb)).astype(BF)
        dya_ref[...] = dya
        dyb_ref[...] = dyb
        dyap_ref[...] = _dot_nt(dya, wa_ref[...]).astype(BF)
        dybp_ref[...] = _dot_nt(dyb, wb_ref[...]).astype(BF)

    row = pl.BlockSpec((tm, D), lambda i: (i, 0))
    wspec = pl.BlockSpec((D, D), lambda i: (0, 0))
    return _pcall(body, name="mix_bwd", grid=(T // tm,),
                  in_specs=[row, row, row, pl.BlockSpec((tm, D), lambda i: (i, 4)),
                            pl.BlockSpec((tm, D), lambda i: (i, 5)), wspec, wspec, wspec],
                  out_specs=[pl.BlockSpec((tm, 2 * D), lambda i: (i, 2)), row, row, row, row],
                  out_shape=[jax.ShapeDtypeStruct((T, 6 * D), BF)] + [jax.ShapeDtypeStruct((T, D), BF)] * 4,
                  vmem_mb=48)(do, ya, yb, z, z, wo, wba, wbb)


def _sgu_bwd(dz, dyb_pre, z, lg, lb, ws, bst, tb=256):
    T = z.shape[0]
    tb = min(tb, T)

    def body(dz_in, dy_ref, zu_ref, zv_ref, lg_ref, lb_ref, ws_ref, bst_ref,
             dz_ref, dws_ref, dbst_ref, dlg_ref, dlb_ref):
        del dz_in
        i = pl.program_id(0)

        @pl.when(i == 0)
        def _():
            dws_ref[...] = jnp.zeros_like(dws_ref)
            dbst_ref[...] = jnp.zeros_like(dbst_ref)
            dlg_ref[...] = jnp.zeros_like(dlg_ref)
            dlb_ref[...] = jnp.zeros_like(dlb_ref)

        gu, dgu = _gelu_and_grad(zu_ref[...])
        gv, dgv = _gelu_and_grad(zv_ref[...])
        rstd, xh = _layernorm_stats(gv)
        vln = xh * lg_ref[...] + lb_ref[...]
        wm, mixed = _sgu_mix(vln, ws_ref, bst_ref, tb)
        dy = dy_ref[...].astype(F32)
        dz_ref[:, 0:D] = (dy * mixed * dgu).astype(BF)
        dmixed = dy * gu
        ri = lax.broadcasted_iota(jnp.int32, (SGU_BLOCK, SGU_BLOCK), 0)
        ci = lax.broadcasted_iota(jnp.int32, (SGU_BLOCK, SGU_BLOCK), 1)
        blocks = []
        for blk in range(tb // SGU_BLOCK):
            rs = slice(blk * SGU_BLOCK, (blk + 1) * SGU_BLOCK)
            cols = []
            for g in range(HEADS):
                cs = slice(g * HD, (g + 1) * HD)
                dmg = dmixed[rs, cs]
                dmb = dmg.astype(BF)
                dbst_ref[:, g:g + 1] += jnp.sum(dmg, axis=1, keepdims=True)
                dws_ref[g] += jnp.where(ri >= ci, _dot_nt(dmb, vln[rs, cs].astype(BF)), 0.0)
                cols.append(_dot_tn(wm[g], dmb))
            blocks.append(jnp.concatenate(cols, axis=1))
        dvln = blocks[0] if len(blocks) == 1 else jnp.concatenate(blocks, axis=0)
        dlg_ref[...] += _colsum(dvln * xh)
        dlb_ref[...] += _colsum(dvln)
        dxh = dvln * lg_ref[...]
        dgv_in = rstd * (dxh - jnp.mean(dxh, axis=-1, keepdims=True)
                         - xh * jnp.mean(dxh * xh, axis=-1, keepdims=True))
        dz_ref[:, D:2 * D] = (dgv_in * dgv).astype(BF)

    row = pl.BlockSpec((tb, D), lambda i: (i, 0))
    vec = pl.BlockSpec((1, D), lambda i: (0, 0))
    wspec = pl.BlockSpec((HEADS, SGU_BLOCK, SGU_BLOCK), lambda i: (0, 0, 0))
    bspec = pl.BlockSpec((SGU_BLOCK, HEADS), lambda i: (0, 0))
    return _pcall(body, name="sgu_bwd", grid=(T // tb,),
                  in_specs=[HBM_SPEC, row, pl.BlockSpec((tb, D), lambda i: (i, 2)),
                            pl.BlockSpec((tb, D), lambda i: (i, 3)), vec, vec, wspec, bspec],
                  out_specs=[pl.BlockSpec((tb, 2 * D), lambda i: (i, 1)), wspec, bspec, vec, vec],
                  out_shape=[jax.ShapeDtypeStruct(dz.shape, BF),
                             jax.ShapeDtypeStruct((HEADS, SGU_BLOCK, SGU_BLOCK), F32),
                             jax.ShapeDtypeStruct((SGU_BLOCK, HEADS), F32),
                             jax.ShapeDtypeStruct((1, D), F32), jax.ShapeDtypeStruct((1, D), F32)],
                  aliases={0: 0}, vmem_mb=48)(dz, dyb_pre, z, z, lg, lb, ws, bst)


def _rglru_bwd(dz, dya_pre, z, h, xc_s, r_s, ig_s, mult_s, a_s, cw, wa, wx, lam, tb=256):
    T = z.shape[0]
    tb = min(tb, T)
    nrow = T // tb
    per = tb // SUBLANES

    def body(dz_in, dy_ref, xr_ref, gr_ref, h_ref, hh_ref, xc_ref, r_ref, ig_ref, mu_ref, a_ref, cw_ref, wa_ref,
             wx_ref, lam_ref, dz_ref, dcw_ref, dcb_ref, dwa_ref, dba_ref, dwx_ref, dbx_ref, dlam_ref, carry, nxt):
        del dz_in
        i = pl.program_id(0)
        first_block = i == nrow - 1

        @pl.when(i == 0)
        def _():
            carry[...] = jnp.zeros_like(carry)
            nxt[...] = jnp.zeros_like(nxt)
            for ref in (dcw_ref, dcb_ref, dwa_ref, dba_ref, dwx_ref, dbx_ref, dlam_ref):
                ref[...] = jnp.zeros_like(ref)

        xc = xc_ref[...].astype(F32)
        r = r_ref[...].astype(F32)
        ig = ig_ref[...].astype(F32)
        mult = mu_ref[...].astype(F32)
        a = a_ref[...]
        lam = lam_ref[...]
        ls = _log_sigmoid(lam)
        hv = h_ref[...]
        hprev = _shift_down(hv, jnp.where(first_block, 0.0, hh_ref[...]), 1)
        gg, dgg = _gelu_and_grad(gr_ref[...])
        dy = dy_ref[...].astype(F32)
        dz_ref[:, D:2 * D] = (dy * hv * dgg).astype(BF)

        rows = lax.broadcasted_iota(jnp.int32, (tb, D), 0)
        v = dy * gg + jnp.where(rows == tb - 1, carry[0:1, :], 0.0)
        q = jnp.where(rows < tb - 1, pltpu.roll(a, tb - 1, 0), 0.0)
        _, gsc = _scan_rows(q, v, reverse=True)
        carry[...] = (a * gsc)[0:SUBLANES]

        xi = ig * xc
        dmult = gsc * xi
        dxi = gsc * mult
        dig = dxi * xc
        dxc = dxi * ig
        dlog_a = gsc * hprev * a - dmult * (a * a) * pl.reciprocal(mult, approx=True)
        dlam_ref[...] += _colsum(dlog_a * r) * (LRU_C * _sigmoid(-lam))
        dpr = dlog_a * (LRU_C * ls) * r * (1.0 - r)
        dpi = dig * ig * (1.0 - ig)
        dba_ref[...] += _colsum(dpr)
        dbx_ref[...] += _colsum(dpi)
        back = []
        for hh in range(HEADS):
            cs = slice(hh * HD, (hh + 1) * HD)
            xh = xc[:, cs].astype(BF)
            dprh = dpr[:, cs].astype(BF)
            dpih = dpi[:, cs].astype(BF)
            dwa_ref[hh] += _dot_tn(xh, dprh)
            dwx_ref[hh] += _dot_tn(xh, dpih)
            back.append(_dot_nt(dprh, wa_ref[hh].astype(BF)) + _dot_nt(dpih, wx_ref[hh].astype(BF)))
        dxc = dxc + jnp.concatenate(back, axis=1)

        n8 = nxt[...]
        d1 = _shift_up(dxc, n8, 1)
        d2 = _shift_up(dxc, n8, 2)
        d3 = _shift_up(dxc, n8, 3)
        nxt[...] = dxc[0:SUBLANES]
        dz_ref[:, 0:D] = (cw_ref[3:4, :] * dxc + cw_ref[2:3, :] * d1 + cw_ref[1:2, :] * d2
                          + cw_ref[0:1, :] * d3).astype(BF)
        xr = xr_ref[...]
        dcw_ref[3:4, :] += _colsum(dxc * xr)
        dcw_ref[2:3, :] += _colsum(d1 * xr)
        dcw_ref[1:2, :] += _colsum(d2 * xr)
        dcw_ref[0:1, :] += _colsum(d3 * xr)
        dcb_ref[...] += _colsum(dxc)

    rev = lambda col: (lambda i: (nrow - 1 - i, col))
    row = pl.BlockSpec((tb, D), rev(0))
    halo = pl.BlockSpec((SUBLANES, D), lambda i: (jnp.maximum((nrow - 1 - i) * per - 1, 0), 0))
    vec = pl.BlockSpec((1, D), lambda i: (0, 0))
    wspec = pl.BlockSpec((HEADS, HD, HD), lambda i: (0, 0, 0))
    c4 = pl.BlockSpec((4, D), lambda i: (0, 0))
    wshape = jax.ShapeDtypeStruct((HEADS, HD, HD), F32)
    vshape = jax.ShapeDtypeStruct((1, D), F32)
    return _pcall(body, name="rglru_bwd", grid=(nrow,),
                  in_specs=[HBM_SPEC, row, row, pl.BlockSpec((tb, D), rev(1)), row, halo,
                            row, row, row, row, row, c4, wspec, wspec, vec],
                  out_specs=[pl.BlockSpec((tb, 2 * D), rev(0)), c4, vec, wspec, vec, wspec, vec, vec],
                  out_shape=[jax.ShapeDtypeStruct(dz.shape, BF), jax.ShapeDtypeStruct((4, D), F32), vshape,
                             wshape, vshape, wshape, vshape, vshape],
                  scratch=[pltpu.VMEM((SUBLANES, D), F32), pltpu.VMEM((SUBLANES, D), F32)],
                  aliases={0: 0}, vmem_mb=56)(dz, dya_pre, z, z, h, h, xc_s, r_s, ig_s, mult_s, a_s, cw, wa, wx, lam)


def _pack_rows(parts):
    out = []
    for p in parts:
        q = p.reshape(-1, LANES)
        pad = (-q.shape[0]) % SUBLANES
        if pad:
            q = jnp.concatenate([q, jnp.zeros((pad, LANES), q.dtype)], axis=0)
        out.append(q)
    return jnp.concatenate(out, axis=0)


def _rows_of(shape):
    n = 1
    for s in shape:
        n *= s
    rows = n // LANES
    return rows + (-rows) % SUBLANES


def kernel(x, c, w_ada, b_ada, norm_mix_g, w_in, rnn_conv_w, rnn_conv_b, lru_w_a, lru_b_a, lru_w_x, lru_b_x, lru_lambda, sgu_ln_g, sgu_ln_b, sgu_w_s, sgu_b_s, w_branch_a, w_branch_b, w_out, norm_ffn_g, w_up, ffn_conv_w, ffn_conv_b, w_down, norm_final_g, loss_target, m_w_ada, m_b_ada, m_norm_mix_g, m_w_in, m_rnn_conv_w, m_rnn_conv_b, m_lru_w_a, m_lru_b_a, m_lru_w_x, m_lru_b_x, m_lru_lambda, m_sgu_ln_g, m_sgu_ln_b, m_sgu_w_s, m_sgu_b_s, m_w_branch_a, m_w_branch_b, m_w_out, m_norm_ffn_g, m_w_up, m_ffn_conv_w, m_ffn_conv_b, m_w_down, m_norm_final_g, v_w_ada, v_b_ada, v_norm_mix_g, v_w_in, v_rnn_conv_w, v_rnn_conv_b, v_lru_w_a, v_lru_b_a, v_lru_w_x, v_lru_b_x, v_lru_lambda, v_sgu_ln_g, v_sgu_ln_b, v_sgu_w_s, v_sgu_b_s, v_w_branch_a, v_w_branch_b, v_w_out, v_norm_ffn_g, v_w_up, v_ffn_conv_w, v_ffn_conv_b, v_w_down, v_norm_final_g):
    args = dict(locals())
    T = x.shape[1]
    mx, my, mc = lax.axis_index("x"), lax.axis_index("y"), lax.axis_index("c")
    chip = 2 * mx + my
    dev = 2 * chip + mc
    vec = lambda a: a.reshape(1, -1)

    xt = x.reshape(T, D)
    tgt = loss_target.reshape(T, D)
    ns = w_in.shape[2]
    dff = w_down.shape[1] * N_CHIPS

    c_all = _gather8(c.reshape(SUBLANES, LANES), "gather_c").reshape(N_DEV, D)
    b_ada_sh = lax.dynamic_slice(b_ada, (0, chip * ns), (1, ns))
    mod_sh = _mod_fwd(c_all, w_ada[0], b_ada_sh)

    mixer_w = _cast_shards([w_in[0], w_branch_a[0], w_branch_b[0], w_out[0]], "cast_mixer_weights")
    w_in4, wba4, wbb4, wo4, rcw4, fcw4, mod4 = _gather_weights(
        list(mixer_w) + [rnn_conv_w[0], ffn_conv_w[0], mod_sh], [True] * 4 + [False] * 3)
    late = _cast_shards([w_up[0], w_down[0]], "cast_late", after=mod4)
    late_plan = _gather_plan(len(late))
    late_handle, late_token = _remote_start(
        late, [lax.empty((N_CHIPS,) + w.shape, w.dtype) for w in late], late_plan, 3 * len(late), "gather_late_start")
    rcw_full = jnp.transpose(rcw4, (1, 0, 2)).reshape(4, D)
    fcw_full = jnp.transpose(fcw4, (1, 0, 2)).reshape(3, 2 * dff)
    mod = lax.dynamic_index_in_dim(mod4, dev, axis=1, keepdims=False).reshape(1, 6 * D)
    shift1, scale1, gate1, shift2, scale2, gate2 = [mod[:, k * D:(k + 1) * D] for k in range(6)]

    bst = jnp.transpose(sgu_b_s[0])
    wba_full = wba4.reshape(D, D)
    wbb_full = wbb4.reshape(D, D)
    wo_full = wo4.reshape(D, D)
    h1, z, h_lru, ya_pre, yb_pre, merged, ya, yb, o1, x2, lru_xc, lru_r, lru_i, lru_mult, lru_a = _mixer_fwd(
        xt, norm_mix_g, scale1 + late_token[0:1, 0:1], shift1, gate1, w_in4, rcw_full, rnn_conv_b,
        lru_w_a[0], lru_b_a, lru_w_x[0], lru_b_x, lru_lambda, sgu_ln_g, sgu_ln_b, sgu_w_s[0], bst,
        wba_full, wbb_full, wo_full)
    late, late_lands = _remote_wait(late_handle, late_plan, o1, "gather_late_wait")
    w_up4, w_down4 = _place_own(late, late_lands)
    wd_full = w_down4.reshape(dff, D)
    h2, up, f, ffn_ga, ffn_vd, loss_part, dx3, dfo, dgf, dgate2 = _ffn_fwd(
        x2, norm_ffn_g, scale2, shift2, gate2, vec(norm_final_g), w_up4, wd_full, fcw_full, ffn_conv_b, tgt)

    dup, dfcw, dfcb, dx2, dshift2, dscale2, dg_ffn, do1, dgate1 = _ffn_bwd(
        dfo, wd_full, up, ffn_ga, ffn_vd, fcw_full, w_up4, x2, dx3, norm_ffn_g, scale2, gate1, o1)
    dwd = _mm_tn_cols(f, dfo, "dw_down", 1, D, mb=D, tm=TN_ROWS_SQUARE)
    dw_up4 = _mm_tn_cols(h2, dup, "dw_up", N_CHIPS, ns)
    dz, dya, dyb, dya_pre, dyb_pre = _mix_bwd(do1, ya, yb, z, wo_full, wba_full, wbb_full)
    dwo = _mm_tn_cols(merged, do1, "dw_out", 1, D)
    dwba = _mm_tn_cols(ya_pre, dya, "dw_branch_a", 1, D)
    dwbb = _mm_tn_cols(yb_pre, dyb, "dw_branch_b", 1, D)

    chip_id = chip.astype(jnp.int32).reshape(1)

    def reduce_start(group, name):
        wire = [g16.reshape(N_CHIPS, -1, g16.shape[-1]) for _, (_, g16) in group]
        lands = [lax.empty((3,) + w.shape[1:], w.dtype) for w in wire]
        return _remote_start(wire, lands, _scatter_plan(len(group)), 3 * len(group), "scatter_start_" + name)

    def reduce_finish(group, handle, after, name):
        _, landed = _remote_wait(handle, _scatter_plan(len(group)), after, "scatter_wait_" + name)
        return [_sum_own_and_landed(chip_id, g32.reshape(N_CHIPS, -1, g32.shape[-1]), l, "sum_chips_" + n)
                for (n, (g32, _)), l in zip(group, landed)]

    group1 = [("w_up", dw_up4), ("w_down", dwd), ("w_branch_a", dwba), ("w_branch_b", dwbb), ("w_out", dwo)]
    handle1, token1 = reduce_start(group1, "late")
    dz, dws, dbst, dlg, dlb = _sgu_bwd(dz, dyb_pre, z, sgu_ln_g + token1[0:1, 0:1], sgu_ln_b, sgu_w_s[0], bst)
    dz, drcw, drcb, dwa, dba, dwx, dbx, dlam = _rglru_bwd(
        dz, dya_pre, z, h_lru, lru_xc, lru_r, lru_i, lru_mult, lru_a, rcw_full, lru_w_a[0], lru_w_x[0], lru_lambda)
    early_small = [("rnn_conv_b", drcb), ("lru_w_a", dwa), ("lru_b_a", dba), ("lru_w_x", dwx), ("lru_b_x", dbx),
                   ("lru_lambda", dlam), ("sgu_ln_g", dlg), ("sgu_ln_b", dlb), ("sgu_w_s", dws),
                   ("sgu_b_s", jnp.transpose(dbst)), ("norm_ffn_g", dg_ffn),
                   ("ffn_conv_b", dfcb), ("norm_final_g", dgf)]
    r_early = sum(_rows_of(args[n].shape) for n, _ in early_small)
    early_pack = _pack_rows([g for _, g in early_small] + [drcw, dfcw])
    early_pack = jnp.concatenate(
        [early_pack, jnp.zeros(((-early_pack.shape[0]) % 256, LANES), F32)], axis=0)
    early_chip = _add_pair(early_pack, _swap_cores([early_pack], "swap_small_grads")[0], "sum_cores_small_grads")
    early_handle, token3 = _remote_start([early_chip], [lax.empty((3,) + early_chip.shape, F32)], _bcast_plan, 3,
                                         "small_grads_start")
    def swap_start(totals, name):
        lands = [lax.empty(t.shape, t.dtype) for t in totals]
        return _remote_start(totals, lands, _sibling_plan(len(totals)), len(totals), "swap_sums_start_" + name)

    out = {}

    def swap_finish(group, handle, after, name):
        mine, theirs = _remote_wait(handle, _sibling_plan(len(group)), after, "swap_sums_wait_" + name)
        for (n, _), a, b in zip(group, mine, theirs):
            shape = args[n].shape
            res = _adamw(args[n][0], args["m_" + n][0], args["v_" + n][0], [a, b], "adamw_" + n)
            for kind, r in zip(("grad_", "delta_", "new_m_", "new_v_"), res):
                out[kind + n] = r.reshape(shape)
        return res[3]

    swap1, token4 = swap_start(reduce_finish(group1, handle1, drcb, "late"), "late")
    group2 = [("w_in", _mm_tn_cols(h1, dz, "dw_in", N_CHIPS, ns))]
    handle2, token2 = reduce_start(group2, "in")
    tokens = token2[0:1, 0:1] + token3[0:1, 0:1] + token4[0:1, 0:1]
    grad_x, dshift1, dscale1, dg_mix = _mm_nt_normbwd(
        dz, w_in4, xt, dx2, norm_mix_g + tokens, scale1, "dh1_norm_bwd")
    swap2, token5 = swap_start(reduce_finish(group2, handle2, dg_mix, "in"), "in")
    dmod = jnp.concatenate([dshift1, dscale1, dgate1, dshift2, dscale2, dgate2], axis=1)
    last = swap_finish(group1, swap1, token5, "late")
    swap_finish(group2, swap2, last, "in")

    late_small = [("b_ada", dmod), ("norm_mix_g", dg_mix)]
    small = late_small + early_small
    late_all = _gather8(_pack_rows([g for _, g in late_small] + [loss_part]), "gather_late_small_grads")
    late_sum = _sum_parts(late_all, "sum_late_small_grads")
    r_late = sum(_rows_of(args[n].shape) for n, _ in late_small)
    loss = late_sum[r_late, 0]
    late_sum = late_sum[:r_late]
    _, (early_landed,) = _remote_wait(early_handle, _bcast_plan, dg_mix, "small_grads_wait")
    early_sum = _sum_chips_in_order(chip_id, early_chip, early_landed, "sum_early_small_grads")
    r_small = sum(_rows_of(args[n].shape) for n, _ in small)
    r_pad = r_small + (-r_small) % 256
    fill = jnp.zeros((r_pad - r_small, LANES), F32)
    g_small = jnp.concatenate([late_sum, early_sum[:r_early], fill], axis=0)

    def pack_small(prefix):
        return jnp.concatenate([_pack_rows([args[prefix + n] for n, _ in small]), fill], axis=0)

    res = _adamw(pack_small(""), pack_small("m_"), pack_small("v_"), [g_small], "adamw_small")
    off = 0
    for n, _ in small:
        shape = args[n].shape
        rows = _rows_of(shape)
        for kind, r in zip(("grad_", "delta_", "new_m_", "new_v_"), res):
            out[kind + n] = r[off:off + rows].reshape(shape)
        off += rows

    rcw_cols = rnn_conv_w.shape[2]
    g_rcw = lax.dynamic_slice(early_sum[r_early:r_early + 32].reshape(4, D), (0, chip * rcw_cols), (4, rcw_cols))
    g_fcw = lax.dynamic_slice(early_sum[r_early + 32:r_early + 32 + 144].reshape(3, 2 * dff), (0, chip * ns), (3, ns))
    conv = [("rnn_conv_w", g_rcw), ("ffn_conv_w", g_fcw)]
    res = _adamw(_pack_rows([args[n] for n, _ in conv]), _pack_rows([args["m_" + n] for n, _ in conv]),
                 _pack_rows([args["v_" + n] for n, _ in conv]), [_pack_rows([g for _, g in conv])], "adamw_conv")
    off = 0
    for n, _ in conv:
        shape = args[n].shape
        cnt = shape[1] * shape[2] // LANES
        for kind, r in zip(("grad_", "delta_", "new_m_", "new_v_"), res):
            out[kind + n] = r[off:off + cnt].reshape(shape)
        off += _rows_of(shape)

    dmod_all = late_all[:, 0:6 * D // LANES, :].reshape(N_DEV, 6 * D)
    dmod_sh = lax.dynamic_slice(dmod_all, (0, chip * ns), (N_DEV, ns))
    res = _ada_adamw(jnp.transpose(c_all), dmod_sh, w_ada[0], m_w_ada[0], v_w_ada[0])
    for kind, r in zip(("grad_", "delta_", "new_m_", "new_v_"), res):
        out[kind + "w_ada"] = r.reshape(w_ada.shape)

    names = ["w_ada", "b_ada", "norm_mix_g", "w_in", "rnn_conv_w", "rnn_conv_b", "lru_w_a", "lru_b_a", "lru_w_x",
             "lru_b_x", "lru_lambda", "sgu_ln_g", "sgu_ln_b", "sgu_w_s", "sgu_b_s", "w_branch_a", "w_branch_b",
             "w_out", "norm_ffn_g", "w_up", "ffn_conv_w", "ffn_conv_b", "w_down", "norm_final_g"]
    result = [loss, grad_x.reshape(x.shape)]
    for kind in ("grad_", "delta_", "new_m_", "new_v_"):
        result += [out[kind + n] for n in names]
    return tuple(result)
```

```python
import jax
import jax.numpy as jnp
from jax import lax
from jax.experimental import pallas as pl
from jax.experimental.pallas import tpu as pltpu

F32 = jnp.float32
BF = jnp.bfloat16

D = 1024
HEADS = 8
HD = D // HEADS
SGU_BLOCK = 128
N_CHIPS = 4
N_DEV = 8
EPS = 1e-6
LRU_C = 8.0
LANES = 128
SUBLANES = 8
ELEMENTWISE_BLOCK_BYTES = 3 << 19
TN_ROWS = 2048
TN_ROWS_SQUARE = 4096

ADAM_LR = 0.001
ADAM_B1 = 0.9
ADAM_B2 = 0.999
ADAM_EPS = 1e-08
ADAM_WD = 0.01
ADAM_STEP = 10

GELU_K0 = 0.7978845608028654
GELU_K1 = 0.044715

HBM_SPEC = pl.BlockSpec(memory_space=pltpu.HBM)
MESH_ID = pl.DeviceIdType.MESH


def _pcall(body, *, name, out_shape, grid=(), in_specs=None, out_specs=None, scratch=(), vmem_mb=32, aliases=None,
           grid_spec=None):
    kw = {}
    if aliases:
        kw["input_output_aliases"] = aliases
    if grid_spec is not None:
        kw["grid_spec"] = grid_spec
        ndim = len(grid_spec.grid)
    else:
        kw.update(grid=grid, in_specs=in_specs, out_specs=out_specs, scratch_shapes=list(scratch))
        ndim = len(grid)
    if ndim:
        params = pltpu.CompilerParams(dimension_semantics=("arbitrary",) * ndim, vmem_limit_bytes=vmem_mb * 2 ** 20)
    else:
        params = pltpu.CompilerParams(vmem_limit_bytes=vmem_mb * 2 ** 20)
    return pl.pallas_call(body, name=name, out_shape=out_shape, compiler_params=params, **kw)


def _gelu_cdf(x, x2):
    return 0.5 * jnp.tanh(x * (GELU_K0 + (GELU_K0 * GELU_K1) * x2)) + 0.5


def _gelu(x):
    return x * _gelu_cdf(x, x * x)


def _gelu_and_grad(x):
    x2 = x * x
    s = _gelu_cdf(x, x2)
    g = x * s
    dg = s * (1.0 + (x - g) * ((2.0 * GELU_K0) + (6.0 * GELU_K0 * GELU_K1) * x2))
    return g, dg


def _sigmoid(x):
    return 1.0 / (1.0 + jnp.exp(-x))


def _sigmoid_t(x):
    return 0.5 * jnp.tanh(0.5 * x) + 0.5


def _log_sigmoid(x):
    e = jnp.exp(-jnp.abs(x))
    u = 1.0 + e
    d = u - 1.0
    l1p = jnp.where(d == 0.0, e, jnp.log(u) * (e / jnp.where(d == 0.0, 1.0, d)))
    return jnp.minimum(x, 0.0) - l1p


def _dot(a, b):
    return jnp.dot(a, b, preferred_element_type=F32)


def _dot_nt(a, b):
    return lax.dot_general(a, b, (((1,), (1,)), ((), ())), preferred_element_type=F32)


def _dot_tn(a, b):
    return lax.dot_general(a, b, (((0,), (0,)), ((), ())), preferred_element_type=F32)


def _shift_down(x, halo, s):
    r = pltpu.roll(x, s, 0)
    rows = lax.broadcasted_iota(jnp.int32, (SUBLANES, x.shape[1]), 0)
    head = jnp.where(rows < s, pltpu.roll(halo, s, 0), r[0:SUBLANES])
    return jnp.concatenate([head, r[SUBLANES:]], axis=0)


def _shift_up(x, halo, s):
    n = x.shape[0]
    r = pltpu.roll(x, n - s, 0)
    rows = lax.broadcasted_iota(jnp.int32, (SUBLANES, x.shape[1]), 0)
    tail = jnp.where(rows >= SUBLANES - s, pltpu.roll(halo, SUBLANES - s, 0), r[n - SUBLANES:n])
    return jnp.concatenate([r[:n - SUBLANES], tail], axis=0)


def _scan_rows(a, u, reverse):
    n, width = a.shape
    rows = lax.broadcasted_iota(jnp.int32, (n, width), 0)
    d = 1
    while d < n:
        if d < SUBLANES:
            keep = rows < n - d if reverse else rows >= d
            shift = n - d if reverse else d
            a_s = jnp.where(keep, pltpu.roll(a, shift, 0), 1.0)
            u_s = jnp.where(keep, pltpu.roll(u, shift, 0), 0.0)
        elif reverse:
            a_s = jnp.concatenate([a[d:], jnp.ones((d, width), a.dtype)], axis=0)
            u_s = jnp.concatenate([u[d:], jnp.zeros((d, width), u.dtype)], axis=0)
        else:
            a_s = jnp.concatenate([jnp.ones((d, width), a.dtype), a[:n - d]], axis=0)
            u_s = jnp.concatenate([jnp.zeros((d, width), u.dtype), u[:n - d]], axis=0)
        u = a * u_s + u
        a = a * a_s
        d *= 2
    return a, u


def _colsum(x):
    return jnp.sum(x, axis=0, keepdims=True)


def _rms_stats(x):
    r = lax.rsqrt(jnp.mean(x * x, axis=-1, keepdims=True) + EPS)
    return r, x * r


def _lru_gates(xc, wa_ref, ba, wx_ref, bx, lam, head0=0):
    pr, pi = [], []
    for hh in range(xc.shape[1] // HD):
        xh = xc[:, hh * HD:(hh + 1) * HD].astype(BF)
        pr.append(_dot(xh, wa_ref[head0 + hh].astype(BF)))
        pi.append(_dot(xh, wx_ref[head0 + hh].astype(BF)))
    r = _sigmoid_t((pr[0] if len(pr) == 1 else jnp.concatenate(pr, axis=1)) + ba)
    ig = _sigmoid_t((pi[0] if len(pi) == 1 else jnp.concatenate(pi, axis=1)) + bx)
    ls = _log_sigmoid(lam)
    log_a = LRU_C * r * ls
    a = jnp.exp(log_a)
    x2 = 2.0 * log_a
    u = a * a
    lu = jnp.log(jnp.maximum(u, 1e-37))
    ratio = x2 * pl.reciprocal(jnp.where(lu == 0.0, 1.0, lu), approx=True)
    em1 = jnp.where(lu == 0.0, x2, jnp.where(u < 1e-30, -1.0, (u - 1.0) * ratio))
    mult = jnp.sqrt(-em1)
    return r, ig, ls, a, mult


def _sgu_mix(vln, ws_ref, bst_ref, tb):
    ri = lax.broadcasted_iota(jnp.int32, (SGU_BLOCK, SGU_BLOCK), 0)
    ci = lax.broadcasted_iota(jnp.int32, (SGU_BLOCK, SGU_BLOCK), 1)
    wm = [jnp.where(ri >= ci, ws_ref[g], 0.0).astype(BF) for g in range(HEADS)]
    blocks = []
    for blk in range(tb // SGU_BLOCK):
        cols = []
        for g in range(HEADS):
            vb = vln[blk * SGU_BLOCK:(blk + 1) * SGU_BLOCK, g * HD:(g + 1) * HD].astype(BF)
            cols.append(_dot(wm[g], vb) + bst_ref[:, g:g + 1])
        blocks.append(jnp.concatenate(cols, axis=1))
    mixed = blocks[0] if len(blocks) == 1 else jnp.concatenate(blocks, axis=0)
    return wm, mixed


def _layernorm_stats(v):
    mu = jnp.mean(v, axis=-1, keepdims=True)
    vc = v - mu
    rstd = lax.rsqrt(jnp.mean(vc * vc, axis=-1, keepdims=True) + EPS)
    return rstd, vc * rstd


def _my_xyc():
    return lax.axis_index("x"), lax.axis_index("y"), lax.axis_index("c")


def _gather_weights(srcs, halve):
    n = len(srcs)
    out_shape = [jax.ShapeDtypeStruct((N_CHIPS,) + s.shape, s.dtype) for s in srcs]

    def body(*refs):
        src, out = refs[:n], refs[n:2 * n]
        send_sems, recv_sems, fwd_send, fwd_recv, loc_sems = refs[2 * n:]
        x, y, c = _my_xyc()
        me = 2 * x + y
        chips = [(1 - x, y), (x, 1 - y), (1 - x, 1 - y)]

        def half(ref, a, which):
            if not halve[a]:
                return ref
            h = srcs[a].shape[0] // 2
            return ref.at[pl.ds(which * h, h)]

        def ici(a, k, frm):
            px, py = chips[k]
            return pltpu.make_async_remote_copy(
                src_ref=half(src[a], a, c), dst_ref=half(out[a].at[frm], a, c),
                send_sem=send_sems.at[a, k], recv_sem=recv_sems.at[a, k],
                device_id=(px, py, c), device_id_type=MESH_ID)

        def d2d(a, k, which):
            px, py = chips[k]
            rows = half(out[a].at[2 * px + py], a, which)
            return pltpu.make_async_remote_copy(
                src_ref=rows, dst_ref=rows, send_sem=fwd_send.at[a, k], recv_sem=fwd_recv.at[a, k],
                device_id=(x, y, 1 - c), device_id_type=MESH_ID)

        local, sends = [], []
        for a in range(n):
            lc = pltpu.make_async_copy(src[a], out[a].at[me], loc_sems.at[a])
            lc.start()
            local.append(lc)
            for k in range(3):
                cp = ici(a, k, me)
                cp.start()
                sends.append(cp)
        for a in range(n):
            for k in range(3):
                px, py = chips[k]
                ici(a, k, 2 * px + py).wait_recv()
                if halve[a]:
                    fw = d2d(a, k, c)
                    fw.start()
                    sends.append(fw)
        for a in range(n):
            if halve[a]:
                for k in range(3):
                    d2d(a, k, 1 - c).wait_recv()
        for cp in sends:
            cp.wait_send()
        for lc in local:
            lc.wait()

    sem = pltpu.SemaphoreType.DMA((n, 3))
    return _pcall(body, name="gather_weights", out_shape=out_shape, in_specs=[HBM_SPEC] * n,
                  out_specs=[HBM_SPEC] * n, scratch=[sem, sem, sem, sem, pltpu.SemaphoreType.DMA((n,))])(*srcs)


SEM_SPEC = pl.BlockSpec(memory_space=pltpu.SEMAPHORE)


def _remote_start(srcs, lands, plan, ncopies, name):
    n, m = len(srcs), len(lands)

    def body(*refs):
        src, land = refs[:n], refs[n:n + m]
        send_sems, recv_sems = refs[n + m], refs[n + m + 1]
        token = refs[-1]
        x, y, c = _my_xyc()
        for i, (s, d, dev) in enumerate(plan(src, land, x, y, c)):
            pltpu.make_async_remote_copy(src_ref=s, dst_ref=d, send_sem=send_sems.at[i], recv_sem=recv_sems.at[i],
                                         device_id=dev, device_id_type=MESH_ID).start()
        token[...] = jnp.zeros_like(token)

    bufs = list(srcs) + list(lands)
    out = pl.pallas_call(
        body, name=name,
        out_shape=(pltpu.SemaphoreType.DMA((ncopies,)), pltpu.SemaphoreType.DMA((ncopies,)),
                   *[pltpu.HBM(b.shape, b.dtype) for b in bufs], jax.ShapeDtypeStruct((SUBLANES, LANES), F32)),
        in_specs=[HBM_SPEC] * (n + m),
        out_specs=(SEM_SPEC, SEM_SPEC, *[HBM_SPEC] * (n + m), pl.BlockSpec(memory_space=pltpu.VMEM)),
        input_output_aliases={i: 2 + i for i in range(n + m)},
        compiler_params=pltpu.CompilerParams(has_side_effects=pltpu.SideEffectType.DATAFLOW_SIDE_EFFECTING),
    )(*[pltpu.with_memory_space_constraint(b, pltpu.HBM) for b in bufs])
    return (out[0], out[1], out[2:2 + n], out[2 + n:2 + n + m]), out[-1]


def _remote_wait(handle, plan, after, name):
    send_sems, recv_sems, srcs, lands = handle
    n, m = len(srcs), len(lands)

    def body(*refs):
        src, land = refs[:n], refs[n:n + m]
        ssem, rsem = refs[n + m], refs[n + m + 1]
        x, y, c = _my_xyc()
        for i, (s, d, dev) in enumerate(plan(src, land, x, y, c)):
            cp = pltpu.make_async_remote_copy(src_ref=s, dst_ref=d, send_sem=ssem.at[i], recv_sem=rsem.at[i],
                                              device_id=dev, device_id_type=MESH_ID)
            cp.wait_send()
            cp.wait_recv()

    bufs = list(srcs) + list(lands)
    out = pl.pallas_call(
        body, name=name, out_shape=tuple(pltpu.HBM(b.shape, b.dtype) for b in bufs),
        in_specs=[HBM_SPEC] * (n + m) + [SEM_SPEC, SEM_SPEC, pl.BlockSpec(memory_space=pl.ANY)],
        out_specs=tuple([HBM_SPEC] * (n + m)), input_output_aliases={i: i for i in range(n + m)},
        compiler_params=pltpu.CompilerParams(has_side_effects=pltpu.SideEffectType.DATAFLOW_SIDE_EFFECTING),
    )(*bufs, send_sems, recv_sems, after)
    return out[:n], out[n:]


def _chips_of(x, y):
    return [(1 - x, y), (x, 1 - y), (1 - x, 1 - y)]


def _gather_plan(count):
    def plan(src, land, x, y, c):
        me = 2 * x + y
        return [(src[a], land[a].at[me], (px, py, c)) for a in range(count) for px, py in _chips_of(x, y)]

    return plan


def _place_own(srcs, lands):
    n = len(srcs)

    def body(*refs):
        src, land, sems = refs[:n], refs[2 * n:3 * n], refs[3 * n]
        me = 2 * lax.axis_index("x") + lax.axis_index("y")
        copies = [pltpu.make_async_copy(src[a], land[a].at[me], sems.at[a]) for a in range(n)]
        for cp in copies:
            cp.start()
        for cp in copies:
            cp.wait()

    return _pcall(body, name="place_own_shards", out_shape=[jax.ShapeDtypeStruct(l.shape, l.dtype) for l in lands],
                  in_specs=[HBM_SPEC] * (2 * n), out_specs=[HBM_SPEC] * n, aliases={n + a: a for a in range(n)},
                  scratch=[pltpu.SemaphoreType.DMA((n,))])(*srcs, *lands)


def _gather8(src, name):
    def body(src_ref, out_ref, send_sems, recv_sems, loc_sem):
        x, y, c = _my_xyc()
        me = 4 * x + 2 * y + c
        lc = pltpu.make_async_copy(src_ref, out_ref.at[me], loc_sem)
        lc.start()
        cps = []
        for k in range(1, N_DEV):
            px = 1 - x if (k >> 2) & 1 else x
            py = 1 - y if (k >> 1) & 1 else y
            pc = 1 - c if k & 1 else c
            cp = pltpu.make_async_remote_copy(
                src_ref=src_ref, dst_ref=out_ref.at[me], send_sem=send_sems.at[k - 1], recv_sem=recv_sems.at[k - 1],
                device_id=(px, py, pc), device_id_type=MESH_ID)
            cp.start()
            cps.append(cp)
        for cp in cps:
            cp.wait()
        lc.wait()

    return _pcall(body, name=name, out_shape=jax.ShapeDtypeStruct((N_DEV,) + src.shape, src.dtype),
                  in_specs=[HBM_SPEC], out_specs=HBM_SPEC,
                  scratch=[pltpu.SemaphoreType.DMA((N_DEV - 1,)), pltpu.SemaphoreType.DMA((N_DEV - 1,)),
                           pltpu.SemaphoreType.DMA])(src)


def _cast_shards(arrs, name, after=None):
    n = len(arrs)
    extra = [] if after is None else [after]

    def body(*refs):
        ins, outs = refs[:n], refs[n + len(extra):]
        for a in range(n):
            outs[a][...] = ins[a][...].astype(BF)

    specs = [pl.BlockSpec((s.shape[0] // 4, s.shape[1]), lambda i: (i, 0)) for s in arrs]
    return _pcall(body, name=name, grid=(4,), in_specs=specs + [pl.BlockSpec(memory_space=pl.ANY)] * len(extra),
                  out_specs=specs, out_shape=[jax.ShapeDtypeStruct(s.shape, BF) for s in arrs])(*arrs, *extra)


def _row_tile(rows, cols):
    t = rows
    while t * cols * 4 > ELEMENTWISE_BLOCK_BYTES and t % (2 * SUBLANES) == 0:
        t //= 2
    return t


def _sum_parts(parts, name):
    p, rows, cols = parts.shape
    tr = _row_tile(rows, cols * p // 2)

    def body(p_ref, o_ref):
        acc = p_ref[0].astype(F32)
        for k in range(1, p):
            acc = acc + p_ref[k].astype(F32)
        o_ref[...] = acc

    return _pcall(body, name=name, grid=(rows // tr,),
                  in_specs=[pl.BlockSpec((p, tr, cols), lambda i: (0, i, 0))],
                  out_specs=pl.BlockSpec((tr, cols), lambda i: (i, 0)),
                  out_shape=jax.ShapeDtypeStruct((rows, cols), F32), vmem_mb=48)(parts)


def _sum_own_and_landed(chip, sums, landed, name):
    _, rows, cols = sums.shape
    tr = _row_tile(rows, 2 * cols)

    def body(chip_ref, own_ref, land_ref, o_ref):
        del chip_ref
        acc = own_ref[0].astype(F32)
        for k in range(3):
            acc = acc + land_ref[k].astype(F32)
        o_ref[...] = acc

    grid_spec = pltpu.PrefetchScalarGridSpec(
        num_scalar_prefetch=1, grid=(rows // tr,),
        in_specs=[pl.BlockSpec((1, tr, cols), lambda i, chip_ref: (chip_ref[0], i, 0)),
                  pl.BlockSpec((3, tr, cols), lambda i, chip_ref: (0, i, 0))],
        out_specs=pl.BlockSpec((tr, cols), lambda i, chip_ref: (i, 0)))
    return _pcall(body, name=name, grid_spec=grid_spec, out_shape=jax.ShapeDtypeStruct((rows, cols), F32),
                  vmem_mb=48)(chip, sums, landed)


def _swap_cores(arrs, name):
    n = len(arrs)

    def body(*refs):
        src, out = refs[:n], refs[n:2 * n]
        send_sems, recv_sems = refs[2 * n:]
        x, y, c = _my_xyc()
        cps = []
        for a in range(n):
            cp = pltpu.make_async_remote_copy(
                src_ref=src[a], dst_ref=out[a], send_sem=send_sems.at[a], recv_sem=recv_sems.at[a],
                device_id=(x, y, 1 - c), device_id_type=MESH_ID)
            cp.start()
            cps.append(cp)
        for cp in cps:
            cp.wait()

    sem = pltpu.SemaphoreType.DMA((n,))
    return _pcall(body, name=name, out_shape=[jax.ShapeDtypeStruct(a.shape, a.dtype) for a in arrs],
                  in_specs=[HBM_SPEC] * n, out_specs=[HBM_SPEC] * n, scratch=[sem, sem])(*arrs)


def _add_pair(a, b, name):
    rows, cols = a.shape
    tr = _row_tile(rows, 2 * cols)

    def body(a_ref, b_ref, o_ref):
        o_ref[...] = a_ref[...] + b_ref[...]

    spec = pl.BlockSpec((tr, cols), lambda i: (i, 0))
    return _pcall(body, name=name, grid=(rows // tr,), in_specs=[spec, spec], out_specs=spec,
                  out_shape=jax.ShapeDtypeStruct((rows, cols), F32))(a, b)


def _sum_chips_in_order(chip, own, landed, name):
    rows, cols = own.shape
    tr = _row_tile(rows, 4 * cols)

    def body(chip_ref, own_ref, land_ref, o_ref):
        me = chip_ref[0]
        acc = None
        for p in range(N_CHIPS):
            q = p ^ me
            k = jnp.where(q == 2, 0, jnp.where(q == 1, 1, 2))
            term = jnp.where(q == 0, own_ref[...], land_ref[k])
            acc = term if acc is None else acc + term
        o_ref[...] = acc

    grid_spec = pltpu.PrefetchScalarGridSpec(
        num_scalar_prefetch=1, grid=(rows // tr,),
        in_specs=[pl.BlockSpec((tr, cols), lambda i, chip_ref: (i, 0)),
                  pl.BlockSpec((3, tr, cols), lambda i, chip_ref: (0, i, 0))],
        out_specs=pl.BlockSpec((tr, cols), lambda i, chip_ref: (i, 0)))
    return _pcall(body, name=name, grid_spec=grid_spec, out_shape=jax.ShapeDtypeStruct((rows, cols), F32))(
        chip, own, landed)


def _bcast_plan(src, land, x, y, c):
    return [(src[0], land[0].at[k], (px, py, c)) for k, (px, py) in enumerate(_chips_of(x, y))]


def _sibling_plan(count):
    def plan(src, land, x, y, c):
        return [(src[a], land[a], (x, y, 1 - c)) for a in range(count)]

    return plan


def _scatter_plan(count):
    def plan(src, land, x, y, c):
        out = []
        for a in range(count):
            for k, (px, py) in enumerate(_chips_of(x, y)):
                out.append((src[a].at[2 * px + py], land[a].at[k], (px, py, c)))
        return out

    return plan


def _adamw_math(w, g, m, v):
    m2 = ADAM_B1 * m + (1.0 - ADAM_B1) * g
    v2 = ADAM_B2 * v + (1.0 - ADAM_B2) * (g * g)
    m_hat = m2 / (1.0 - ADAM_B1 ** ADAM_STEP)
    v_hat = v2 / (1.0 - ADAM_B2 ** ADAM_STEP)
    delta = -ADAM_LR * (m_hat / (jnp.sqrt(v_hat) + ADAM_EPS) + ADAM_WD * w)
    return delta, m2, v2


def _adamw(w, m, v, grads, name):
    rows, cols = w.shape
    tr = _row_tile(rows, cols)
    ng = len(grads)

    def body(*refs):
        w_ref, m_ref, v_ref = refs[:3]
        g = refs[3][...]
        for k in range(1, ng):
            g = g + refs[3 + k][...]
        g_ref, d_ref, m2_ref, v2_ref = refs[3 + ng:]
        delta, m2, v2 = _adamw_math(w_ref[...], g, m_ref[...], v_ref[...])
        g_ref[...] = g
        d_ref[...] = delta
        m2_ref[...] = m2
        v2_ref[...] = v2

    spec = pl.BlockSpec((tr, cols), lambda i: (i, 0))
    return _pcall(body, name=name, grid=(rows // tr,), in_specs=[spec] * (3 + ng), out_specs=[spec] * 4,
                  out_shape=[jax.ShapeDtypeStruct((rows, cols), F32)] * 4, vmem_mb=48)(w, m, v, *grads)


def _ada_adamw(ct, dmod, w, m, v):
    rows, cols = w.shape
    tr = _row_tile(rows, cols)

    def body(ct_ref, dm_ref, w_ref, m_ref, v_ref, g_ref, d_ref, m2_ref, v2_ref):
        cv = ct_ref[...]
        ca = cv * _sigmoid(cv)
        g = ca[:, 0:1] * dm_ref[0:1, :]
        for b in range(1, N_DEV):
            g = g + ca[:, b:b + 1] * dm_ref[b:b + 1, :]
        delta, m2, v2 = _adamw_math(w_ref[...], g, m_ref[...], v_ref[...])
        g_ref[...] = g
        d_ref[...] = delta
        m2_ref[...] = m2
        v2_ref[...] = v2

    spec = pl.BlockSpec((tr, cols), lambda i: (i, 0))
    return _pcall(body, name="ada_adamw", grid=(rows // tr,),
                  in_specs=[pl.BlockSpec((tr, N_DEV), lambda i: (i, 0)), pl.BlockSpec((N_DEV, cols), lambda i: (0, 0)),
                            spec, spec, spec],
                  out_specs=[spec] * 4, out_shape=[jax.ShapeDtypeStruct((rows, cols), F32)] * 4,
                  vmem_mb=48)(ct, dmod, w, m, v)


def _mod_fwd(c_all, w, b):
    cols = w.shape[1]
    tn = cols // 3

    def body(c_ref, w_ref, b_ref, o_ref):
        cv = c_ref[...]
        ca = (cv * _sigmoid(cv)).astype(BF)
        o_ref[...] = _dot(ca, w_ref[...].astype(BF)) + b_ref[...]

    return _pcall(body, name="mod_fwd", grid=(3,),
                  in_specs=[pl.BlockSpec((N_DEV, D), lambda j: (0, 0)), pl.BlockSpec((D, tn), lambda j: (0, j)),
                            pl.BlockSpec((1, tn), lambda j: (0, j))],
                  out_specs=pl.BlockSpec((N_DEV, tn), lambda j: (0, j)),
                  out_shape=jax.ShapeDtypeStruct((N_DEV, cols), F32))(c_all, w, b)


def _resident(shape):
    zeros = (0,) * len(shape)
    return pl.BlockSpec(shape, lambda *_: zeros, pipeline_mode=pl.Buffered(1))


def _mixer_fwd(x, g, scale, shift, gate1, w_in4, cw, cb, wa, ba, wx, bx, lam, lg, lb, ws, bst, wba, wbb, wo,
               tm=256, chunk=256, piece=512):
    T = x.shape[0]
    tm = min(tm, T)
    ns = w_in4.shape[2]
    per = ns // piece

    def body(x_ref, g_ref, sc_ref, sh_ref, g1_ref, w_ref, cw_ref, cb_ref, wa_ref, ba_ref, wx_ref, bx_ref, lam_ref,
             lg_ref, lb_ref, ws_ref, bst_ref, wba_ref, wbb_ref, wo_ref,
             h1_ref, z_ref, hl_ref, yap_ref, ybp_ref, mg_ref, ya_ref, yb_ref, o_ref, x2_ref,
             xc_ref, r_ref, ig_ref, mu_ref, a_ref, prev, hc):
        i = pl.program_id(0)

        @pl.when(i == 0)
        def _():
            prev[...] = jnp.zeros_like(prev)
            hc[...] = jnp.zeros_like(hc)

        xv = x_ref[...]
        _, xh = _rms_stats(xv)
        h = ((xh * g_ref[...]) * (1.0 + sc_ref[...]) + sh_ref[...]).astype(BF)
        h1_ref[...] = h

        def proj(col, width):
            for c0 in range(col, col + width, piece):
                w = min(piece, col + width - c0)
                j, off = c0 // ns, c0 % ns
                z_ref[:, c0:c0 + w] = _dot(h, w_ref[j, :, off:off + w])

        def lru_chunk(c0):
            cs = slice(c0, c0 + chunk)
            xr = z_ref[:, cs]
            pv = prev[:, cs]
            xc = (cb_ref[:, cs] + cw_ref[3:4, cs] * xr + cw_ref[2:3, cs] * _shift_down(xr, pv, 1)
                  + cw_ref[1:2, cs] * _shift_down(xr, pv, 2) + cw_ref[0:1, cs] * _shift_down(xr, pv, 3))
            prev[:, cs] = xr[tm - SUBLANES:tm]
            r, ig, _, a, mult = _lru_gates(xc, wa_ref, ba_ref[:, cs], wx_ref, bx_ref[:, cs], lam_ref[:, cs],
                                           head0=c0 // HD)
            xc_ref[:, cs] = xc.astype(BF)
            r_ref[:, cs] = r.astype(BF)
            ig_ref[:, cs] = ig.astype(BF)
            a_ref[:, cs] = a
            mu_ref[:, cs] = mult.astype(BF)
            a, u = _scan_rows(a, mult * (ig * xc), reverse=False)
            hv = u + a * hc[SUBLANES - 1:SUBLANES, cs]
            hc[:, cs] = hv[tm - SUBLANES:tm]
            hl_ref[:, cs] = hv
            yap_ref[:, cs] = (hv * _gelu(z_ref[:, D + c0:D + c0 + chunk])).astype(BF)

        proj(0, chunk)
        proj(D, chunk)
        for c0 in range(0, D, chunk):
            if c0 + chunk < D:
                proj(c0 + chunk, chunk)
                proj(D + c0 + chunk, chunk)
            else:
                proj(2 * D, 2 * D)
            lru_chunk(c0)
        proj(4 * D, 2 * D)
        _, xhn = _layernorm_stats(_gelu(z_ref[:, 3 * D:4 * D]))
        vln = xhn * lg_ref[...] + lb_ref[...]
        _, mixed = _sgu_mix(vln, ws_ref, bst_ref, tm)
        ybp = (_gelu(z_ref[:, 2 * D:3 * D]) * mixed).astype(BF)
        ybp_ref[...] = ybp
        ya = _dot(yap_ref[...], wba_ref[...])
        yb = _dot(ybp, wbb_ref[...])
        merged = (_sigmoid_t(z_ref[:, 4 * D:5 * D]) * ya + _sigmoid_t(z_ref[:, 5 * D:6 * D]) * yb).astype(BF)
        o = _dot(merged, wo_ref[...])
        x2_ref[...] = xv + g1_ref[...] * o
        mg_ref[...] = merged
        ya_ref[...] = ya.astype(BF)
        yb_ref[...] = yb.astype(BF)
        o_ref[...] = o.astype(BF)

    row = pl.BlockSpec((tm, D), lambda i: (i, 0))
    vec = pl.BlockSpec((1, D), lambda i: (0, 0))
    bf_row = jax.ShapeDtypeStruct((T, D), BF)
    f32_row = jax.ShapeDtypeStruct((T, D), F32)
    return _pcall(body, name="mixer_fwd", grid=(T // tm,),
                  in_specs=[row, vec, vec, vec, vec, _resident(w_in4.shape), _resident(cw.shape), vec,
                            _resident(wa.shape), vec, _resident(wx.shape), vec, vec, vec, vec,
                            _resident(ws.shape), _resident(bst.shape),
                            _resident(wba.shape), _resident(wbb.shape), _resident(wo.shape)],
                  out_specs=[row, pl.BlockSpec((tm, 6 * D), lambda i: (i, 0))] + [row] * 13,
                  out_shape=[bf_row, jax.ShapeDtypeStruct((T, 6 * D), F32), f32_row, bf_row, bf_row, bf_row, bf_row,
                             bf_row, bf_row, f32_row, bf_row, bf_row, bf_row, bf_row, f32_row],
                  scratch=[pltpu.VMEM((SUBLANES, D), F32), pltpu.VMEM((SUBLANES, D), F32)], vmem_mb=60)(
        x, g, scale, shift, gate1, w_in4, cw, cb, wa, ba, wx, bx, lam, lg, lb, ws, bst, wba, wbb, wo)


def _ffn_fwd(x2, g, scale, shift, gate2, gf, w_up4, wd, cw, cb, target, tm=256, chunk=768):
    T = x2.shape[0]
    tm = min(tm, T)
    ns = w_up4.shape[2]
    dff = wd.shape[0]
    nchunk = dff // chunk
    per = ns // chunk

    def body(x2_ref, g_ref, sc_ref, sh_ref, g2_ref, gf_ref, wu_ref, wd_ref, cw_ref, cb_ref, t_ref,
             h2_ref, up_ref, f_ref, ga_ref, vd_ref, loss_ref, dx3_ref, dfo_ref, dgf_ref, dg2_ref, prev):
        i = pl.program_id(0)

        @pl.when(i == 0)
        def _():
            prev[...] = jnp.zeros_like(prev)
            loss_ref[...] = jnp.zeros_like(loss_ref)
            dgf_ref[...] = jnp.zeros_like(dgf_ref)
            dg2_ref[...] = jnp.zeros_like(dg2_ref)

        x2v = x2_ref[...]
        _, xh2 = _rms_stats(x2v)
        h2 = ((xh2 * g_ref[...]) * (1.0 + sc_ref[...]) + sh_ref[...]).astype(BF)
        h2_ref[...] = h2

        def conv(u, col):
            cs = slice(col, col + chunk)
            p = prev[:, cs]
            hid = (cb_ref[:, cs] + cw_ref[2:3, cs] * u + cw_ref[1:2, cs] * _shift_down(u, p, 1)
                   + cw_ref[0:1, cs] * _shift_down(u, p, 2))
            prev[:, cs] = u[tm - SUBLANES:tm]
            up_ref[:, cs] = u.astype(BF)
            return hid

        def up_proj(k):
            off = (k % per) * chunk
            return (_dot(h2, wu_ref[k // per, :, off:off + chunk]),
                    _dot(h2, wu_ref[N_CHIPS // 2 + k // per, :, off:off + chunk]))

        fo = None
        nxt = up_proj(0)
        for k in range(nchunk):
            col = k * chunk
            ua, uv = nxt
            if k + 1 < nchunk:
                nxt = up_proj(k + 1)
            act = conv(ua, col)
            val = conv(uv, dff + col)
            ga, dga = _gelu_and_grad(act)
            fk = (ga * val).astype(BF)
            f_ref[:, col:col + chunk] = fk
            ga_ref[:, col:col + chunk] = ga.astype(BF)
            vd_ref[:, col:col + chunk] = (val * dga).astype(BF)
            part = _dot(fk, wd_ref[col:col + chunk, :])
            fo = part if fo is None else fo + part

        x3 = x2v + g2_ref[...] * fo
        rstd, xh = _rms_stats(x3)
        err = xh * gf_ref[...] - t_ref[...]
        loss_ref[...] += 0.5 * jnp.sum(jnp.mean(err * err, axis=-1, keepdims=True), axis=0, keepdims=True)
        dy = err * (1.0 / D)
        dgf_ref[...] += _colsum(dy * xh)
        dxh = dy * gf_ref[...]
        dx3 = rstd * (dxh - xh * jnp.mean(dxh * xh, axis=-1, keepdims=True))
        dg2_ref[...] += _colsum(dx3 * fo)
        dx3_ref[...] = dx3
        dfo_ref[...] = (g2_ref[...] * dx3).astype(BF)

    row = pl.BlockSpec((tm, D), lambda i: (i, 0))
    vec = pl.BlockSpec((1, D), lambda i: (0, 0))
    wide = pl.BlockSpec((tm, 2 * dff), lambda i: (i, 0))
    half = pl.BlockSpec((tm, dff), lambda i: (i, 0))
    return _pcall(body, name="ffn_fwd", grid=(T // tm,),
                  in_specs=[row, vec, vec, vec, vec, vec, _resident(w_up4.shape), _resident(wd.shape),
                            _resident(cw.shape), _resident(cb.shape), row],
                  out_specs=[row, wide, half, half, half, pl.BlockSpec((1, LANES), lambda i: (0, 0)), row, row, vec, vec],
                  out_shape=[jax.ShapeDtypeStruct((T, D), BF), jax.ShapeDtypeStruct((T, 2 * dff), BF),
                             jax.ShapeDtypeStruct((T, dff), BF), jax.ShapeDtypeStruct((T, dff), BF),
                             jax.ShapeDtypeStruct((T, dff), BF), jax.ShapeDtypeStruct((1, LANES), F32),
                             jax.ShapeDtypeStruct((T, D), F32), jax.ShapeDtypeStruct((T, D), BF),
                             jax.ShapeDtypeStruct((1, D), F32), jax.ShapeDtypeStruct((1, D), F32)],
                  scratch=[pltpu.VMEM((SUBLANES, 2 * dff), F32)], vmem_mb=56)(
        x2, g, scale, shift, gate2, gf, w_up4, wd, cw, cb, target)


def _ffn_bwd(dfo, wd, up, ga, vd, cw, w_up4, x2, resid, g, scale, gate, o, tm=256, chunk=1536):
    T = up.shape[0]
    tm = min(tm, T)
    dff = wd.shape[0]
    ns = w_up4.shape[2]
    nchunk = dff // chunk
    per = ns // chunk
    nrow = T // tm

    def body(dfo_ref, wd_ref, up_ref, ga_ref, vd_ref, cw_ref, wu_ref, x_ref, r_ref, g_ref, sc_ref, gt_ref, o_ref,
             du_ref, dcw_ref, dcb_ref, dx_ref, dsh_ref, dsc_ref, dg_ref, do_ref, dgt_ref, nxt):
        i = pl.program_id(0)

        @pl.when(i == 0)
        def _():
            nxt[...] = jnp.zeros_like(nxt)
            for ref in (dcw_ref, dcb_ref, dsh_ref, dsc_ref, dg_ref, dgt_ref):
                ref[...] = jnp.zeros_like(ref)

        dfo_t = dfo_ref[...]

        def conv_bwd(dh, col):
            cs = slice(col, col + chunk)
            n8 = nxt[:, cs]
            dh1 = _shift_up(dh, n8, 1)
            dh2 = _shift_up(dh, n8, 2)
            nxt[:, cs] = dh[0:SUBLANES]
            du = (cw_ref[2:3, cs] * dh + cw_ref[1:2, cs] * dh1 + cw_ref[0:1, cs] * dh2).astype(BF)
            du_ref[:, cs] = du
            u = up_ref[:, cs].astype(F32)
            dcw_ref[2:3, cs] += _colsum(dh * u)
            dcw_ref[1:2, cs] += _colsum(dh1 * u)
            dcw_ref[0:1, cs] += _colsum(dh2 * u)
            dcb_ref[:, cs] += _colsum(dh)
            return du

        def down_bwd(k):
            return _dot_nt(dfo_t, wd_ref[k * chunk:(k + 1) * chunk, :])

        dh = None
        df_next = down_bwd(0)
        for k in range(nchunk):
            col = k * chunk
            off = (k % per) * chunk
            df = df_next
            if k + 1 < nchunk:
                df_next = down_bwd(k + 1)
            du_a = conv_bwd(df * vd_ref[:, col:col + chunk].astype(F32), col)
            du_v = conv_bwd(df * ga_ref[:, col:col + chunk].astype(F32), dff + col)
            part = (_dot_nt(du_a, wu_ref[k // per, :, off:off + chunk])
                    + _dot_nt(du_v, wu_ref[N_CHIPS // 2 + k // per, :, off:off + chunk]))
            dh = part if dh is None else dh + part

        rstd, xh = _rms_stats(x_ref[...])
        dsh_ref[...] += _colsum(dh)
        dsc_ref[...] += _colsum(dh * (xh * g_ref[...]))
        dn = dh * (1.0 + sc_ref[...])
        dg_ref[...] += _colsum(dn * xh)
        dxh = dn * g_ref[...]
        dx = r_ref[...] + rstd * (dxh - xh * jnp.mean(dxh * xh, axis=-1, keepdims=True))
        dx_ref[...] = dx
        do_ref[...] = (gt_ref[...] * dx).astype(BF)
        dgt_ref[...] += _colsum(dx * o_ref[...].astype(F32))

    rev = lambda i: (nrow - 1 - i, 0)
    row = pl.BlockSpec((tm, D), rev)
    vec = pl.BlockSpec((1, D), lambda i: (0, 0))
    wide = pl.BlockSpec((tm, 2 * dff), rev)
    half = pl.BlockSpec((tm, dff), rev)
    cw3 = pl.BlockSpec((3, 2 * dff), lambda i: (0, 0))
    cb1 = pl.BlockSpec((1, 2 * dff), lambda i: (0, 0))
    vshape = jax.ShapeDtypeStruct((1, D), F32)
    return _pcall(body, name="ffn_bwd", grid=(nrow,),
                  in_specs=[row, _resident(wd.shape), wide, half, half, _resident(cw.shape), _resident(w_up4.shape),
                            row, row, vec, vec, vec, row],
                  out_specs=[wide, cw3, cb1, row, vec, vec, vec, row, vec],
                  out_shape=[jax.ShapeDtypeStruct((T, 2 * dff), BF), jax.ShapeDtypeStruct((3, 2 * dff), F32),
                             jax.ShapeDtypeStruct((1, 2 * dff), F32), jax.ShapeDtypeStruct((T, D), F32),
                             vshape, vshape, vshape, jax.ShapeDtypeStruct((T, D), BF), vshape],
                  scratch=[pltpu.VMEM((SUBLANES, 2 * dff), F32)], vmem_mb=60)(
        dfo, wd, up, ga, vd, cw, w_up4, x2, resid, g, scale, gate, o)


def _mm_tn_cols(a, b, name, nshard, nb, mb=None, tm=TN_ROWS):
    T, M = a.shape
    tm = min(tm, T)
    mb = M if mb is None else mb
    ns = b.shape[1] // nshard
    per = ns // nb
    nk = T // tm
    vmem_mb = (2 * 2 * tm * (mb + nb) + 2 * (4 + 2) * mb * nb) // 2 ** 20 + 8

    def body(a_ref, b_ref, o_ref, c_ref):
        k = pl.program_id(2)

        @pl.when(k == 0)
        def _():
            o_ref[...] = jnp.zeros_like(o_ref)

        o_ref[0] += _dot_tn(a_ref[...], b_ref[...])

        @pl.when(k == nk - 1)
        def _():
            c_ref[...] = o_ref[...].astype(BF)

    out_spec = pl.BlockSpec((1, mb, nb), lambda m, t, k: (t // per, m, t % per))
    return _pcall(body, name=name, grid=(M // mb, nshard * per, nk),
                  in_specs=[pl.BlockSpec((tm, mb), lambda m, t, k: (k, m)),
                            pl.BlockSpec((tm, nb), lambda m, t, k: (k, t))],
                  out_specs=[out_spec, out_spec],
                  out_shape=[jax.ShapeDtypeStruct((nshard, M, ns), F32), jax.ShapeDtypeStruct((nshard, M, ns), BF)],
                  vmem_mb=vmem_mb)(a, b)


def _mm_nt_normbwd(dz, w4, x, resid, g, scale, name, tm=256):
    T = x.shape[0]
    tm = min(tm, T)
    ns = w4.shape[2]

    def body(dz_ref, w_ref, x_ref, r_ref, g_ref, sc_ref, dx_ref, dsh_ref, dsc_ref, dg_ref):
        i = pl.program_id(0)

        @pl.when(i == 0)
        def _():
            dsh_ref[...] = jnp.zeros_like(dsh_ref)
            dsc_ref[...] = jnp.zeros_like(dsc_ref)
            dg_ref[...] = jnp.zeros_like(dg_ref)

        dh = None
        for j in range(N_CHIPS):
            part = _dot_nt(dz_ref[:, j * ns:(j + 1) * ns], w_ref[j])
            dh = part if dh is None else dh + part
        rstd, xh = _rms_stats(x_ref[...])
        dsh_ref[...] += _colsum(dh)
        dsc_ref[...] += _colsum(dh * (xh * g_ref[...]))
        dn = dh * (1.0 + sc_ref[...])
        dg_ref[...] += _colsum(dn * xh)
        dxh = dn * g_ref[...]
        dx_ref[...] = r_ref[...] + rstd * (dxh - xh * jnp.mean(dxh * xh, axis=-1, keepdims=True))

    row = pl.BlockSpec((tm, D), lambda i: (i, 0))
    vec = pl.BlockSpec((1, D), lambda i: (0, 0))
    return _pcall(body, name=name, grid=(T // tm,),
                  in_specs=[pl.BlockSpec((tm, N_CHIPS * ns), lambda i: (i, 0)), _resident(w4.shape), row, row, vec, vec],
                  out_specs=[row, vec, vec, vec],
                  out_shape=[jax.ShapeDtypeStruct((T, D), F32)] + [jax.ShapeDtypeStruct((1, D), F32)] * 3,
                  vmem_mb=48)(dz, w4, x, resid, g, scale)


def _mix_bwd(do, ya, yb, z, wo, wba, wbb, tm=512):
    T = do.shape[0]
    tm = min(tm, T)

    def body(do_ref, ya_ref, yb_ref, ga_ref, gb_ref, wo_ref, wa_ref, wb_ref,
             dz_ref, dya_ref, dyb_ref, dyap_ref, dybp_ref):
        dm = _dot_nt(do_ref[...], wo_ref[...])
        sa = _sigmoid_t(ga_ref[...])
        sb = _sigmoid_t(gb_ref[...])
        dya = (sa * dm).astype(BF)
        dyb = (sb * dm).astype(BF)
        dz_ref[:, 0:D] = (dm * ya_ref[...].astype(F32) * sa * (1.0 - sa)).astype(BF)
        dz_ref[:, D:2 * D] = (dm * yb_ref[...].astype(F32) * sb * (1.0 - sb)).astype(BF)
        dya_ref[...] = dya
        dyb_ref[...] = dyb
        dyap_ref[...] = _dot_nt(dya, wa_ref[...]).astype(BF)
        dybp_ref[...] = _dot_nt(dyb, wb_ref[...]).astype(BF)

    row = pl.BlockSpec((tm, D), lambda i: (i, 0))
    wspec = pl.BlockSpec((D, D), lambda i: (0, 0))
    return _pcall(body, name="mix_bwd", grid=(T // tm,),
                  in_specs=[row, row, row, pl.BlockSpec((tm, D), lambda i: (i, 4)),
                            pl.BlockSpec((tm, D), lambda i: (i, 5)), wspec, wspec, wspec],
                  out_specs=[pl.BlockSpec((tm, 2 * D), lambda i: (i, 2)), row, row, row, row],
                  out_shape=[jax.ShapeDtypeStruct((T, 6 * D), BF)] + [jax.ShapeDtypeStruct((T, D), BF)] * 4,
                  vmem_mb=48)(do, ya, yb, z, z, wo, wba, wbb)


def _sgu_bwd(dz, dyb_pre, z, lg, lb, ws, bst, tb=512):
    T = z.shape[0]
    tb = min(tb, T)

    def body(dz_in, dy_ref, zu_ref, zv_ref, lg_ref, lb_ref, ws_ref, bst_ref,
             dz_ref, dws_ref, dbst_ref, dlg_ref, dlb_ref):
        del dz_in
        i = pl.program_id(0)

        @pl.when(i == 0)
        def _():
            dws_ref[...] = jnp.zeros_like(dws_ref)
            dbst_ref[...] = jnp.zeros_like(dbst_ref)
            dlg_ref[...] = jnp.zeros_like(dlg_ref)
            dlb_ref[...] = jnp.zeros_like(dlb_ref)

        gu, dgu = _gelu_and_grad(zu_ref[...])
        gv, dgv = _gelu_and_grad(zv_ref[...])
        rstd, xh = _layernorm_stats(gv)
        vln = xh * lg_ref[...] + lb_ref[...]
        wm, mixed = _sgu_mix(vln, ws_ref, bst_ref, tb)
        dy = dy_ref[...].astype(F32)
        dz_ref[:, 0:D] = (dy * mixed * dgu).astype(BF)
        dmixed = dy * gu
        ri = lax.broadcasted_iota(jnp.int32, (SGU_BLOCK, SGU_BLOCK), 0)
        ci = lax.broadcasted_iota(jnp.int32, (SGU_BLOCK, SGU_BLOCK), 1)
        blocks = []
        for blk in range(tb // SGU_BLOCK):
            rs = slice(blk * SGU_BLOCK, (blk + 1) * SGU_BLOCK)
            cols = []
            for g in range(HEADS):
                cs = slice(g * HD, (g + 1) * HD)
                dmg = dmixed[rs, cs]
                dmb = dmg.astype(BF)
                dbst_ref[:, g:g + 1] += jnp.sum(dmg, axis=1, keepdims=True)
                dws_ref[g] += jnp.where(ri >= ci, _dot_nt(dmb, vln[rs, cs].astype(BF)), 0.0)
                cols.append(_dot_tn(wm[g], dmb))
            blocks.append(jnp.concatenate(cols, axis=1))
        dvln = blocks[0] if len(blocks) == 1 else jnp.concatenate(blocks, axis=0)
        dlg_ref[...] += _colsum(dvln * xh)
        dlb_ref[...] += _colsum(dvln)
        dxh = dvln * lg_ref[...]
        dgv_in = rstd * (dxh - jnp.mean(dxh, axis=-1, keepdims=True)
                         - xh * jnp.mean(dxh * xh, axis=-1, keepdims=True))
        dz_ref[:, D:2 * D] = (dgv_in * dgv).astype(BF)

    row = pl.BlockSpec((tb, D), lambda i: (i, 0))
    vec = pl.BlockSpec((1, D), lambda i: (0, 0))
    wspec = pl.BlockSpec((HEADS, SGU_BLOCK, SGU_BLOCK), lambda i: (0, 0, 0))
    bspec = pl.BlockSpec((SGU_BLOCK, HEADS), lambda i: (0, 0))
    return _pcall(body, name="sgu_bwd", grid=(T // tb,),
                  in_specs=[HBM_SPEC, row, pl.BlockSpec((tb, D), lambda i: (i, 2)),
                            pl.BlockSpec((tb, D), lambda i: (i, 3)), vec, vec, wspec, bspec],
                  out_specs=[pl.BlockSpec((tb, 2 * D), lambda i: (i, 1)), wspec, bspec, vec, vec],
                  out_shape=[jax.ShapeDtypeStruct(dz.shape, BF),
                             jax.ShapeDtypeStruct((HEADS, SGU_BLOCK, SGU_BLOCK), F32),
                             jax.ShapeDtypeStruct((SGU_BLOCK, HEADS), F32),
                             jax.ShapeDtypeStruct((1, D), F32), jax.ShapeDtypeStruct((1, D), F32)],
                  aliases={0: 0}, vmem_mb=48)(dz, dyb_pre, z, z, lg, lb, ws, bst)


def _rglru_bwd(dz, dya_pre, z, h, xc_s, r_s, ig_s, mult_s, a_s, cw, wa, wx, lam, tb=256):
    T = z.shape[0]
    tb = min(tb, T)
    nrow = T // tb
    per = tb // SUBLANES

    def body(dz_in, dy_ref, xr_ref, gr_ref, h_ref, hh_ref, xc_ref, r_ref, ig_ref, mu_ref, a_ref, cw_ref, wa_ref,
             wx_ref, lam_ref, dz_ref, dcw_ref, dcb_ref, dwa_ref, dba_ref, dwx_ref, dbx_ref, dlam_ref, carry, nxt):
        del dz_in
        i = pl.program_id(0)
        first_block = i == nrow - 1

        @pl.when(i == 0)
        def _():
            carry[...] = jnp.zeros_like(carry)
            nxt[...] = jnp.zeros_like(nxt)
            for ref in (dcw_ref, dcb_ref, dwa_ref, dba_ref, dwx_ref, dbx_ref, dlam_ref):
                ref[...] = jnp.zeros_like(ref)

        xc = xc_ref[...].astype(F32)
        r = r_ref[...].astype(F32)
        ig = ig_ref[...].astype(F32)
        mult = mu_ref[...].astype(F32)
        a = a_ref[...]
        lam = lam_ref[...]
        ls = _log_sigmoid(lam)
        hv = h_ref[...]
        hprev = _shift_down(hv, jnp.where(first_block, 0.0, hh_ref[...]), 1)
        gg, dgg = _gelu_and_grad(gr_ref[...])
        dy = dy_ref[...].astype(F32)
        dz_ref[:, D:2 * D] = (dy * hv * dgg).astype(BF)

        rows = lax.broadcasted_iota(jnp.int32, (tb, D), 0)
        v = dy * gg + jnp.where(rows == tb - 1, carry[0:1, :], 0.0)
        q = jnp.where(rows < tb - 1, pltpu.roll(a, tb - 1, 0), 0.0)
        _, gsc = _scan_rows(q, v, reverse=True)
        carry[...] = (a * gsc)[0:SUBLANES]

        xi = ig * xc
        dmult = gsc * xi
        dxi = gsc * mult
        dig = dxi * xc
        dxc = dxi * ig
        dlog_a = gsc * hprev * a - dmult * (a * a) * pl.reciprocal(mult, approx=True)
        dlam_ref[...] += _colsum(dlog_a * r) * (LRU_C * _sigmoid(-lam))
        dpr = dlog_a * (LRU_C * ls) * r * (1.0 - r)
        dpi = dig * ig * (1.0 - ig)
        dba_ref[...] += _colsum(dpr)
        dbx_ref[...] += _colsum(dpi)
        back = []
        for hh in range(HEADS):
            cs = slice(hh * HD, (hh + 1) * HD)
            xh = xc[:, cs].astype(BF)
            dprh = dpr[:, cs].astype(BF)
            dpih = dpi[:, cs].astype(BF)
            dwa_ref[hh] += _dot_tn(xh, dprh)
            dwx_ref[hh] += _dot_tn(xh, dpih)
            back.append(_dot_nt(dprh, wa_ref[hh].astype(BF)) + _dot_nt(dpih, wx_ref[hh].astype(BF)))
        dxc = dxc + jnp.concatenate(back, axis=1)

        n8 = nxt[...]
        d1 = _shift_up(dxc, n8, 1)
        d2 = _shift_up(dxc, n8, 2)
        d3 = _shift_up(dxc, n8, 3)
        nxt[...] = dxc[0:SUBLANES]
        dz_ref[:, 0:D] = (cw_ref[3:4, :] * dxc + cw_ref[2:3, :] * d1 + cw_ref[1:2, :] * d2
                          + cw_ref[0:1, :] * d3).astype(BF)
        xr = xr_ref[...]
        dcw_ref[3:4, :] += _colsum(dxc * xr)
        dcw_ref[2:3, :] += _colsum(d1 * xr)
        dcw_ref[1:2, :] += _colsum(d2 * xr)
        dcw_ref[0:1, :] += _colsum(d3 * xr)
        dcb_ref[...] += _colsum(dxc)

    rev = lambda col: (lambda i: (nrow - 1 - i, col))
    row = pl.BlockSpec((tb, D), rev(0))
    halo = pl.BlockSpec((SUBLANES, D), lambda i: (jnp.maximum((nrow - 1 - i) * per - 1, 0), 0))
    vec = pl.BlockSpec((1, D), lambda i: (0, 0))
    wspec = pl.BlockSpec((HEADS, HD, HD), lambda i: (0, 0, 0))
    c4 = pl.BlockSpec((4, D), lambda i: (0, 0))
    wshape = jax.ShapeDtypeStruct((HEADS, HD, HD), F32)
    vshape = jax.ShapeDtypeStruct((1, D), F32)
    return _pcall(body, name="rglru_bwd", grid=(nrow,),
                  in_specs=[HBM_SPEC, row, row, pl.BlockSpec((tb, D), rev(1)), row, halo,
                            row, row, row, row, row, c4, wspec, wspec, vec],
                  out_specs=[pl.BlockSpec((tb, 2 * D), rev(0)), c4, vec, wspec, vec, wspec, vec, vec],
                  out_shape=[jax.ShapeDtypeStruct(dz.shape, BF), jax.ShapeDtypeStruct((4, D), F32), vshape,
                             wshape, vshape, wshape, vshape, vshape],
                  scratch=[pltpu.VMEM((SUBLANES, D), F32), pltpu.VMEM((SUBLANES, D), F32)],
                  aliases={0: 0}, vmem_mb=56)(dz, dya_pre, z, z, h, h, xc_s, r_s, ig_s, mult_s, a_s, cw, wa, wx, lam)


def _pack_rows(parts):
    out = []
    for p in parts:
        q = p.reshape(-1, LANES)
        pad = (-q.shape[0]) % SUBLANES
        if pad:
            q = jnp.concatenate([q, jnp.zeros((pad, LANES), q.dtype)], axis=0)
        out.append(q)
    return jnp.concatenate(out, axis=0)


def _rows_of(shape):
    n = 1
    for s in shape:
        n *= s
    rows = n // LANES
    return rows + (-rows) % SUBLANES


def kernel(x, c, w_ada, b_ada, norm_mix_g, w_in, rnn_conv_w, rnn_conv_b, lru_w_a, lru_b_a, lru_w_x, lru_b_x, lru_lambda, sgu_ln_g, sgu_ln_b, sgu_w_s, sgu_b_s, w_branch_a, w_branch_b, w_out, norm_ffn_g, w_up, ffn_conv_w, ffn_conv_b, w_down, norm_final_g, loss_target, m_w_ada, m_b_ada, m_norm_mix_g, m_w_in, m_rnn_conv_w, m_rnn_conv_b, m_lru_w_a, m_lru_b_a, m_lru_w_x, m_lru_b_x, m_lru_lambda, m_sgu_ln_g, m_sgu_ln_b, m_sgu_w_s, m_sgu_b_s, m_w_branch_a, m_w_branch_b, m_w_out, m_norm_ffn_g, m_w_up, m_ffn_conv_w, m_ffn_conv_b, m_w_down, m_norm_final_g, v_w_ada, v_b_ada, v_norm_mix_g, v_w_in, v_rnn_conv_w, v_rnn_conv_b, v_lru_w_a, v_lru_b_a, v_lru_w_x, v_lru_b_x, v_lru_lambda, v_sgu_ln_g, v_sgu_ln_b, v_sgu_w_s, v_sgu_b_s, v_w_branch_a, v_w_branch_b, v_w_out, v_norm_ffn_g, v_w_up, v_ffn_conv_w, v_ffn_conv_b, v_w_down, v_norm_final_g):
    args = dict(locals())
    T = x.shape[1]
    mx, my, mc = lax.axis_index("x"), lax.axis_index("y"), lax.axis_index("c")
    chip = 2 * mx + my
    dev = 2 * chip + mc
    vec = lambda a: a.reshape(1, -1)

    xt = x.reshape(T, D)
    tgt = loss_target.reshape(T, D)
    ns = w_in.shape[2]
    dff = w_down.shape[1] * N_CHIPS

    c_all = _gather8(c.reshape(SUBLANES, LANES), "gather_c").reshape(N_DEV, D)
    b_ada_sh = lax.dynamic_slice(b_ada, (0, chip * ns), (1, ns))
    mod_sh = _mod_fwd(c_all, w_ada[0], b_ada_sh)

    mixer_w = _cast_shards([w_in[0], w_branch_a[0], w_branch_b[0], w_out[0]], "cast_mixer_weights")
    w_in4, wba4, wbb4, wo4, rcw4, fcw4, mod4 = _gather_weights(
        list(mixer_w) + [rnn_conv_w[0], ffn_conv_w[0], mod_sh], [True] * 4 + [False] * 3)
    late = _cast_shards([w_up[0], w_down[0]], "cast_late", after=mod4)
    late_plan = _gather_plan(len(late))
    late_handle, late_token = _remote_start(
        late, [lax.empty((N_CHIPS,) + w.shape, w.dtype) for w in late], late_plan, 3 * len(late), "gather_late_start")
    rcw_full = jnp.transpose(rcw4, (1, 0, 2)).reshape(4, D)
    fcw_full = jnp.transpose(fcw4, (1, 0, 2)).reshape(3, 2 * dff)
    mod = lax.dynamic_index_in_dim(mod4, dev, axis=1, keepdims=False).reshape(1, 6 * D)
    shift1, scale1, gate1, shift2, scale2, gate2 = [mod[:, k * D:(k + 1) * D] for k in range(6)]

    bst = jnp.transpose(sgu_b_s[0])
    wba_full = wba4.reshape(D, D)
    wbb_full = wbb4.reshape(D, D)
    wo_full = wo4.reshape(D, D)
    h1, z, h_lru, ya_pre, yb_pre, merged, ya, yb, o1, x2, lru_xc, lru_r, lru_i, lru_mult, lru_a = _mixer_fwd(
        xt, norm_mix_g, scale1 + late_token[0:1, 0:1], shift1, gate1, w_in4, rcw_full, rnn_conv_b,
        lru_w_a[0], lru_b_a, lru_w_x[0], lru_b_x, lru_lambda, sgu_ln_g, sgu_ln_b, sgu_w_s[0], bst,
        wba_full, wbb_full, wo_full)
    late, late_lands = _remote_wait(late_handle, late_plan, o1, "gather_late_wait")
    w_up4, w_down4 = _place_own(late, late_lands)
    wd_full = w_down4.reshape(dff, D)
    h2, up, f, ffn_ga, ffn_vd, loss_part, dx3, dfo, dgf, dgate2 = _ffn_fwd(
        x2, norm_ffn_g, scale2, shift2, gate2, vec(norm_final_g), w_up4, wd_full, fcw_full, ffn_conv_b, tgt)

    dup, dfcw, dfcb, dx2, dshift2, dscale2, dg_ffn, do1, dgate1 = _ffn_bwd(
        dfo, wd_full, up, ffn_ga, ffn_vd, fcw_full, w_up4, x2, dx3, norm_ffn_g, scale2, gate1, o1)
    dwd = _mm_tn_cols(f, dfo, "dw_down", 1, D, mb=D, tm=TN_ROWS_SQUARE)
    dw_up4 = _mm_tn_cols(h2, dup, "dw_up", N_CHIPS, ns)
    dz, dya, dyb, dya_pre, dyb_pre = _mix_bwd(do1, ya, yb, z, wo_full, wba_full, wbb_full)
    dwo = _mm_tn_cols(merged, do1, "dw_out", 1, D)
    dwba = _mm_tn_cols(ya_pre, dya, "dw_branch_a", 1, D)
    dwbb = _mm_tn_cols(yb_pre, dyb, "dw_branch_b", 1, D)

    chip_id = chip.astype(jnp.int32).reshape(1)

    def reduce_start(group, name):
        wire = [g16.reshape(N_CHIPS, -1, g16.shape[-1]) for _, (_, g16) in group]
        lands = [lax.empty((3,) + w.shape[1:], w.dtype) for w in wire]
        return _remote_start(wire, lands, _scatter_plan(len(group)), 3 * len(group), "scatter_start_" + name)

    def reduce_finish(group, handle, after, name):
        _, landed = _remote_wait(handle, _scatter_plan(len(group)), after, "scatter_wait_" + name)
        return [_sum_own_and_landed(chip_id, g32.reshape(N_CHIPS, -1, g32.shape[-1]), l, "sum_chips_" + n)
                for (n, (g32, _)), l in zip(group, landed)]

    group1 = [("w_up", dw_up4), ("w_down", dwd), ("w_branch_a", dwba), ("w_branch_b", dwbb), ("w_out", dwo)]
    handle1, token1 = reduce_start(group1, "late")
    dz, dws, dbst, dlg, dlb = _sgu_bwd(dz, dyb_pre, z, sgu_ln_g + token1[0:1, 0:1], sgu_ln_b, sgu_w_s[0], bst)
    dz, drcw, drcb, dwa, dba, dwx, dbx, dlam = _rglru_bwd(
        dz, dya_pre, z, h_lru, lru_xc, lru_r, lru_i, lru_mult, lru_a, rcw_full, lru_w_a[0], lru_w_x[0], lru_lambda)
    early_small = [("rnn_conv_b", drcb), ("lru_w_a", dwa), ("lru_b_a", dba), ("lru_w_x", dwx), ("lru_b_x", dbx),
                   ("lru_lambda", dlam), ("sgu_ln_g", dlg), ("sgu_ln_b", dlb), ("sgu_w_s", dws),
                   ("sgu_b_s", jnp.transpose(dbst)), ("norm_ffn_g", dg_ffn),
                   ("ffn_conv_b", dfcb), ("norm_final_g", dgf)]
    r_early = sum(_rows_of(args[n].shape) for n, _ in early_small)
    early_pack = _pack_rows([g for _, g in early_small] + [drcw, dfcw])
    early_pack = jnp.concatenate(
        [early_pack, jnp.zeros(((-early_pack.shape[0]) % 256, LANES), F32)], axis=0)
    early_chip = _add_pair(early_pack, _swap_cores([early_pack], "swap_small_grads")[0], "sum_cores_small_grads")
    early_handle, token3 = _remote_start([early_chip], [lax.empty((3,) + early_chip.shape, F32)], _bcast_plan, 3,
                                         "small_grads_start")
    def swap_start(totals, name):
        lands = [lax.empty(t.shape, t.dtype) for t in totals]
        return _remote_start(totals, lands, _sibling_plan(len(totals)), len(totals), "swap_sums_start_" + name)

    out = {}

    def swap_finish(group, handle, after, name):
        mine, theirs = _remote_wait(handle, _sibling_plan(len(group)), after, "swap_sums_wait_" + name)
        for (n, _), a, b in zip(group, mine, theirs):
            shape = args[n].shape
            res = _adamw(args[n][0], args["m_" + n][0], args["v_" + n][0], [a, b], "adamw_" + n)
            for kind, r in zip(("grad_", "delta_", "new_m_", "new_v_"), res):
                out[kind + n] = r.reshape(shape)
        return res[3]

    swap1, token4 = swap_start(reduce_finish(group1, handle1, drcb, "late"), "late")
    group2 = [("w_in", _mm_tn_cols(h1, dz, "dw_in", N_CHIPS, ns))]
    handle2, token2 = reduce_start(group2, "in")
    tokens = token2[0:1, 0:1] + token3[0:1, 0:1] + token4[0:1, 0:1]
    grad_x, dshift1, dscale1, dg_mix = _mm_nt_normbwd(
        dz, w_in4, xt, dx2, norm_mix_g + tokens, scale1, "dh1_norm_bwd")
    swap2, token5 = swap_start(reduce_finish(group2, handle2, dg_mix, "in"), "in")
    dmod = jnp.concatenate([dshift1, dscale1, dgate1, dshift2, dscale2, dgate2], axis=1)
    last = swap_finish(group1, swap1, token5, "late")
    swap_finish(group2, swap2, last, "in")

    late_small = [("b_ada", dmod), ("norm_mix_g", dg_mix)]
    small = late_small + early_small
    late_all = _gather8(_pack_rows([g for _, g in late_small] + [loss_part]), "gather_late_small_grads")
    late_sum = _sum_parts(late_all, "sum_late_small_grads")
    r_late = sum(_rows_of(args[n].shape) for n, _ in late_small)
    loss = late_sum[r_late, 0]
    late_sum = late_sum[:r_late]
    _, (early_landed,) = _remote_wait(early_handle, _bcast_plan, dg_mix, "small_grads_wait")
    early_sum = _sum_chips_in_order(chip_id, early_chip, early_landed, "sum_early_small_grads")
    r_small = sum(_rows_of(args[n].shape) for n, _ in small)
    r_pad = r_small + (-r_small) % 256
    fill = jnp.zeros((r_pad - r_small, LANES), F32)
    g_small = jnp.concatenate([late_sum, early_sum[:r_early], fill], axis=0)

    def pack_small(prefix):
        return jnp.concatenate([_pack_rows([args[prefix + n] for n, _ in small]), fill], axis=0)

    res = _adamw(pack_small(""), pack_small("m_"), pack_small("v_"), [g_small], "adamw_small")
    off = 0
    for n, _ in small:
        shape = args[n].shape
        rows = _rows_of(shape)
        for kind, r in zip(("grad_", "delta_", "new_m_", "new_v_"), res):
            out[kind + n] = r[off:off + rows].reshape(shape)
        off += rows

    rcw_cols = rnn_conv_w.shape[2]
    g_rcw = lax.dynamic_slice(early_sum[r_early:r_early + 32].reshape(4, D), (0, chip * rcw_cols), (4, rcw_cols))
    g_fcw = lax.dynamic_slice(early_sum[r_early + 32:r_early + 32 + 144].reshape(3, 2 * dff), (0, chip * ns), (3, ns))
    conv = [("rnn_conv_w", g_rcw), ("ffn_conv_w", g_fcw)]
    res = _adamw(_pack_rows([args[n] for n, _ in conv]), _pack_rows([args["m_" + n] for n, _ in conv]),
                 _pack_rows([args["v_" + n] for n, _ in conv]), [_pack_rows([g for _, g in conv])], "adamw_conv")
    off = 0
    for n, _ in conv:
        shape = args[n].shape
        cnt = shape[1] * shape[2] // LANES
        for kind, r in zip(("grad_", "delta_", "new_m_", "new_v_"), res):
            out[kind + n] = r[off:off + cnt].reshape(shape)
        off += _rows_of(shape)

    dmod_all = late_all[:, 0:6 * D // LANES, :].reshape(N_DEV, 6 * D)
    dmod_sh = lax.dynamic_slice(dmod_all, (0, chip * ns), (N_DEV, ns))
    res = _ada_adamw(jnp.transpose(c_all), dmod_sh, w_ada[0], m_w_ada[0], v_w_ada[0])
    for kind, r in zip(("grad_", "delta_", "new_m_", "new_v_"), res):
        out[kind + "w_ada"] = r.reshape(w_ada.shape)

    names = ["w_ada", "b_ada", "norm_mix_g", "w_in", "rnn_conv_w", "rnn_conv_b", "lru_w_a", "lru_b_a", "lru_w_x",
             "lru_b_x", "lru_lambda", "sgu_ln_g", "sgu_ln_b", "sgu_w_s", "sgu_b_s", "w_branch_a", "w_branch_b",
             "w_out", "norm_ffn_g", "w_up", "ffn_conv_w", "ffn_conv_b", "w_down", "norm_final_g"]
    result = [loss, grad_x.reshape(x.shape)]
    for kind in ("grad_", "delta_", "new_m_", "new_v_"):
        result += [out[kind + n] for n in names]
    return tuple(result)
```

```python
import jax
import jax.numpy as jnp
from jax import lax
from jax.experimental import pallas as pl
from jax.experimental.pallas import tpu as pltpu

F32 = jnp.float32
BF = jnp.bfloat16

D = 1024
HEADS = 8
HD = D // HEADS
SGU_BLOCK = 128
N_CHIPS = 4
N_DEV = 8
EPS = 1e-6
LRU_C = 8.0
LANES = 128
SUBLANES = 8
ELEMENTWISE_BLOCK_BYTES = 3 << 19
TN_ROWS = 2048
TN_ROWS_SQUARE = 4096

ADAM_LR = 0.001
ADAM_B1 = 0.9
ADAM_B2 = 0.999
ADAM_EPS = 1e-08
ADAM_WD = 0.01
ADAM_STEP = 10

GELU_K0 = 0.7978845608028654
GELU_K1 = 0.044715

HBM_SPEC = pl.BlockSpec(memory_space=pltpu.HBM)
MESH_ID = pl.DeviceIdType.MESH


def _pcall(body, *, name, out_shape, grid=(), in_specs=None, out_specs=None, scratch=(), vmem_mb=32, aliases=None,
           grid_spec=None):
    kw = {}
    if aliases:
        kw["input_output_aliases"] = aliases
    if grid_spec is not None:
        kw["grid_spec"] = grid_spec
        ndim = len(grid_spec.grid)
    else:
        kw.update(grid=grid, in_specs=in_specs, out_specs=out_specs, scratch_shapes=list(scratch))
        ndim = len(grid)
    if ndim:
        params = pltpu.CompilerParams(dimension_semantics=("arbitrary",) * ndim, vmem_limit_bytes=vmem_mb * 2 ** 20)
    else:
        params = pltpu.CompilerParams(vmem_limit_bytes=vmem_mb * 2 ** 20)
    return pl.pallas_call(body, name=name, out_shape=out_shape, compiler_params=params, **kw)


def _gelu_cdf(x, x2):
    return 0.5 * jnp.tanh(x * (GELU_K0 + (GELU_K0 * GELU_K1) * x2)) + 0.5


def _gelu(x):
    return x * _gelu_cdf(x, x * x)


def _gelu_and_grad(x):
    x2 = x * x
    s = _gelu_cdf(x, x2)
    g = x * s
    dg = s * (1.0 + (x - g) * ((2.0 * GELU_K0) + (6.0 * GELU_K0 * GELU_K1) * x2))
    return g, dg


def _sigmoid(x):
    return 1.0 / (1.0 + jnp.exp(-x))


def _sigmoid_t(x):
    return 0.5 * jnp.tanh(0.5 * x) + 0.5


def _log_sigmoid(x):
    e = jnp.exp(-jnp.abs(x))
    u = 1.0 + e
    d = u - 1.0
    l1p = jnp.where(d == 0.0, e, jnp.log(u) * (e / jnp.where(d == 0.0, 1.0, d)))
    return jnp.minimum(x, 0.0) - l1p


def _dot(a, b):
    return jnp.dot(a, b, preferred_element_type=F32)


def _dot_nt(a, b):
    return lax.dot_general(a, b, (((1,), (1,)), ((), ())), preferred_element_type=F32)


def _dot_tn(a, b):
    return lax.dot_general(a, b, (((0,), (0,)), ((), ())), preferred_element_type=F32)


def _shift_down(x, halo, s):
    r = pltpu.roll(x, s, 0)
    rows = lax.broadcasted_iota(jnp.int32, (SUBLANES, x.shape[1]), 0)
    head = jnp.where(rows < s, pltpu.roll(halo, s, 0), r[0:SUBLANES])
    return jnp.concatenate([head, r[SUBLANES:]], axis=0)


def _shift_up(x, halo, s):
    n = x.shape[0]
    r = pltpu.roll(x, n - s, 0)
    rows = lax.broadcasted_iota(jnp.int32, (SUBLANES, x.shape[1]), 0)
    tail = jnp.where(rows >= SUBLANES - s, pltpu.roll(halo, SUBLANES - s, 0), r[n - SUBLANES:n])
    return jnp.concatenate([r[:n - SUBLANES], tail], axis=0)


def _scan_rows(a, u, reverse):
    n, width = a.shape
    rows = lax.broadcasted_iota(jnp.int32, (n, width), 0)
    d = 1
    while d < n:
        if d < SUBLANES:
            keep = rows < n - d if reverse else rows >= d
            shift = n - d if reverse else d
            a_s = jnp.where(keep, pltpu.roll(a, shift, 0), 1.0)
            u_s = jnp.where(keep, pltpu.roll(u, shift, 0), 0.0)
        elif reverse:
            a_s = jnp.concatenate([a[d:], jnp.ones((d, width), a.dtype)], axis=0)
            u_s = jnp.concatenate([u[d:], jnp.zeros((d, width), u.dtype)], axis=0)
        else:
            a_s = jnp.concatenate([jnp.ones((d, width), a.dtype), a[:n - d]], axis=0)
            u_s = jnp.concatenate([jnp.zeros((d, width), u.dtype), u[:n - d]], axis=0)
        u = a * u_s + u
        a = a * a_s
        d *= 2
    return a, u


def _colsum(x):
    return jnp.sum(x, axis=0, keepdims=True)


def _rms_stats(x):
    r = lax.rsqrt(jnp.mean(x * x, axis=-1, keepdims=True) + EPS)
    return r, x * r


def _lru_gates(xc, wa_ref, ba, wx_ref, bx, lam, head0=0):
    pr, pi = [], []
    for hh in range(xc.shape[1] // HD):
        xh = xc[:, hh * HD:(hh + 1) * HD].astype(BF)
        pr.append(_dot(xh, wa_ref[head0 + hh].astype(BF)))
        pi.append(_dot(xh, wx_ref[head0 + hh].astype(BF)))
    r = _sigmoid_t((pr[0] if len(pr) == 1 else jnp.concatenate(pr, axis=1)) + ba)
    ig = _sigmoid_t((pi[0] if len(pi) == 1 else jnp.concatenate(pi, axis=1)) + bx)
    ls = _log_sigmoid(lam)
    log_a = LRU_C * r * ls
    a = jnp.exp(log_a)
    x2 = 2.0 * log_a
    u = a * a
    lu = jnp.log(jnp.maximum(u, 1e-37))
    ratio = x2 * pl.reciprocal(jnp.where(lu == 0.0, 1.0, lu), approx=True)
    em1 = jnp.where(lu == 0.0, x2, jnp.where(u < 1e-30, -1.0, (u - 1.0) * ratio))
    mult = jnp.sqrt(-em1)
    return r, ig, ls, a, mult


def _sgu_mix(vln, ws_ref, bst_ref, tb):
    ri = lax.broadcasted_iota(jnp.int32, (SGU_BLOCK, SGU_BLOCK), 0)
    ci = lax.broadcasted_iota(jnp.int32, (SGU_BLOCK, SGU_BLOCK), 1)
    wm = [jnp.where(ri >= ci, ws_ref[g], 0.0).astype(BF) for g in range(HEADS)]
    blocks = []
    for blk in range(tb // SGU_BLOCK):
        cols = []
        for g in range(HEADS):
            vb = vln[blk * SGU_BLOCK:(blk + 1) * SGU_BLOCK, g * HD:(g + 1) * HD].astype(BF)
            cols.append(_dot(wm[g], vb) + bst_ref[:, g:g + 1])
        blocks.append(jnp.concatenate(cols, axis=1))
    mixed = blocks[0] if len(blocks) == 1 else jnp.concatenate(blocks, axis=0)
    return wm, mixed


def _layernorm_stats(v):
    mu = jnp.mean(v, axis=-1, keepdims=True)
    vc = v - mu
    rstd = lax.rsqrt(jnp.mean(vc * vc, axis=-1, keepdims=True) + EPS)
    return rstd, vc * rstd


def _my_xyc():
    return lax.axis_index("x"), lax.axis_index("y"), lax.axis_index("c")


def _gather_weights(srcs, halve):
    n = len(srcs)
    out_shape = [jax.ShapeDtypeStruct((N_CHIPS,) + s.shape, s.dtype) for s in srcs]

    def body(*refs):
        src, out = refs[:n], refs[n:2 * n]
        send_sems, recv_sems, fwd_send, fwd_recv, loc_sems = refs[2 * n:]
        x, y, c = _my_xyc()
        me = 2 * x + y
        chips = [(1 - x, y), (x, 1 - y), (1 - x, 1 - y)]

        def half(ref, a, which):
            if not halve[a]:
                return ref
            h = srcs[a].shape[0] // 2
            return ref.at[pl.ds(which * h, h)]

        def ici(a, k, frm):
            px, py = chips[k]
            return pltpu.make_async_remote_copy(
                src_ref=half(src[a], a, c), dst_ref=half(out[a].at[frm], a, c),
                send_sem=send_sems.at[a, k], recv_sem=recv_sems.at[a, k],
                device_id=(px, py, c), device_id_type=MESH_ID)

        def d2d(a, k, which):
            px, py = chips[k]
            rows = half(out[a].at[2 * px + py], a, which)
            return pltpu.make_async_remote_copy(
                src_ref=rows, dst_ref=rows, send_sem=fwd_send.at[a, k], recv_sem=fwd_recv.at[a, k],
                device_id=(x, y, 1 - c), device_id_type=MESH_ID)

        local, sends = [], []
        for a in range(n):
            lc = pltpu.make_async_copy(src[a], out[a].at[me], loc_sems.at[a])
            lc.start()
            local.append(lc)
            for k in range(3):
                cp = ici(a, k, me)
                cp.start()
                sends.append(cp)
        for a in range(n):
            for k in range(3):
                px, py = chips[k]
                ici(a, k, 2 * px + py).wait_recv()
                if halve[a]:
                    fw = d2d(a, k, c)
                    fw.start()
                    sends.append(fw)
        for a in range(n):
            if halve[a]:
                for k in range(3):
                    d2d(a, k, 1 - c).wait_recv()
        for cp in sends:
            cp.wait_send()
        for lc in local:
            lc.wait()

    sem = pltpu.SemaphoreType.DMA((n, 3))
    return _pcall(body, name="gather_weights", out_shape=out_shape, in_specs=[HBM_SPEC] * n,
                  out_specs=[HBM_SPEC] * n, scratch=[sem, sem, sem, sem, pltpu.SemaphoreType.DMA((n,))])(*srcs)


SEM_SPEC = pl.BlockSpec(memory_space=pltpu.SEMAPHORE)


def _remote_start(srcs, lands, plan, ncopies, name):
    n, m = len(srcs), len(lands)

    def body(*refs):
        src, land = refs[:n], refs[n:n + m]
        send_sems, recv_sems = refs[n + m], refs[n + m + 1]
        token = refs[-1]
        x, y, c = _my_xyc()
        for i, (s, d, dev) in enumerate(plan(src, land, x, y, c)):
            pltpu.make_async_remote_copy(src_ref=s, dst_ref=d, send_sem=send_sems.at[i], recv_sem=recv_sems.at[i],
                                         device_id=dev, device_id_type=MESH_ID).start()
        token[...] = jnp.zeros_like(token)

    bufs = list(srcs) + list(lands)
    out = pl.pallas_call(
        body, name=name,
        out_shape=(pltpu.SemaphoreType.DMA((ncopies,)), pltpu.SemaphoreType.DMA((ncopies,)),
                   *[pltpu.HBM(b.shape, b.dtype) for b in bufs], jax.ShapeDtypeStruct((SUBLANES, LANES), F32)),
        in_specs=[HBM_SPEC] * (n + m),
        out_specs=(SEM_SPEC, SEM_SPEC, *[HBM_SPEC] * (n + m), pl.BlockSpec(memory_space=pltpu.VMEM)),
        input_output_aliases={i: 2 + i for i in range(n + m)},
        compiler_params=pltpu.CompilerParams(has_side_effects=pltpu.SideEffectType.DATAFLOW_SIDE_EFFECTING),
    )(*[pltpu.with_memory_space_constraint(b, pltpu.HBM) for b in bufs])
    return (out[0], out[1], out[2:2 + n], out[2 + n:2 + n + m]), out[-1]


def _remote_wait(handle, plan, after, name):
    send_sems, recv_sems, srcs, lands = handle
    n, m = len(srcs), len(lands)

    def body(*refs):
        src, land = refs[:n], refs[n:n + m]
        ssem, rsem = refs[n + m], refs[n + m + 1]
        x, y, c = _my_xyc()
        for i, (s, d, dev) in enumerate(plan(src, land, x, y, c)):
            cp = pltpu.make_async_remote_copy(src_ref=s, dst_ref=d, send_sem=ssem.at[i], recv_sem=rsem.at[i],
                                              device_id=dev, device_id_type=MESH_ID)
            cp.wait_send()
            cp.wait_recv()

    bufs = list(srcs) + list(lands)
    out = pl.pallas_call(
        body, name=name, out_shape=tuple(pltpu.HBM(b.shape, b.dtype) for b in bufs),
        in_specs=[HBM_SPEC] * (n + m) + [SEM_SPEC, SEM_SPEC, pl.BlockSpec(memory_space=pl.ANY)],
        out_specs=tuple([HBM_SPEC] * (n + m)), input_output_aliases={i: i for i in range(n + m)},
        compiler_params=pltpu.CompilerParams(has_side_effects=pltpu.SideEffectType.DATAFLOW_SIDE_EFFECTING),
    )(*bufs, send_sems, recv_sems, after)
    return out[:n], out[n:]


def _chips_of(x, y):
    return [(1 - x, y), (x, 1 - y), (1 - x, 1 - y)]


def _gather_plan(count):
    def plan(src, land, x, y, c):
        me = 2 * x + y
        return [(src[a], land[a].at[me], (px, py, c)) for a in range(count) for px, py in _chips_of(x, y)]

    return plan


def _place_own(srcs, lands):
    n = len(srcs)

    def body(*refs):
        src, land, sems = refs[:n], refs[2 * n:3 * n], refs[3 * n]
        me = 2 * lax.axis_index("x") + lax.axis_index("y")
        copies = [pltpu.make_async_copy(src[a], land[a].at[me], sems.at[a]) for a in range(n)]
        for cp in copies:
            cp.start()
        for cp in copies:
            cp.wait()

    return _pcall(body, name="place_own_shards", out_shape=[jax.ShapeDtypeStruct(l.shape, l.dtype) for l in lands],
                  in_specs=[HBM_SPEC] * (2 * n), out_specs=[HBM_SPEC] * n, aliases={n + a: a for a in range(n)},
                  scratch=[pltpu.SemaphoreType.DMA((n,))])(*srcs, *lands)


def _gather8(src, name):
    def body(src_ref, out_ref, send_sems, recv_sems, loc_sem):
        x, y, c = _my_xyc()
        me = 4 * x + 2 * y + c
        lc = pltpu.make_async_copy(src_ref, out_ref.at[me], loc_sem)
        lc.start()
        cps = []
        for k in range(1, N_DEV):
            px = 1 - x if (k >> 2) & 1 else x
            py = 1 - y if (k >> 1) & 1 else y
            pc = 1 - c if k & 1 else c
            cp = pltpu.make_async_remote_copy(
                src_ref=src_ref, dst_ref=out_ref.at[me], send_sem=send_sems.at[k - 1], recv_sem=recv_sems.at[k - 1],
                device_id=(px, py, pc), device_id_type=MESH_ID)
            cp.start()
            cps.append(cp)
        for cp in cps:
            cp.wait()
        lc.wait()

    return _pcall(body, name=name, out_shape=jax.ShapeDtypeStruct((N_DEV,) + src.shape, src.dtype),
                  in_specs=[HBM_SPEC], out_specs=HBM_SPEC,
                  scratch=[pltpu.SemaphoreType.DMA((N_DEV - 1,)), pltpu.SemaphoreType.DMA((N_DEV - 1,)),
                           pltpu.SemaphoreType.DMA])(src)


def _cast_shards(arrs, name, after=None):
    n = len(arrs)
    extra = [] if after is None else [after]

    def body(*refs):
        ins, outs = refs[:n], refs[n + len(extra):]
        for a in range(n):
            outs[a][...] = ins[a][...].astype(BF)

    specs = [pl.BlockSpec((s.shape[0] // 4, s.shape[1]), lambda i: (i, 0)) for s in arrs]
    return _pcall(body, name=name, grid=(4,), in_specs=specs + [pl.BlockSpec(memory_space=pl.ANY)] * len(extra),
                  out_specs=specs, out_shape=[jax.ShapeDtypeStruct(s.shape, BF) for s in arrs])(*arrs, *extra)


def _row_tile(rows, cols):
    t = rows
    while t * cols * 4 > ELEMENTWISE_BLOCK_BYTES and t % (2 * SUBLANES) == 0:
        t //= 2
    return t


def _sum_parts(parts, name):
    p, rows, cols = parts.shape
    tr = _row_tile(rows, cols * p // 2)

    def body(p_ref, o_ref):
        acc = p_ref[0].astype(F32)
        for k in range(1, p):
            acc = acc + p_ref[k].astype(F32)
        o_ref[...] = acc

    return _pcall(body, name=name, grid=(rows // tr,),
                  in_specs=[pl.BlockSpec((p, tr, cols), lambda i: (0, i, 0))],
                  out_specs=pl.BlockSpec((tr, cols), lambda i: (i, 0)),
                  out_shape=jax.ShapeDtypeStruct((rows, cols), F32), vmem_mb=48)(parts)


def _sum_own_and_landed(chip, sums, landed, name):
    _, rows, cols = sums.shape
    tr = _row_tile(rows, 2 * cols)

    def body(chip_ref, own_ref, land_ref, o_ref):
        del chip_ref
        acc = own_ref[0].astype(F32)
        for k in range(3):
            acc = acc + land_ref[k].astype(F32)
        o_ref[...] = acc

    grid_spec = pltpu.PrefetchScalarGridSpec(
        num_scalar_prefetch=1, grid=(rows // tr,),
        in_specs=[pl.BlockSpec((1, tr, cols), lambda i, chip_ref: (chip_ref[0], i, 0)),
                  pl.BlockSpec((3, tr, cols), lambda i, chip_ref: (0, i, 0))],
        out_specs=pl.BlockSpec((tr, cols), lambda i, chip_ref: (i, 0)))
    return _pcall(body, name=name, grid_spec=grid_spec, out_shape=jax.ShapeDtypeStruct((rows, cols), F32),
                  vmem_mb=48)(chip, sums, landed)


def _swap_cores(arrs, name):
    n = len(arrs)

    def body(*refs):
        src, out = refs[:n], refs[n:2 * n]
        send_sems, recv_sems = refs[2 * n:]
        x, y, c = _my_xyc()
        cps = []
        for a in range(n):
            cp = pltpu.make_async_remote_copy(
                src_ref=src[a], dst_ref=out[a], send_sem=send_sems.at[a], recv_sem=recv_sems.at[a],
                device_id=(x, y, 1 - c), device_id_type=MESH_ID)
            cp.start()
            cps.append(cp)
        for cp in cps:
            cp.wait()

    sem = pltpu.SemaphoreType.DMA((n,))
    return _pcall(body, name=name, out_shape=[jax.ShapeDtypeStruct(a.shape, a.dtype) for a in arrs],
                  in_specs=[HBM_SPEC] * n, out_specs=[HBM_SPEC] * n, scratch=[sem, sem])(*arrs)


def _add_pair(a, b, name):
    rows, cols = a.shape
    tr = _row_tile(rows, 2 * cols)

    def body(a_ref, b_ref, o_ref):
        o_ref[...] = a_ref[...] + b_ref[...]

    spec = pl.BlockSpec((tr, cols), lambda i: (i, 0))
    return _pcall(body, name=name, grid=(rows // tr,), in_specs=[spec, spec], out_specs=spec,
                  out_shape=jax.ShapeDtypeStruct((rows, cols), F32))(a, b)


def _sum_chips_in_order(chip, own, landed, name):
    rows, cols = own.shape
    tr = _row_tile(rows, 4 * cols)

    def body(chip_ref, own_ref, land_ref, o_ref):
        me = chip_ref[0]
        acc = None
        for p in range(N_CHIPS):
            q = p ^ me
            k = jnp.where(q == 2, 0, jnp.where(q == 1, 1, 2))
            term = jnp.where(q == 0, own_ref[...], land_ref[k])
            acc = term if acc is None else acc + term
        o_ref[...] = acc

    grid_spec = pltpu.PrefetchScalarGridSpec(
        num_scalar_prefetch=1, grid=(rows // tr,),
        in_specs=[pl.BlockSpec((tr, cols), lambda i, chip_ref: (i, 0)),
                  pl.BlockSpec((3, tr, cols), lambda i, chip_ref: (0, i, 0))],
        out_specs=pl.BlockSpec((tr, cols), lambda i, chip_ref: (i, 0)))
    return _pcall(body, name=name, grid_spec=grid_spec, out_shape=jax.ShapeDtypeStruct((rows, cols), F32))(
        chip, own, landed)


def _bcast_plan(src, land, x, y, c):
    return [(src[0], land[0].at[k], (px, py, c)) for k, (px, py) in enumerate(_chips_of(x, y))]


def _sibling_plan(count):
    def plan(src, land, x, y, c):
        return [(src[a], land[a], (x, y, 1 - c)) for a in range(count)]

    return plan


def _scatter_plan(count):
    def plan(src, land, x, y, c):
        out = []
        for a in range(count):
            for k, (px, py) in enumerate(_chips_of(x, y)):
                out.append((src[a].at[2 * px + py], land[a].at[k], (px, py, c)))
        return out

    return plan


def _adamw_math(w, g, m, v):
    m2 = ADAM_B1 * m + (1.0 - ADAM_B1) * g
    v2 = ADAM_B2 * v + (1.0 - ADAM_B2) * (g * g)
    m_hat = m2 / (1.0 - ADAM_B1 ** ADAM_STEP)
    v_hat = v2 / (1.0 - ADAM_B2 ** ADAM_STEP)
    delta = -ADAM_LR * (m_hat / (jnp.sqrt(v_hat) + ADAM_EPS) + ADAM_WD * w)
    return delta, m2, v2


def _adamw(w, m, v, grads, name):
    rows, cols = w.shape
    tr = _row_tile(rows, cols)
    ng = len(grads)

    def body(*refs):
        w_ref, m_ref, v_ref = refs[:3]
        g = refs[3][...]
        for k in range(1, ng):
            g = g + refs[3 + k][...]
        g_ref, d_ref, m2_ref, v2_ref = refs[3 + ng:]
        delta, m2, v2 = _adamw_math(w_ref[...], g, m_ref[...], v_ref[...])
        g_ref[...] = g
        d_ref[...] = delta
        m2_ref[...] = m2
        v2_ref[...] = v2

    spec = pl.BlockSpec((tr, cols), lambda i: (i, 0))
    return _pcall(body, name=name, grid=(rows // tr,), in_specs=[spec] * (3 + ng), out_specs=[spec] * 4,
                  out_shape=[jax.ShapeDtypeStruct((rows, cols), F32)] * 4, vmem_mb=48)(w, m, v, *grads)


def _ada_adamw(ct, dmod, w, m, v):
    rows, cols = w.shape
    tr = _row_tile(rows, cols)

    def body(ct_ref, dm_ref, w_ref, m_ref, v_ref, g_ref, d_ref, m2_ref, v2_ref):
        cv = ct_ref[...]
        ca = cv * _sigmoid(cv)
        g = ca[:, 0:1] * dm_ref[0:1, :]
        for b in range(1, N_DEV):
            g = g + ca[:, b:b + 1] * dm_ref[b:b + 1, :]
        delta, m2, v2 = _adamw_math(w_ref[...], g, m_ref[...], v_ref[...])
        g_ref[...] = g
        d_ref[...] = delta
        m2_ref[...] = m2
        v2_ref[...] = v2

    spec = pl.BlockSpec((tr, cols), lambda i: (i, 0))
    return _pcall(body, name="ada_adamw", grid=(rows // tr,),
                  in_specs=[pl.BlockSpec((tr, N_DEV), lambda i: (i, 0)), pl.BlockSpec((N_DEV, cols), lambda i: (0, 0)),
                            spec, spec, spec],
                  out_specs=[spec] * 4, out_shape=[jax.ShapeDtypeStruct((rows, cols), F32)] * 4,
                  vmem_mb=48)(ct, dmod, w, m, v)


def _mod_fwd(c_all, w, b):
    cols = w.shape[1]
    tn = cols // 3

    def body(c_ref, w_ref, b_ref, o_ref):
        cv = c_ref[...]
        ca = (cv * _sigmoid(cv)).astype(BF)
        o_ref[...] = _dot(ca, w_ref[...].astype(BF)) + b_ref[...]

    return _pcall(body, name="mod_fwd", grid=(3,),
                  in_specs=[pl.BlockSpec((N_DEV, D), lambda j: (0, 0)), pl.BlockSpec((D, tn), lambda j: (0, j)),
                            pl.BlockSpec((1, tn), lambda j: (0, j))],
                  out_specs=pl.BlockSpec((N_DEV, tn), lambda j: (0, j)),
                  out_shape=jax.ShapeDtypeStruct((N_DEV, cols), F32))(c_all, w, b)


def _resident(shape):
    zeros = (0,) * len(shape)
    return pl.BlockSpec(shape, lambda *_: zeros, pipeline_mode=pl.Buffered(1))


def _mixer_fwd(x, g, scale, shift, gate1, w_in4, cw, cb, wa, ba, wx, bx, lam, lg, lb, ws, bst, wba, wbb, wo,
               tm=256, chunk=256, piece=512):
    T = x.shape[0]
    tm = min(tm, T)
    ns = w_in4.shape[2]
    per = ns // piece

    def body(x_ref, g_ref, sc_ref, sh_ref, g1_ref, w_ref, cw_ref, cb_ref, wa_ref, ba_ref, wx_ref, bx_ref, lam_ref,
             lg_ref, lb_ref, ws_ref, bst_ref, wba_ref, wbb_ref, wo_ref,
             h1_ref, z_ref, hl_ref, yap_ref, ybp_ref, mg_ref, ya_ref, yb_ref, o_ref, x2_ref,
             xc_ref, r_ref, ig_ref, mu_ref, a_ref, prev, hc):
        i = pl.program_id(0)

        @pl.when(i == 0)
        def _():
            prev[...] = jnp.zeros_like(prev)
            hc[...] = jnp.zeros_like(hc)

        xv = x_ref[...]
        _, xh = _rms_stats(xv)
        h = ((xh * g_ref[...]) * (1.0 + sc_ref[...]) + sh_ref[...]).astype(BF)
        h1_ref[...] = h

        def proj(col, width):
            for c0 in range(col, col + width, piece):
                w = min(piece, col + width - c0)
                j, off = c0 // ns, c0 % ns
                z_ref[:, c0:c0 + w] = _dot(h, w_ref[j, :, off:off + w])

        def lru_chunk(c0):
            cs = slice(c0, c0 + chunk)
            xr = z_ref[:, cs]
            pv = prev[:, cs]
            xc = (cb_ref[:, cs] + cw_ref[3:4, cs] * xr + cw_ref[2:3, cs] * _shift_down(xr, pv, 1)
                  + cw_ref[1:2, cs] * _shift_down(xr, pv, 2) + cw_ref[0:1, cs] * _shift_down(xr, pv, 3))
            prev[:, cs] = xr[tm - SUBLANES:tm]
            r, ig, _, a, mult = _lru_gates(xc, wa_ref, ba_ref[:, cs], wx_ref, bx_ref[:, cs], lam_ref[:, cs],
                                           head0=c0 // HD)
            xc_ref[:, cs] = xc.astype(BF)
            r_ref[:, cs] = r.astype(BF)
            ig_ref[:, cs] = ig.astype(BF)
            a_ref[:, cs] = a
            mu_ref[:, cs] = mult.astype(BF)
            a, u = _scan_rows(a, mult * (ig * xc), reverse=False)
            hv = u + a * hc[SUBLANES - 1:SUBLANES, cs]
            hc[:, cs] = hv[tm - SUBLANES:tm]
            hl_ref[:, cs] = hv
            yap_ref[:, cs] = (hv * _gelu(z_ref[:, D + c0:D + c0 + chunk])).astype(BF)

        proj(0, chunk)
        proj(D, chunk)
        for c0 in range(0, D, chunk):
            if c0 + chunk < D:
                proj(c0 + chunk, chunk)
                proj(D + c0 + chunk, chunk)
            else:
                proj(2 * D, 2 * D)
            lru_chunk(c0)
        proj(4 * D, 2 * D)
        _, xhn = _layernorm_stats(_gelu(z_ref[:, 3 * D:4 * D]))
        vln = xhn * lg_ref[...] + lb_ref[...]
        _, mixed = _sgu_mix(vln, ws_ref, bst_ref, tm)
        ybp = (_gelu(z_ref[:, 2 * D:3 * D]) * mixed).astype(BF)
        ybp_ref[...] = ybp
        ya = _dot(yap_ref[...], wba_ref[...])
        yb = _dot(ybp, wbb_ref[...])
        merged = (_sigmoid_t(z_ref[:, 4 * D:5 * D]) * ya + _sigmoid_t(z_ref[:, 5 * D:6 * D]) * yb).astype(BF)
        o = _dot(merged, wo_ref[...])
        x2_ref[...] = xv + g1_ref[...] * o
        mg_ref[...] = merged
        ya_ref[...] = ya.astype(BF)
        yb_ref[...] = yb.astype(BF)
        o_ref[...] = o.astype(BF)

    row = pl.BlockSpec((tm, D), lambda i: (i, 0))
    vec = pl.BlockSpec((1, D), lambda i: (0, 0))
    bf_row = jax.ShapeDtypeStruct((T, D), BF)
    f32_row = jax.ShapeDtypeStruct((T, D), F32)
    return _pcall(body, name="mixer_fwd", grid=(T // tm,),
                  in_specs=[row, vec, vec, vec, vec, _resident(w_in4.shape), _resident(cw.shape), vec,
                            _resident(wa.shape), vec, _resident(wx.shape), vec, vec, vec, vec,
                            _resident(ws.shape), _resident(bst.shape),
                            _resident(wba.shape), _resident(wbb.shape), _resident(wo.shape)],
                  out_specs=[row, pl.BlockSpec((tm, 6 * D), lambda i: (i, 0))] + [row] * 13,
                  out_shape=[bf_row, jax.ShapeDtypeStruct((T, 6 * D), F32), f32_row, bf_row, bf_row, bf_row, bf_row,
                             bf_row, bf_row, f32_row, bf_row, bf_row, bf_row, bf_row, f32_row],
                  scratch=[pltpu.VMEM((SUBLANES, D), F32), pltpu.VMEM((SUBLANES, D), F32)], vmem_mb=60)(
        x, g, scale, shift, gate1, w_in4, cw, cb, wa, ba, wx, bx, lam, lg, lb, ws, bst, wba, wbb, wo)


def _ffn_fwd(x2, g, scale, shift, gate2, gf, w_up4, wd, cw, cb, target, tm=256, chunk=768):
    T = x2.shape[0]
    tm = min(tm, T)
    ns = w_up4.shape[2]
    dff = wd.shape[0]
    nchunk = dff // chunk
    per = ns // chunk

    def body(x2_ref, g_ref, sc_ref, sh_ref, g2_ref, gf_ref, wu_ref, wd_ref, cw_ref, cb_ref, t_ref,
             h2_ref, up_ref, f_ref, ga_ref, vd_ref, loss_ref, dx3_ref, dfo_ref, dgf_ref, dg2_ref, prev):
        i = pl.program_id(0)

        @pl.when(i == 0)
        def _():
            prev[...] = jnp.zeros_like(prev)
            loss_ref[...] = jnp.zeros_like(loss_ref)
            dgf_ref[...] = jnp.zeros_like(dgf_ref)
            dg2_ref[...] = jnp.zeros_like(dg2_ref)

        x2v = x2_ref[...]
        _, xh2 = _rms_stats(x2v)
        h2 = ((xh2 * g_ref[...]) * (1.0 + sc_ref[...]) + sh_ref[...]).astype(BF)
        h2_ref[...] = h2

        def conv(u, col):
            cs = slice(col, col + chunk)
            p = prev[:, cs]
            hid = (cb_ref[:, cs] + cw_ref[2:3, cs] * u + cw_ref[1:2, cs] * _shift_down(u, p, 1)
                   + cw_ref[0:1, cs] * _shift_down(u, p, 2))
            prev[:, cs] = u[tm - SUBLANES:tm]
            up_ref[:, cs] = u.astype(BF)
            return hid

        def up_proj(k):
            off = (k % per) * chunk
            return (_dot(h2, wu_ref[k // per, :, off:off + chunk]),
                    _dot(h2, wu_ref[N_CHIPS // 2 + k // per, :, off:off + chunk]))

        fo = None
        nxt = up_proj(0)
        for k in range(nchunk):
            col = k * chunk
            ua, uv = nxt
            if k + 1 < nchunk:
                nxt = up_proj(k + 1)
            act = conv(ua, col)
            val = conv(uv, dff + col)
            ga, dga = _gelu_and_grad(act)
            fk = (ga * val).astype(BF)
            f_ref[:, col:col + chunk] = fk
            ga_ref[:, col:col + chunk] = ga.astype(BF)
            vd_ref[:, col:col + chunk] = (val * dga).astype(BF)
            part = _dot(fk, wd_ref[col:col + chunk, :])
            fo = part if fo is None else fo + part

        x3 = x2v + g2_ref[...] * fo
        rstd, xh = _rms_stats(x3)
        err = xh * gf_ref[...] - t_ref[...]
        loss_ref[...] += 0.5 * jnp.sum(jnp.mean(err * err, axis=-1, keepdims=True), axis=0, keepdims=True)
        dy = err * (1.0 / D)
        dgf_ref[...] += _colsum(dy * xh)
        dxh = dy * gf_ref[...]
        dx3 = rstd * (dxh - xh * jnp.mean(dxh * xh, axis=-1, keepdims=True))
        dg2_ref[...] += _colsum(dx3 * fo)
        dx3_ref[...] = dx3
        dfo_ref[...] = (g2_ref[...] * dx3).astype(BF)

    row = pl.BlockSpec((tm, D), lambda i: (i, 0))
    vec = pl.BlockSpec((1, D), lambda i: (0, 0))
    wide = pl.BlockSpec((tm, 2 * dff), lambda i: (i, 0))
    half = pl.BlockSpec((tm, dff), lambda i: (i, 0))
    return _pcall(body, name="ffn_fwd", grid=(T // tm,),
                  in_specs=[row, vec, vec, vec, vec, vec, _resident(w_up4.shape), _resident(wd.shape),
                            _resident(cw.shape), _resident(cb.shape), row],
                  out_specs=[row, wide, half, half, half, pl.BlockSpec((1, LANES), lambda i: (0, 0)), row, row, vec, vec],
                  out_shape=[jax.ShapeDtypeStruct((T, D), BF), jax.ShapeDtypeStruct((T, 2 * dff), BF),
                             jax.ShapeDtypeStruct((T, dff), BF), jax.ShapeDtypeStruct((T, dff), BF),
                             jax.ShapeDtypeStruct((T, dff), BF), jax.ShapeDtypeStruct((1, LANES), F32),
                             jax.ShapeDtypeStruct((T, D), F32), jax.ShapeDtypeStruct((T, D), BF),
                             jax.ShapeDtypeStruct((1, D), F32), jax.ShapeDtypeStruct((1, D), F32)],
                  scratch=[pltpu.VMEM((SUBLANES, 2 * dff), F32)], vmem_mb=56)(
        x2, g, scale, shift, gate2, gf, w_up4, wd, cw, cb, target)


def _ffn_bwd(dfo, wd, up, ga, vd, cw, w_up4, x2, resid, g, scale, gate, o, tm=256, chunk=1536):
    T = up.shape[0]
    tm = min(tm, T)
    dff = wd.shape[0]
    ns = w_up4.shape[2]
    nchunk = dff // chunk
    per = ns // chunk
    nrow = T // tm

    def body(dfo_ref, wd_ref, up_ref, ga_ref, vd_ref, cw_ref, wu_ref, x_ref, r_ref, g_ref, sc_ref, gt_ref, o_ref,
             du_ref, dcw_ref, dcb_ref, dx_ref, dsh_ref, dsc_ref, dg_ref, do_ref, dgt_ref, nxt):
        i = pl.program_id(0)

        @pl.when(i == 0)
        def _():
            nxt[...] = jnp.zeros_like(nxt)
            for ref in (dcw_ref, dcb_ref, dsh_ref, dsc_ref, dg_ref, dgt_ref):
                ref[...] = jnp.zeros_like(ref)

        dfo_t = dfo_ref[...]

        def conv_bwd(dh, col):
            cs = slice(col, col + chunk)
            n8 = nxt[:, cs]
            dh1 = _shift_up(dh, n8, 1)
            dh2 = _shift_up(dh, n8, 2)
            nxt[:, cs] = dh[0:SUBLANES]
            du = (cw_ref[2:3, cs] * dh + cw_ref[1:2, cs] * dh1 + cw_ref[0:1, cs] * dh2).astype(BF)
            du_ref[:, cs] = du
            u = up_ref[:, cs].astype(F32)
            dcw_ref[2:3, cs] += _colsum(dh * u)
            dcw_ref[1:2, cs] += _colsum(dh1 * u)
            dcw_ref[0:1, cs] += _colsum(dh2 * u)
            dcb_ref[:, cs] += _colsum(dh)
            return du

        def down_bwd(k):
            return _dot_nt(dfo_t, wd_ref[k * chunk:(k + 1) * chunk, :])

        dh = None
        df_next = down_bwd(0)
        for k in range(nchunk):
            col = k * chunk
            off = (k % per) * chunk
            df = df_next
            if k + 1 < nchunk:
                df_next = down_bwd(k + 1)
            du_a = conv_bwd(df * vd_ref[:, col:col + chunk].astype(F32), col)
            du_v = conv_bwd(df * ga_ref[:, col:col + chunk].astype(F32), dff + col)
            part = (_dot_nt(du_a, wu_ref[k // per, :, off:off + chunk])
                    + _dot_nt(du_v, wu_ref[N_CHIPS // 2 + k // per, :, off:off + chunk]))
            dh = part if dh is None else dh + part

        rstd, xh = _rms_stats(x_ref[...])
        dsh_ref[...] += _colsum(dh)
        dsc_ref[...] += _colsum(dh * (xh * g_ref[...]))
        dn = dh * (1.0 + sc_ref[...])
        dg_ref[...] += _colsum(dn * xh)
        dxh = dn * g_ref[...]
        dx = r_ref[...] + rstd * (dxh - xh * jnp.mean(dxh * xh, axis=-1, keepdims=True))
        dx_ref[...] = dx
        do_ref[...] = (gt_ref[...] * dx).astype(BF)
        dgt_ref[...] += _colsum(dx * o_ref[...].astype(F32))

    rev = lambda i: (nrow - 1 - i, 0)
    row = pl.BlockSpec((tm, D), rev)
    vec = pl.BlockSpec((1, D), lambda i: (0, 0))
    wide = pl.BlockSpec((tm, 2 * dff), rev)
    half = pl.BlockSpec((tm, dff), rev)
    cw3 = pl.BlockSpec((3, 2 * dff), lambda i: (0, 0))
    cb1 = pl.BlockSpec((1, 2 * dff), lambda i: (0, 0))
    vshape = jax.ShapeDtypeStruct((1, D), F32)
    return _pcall(body, name="ffn_bwd", grid=(nrow,),
                  in_specs=[row, _resident(wd.shape), wide, half, half, _resident(cw.shape), _resident(w_up4.shape),
                            row, row, vec, vec, vec, row],
                  out_specs=[wide, cw3, cb1, row, vec, vec, vec, row, vec],
                  out_shape=[jax.ShapeDtypeStruct((T, 2 * dff), BF), jax.ShapeDtypeStruct((3, 2 * dff), F32),
                             jax.ShapeDtypeStruct((1, 2 * dff), F32), jax.ShapeDtypeStruct((T, D), F32),
                             vshape, vshape, vshape, jax.ShapeDtypeStruct((T, D), BF), vshape],
                  scratch=[pltpu.VMEM((SUBLANES, 2 * dff), F32)], vmem_mb=60)(
        dfo, wd, up, ga, vd, cw, w_up4, x2, resid, g, scale, gate, o)


def _mm_tn_cols(a, b, name, nshard, nb, mb=None, tm=TN_ROWS):
    T, M = a.shape
    tm = min(tm, T)
    mb = M if mb is None else mb
    ns = b.shape[1] // nshard
    per = ns // nb
    nk = T // tm
    vmem_mb = (2 * 2 * tm * (mb + nb) + 2 * (4 + 2) * mb * nb) // 2 ** 20 + 8

    def body(a_ref, b_ref, o_ref, c_ref):
        k = pl.program_id(2)

        @pl.when(k == 0)
        def _():
            o_ref[...] = jnp.zeros_like(o_ref)

        o_ref[0] += _dot_tn(a_ref[...], b_ref[...])

        @pl.when(k == nk - 1)
        def _():
            c_ref[...] = o_ref[...].astype(BF)

    out_spec = pl.BlockSpec((1, mb, nb), lambda m, t, k: (t // per, m, t % per))
    return _pcall(body, name=name, grid=(M // mb, nshard * per, nk),
                  in_specs=[pl.BlockSpec((tm, mb), lambda m, t, k: (k, m)),
                            pl.BlockSpec((tm, nb), lambda m, t, k: (k, t))],
                  out_specs=[out_spec, out_spec],
                  out_shape=[jax.ShapeDtypeStruct((nshard, M, ns), F32), jax.ShapeDtypeStruct((nshard, M, ns), BF)],
                  vmem_mb=vmem_mb)(a, b)


def _mm_nt_normbwd(dz, w4, x, resid, g, scale, name, tm=512):
    T = x.shape[0]
    tm = min(tm, T)
    ns = w4.shape[2]

    def body(dz_ref, w_ref, x_ref, r_ref, g_ref, sc_ref, dx_ref, dsh_ref, dsc_ref, dg_ref):
        i = pl.program_id(0)

        @pl.when(i == 0)
        def _():
            dsh_ref[...] = jnp.zeros_like(dsh_ref)
            dsc_ref[...] = jnp.zeros_like(dsc_ref)
            dg_ref[...] = jnp.zeros_like(dg_ref)

        dh = None
        for j in range(N_CHIPS):
            part = _dot_nt(dz_ref[:, j * ns:(j + 1) * ns], w_ref[j])
            dh = part if dh is None else dh + part
        rstd, xh = _rms_stats(x_ref[...])
        dsh_ref[...] += _colsum(dh)
        dsc_ref[...] += _colsum(dh * (xh * g_ref[...]))
        dn = dh * (1.0 + sc_ref[...])
        dg_ref[...] += _colsum(dn * xh)
        dxh = dn * g_ref[...]
        dx_ref[...] = r_ref[...] + rstd * (dxh - xh * jnp.mean(dxh * xh, axis=-1, keepdims=True))

    row = pl.BlockSpec((tm, D), lambda i: (i, 0))
    vec = pl.BlockSpec((1, D), lambda i: (0, 0))
    return _pcall(body, name=name, grid=(T // tm,),
                  in_specs=[pl.BlockSpec((tm, N_CHIPS * ns), lambda i: (i, 0)), _resident(w4.shape), row, row, vec, vec],
                  out_specs=[row, vec, vec, vec],
                  out_shape=[jax.ShapeDtypeStruct((T, D), F32)] + [jax.ShapeDtypeStruct((1, D), F32)] * 3,
                  vmem_mb=48)(dz, w4, x, resid, g, scale)


def _mix_bwd(do, ya, yb, z, wo, wba, wbb, tm=512):
    T = do.shape[0]
    tm = min(tm, T)

    def body(do_ref, ya_ref, yb_ref, ga_ref, gb_ref, wo_ref, wa_ref, wb_ref,
             dz_ref, dya_ref, dyb_ref, dyap_ref, dybp_ref):
        dm = _dot_nt(do_ref[...], wo_ref[...])
        sa = _sigmoid_t(ga_ref[...])
        sb = _sigmoid_t(gb_ref[...])
        dya = (sa * dm).astype(BF)
        dyb = (sb * dm).astype(BF)
        dz_ref[:, 0:D] = (dm * ya_ref[...].astype(F32) * sa * (1.0 - sa)).astype(BF)
        dz_ref[:, D:2 * D] = (dm * yb_ref[...].astype(F32) * sb * (1.0 - sb)).astype(BF)
        dya_ref[...] = dya
        dyb_ref[...] = dyb
        dyap_ref[...] = _dot_nt(dya, wa_ref[...]).astype(BF)
        dybp_ref[...] = _dot_nt(dyb, wb_ref[...]).astype(BF)

    row = pl.BlockSpec((tm, D), lambda i: (i, 0))
    wspec = pl.BlockSpec((D, D), lambda i: (0, 0))
    return _pcall(body, name="mix_bwd", grid=(T // tm,),
                  in_specs=[row, row, row, pl.BlockSpec((tm, D), lambda i: (i, 4)),
                            pl.BlockSpec((tm, D), lambda i: (i, 5)), wspec, wspec, wspec],
                  out_specs=[pl.BlockSpec((tm, 2 * D), lambda i: (i, 2)), row, row, row, row],
                  out_shape=[jax.ShapeDtypeStruct((T, 6 * D), BF)] + [jax.ShapeDtypeStruct((T, D), BF)] * 4,
                  vmem_mb=48)(do, ya, yb, z, z, wo, wba, wbb)


def _sgu_bwd(dz, dyb_pre, z, lg, lb, ws, bst, tb=512):
    T = z.shape[0]
    tb = min(tb, T)

    def body(dz_in, dy_ref, zu_ref, zv_ref, lg_ref, lb_ref, ws_ref, bst_ref,
             dz_ref, dws_ref, dbst_ref, dlg_ref, dlb_ref):
        del dz_in
        i = pl.program_id(0)

        @pl.when(i == 0)
        def _():
            dws_ref[...] = jnp.zeros_like(dws_ref)
            dbst_ref[...] = jnp.zeros_like(dbst_ref)
            dlg_ref[...] = jnp.zeros_like(dlg_ref)
            dlb_ref[...] = jnp.zeros_like(dlb_ref)

        gu, dgu = _gelu_and_grad(zu_ref[...])
        gv, dgv = _gelu_and_grad(zv_ref[...])
        rstd, xh = _layernorm_stats(gv)
        vln = xh * lg_ref[...] + lb_ref[...]
        wm, mixed = _sgu_mix(vln, ws_ref, bst_ref, tb)
        dy = dy_ref[...].astype(F32)
        dz_ref[:, 0:D] = (dy * mixed * dgu).astype(BF)
        dmixed = dy * gu
        ri = lax.broadcasted_iota(jnp.int32, (SGU_BLOCK, SGU_BLOCK), 0)
        ci = lax.broadcasted_iota(jnp.int32, (SGU_BLOCK, SGU_BLOCK), 1)
        blocks = []
        for blk in range(tb // SGU_BLOCK):
            rs = slice(blk * SGU_BLOCK, (blk + 1) * SGU_BLOCK)
            cols = []
            for g in range(HEADS):
                cs = slice(g * HD, (g + 1) * HD)
                dmg = dmixed[rs, cs]
                dmb = dmg.astype(BF)
                dbst_ref[:, g:g + 1] += jnp.sum(dmg, axis=1, keepdims=True)
                dws_ref[g] += jnp.where(ri >= ci, _dot_nt(dmb, vln[rs, cs].astype(BF)), 0.0)
                cols.append(_dot_tn(wm[g], dmb))
            blocks.append(jnp.concatenate(cols, axis=1))
        dvln = blocks[0] if len(blocks) == 1 else jnp.concatenate(blocks, axis=0)
        dlg_ref[...] += _colsum(dvln * xh)
        dlb_ref[...] += _colsum(dvln)
        dxh = dvln * lg_ref[...]
        dgv_in = rstd * (dxh - jnp.mean(dxh, axis=-1, keepdims=True)
                         - xh * jnp.mean(dxh * xh, axis=-1, keepdims=True))
        dz_ref[:, D:2 * D] = (dgv_in * dgv).astype(BF)

    row = pl.BlockSpec((tb, D), lambda i: (i, 0))
    vec = pl.BlockSpec((1, D), lambda i: (0, 0))
    wspec = pl.BlockSpec((HEADS, SGU_BLOCK, SGU_BLOCK), lambda i: (0, 0, 0))
    bspec = pl.BlockSpec((SGU_BLOCK, HEADS), lambda i: (0, 0))
    return _pcall(body, name="sgu_bwd", grid=(T // tb,),
                  in_specs=[HBM_SPEC, row, pl.BlockSpec((tb, D), lambda i: (i, 2)),
                            pl.BlockSpec((tb, D), lambda i: (i, 3)), vec, vec, wspec, bspec],
                  out_specs=[pl.BlockSpec((tb, 2 * D), lambda i: (i, 1)), wspec, bspec, vec, vec],
                  out_shape=[jax.ShapeDtypeStruct(dz.shape, BF),
                             jax.ShapeDtypeStruct((HEADS, SGU_BLOCK, SGU_BLOCK), F32),
                             jax.ShapeDtypeStruct((SGU_BLOCK, HEADS), F32),
                             jax.ShapeDtypeStruct((1, D), F32), jax.ShapeDtypeStruct((1, D), F32)],
                  aliases={0: 0}, vmem_mb=48)(dz, dyb_pre, z, z, lg, lb, ws, bst)


def _rglru_bwd(dz, dya_pre, z, h, xc_s, r_s, ig_s, mult_s, a_s, cw, wa, wx, lam, tb=256):
    T = z.shape[0]
    tb = min(tb, T)
    nrow = T // tb
    per = tb // SUBLANES

    def body(dz_in, dy_ref, xr_ref, gr_ref, h_ref, hh_ref, xc_ref, r_ref, ig_ref, mu_ref, a_ref, cw_ref, wa_ref,
             wx_ref, lam_ref, dz_ref, dcw_ref, dcb_ref, dwa_ref, dba_ref, dwx_ref, dbx_ref, dlam_ref, carry, nxt):
        del dz_in
        i = pl.program_id(0)
        first_block = i == nrow - 1

        @pl.when(i == 0)
        def _():
            carry[...] = jnp.zeros_like(carry)
            nxt[...] = jnp.zeros_like(nxt)
            for ref in (dcw_ref, dcb_ref, dwa_ref, dba_ref, dwx_ref, dbx_ref, dlam_ref):
                ref[...] = jnp.zeros_like(ref)

        xc = xc_ref[...].astype(F32)
        r = r_ref[...].astype(F32)
        ig = ig_ref[...].astype(F32)
        mult = mu_ref[...].astype(F32)
        a = a_ref[...]
        lam = lam_ref[...]
        ls = _log_sigmoid(lam)
        hv = h_ref[...]
        hprev = _shift_down(hv, jnp.where(first_block, 0.0, hh_ref[...]), 1)
        gg, dgg = _gelu_and_grad(gr_ref[...])
        dy = dy_ref[...].astype(F32)
        dz_ref[:, D:2 * D] = (dy * hv * dgg).astype(BF)

        rows = lax.broadcasted_iota(jnp.int32, (tb, D), 0)
        v = dy * gg + jnp.where(rows == tb - 1, carry[0:1, :], 0.0)
        q = jnp.where(rows < tb - 1, pltpu.roll(a, tb - 1, 0), 0.0)
        _, gsc = _scan_rows(q, v, reverse=True)
        carry[...] = (a * gsc)[0:SUBLANES]

        xi = ig * xc
        dmult = gsc * xi
        dxi = gsc * mult
        dig = dxi * xc
        dxc = dxi * ig
        dlog_a = gsc * hprev * a - dmult * (a * a) * pl.reciprocal(mult, approx=True)
        dlam_ref[...] += _colsum(dlog_a * r) * (LRU_C * _sigmoid(-lam))
        dpr = dlog_a * (LRU_C * ls) * r * (1.0 - r)
        dpi = dig * ig * (1.0 - ig)
        dba_ref[...] += _colsum(dpr)
        dbx_ref[...] += _colsum(dpi)
        back = []
        for hh in range(HEADS):
            cs = slice(hh * HD, (hh + 1) * HD)
            xh = xc[:, cs].astype(BF)
            dprh = dpr[:, cs].astype(BF)
            dpih = dpi[:, cs].astype(BF)
            dwa_ref[hh] += _dot_tn(xh, dprh)
            dwx_ref[hh] += _dot_tn(xh, dpih)
            back.append(_dot_nt(dprh, wa_ref[hh].astype(BF)) + _dot_nt(dpih, wx_ref[hh].astype(BF)))
        dxc = dxc + jnp.concatenate(back, axis=1)

        n8 = nxt[...]
        d1 = _shift_up(dxc, n8, 1)
        d2 = _shift_up(dxc, n8, 2)
        d3 = _shift_up(dxc, n8, 3)
        nxt[...] = dxc[0:SUBLANES]
        dz_ref[:, 0:D] = (cw_ref[3:4, :] * dxc + cw_ref[2:3, :] * d1 + cw_ref[1:2, :] * d2
                          + cw_ref[0:1, :] * d3).astype(BF)
        xr = xr_ref[...]
        dcw_ref[3:4, :] += _colsum(dxc * xr)
        dcw_ref[2:3, :] += _colsum(d1 * xr)
        dcw_ref[1:2, :] += _colsum(d2 * xr)
        dcw_ref[0:1, :] += _colsum(d3 * xr)
        dcb_ref[...] += _colsum(dxc)

    rev = lambda col: (lambda i: (nrow - 1 - i, col))
    row = pl.BlockSpec((tb, D), rev(0))
    halo = pl.BlockSpec((SUBLANES, D), lambda i: (jnp.maximum((nrow - 1 - i) * per - 1, 0), 0))
    vec = pl.BlockSpec((1, D), lambda i: (0, 0))
    wspec = pl.BlockSpec((HEADS, HD, HD), lambda i: (0, 0, 0))
    c4 = pl.BlockSpec((4, D), lambda i: (0, 0))
    wshape = jax.ShapeDtypeStruct((HEADS, HD, HD), F32)
    vshape = jax.ShapeDtypeStruct((1, D), F32)
    return _pcall(body, name="rglru_bwd", grid=(nrow,),
                  in_specs=[HBM_SPEC, row, row, pl.BlockSpec((tb, D), rev(1)), row, halo,
                            row, row, row, row, row, c4, wspec, wspec, vec],
                  out_specs=[pl.BlockSpec((tb, 2 * D), rev(0)), c4, vec, wspec, vec, wspec, vec, vec],
                  out_shape=[jax.ShapeDtypeStruct(dz.shape, BF), jax.ShapeDtypeStruct((4, D), F32), vshape,
                             wshape, vshape, wshape, vshape, vshape],
                  scratch=[pltpu.VMEM((SUBLANES, D), F32), pltpu.VMEM((SUBLANES, D), F32)],
                  aliases={0: 0}, vmem_mb=56)(dz, dya_pre, z, z, h, h, xc_s, r_s, ig_s, mult_s, a_s, cw, wa, wx, lam)


def _pack_rows(parts):
    out = []
    for p in parts:
        q = p.reshape(-1, LANES)
        pad = (-q.shape[0]) % SUBLANES
        if pad:
            q = jnp.concatenate([q, jnp.zeros((pad, LANES), q.dtype)], axis=0)
        out.append(q)
    return jnp.concatenate(out, axis=0)


def _rows_of(shape):
    n = 1
    for s in shape:
        n *= s
    rows = n // LANES
    return rows + (-rows) % SUBLANES


def kernel(x, c, w_ada, b_ada, norm_mix_g, w_in, rnn_conv_w, rnn_conv_b, lru_w_a, lru_b_a, lru_w_x, lru_b_x, lru_lambda, sgu_ln_g, sgu_ln_b, sgu_w_s, sgu_b_s, w_branch_a, w_branch_b, w_out, norm_ffn_g, w_up, ffn_conv_w, ffn_conv_b, w_down, norm_final_g, loss_target, m_w_ada, m_b_ada, m_norm_mix_g, m_w_in, m_rnn_conv_w, m_rnn_conv_b, m_lru_w_a, m_lru_b_a, m_lru_w_x, m_lru_b_x, m_lru_lambda, m_sgu_ln_g, m_sgu_ln_b, m_sgu_w_s, m_sgu_b_s, m_w_branch_a, m_w_branch_b, m_w_out, m_norm_ffn_g, m_w_up, m_ffn_conv_w, m_ffn_conv_b, m_w_down, m_norm_final_g, v_w_ada, v_b_ada, v_norm_mix_g, v_w_in, v_rnn_conv_w, v_rnn_conv_b, v_lru_w_a, v_lru_b_a, v_lru_w_x, v_lru_b_x, v_lru_lambda, v_sgu_ln_g, v_sgu_ln_b, v_sgu_w_s, v_sgu_b_s, v_w_branch_a, v_w_branch_b, v_w_out, v_norm_ffn_g, v_w_up, v_ffn_conv_w, v_ffn_conv_b, v_w_down, v_norm_final_g):
    args = dict(locals())
    T = x.shape[1]
    mx, my, mc = lax.axis_index("x"), lax.axis_index("y"), lax.axis_index("c")
    chip = 2 * mx + my
    dev = 2 * chip + mc
    vec = lambda a: a.reshape(1, -1)

    xt = x.reshape(T, D)
    tgt = loss_target.reshape(T, D)
    ns = w_in.shape[2]
    dff = w_down.shape[1] * N_CHIPS

    c_all = _gather8(c.reshape(SUBLANES, LANES), "gather_c").reshape(N_DEV, D)
    b_ada_sh = lax.dynamic_slice(b_ada, (0, chip * ns), (1, ns))
    mod_sh = _mod_fwd(c_all, w_ada[0], b_ada_sh)

    mixer_w = _cast_shards([w_in[0], w_branch_a[0], w_branch_b[0], w_out[0]], "cast_mixer_weights")
    w_in4, wba4, wbb4, wo4, rcw4, fcw4, mod4 = _gather_weights(
        list(mixer_w) + [rnn_conv_w[0], ffn_conv_w[0], mod_sh], [True] * 4 + [False] * 3)
    late = _cast_shards([w_up[0], w_down[0]], "cast_late", after=mod4)
    late_plan = _gather_plan(len(late))
    late_handle, late_token = _remote_start(
        late, [lax.empty((N_CHIPS,) + w.shape, w.dtype) for w in late], late_plan, 3 * len(late), "gather_late_start")
    rcw_full = jnp.transpose(rcw4, (1, 0, 2)).reshape(4, D)
    fcw_full = jnp.transpose(fcw4, (1, 0, 2)).reshape(3, 2 * dff)
    mod = lax.dynamic_index_in_dim(mod4, dev, axis=1, keepdims=False).reshape(1, 6 * D)
    shift1, scale1, gate1, shift2, scale2, gate2 = [mod[:, k * D:(k + 1) * D] for k in range(6)]

    bst = jnp.transpose(sgu_b_s[0])
    wba_full = wba4.reshape(D, D)
    wbb_full = wbb4.reshape(D, D)
    wo_full = wo4.reshape(D, D)
    h1, z, h_lru, ya_pre, yb_pre, merged, ya, yb, o1, x2, lru_xc, lru_r, lru_i, lru_mult, lru_a = _mixer_fwd(
        xt, norm_mix_g, scale1 + late_token[0:1, 0:1], shift1, gate1, w_in4, rcw_full, rnn_conv_b,
        lru_w_a[0], lru_b_a, lru_w_x[0], lru_b_x, lru_lambda, sgu_ln_g, sgu_ln_b, sgu_w_s[0], bst,
        wba_full, wbb_full, wo_full)
    late, late_lands = _remote_wait(late_handle, late_plan, o1, "gather_late_wait")
    w_up4, w_down4 = _place_own(late, late_lands)
    wd_full = w_down4.reshape(dff, D)
    h2, up, f, ffn_ga, ffn_vd, loss_part, dx3, dfo, dgf, dgate2 = _ffn_fwd(
        x2, norm_ffn_g, scale2, shift2, gate2, vec(norm_final_g), w_up4, wd_full, fcw_full, ffn_conv_b, tgt)

    dup, dfcw, dfcb, dx2, dshift2, dscale2, dg_ffn, do1, dgate1 = _ffn_bwd(
        dfo, wd_full, up, ffn_ga, ffn_vd, fcw_full, w_up4, x2, dx3, norm_ffn_g, scale2, gate1, o1)
    dwd = _mm_tn_cols(f, dfo, "dw_down", 1, D, mb=D, tm=TN_ROWS_SQUARE)
    dw_up4 = _mm_tn_cols(h2, dup, "dw_up", N_CHIPS, ns // 2, tm=TN_ROWS_SQUARE)
    dz, dya, dyb, dya_pre, dyb_pre = _mix_bwd(do1, ya, yb, z, wo_full, wba_full, wbb_full)
    dwo = _mm_tn_cols(merged, do1, "dw_out", 1, D)
    dwba = _mm_tn_cols(ya_pre, dya, "dw_branch_a", 1, D)
    dwbb = _mm_tn_cols(yb_pre, dyb, "dw_branch_b", 1, D)

    chip_id = chip.astype(jnp.int32).reshape(1)

    def reduce_start(group, name):
        wire = [g16.reshape(N_CHIPS, -1, g16.shape[-1]) for _, (_, g16) in group]
        lands = [lax.empty((3,) + w.shape[1:], w.dtype) for w in wire]
        return _remote_start(wire, lands, _scatter_plan(len(group)), 3 * len(group), "scatter_start_" + name)

    def reduce_finish(group, handle, after, name):
        _, landed = _remote_wait(handle, _scatter_plan(len(group)), after, "scatter_wait_" + name)
        return [_sum_own_and_landed(chip_id, g32.reshape(N_CHIPS, -1, g32.shape[-1]), l, "sum_chips_" + n)
                for (n, (g32, _)), l in zip(group, landed)]

    group1 = [("w_up", dw_up4), ("w_down", dwd), ("w_branch_a", dwba), ("w_branch_b", dwbb), ("w_out", dwo)]
    handle1, token1 = reduce_start(group1, "late")
    dz, dws, dbst, dlg, dlb = _sgu_bwd(dz, dyb_pre, z, sgu_ln_g + token1[0:1, 0:1], sgu_ln_b, sgu_w_s[0], bst)
    dz, drcw, drcb, dwa, dba, dwx, dbx, dlam = _rglru_bwd(
        dz, dya_pre, z, h_lru, lru_xc, lru_r, lru_i, lru_mult, lru_a, rcw_full, lru_w_a[0], lru_w_x[0], lru_lambda)
    early_small = [("rnn_conv_b", drcb), ("lru_w_a", dwa), ("lru_b_a", dba), ("lru_w_x", dwx), ("lru_b_x", dbx),
                   ("lru_lambda", dlam), ("sgu_ln_g", dlg), ("sgu_ln_b", dlb), ("sgu_w_s", dws),
                   ("sgu_b_s", jnp.transpose(dbst)), ("norm_ffn_g", dg_ffn),
                   ("ffn_conv_b", dfcb), ("norm_final_g", dgf)]
    r_early = sum(_rows_of(args[n].shape) for n, _ in early_small)
    early_pack = _pack_rows([g for _, g in early_small] + [drcw, dfcw])
    early_pack = jnp.concatenate(
        [early_pack, jnp.zeros(((-early_pack.shape[0]) % 256, LANES), F32)], axis=0)
    early_chip = _add_pair(early_pack, _swap_cores([early_pack], "swap_small_grads")[0], "sum_cores_small_grads")
    early_handle, token3 = _remote_start([early_chip], [lax.empty((3,) + early_chip.shape, F32)], _bcast_plan, 3,
                                         "small_grads_start")
    def swap_start(totals, name):
        lands = [lax.empty(t.shape, t.dtype) for t in totals]
        return _remote_start(totals, lands, _sibling_plan(len(totals)), len(totals), "swap_sums_start_" + name)

    out = {}

    def swap_finish(group, handle, after, name):
        mine, theirs = _remote_wait(handle, _sibling_plan(len(group)), after, "swap_sums_wait_" + name)
        for (n, _), a, b in zip(group, mine, theirs):
            shape = args[n].shape
            res = _adamw(args[n][0], args["m_" + n][0], args["v_" + n][0], [a, b], "adamw_" + n)
            for kind, r in zip(("grad_", "delta_", "new_m_", "new_v_"), res):
                out[kind + n] = r.reshape(shape)
        return res[3]

    swap1, token4 = swap_start(reduce_finish(group1, handle1, drcb, "late"), "late")
    group2 = [("w_in", _mm_tn_cols(h1, dz, "dw_in", N_CHIPS, ns // 2, tm=TN_ROWS_SQUARE))]
    handle2, token2 = reduce_start(group2, "in")
    tokens = token2[0:1, 0:1] + token3[0:1, 0:1] + token4[0:1, 0:1]
    grad_x, dshift1, dscale1, dg_mix = _mm_nt_normbwd(
        dz, w_in4, xt, dx2, norm_mix_g + tokens, scale1, "dh1_norm_bwd")
    swap2, token5 = swap_start(reduce_finish(group2, handle2, dg_mix, "in"), "in")
    dmod = jnp.concatenate([dshift1, dscale1, dgate1, dshift2, dscale2, dgate2], axis=1)
    last = swap_finish(group1, swap1, token5, "late")
    swap_finish(group2, swap2, last, "in")

    late_small = [("b_ada", dmod), ("norm_mix_g", dg_mix)]
    small = late_small + early_small
    late_all = _gather8(_pack_rows([g for _, g in late_small] + [loss_part]), "gather_late_small_grads")
    late_sum = _sum_parts(late_all, "sum_late_small_grads")
    r_late = sum(_rows_of(args[n].shape) for n, _ in late_small)
    loss = late_sum[r_late, 0]
    late_sum = late_sum[:r_late]
    _, (early_landed,) = _remote_wait(early_handle, _bcast_plan, dg_mix, "small_grads_wait")
    early_sum = _sum_chips_in_order(chip_id, early_chip, early_landed, "sum_early_small_grads")
    r_small = sum(_rows_of(args[n].shape) for n, _ in small)
    r_pad = r_small + (-r_small) % 256
    fill = jnp.zeros((r_pad - r_small, LANES), F32)
    g_small = jnp.concatenate([late_sum, early_sum[:r_early], fill], axis=0)

    def pack_small(prefix):
        return jnp.concatenate([_pack_rows([args[prefix + n] for n, _ in small]), fill], axis=0)

    res = _adamw(pack_small(""), pack_small("m_"), pack_small("v_"), [g_small], "adamw_small")
    off = 0
    for n, _ in small:
        shape = args[n].shape
        rows = _rows_of(shape)
        for kind, r in zip(("grad_", "delta_", "new_m_", "new_v_"), res):
            out[kind + n] = r[off:off + rows].reshape(shape)
        off += rows

    rcw_cols = rnn_conv_w.shape[2]
    g_rcw = lax.dynamic_slice(early_sum[r_early:r_early + 32].reshape(4, D), (0, chip * rcw_cols), (4, rcw_cols))
    g_fcw = lax.dynamic_slice(early_sum[r_early + 32:r_early + 32 + 144].reshape(3, 2 * dff), (0, chip * ns), (3, ns))
    conv = [("rnn_conv_w", g_rcw), ("ffn_conv_w", g_fcw)]
    res = _adamw(_pack_rows([args[n] for n, _ in conv]), _pack_rows([args["m_" + n] for n, _ in conv]),
                 _pack_rows([args["v_" + n] for n, _ in conv]), [_pack_rows([g for _, g in conv])], "adamw_conv")
    off = 0
    for n, _ in conv:
        shape = args[n].shape
        cnt = shape[1] * shape[2] // LANES
        for kind, r in zip(("grad_", "delta_", "new_m_", "new_v_"), res):
            out[kind + n] = r[off:off + cnt].reshape(shape)
        off += _rows_of(shape)

    dmod_all = late_all[:, 0:6 * D // LANES, :].reshape(N_DEV, 6 * D)
    dmod_sh = lax.dynamic_slice(dmod_all, (0, chip * ns), (N_DEV, ns))
    res = _ada_adamw(jnp.transpose(c_all), dmod_sh, w_ada[0], m_w_ada[0], v_w_ada[0])
    for kind, r in zip(("grad_", "delta_", "new_m_", "new_v_"), res):
        out[kind + "w_ada"] = r.reshape(w_ada.shape)

    names = ["w_ada", "b_ada", "norm_mix_g", "w_in", "rnn_conv_w", "rnn_conv_b", "lru_w_a", "lru_b_a", "lru_w_x",
             "lru_b_x", "lru_lambda", "sgu_ln_g", "sgu_ln_b", "sgu_w_s", "sgu_b_s", "w_branch_a", "w_branch_b",
             "w_out", "norm_ffn_g", "w_up", "ffn_conv_w", "ffn_conv_b", "w_down", "norm_final_g"]
    result = [loss, grad_x.reshape(x.shape)]
    for kind in ("grad_", "delta_", "new_m_", "new_v_"):
        result += [out[kind + n] for n in names]
    return tuple(result)
```

```python
import jax
import jax.numpy as jnp
from jax import lax
from jax.experimental import pallas as pl
from jax.experimental.pallas import tpu as pltpu

F32 = jnp.float32
BF = jnp.bfloat16

D = 1024
HEADS = 8
HD = D // HEADS
SGU_BLOCK = 128
N_CHIPS = 4
N_DEV = 8
EPS = 1e-6
LRU_C = 8.0
LANES = 128
SUBLANES = 8
ELEMENTWISE_BLOCK_BYTES = 3 << 19
TN_ROWS = 2048
TN_ROWS_SQUARE = 4096

ADAM_LR = 0.001
ADAM_B1 = 0.9
ADAM_B2 = 0.999
ADAM_EPS = 1e-08
ADAM_WD = 0.01
ADAM_STEP = 10

GELU_K0 = 0.7978845608028654
GELU_K1 = 0.044715

HBM_SPEC = pl.BlockSpec(memory_space=pltpu.HBM)
MESH_ID = pl.DeviceIdType.MESH


def _pcall(body, *, name, out_shape, grid=(), in_specs=None, out_specs=None, scratch=(), vmem_mb=32, aliases=None,
           grid_spec=None):
    kw = {}
    if aliases:
        kw["input_output_aliases"] = aliases
    if grid_spec is not None:
        kw["grid_spec"] = grid_spec
        ndim = len(grid_spec.grid)
    else:
        kw.update(grid=grid, in_specs=in_specs, out_specs=out_specs, scratch_shapes=list(scratch))
        ndim = len(grid)
    if ndim:
        params = pltpu.CompilerParams(dimension_semantics=("arbitrary",) * ndim, vmem_limit_bytes=vmem_mb * 2 ** 20)
    else:
        params = pltpu.CompilerParams(vmem_limit_bytes=vmem_mb * 2 ** 20)
    return pl.pallas_call(body, name=name, out_shape=out_shape, compiler_params=params, **kw)


def _gelu_cdf(x, x2):
    return 0.5 * jnp.tanh(x * (GELU_K0 + (GELU_K0 * GELU_K1) * x2)) + 0.5


def _gelu(x):
    return x * _gelu_cdf(x, x * x)


def _gelu_and_grad(x):
    x2 = x * x
    s = _gelu_cdf(x, x2)
    g = x * s
    dg = s * (1.0 + (x - g) * ((2.0 * GELU_K0) + (6.0 * GELU_K0 * GELU_K1) * x2))
    return g, dg


def _sigmoid(x):
    return 1.0 / (1.0 + jnp.exp(-x))


def _sigmoid_t(x):
    return 0.5 * jnp.tanh(0.5 * x) + 0.5


def _log_sigmoid(x):
    e = jnp.exp(-jnp.abs(x))
    u = 1.0 + e
    d = u - 1.0
    l1p = jnp.where(d == 0.0, e, jnp.log(u) * (e / jnp.where(d == 0.0, 1.0, d)))
    return jnp.minimum(x, 0.0) - l1p


def _dot(a, b):
    return jnp.dot(a, b, preferred_element_type=F32)


def _dot_nt(a, b):
    return lax.dot_general(a, b, (((1,), (1,)), ((), ())), preferred_element_type=F32)


def _dot_tn(a, b):
    return lax.dot_general(a, b, (((0,), (0,)), ((), ())), preferred_element_type=F32)


def _shift_down(x, halo, s):
    r = pltpu.roll(x, s, 0)
    rows = lax.broadcasted_iota(jnp.int32, (SUBLANES, x.shape[1]), 0)
    head = jnp.where(rows < s, pltpu.roll(halo, s, 0), r[0:SUBLANES])
    return jnp.concatenate([head, r[SUBLANES:]], axis=0)


def _shift_up(x, halo, s):
    n = x.shape[0]
    r = pltpu.roll(x, n - s, 0)
    rows = lax.broadcasted_iota(jnp.int32, (SUBLANES, x.shape[1]), 0)
    tail = jnp.where(rows >= SUBLANES - s, pltpu.roll(halo, SUBLANES - s, 0), r[n - SUBLANES:n])
    return jnp.concatenate([r[:n - SUBLANES], tail], axis=0)


def _scan_rows(a, u, reverse):
    n, width = a.shape
    rows = lax.broadcasted_iota(jnp.int32, (n, width), 0)
    d = 1
    while d < n:
        if d < SUBLANES:
            keep = rows < n - d if reverse else rows >= d
            shift = n - d if reverse else d
            a_s = jnp.where(keep, pltpu.roll(a, shift, 0), 1.0)
            u_s = jnp.where(keep, pltpu.roll(u, shift, 0), 0.0)
        elif reverse:
            a_s = jnp.concatenate([a[d:], jnp.ones((d, width), a.dtype)], axis=0)
            u_s = jnp.concatenate([u[d:], jnp.zeros((d, width), u.dtype)], axis=0)
        else:
            a_s = jnp.concatenate([jnp.ones((d, width), a.dtype), a[:n - d]], axis=0)
            u_s = jnp.concatenate([jnp.zeros((d, width), u.dtype), u[:n - d]], axis=0)
        u = a * u_s + u
        a = a * a_s
        d *= 2
    return a, u


def _colsum(x):
    return jnp.sum(x, axis=0, keepdims=True)


def _rms_stats(x):
    r = lax.rsqrt(jnp.mean(x * x, axis=-1, keepdims=True) + EPS)
    return r, x * r


def _lru_gates(xc, wa_ref, ba, wx_ref, bx, lam, head0=0):
    pr, pi = [], []
    for hh in range(xc.shape[1] // HD):
        xh = xc[:, hh * HD:(hh + 1) * HD].astype(BF)
        pr.append(_dot(xh, wa_ref[head0 + hh].astype(BF)))
        pi.append(_dot(xh, wx_ref[head0 + hh].astype(BF)))
    r = _sigmoid_t((pr[0] if len(pr) == 1 else jnp.concatenate(pr, axis=1)) + ba)
    ig = _sigmoid_t((pi[0] if len(pi) == 1 else jnp.concatenate(pi, axis=1)) + bx)
    ls = _log_sigmoid(lam)
    log_a = LRU_C * r * ls
    a = jnp.exp(log_a)
    x2 = 2.0 * log_a
    u = a * a
    lu = jnp.log(jnp.maximum(u, 1e-37))
    ratio = x2 * pl.reciprocal(jnp.where(lu == 0.0, 1.0, lu), approx=True)
    em1 = jnp.where(lu == 0.0, x2, jnp.where(u < 1e-30, -1.0, (u - 1.0) * ratio))
    mult = jnp.sqrt(-em1)
    return r, ig, ls, a, mult


def _sgu_mix(vln, ws_ref, bst_ref, tb):
    ri = lax.broadcasted_iota(jnp.int32, (SGU_BLOCK, SGU_BLOCK), 0)
    ci = lax.broadcasted_iota(jnp.int32, (SGU_BLOCK, SGU_BLOCK), 1)
    wm = [jnp.where(ri >= ci, ws_ref[g], 0.0).astype(BF) for g in range(HEADS)]
    blocks = []
    for blk in range(tb // SGU_BLOCK):
        cols = []
        for g in range(HEADS):
            vb = vln[blk * SGU_BLOCK:(blk + 1) * SGU_BLOCK, g * HD:(g + 1) * HD].astype(BF)
            cols.append(_dot(wm[g], vb) + bst_ref[:, g:g + 1])
        blocks.append(jnp.concatenate(cols, axis=1))
    mixed = blocks[0] if len(blocks) == 1 else jnp.concatenate(blocks, axis=0)
    return wm, mixed


def _layernorm_stats(v):
    mu = jnp.mean(v, axis=-1, keepdims=True)
    vc = v - mu
    rstd = lax.rsqrt(jnp.mean(vc * vc, axis=-1, keepdims=True) + EPS)
    return rstd, vc * rstd


def _my_xyc():
    return lax.axis_index("x"), lax.axis_index("y"), lax.axis_index("c")


def _gather_weights(srcs, halve):
    n = len(srcs)
    out_shape = [jax.ShapeDtypeStruct((N_CHIPS,) + s.shape, s.dtype) for s in srcs]

    def body(*refs):
        src, out = refs[:n], refs[n:2 * n]
        send_sems, recv_sems, fwd_send, fwd_recv, loc_sems = refs[2 * n:]
        x, y, c = _my_xyc()
        me = 2 * x + y
        chips = [(1 - x, y), (x, 1 - y), (1 - x, 1 - y)]

        def half(ref, a, which):
            if not halve[a]:
                return ref
            h = srcs[a].shape[0] // 2
            return ref.at[pl.ds(which * h, h)]

        def ici(a, k, frm):
            px, py = chips[k]
            return pltpu.make_async_remote_copy(
                src_ref=half(src[a], a, c), dst_ref=half(out[a].at[frm], a, c),
                send_sem=send_sems.at[a, k], recv_sem=recv_sems.at[a, k],
                device_id=(px, py, c), device_id_type=MESH_ID)

        def d2d(a, k, which):
            px, py = chips[k]
            rows = half(out[a].at[2 * px + py], a, which)
            return pltpu.make_async_remote_copy(
                src_ref=rows, dst_ref=rows, send_sem=fwd_send.at[a, k], recv_sem=fwd_recv.at[a, k],
                device_id=(x, y, 1 - c), device_id_type=MESH_ID)

        local, sends = [], []
        for a in range(n):
            lc = pltpu.make_async_copy(src[a], out[a].at[me], loc_sems.at[a])
            lc.start()
            local.append(lc)
            for k in range(3):
                cp = ici(a, k, me)
                cp.start()
                sends.append(cp)
        for a in range(n):
            for k in range(3):
                px, py = chips[k]
                ici(a, k, 2 * px + py).wait_recv()
                if halve[a]:
                    fw = d2d(a, k, c)
                    fw.start()
                    sends.append(fw)
        for a in range(n):
            if halve[a]:
                for k in range(3):
                    d2d(a, k, 1 - c).wait_recv()
        for cp in sends:
            cp.wait_send()
        for lc in local:
            lc.wait()

    sem = pltpu.SemaphoreType.DMA((n, 3))
    return _pcall(body, name="gather_weights", out_shape=out_shape, in_specs=[HBM_SPEC] * n,
                  out_specs=[HBM_SPEC] * n, scratch=[sem, sem, sem, sem, pltpu.SemaphoreType.DMA((n,))])(*srcs)


SEM_SPEC = pl.BlockSpec(memory_space=pltpu.SEMAPHORE)


def _remote_start(srcs, lands, plan, ncopies, name):
    n, m = len(srcs), len(lands)

    def body(*refs):
        src, land = refs[:n], refs[n:n + m]
        send_sems, recv_sems = refs[n + m], refs[n + m + 1]
        token = refs[-1]
        x, y, c = _my_xyc()
        for i, (s, d, dev) in enumerate(plan(src, land, x, y, c)):
            pltpu.make_async_remote_copy(src_ref=s, dst_ref=d, send_sem=send_sems.at[i], recv_sem=recv_sems.at[i],
                                         device_id=dev, device_id_type=MESH_ID).start()
        token[...] = jnp.zeros_like(token)

    bufs = list(srcs) + list(lands)
    out = pl.pallas_call(
        body, name=name,
        out_shape=(pltpu.SemaphoreType.DMA((ncopies,)), pltpu.SemaphoreType.DMA((ncopies,)),
                   *[pltpu.HBM(b.shape, b.dtype) for b in bufs], jax.ShapeDtypeStruct((SUBLANES, LANES), F32)),
        in_specs=[HBM_SPEC] * (n + m),
        out_specs=(SEM_SPEC, SEM_SPEC, *[HBM_SPEC] * (n + m), pl.BlockSpec(memory_space=pltpu.VMEM)),
        input_output_aliases={i: 2 + i for i in range(n + m)},
        compiler_params=pltpu.CompilerParams(has_side_effects=pltpu.SideEffectType.DATAFLOW_SIDE_EFFECTING),
    )(*[pltpu.with_memory_space_constraint(b, pltpu.HBM) for b in bufs])
    return (out[0], out[1], out[2:2 + n], out[2 + n:2 + n + m]), out[-1]


def _remote_wait(handle, plan, after, name):
    send_sems, recv_sems, srcs, lands = handle
    n, m = len(srcs), len(lands)

    def body(*refs):
        src, land = refs[:n], refs[n:n + m]
        ssem, rsem = refs[n + m], refs[n + m + 1]
        x, y, c = _my_xyc()
        for i, (s, d, dev) in enumerate(plan(src, land, x, y, c)):
            cp = pltpu.make_async_remote_copy(src_ref=s, dst_ref=d, send_sem=ssem.at[i], recv_sem=rsem.at[i],
                                              device_id=dev, device_id_type=MESH_ID)
            cp.wait_send()
            cp.wait_recv()

    bufs = list(srcs) + list(lands)
    out = pl.pallas_call(
        body, name=name, out_shape=tuple(pltpu.HBM(b.shape, b.dtype) for b in bufs),
        in_specs=[HBM_SPEC] * (n + m) + [SEM_SPEC, SEM_SPEC, pl.BlockSpec(memory_space=pl.ANY)],
        out_specs=tuple([HBM_SPEC] * (n + m)), input_output_aliases={i: i for i in range(n + m)},
        compiler_params=pltpu.CompilerParams(has_side_effects=pltpu.SideEffectType.DATAFLOW_SIDE_EFFECTING),
    )(*bufs, send_sems, recv_sems, after)
    return out[:n], out[n:]


def _chips_of(x, y):
    return [(1 - x, y), (x, 1 - y), (1 - x, 1 - y)]


def _gather_plan(count):
    def plan(src, land, x, y, c):
        me = 2 * x + y
        return [(src[a], land[a].at[me], (px, py, c)) for a in range(count) for px, py in _chips_of(x, y)]

    return plan


def _place_own(srcs, lands):
    n = len(srcs)

    def body(*refs):
        src, land, sems = refs[:n], refs[2 * n:3 * n], refs[3 * n]
        me = 2 * lax.axis_index("x") + lax.axis_index("y")
        copies = [pltpu.make_async_copy(src[a], land[a].at[me], sems.at[a]) for a in range(n)]
        for cp in copies:
            cp.start()
        for cp in copies:
            cp.wait()

    return _pcall(body, name="place_own_shards", out_shape=[jax.ShapeDtypeStruct(l.shape, l.dtype) for l in lands],
                  in_specs=[HBM_SPEC] * (2 * n), out_specs=[HBM_SPEC] * n, aliases={n + a: a for a in range(n)},
                  scratch=[pltpu.SemaphoreType.DMA((n,))])(*srcs, *lands)


def _gather8(src, name):
    def body(src_ref, out_ref, send_sems, recv_sems, loc_sem):
        x, y, c = _my_xyc()
        me = 4 * x + 2 * y + c
        lc = pltpu.make_async_copy(src_ref, out_ref.at[me], loc_sem)
        lc.start()
        cps = []
        for k in range(1, N_DEV):
            px = 1 - x if (k >> 2) & 1 else x
            py = 1 - y if (k >> 1) & 1 else y
            pc = 1 - c if k & 1 else c
            cp = pltpu.make_async_remote_copy(
                src_ref=src_ref, dst_ref=out_ref.at[me], send_sem=send_sems.at[k - 1], recv_sem=recv_sems.at[k - 1],
                device_id=(px, py, pc), device_id_type=MESH_ID)
            cp.start()
            cps.append(cp)
        for cp in cps:
            cp.wait()
        lc.wait()

    return _pcall(body, name=name, out_shape=jax.ShapeDtypeStruct((N_DEV,) + src.shape, src.dtype),
                  in_specs=[HBM_SPEC], out_specs=HBM_SPEC,
                  scratch=[pltpu.SemaphoreType.DMA((N_DEV - 1,)), pltpu.SemaphoreType.DMA((N_DEV - 1,)),
                           pltpu.SemaphoreType.DMA])(src)


def _cast_shards(arrs, name, after=None):
    n = len(arrs)
    extra = [] if after is None else [after]

    def body(*refs):
        ins, outs = refs[:n], refs[n + len(extra):]
        for a in range(n):
            outs[a][...] = ins[a][...].astype(BF)

    specs = [pl.BlockSpec((s.shape[0] // 4, s.shape[1]), lambda i: (i, 0)) for s in arrs]
    return _pcall(body, name=name, grid=(4,), in_specs=specs + [pl.BlockSpec(memory_space=pl.ANY)] * len(extra),
                  out_specs=specs, out_shape=[jax.ShapeDtypeStruct(s.shape, BF) for s in arrs])(*arrs, *extra)


def _row_tile(rows, cols):
    t = rows
    while t * cols * 4 > ELEMENTWISE_BLOCK_BYTES and t % (2 * SUBLANES) == 0:
        t //= 2
    return t


def _sum_parts(parts, name):
    p, rows, cols = parts.shape
    tr = _row_tile(rows, cols * p // 2)

    def body(p_ref, o_ref):
        acc = p_ref[0].astype(F32)
        for k in range(1, p):
            acc = acc + p_ref[k].astype(F32)
        o_ref[...] = acc

    return _pcall(body, name=name, grid=(rows // tr,),
                  in_specs=[pl.BlockSpec((p, tr, cols), lambda i: (0, i, 0))],
                  out_specs=pl.BlockSpec((tr, cols), lambda i: (i, 0)),
                  out_shape=jax.ShapeDtypeStruct((rows, cols), F32), vmem_mb=48)(parts)


def _sum_own_and_landed(chip, sums, landed, name):
    _, rows, cols = sums.shape
    tr = _row_tile(rows, 2 * cols)

    def body(chip_ref, own_ref, land_ref, o_ref):
        del chip_ref
        acc = own_ref[0].astype(F32)
        for k in range(3):
            acc = acc + land_ref[k].astype(F32)
        o_ref[...] = acc

    grid_spec = pltpu.PrefetchScalarGridSpec(
        num_scalar_prefetch=1, grid=(rows // tr,),
        in_specs=[pl.BlockSpec((1, tr, cols), lambda i, chip_ref: (chip_ref[0], i, 0)),
                  pl.BlockSpec((3, tr, cols), lambda i, chip_ref: (0, i, 0))],
        out_specs=pl.BlockSpec((tr, cols), lambda i, chip_ref: (i, 0)))
    return _pcall(body, name=name, grid_spec=grid_spec, out_shape=jax.ShapeDtypeStruct((rows, cols), F32),
                  vmem_mb=48)(chip, sums, landed)


def _swap_cores(arrs, name):
    n = len(arrs)

    def body(*refs):
        src, out = refs[:n], refs[n:2 * n]
        send_sems, recv_sems = refs[2 * n:]
        x, y, c = _my_xyc()
        cps = []
        for a in range(n):
            cp = pltpu.make_async_remote_copy(
                src_ref=src[a], dst_ref=out[a], send_sem=send_sems.at[a], recv_sem=recv_sems.at[a],
                device_id=(x, y, 1 - c), device_id_type=MESH_ID)
            cp.start()
            cps.append(cp)
        for cp in cps:
            cp.wait()

    sem = pltpu.SemaphoreType.DMA((n,))
    return _pcall(body, name=name, out_shape=[jax.ShapeDtypeStruct(a.shape, a.dtype) for a in arrs],
                  in_specs=[HBM_SPEC] * n, out_specs=[HBM_SPEC] * n, scratch=[sem, sem])(*arrs)


def _add_pair(a, b, name):
    rows, cols = a.shape
    tr = _row_tile(rows, 2 * cols)

    def body(a_ref, b_ref, o_ref):
        o_ref[...] = a_ref[...] + b_ref[...]

    spec = pl.BlockSpec((tr, cols), lambda i: (i, 0))
    return _pcall(body, name=name, grid=(rows // tr,), in_specs=[spec, spec], out_specs=spec,
                  out_shape=jax.ShapeDtypeStruct((rows, cols), F32))(a, b)


def _sum_chips_in_order(chip, own, landed, name):
    rows, cols = own.shape
    tr = _row_tile(rows, 4 * cols)

    def body(chip_ref, own_ref, land_ref, o_ref):
        me = chip_ref[0]
        acc = None
        for p in range(N_CHIPS):
            q = p ^ me
            k = jnp.where(q == 2, 0, jnp.where(q == 1, 1, 2))
            term = jnp.where(q == 0, own_ref[...], land_ref[k])
            acc = term if acc is None else acc + term
        o_ref[...] = acc

    grid_spec = pltpu.PrefetchScalarGridSpec(
        num_scalar_prefetch=1, grid=(rows // tr,),
        in_specs=[pl.BlockSpec((tr, cols), lambda i, chip_ref: (i, 0)),
                  pl.BlockSpec((3, tr, cols), lambda i, chip_ref: (0, i, 0))],
        out_specs=pl.BlockSpec((tr, cols), lambda i, chip_ref: (i, 0)))
    return _pcall(body, name=name, grid_spec=grid_spec, out_shape=jax.ShapeDtypeStruct((rows, cols), F32))(
        chip, own, landed)


def _bcast_plan(src, land, x, y, c):
    return [(src[0], land[0].at[k], (px, py, c)) for k, (px, py) in enumerate(_chips_of(x, y))]


def _sibling_plan(count):
    def plan(src, land, x, y, c):
        return [(src[a], land[a], (x, y, 1 - c)) for a in range(count)]

    return plan


def _scatter_plan(count):
    def plan(src, land, x, y, c):
        out = []
        for a in range(count):
            for k, (px, py) in enumerate(_chips_of(x, y)):
                out.append((src[a].at[2 * px + py], land[a].at[k], (px, py, c)))
        return out

    return plan


def _adamw_math(w, g, m, v):
    m2 = ADAM_B1 * m + (1.0 - ADAM_B1) * g
    v2 = ADAM_B2 * v + (1.0 - ADAM_B2) * (g * g)
    m_hat = m2 / (1.0 - ADAM_B1 ** ADAM_STEP)
    v_hat = v2 / (1.0 - ADAM_B2 ** ADAM_STEP)
    delta = -ADAM_LR * (m_hat / (jnp.sqrt(v_hat) + ADAM_EPS) + ADAM_WD * w)
    return delta, m2, v2


def _adamw(w, m, v, grads, name):
    rows, cols = w.shape
    tr = _row_tile(rows, cols)
    ng = len(grads)

    def body(*refs):
        w_ref, m_ref, v_ref = refs[:3]
        g = refs[3][...]
        for k in range(1, ng):
            g = g + refs[3 + k][...]
        g_ref, d_ref, m2_ref, v2_ref = refs[3 + ng:]
        delta, m2, v2 = _adamw_math(w_ref[...], g, m_ref[...], v_ref[...])
        g_ref[...] = g
        d_ref[...] = delta
        m2_ref[...] = m2
        v2_ref[...] = v2

    spec = pl.BlockSpec((tr, cols), lambda i: (i, 0))
    return _pcall(body, name=name, grid=(rows // tr,), in_specs=[spec] * (3 + ng), out_specs=[spec] * 4,
                  out_shape=[jax.ShapeDtypeStruct((rows, cols), F32)] * 4, vmem_mb=48)(w, m, v, *grads)


def _ada_adamw(ct, dmod, w, m, v):
    rows, cols = w.shape
    tr = _row_tile(rows, cols)

    def body(ct_ref, dm_ref, w_ref, m_ref, v_ref, g_ref, d_ref, m2_ref, v2_ref):
        cv = ct_ref[...]
        ca = cv * _sigmoid(cv)
        g = ca[:, 0:1] * dm_ref[0:1, :]
        for b in range(1, N_DEV):
            g = g + ca[:, b:b + 1] * dm_ref[b:b + 1, :]
        delta, m2, v2 = _adamw_math(w_ref[...], g, m_ref[...], v_ref[...])
        g_ref[...] = g
        d_ref[...] = delta
        m2_ref[...] = m2
        v2_ref[...] = v2

    spec = pl.BlockSpec((tr, cols), lambda i: (i, 0))
    return _pcall(body, name="ada_adamw", grid=(rows // tr,),
                  in_specs=[pl.BlockSpec((tr, N_DEV), lambda i: (i, 0)), pl.BlockSpec((N_DEV, cols), lambda i: (0, 0)),
                            spec, spec, spec],
                  out_specs=[spec] * 4, out_shape=[jax.ShapeDtypeStruct((rows, cols), F32)] * 4,
                  vmem_mb=48)(ct, dmod, w, m, v)


def _mod_fwd(c_all, w, b):
    cols = w.shape[1]
    tn = cols // 3

    def body(c_ref, w_ref, b_ref, o_ref):
        cv = c_ref[...]
        ca = (cv * _sigmoid(cv)).astype(BF)
        o_ref[...] = _dot(ca, w_ref[...].astype(BF)) + b_ref[...]

    return _pcall(body, name="mod_fwd", grid=(3,),
                  in_specs=[pl.BlockSpec((N_DEV, D), lambda j: (0, 0)), pl.BlockSpec((D, tn), lambda j: (0, j)),
                            pl.BlockSpec((1, tn), lambda j: (0, j))],
                  out_specs=pl.BlockSpec((N_DEV, tn), lambda j: (0, j)),
                  out_shape=jax.ShapeDtypeStruct((N_DEV, cols), F32))(c_all, w, b)


def _resident(shape):
    zeros = (0,) * len(shape)
    return pl.BlockSpec(shape, lambda *_: zeros, pipeline_mode=pl.Buffered(1))


def _mixer_fwd(x, g, scale, shift, gate1, w_in4, cw, cb, wa, ba, wx, bx, lam, lg, lb, ws, bst, wba, wbb, wo,
               tm=256, chunk=256, piece=512):
    T = x.shape[0]
    tm = min(tm, T)
    ns = w_in4.shape[2]
    per = ns // piece

    def body(x_ref, g_ref, sc_ref, sh_ref, g1_ref, w_ref, cw_ref, cb_ref, wa_ref, ba_ref, wx_ref, bx_ref, lam_ref,
             lg_ref, lb_ref, ws_ref, bst_ref, wba_ref, wbb_ref, wo_ref,
             h1_ref, z_ref, hl_ref, yap_ref, ybp_ref, mg_ref, ya_ref, yb_ref, o_ref, x2_ref,
             xc_ref, r_ref, ig_ref, mu_ref, a_ref, prev, hc):
        i = pl.program_id(0)

        @pl.when(i == 0)
        def _():
            prev[...] = jnp.zeros_like(prev)
            hc[...] = jnp.zeros_like(hc)

        xv = x_ref[...]
        _, xh = _rms_stats(xv)
        h = ((xh * g_ref[...]) * (1.0 + sc_ref[...]) + sh_ref[...]).astype(BF)
        h1_ref[...] = h

        def proj(col, width):
            for c0 in range(col, col + width, piece):
                w = min(piece, col + width - c0)
                j, off = c0 // ns, c0 % ns
                z_ref[:, c0:c0 + w] = _dot(h, w_ref[j, :, off:off + w])

        def lru_chunk(c0):
            cs = slice(c0, c0 + chunk)
            xr = z_ref[:, cs]
            pv = prev[:, cs]
            xc = (cb_ref[:, cs] + cw_ref[3:4, cs] * xr + cw_ref[2:3, cs] * _shift_down(xr, pv, 1)
                  + cw_ref[1:2, cs] * _shift_down(xr, pv, 2) + cw_ref[0:1, cs] * _shift_down(xr, pv, 3))
            prev[:, cs] = xr[tm - SUBLANES:tm]
            r, ig, _, a, mult = _lru_gates(xc, wa_ref, ba_ref[:, cs], wx_ref, bx_ref[:, cs], lam_ref[:, cs],
                                           head0=c0 // HD)
            xc_ref[:, cs] = xc.astype(BF)
            r_ref[:, cs] = r.astype(BF)
            ig_ref[:, cs] = ig.astype(BF)
            a_ref[:, cs] = a
            mu_ref[:, cs] = mult.astype(BF)
            a, u = _scan_rows(a, mult * (ig * xc), reverse=False)
            hv = u + a * hc[SUBLANES - 1:SUBLANES, cs]
            hc[:, cs] = hv[tm - SUBLANES:tm]
            hl_ref[:, cs] = hv
            yap_ref[:, cs] = (hv * _gelu(z_ref[:, D + c0:D + c0 + chunk])).astype(BF)

        proj(0, chunk)
        proj(D, chunk)
        for c0 in range(0, D, chunk):
            if c0 + chunk < D:
                proj(c0 + chunk, chunk)
                proj(D + c0 + chunk, chunk)
            else:
                proj(2 * D, 2 * D)
            lru_chunk(c0)
        proj(4 * D, 2 * D)
        _, xhn = _layernorm_stats(_gelu(z_ref[:, 3 * D:4 * D]))
        vln = xhn * lg_ref[...] + lb_ref[...]
        _, mixed = _sgu_mix(vln, ws_ref, bst_ref, tm)
        ybp = (_gelu(z_ref[:, 2 * D:3 * D]) * mixed).astype(BF)
        ybp_ref[...] = ybp
        ya = _dot(yap_ref[...], wba_ref[...])
        yb = _dot(ybp, wbb_ref[...])
        merged = (_sigmoid_t(z_ref[:, 4 * D:5 * D]) * ya + _sigmoid_t(z_ref[:, 5 * D:6 * D]) * yb).astype(BF)
        o = _dot(merged, wo_ref[...])
        x2_ref[...] = xv + g1_ref[...] * o
        mg_ref[...] = merged
        ya_ref[...] = ya.astype(BF)
        yb_ref[...] = yb.astype(BF)
        o_ref[...] = o.astype(BF)

    row = pl.BlockSpec((tm, D), lambda i: (i, 0))
    vec = pl.BlockSpec((1, D), lambda i: (0, 0))
    bf_row = jax.ShapeDtypeStruct((T, D), BF)
    f32_row = jax.ShapeDtypeStruct((T, D), F32)
    return _pcall(body, name="mixer_fwd", grid=(T // tm,),
                  in_specs=[row, vec, vec, vec, vec, _resident(w_in4.shape), _resident(cw.shape), vec,
                            _resident(wa.shape), vec, _resident(wx.shape), vec, vec, vec, vec,
                            _resident(ws.shape), _resident(bst.shape),
                            _resident(wba.shape), _resident(wbb.shape), _resident(wo.shape)],
                  out_specs=[row, pl.BlockSpec((tm, 6 * D), lambda i: (i, 0))] + [row] * 13,
                  out_shape=[bf_row, jax.ShapeDtypeStruct((T, 6 * D), F32), f32_row, bf_row, bf_row, bf_row, bf_row,
                             bf_row, bf_row, f32_row, bf_row, bf_row, bf_row, bf_row, f32_row],
                  scratch=[pltpu.VMEM((SUBLANES, D), F32), pltpu.VMEM((SUBLANES, D), F32)], vmem_mb=60)(
        x, g, scale, shift, gate1, w_in4, cw, cb, wa, ba, wx, bx, lam, lg, lb, ws, bst, wba, wbb, wo)


def _ffn_fwd(x2, g, scale, shift, gate2, gf, w_up4, wd, cw, cb, target, tm=256, chunk=768):
    T = x2.shape[0]
    tm = min(tm, T)
    ns = w_up4.shape[2]
    dff = wd.shape[0]
    nchunk = dff // chunk
    per = ns // chunk

    def body(x2_ref, g_ref, sc_ref, sh_ref, g2_ref, gf_ref, wu_ref, wd_ref, cw_ref, cb_ref, t_ref,
             h2_ref, up_ref, f_ref, ga_ref, vd_ref, loss_ref, dx3_ref, dfo_ref, dgf_ref, dg2_ref, prev):
        i = pl.program_id(0)

        @pl.when(i == 0)
        def _():
            prev[...] = jnp.zeros_like(prev)
            loss_ref[...] = jnp.zeros_like(loss_ref)
            dgf_ref[...] = jnp.zeros_like(dgf_ref)
            dg2_ref[...] = jnp.zeros_like(dg2_ref)

        x2v = x2_ref[...]
        _, xh2 = _rms_stats(x2v)
        h2 = ((xh2 * g_ref[...]) * (1.0 + sc_ref[...]) + sh_ref[...]).astype(BF)
        h2_ref[...] = h2

        def conv(u, col):
            cs = slice(col, col + chunk)
            p = prev[:, cs]
            hid = (cb_ref[:, cs] + cw_ref[2:3, cs] * u + cw_ref[1:2, cs] * _shift_down(u, p, 1)
                   + cw_ref[0:1, cs] * _shift_down(u, p, 2))
            prev[:, cs] = u[tm - SUBLANES:tm]
            up_ref[:, cs] = u.astype(BF)
            return hid

        def up_proj(k):
            off = (k % per) * chunk
            return (_dot(h2, wu_ref[k // per, :, off:off + chunk]),
                    _dot(h2, wu_ref[N_CHIPS // 2 + k // per, :, off:off + chunk]))

        fo = None
        nxt = up_proj(0)
        for k in range(nchunk):
            col = k * chunk
            ua, uv = nxt
            if k + 1 < nchunk:
                nxt = up_proj(k + 1)
            act = conv(ua, col)
            val = conv(uv, dff + col)
            ga, dga = _gelu_and_grad(act)
            fk = (ga * val).astype(BF)
            f_ref[:, col:col + chunk] = fk
            ga_ref[:, col:col + chunk] = ga.astype(BF)
            vd_ref[:, col:col + chunk] = (val * dga).astype(BF)
            part = _dot(fk, wd_ref[col:col + chunk, :])
            fo = part if fo is None else fo + part

        x3 = x2v + g2_ref[...] * fo
        rstd, xh = _rms_stats(x3)
        err = xh * gf_ref[...] - t_ref[...]
        loss_ref[...] += 0.5 * jnp.sum(jnp.mean(err * err, axis=-1, keepdims=True), axis=0, keepdims=True)
        dy = err * (1.0 / D)
        dgf_ref[...] += _colsum(dy * xh)
        dxh = dy * gf_ref[...]
        dx3 = rstd * (dxh - xh * jnp.mean(dxh * xh, axis=-1, keepdims=True))
        dg2_ref[...] += _colsum(dx3 * fo)
        dx3_ref[...] = dx3
        dfo_ref[...] = (g2_ref[...] * dx3).astype(BF)

    row = pl.BlockSpec((tm, D), lambda i: (i, 0))
    vec = pl.BlockSpec((1, D), lambda i: (0, 0))
    wide = pl.BlockSpec((tm, 2 * dff), lambda i: (i, 0))
    half = pl.BlockSpec((tm, dff), lambda i: (i, 0))
    return _pcall(body, name="ffn_fwd", grid=(T // tm,),
                  in_specs=[row, vec, vec, vec, vec, vec, _resident(w_up4.shape), _resident(wd.shape),
                            _resident(cw.shape), _resident(cb.shape), row],
                  out_specs=[row, wide, half, half, half, pl.BlockSpec((1, LANES), lambda i: (0, 0)), row, row, vec, vec],
                  out_shape=[jax.ShapeDtypeStruct((T, D), BF), jax.ShapeDtypeStruct((T, 2 * dff), BF),
                             jax.ShapeDtypeStruct((T, dff), BF), jax.ShapeDtypeStruct((T, dff), BF),
                             jax.ShapeDtypeStruct((T, dff), BF), jax.ShapeDtypeStruct((1, LANES), F32),
                             jax.ShapeDtypeStruct((T, D), F32), jax.ShapeDtypeStruct((T, D), BF),
                             jax.ShapeDtypeStruct((1, D), F32), jax.ShapeDtypeStruct((1, D), F32)],
                  scratch=[pltpu.VMEM((SUBLANES, 2 * dff), F32)], vmem_mb=56)(
        x2, g, scale, shift, gate2, gf, w_up4, wd, cw, cb, target)


def _ffn_bwd(dfo, wd, up, ga, vd, cw, w_up4, x2, resid, g, scale, gate, o, tm=256, chunk=1536):
    T = up.shape[0]
    tm = min(tm, T)
    dff = wd.shape[0]
    ns = w_up4.shape[2]
    nchunk = dff // chunk
    per = ns // chunk
    nrow = T // tm

    def body(dfo_ref, wd_ref, up_ref, ga_ref, vd_ref, cw_ref, wu_ref, x_ref, r_ref, g_ref, sc_ref, gt_ref, o_ref,
             du_ref, dcw_ref, dcb_ref, dx_ref, dsh_ref, dsc_ref, dg_ref, do_ref, dgt_ref, nxt):
        i = pl.program_id(0)

        @pl.when(i == 0)
        def _():
            nxt[...] = jnp.zeros_like(nxt)
            for ref in (dcw_ref, dcb_ref, dsh_ref, dsc_ref, dg_ref, dgt_ref):
                ref[...] = jnp.zeros_like(ref)

        dfo_t = dfo_ref[...]

        def conv_bwd(dh, col):
            cs = slice(col, col + chunk)
            n8 = nxt[:, cs]
            dh1 = _shift_up(dh, n8, 1)
            dh2 = _shift_up(dh, n8, 2)
            nxt[:, cs] = dh[0:SUBLANES]
            du = (cw_ref[2:3, cs] * dh + cw_ref[1:2, cs] * dh1 + cw_ref[0:1, cs] * dh2).astype(BF)
            du_ref[:, cs] = du
            u = up_ref[:, cs].astype(F32)
            dcw_ref[2:3, cs] += _colsum(dh * u)
            dcw_ref[1:2, cs] += _colsum(dh1 * u)
            dcw_ref[0:1, cs] += _colsum(dh2 * u)
            dcb_ref[:, cs] += _colsum(dh)
            return du

        def down_bwd(k):
            return _dot_nt(dfo_t, wd_ref[k * chunk:(k + 1) * chunk, :])

        dh = None
        df_next = down_bwd(0)
        for k in range(nchunk):
            col = k * chunk
            off = (k % per) * chunk
            df = df_next
            if k + 1 < nchunk:
                df_next = down_bwd(k + 1)
            du_a = conv_bwd(df * vd_ref[:, col:col + chunk].astype(F32), col)
            du_v = conv_bwd(df * ga_ref[:, col:col + chunk].astype(F32), dff + col)
            part = (_dot_nt(du_a, wu_ref[k // per, :, off:off + chunk])
                    + _dot_nt(du_v, wu_ref[N_CHIPS // 2 + k // per, :, off:off + chunk]))
            dh = part if dh is None else dh + part

        rstd, xh = _rms_stats(x_ref[...])
        dsh_ref[...] += _colsum(dh)
        dsc_ref[...] += _colsum(dh * (xh * g_ref[...]))
        dn = dh * (1.0 + sc_ref[...])
        dg_ref[...] += _colsum(dn * xh)
        dxh = dn * g_ref[...]
        dx = r_ref[...] + rstd * (dxh - xh * jnp.mean(dxh * xh, axis=-1, keepdims=True))
        dx_ref[...] = dx
        do_ref[...] = (gt_ref[...] * dx).astype(BF)
        dgt_ref[...] += _colsum(dx * o_ref[...].astype(F32))

    rev = lambda i: (nrow - 1 - i, 0)
    row = pl.BlockSpec((tm, D), rev)
    vec = pl.BlockSpec((1, D), lambda i: (0, 0))
    wide = pl.BlockSpec((tm, 2 * dff), rev)
    half = pl.BlockSpec((tm, dff), rev)
    cw3 = pl.BlockSpec((3, 2 * dff), lambda i: (0, 0))
    cb1 = pl.BlockSpec((1, 2 * dff), lambda i: (0, 0))
    vshape = jax.ShapeDtypeStruct((1, D), F32)
    return _pcall(body, name="ffn_bwd", grid=(nrow,),
                  in_specs=[row, _resident(wd.shape), wide, half, half, _resident(cw.shape), _resident(w_up4.shape),
                            row, row, vec, vec, vec, row],
                  out_specs=[wide, cw3, cb1, row, vec, vec, vec, row, vec],
                  out_shape=[jax.ShapeDtypeStruct((T, 2 * dff), BF), jax.ShapeDtypeStruct((3, 2 * dff), F32),
                             jax.ShapeDtypeStruct((1, 2 * dff), F32), jax.ShapeDtypeStruct((T, D), F32),
                             vshape, vshape, vshape, jax.ShapeDtypeStruct((T, D), BF), vshape],
                  scratch=[pltpu.VMEM((SUBLANES, 2 * dff), F32)], vmem_mb=60)(
        dfo, wd, up, ga, vd, cw, w_up4, x2, resid, g, scale, gate, o)


def _mm_tn_cols(a, b, name, nshard, nb, mb=None, tm=TN_ROWS):
    T, M = a.shape
    tm = min(tm, T)
    mb = M if mb is None else mb
    ns = b.shape[1] // nshard
    per = ns // nb
    nk = T // tm
    vmem_mb = (2 * 2 * tm * (mb + nb) + 2 * (4 + 2) * mb * nb) // 2 ** 20 + 8

    def body(a_ref, b_ref, o_ref, c_ref):
        k = pl.program_id(2)

        @pl.when(k == 0)
        def _():
            o_ref[...] = jnp.zeros_like(o_ref)

        o_ref[0] += _dot_tn(a_ref[...], b_ref[...])

        @pl.when(k == nk - 1)
        def _():
            c_ref[...] = o_ref[...].astype(BF)

    out_spec = pl.BlockSpec((1, mb, nb), lambda m, t, k: (t // per, m, t % per))
    return _pcall(body, name=name, grid=(M // mb, nshard * per, nk),
                  in_specs=[pl.BlockSpec((tm, mb), lambda m, t, k: (k, m)),
                            pl.BlockSpec((tm, nb), lambda m, t, k: (k, t))],
                  out_specs=[out_spec, out_spec],
                  out_shape=[jax.ShapeDtypeStruct((nshard, M, ns), F32), jax.ShapeDtypeStruct((nshard, M, ns), BF)],
                  vmem_mb=vmem_mb)(a, b)


def _mm_nt_normbwd(dz, w4, x, resid, g, scale, name, tm=256):
    T = x.shape[0]
    tm = min(tm, T)
    ns = w4.shape[2]

    def body(dz_ref, w_ref, x_ref, r_ref, g_ref, sc_ref, dx_ref, dsh_ref, dsc_ref, dg_ref):
        i = pl.program_id(0)

        @pl.when(i == 0)
        def _():
            dsh_ref[...] = jnp.zeros_like(dsh_ref)
            dsc_ref[...] = jnp.zeros_like(dsc_ref)
            dg_ref[...] = jnp.zeros_like(dg_ref)

        dh = None
        for j in range(N_CHIPS):
            part = _dot_nt(dz_ref[:, j * ns:(j + 1) * ns], w_ref[j])
            dh = part if dh is None else dh + part
        rstd, xh = _rms_stats(x_ref[...])
        dsh_ref[...] += _colsum(dh)
        dsc_ref[...] += _colsum(dh * (xh * g_ref[...]))
        dn = dh * (1.0 + sc_ref[...])
        dg_ref[...] += _colsum(dn * xh)
        dxh = dn * g_ref[...]
        dx_ref[...] = r_ref[...] + rstd * (dxh - xh * jnp.mean(dxh * xh, axis=-1, keepdims=True))

    row = pl.BlockSpec((tm, D), lambda i: (i, 0))
    vec = pl.BlockSpec((1, D), lambda i: (0, 0))
    return _pcall(body, name=name, grid=(T // tm,),
                  in_specs=[pl.BlockSpec((tm, N_CHIPS * ns), lambda i: (i, 0)), _resident(w4.shape), row, row, vec, vec],
                  out_specs=[row, vec, vec, vec],
                  out_shape=[jax.ShapeDtypeStruct((T, D), F32)] + [jax.ShapeDtypeStruct((1, D), F32)] * 3,
                  vmem_mb=48)(dz, w4, x, resid, g, scale)


def _mix_bwd(do, ya, yb, z, wo, wba, wbb, tm=512):
    T = do.shape[0]
    tm = min(tm, T)

    def body(do_ref, ya_ref, yb_ref, ga_ref, gb_ref, wo_ref, wa_ref, wb_ref,
             dz_ref, dya_ref, dyb_ref, dyap_ref, dybp_ref):
        dm = _dot_nt(do_ref[...], wo_ref[...])
        sa = _sigmoid_t(ga_ref[...])
        sb = _sigmoid_t(gb_ref[...])
        dya = (sa * dm).astype(BF)
        dyb = (sb * dm).astype(BF)
        dz_ref[:, 0:D] = (dm * ya_ref[...].astype(F32) * sa * (1.0 - sa)).astype(BF)
        dz_ref[:, D:2 * D] = (dm * yb_ref[...].astype(F32) * sb * (1.0 - sb)).astype(BF)
        dya_ref[...] = dya
        dyb_ref[...] = dyb
        dyap_ref[...] = _dot_nt(dya, wa_ref[...]).astype(BF)
        dybp_ref[...] = _dot_nt(dyb, wb_ref[...]).astype(BF)

    row = pl.BlockSpec((tm, D), lambda i: (i, 0))
    wspec = pl.BlockSpec((D, D), lambda i: (0, 0))
    return _pcall(body, name="mix_bwd", grid=(T // tm,),
                  in_specs=[row, row, row, pl.BlockSpec((tm, D), lambda i: (i, 4)),
                            pl.BlockSpec((tm, D), lambda i: (i, 5)), wspec, wspec, wspec],
                  out_specs=[pl.BlockSpec((tm, 2 * D), lambda i: (i, 2)), row, row, row, row],
                  out_shape=[jax.ShapeDtypeStruct((T, 6 * D), BF)] + [jax.ShapeDtypeStruct((T, D), BF)] * 4,
                  vmem_mb=48)(do, ya, yb, z, z, wo, wba, wbb)


def _sgu_bwd(dz, dyb_pre, z, lg, lb, ws, bst, tb=512):
    T = z.shape[0]
    tb = min(tb, T)

    def body(dz_in, dy_ref, zu_ref, zv_ref, lg_ref, lb_ref, ws_ref, bst_ref,
             dz_ref, dws_ref, dbst_ref, dlg_ref, dlb_ref):
        del dz_in
        i = pl.program_id(0)

        @pl.when(i == 0)
        def _():
            dws_ref[...] = jnp.zeros_like(dws_ref)
            dbst_ref[...] = jnp.zeros_like(dbst_ref)
            dlg_ref[...] = jnp.zeros_like(dlg_ref)
            dlb_ref[...] = jnp.zeros_like(dlb_ref)

        gu, dgu = _gelu_and_grad(zu_ref[...])
        gv, dgv = _gelu_and_grad(zv_ref[...])
        rstd, xh = _layernorm_stats(gv)
        vln = xh * lg_ref[...] + lb_ref[...]
        wm, mixed = _sgu_mix(vln, ws_ref, bst_ref, tb)
        dy = dy_ref[...].astype(F32)
        dz_ref[:, 0:D] = (dy * mixed * dgu).astype(BF)
        dmixed = dy * gu
        ri = lax.broadcasted_iota(jnp.int32, (SGU_BLOCK, SGU_BLOCK), 0)
        ci = lax.broadcasted_iota(jnp.int32, (SGU_BLOCK, SGU_BLOCK), 1)
        blocks = []
        for blk in range(tb // SGU_BLOCK):
            rs = slice(blk * SGU_BLOCK, (blk + 1) * SGU_BLOCK)
            cols = []
            for g in range(HEADS):
                cs = slice(g * HD, (g + 1) * HD)
                dmg = dmixed[rs, cs]
                dmb = dmg.astype(BF)
                dbst_ref[:, g:g + 1] += jnp.sum(dmg, axis=1, keepdims=True)
                dws_ref[g] += jnp.where(ri >= ci, _dot_nt(dmb, vln[rs, cs].astype(BF)), 0.0)
                cols.append(_dot_tn(wm[g], dmb))
            blocks.append(jnp.concatenate(cols, axis=1))
        dvln = blocks[0] if len(blocks) == 1 else jnp.concatenate(blocks, axis=0)
        dlg_ref[...] += _colsum(dvln * xh)
        dlb_ref[...] += _colsum(dvln)
        dxh = dvln * lg_ref[...]
        dgv_in = rstd * (dxh - jnp.mean(dxh, axis=-1, keepdims=True)
                         - xh * jnp.mean(dxh * xh, axis=-1, keepdims=True))
        dz_ref[:, D:2 * D] = (dgv_in * dgv).astype(BF)

    row = pl.BlockSpec((tb, D), lambda i: (i, 0))
    vec = pl.BlockSpec((1, D), lambda i: (0, 0))
    wspec = pl.BlockSpec((HEADS, SGU_BLOCK, SGU_BLOCK), lambda i: (0, 0, 0))
    bspec = pl.BlockSpec((SGU_BLOCK, HEADS), lambda i: (0, 0))
    return _pcall(body, name="sgu_bwd", grid=(T // tb,),
                  in_specs=[HBM_SPEC, row, pl.BlockSpec((tb, D), lambda i: (i, 2)),
                            pl.BlockSpec((tb, D), lambda i: (i, 3)), vec, vec, wspec, bspec],
                  out_specs=[pl.BlockSpec((tb, 2 * D), lambda i: (i, 1)), wspec, bspec, vec, vec],
                  out_shape=[jax.ShapeDtypeStruct(dz.shape, BF),
                             jax.ShapeDtypeStruct((HEADS, SGU_BLOCK, SGU_BLOCK), F32),
                             jax.ShapeDtypeStruct((SGU_BLOCK, HEADS), F32),
                             jax.ShapeDtypeStruct((1, D), F32), jax.ShapeDtypeStruct((1, D), F32)],
                  aliases={0: 0}, vmem_mb=48)(dz, dyb_pre, z, z, lg, lb, ws, bst)


def _rglru_bwd(dz, dya_pre, z, h, xc_s, r_s, ig_s, mult_s, a_s, cw, wa, wx, lam, tb=256, chunk=512):
    T = z.shape[0]
    tb = min(tb, T)
    nrow = T // tb
    per = tb // SUBLANES

    def body(dz_in, dy_ref, xr_ref, gr_ref, h_ref, hh_ref, xc_ref, r_ref, ig_ref, mu_ref, a_ref, cw_ref, wa_ref,
             wx_ref, lam_ref, dz_ref, dcw_ref, dcb_ref, dwa_ref, dba_ref, dwx_ref, dbx_ref, dlam_ref, carry, nxt):
        del dz_in
        i = pl.program_id(0)
        first_block = i == nrow - 1

        @pl.when(i == 0)
        def _():
            carry[...] = jnp.zeros_like(carry)
            nxt[...] = jnp.zeros_like(nxt)
            for ref in (dcw_ref, dcb_ref, dwa_ref, dba_ref, dwx_ref, dbx_ref, dlam_ref):
                ref[...] = jnp.zeros_like(ref)

        rows = lax.broadcasted_iota(jnp.int32, (tb, chunk), 0)
        for c0 in range(0, D, chunk):
            cs = slice(c0, c0 + chunk)
            head0 = c0 // HD
            xc = xc_ref[:, cs].astype(F32)
            r = r_ref[:, cs].astype(F32)
            ig = ig_ref[:, cs].astype(F32)
            mult = mu_ref[:, cs].astype(F32)
            a = a_ref[:, cs]
            lam = lam_ref[:, cs]
            ls = _log_sigmoid(lam)
            hv = h_ref[:, cs]
            hprev = _shift_down(hv, jnp.where(first_block, 0.0, hh_ref[:, cs]), 1)
            gg, dgg = _gelu_and_grad(gr_ref[:, cs])
            dy = dy_ref[:, cs].astype(F32)
            dz_ref[:, D + c0:D + c0 + chunk] = (dy * hv * dgg).astype(BF)

            v = dy * gg + jnp.where(rows == tb - 1, carry[0:1, cs], 0.0)
            q = jnp.where(rows < tb - 1, pltpu.roll(a, tb - 1, 0), 0.0)
            _, gsc = _scan_rows(q, v, reverse=True)
            carry[:, cs] = (a * gsc)[0:SUBLANES]

            xi = ig * xc
            dmult = gsc * xi
            dxi = gsc * mult
            dig = dxi * xc
            dxc = dxi * ig
            dlog_a = gsc * hprev * a - dmult * (a * a) * pl.reciprocal(mult, approx=True)
            dlam_ref[:, cs] += _colsum(dlog_a * r) * (LRU_C * _sigmoid(-lam))
            dpr = dlog_a * (LRU_C * ls) * r * (1.0 - r)
            dpi = dig * ig * (1.0 - ig)
            dba_ref[:, cs] += _colsum(dpr)
            dbx_ref[:, cs] += _colsum(dpi)
            back = []
            for hh in range(chunk // HD):
                hs = slice(hh * HD, (hh + 1) * HD)
                xh = xc[:, hs].astype(BF)
                dprh = dpr[:, hs].astype(BF)
                dpih = dpi[:, hs].astype(BF)
                dwa_ref[head0 + hh] += _dot_tn(xh, dprh)
                dwx_ref[head0 + hh] += _dot_tn(xh, dpih)
                back.append(_dot_nt(dprh, wa_ref[head0 + hh].astype(BF))
                            + _dot_nt(dpih, wx_ref[head0 + hh].astype(BF)))
            dxc = dxc + (back[0] if len(back) == 1 else jnp.concatenate(back, axis=1))

            n8 = nxt[:, cs]
            d1 = _shift_up(dxc, n8, 1)
            d2 = _shift_up(dxc, n8, 2)
            d3 = _shift_up(dxc, n8, 3)
            nxt[:, cs] = dxc[0:SUBLANES]
            dz_ref[:, cs] = (cw_ref[3:4, cs] * dxc + cw_ref[2:3, cs] * d1 + cw_ref[1:2, cs] * d2
                             + cw_ref[0:1, cs] * d3).astype(BF)
            xr = xr_ref[:, cs]
            dcw_ref[3:4, cs] += _colsum(dxc * xr)
            dcw_ref[2:3, cs] += _colsum(d1 * xr)
            dcw_ref[1:2, cs] += _colsum(d2 * xr)
            dcw_ref[0:1, cs] += _colsum(d3 * xr)
            dcb_ref[:, cs] += _colsum(dxc)

    rev = lambda col: (lambda i: (nrow - 1 - i, col))
    row = pl.BlockSpec((tb, D), rev(0))
    halo = pl.BlockSpec((SUBLANES, D), lambda i: (jnp.maximum((nrow - 1 - i) * per - 1, 0), 0))
    vec = pl.BlockSpec((1, D), lambda i: (0, 0))
    wspec = pl.BlockSpec((HEADS, HD, HD), lambda i: (0, 0, 0))
    c4 = pl.BlockSpec((4, D), lambda i: (0, 0))
    wshape = jax.ShapeDtypeStruct((HEADS, HD, HD), F32)
    vshape = jax.ShapeDtypeStruct((1, D), F32)
    return _pcall(body, name="rglru_bwd", grid=(nrow,),
                  in_specs=[HBM_SPEC, row, row, pl.BlockSpec((tb, D), rev(1)), row, halo,
                            row, row, row, row, row, c4, wspec, wspec, vec],
                  out_specs=[pl.BlockSpec((tb, 2 * D), rev(0)), c4, vec, wspec, vec, wspec, vec, vec],
                  out_shape=[jax.ShapeDtypeStruct(dz.shape, BF), jax.ShapeDtypeStruct((4, D), F32), vshape,
                             wshape, vshape, wshape, vshape, vshape],
                  scratch=[pltpu.VMEM((SUBLANES, D), F32), pltpu.VMEM((SUBLANES, D), F32)],
                  aliases={0: 0}, vmem_mb=56)(dz, dya_pre, z, z, h, h, xc_s, r_s, ig_s, mult_s, a_s, cw, wa, wx, lam)


def _pack_rows(parts):
    out = []
    for p in parts:
        q = p.reshape(-1, LANES)
        pad = (-q.shape[0]) % SUBLANES
        if pad:
            q = jnp.concatenate([q, jnp.zeros((pad, LANES), q.dtype)], axis=0)
        out.append(q)
    return jnp.concatenate(out, axis=0)


def _rows_of(shape):
    n = 1
    for s in shape:
        n *= s
    rows = n // LANES
    return rows + (-rows) % SUBLANES


def kernel(x, c, w_ada, b_ada, norm_mix_g, w_in, rnn_conv_w, rnn_conv_b, lru_w_a, lru_b_a, lru_w_x, lru_b_x, lru_lambda, sgu_ln_g, sgu_ln_b, sgu_w_s, sgu_b_s, w_branch_a, w_branch_b, w_out, norm_ffn_g, w_up, ffn_conv_w, ffn_conv_b, w_down, norm_final_g, loss_target, m_w_ada, m_b_ada, m_norm_mix_g, m_w_in, m_rnn_conv_w, m_rnn_conv_b, m_lru_w_a, m_lru_b_a, m_lru_w_x, m_lru_b_x, m_lru_lambda, m_sgu_ln_g, m_sgu_ln_b, m_sgu_w_s, m_sgu_b_s, m_w_branch_a, m_w_branch_b, m_w_out, m_norm_ffn_g, m_w_up, m_ffn_conv_w, m_ffn_conv_b, m_w_down, m_norm_final_g, v_w_ada, v_b_ada, v_norm_mix_g, v_w_in, v_rnn_conv_w, v_rnn_conv_b, v_lru_w_a, v_lru_b_a, v_lru_w_x, v_lru_b_x, v_lru_lambda, v_sgu_ln_g, v_sgu_ln_b, v_sgu_w_s, v_sgu_b_s, v_w_branch_a, v_w_branch_b, v_w_out, v_norm_ffn_g, v_w_up, v_ffn_conv_w, v_ffn_conv_b, v_w_down, v_norm_final_g):
    args = dict(locals())
    T = x.shape[1]
    mx, my, mc = lax.axis_index("x"), lax.axis_index("y"), lax.axis_index("c")
    chip = 2 * mx + my
    dev = 2 * chip + mc
    vec = lambda a: a.reshape(1, -1)

    xt = x.reshape(T, D)
    tgt = loss_target.reshape(T, D)
    ns = w_in.shape[2]
    dff = w_down.shape[1] * N_CHIPS

    c_all = _gather8(c.reshape(SUBLANES, LANES), "gather_c").reshape(N_DEV, D)
    b_ada_sh = lax.dynamic_slice(b_ada, (0, chip * ns), (1, ns))
    mod_sh = _mod_fwd(c_all, w_ada[0], b_ada_sh)

    mixer_w = _cast_shards([w_in[0], w_branch_a[0], w_branch_b[0], w_out[0]], "cast_mixer_weights")
    w_in4, wba4, wbb4, wo4, rcw4, fcw4, mod4 = _gather_weights(
        list(mixer_w) + [rnn_conv_w[0], ffn_conv_w[0], mod_sh], [True] * 4 + [False] * 3)
    late = _cast_shards([w_up[0], w_down[0]], "cast_late", after=mod4)
    late_plan = _gather_plan(len(late))
    late_handle, late_token = _remote_start(
        late, [lax.empty((N_CHIPS,) + w.shape, w.dtype) for w in late], late_plan, 3 * len(late), "gather_late_start")
    rcw_full = jnp.transpose(rcw4, (1, 0, 2)).reshape(4, D)
    fcw_full = jnp.transpose(fcw4, (1, 0, 2)).reshape(3, 2 * dff)
    mod = lax.dynamic_index_in_dim(mod4, dev, axis=1, keepdims=False).reshape(1, 6 * D)
    shift1, scale1, gate1, shift2, scale2, gate2 = [mod[:, k * D:(k + 1) * D] for k in range(6)]

    bst = jnp.transpose(sgu_b_s[0])
    wba_full = wba4.reshape(D, D)
    wbb_full = wbb4.reshape(D, D)
    wo_full = wo4.reshape(D, D)
    h1, z, h_lru, ya_pre, yb_pre, merged, ya, yb, o1, x2, lru_xc, lru_r, lru_i, lru_mult, lru_a = _mixer_fwd(
        xt, norm_mix_g, scale1 + late_token[0:1, 0:1], shift1, gate1, w_in4, rcw_full, rnn_conv_b,
        lru_w_a[0], lru_b_a, lru_w_x[0], lru_b_x, lru_lambda, sgu_ln_g, sgu_ln_b, sgu_w_s[0], bst,
        wba_full, wbb_full, wo_full)
    late, late_lands = _remote_wait(late_handle, late_plan, o1, "gather_late_wait")
    w_up4, w_down4 = _place_own(late, late_lands)
    wd_full = w_down4.reshape(dff, D)
    h2, up, f, ffn_ga, ffn_vd, loss_part, dx3, dfo, dgf, dgate2 = _ffn_fwd(
        x2, norm_ffn_g, scale2, shift2, gate2, vec(norm_final_g), w_up4, wd_full, fcw_full, ffn_conv_b, tgt)

    dup, dfcw, dfcb, dx2, dshift2, dscale2, dg_ffn, do1, dgate1 = _ffn_bwd(
        dfo, wd_full, up, ffn_ga, ffn_vd, fcw_full, w_up4, x2, dx3, norm_ffn_g, scale2, gate1, o1)
    dwd = _mm_tn_cols(f, dfo, "dw_down", 1, D, mb=D, tm=TN_ROWS_SQUARE)
    dw_up4 = _mm_tn_cols(h2, dup, "dw_up", N_CHIPS, ns)
    dz, dya, dyb, dya_pre, dyb_pre = _mix_bwd(do1, ya, yb, z, wo_full, wba_full, wbb_full)
    dwo = _mm_tn_cols(merged, do1, "dw_out", 1, D)
    dwba = _mm_tn_cols(ya_pre, dya, "dw_branch_a", 1, D)
    dwbb = _mm_tn_cols(yb_pre, dyb, "dw_branch_b", 1, D)

    chip_id = chip.astype(jnp.int32).reshape(1)

    def reduce_start(group, name):
        wire = [g16.reshape(N_CHIPS, -1, g16.shape[-1]) for _, (_, g16) in group]
        lands = [lax.empty((3,) + w.shape[1:], w.dtype) for w in wire]
        return _remote_start(wire, lands, _scatter_plan(len(group)), 3 * len(group), "scatter_start_" + name)

    def reduce_finish(group, handle, after, name):
        _, landed = _remote_wait(handle, _scatter_plan(len(group)), after, "scatter_wait_" + name)
        return [_sum_own_and_landed(chip_id, g32.reshape(N_CHIPS, -1, g32.shape[-1]), l, "sum_chips_" + n)
                for (n, (g32, _)), l in zip(group, landed)]

    group1 = [("w_up", dw_up4), ("w_down", dwd), ("w_branch_a", dwba), ("w_branch_b", dwbb), ("w_out", dwo)]
    handle1, token1 = reduce_start(group1, "late")
    dz, dws, dbst, dlg, dlb = _sgu_bwd(dz, dyb_pre, z, sgu_ln_g + token1[0:1, 0:1], sgu_ln_b, sgu_w_s[0], bst)
    dz, drcw, drcb, dwa, dba, dwx, dbx, dlam = _rglru_bwd(
        dz, dya_pre, z, h_lru, lru_xc, lru_r, lru_i, lru_mult, lru_a, rcw_full, lru_w_a[0], lru_w_x[0], lru_lambda)
    early_small = [("rnn_conv_b", drcb), ("lru_w_a", dwa), ("lru_b_a", dba), ("lru_w_x", dwx), ("lru_b_x", dbx),
                   ("lru_lambda", dlam), ("sgu_ln_g", dlg), ("sgu_ln_b", dlb), ("sgu_w_s", dws),
                   ("sgu_b_s", jnp.transpose(dbst)), ("norm_ffn_g", dg_ffn),
                   ("ffn_conv_b", dfcb), ("norm_final_g", dgf)]
    r_early = sum(_rows_of(args[n].shape) for n, _ in early_small)
    early_pack = _pack_rows([g for _, g in early_small] + [drcw, dfcw])
    early_pack = jnp.concatenate(
        [early_pack, jnp.zeros(((-early_pack.shape[0]) % 256, LANES), F32)], axis=0)
    early_chip = _add_pair(early_pack, _swap_cores([early_pack], "swap_small_grads")[0], "sum_cores_small_grads")
    early_handle, token3 = _remote_start([early_chip], [lax.empty((3,) + early_chip.shape, F32)], _bcast_plan, 3,
                                         "small_grads_start")
    def swap_start(totals, name):
        lands = [lax.empty(t.shape, t.dtype) for t in totals]
        return _remote_start(totals, lands, _sibling_plan(len(totals)), len(totals), "swap_sums_start_" + name)

    out = {}

    def swap_finish(group, handle, after, name):
        mine, theirs = _remote_wait(handle, _sibling_plan(len(group)), after, "swap_sums_wait_" + name)
        for (n, _), a, b in zip(group, mine, theirs):
            shape = args[n].shape
            res = _adamw(args[n][0], args["m_" + n][0], args["v_" + n][0], [a, b], "adamw_" + n)
            for kind, r in zip(("grad_", "delta_", "new_m_", "new_v_"), res):
                out[kind + n] = r.reshape(shape)
        return res[3]

    swap1, token4 = swap_start(reduce_finish(group1, handle1, drcb, "late"), "late")
    group2 = [("w_in", _mm_tn_cols(h1, dz, "dw_in", N_CHIPS, ns))]
    handle2, token2 = reduce_start(group2, "in")
    tokens = token2[0:1, 0:1] + token3[0:1, 0:1] + token4[0:1, 0:1]
    grad_x, dshift1, dscale1, dg_mix = _mm_nt_normbwd(
        dz, w_in4, xt, dx2, norm_mix_g + tokens, scale1, "dh1_norm_bwd")
    swap2, token5 = swap_start(reduce_finish(group2, handle2, dg_mix, "in"), "in")
    dmod = jnp.concatenate([dshift1, dscale1, dgate1, dshift2, dscale2, dgate2], axis=1)
    last = swap_finish(group1, swap1, token5, "late")
    swap_finish(group2, swap2, last, "in")

    late_small = [("b_ada", dmod), ("norm_mix_g", dg_mix)]
    small = late_small + early_small
    late_all = _gather8(_pack_rows([g for _, g in late_small] + [loss_part]), "gather_late_small_grads")
    late_sum = _sum_parts(late_all, "sum_late_small_grads")
    r_late = sum(_rows_of(args[n].shape) for n, _ in late_small)
    loss = late_sum[r_late, 0]
    late_sum = late_sum[:r_late]
    _, (early_landed,) = _remote_wait(early_handle, _bcast_plan, dg_mix, "small_grads_wait")
    early_sum = _sum_chips_in_order(chip_id, early_chip, early_landed, "sum_early_small_grads")
    r_small = sum(_rows_of(args[n].shape) for n, _ in small)
    r_pad = r_small + (-r_small) % 256
    fill = jnp.zeros((r_pad - r_small, LANES), F32)
    g_small = jnp.concatenate([late_sum, early_sum[:r_early], fill], axis=0)

    def pack_small(prefix):
        return jnp.concatenate([_pack_rows([args[prefix + n] for n, _ in small]), fill], axis=0)

    res = _adamw(pack_small(""), pack_small("m_"), pack_small("v_"), [g_small], "adamw_small")
    off = 0
    for n, _ in small:
        shape = args[n].shape
        rows = _rows_of(shape)
        for kind, r in zip(("grad_", "delta_", "new_m_", "new_v_"), res):
            out[kind + n] = r[off:off + rows].reshape(shape)
        off += rows

    rcw_cols = rnn_conv_w.shape[2]
    g_rcw = lax.dynamic_slice(early_sum[r_early:r_early + 32].reshape(4, D), (0, chip * rcw_cols), (4, rcw_cols))
    g_fcw = lax.dynamic_slice(early_sum[r_early + 32:r_early + 32 + 144].reshape(3, 2 * dff), (0, chip * ns), (3, ns))
    conv = [("rnn_conv_w", g_rcw), ("ffn_conv_w", g_fcw)]
    res = _adamw(_pack_rows([args[n] for n, _ in conv]), _pack_rows([args["m_" + n] for n, _ in conv]),
                 _pack_rows([args["v_" + n] for n, _ in conv]), [_pack_rows([g for _, g in conv])], "adamw_conv")
    off = 0
    for n, _ in conv:
        shape = args[n].shape
        cnt = shape[1] * shape[2] // LANES
        for kind, r in zip(("grad_", "delta_", "new_m_", "new_v_"), res):
            out[kind + n] = r[off:off + cnt].reshape(shape)
        off += _rows_of(shape)

    dmod_all = late_all[:, 0:6 * D // LANES, :].reshape(N_DEV, 6 * D)
    dmod_sh = lax.dynamic_slice(dmod_all, (0, chip * ns), (N_DEV, ns))
    res = _ada_adamw(jnp.transpose(c_all), dmod_sh, w_ada[0], m_w_ada[0], v_w_ada[0])
    for kind, r in zip(("grad_", "delta_", "new_m_", "new_v_"), res):
        out[kind + "w_ada"] = r.reshape(w_ada.shape)

    names = ["w_ada", "b_ada", "norm_mix_g", "w_in", "rnn_conv_w", "rnn_conv_b", "lru_w_a", "lru_b_a", "lru_w_x",
             "lru_b_x", "lru_lambda", "sgu_ln_g", "sgu_ln_b", "sgu_w_s", "sgu_b_s", "w_branch_a", "w_branch_b",
             "w_out", "norm_ffn_g", "w_up", "ffn_conv_w", "ffn_conv_b", "w_down", "norm_final_g"]
    result = [loss, grad_x.reshape(x.shape)]
    for kind in ("grad_", "delta_", "new_m_", "new_v_"):
        result += [out[kind + n] for n in names]
    return tuple(result)
```

```python
import jax
import jax.numpy as jnp
from jax import lax
from jax.experimental import pallas as pl
from jax.experimental.pallas import tpu as pltpu

F32 = jnp.float32
BF = jnp.bfloat16

D = 1024
HEADS = 8
HD = D // HEADS
SGU_BLOCK = 128
N_CHIPS = 4
N_DEV = 8
EPS = 1e-6
LRU_C = 8.0
LANES = 128
SUBLANES = 8
ELEMENTWISE_BLOCK_BYTES = 3 << 19
TN_ROWS = 2048
TN_ROWS_SQUARE = 4096

ADAM_LR = 0.001
ADAM_B1 = 0.9
ADAM_B2 = 0.999
ADAM_EPS = 1e-08
ADAM_WD = 0.01
ADAM_STEP = 10

GELU_K0 = 0.7978845608028654
GELU_K1 = 0.044715

HBM_SPEC = pl.BlockSpec(memory_space=pltpu.HBM)
MESH_ID = pl.DeviceIdType.MESH


def _pcall(body, *, name, out_shape, grid=(), in_specs=None, out_specs=None, scratch=(), vmem_mb=32, aliases=None,
           grid_spec=None):
    kw = {}
    if aliases:
        kw["input_output_aliases"] = aliases
    if grid_spec is not None:
        kw["grid_spec"] = grid_spec
        ndim = len(grid_spec.grid)
    else:
        kw.update(grid=grid, in_specs=in_specs, out_specs=out_specs, scratch_shapes=list(scratch))
        ndim = len(grid)
    if ndim:
        params = pltpu.CompilerParams(dimension_semantics=("arbitrary",) * ndim, vmem_limit_bytes=vmem_mb * 2 ** 20)
    else:
        params = pltpu.CompilerParams(vmem_limit_bytes=vmem_mb * 2 ** 20)
    return pl.pallas_call(body, name=name, out_shape=out_shape, compiler_params=params, **kw)


def _gelu_cdf(x, x2):
    return 0.5 * jnp.tanh(x * (GELU_K0 + (GELU_K0 * GELU_K1) * x2)) + 0.5


def _gelu(x):
    return x * _gelu_cdf(x, x * x)


def _gelu_and_grad(x):
    x2 = x * x
    s = _gelu_cdf(x, x2)
    g = x * s
    dg = s * (1.0 + (x - g) * ((2.0 * GELU_K0) + (6.0 * GELU_K0 * GELU_K1) * x2))
    return g, dg


def _sigmoid(x):
    return 1.0 / (1.0 + jnp.exp(-x))


def _sigmoid_t(x):
    return 0.5 * jnp.tanh(0.5 * x) + 0.5


def _log_sigmoid(x):
    e = jnp.exp(-jnp.abs(x))
    u = 1.0 + e
    d = u - 1.0
    l1p = jnp.where(d == 0.0, e, jnp.log(u) * (e / jnp.where(d == 0.0, 1.0, d)))
    return jnp.minimum(x, 0.0) - l1p


def _dot(a, b):
    return jnp.dot(a, b, preferred_element_type=F32)


def _dot_nt(a, b):
    return lax.dot_general(a, b, (((1,), (1,)), ((), ())), preferred_element_type=F32)


def _dot_tn(a, b):
    return lax.dot_general(a, b, (((0,), (0,)), ((), ())), preferred_element_type=F32)


def _shift_down(x, halo, s):
    r = pltpu.roll(x, s, 0)
    rows = lax.broadcasted_iota(jnp.int32, (SUBLANES, x.shape[1]), 0)
    head = jnp.where(rows < s, pltpu.roll(halo, s, 0), r[0:SUBLANES])
    return jnp.concatenate([head, r[SUBLANES:]], axis=0)


def _shift_up(x, halo, s):
    n = x.shape[0]
    r = pltpu.roll(x, n - s, 0)
    rows = lax.broadcasted_iota(jnp.int32, (SUBLANES, x.shape[1]), 0)
    tail = jnp.where(rows >= SUBLANES - s, pltpu.roll(halo, SUBLANES - s, 0), r[n - SUBLANES:n])
    return jnp.concatenate([r[:n - SUBLANES], tail], axis=0)


def _scan_rows(a, u, reverse):
    n, width = a.shape
    rows = lax.broadcasted_iota(jnp.int32, (n, width), 0)
    d = 1
    while d < n:
        if d < SUBLANES:
            keep = rows < n - d if reverse else rows >= d
            shift = n - d if reverse else d
            a_s = jnp.where(keep, pltpu.roll(a, shift, 0), 1.0)
            u_s = jnp.where(keep, pltpu.roll(u, shift, 0), 0.0)
        elif reverse:
            a_s = jnp.concatenate([a[d:], jnp.ones((d, width), a.dtype)], axis=0)
            u_s = jnp.concatenate([u[d:], jnp.zeros((d, width), u.dtype)], axis=0)
        else:
            a_s = jnp.concatenate([jnp.ones((d, width), a.dtype), a[:n - d]], axis=0)
            u_s = jnp.concatenate([jnp.zeros((d, width), u.dtype), u[:n - d]], axis=0)
        u = a * u_s + u
        a = a * a_s
        d *= 2
    return a, u


def _colsum(x):
    return jnp.sum(x, axis=0, keepdims=True)


def _rms_stats(x):
    r = lax.rsqrt(jnp.mean(x * x, axis=-1, keepdims=True) + EPS)
    return r, x * r


def _lru_gates(xc, wa_ref, ba, wx_ref, bx, lam, head0=0):
    pr, pi = [], []
    for hh in range(xc.shape[1] // HD):
        xh = xc[:, hh * HD:(hh + 1) * HD].astype(BF)
        pr.append(_dot(xh, wa_ref[head0 + hh].astype(BF)))
        pi.append(_dot(xh, wx_ref[head0 + hh].astype(BF)))
    r = _sigmoid_t((pr[0] if len(pr) == 1 else jnp.concatenate(pr, axis=1)) + ba)
    ig = _sigmoid_t((pi[0] if len(pi) == 1 else jnp.concatenate(pi, axis=1)) + bx)
    ls = _log_sigmoid(lam)
    log_a = LRU_C * r * ls
    a = jnp.exp(log_a)
    x2 = 2.0 * log_a
    u = a * a
    lu = jnp.log(jnp.maximum(u, 1e-37))
    ratio = x2 * pl.reciprocal(jnp.where(lu == 0.0, 1.0, lu), approx=True)
    em1 = jnp.where(lu == 0.0, x2, jnp.where(u < 1e-30, -1.0, (u - 1.0) * ratio))
    mult = jnp.sqrt(-em1)
    return r, ig, ls, a, mult


def _sgu_mix(vln, ws_ref, bst_ref, tb):
    ri = lax.broadcasted_iota(jnp.int32, (SGU_BLOCK, SGU_BLOCK), 0)
    ci = lax.broadcasted_iota(jnp.int32, (SGU_BLOCK, SGU_BLOCK), 1)
    wm = [jnp.where(ri >= ci, ws_ref[g], 0.0).astype(BF) for g in range(HEADS)]
    blocks = []
    for blk in range(tb // SGU_BLOCK):
        cols = []
        for g in range(HEADS):
            vb = vln[blk * SGU_BLOCK:(blk + 1) * SGU_BLOCK, g * HD:(g + 1) * HD].astype(BF)
            cols.append(_dot(wm[g], vb) + bst_ref[:, g:g + 1])
        blocks.append(jnp.concatenate(cols, axis=1))
    mixed = blocks[0] if len(blocks) == 1 else jnp.concatenate(blocks, axis=0)
    return wm, mixed


def _layernorm_stats(v):
    mu = jnp.mean(v, axis=-1, keepdims=True)
    vc = v - mu
    rstd = lax.rsqrt(jnp.mean(vc * vc, axis=-1, keepdims=True) + EPS)
    return rstd, vc * rstd


def _my_xyc():
    return lax.axis_index("x"), lax.axis_index("y"), lax.axis_index("c")


def _gather_weights(srcs, halve):
    n = len(srcs)
    out_shape = [jax.ShapeDtypeStruct((N_CHIPS,) + s.shape, s.dtype) for s in srcs]

    def body(*refs):
        src, out = refs[:n], refs[n:2 * n]
        send_sems, recv_sems, fwd_send, fwd_recv, loc_sems = refs[2 * n:]
        x, y, c = _my_xyc()
        me = 2 * x + y
        chips = [(1 - x, y), (x, 1 - y), (1 - x, 1 - y)]

        def half(ref, a, which):
            if not halve[a]:
                return ref
            h = srcs[a].shape[0] // 2
            return ref.at[pl.ds(which * h, h)]

        def ici(a, k, frm):
            px, py = chips[k]
            return pltpu.make_async_remote_copy(
                src_ref=half(src[a], a, c), dst_ref=half(out[a].at[frm], a, c),
                send_sem=send_sems.at[a, k], recv_sem=recv_sems.at[a, k],
                device_id=(px, py, c), device_id_type=MESH_ID)

        def d2d(a, k, which):
            px, py = chips[k]
            rows = half(out[a].at[2 * px + py], a, which)
            return pltpu.make_async_remote_copy(
                src_ref=rows, dst_ref=rows, send_sem=fwd_send.at[a, k], recv_sem=fwd_recv.at[a, k],
                device_id=(x, y, 1 - c), device_id_type=MESH_ID)

        local, sends = [], []
        for a in range(n):
            lc = pltpu.make_async_copy(src[a], out[a].at[me], loc_sems.at[a])
            lc.start()
            local.append(lc)
            for k in range(3):
                cp = ici(a, k, me)
                cp.start()
                sends.append(cp)
        for a in range(n):
            for k in range(3):
                px, py = chips[k]
                ici(a, k, 2 * px + py).wait_recv()
                if halve[a]:
                    fw = d2d(a, k, c)
                    fw.start()
                    sends.append(fw)
        for a in range(n):
            if halve[a]:
                for k in range(3):
                    d2d(a, k, 1 - c).wait_recv()
        for cp in sends:
            cp.wait_send()
        for lc in local:
            lc.wait()

    sem = pltpu.SemaphoreType.DMA((n, 3))
    return _pcall(body, name="gather_weights", out_shape=out_shape, in_specs=[HBM_SPEC] * n,
                  out_specs=[HBM_SPEC] * n, scratch=[sem, sem, sem, sem, pltpu.SemaphoreType.DMA((n,))])(*srcs)


SEM_SPEC = pl.BlockSpec(memory_space=pltpu.SEMAPHORE)


def _remote_start(srcs, lands, plan, ncopies, name):
    n, m = len(srcs), len(lands)

    def body(*refs):
        src, land = refs[:n], refs[n:n + m]
        send_sems, recv_sems = refs[n + m], refs[n + m + 1]
        token = refs[-1]
        x, y, c = _my_xyc()
        for i, (s, d, dev) in enumerate(plan(src, land, x, y, c)):
            pltpu.make_async_remote_copy(src_ref=s, dst_ref=d, send_sem=send_sems.at[i], recv_sem=recv_sems.at[i],
                                         device_id=dev, device_id_type=MESH_ID).start()
        token[...] = jnp.zeros_like(token)

    bufs = list(srcs) + list(lands)
    out = pl.pallas_call(
        body, name=name,
        out_shape=(pltpu.SemaphoreType.DMA((ncopies,)), pltpu.SemaphoreType.DMA((ncopies,)),
                   *[pltpu.HBM(b.shape, b.dtype) for b in bufs], jax.ShapeDtypeStruct((SUBLANES, LANES), F32)),
        in_specs=[HBM_SPEC] * (n + m),
        out_specs=(SEM_SPEC, SEM_SPEC, *[HBM_SPEC] * (n + m), pl.BlockSpec(memory_space=pltpu.VMEM)),
        input_output_aliases={i: 2 + i for i in range(n + m)},
        compiler_params=pltpu.CompilerParams(has_side_effects=pltpu.SideEffectType.DATAFLOW_SIDE_EFFECTING),
    )(*[pltpu.with_memory_space_constraint(b, pltpu.HBM) for b in bufs])
    return (out[0], out[1], out[2:2 + n], out[2 + n:2 + n + m]), out[-1]


def _remote_wait(handle, plan, after, name):
    send_sems, recv_sems, srcs, lands = handle
    n, m = len(srcs), len(lands)

    def body(*refs):
        src, land = refs[:n], refs[n:n + m]
        ssem, rsem = refs[n + m], refs[n + m + 1]
        x, y, c = _my_xyc()
        for i, (s, d, dev) in enumerate(plan(src, land, x, y, c)):
            cp = pltpu.make_async_remote_copy(src_ref=s, dst_ref=d, send_sem=ssem.at[i], recv_sem=rsem.at[i],
                                              device_id=dev, device_id_type=MESH_ID)
            cp.wait_send()
            cp.wait_recv()

    bufs = list(srcs) + list(lands)
    out = pl.pallas_call(
        body, name=name, out_shape=tuple(pltpu.HBM(b.shape, b.dtype) for b in bufs),
        in_specs=[HBM_SPEC] * (n + m) + [SEM_SPEC, SEM_SPEC, pl.BlockSpec(memory_space=pl.ANY)],
        out_specs=tuple([HBM_SPEC] * (n + m)), input_output_aliases={i: i for i in range(n + m)},
        compiler_params=pltpu.CompilerParams(has_side_effects=pltpu.SideEffectType.DATAFLOW_SIDE_EFFECTING),
    )(*bufs, send_sems, recv_sems, after)
    return out[:n], out[n:]


def _chips_of(x, y):
    return [(1 - x, y), (x, 1 - y), (1 - x, 1 - y)]


def _gather_plan(count):
    def plan(src, land, x, y, c):
        me = 2 * x + y
        return [(src[a], land[a].at[me], (px, py, c)) for a in range(count) for px, py in _chips_of(x, y)]

    return plan


def _place_own(srcs, lands):
    n = len(srcs)

    def body(*refs):
        src, land, sems = refs[:n], refs[2 * n:3 * n], refs[3 * n]
        me = 2 * lax.axis_index("x") + lax.axis_index("y")
        copies = [pltpu.make_async_copy(src[a], land[a].at[me], sems.at[a]) for a in range(n)]
        for cp in copies:
            cp.start()
        for cp in copies:
            cp.wait()

    return _pcall(body, name="place_own_shards", out_shape=[jax.ShapeDtypeStruct(l.shape, l.dtype) for l in lands],
                  in_specs=[HBM_SPEC] * (2 * n), out_specs=[HBM_SPEC] * n, aliases={n + a: a for a in range(n)},
                  scratch=[pltpu.SemaphoreType.DMA((n,))])(*srcs, *lands)


def _gather8(src, name):
    def body(src_ref, out_ref, send_sems, recv_sems, loc_sem):
        x, y, c = _my_xyc()
        me = 4 * x + 2 * y + c
        lc = pltpu.make_async_copy(src_ref, out_ref.at[me], loc_sem)
        lc.start()
        cps = []
        for k in range(1, N_DEV):
            px = 1 - x if (k >> 2) & 1 else x
            py = 1 - y if (k >> 1) & 1 else y
            pc = 1 - c if k & 1 else c
            cp = pltpu.make_async_remote_copy(
                src_ref=src_ref, dst_ref=out_ref.at[me], send_sem=send_sems.at[k - 1], recv_sem=recv_sems.at[k - 1],
                device_id=(px, py, pc), device_id_type=MESH_ID)
            cp.start()
            cps.append(cp)
        for cp in cps:
            cp.wait()
        lc.wait()

    return _pcall(body, name=name, out_shape=jax.ShapeDtypeStruct((N_DEV,) + src.shape, src.dtype),
                  in_specs=[HBM_SPEC], out_specs=HBM_SPEC,
                  scratch=[pltpu.SemaphoreType.DMA((N_DEV - 1,)), pltpu.SemaphoreType.DMA((N_DEV - 1,)),
                           pltpu.SemaphoreType.DMA])(src)


def _cast_shards(arrs, name, after=None):
    n = len(arrs)
    extra = [] if after is None else [after]

    def body(*refs):
        ins, outs = refs[:n], refs[n + len(extra):]
        for a in range(n):
            outs[a][...] = ins[a][...].astype(BF)

    specs = [pl.BlockSpec((s.shape[0] // 4, s.shape[1]), lambda i: (i, 0)) for s in arrs]
    return _pcall(body, name=name, grid=(4,), in_specs=specs + [pl.BlockSpec(memory_space=pl.ANY)] * len(extra),
                  out_specs=specs, out_shape=[jax.ShapeDtypeStruct(s.shape, BF) for s in arrs])(*arrs, *extra)


def _row_tile(rows, cols):
    t = rows
    while t * cols * 4 > ELEMENTWISE_BLOCK_BYTES and t % (2 * SUBLANES) == 0:
        t //= 2
    return t


def _sum_parts(parts, name):
    p, rows, cols = parts.shape
    tr = _row_tile(rows, cols * p // 2)

    def body(p_ref, o_ref):
        acc = p_ref[0].astype(F32)
        for k in range(1, p):
            acc = acc + p_ref[k].astype(F32)
        o_ref[...] = acc

    return _pcall(body, name=name, grid=(rows // tr,),
                  in_specs=[pl.BlockSpec((p, tr, cols), lambda i: (0, i, 0))],
                  out_specs=pl.BlockSpec((tr, cols), lambda i: (i, 0)),
                  out_shape=jax.ShapeDtypeStruct((rows, cols), F32), vmem_mb=48)(parts)


def _sum_own_and_landed(chip, sums, landed, name):
    _, rows, cols = sums.shape
    tr = _row_tile(rows, 2 * cols)

    def body(chip_ref, own_ref, land_ref, o_ref):
        del chip_ref
        acc = own_ref[0].astype(F32)
        for k in range(3):
            acc = acc + land_ref[k].astype(F32)
        o_ref[...] = acc

    grid_spec = pltpu.PrefetchScalarGridSpec(
        num_scalar_prefetch=1, grid=(rows // tr,),
        in_specs=[pl.BlockSpec((1, tr, cols), lambda i, chip_ref: (chip_ref[0], i, 0)),
                  pl.BlockSpec((3, tr, cols), lambda i, chip_ref: (0, i, 0))],
        out_specs=pl.BlockSpec((tr, cols), lambda i, chip_ref: (i, 0)))
    return _pcall(body, name=name, grid_spec=grid_spec, out_shape=jax.ShapeDtypeStruct((rows, cols), F32),
                  vmem_mb=48)(chip, sums, landed)


def _swap_cores(arrs, name):
    n = len(arrs)

    def body(*refs):
        src, out = refs[:n], refs[n:2 * n]
        send_sems, recv_sems = refs[2 * n:]
        x, y, c = _my_xyc()
        cps = []
        for a in range(n):
            cp = pltpu.make_async_remote_copy(
                src_ref=src[a], dst_ref=out[a], send_sem=send_sems.at[a], recv_sem=recv_sems.at[a],
                device_id=(x, y, 1 - c), device_id_type=MESH_ID)
            cp.start()
            cps.append(cp)
        for cp in cps:
            cp.wait()

    sem = pltpu.SemaphoreType.DMA((n,))
    return _pcall(body, name=name, out_shape=[jax.ShapeDtypeStruct(a.shape, a.dtype) for a in arrs],
                  in_specs=[HBM_SPEC] * n, out_specs=[HBM_SPEC] * n, scratch=[sem, sem])(*arrs)


def _add_pair(a, b, name):
    rows, cols = a.shape
    tr = _row_tile(rows, 2 * cols)

    def body(a_ref, b_ref, o_ref):
        o_ref[...] = a_ref[...] + b_ref[...]

    spec = pl.BlockSpec((tr, cols), lambda i: (i, 0))
    return _pcall(body, name=name, grid=(rows // tr,), in_specs=[spec, spec], out_specs=spec,
                  out_shape=jax.ShapeDtypeStruct((rows, cols), F32))(a, b)


def _sum_chips_in_order(chip, own, landed, name):
    rows, cols = own.shape
    tr = _row_tile(rows, 4 * cols)

    def body(chip_ref, own_ref, land_ref, o_ref):
        me = chip_ref[0]
        acc = None
        for p in range(N_CHIPS):
            q = p ^ me
            k = jnp.where(q == 2, 0, jnp.where(q == 1, 1, 2))
            term = jnp.where(q == 0, own_ref[...], land_ref[k])
            acc = term if acc is None else acc + term
        o_ref[...] = acc

    grid_spec = pltpu.PrefetchScalarGridSpec(
        num_scalar_prefetch=1, grid=(rows // tr,),
        in_specs=[pl.BlockSpec((tr, cols), lambda i, chip_ref: (i, 0)),
                  pl.BlockSpec((3, tr, cols), lambda i, chip_ref: (0, i, 0))],
        out_specs=pl.BlockSpec((tr, cols), lambda i, chip_ref: (i, 0)))
    return _pcall(body, name=name, grid_spec=grid_spec, out_shape=jax.ShapeDtypeStruct((rows, cols), F32))(
        chip, own, landed)


def _bcast_plan(src, land, x, y, c):
    return [(src[0], land[0].at[k], (px, py, c)) for k, (px, py) in enumerate(_chips_of(x, y))]


def _sibling_plan(count):
    def plan(src, land, x, y, c):
        return [(src[a], land[a], (x, y, 1 - c)) for a in range(count)]

    return plan


def _scatter_plan(count):
    def plan(src, land, x, y, c):
        out = []
        for a in range(count):
            for k, (px, py) in enumerate(_chips_of(x, y)):
                out.append((src[a].at[2 * px + py], land[a].at[k], (px, py, c)))
        return out

    return plan


def _adamw_math(w, g, m, v):
    m2 = ADAM_B1 * m + (1.0 - ADAM_B1) * g
    v2 = ADAM_B2 * v + (1.0 - ADAM_B2) * (g * g)
    m_hat = m2 / (1.0 - ADAM_B1 ** ADAM_STEP)
    v_hat = v2 / (1.0 - ADAM_B2 ** ADAM_STEP)
    delta = -ADAM_LR * (m_hat / (jnp.sqrt(v_hat) + ADAM_EPS) + ADAM_WD * w)
    return delta, m2, v2


def _adamw(w, m, v, grads, name):
    rows, cols = w.shape
    tr = _row_tile(rows, cols)
    ng = len(grads)

    def body(*refs):
        w_ref, m_ref, v_ref = refs[:3]
        g = refs[3][...]
        for k in range(1, ng):
            g = g + refs[3 + k][...]
        g_ref, d_ref, m2_ref, v2_ref = refs[3 + ng:]
        delta, m2, v2 = _adamw_math(w_ref[...], g, m_ref[...], v_ref[...])
        g_ref[...] = g
        d_ref[...] = delta
        m2_ref[...] = m2
        v2_ref[...] = v2

    spec = pl.BlockSpec((tr, cols), lambda i: (i, 0))
    return _pcall(body, name=name, grid=(rows // tr,), in_specs=[spec] * (3 + ng), out_specs=[spec] * 4,
                  out_shape=[jax.ShapeDtypeStruct((rows, cols), F32)] * 4, vmem_mb=48)(w, m, v, *grads)


def _ada_adamw(ct, dmod, w, m, v):
    rows, cols = w.shape
    tr = _row_tile(rows, cols)

    def body(ct_ref, dm_ref, w_ref, m_ref, v_ref, g_ref, d_ref, m2_ref, v2_ref):
        cv = ct_ref[...]
        ca = cv * _sigmoid(cv)
        g = ca[:, 0:1] * dm_ref[0:1, :]
        for b in range(1, N_DEV):
            g = g + ca[:, b:b + 1] * dm_ref[b:b + 1, :]
        delta, m2, v2 = _adamw_math(w_ref[...], g, m_ref[...], v_ref[...])
        g_ref[...] = g
        d_ref[...] = delta
        m2_ref[...] = m2
        v2_ref[...] = v2

    spec = pl.BlockSpec((tr, cols), lambda i: (i, 0))
    return _pcall(body, name="ada_adamw", grid=(rows // tr,),
                  in_specs=[pl.BlockSpec((tr, N_DEV), lambda i: (i, 0)), pl.BlockSpec((N_DEV, cols), lambda i: (0, 0)),
                            spec, spec, spec],
                  out_specs=[spec] * 4, out_shape=[jax.ShapeDtypeStruct((rows, cols), F32)] * 4,
                  vmem_mb=48)(ct, dmod, w, m, v)


def _mod_fwd(c_all, w, b):
    cols = w.shape[1]
    tn = cols // 3

    def body(c_ref, w_ref, b_ref, o_ref):
        cv = c_ref[...]
        ca = (cv * _sigmoid(cv)).astype(BF)
        o_ref[...] = _dot(ca, w_ref[...].astype(BF)) + b_ref[...]

    return _pcall(body, name="mod_fwd", grid=(3,),
                  in_specs=[pl.BlockSpec((N_DEV, D), lambda j: (0, 0)), pl.BlockSpec((D, tn), lambda j: (0, j)),
                            pl.BlockSpec((1, tn), lambda j: (0, j))],
                  out_specs=pl.BlockSpec((N_DEV, tn), lambda j: (0, j)),
                  out_shape=jax.ShapeDtypeStruct((N_DEV, cols), F32))(c_all, w, b)


def _resident(shape):
    zeros = (0,) * len(shape)
    return pl.BlockSpec(shape, lambda *_: zeros, pipeline_mode=pl.Buffered(1))


def _mixer_fwd(x, g, scale, shift, gate1, w_in4, cw, cb, wa, ba, wx, bx, lam, lg, lb, ws, bst, wba, wbb, wo,
               tm=256, chunk=256, piece=512):
    T = x.shape[0]
    tm = min(tm, T)
    ns = w_in4.shape[2]
    per = ns // piece

    def body(x_ref, g_ref, sc_ref, sh_ref, g1_ref, w_ref, cw_ref, cb_ref, wa_ref, ba_ref, wx_ref, bx_ref, lam_ref,
             lg_ref, lb_ref, ws_ref, bst_ref, wba_ref, wbb_ref, wo_ref,
             h1_ref, z_ref, hl_ref, yap_ref, ybp_ref, mg_ref, ya_ref, yb_ref, o_ref, x2_ref,
             xc_ref, r_ref, ig_ref, mu_ref, a_ref, prev, hc):
        i = pl.program_id(0)

        @pl.when(i == 0)
        def _():
            prev[...] = jnp.zeros_like(prev)
            hc[...] = jnp.zeros_like(hc)

        xv = x_ref[...]
        _, xh = _rms_stats(xv)
        h = ((xh * g_ref[...]) * (1.0 + sc_ref[...]) + sh_ref[...]).astype(BF)
        h1_ref[...] = h

        def proj(col, width):
            for c0 in range(col, col + width, piece):
                w = min(piece, col + width - c0)
                j, off = c0 // ns, c0 % ns
                z_ref[:, c0:c0 + w] = _dot(h, w_ref[j, :, off:off + w])

        def lru_chunk(c0):
            cs = slice(c0, c0 + chunk)
            xr = z_ref[:, cs]
            pv = prev[:, cs]
            xc = (cb_ref[:, cs] + cw_ref[3:4, cs] * xr + cw_ref[2:3, cs] * _shift_down(xr, pv, 1)
                  + cw_ref[1:2, cs] * _shift_down(xr, pv, 2) + cw_ref[0:1, cs] * _shift_down(xr, pv, 3))
            prev[:, cs] = xr[tm - SUBLANES:tm]
            r, ig, _, a, mult = _lru_gates(xc, wa_ref, ba_ref[:, cs], wx_ref, bx_ref[:, cs], lam_ref[:, cs],
                                           head0=c0 // HD)
            xc_ref[:, cs] = xc.astype(BF)
            r_ref[:, cs] = r.astype(BF)
            ig_ref[:, cs] = ig.astype(BF)
            a_ref[:, cs] = a
            mu_ref[:, cs] = mult.astype(BF)
            a, u = _scan_rows(a, mult * (ig * xc), reverse=False)
            hv = u + a * hc[SUBLANES - 1:SUBLANES, cs]
            hc[:, cs] = hv[tm - SUBLANES:tm]
            hl_ref[:, cs] = hv
            yap_ref[:, cs] = (hv * _gelu(z_ref[:, D + c0:D + c0 + chunk])).astype(BF)

        proj(0, chunk)
        proj(D, chunk)
        for c0 in range(0, D, chunk):
            if c0 + chunk < D:
                proj(c0 + chunk, chunk)
                proj(D + c0 + chunk, chunk)
            else:
                proj(2 * D, 2 * D)
            lru_chunk(c0)
        proj(4 * D, 2 * D)
        _, xhn = _layernorm_stats(_gelu(z_ref[:, 3 * D:4 * D]))
        vln = xhn * lg_ref[...] + lb_ref[...]
        _, mixed = _sgu_mix(vln, ws_ref, bst_ref, tm)
        ybp = (_gelu(z_ref[:, 2 * D:3 * D]) * mixed).astype(BF)
        ybp_ref[...] = ybp
        ya = _dot(yap_ref[...], wba_ref[...])
        yb = _dot(ybp, wbb_ref[...])
        merged = (_sigmoid_t(z_ref[:, 4 * D:5 * D]) * ya + _sigmoid_t(z_ref[:, 5 * D:6 * D]) * yb).astype(BF)
        o = _dot(merged, wo_ref[...])
        x2_ref[...] = xv + g1_ref[...] * o
        mg_ref[...] = merged
        ya_ref[...] = ya.astype(BF)
        yb_ref[...] = yb.astype(BF)
        o_ref[...] = o.astype(BF)

    row = pl.BlockSpec((tm, D), lambda i: (i, 0))
    vec = pl.BlockSpec((1, D), lambda i: (0, 0))
    bf_row = jax.ShapeDtypeStruct((T, D), BF)
    f32_row = jax.ShapeDtypeStruct((T, D), F32)
    return _pcall(body, name="mixer_fwd", grid=(T // tm,),
                  in_specs=[row, vec, vec, vec, vec, _resident(w_in4.shape), _resident(cw.shape), vec,
                            _resident(wa.shape), vec, _resident(wx.shape), vec, vec, vec, vec,
                            _resident(ws.shape), _resident(bst.shape),
                            _resident(wba.shape), _resident(wbb.shape), _resident(wo.shape)],
                  out_specs=[row, pl.BlockSpec((tm, 6 * D), lambda i: (i, 0))] + [row] * 13,
                  out_shape=[bf_row, jax.ShapeDtypeStruct((T, 6 * D), F32), f32_row, bf_row, bf_row, bf_row, bf_row,
                             bf_row, bf_row, f32_row, bf_row, bf_row, bf_row, bf_row, f32_row],
                  scratch=[pltpu.VMEM((SUBLANES, D), F32), pltpu.VMEM((SUBLANES, D), F32)], vmem_mb=60)(
        x, g, scale, shift, gate1, w_in4, cw, cb, wa, ba, wx, bx, lam, lg, lb, ws, bst, wba, wbb, wo)


def _ffn_fwd(x2, g, scale, shift, gate2, gf, w_up4, wd, cw, cb, target, tm=256, chunk=768):
    T = x2.shape[0]
    tm = min(tm, T)
    ns = w_up4.shape[2]
    dff = wd.shape[0]
    nchunk = dff // chunk
    per = ns // chunk

    def body(x2_ref, g_ref, sc_ref, sh_ref, g2_ref, gf_ref, wu_ref, wd_ref, cw_ref, cb_ref, t_ref,
             h2_ref, up_ref, f_ref, ga_ref, vd_ref, loss_ref, dx3_ref, dfo_ref, dgf_ref, dg2_ref, prev):
        i = pl.program_id(0)

        @pl.when(i == 0)
        def _():
            prev[...] = jnp.zeros_like(prev)
            loss_ref[...] = jnp.zeros_like(loss_ref)
            dgf_ref[...] = jnp.zeros_like(dgf_ref)
            dg2_ref[...] = jnp.zeros_like(dg2_ref)

        x2v = x2_ref[...]
        _, xh2 = _rms_stats(x2v)
        h2 = ((xh2 * g_ref[...]) * (1.0 + sc_ref[...]) + sh_ref[...]).astype(BF)
        h2_ref[...] = h2

        def conv(u, col):
            cs = slice(col, col + chunk)
            p = prev[:, cs]
            hid = (cb_ref[:, cs] + cw_ref[2:3, cs] * u + cw_ref[1:2, cs] * _shift_down(u, p, 1)
                   + cw_ref[0:1, cs] * _shift_down(u, p, 2))
            prev[:, cs] = u[tm - SUBLANES:tm]
            up_ref[:, cs] = u.astype(BF)
            return hid

        def up_proj(k):
            off = (k % per) * chunk
            return (_dot(h2, wu_ref[k // per, :, off:off + chunk]),
                    _dot(h2, wu_ref[N_CHIPS // 2 + k // per, :, off:off + chunk]))

        fo = None
        nxt = up_proj(0)
        for k in range(nchunk):
            col = k * chunk
            ua, uv = nxt
            if k + 1 < nchunk:
                nxt = up_proj(k + 1)
            act = conv(ua, col)
            val = conv(uv, dff + col)
            ga, dga = _gelu_and_grad(act)
            fk = (ga * val).astype(BF)
            f_ref[:, col:col + chunk] = fk
            ga_ref[:, col:col + chunk] = ga.astype(BF)
            vd_ref[:, col:col + chunk] = (val * dga).astype(BF)
            part = _dot(fk, wd_ref[col:col + chunk, :])
            fo = part if fo is None else fo + part

        x3 = x2v + g2_ref[...] * fo
        rstd, xh = _rms_stats(x3)
        err = xh * gf_ref[...] - t_ref[...]
        loss_ref[...] += 0.5 * jnp.sum(jnp.mean(err * err, axis=-1, keepdims=True), axis=0, keepdims=True)
        dy = err * (1.0 / D)
        dgf_ref[...] += _colsum(dy * xh)
        dxh = dy * gf_ref[...]
        dx3 = rstd * (dxh - xh * jnp.mean(dxh * xh, axis=-1, keepdims=True))
        dg2_ref[...] += _colsum(dx3 * fo)
        dx3_ref[...] = dx3
        dfo_ref[...] = (g2_ref[...] * dx3).astype(BF)

    row = pl.BlockSpec((tm, D), lambda i: (i, 0))
    vec = pl.BlockSpec((1, D), lambda i: (0, 0))
    wide = pl.BlockSpec((tm, 2 * dff), lambda i: (i, 0))
    half = pl.BlockSpec((tm, dff), lambda i: (i, 0))
    return _pcall(body, name="ffn_fwd", grid=(T // tm,),
                  in_specs=[row, vec, vec, vec, vec, vec, _resident(w_up4.shape), _resident(wd.shape),
                            _resident(cw.shape), _resident(cb.shape), row],
                  out_specs=[row, wide, half, half, half, pl.BlockSpec((1, LANES), lambda i: (0, 0)), row, row, vec, vec],
                  out_shape=[jax.ShapeDtypeStruct((T, D), BF), jax.ShapeDtypeStruct((T, 2 * dff), BF),
                             jax.ShapeDtypeStruct((T, dff), BF), jax.ShapeDtypeStruct((T, dff), BF),
                             jax.ShapeDtypeStruct((T, dff), BF), jax.ShapeDtypeStruct((1, LANES), F32),
                             jax.ShapeDtypeStruct((T, D), F32), jax.ShapeDtypeStruct((T, D), BF),
                             jax.ShapeDtypeStruct((1, D), F32), jax.ShapeDtypeStruct((1, D), F32)],
                  scratch=[pltpu.VMEM((SUBLANES, 2 * dff), F32)], vmem_mb=56)(
        x2, g, scale, shift, gate2, gf, w_up4, wd, cw, cb, target)


def _ffn_bwd(dfo, wd, up, ga, vd, cw, w_up4, x2, resid, g, scale, gate, o, tm=256, chunk=1536):
    T = up.shape[0]
    tm = min(tm, T)
    dff = wd.shape[0]
    ns = w_up4.shape[2]
    nchunk = dff // chunk
    per = ns // chunk
    nrow = T // tm

    def body(dfo_ref, wd_ref, up_ref, ga_ref, vd_ref, cw_ref, wu_ref, x_ref, r_ref, g_ref, sc_ref, gt_ref, o_ref,
             du_ref, dcw_ref, dcb_ref, dx_ref, dsh_ref, dsc_ref, dg_ref, do_ref, dgt_ref, nxt):
        i = pl.program_id(0)

        @pl.when(i == 0)
        def _():
            nxt[...] = jnp.zeros_like(nxt)
            for ref in (dcw_ref, dcb_ref, dsh_ref, dsc_ref, dg_ref, dgt_ref):
                ref[...] = jnp.zeros_like(ref)

        dfo_t = dfo_ref[...]

        def conv_bwd(dh, col):
            cs = slice(col, col + chunk)
            n8 = nxt[:, cs]
            dh1 = _shift_up(dh, n8, 1)
            dh2 = _shift_up(dh, n8, 2)
            nxt[:, cs] = dh[0:SUBLANES]
            du = (cw_ref[2:3, cs] * dh + cw_ref[1:2, cs] * dh1 + cw_ref[0:1, cs] * dh2).astype(BF)
            du_ref[:, cs] = du
            u = up_ref[:, cs].astype(F32)
            dcw_ref[2:3, cs] += _colsum(dh * u)
            dcw_ref[1:2, cs] += _colsum(dh1 * u)
            dcw_ref[0:1, cs] += _colsum(dh2 * u)
            dcb_ref[:, cs] += _colsum(dh)
            return du

        def down_bwd(k):
            return _dot_nt(dfo_t, wd_ref[k * chunk:(k + 1) * chunk, :])

        dh = None
        df_next = down_bwd(0)
        for k in range(nchunk):
            col = k * chunk
            off = (k % per) * chunk
            df = df_next
            if k + 1 < nchunk:
                df_next = down_bwd(k + 1)
            du_a = conv_bwd(df * vd_ref[:, col:col + chunk].astype(F32), col)
            du_v = conv_bwd(df * ga_ref[:, col:col + chunk].astype(F32), dff + col)
            part = (_dot_nt(du_a, wu_ref[k // per, :, off:off + chunk])
                    + _dot_nt(du_v, wu_ref[N_CHIPS // 2 + k // per, :, off:off + chunk]))
            dh = part if dh is None else dh + part

        rstd, xh = _rms_stats(x_ref[...])
        dsh_ref[...] += _colsum(dh)
        dsc_ref[...] += _colsum(dh * (xh * g_ref[...]))
        dn = dh * (1.0 + sc_ref[...])
        dg_ref[...] += _colsum(dn * xh)
        dxh = dn * g_ref[...]
        dx = r_ref[...] + rstd * (dxh - xh * jnp.mean(dxh * xh, axis=-1, keepdims=True))
        dx_ref[...] = dx
        do_ref[...] = (gt_ref[...] * dx).astype(BF)
        dgt_ref[...] += _colsum(dx * o_ref[...].astype(F32))

    rev = lambda i: (nrow - 1 - i, 0)
    row = pl.BlockSpec((tm, D), rev)
    vec = pl.BlockSpec((1, D), lambda i: (0, 0))
    wide = pl.BlockSpec((tm, 2 * dff), rev)
    half = pl.BlockSpec((tm, dff), rev)
    cw3 = pl.BlockSpec((3, 2 * dff), lambda i: (0, 0))
    cb1 = pl.BlockSpec((1, 2 * dff), lambda i: (0, 0))
    vshape = jax.ShapeDtypeStruct((1, D), F32)
    return _pcall(body, name="ffn_bwd", grid=(nrow,),
                  in_specs=[row, _resident(wd.shape), wide, half, half, _resident(cw.shape), _resident(w_up4.shape),
                            row, row, vec, vec, vec, row],
                  out_specs=[wide, cw3, cb1, row, vec, vec, vec, row, vec],
                  out_shape=[jax.ShapeDtypeStruct((T, 2 * dff), BF), jax.ShapeDtypeStruct((3, 2 * dff), F32),
                             jax.ShapeDtypeStruct((1, 2 * dff), F32), jax.ShapeDtypeStruct((T, D), F32),
                             vshape, vshape, vshape, jax.ShapeDtypeStruct((T, D), BF), vshape],
                  scratch=[pltpu.VMEM((SUBLANES, 2 * dff), F32)], vmem_mb=60)(
        dfo, wd, up, ga, vd, cw, w_up4, x2, resid, g, scale, gate, o)


def _mm_tn_cols(a, b, name, nshard, nb, mb=None, tm=TN_ROWS):
    T, M = a.shape
    tm = min(tm, T)
    mb = M if mb is None else mb
    ns = b.shape[1] // nshard
    per = ns // nb
    nk = T // tm
    vmem_mb = (2 * 2 * tm * (mb + nb) + 2 * (4 + 2) * mb * nb) // 2 ** 20 + 8

    def body(a_ref, b_ref, o_ref, c_ref):
        k = pl.program_id(2)

        @pl.when(k == 0)
        def _():
            o_ref[...] = jnp.zeros_like(o_ref)

        o_ref[0] += _dot_tn(a_ref[...], b_ref[...])

        @pl.when(k == nk - 1)
        def _():
            c_ref[...] = o_ref[...].astype(BF)

    out_spec = pl.BlockSpec((1, mb, nb), lambda m, t, k: (t // per, m, t % per))
    return _pcall(body, name=name, grid=(M // mb, nshard * per, nk),
                  in_specs=[pl.BlockSpec((tm, mb), lambda m, t, k: (k, m)),
                            pl.BlockSpec((tm, nb), lambda m, t, k: (k, t))],
                  out_specs=[out_spec, out_spec],
                  out_shape=[jax.ShapeDtypeStruct((nshard, M, ns), F32), jax.ShapeDtypeStruct((nshard, M, ns), BF)],
                  vmem_mb=vmem_mb)(a, b)


def _mm_nt_normbwd(dz, w4, x, resid, g, scale, name, tm=512):
    T = x.shape[0]
    tm = min(tm, T)
    ns = w4.shape[2]

    def body(dz_ref, w_ref, x_ref, r_ref, g_ref, sc_ref, dx_ref, dsh_ref, dsc_ref, dg_ref):
        i = pl.program_id(0)

        @pl.when(i == 0)
        def _():
            dsh_ref[...] = jnp.zeros_like(dsh_ref)
            dsc_ref[...] = jnp.zeros_like(dsc_ref)
            dg_ref[...] = jnp.zeros_like(dg_ref)

        dh = None
        for j in range(N_CHIPS):
            part = _dot_nt(dz_ref[:, j * ns:(j + 1) * ns], w_ref[j])
            dh = part if dh is None else dh + part
        rstd, xh = _rms_stats(x_ref[...])
        dsh_ref[...] += _colsum(dh)
        dsc_ref[...] += _colsum(dh * (xh * g_ref[...]))
        dn = dh * (1.0 + sc_ref[...])
        dg_ref[...] += _colsum(dn * xh)
        dxh = dn * g_ref[...]
        dx_ref[...] = r_ref[...] + rstd * (dxh - xh * jnp.mean(dxh * xh, axis=-1, keepdims=True))

    row = pl.BlockSpec((tm, D), lambda i: (i, 0))
    vec = pl.BlockSpec((1, D), lambda i: (0, 0))
    return _pcall(body, name=name, grid=(T // tm,),
                  in_specs=[pl.BlockSpec((tm, N_CHIPS * ns), lambda i: (i, 0)), _resident(w4.shape), row, row, vec, vec],
                  out_specs=[row, vec, vec, vec],
                  out_shape=[jax.ShapeDtypeStruct((T, D), F32)] + [jax.ShapeDtypeStruct((1, D), F32)] * 3,
                  vmem_mb=48)(dz, w4, x, resid, g, scale)


def _mix_bwd(do, ya, yb, z, wo, wba, wbb, tm=512):
    T = do.shape[0]
    tm = min(tm, T)

    def body(do_ref, ya_ref, yb_ref, ga_ref, gb_ref, wo_ref, wa_ref, wb_ref,
             dz_ref, dya_ref, dyb_ref, dyap_ref, dybp_ref):
        dm = _dot_nt(do_ref[...], wo_ref[...])
        sa = _sigmoid_t(ga_ref[...])
        sb = _sigmoid_t(gb_ref[...])
        dya = (sa * dm).astype(BF)
        dyb = (sb * dm).astype(BF)
        dz_ref[:, 0:D] = (dm * ya_ref[...].astype(F32) * sa * (1.0 - sa)).astype(BF)
        dz_ref[:, D:2 * D] = (dm * yb_ref[...].astype(F32) * sb * (1.0 - sb)).astype(BF)
        dya_ref[...] = dya
        dyb_ref[...] = dyb
        dyap_ref[...] = _dot_nt(dya, wa_ref[...]).astype(BF)
        dybp_ref[...] = _dot_nt(dyb, wb_ref[...]).astype(BF)

    row = pl.BlockSpec((tm, D), lambda i: (i, 0))
    wspec = pl.BlockSpec((D, D), lambda i: (0, 0))
    return _pcall(body, name="mix_bwd", grid=(T // tm,),
                  in_specs=[row, row, row, pl.BlockSpec((tm, D), lambda i: (i, 4)),
                            pl.BlockSpec((tm, D), lambda i: (i, 5)), wspec, wspec, wspec],
                  out_specs=[pl.BlockSpec((tm, 2 * D), lambda i: (i, 2)), row, row, row, row],
                  out_shape=[jax.ShapeDtypeStruct((T, 6 * D), BF)] + [jax.ShapeDtypeStruct((T, D), BF)] * 4,
                  vmem_mb=48)(do, ya, yb, z, z, wo, wba, wbb)


def _sgu_bwd(dz, dyb_pre, z, lg, lb, ws, bst, tb=512):
    T = z.shape[0]
    tb = min(tb, T)

    def body(dz_in, dy_ref, zu_ref, zv_ref, lg_ref, lb_ref, ws_ref, bst_ref,
             dz_ref, dws_ref, dbst_ref, dlg_ref, dlb_ref):
        del dz_in
        i = pl.program_id(0)

        @pl.when(i == 0)
        def _():
            dws_ref[...] = jnp.zeros_like(dws_ref)
            dbst_ref[...] = jnp.zeros_like(dbst_ref)
            dlg_ref[...] = jnp.zeros_like(dlg_ref)
            dlb_ref[...] = jnp.zeros_like(dlb_ref)

        gu, dgu = _gelu_and_grad(zu_ref[...])
        gv, dgv = _gelu_and_grad(zv_ref[...])
        rstd, xh = _layernorm_stats(gv)
        vln = xh * lg_ref[...] + lb_ref[...]
        wm, mixed = _sgu_mix(vln, ws_ref, bst_ref, tb)
        dy = dy_ref[...].astype(F32)
        dz_ref[:, 0:D] = (dy * mixed * dgu).astype(BF)
        dmixed = dy * gu
        ri = lax.broadcasted_iota(jnp.int32, (SGU_BLOCK, SGU_BLOCK), 0)
        ci = lax.broadcasted_iota(jnp.int32, (SGU_BLOCK, SGU_BLOCK), 1)
        blocks = []
        for blk in range(tb // SGU_BLOCK):
            rs = slice(blk * SGU_BLOCK, (blk + 1) * SGU_BLOCK)
            cols = []
            for g in range(HEADS):
                cs = slice(g * HD, (g + 1) * HD)
                dmg = dmixed[rs, cs]
                dmb = dmg.astype(BF)
                dbst_ref[:, g:g + 1] += jnp.sum(dmg, axis=1, keepdims=True)
                dws_ref[g] += jnp.where(ri >= ci, _dot_nt(dmb, vln[rs, cs].astype(BF)), 0.0)
                cols.append(_dot_tn(wm[g], dmb))
            blocks.append(jnp.concatenate(cols, axis=1))
        dvln = blocks[0] if len(blocks) == 1 else jnp.concatenate(blocks, axis=0)
        dlg_ref[...] += _colsum(dvln * xh)
        dlb_ref[...] += _colsum(dvln)
        dxh = dvln * lg_ref[...]
        dgv_in = rstd * (dxh - jnp.mean(dxh, axis=-1, keepdims=True)
                         - xh * jnp.mean(dxh * xh, axis=-1, keepdims=True))
        dz_ref[:, D:2 * D] = (dgv_in * dgv).astype(BF)

    row = pl.BlockSpec((tb, D), lambda i: (i, 0))
    vec = pl.BlockSpec((1, D), lambda i: (0, 0))
    wspec = pl.BlockSpec((HEADS, SGU_BLOCK, SGU_BLOCK), lambda i: (0, 0, 0))
    bspec = pl.BlockSpec((SGU_BLOCK, HEADS), lambda i: (0, 0))
    return _pcall(body, name="sgu_bwd", grid=(T // tb,),
                  in_specs=[HBM_SPEC, row, pl.BlockSpec((tb, D), lambda i: (i, 2)),
                            pl.BlockSpec((tb, D), lambda i: (i, 3)), vec, vec, wspec, bspec],
                  out_specs=[pl.BlockSpec((tb, 2 * D), lambda i: (i, 1)), wspec, bspec, vec, vec],
                  out_shape=[jax.ShapeDtypeStruct(dz.shape, BF),
                             jax.ShapeDtypeStruct((HEADS, SGU_BLOCK, SGU_BLOCK), F32),
                             jax.ShapeDtypeStruct((SGU_BLOCK, HEADS), F32),
                             jax.ShapeDtypeStruct((1, D), F32), jax.ShapeDtypeStruct((1, D), F32)],
                  aliases={0: 0}, vmem_mb=48)(dz, dyb_pre, z, z, lg, lb, ws, bst)


def _rglru_bwd(dz, dya_pre, z, h, xc_s, r_s, ig_s, mult_s, a_s, cw, wa, wx, lam, tb=256):
    T = z.shape[0]
    tb = min(tb, T)
    nrow = T // tb
    per = tb // SUBLANES

    def body(dz_in, dy_ref, xr_ref, gr_ref, h_ref, hh_ref, xc_ref, r_ref, ig_ref, mu_ref, a_ref, cw_ref, wa_ref,
             wx_ref, lam_ref, dz_ref, dcw_ref, dcb_ref, dwa_ref, dba_ref, dwx_ref, dbx_ref, dlam_ref, carry, nxt):
        del dz_in
        i = pl.program_id(0)
        first_block = i == nrow - 1

        @pl.when(i == 0)
        def _():
            carry[...] = jnp.zeros_like(carry)
            nxt[...] = jnp.zeros_like(nxt)
            for ref in (dcw_ref, dcb_ref, dwa_ref, dba_ref, dwx_ref, dbx_ref, dlam_ref):
                ref[...] = jnp.zeros_like(ref)

        xc = xc_ref[...].astype(F32)
        r = r_ref[...].astype(F32)
        ig = ig_ref[...].astype(F32)
        mult = mu_ref[...].astype(F32)
        a = a_ref[...]
        lam = lam_ref[...]
        ls = _log_sigmoid(lam)
        hv = h_ref[...]
        hprev = _shift_down(hv, jnp.where(first_block, 0.0, hh_ref[...]), 1)
        gg, dgg = _gelu_and_grad(gr_ref[...])
        dy = dy_ref[...].astype(F32)
        dz_ref[:, D:2 * D] = (dy * hv * dgg).astype(BF)

        rows = lax.broadcasted_iota(jnp.int32, (tb, D), 0)
        v = dy * gg + jnp.where(rows == tb - 1, carry[0:1, :], 0.0)
        q = jnp.where(rows < tb - 1, pltpu.roll(a, tb - 1, 0), 0.0)
        _, gsc = _scan_rows(q, v, reverse=True)
        carry[...] = (a * gsc)[0:SUBLANES]

        xi = ig * xc
        dmult = gsc * xi
        dxi = gsc * mult
        dig = dxi * xc
        dxc = dxi * ig
        dlog_a = gsc * hprev * a - dmult * (a * a) * pl.reciprocal(mult, approx=True)
        dlam_ref[...] += _colsum(dlog_a * r) * (LRU_C * _sigmoid(-lam))
        dpr = dlog_a * (LRU_C * ls) * r * (1.0 - r)
        dpi = dig * ig * (1.0 - ig)
        dba_ref[...] += _colsum(dpr)
        dbx_ref[...] += _colsum(dpi)
        back = []
        for hh in range(HEADS):
            cs = slice(hh * HD, (hh + 1) * HD)
            xh = xc[:, cs].astype(BF)
            dprh = dpr[:, cs].astype(BF)
            dpih = dpi[:, cs].astype(BF)
            dwa_ref[hh] += _dot_tn(xh, dprh)
            dwx_ref[hh] += _dot_tn(xh, dpih)
            back.append(_dot_nt(dprh, wa_ref[hh].astype(BF)) + _dot_nt(dpih, wx_ref[hh].astype(BF)))
        dxc = dxc + jnp.concatenate(back, axis=1)

        n8 = nxt[...]
        d1 = _shift_up(dxc, n8, 1)
        d2 = _shift_up(dxc, n8, 2)
        d3 = _shift_up(dxc, n8, 3)
        nxt[...] = dxc[0:SUBLANES]
        dz_ref[:, 0:D] = (cw_ref[3:4, :] * dxc + cw_ref[2:3, :] * d1 + cw_ref[1:2, :] * d2
                          + cw_ref[0:1, :] * d3).astype(BF)
        xr = xr_ref[...]
        dcw_ref[3:4, :] += _colsum(dxc * xr)
        dcw_ref[2:3, :] += _colsum(d1 * xr)
        dcw_ref[1:2, :] += _colsum(d2 * xr)
        dcw_ref[0:1, :] += _colsum(d3 * xr)
        dcb_ref[...] += _colsum(dxc)

    rev = lambda col: (lambda i: (nrow - 1 - i, col))
    row = pl.BlockSpec((tb, D), rev(0))
    halo = pl.BlockSpec((SUBLANES, D), lambda i: (jnp.maximum((nrow - 1 - i) * per - 1, 0), 0))
    vec = pl.BlockSpec((1, D), lambda i: (0, 0))
    wspec = pl.BlockSpec((HEADS, HD, HD), lambda i: (0, 0, 0))
    c4 = pl.BlockSpec((4, D), lambda i: (0, 0))
    wshape = jax.ShapeDtypeStruct((HEADS, HD, HD), F32)
    vshape = jax.ShapeDtypeStruct((1, D), F32)
    return _pcall(body, name="rglru_bwd", grid=(nrow,),
                  in_specs=[HBM_SPEC, row, row, pl.BlockSpec((tb, D), rev(1)), row, halo,
                            row, row, row, row, row, c4, wspec, wspec, vec],
                  out_specs=[pl.BlockSpec((tb, 2 * D), rev(0)), c4, vec, wspec, vec, wspec, vec, vec],
                  out_shape=[jax.ShapeDtypeStruct(dz.shape, BF), jax.ShapeDtypeStruct((4, D), F32), vshape,
                             wshape, vshape, wshape, vshape, vshape],
                  scratch=[pltpu.VMEM((SUBLANES, D), F32), pltpu.VMEM((SUBLANES, D), F32)],
                  aliases={0: 0}, vmem_mb=56)(dz, dya_pre, z, z, h, h, xc_s, r_s, ig_s, mult_s, a_s, cw, wa, wx, lam)


def _pack_rows(parts):
    out = []
    for p in parts:
        q = p.reshape(-1, LANES)
        pad = (-q.shape[0]) % SUBLANES
        if pad:
            q = jnp.concatenate([q, jnp.zeros((pad, LANES), q.dtype)], axis=0)
        out.append(q)
    return jnp.concatenate(out, axis=0)


def _rows_of(shape):
    n = 1
    for s in shape:
        n *= s
    rows = n // LANES
    return rows + (-rows) % SUBLANES


def kernel(x, c, w_ada, b_ada, norm_mix_g, w_in, rnn_conv_w, rnn_conv_b, lru_w_a, lru_b_a, lru_w_x, lru_b_x, lru_lambda, sgu_ln_g, sgu_ln_b, sgu_w_s, sgu_b_s, w_branch_a, w_branch_b, w_out, norm_ffn_g, w_up, ffn_conv_w, ffn_conv_b, w_down, norm_final_g, loss_target, m_w_ada, m_b_ada, m_norm_mix_g, m_w_in, m_rnn_conv_w, m_rnn_conv_b, m_lru_w_a, m_lru_b_a, m_lru_w_x, m_lru_b_x, m_lru_lambda, m_sgu_ln_g, m_sgu_ln_b, m_sgu_w_s, m_sgu_b_s, m_w_branch_a, m_w_branch_b, m_w_out, m_norm_ffn_g, m_w_up, m_ffn_conv_w, m_ffn_conv_b, m_w_down, m_norm_final_g, v_w_ada, v_b_ada, v_norm_mix_g, v_w_in, v_rnn_conv_w, v_rnn_conv_b, v_lru_w_a, v_lru_b_a, v_lru_w_x, v_lru_b_x, v_lru_lambda, v_sgu_ln_g, v_sgu_ln_b, v_sgu_w_s, v_sgu_b_s, v_w_branch_a, v_w_branch_b, v_w_out, v_norm_ffn_g, v_w_up, v_ffn_conv_w, v_ffn_conv_b, v_w_down, v_norm_final_g):
    args = dict(locals())
    T = x.shape[1]
    mx, my, mc = lax.axis_index("x"), lax.axis_index("y"), lax.axis_index("c")
    chip = 2 * mx + my
    dev = 2 * chip + mc
    vec = lambda a: a.reshape(1, -1)

    xt = x.reshape(T, D)
    tgt = loss_target.reshape(T, D)
    ns = w_in.shape[2]
    dff = w_down.shape[1] * N_CHIPS

    c_all = _gather8(c.reshape(SUBLANES, LANES), "gather_c").reshape(N_DEV, D)
    b_ada_sh = lax.dynamic_slice(b_ada, (0, chip * ns), (1, ns))
    mod_sh = _mod_fwd(c_all, w_ada[0], b_ada_sh)

    mixer_w = _cast_shards([w_in[0], w_branch_a[0], w_branch_b[0], w_out[0]], "cast_mixer_weights")
    w_in4, wba4, wbb4, wo4, rcw4, fcw4, mod4 = _gather_weights(
        list(mixer_w) + [rnn_conv_w[0], ffn_conv_w[0], mod_sh], [True] * 4 + [False] * 3)
    late = _cast_shards([w_up[0], w_down[0]], "cast_late", after=mod4)
    late_plan = _gather_plan(len(late))
    late_handle, late_token = _remote_start(
        late, [lax.empty((N_CHIPS,) + w.shape, w.dtype) for w in late], late_plan, 3 * len(late), "gather_late_start")
    rcw_full = jnp.transpose(rcw4, (1, 0, 2)).reshape(4, D)
    fcw_full = jnp.transpose(fcw4, (1, 0, 2)).reshape(3, 2 * dff)
    mod = lax.dynamic_index_in_dim(mod4, dev, axis=1, keepdims=False).reshape(1, 6 * D)
    shift1, scale1, gate1, shift2, scale2, gate2 = [mod[:, k * D:(k + 1) * D] for k in range(6)]

    bst = jnp.transpose(sgu_b_s[0])
    wba_full = wba4.reshape(D, D)
    wbb_full = wbb4.reshape(D, D)
    wo_full = wo4.reshape(D, D)
    h1, z, h_lru, ya_pre, yb_pre, merged, ya, yb, o1, x2, lru_xc, lru_r, lru_i, lru_mult, lru_a = _mixer_fwd(
        xt, norm_mix_g, scale1 + late_token[0:1, 0:1], shift1, gate1, w_in4, rcw_full, rnn_conv_b,
        lru_w_a[0], lru_b_a, lru_w_x[0], lru_b_x, lru_lambda, sgu_ln_g, sgu_ln_b, sgu_w_s[0], bst,
        wba_full, wbb_full, wo_full)
    late, late_lands = _remote_wait(late_handle, late_plan, o1, "gather_late_wait")
    w_up4, w_down4 = _place_own(late, late_lands)
    wd_full = w_down4.reshape(dff, D)
    h2, up, f, ffn_ga, ffn_vd, loss_part, dx3, dfo, dgf, dgate2 = _ffn_fwd(
        x2, norm_ffn_g, scale2, shift2, gate2, vec(norm_final_g), w_up4, wd_full, fcw_full, ffn_conv_b, tgt)

    dup, dfcw, dfcb, dx2, dshift2, dscale2, dg_ffn, do1, dgate1 = _ffn_bwd(
        dfo, wd_full, up, ffn_ga, ffn_vd, fcw_full, w_up4, x2, dx3, norm_ffn_g, scale2, gate1, o1)
    dwd = _mm_tn_cols(f, dfo, "dw_down", 1, D, mb=D, tm=TN_ROWS_SQUARE)
    dw_up4 = _mm_tn_cols(h2, dup, "dw_up", N_CHIPS, ns)
    dz, dya, dyb, dya_pre, dyb_pre = _mix_bwd(do1, ya, yb, z, wo_full, wba_full, wbb_full)
    dwo = _mm_tn_cols(merged, do1, "dw_out", 1, D)
    dwba = _mm_tn_cols(ya_pre, dya, "dw_branch_a", 1, D)
    dwbb = _mm_tn_cols(yb_pre, dyb, "dw_branch_b", 1, D)

    chip_id = chip.astype(jnp.int32).reshape(1)

    def reduce_start(group, name):
        wire = [g16.reshape(N_CHIPS, -1, g16.shape[-1]) for _, (_, g16) in group]
        lands = [lax.empty((3,) + w.shape[1:], w.dtype) for w in wire]
        return _remote_start(wire, lands, _scatter_plan(len(group)), 3 * len(group), "scatter_start_" + name)

    def reduce_finish(group, handle, after, name):
        _, landed = _remote_wait(handle, _scatter_plan(len(group)), after, "scatter_wait_" + name)
        return [_sum_own_and_landed(chip_id, g32.reshape(N_CHIPS, -1, g32.shape[-1]), l, "sum_chips_" + n)
                for (n, (g32, _)), l in zip(group, landed)]

    group1 = [("w_up", dw_up4), ("w_down", dwd), ("w_branch_a", dwba), ("w_branch_b", dwbb), ("w_out", dwo)]
    handle1, token1 = reduce_start(group1, "late")
    dz, dws, dbst, dlg, dlb = _sgu_bwd(dz, dyb_pre, z, sgu_ln_g + token1[0:1, 0:1], sgu_ln_b, sgu_w_s[0], bst)
    dz, drcw, drcb, dwa, dba, dwx, dbx, dlam = _rglru_bwd(
        dz, dya_pre, z, h_lru, lru_xc, lru_r, lru_i, lru_mult, lru_a, rcw_full, lru_w_a[0], lru_w_x[0], lru_lambda)
    early_small = [("rnn_conv_b", drcb), ("lru_w_a", dwa), ("lru_b_a", dba), ("lru_w_x", dwx), ("lru_b_x", dbx),
                   ("lru_lambda", dlam), ("sgu_ln_g", dlg), ("sgu_ln_b", dlb), ("sgu_w_s", dws),
                   ("sgu_b_s", jnp.transpose(dbst)), ("norm_ffn_g", dg_ffn),
                   ("ffn_conv_b", dfcb), ("norm_final_g", dgf)]
    r_early = sum(_rows_of(args[n].shape) for n, _ in early_small)
    early_pack = _pack_rows([g for _, g in early_small] + [drcw, dfcw])
    early_pack = jnp.concatenate(
        [early_pack, jnp.zeros(((-early_pack.shape[0]) % 256, LANES), F32)], axis=0)
    early_chip = _add_pair(early_pack, _swap_cores([early_pack], "swap_small_grads")[0], "sum_cores_small_grads")
    early_handle, token3 = _remote_start([early_chip], [lax.empty((3,) + early_chip.shape, F32)], _bcast_plan, 3,
                                         "small_grads_start")
    def swap_start(totals, name):
        lands = [lax.empty(t.shape, t.dtype) for t in totals]
        return _remote_start(totals, lands, _sibling_plan(len(totals)), len(totals), "swap_sums_start_" + name)

    out = {}

    def swap_finish(group, handle, after, name):
        mine, theirs = _remote_wait(handle, _sibling_plan(len(group)), after, "swap_sums_wait_" + name)
        for (n, _), a, b in zip(group, mine, theirs):
            shape = args[n].shape
            res = _adamw(args[n][0], args["m_" + n][0], args["v_" + n][0], [a, b], "adamw_" + n)
            for kind, r in zip(("grad_", "delta_", "new_m_", "new_v_"), res):
                out[kind + n] = r.reshape(shape)
        return res[3]

    swap1, token4 = swap_start(reduce_finish(group1, handle1, drcb, "late"), "late")
    group2 = [("w_in", _mm_tn_cols(h1, dz, "dw_in", N_CHIPS, ns))]
    handle2, token2 = reduce_start(group2, "in")
    tokens = token2[0:1, 0:1] + token3[0:1, 0:1] + token4[0:1, 0:1]
    grad_x, dshift1, dscale1, dg_mix = _mm_nt_normbwd(
        dz, w_in4, xt, dx2, norm_mix_g + tokens, scale1, "dh1_norm_bwd")
    swap2, token5 = swap_start(reduce_finish(group2, handle2, dg_mix, "in"), "in")
    dmod = jnp.concatenate([dshift1, dscale1, dgate1, dshift2, dscale2, dgate2], axis=1)
    last = swap_finish(group1, swap1, token5, "late")
    swap_finish(group2, swap2, last, "in")

    late_small = [("b_ada", dmod), ("norm_mix_g", dg_mix)]
    small = late_small + early_small
    late_all = _gather8(_pack_rows([g for _, g in late_small] + [loss_part]), "gather_late_small_grads")
    late_sum = _sum_parts(late_all, "sum_late_small_grads")
    r_late = sum(_rows_of(args[n].shape) for n, _ in late_small)
    loss = late_sum[r_late, 0]
    late_sum = late_sum[:r_late]
    _, (early_landed,) = _remote_wait(early_handle, _bcast_plan, dg_mix, "small_grads_wait")
    early_sum = _sum_chips_in_order(chip_id, early_chip, early_landed, "sum_early_small_grads")
    r_small = sum(_rows_of(args[n].shape) for n, _ in small)
    r_pad = r_small + (-r_small) % 256
    fill = jnp.zeros((r_pad - r_small, LANES), F32)
    g_small = jnp.concatenate([late_sum, early_sum[:r_early], fill], axis=0)

    def pack_small(prefix):
        return jnp.concatenate([_pack_rows([args[prefix + n] for n, _ in small]), fill], axis=0)

    res = _adamw(pack_small(""), pack_small("m_"), pack_small("v_"), [g_small], "adamw_small")
    off = 0
    for n, _ in small:
        shape = args[n].shape
        rows = _rows_of(shape)
        for kind, r in zip(("grad_", "delta_", "new_m_", "new_v_"), res):
            out[kind + n] = r[off:off + rows].reshape(shape)
        off += rows

    rcw_cols = rnn_conv_w.shape[2]
    g_rcw = lax.dynamic_slice(early_sum[r_early:r_early + 32].reshape(4, D), (0, chip * rcw_cols), (4, rcw_cols))
    g_fcw = lax.dynamic_slice(early_sum[r_early + 32:r_early + 32 + 144].reshape(3, 2 * dff), (0, chip * ns), (3, ns))
    conv = [("rnn_conv_w", g_rcw), ("ffn_conv_w", g_fcw)]
    res = _adamw(_pack_rows([args[n] for n, _ in conv]), _pack_rows([args["m_" + n] for n, _ in conv]),
                 _pack_rows([args["v_" + n] for n, _ in conv]), [_pack_rows([g for _, g in conv])], "adamw_conv")
    off = 0
    for n, _ in conv:
        shape = args[n].shape
        cnt = shape[1] * shape[2] // LANES
        for kind, r in zip(("grad_", "delta_", "new_m_", "new_v_"), res):
            out[kind + n] = r[off:off + cnt].reshape(shape)
        off += _rows_of(shape)

    dmod_all = late_all[:, 0:6 * D // LANES, :].reshape(N_DEV, 6 * D)
    dmod_sh = lax.dynamic_slice(dmod_all, (0, chip * ns), (N_DEV, ns))
    res = _ada_adamw(jnp.transpose(c_all), dmod_sh, w_ada[0], m_w_ada[0], v_w_ada[0])
    for kind, r in zip(("grad_", "delta_", "new_m_", "new_v_"), res):
        out[kind + "w_ada"] = r.reshape(w_ada.shape)

    names = ["w_ada", "b_ada", "norm_mix_g", "w_in", "rnn_conv_w", "rnn_conv_b", "lru_w_a", "lru_b_a", "lru_w_x",
             "lru_b_x", "lru_lambda", "sgu_ln_g", "sgu_ln_b", "sgu_w_s", "sgu_b_s", "w_branch_a", "w_branch_b",
             "w_out", "norm_ffn_g", "w_up", "ffn_conv_w", "ffn_conv_b", "w_down", "norm_final_g"]
    result = [loss, grad_x.reshape(x.shape)]
    for kind in ("grad_", "delta_", "new_m_", "new_v_"):
        result += [out[kind + n] for n in names]
    return tuple(result)
```
